```python
import math
import jax, jax.numpy as jnp
from jax import lax
import numpy as np


D_MODEL = 1024
BATCH = 8
SEQ = 8192
DEPTH = 4

HEAD_DIM = 64
HEADS_PER_GROUP = D_MODEL // 2 // HEAD_DIM
ATTN_PATTERN = ((128, 1), (512, 4), (2048, 16))
N_ATTN_GROUPS = len(ATTN_PATTERN)
ATTN_WIDTH = HEADS_PER_GROUP * HEAD_DIM
QKV_GROUP_WIDTH = N_ATTN_GROUPS * ATTN_WIDTH
BLK = 128
SSM_WIDTH = D_MODEL // 2
SSM_GROUP = 16
SSM_GROUPS = SSM_WIDTH // SSM_GROUP
SSM_STATE = 64
DT_MIN = 1e-3
DT_MAX = 1e-1
D_FF = ((8 * D_MODEL + 3 * 256 - 1) // (3 * 256)) * 256
EPS = 1e-6
IN_COLS = 3 * QKV_GROUP_WIDTH + SSM_WIDTH + 2 * D_MODEL
SPLIT_POINTS = (QKV_GROUP_WIDTH, 2 * QKV_GROUP_WIDTH, 3 * QKV_GROUP_WIDTH,
                3 * QKV_GROUP_WIDTH + SSM_WIDTH, 3 * QKV_GROUP_WIDTH + SSM_WIDTH + D_MODEL)

kernel_name = "hybrid_gated_dilated_attn_s5_swiglu"


def rms_norm(t, gain):
    t32 = t.astype(jnp.float32)
    y = t32 * lax.rsqrt(jnp.mean(t32 * t32, axis=-1, keepdims=True) + EPS) * gain.astype(jnp.float32)
    return y.astype(t.dtype)


def head_rms_norm(t, gain):
    t32 = t.astype(jnp.float32)
    return t32 * lax.rsqrt(jnp.mean(t32 * t32, axis=-1, keepdims=True) + EPS) * gain.astype(jnp.float32)


def dilated_window_attention(q, k, v, window, dilation):
    b_, L, H, E = q.shape
    span = window // dilation
    unit = dilation * BLK
    Lp = -(-L // unit) * unit
    M = Lp // dilation
    nb = M // BLK

    def to_blocks(t):
        t = jnp.pad(t, ((0, 0), (0, Lp - L), (0, 0), (0, 0)))
        t = t.reshape(b_, M, dilation, H, E).transpose(0, 2, 3, 1, 4)
        return t.reshape(b_, dilation, H, nb, BLK, E)

    def with_prev(t):
        prev = jnp.pad(t[:, :, :, :-1], ((0, 0), (0, 0), (0, 0), (1, 0), (0, 0), (0, 0)))
        return jnp.concatenate([prev, t], axis=4)

    qb = to_blocks(q)
    kw = with_prev(to_blocks(k))
    vw = with_prev(to_blocks(v))
    s = jnp.einsum('brhnqe,brhnke->brhnqk', qb, kw) * (HEAD_DIM ** -0.5)
    qi = jnp.arange(BLK)[:, None]
    ki = jnp.arange(2 * BLK)[None, :]
    dist = BLK + qi - ki
    blk = jnp.arange(nb)[:, None, None]
    mask = (dist >= 0) & (dist <= span) & (blk * BLK + ki - BLK >= 0)
    s = jnp.where(mask, s, -jnp.inf)
    m = jnp.max(s, axis=-1, keepdims=True)
    p = jnp.exp(s - m)
    denom = jnp.sum(p, axis=-1, keepdims=True)
    o = jnp.einsum('brhnqk,brhnke->brhnqe', p, vw) / denom
    lse = (m + jnp.log(denom))[..., 0]
    o = o.reshape(b_, dilation, H, M, E).transpose(0, 3, 1, 2, 4).reshape(b_, Lp, H, E)[:, :L]
    lse = lse.reshape(b_, dilation, H, M).transpose(0, 3, 1, 2).reshape(b_, Lp, H)[:, :L]
    return o, lse


def dilated_attention(q, k, v, g_q, g_k):
    b_, L, _ = q.shape
    shape = (b_, L, N_ATTN_GROUPS, HEADS_PER_GROUP, HEAD_DIM)
    q = head_rms_norm(q.reshape(shape), g_q)
    k = head_rms_norm(k.reshape(shape), g_k)
    v = v.reshape(shape).astype(jnp.float32)
    outs, lses = [], []
    for gi, (window, dilation) in enumerate(ATTN_PATTERN):
        o, lse = dilated_window_attention(q[:, :, gi], k[:, :, gi], v[:, :, gi], window, dilation)
        outs.append(o)
        lses.append(lse)
    w = jax.nn.softmax(jnp.stack(lses, axis=0), axis=0)
    out = jnp.sum(w[..., None] * jnp.stack(outs, axis=0), axis=0)
    return out.reshape(b_, L, ATTN_WIDTH)


def _complex_linear_combine(e1, e2):
    a1r, a1i, b1r, b1i = e1
    a2r, a2i, b2r, b2i = e2
    ar = a2r * a1r - a2i * a1i
    ai = a2r * a1i + a2i * a1r
    br = a2r * b1r - a2i * b1i + b2r
    bi = a2r * b1i + a2i * b1r + b2i
    return (ar, ai, br, bi)


def s5_ssm(u, lam_re, lam_im, log_dt, b_re, b_im, c_re, c_im, d_skip):
    b_, L, _ = u.shape
    u = u.astype(jnp.float32)
    lr = lam_re.astype(jnp.float32)
    li = lam_im.astype(jnp.float32)
    dt = jnp.exp(log_dt.astype(jnp.float32))[:, None]
    mag = jnp.exp(lr * dt)
    ang = li * dt
    abar_re = mag * jnp.cos(ang)
    abar_im = mag * jnp.sin(ang)
    nr = abar_re - 1.0
    ni = abar_im
    den = lr * lr + li * li
    cr = ((nr * lr + ni * li) / den)[..., None]
    ci = ((ni * lr - nr * li) / den)[..., None]
    br = b_re.astype(jnp.float32)
    bi = b_im.astype(jnp.float32)
    bbar_re = cr * br - ci * bi
    bbar_im = cr * bi + ci * br
    ug = u.reshape(b_, L, SSM_GROUPS, SSM_GROUP)
    bu_re = jnp.einsum('blgc,gpc->lbgp', ug, bbar_re)
    bu_im = jnp.einsum('blgc,gpc->lbgp', ug, bbar_im)
    a_re = jnp.broadcast_to(abar_re[None, None], (L, 1, SSM_GROUPS, SSM_STATE))
    a_im = jnp.broadcast_to(abar_im[None, None], (L, 1, SSM_GROUPS, SSM_STATE))
    _, _, xr, xi = lax.associative_scan(_complex_linear_combine, (a_re, a_im, bu_re, bu_im), axis=0)
    y = (jnp.einsum('lbgp,gcp->blgc', xr, c_re.astype(jnp.float32))
         - jnp.einsum('lbgp,gcp->blgc', xi, c_im.astype(jnp.float32)))
    return y.reshape(b_, L, SSM_WIDTH) + d_skip.astype(jnp.float32) * u


def _fwd_setup_inputs(seed: int = 0) -> dict:
    key = jax.random.key(seed)
    ks = jax.random.split(key, 24)
    f32 = jnp.float32

    def normal(k, shape, scale):
        return jax.random.normal(k, shape, f32) * scale

    n_idx = jnp.arange(SSM_STATE, dtype=f32)
    return {
        "x": normal(ks[0], (BATCH, SEQ, D_MODEL), 1.0),
        "g_mix": 1.0 + normal(ks[1], (DEPTH, D_MODEL), 0.02),
        "w_in": normal(ks[2], (DEPTH, D_MODEL, IN_COLS), D_MODEL ** -0.5),
        "g_q": 1.0 + normal(ks[3], (DEPTH, HEAD_DIM), 0.02),
        "g_k": 1.0 + normal(ks[4], (DEPTH, HEAD_DIM), 0.02),
        "w_attn_proj": normal(ks[5], (DEPTH, ATTN_WIDTH, D_MODEL), ATTN_WIDTH ** -0.5),
        "lambda_re": -0.5 + normal(ks[6], (DEPTH, SSM_GROUPS, SSM_STATE), 0.01),
        "lambda_im": math.pi * n_idx + normal(ks[7], (DEPTH, SSM_GROUPS, SSM_STATE), 0.01),
        "log_dt": jax.random.uniform(ks[8], (DEPTH, SSM_GROUPS), f32, math.log(DT_MIN), math.log(DT_MAX)),
        "b_re": normal(ks[9], (DEPTH, SSM_GROUPS, SSM_STATE, SSM_GROUP), (2 * SSM_GROUP) ** -0.5),
        "b_im": normal(ks[10], (DEPTH, SSM_GROUPS, SSM_STATE, SSM_GROUP), (2 * SSM_GROUP) ** -0.5),
        "c_re": normal(ks[11], (DEPTH, SSM_GROUPS, SSM_GROUP, SSM_STATE), (2 * SSM_STATE) ** -0.5),
        "c_im": normal(ks[12], (DEPTH, SSM_GROUPS, SSM_GROUP, SSM_STATE), (2 * SSM_STATE) ** -0.5),
        "d_skip": normal(ks[13], (DEPTH, SSM_WIDTH), 1.0),
        "w_glu_a": normal(ks[14], (DEPTH, SSM_WIDTH, D_MODEL), SSM_WIDTH ** -0.5),
        "w_glu_b": normal(ks[15], (DEPTH, SSM_WIDTH, D_MODEL), SSM_WIDTH ** -0.5),
        "w_out": normal(ks[16], (DEPTH, D_MODEL, D_MODEL), D_MODEL ** -0.5),
        "g_ffn": 1.0 + normal(ks[17], (DEPTH, D_MODEL), 0.02),
        "w_ffn_gate": normal(ks[18], (DEPTH, D_MODEL, D_FF), D_MODEL ** -0.5),
        "w_ffn_up": normal(ks[19], (DEPTH, D_MODEL, D_FF), D_MODEL ** -0.5),
        "w_ffn_down": normal(ks[20], (DEPTH, D_FF, D_MODEL), D_FF ** -0.5),
    }


def _fwd_reference(x, g_mix, w_in, g_q, g_k, w_attn_proj, lambda_re, lambda_im, log_dt,
              b_re, b_im, c_re, c_im, d_skip, w_glu_a, w_glu_b, w_out,
              g_ffn, w_ffn_gate, w_ffn_up, w_ffn_down):
    for l in range(DEPTH):
        h = rms_norm(x, g_mix[l])
        z = h @ w_in[l]
        q, k, v, u, gate_a, gate_s = jnp.split(z, SPLIT_POINTS, axis=-1)
        a = dilated_attention(q, k, v, g_q[l], g_k[l]).astype(x.dtype)
        a_out = a @ w_attn_proj[l]
        y = s5_ssm(u, lambda_re[l], lambda_im[l], log_dt[l], b_re[l], b_im[l],
                   c_re[l], c_im[l], d_skip[l])
        y = jax.nn.gelu(y).astype(x.dtype)
        s_out = (y @ w_glu_a[l]) * jax.nn.sigmoid(y @ w_glu_b[l])
        mix = jax.nn.sigmoid(gate_a) * a_out + jax.nn.sigmoid(gate_s) * s_out
        x = x + mix @ w_out[l]
        h2 = rms_norm(x, g_ffn[l])
        x = x + (jax.nn.silu(h2 @ w_ffn_gate[l]) * (h2 @ w_ffn_up[l])) @ w_ffn_down[l]
    return x


import jax as _jax
import jax.numpy as _jnp

TWIN_FORMAT = 'train_step'
FWD_PARAMS = ['x', 'g_mix', 'w_in', 'g_q', 'g_k', 'w_attn_proj', 'lambda_re', 'lambda_im', 'log_dt', 'b_re', 'b_im', 'c_re', 'c_im', 'd_skip', 'w_glu_a', 'w_glu_b', 'w_out', 'g_ffn', 'w_ffn_gate', 'w_ffn_up', 'w_ffn_down']
TWIN_WEIGHTS = ['g_mix', 'w_in', 'g_q', 'g_k', 'w_attn_proj', 'lambda_re', 'lambda_im', 'log_dt', 'b_re', 'b_im', 'c_re', 'c_im', 'd_skip', 'w_glu_a', 'w_glu_b', 'w_out', 'g_ffn', 'w_ffn_gate', 'w_ffn_up', 'w_ffn_down']
TWIN_DIFF_INPUT = 'x'
TWIN_INPUTS = ['x', 'g_mix', 'w_in', 'g_q', 'g_k', 'w_attn_proj', 'lambda_re', 'lambda_im', 'log_dt', 'b_re', 'b_im', 'c_re', 'c_im', 'd_skip', 'w_glu_a', 'w_glu_b', 'w_out', 'g_ffn', 'w_ffn_gate', 'w_ffn_up', 'w_ffn_down', 'loss_target', 'm_g_mix', 'm_w_in', 'm_g_q', 'm_g_k', 'm_w_attn_proj', 'm_lambda_re', 'm_lambda_im', 'm_log_dt', 'm_b_re', 'm_b_im', 'm_c_re', 'm_c_im', 'm_d_skip', 'm_w_glu_a', 'm_w_glu_b', 'm_w_out', 'm_g_ffn', 'm_w_ffn_gate', 'm_w_ffn_up', 'm_w_ffn_down', 'v_g_mix', 'v_w_in', 'v_g_q', 'v_g_k', 'v_w_attn_proj', 'v_lambda_re', 'v_lambda_im', 'v_log_dt', 'v_b_re', 'v_b_im', 'v_c_re', 'v_c_im', 'v_d_skip', 'v_w_glu_a', 'v_w_glu_b', 'v_w_out', 'v_g_ffn', 'v_w_ffn_gate', 'v_w_ffn_up', 'v_w_ffn_down']
TWIN_OUTPUTS = ['loss', 'grad_x', 'grad_g_mix', 'grad_w_in', 'grad_g_q', 'grad_g_k', 'grad_w_attn_proj', 'grad_lambda_re', 'grad_lambda_im', 'grad_log_dt', 'grad_b_re', 'grad_b_im', 'grad_c_re', 'grad_c_im', 'grad_d_skip', 'grad_w_glu_a', 'grad_w_glu_b', 'grad_w_out', 'grad_g_ffn', 'grad_w_ffn_gate', 'grad_w_ffn_up', 'grad_w_ffn_down', 'delta_g_mix', 'delta_w_in', 'delta_g_q', 'delta_g_k', 'delta_w_attn_proj', 'delta_lambda_re', 'delta_lambda_im', 'delta_log_dt', 'delta_b_re', 'delta_b_im', 'delta_c_re', 'delta_c_im', 'delta_d_skip', 'delta_w_glu_a', 'delta_w_glu_b', 'delta_w_out', 'delta_g_ffn', 'delta_w_ffn_gate', 'delta_w_ffn_up', 'delta_w_ffn_down', 'new_m_g_mix', 'new_m_w_in', 'new_m_g_q', 'new_m_g_k', 'new_m_w_attn_proj', 'new_m_lambda_re', 'new_m_lambda_im', 'new_m_log_dt', 'new_m_b_re', 'new_m_b_im', 'new_m_c_re', 'new_m_c_im', 'new_m_d_skip', 'new_m_w_glu_a', 'new_m_w_glu_b', 'new_m_w_out', 'new_m_g_ffn', 'new_m_w_ffn_gate', 'new_m_w_ffn_up', 'new_m_w_ffn_down', 'new_v_g_mix', 'new_v_w_in', 'new_v_g_q', 'new_v_g_k', 'new_v_w_attn_proj', 'new_v_lambda_re', 'new_v_lambda_im', 'new_v_log_dt', 'new_v_b_re', 'new_v_b_im', 'new_v_c_re', 'new_v_c_im', 'new_v_d_skip', 'new_v_w_glu_a', 'new_v_w_glu_b', 'new_v_w_out', 'new_v_g_ffn', 'new_v_w_ffn_gate', 'new_v_w_ffn_up', 'new_v_w_ffn_down']
TWIN_LEAF_KINDS = {'loss': 'loss', 'grad_x': 'grad_x', 'grad_g_mix': 'grad_w', 'grad_w_in': 'grad_w', 'grad_g_q': 'grad_w', 'grad_g_k': 'grad_w', 'grad_w_attn_proj': 'grad_w', 'grad_lambda_re': 'grad_w', 'grad_lambda_im': 'grad_w', 'grad_log_dt': 'grad_w', 'grad_b_re': 'grad_w', 'grad_b_im': 'grad_w', 'grad_c_re': 'grad_w', 'grad_c_im': 'grad_w', 'grad_d_skip': 'grad_w', 'grad_w_glu_a': 'grad_w', 'grad_w_glu_b': 'grad_w', 'grad_w_out': 'grad_w', 'grad_g_ffn': 'grad_w', 'grad_w_ffn_gate': 'grad_w', 'grad_w_ffn_up': 'grad_w', 'grad_w_ffn_down': 'grad_w', 'delta_g_mix': 'delta_w', 'delta_w_in': 'delta_w', 'delta_g_q': 'delta_w', 'delta_g_k': 'delta_w', 'delta_w_attn_proj': 'delta_w', 'delta_lambda_re': 'delta_w', 'delta_lambda_im': 'delta_w', 'delta_log_dt': 'delta_w', 'delta_b_re': 'delta_w', 'delta_b_im': 'delta_w', 'delta_c_re': 'delta_w', 'delta_c_im': 'delta_w', 'delta_d_skip': 'delta_w', 'delta_w_glu_a': 'delta_w', 'delta_w_glu_b': 'delta_w', 'delta_w_out': 'delta_w', 'delta_g_ffn': 'delta_w', 'delta_w_ffn_gate': 'delta_w', 'delta_w_ffn_up': 'delta_w', 'delta_w_ffn_down': 'delta_w', 'new_m_g_mix': 'new_m', 'new_m_w_in': 'new_m', 'new_m_g_q': 'new_m', 'new_m_g_k': 'new_m', 'new_m_w_attn_proj': 'new_m', 'new_m_lambda_re': 'new_m', 'new_m_lambda_im': 'new_m', 'new_m_log_dt': 'new_m', 'new_m_b_re': 'new_m', 'new_m_b_im': 'new_m', 'new_m_c_re': 'new_m', 'new_m_c_im': 'new_m', 'new_m_d_skip': 'new_m', 'new_m_w_glu_a': 'new_m', 'new_m_w_glu_b': 'new_m', 'new_m_w_out': 'new_m', 'new_m_g_ffn': 'new_m', 'new_m_w_ffn_gate': 'new_m', 'new_m_w_ffn_up': 'new_m', 'new_m_w_ffn_down': 'new_m', 'new_v_g_mix': 'new_v', 'new_v_w_in': 'new_v', 'new_v_g_q': 'new_v', 'new_v_g_k': 'new_v', 'new_v_w_attn_proj': 'new_v', 'new_v_lambda_re': 'new_v', 'new_v_lambda_im': 'new_v', 'new_v_log_dt': 'new_v', 'new_v_b_re': 'new_v', 'new_v_b_im': 'new_v', 'new_v_c_re': 'new_v', 'new_v_c_im': 'new_v', 'new_v_d_skip': 'new_v', 'new_v_w_glu_a': 'new_v', 'new_v_w_glu_b': 'new_v', 'new_v_w_out': 'new_v', 'new_v_g_ffn': 'new_v', 'new_v_w_ffn_gate': 'new_v', 'new_v_w_ffn_up': 'new_v', 'new_v_w_ffn_down': 'new_v'}


def _forward(args):
    return _fwd_reference(*[args[k] for k in FWD_PARAMS])


def _output_shape():
    def fwd():
        inp = _fwd_setup_inputs(0)
        return _fwd_reference(*[inp[k] for k in FWD_PARAMS])
    out = _jax.eval_shape(fwd)
    return out.shape, out.dtype

N_MICROBATCH = 1
ADAM_LR = 0.001
ADAM_B1 = 0.9
ADAM_B2 = 0.999
ADAM_EPS = 1e-08
ADAM_WD = 0.01
ADAM_STEP = 10
PER_EXAMPLE_BATCH_AXIS = {'x': 0, 'loss_target': 0}
SHARED_INPUTS = []
_WEIGHT_DTYPES = {'g_mix': _jnp.float32, 'w_in': _jnp.float32, 'g_q': _jnp.float32, 'g_k': _jnp.float32, 'w_attn_proj': _jnp.float32, 'lambda_re': _jnp.float32, 'lambda_im': _jnp.float32, 'log_dt': _jnp.float32, 'b_re': _jnp.float32, 'b_im': _jnp.float32, 'c_re': _jnp.float32, 'c_im': _jnp.float32, 'd_skip': _jnp.float32, 'w_glu_a': _jnp.float32, 'w_glu_b': _jnp.float32, 'w_out': _jnp.float32, 'g_ffn': _jnp.float32, 'w_ffn_gate': _jnp.float32, 'w_ffn_up': _jnp.float32, 'w_ffn_down': _jnp.float32}
MOMENT_SCALE = {'g_mix': 2.917088e+00, 'w_in': 1.981399e-01, 'g_q': 1.232463e+00, 'g_k': 1.230485e+00, 'w_attn_proj': 2.502684e-01, 'lambda_re': 5.711260e-02, 'lambda_im': 5.434732e-02, 'log_dt': 7.216404e+00, 'b_re': 3.643089e-02, 'b_im': 3.127924e-02, 'c_re': 6.344197e-02, 'c_im': 7.329261e-02, 'd_skip': 6.218143e+00, 'w_glu_a': 1.658580e+00, 'w_glu_b': 4.075839e-01, 'w_out': 1.515072e+00, 'g_ffn': 4.954604e+01, 'w_ffn_gate': 5.737070e-01, 'w_ffn_up': 4.242136e-01, 'w_ffn_down': 6.866484e-01}


def _to_microbatches(a, axis):
    t = _jnp.moveaxis(a, axis, 0)
    t = t.reshape((N_MICROBATCH, t.shape[0] // N_MICROBATCH) + t.shape[1:])
    return _jnp.moveaxis(t, 1, axis + 1)


def setup_inputs(seed: int = 0) -> dict:
    inp = _fwd_setup_inputs(seed)
    key = _jax.random.fold_in(_jax.random.key(seed), 7919)
    shape, _ = _output_shape()
    out = dict(inp)
    out["loss_target"] = _jax.random.normal(_jax.random.fold_in(key, 0), shape, _jnp.float32)
    for i, name in enumerate(TWIN_WEIGHTS):
        w = inp[name].astype(_jnp.float32)
        if MOMENT_SCALE is None:
            s = _jnp.sqrt(_jnp.mean(_jnp.square(w)) + 1e-30)
        else:
            s = MOMENT_SCALE[name]
        km, kv = _jax.random.split(_jax.random.fold_in(key, i + 1))
        out[name] = w
        out["m_" + name] = s * _jax.random.normal(km, w.shape, _jnp.float32)
        out["v_" + name] = (s * s) * _jax.random.uniform(kv, w.shape, _jnp.float32, 0.5, 1.5)
    if N_MICROBATCH > 1:
        for name, axis in PER_EXAMPLE_BATCH_AXIS.items():
            out[name] = _to_microbatches(out[name], axis)
    return {'x': out['x'], 'g_mix': out['g_mix'], 'w_in': out['w_in'], 'g_q': out['g_q'], 'g_k': out['g_k'], 'w_attn_proj': out['w_attn_proj'], 'lambda_re': out['lambda_re'], 'lambda_im': out['lambda_im'], 'log_dt': out['log_dt'], 'b_re': out['b_re'], 'b_im': out['b_im'], 'c_re': out['c_re'], 'c_im': out['c_im'], 'd_skip': out['d_skip'], 'w_glu_a': out['w_glu_a'], 'w_glu_b': out['w_glu_b'], 'w_out': out['w_out'], 'g_ffn': out['g_ffn'], 'w_ffn_gate': out['w_ffn_gate'], 'w_ffn_up': out['w_ffn_up'], 'w_ffn_down': out['w_ffn_down'], 'loss_target': out['loss_target'], 'm_g_mix': out['m_g_mix'], 'm_w_in': out['m_w_in'], 'm_g_q': out['m_g_q'], 'm_g_k': out['m_g_k'], 'm_w_attn_proj': out['m_w_attn_proj'], 'm_lambda_re': out['m_lambda_re'], 'm_lambda_im': out['m_lambda_im'], 'm_log_dt': out['m_log_dt'], 'm_b_re': out['m_b_re'], 'm_b_im': out['m_b_im'], 'm_c_re': out['m_c_re'], 'm_c_im': out['m_c_im'], 'm_d_skip': out['m_d_skip'], 'm_w_glu_a': out['m_w_glu_a'], 'm_w_glu_b': out['m_w_glu_b'], 'm_w_out': out['m_w_out'], 'm_g_ffn': out['m_g_ffn'], 'm_w_ffn_gate': out['m_w_ffn_gate'], 'm_w_ffn_up': out['m_w_ffn_up'], 'm_w_ffn_down': out['m_w_ffn_down'], 'v_g_mix': out['v_g_mix'], 'v_w_in': out['v_w_in'], 'v_g_q': out['v_g_q'], 'v_g_k': out['v_g_k'], 'v_w_attn_proj': out['v_w_attn_proj'], 'v_lambda_re': out['v_lambda_re'], 'v_lambda_im': out['v_lambda_im'], 'v_log_dt': out['v_log_dt'], 'v_b_re': out['v_b_re'], 'v_b_im': out['v_b_im'], 'v_c_re': out['v_c_re'], 'v_c_im': out['v_c_im'], 'v_d_skip': out['v_d_skip'], 'v_w_glu_a': out['v_w_glu_a'], 'v_w_glu_b': out['v_w_glu_b'], 'v_w_out': out['v_w_out'], 'v_g_ffn': out['v_g_ffn'], 'v_w_ffn_gate': out['v_w_ffn_gate'], 'v_w_ffn_up': out['v_w_ffn_up'], 'v_w_ffn_down': out['v_w_ffn_down']}


def _loss(weights, diff, rest, loss_target):
    with _jax.named_scope("forward"):
        args = {**rest, TWIN_DIFF_INPUT: diff, **{k: w.astype(_WEIGHT_DTYPES[k]) for k, w in weights.items()}}
        y = _forward(args)
    with _jax.named_scope("loss_head"):
        err = _jnp.square(y.astype(_jnp.float32) - loss_target)
        return 0.5 * _jnp.sum(_jnp.mean(err, axis=-1)) if err.ndim else 0.5 * err


def _adamw(w, g, m, v):
    m = ADAM_B1 * m + (1.0 - ADAM_B1) * g
    v = ADAM_B2 * v + (1.0 - ADAM_B2) * _jnp.square(g)
    m_hat = m / (1.0 - ADAM_B1 ** ADAM_STEP)
    v_hat = v / (1.0 - ADAM_B2 ** ADAM_STEP)
    delta = -ADAM_LR * (m_hat / (_jnp.sqrt(v_hat) + ADAM_EPS) + ADAM_WD * w)
    return delta, m, v


def reference(x, g_mix, w_in, g_q, g_k, w_attn_proj, lambda_re, lambda_im, log_dt, b_re, b_im, c_re, c_im, d_skip, w_glu_a, w_glu_b, w_out, g_ffn, w_ffn_gate, w_ffn_up, w_ffn_down, loss_target, m_g_mix, m_w_in, m_g_q, m_g_k, m_w_attn_proj, m_lambda_re, m_lambda_im, m_log_dt, m_b_re, m_b_im, m_c_re, m_c_im, m_d_skip, m_w_glu_a, m_w_glu_b, m_w_out, m_g_ffn, m_w_ffn_gate, m_w_ffn_up, m_w_ffn_down, v_g_mix, v_w_in, v_g_q, v_g_k, v_w_attn_proj, v_lambda_re, v_lambda_im, v_log_dt, v_b_re, v_b_im, v_c_re, v_c_im, v_d_skip, v_w_glu_a, v_w_glu_b, v_w_out, v_g_ffn, v_w_ffn_gate, v_w_ffn_up, v_w_ffn_down):
    given = dict(x=x, g_mix=g_mix, w_in=w_in, g_q=g_q, g_k=g_k, w_attn_proj=w_attn_proj, lambda_re=lambda_re, lambda_im=lambda_im, log_dt=log_dt, b_re=b_re, b_im=b_im, c_re=c_re, c_im=c_im, d_skip=d_skip, w_glu_a=w_glu_a, w_glu_b=w_glu_b, w_out=w_out, g_ffn=g_ffn, w_ffn_gate=w_ffn_gate, w_ffn_up=w_ffn_up, w_ffn_down=w_ffn_down, loss_target=loss_target, m_g_mix=m_g_mix, m_w_in=m_w_in, m_g_q=m_g_q, m_g_k=m_g_k, m_w_attn_proj=m_w_attn_proj, m_lambda_re=m_lambda_re, m_lambda_im=m_lambda_im, m_log_dt=m_log_dt, m_b_re=m_b_re, m_b_im=m_b_im, m_c_re=m_c_re, m_c_im=m_c_im, m_d_skip=m_d_skip, m_w_glu_a=m_w_glu_a, m_w_glu_b=m_w_glu_b, m_w_out=m_w_out, m_g_ffn=m_g_ffn, m_w_ffn_gate=m_w_ffn_gate, m_w_ffn_up=m_w_ffn_up, m_w_ffn_down=m_w_ffn_down, v_g_mix=v_g_mix, v_w_in=v_w_in, v_g_q=v_g_q, v_g_k=v_g_k, v_w_attn_proj=v_w_attn_proj, v_lambda_re=v_lambda_re, v_lambda_im=v_lambda_im, v_log_dt=v_log_dt, v_b_re=v_b_re, v_b_im=v_b_im, v_c_re=v_c_re, v_c_im=v_c_im, v_d_skip=v_d_skip, v_w_glu_a=v_w_glu_a, v_w_glu_b=v_w_glu_b, v_w_out=v_w_out, v_g_ffn=v_g_ffn, v_w_ffn_gate=v_w_ffn_gate, v_w_ffn_up=v_w_ffn_up, v_w_ffn_down=v_w_ffn_down)
    weights = {n: given[n] for n in TWIN_WEIGHTS}
    shared = {n: given[n] for n in SHARED_INPUTS}
    per_example = {n: given[n] for n in ['x']}
    grad_fn = _jax.value_and_grad(_loss, argnums=(0, 1))

    def one_microbatch(ex, loss_target):
        ex = dict(ex)
        diff = ex.pop(TWIN_DIFF_INPUT)
        return grad_fn(weights, diff, {**shared, **ex}, loss_target)

    if N_MICROBATCH == 1:
        loss, (grad_w, grad_x) = one_microbatch(per_example, given["loss_target"])
    else:
        def body(carry, xs):
            loss_sum, grad_sum = carry
            l_k, (gw_k, gx_k) = one_microbatch(xs[0], xs[1])
            with _jax.named_scope("update"):
                return (loss_sum + l_k, _jax.tree.map(_jnp.add, grad_sum, gw_k)), gx_k

        init = (_jnp.zeros((), _jnp.float32), _jax.tree.map(_jnp.zeros_like, weights))
        (loss, grad_w), grad_x = _jax.lax.scan(body, init, (per_example, given["loss_target"]))
    with _jax.named_scope("update"):
        delta_w, new_m, new_v = {}, {}, {}
        for n in TWIN_WEIGHTS:
            delta_w[n], new_m[n], new_v[n] = _adamw(weights[n], grad_w[n], given["m_" + n], given["v_" + n])
    return (loss, grad_x, *[grad_w[n] for n in TWIN_WEIGHTS], *[delta_w[n] for n in TWIN_WEIGHTS],
            *[new_m[n] for n in TWIN_WEIGHTS], *[new_v[n] for n in TWIN_WEIGHTS])
```

```python
import functools
import math

import jax
import jax.numpy as jnp
from jax import lax
from jax.experimental import pallas as pl
from jax.experimental.pallas import tpu as pltpu

F32 = jnp.float32
BF16 = jnp.bfloat16

D_MODEL = 1024
DEPTH = 4
HEAD_DIM = 64
N_HEADS = 8
ATTN_WIDTH = N_HEADS * HEAD_DIM
ATTN_PATTERN = ((128, 1), (512, 4), (2048, 16))
N_GROUPS = len(ATTN_PATTERN)
BLK = 128
SSM_WIDTH = 512
SSM_GROUP = 16
SSM_GROUPS = 32
SSM_STATE = 64
D_FF = 2816
IN_COLS = 7168
EPS = 1e-6
ADAM_LR, ADAM_B1, ADAM_B2, ADAM_EPS, ADAM_WD, ADAM_STEP = 0.001, 0.9, 0.999, 1e-08, 0.01, 10

N_CHIPS = 4
MESH = pl.DeviceIdType.MESH

LANES = 128
SUBLANES = 8
VMEM_LIMIT = 56 * 1024 * 1024

TM = 512
TM_MIX = 256

SSM_TB = 512
SSM_TC = 64
SSM_SUB = SUBLANES
SSM_PITCH = 72
N_SLAB = SSM_GROUPS * SSM_STATE // LANES
SLABS_PER_WIN = 4
SCAN_GROUP = 4


def _params(sem=None, collective=False):
    return pltpu.CompilerParams(dimension_semantics=sem, vmem_limit_bytes=VMEM_LIMIT)


def _call(body, *, name, grid, in_specs, out_specs, out_shape, scratch=(), sem=None, aliases=None,
          prefetch=0):
    kw = {}
    if aliases:
        kw["input_output_aliases"] = aliases
    if prefetch:
        gs = pltpu.PrefetchScalarGridSpec(num_scalar_prefetch=prefetch, grid=grid, in_specs=in_specs,
                                          out_specs=out_specs, scratch_shapes=list(scratch))
        return pl.pallas_call(body, name=name, grid_spec=gs, out_shape=out_shape,
                              compiler_params=_params(sem), **kw)
    return pl.pallas_call(body, name=name, grid=grid, in_specs=in_specs, out_specs=out_specs,
                          out_shape=out_shape, scratch_shapes=list(scratch),
                          compiler_params=_params(sem), **kw)


def _sds(shape, dtype):
    return jax.ShapeDtypeStruct(shape, dtype)


def _sigmoid(v):
    return 1.0 / (1.0 + jnp.exp(-v))


def _dot(a, b):
    return jnp.dot(a, b, preferred_element_type=F32)


def _dot_nt(a, b):
    return lax.dot_general(a, b, (((1,), (1,)), ((), ())), preferred_element_type=F32)


def _dot_tn(a, b):
    return lax.dot_general(a, b, (((0,), (0,)), ((), ())), preferred_element_type=F32)


def _in_proj_fwd(x, g, w):
    L = x.shape[0]
    ns = w.shape[2]
    tn = ns // 2
    nj = ns // tn

    def body(x_ref, g_ref, w_ref, z_ref, h_ref):
        @pl.when(pl.program_id(1) == 0)
        def _():
            xv = x_ref[...]
            r = lax.rsqrt(jnp.mean(xv * xv, axis=-1, keepdims=True) + EPS)
            h_ref[...] = (xv * r * g_ref[...]).astype(BF16)
        z_ref[...] = _dot(h_ref[...], w_ref[...]).astype(BF16)

    return _call(
        body, name="in_proj_fwd", grid=(L // TM, N_CHIPS * nj),
        in_specs=[pl.BlockSpec((TM, D_MODEL), lambda i, j: (i, 0)),
                  pl.BlockSpec((1, D_MODEL), lambda i, j: (0, 0)),
                  pl.BlockSpec((None, D_MODEL, tn), lambda i, j: (j // nj, 0, j % nj))],
        out_specs=[pl.BlockSpec((TM, tn), lambda i, j: (i, j)),
                   pl.BlockSpec((TM, D_MODEL), lambda i, j: (i, 0))],
        out_shape=[_sds((L, N_CHIPS * ns), BF16), _sds((L, D_MODEL), BF16)],
        sem=("parallel", "arbitrary"))(x, g, w)


def _head_norm(t, gain):
    r = lax.rsqrt(jnp.mean(t * t, axis=-1, keepdims=True) + EPS)
    th = t * r
    return th, th * gain, r


def _attn_masks():
    qi = lax.broadcasted_iota(jnp.int32, (BLK, BLK), 0)
    ki = lax.broadcasted_iota(jnp.int32, (BLK, BLK), 1)
    return qi >= ki, ki >= qi


def _attn_fwd(z, gq, gk, gi):
    L = z.shape[0]
    _, d = ATTN_PATTERN[gi]
    M = L // d
    nb = M // BLK
    zv = z.reshape(M, d * IN_COLS)
    cpb = IN_COLS // ATTN_WIDTH
    scale = HEAD_DIM ** -0.5

    def body(q_ref, kc_ref, kp_ref, vc_ref, vp_ref, gq_ref, gk_ref, o_ref, l_ref):
        n = pl.program_id(1)
        mask_c, mask_p0 = _attn_masks()
        mask_p = jnp.logical_and(mask_p0, n > 0)
        gqv = gq_ref[...]
        gkv = gk_ref[...]
        for h in range(N_HEADS):
            sl = slice(h * HEAD_DIM, (h + 1) * HEAD_DIM)
            _, qn, _ = _head_norm(q_ref[:, sl].astype(F32), gqv)
            _, kcn, _ = _head_norm(kc_ref[:, sl].astype(F32), gkv)
            _, kpn, _ = _head_norm(kp_ref[:, sl].astype(F32), gkv)
            qb = qn.astype(BF16)
            s_c = jnp.where(mask_c, _dot_nt(qb, kcn.astype(BF16)) * scale, -jnp.inf)
            s_p = jnp.where(mask_p, _dot_nt(qb, kpn.astype(BF16)) * scale, -jnp.inf)
            m = jnp.maximum(jnp.max(s_c, axis=-1, keepdims=True), jnp.max(s_p, axis=-1, keepdims=True))
            p_c = jnp.exp(s_c - m)
            p_p = jnp.exp(s_p - m)
            den = jnp.sum(p_c, axis=-1, keepdims=True) + jnp.sum(p_p, axis=-1, keepdims=True)
            acc = _dot(p_c.astype(BF16), vc_ref[:, sl]) + _dot(p_p.astype(BF16), vp_ref[:, sl])
            o_ref[:, sl] = acc / den
            l_ref[:, sl] = jnp.broadcast_to(m + jnp.log(den), (BLK, HEAD_DIM))

    def col(kind):
        return lambda r, n: (n, r * cpb + 3 * kind + gi)

    def colp(kind):
        return lambda r, n: (jnp.maximum(n - 1, 0), r * cpb + 3 * kind + gi)

    blk = (BLK, ATTN_WIDTH)
    o, l = _call(
        body, name=f"attn_fwd_g{gi}", grid=(d, nb),
        in_specs=[pl.BlockSpec(blk, col(0)), pl.BlockSpec(blk, col(1)), pl.BlockSpec(blk, colp(1)),
                  pl.BlockSpec(blk, col(2)), pl.BlockSpec(blk, colp(2)),
                  pl.BlockSpec((1, HEAD_DIM), lambda r, n: (0, 0)),
                  pl.BlockSpec((1, HEAD_DIM), lambda r, n: (0, 0))],
        out_specs=[pl.BlockSpec(blk, lambda r, n: (n, r)), pl.BlockSpec(blk, lambda r, n: (n, r))],
        out_shape=[_sds((M, d * ATTN_WIDTH), F32), _sds((M, d * ATTN_WIDTH), F32)],
        sem=("parallel", "parallel"))(zv, zv, zv, zv, zv, gq, gk)
    return o.reshape(L, ATTN_WIDTH), l.reshape(L, ATTN_WIDTH)


def _gelu(v):
    c = math.sqrt(2.0 / math.pi)
    return 0.5 * v * (1.0 + jnp.tanh(c * (v + 0.044715 * v * v * v)))


def _gelu_grad(v):
    c = math.sqrt(2.0 / math.pi)
    t = jnp.tanh(c * (v + 0.044715 * v * v * v))
    return 0.5 * (1.0 + t) + 0.5 * v * (1.0 - t * t) * c * (1.0 + 3.0 * 0.044715 * v * v)


def _ssm_fill(u, bwre_ref, bwim_ref, sre, sim):
    for k in range(N_SLAB):
        w = k // SLABS_PER_WIN
        uw = u[:, w * LANES:(w + 1) * LANES]
        br = _dot(uw, bwre_ref[k])
        bi = _dot(uw, bwim_ref[k])
        for j in range(SSM_SUB):
            sre[k, j * SSM_PITCH:j * SSM_PITCH + SSM_TC, :] = br[j * SSM_TC:(j + 1) * SSM_TC, :]
            sim[k, j * SSM_PITCH:j * SSM_PITCH + SSM_TC, :] = bi[j * SSM_TC:(j + 1) * SSM_TC, :]


def _rows(i):
    return pl.ds(i, SSM_SUB, stride=SSM_PITCH)


def _slab_rows(ref, k):
    return jnp.concatenate([ref[k, j * SSM_PITCH:j * SSM_PITCH + SSM_TC, :] for j in range(SSM_SUB)], axis=0)


def _bcast(ref, k):
    return jnp.broadcast_to(ref[pl.ds(k, 1), :], (SSM_SUB, LANES))


def _scan(sre, sim, are_ref, aim_ref, k0, init, *, reverse, store, sign=1.0):
    ar = [_bcast(are_ref, k0 + kk) for kk in range(SCAN_GROUP)]
    ai = [sign * _bcast(aim_ref, k0 + kk) for kk in range(SCAN_GROUP)]

    def step(t, carry):
        i = SSM_TC - 1 - t if reverse else t
        out = []
        for kk in range(SCAN_GROUP):
            k = k0 + kk
            xr, xi = carry[2 * kk], carry[2 * kk + 1]
            nr = ar[kk] * xr - ai[kk] * xi + sre[k, _rows(i), :]
            ni = ar[kk] * xi + ai[kk] * xr + sim[k, _rows(i), :]
            if store:
                sre[k, _rows(i), :] = nr
                sim[k, _rows(i), :] = ni
            out += [nr, ni]
        return tuple(out)

    flat = []
    for re, im in init:
        flat += [re, im]
    res = lax.fori_loop(0, SSM_TC, step, tuple(flat))
    return [(res[2 * kk], res[2 * kk + 1]) for kk in range(SCAN_GROUP)]


def _ssm_seeds(ends_re, ends_im, a64re_ref, a64im_ref, carry_re, carry_im, seed_re, seed_im, k,
               *, reverse, sign=1.0):
    ar = a64re_ref[pl.ds(k, 1), :]
    ai = sign * a64im_ref[pl.ds(k, 1), :]
    cr = carry_re[pl.ds(k, 1), :]
    ci = carry_im[pl.ds(k, 1), :]
    order = range(SSM_SUB - 1, -1, -1) if reverse else range(SSM_SUB)
    for j in order:
        seed_re[k, pl.ds(j, 1), :] = cr
        seed_im[k, pl.ds(j, 1), :] = ci
        er = ends_re[k, pl.ds(j, 1), :]
        ei = ends_im[k, pl.ds(j, 1), :]
        cr, ci = ar * cr - ai * ci + er, ar * ci + ai * cr + ei
    carry_re[pl.ds(k, 1), :] = cr
    carry_im[pl.ds(k, 1), :] = ci


def _ssm_specs_consts():
    c2 = pl.BlockSpec((N_SLAB, LANES), lambda b: (0, 0))
    c3 = pl.BlockSpec((N_SLAB, LANES, LANES), lambda b: (0, 0, 0))
    return c2, c3


def _ssm_scratch():
    rows = SSM_SUB * SSM_PITCH
    return [pltpu.VMEM((N_SLAB, rows, LANES), F32), pltpu.VMEM((N_SLAB, rows, LANES), F32)]


def _ssm_fwd(z, pk, dskip):
    L = z.shape[0]
    nb = L // SSM_TB
    ucol = (3 * N_GROUPS * ATTN_WIDTH) // SSM_WIDTH

    def body(u_ref, are_ref, aim_ref, a64re_ref, a64im_ref, bwre_ref, bwim_ref, cwre_ref, cwim_ref, d_ref,
             ypre_ref, yact_ref, sdre_ref, sdim_ref, sre, sim, carry_re, carry_im, ends_re, ends_im,
             seed_re, seed_im):
        @pl.when(pl.program_id(0) == 0)
        def _():
            carry_re[...] = jnp.zeros_like(carry_re)
            carry_im[...] = jnp.zeros_like(carry_im)

        u = u_ref[...]
        _ssm_fill(u, bwre_ref, bwim_ref, sre, sim)
        zero = jnp.zeros((SSM_SUB, LANES), F32)
        for k0 in range(0, N_SLAB, SCAN_GROUP):
            ends = _scan(sre, sim, are_ref, aim_ref, k0, [(zero, zero)] * SCAN_GROUP, reverse=False, store=False)
            for kk in range(SCAN_GROUP):
                ends_re[k0 + kk] = ends[kk][0]
                ends_im[k0 + kk] = ends[kk][1]
            for kk in range(SCAN_GROUP):
                _ssm_seeds(ends_re, ends_im, a64re_ref, a64im_ref, carry_re, carry_im, seed_re, seed_im,
                           k0 + kk, reverse=False)
            init = [(seed_re[k0 + kk], seed_im[k0 + kk]) for kk in range(SCAN_GROUP)]
            _scan(sre, sim, are_ref, aim_ref, k0, init, reverse=False, store=True)
        sdre_ref[...] = seed_re[...]
        sdim_ref[...] = seed_im[...]
        for w in range(N_SLAB // SLABS_PER_WIN):
            acc = jnp.zeros((SSM_TB, LANES), F32)
            for kk in range(SLABS_PER_WIN):
                k = w * SLABS_PER_WIN + kk
                acc += _dot(_slab_rows(sre, k).astype(BF16), cwre_ref[k])
                acc -= _dot(_slab_rows(sim, k).astype(BF16), cwim_ref[k])
            cols = slice(w * LANES, (w + 1) * LANES)
            ypre = acc + d_ref[:, cols] * u[:, cols].astype(F32)
            ypre_ref[:, cols] = ypre
            yact_ref[:, cols] = _gelu(ypre).astype(BF16)

    c2, c3 = _ssm_specs_consts()
    seed_spec = pl.BlockSpec((None, N_SLAB, SSM_SUB, LANES), lambda b: (b, 0, 0, 0))
    small = pltpu.VMEM((N_SLAB, LANES), F32)
    tile = pltpu.VMEM((N_SLAB, SSM_SUB, LANES), F32)
    return _call(
        body, name="ssm_fwd", grid=(nb,),
        in_specs=[pl.BlockSpec((SSM_TB, SSM_WIDTH), lambda b: (b, ucol)), c2, c2, c2, c2, c3, c3, c3, c3,
                  pl.BlockSpec((1, SSM_WIDTH), lambda b: (0, 0))],
        out_specs=[pl.BlockSpec((SSM_TB, SSM_WIDTH), lambda b: (b, 0)),
                   pl.BlockSpec((SSM_TB, SSM_WIDTH), lambda b: (b, 0)), seed_spec, seed_spec],
        out_shape=[_sds((L, SSM_WIDTH), F32), _sds((L, SSM_WIDTH), BF16),
                   _sds((nb, N_SLAB, SSM_SUB, LANES), F32), _sds((nb, N_SLAB, SSM_SUB, LANES), F32)],
        scratch=_ssm_scratch() + [small, small, tile, tile, tile, tile],
        sem=("arbitrary",))(z, pk["a_re"], pk["a_im"], pk["a64_re"], pk["a64_im"],
                            pk["bw_re"].astype(BF16), pk["bw_im"].astype(BF16),
                            pk["cw_re"].astype(BF16), pk["cw_im"].astype(BF16), dskip)


def _combine_weights(l0, l1, l2):
    m = jnp.maximum(jnp.maximum(l0, l1), l2)
    e0, e1, e2 = jnp.exp(l0 - m), jnp.exp(l1 - m), jnp.exp(l2 - m)
    inv = 1.0 / (e0 + e1 + e2)
    return e0 * inv, e1 * inv, e2 * inv


def _mix_fwd(x, z, os_, ls_, yact, w_ap, w_ga, w_gb, w_out):
    L = x.shape[0]
    cs = D_MODEL // N_CHIPS
    ga_col = (3 * N_GROUPS * ATTN_WIDTH + SSM_WIDTH) // D_MODEL

    def body(x_ref, ga_ref, gs_ref, o0, o1, o2, l0, l1, l2, y_ref, wap_ref, wga_ref, wgb_ref, wout_ref,
             x1_ref, a_ref, aout_ref, sa_ref, sb_ref, mix_ref):
        w0, w1, w2 = _combine_weights(l0[...], l1[...], l2[...])
        a = (w0 * o0[...] + w1 * o1[...] + w2 * o2[...]).astype(BF16)
        a_ref[...] = a
        y = y_ref[...]
        for s in range(N_CHIPS):
            cols = slice(s * cs, (s + 1) * cs)
            aout_ref[:, cols] = _dot(a, wap_ref[s]).astype(BF16)
            sa_ref[:, cols] = _dot(y, wga_ref[s]).astype(BF16)
            sb_ref[:, cols] = _dot(y, wgb_ref[s]).astype(BF16)
        s_out = sa_ref[...].astype(F32) * _sigmoid(sb_ref[...].astype(F32))
        mix = (_sigmoid(ga_ref[...].astype(F32)) * aout_ref[...].astype(F32)
               + _sigmoid(gs_ref[...].astype(F32)) * s_out).astype(BF16)
        mix_ref[...] = mix
        x1_ref[...] = x_ref[...] + _dot(mix, wout_ref[...])

    tok = lambda w: pl.BlockSpec((TM_MIX, w), lambda i: (i, 0))
    wsm = pl.BlockSpec((N_CHIPS, ATTN_WIDTH, cs), lambda i: (0, 0, 0))
    return _call(
        body, name="mix_fwd", grid=(L // TM_MIX,),
        in_specs=[tok(D_MODEL), pl.BlockSpec((TM_MIX, D_MODEL), lambda i: (i, ga_col)),
                  pl.BlockSpec((TM_MIX, D_MODEL), lambda i: (i, ga_col + 1))]
                 + [tok(ATTN_WIDTH)] * 7 + [wsm, wsm, wsm, pl.BlockSpec((D_MODEL, D_MODEL), lambda i: (0, 0))],
        out_specs=[tok(D_MODEL), tok(ATTN_WIDTH), tok(D_MODEL), tok(D_MODEL), tok(D_MODEL), tok(D_MODEL)],
        out_shape=[_sds((L, D_MODEL), F32), _sds((L, ATTN_WIDTH), BF16)] + [_sds((L, D_MODEL), BF16)] * 4,
        sem=("parallel",))(x, z, z, *os_, *ls_, yact, w_ap, w_ga, w_gb, w_out.reshape(D_MODEL, D_MODEL))


def _ffn_fwd(x1, g, w_g, w_u, w_d):
    L = x1.shape[0]
    fs = D_FF // N_CHIPS

    def body(x_ref, g_ref, wg_ref, wu_ref, wd_ref, x2_ref, h_ref, gate_ref, up_ref, act_ref, acc):
        s = pl.program_id(1)

        @pl.when(s == 0)
        def _():
            xv = x_ref[...]
            r = lax.rsqrt(jnp.mean(xv * xv, axis=-1, keepdims=True) + EPS)
            h_ref[...] = (xv * r * g_ref[...]).astype(BF16)
            acc[...] = jnp.zeros_like(acc)

        h = h_ref[...]
        gate = _dot(h, wg_ref[...])
        up = _dot(h, wu_ref[...])
        act = (gate * _sigmoid(gate) * up).astype(BF16)
        gate_ref[...] = gate.astype(BF16)
        up_ref[...] = up.astype(BF16)
        act_ref[...] = act
        acc[...] += _dot(act, wd_ref[...])

        @pl.when(s == N_CHIPS - 1)
        def _():
            x2_ref[...] = x_ref[...] + acc[...]

    tok = pl.BlockSpec((TM, D_MODEL), lambda i, s: (i, 0))
    ffs = pl.BlockSpec((None, TM, fs), lambda i, s: (s, i, 0))
    return _call(
        body, name="ffn_fwd", grid=(L // TM, N_CHIPS),
        in_specs=[tok, pl.BlockSpec((1, D_MODEL), lambda i, s: (0, 0)),
                  pl.BlockSpec((None, D_MODEL, fs), lambda i, s: (s, 0, 0)),
                  pl.BlockSpec((None, D_MODEL, fs), lambda i, s: (s, 0, 0)),
                  pl.BlockSpec((None, fs, D_MODEL), lambda i, s: (s, 0, 0))],
        out_specs=[tok, tok, ffs, ffs, ffs],
        out_shape=[_sds((L, D_MODEL), F32), _sds((L, D_MODEL), BF16)] + [_sds((N_CHIPS, L, fs), BF16)] * 3,
        scratch=[pltpu.VMEM((TM, D_MODEL), F32)],
        sem=("parallel", "arbitrary"))(x1, g, w_g, w_u, w_d)


def _loss_head(xl, target):
    L = xl.shape[0]

    def body(x_ref, t_ref, dx_ref, loss_ref, acc):
        i = pl.program_id(0)

        @pl.when(i == 0)
        def _():
            acc[...] = jnp.zeros_like(acc)

        e = x_ref[...] - t_ref[...]
        dx_ref[...] = e * (1.0 / D_MODEL)
        acc[...] += jnp.sum((e * e).reshape(TM // SUBLANES, SUBLANES, D_MODEL), axis=0)

        @pl.when(i == pl.num_programs(0) - 1)
        def _():
            loss_ref[...] = (0.5 / D_MODEL) * jnp.sum(acc[...]).reshape(1, 1)

    tok = pl.BlockSpec((TM, D_MODEL), lambda i: (i, 0))
    return _call(
        body, name="loss_head", grid=(L // TM,), in_specs=[tok, tok],
        out_specs=[tok, pl.BlockSpec((1, 1), lambda i: (0, 0))],
        out_shape=[_sds((L, D_MODEL), F32), _sds((1, 1), F32)],
        scratch=[pltpu.VMEM((SUBLANES, D_MODEL), F32)], sem=("arbitrary",))(xl, target)


def _ssm_pack(lam_re, lam_im, log_dt, b_re, b_im, c_re, c_im):
    dt = jnp.exp(log_dt)[:, None]
    mag = jnp.exp(lam_re * dt)
    ang = lam_im * dt
    ar = mag * jnp.cos(ang)
    ai = mag * jnp.sin(ang)
    nr = ar - 1.0
    ni = ai
    den = lam_re * lam_re + lam_im * lam_im
    cr = ((nr * lam_re + ni * lam_im) / den)[..., None]
    ci = ((ni * lam_re - nr * lam_im) / den)[..., None]
    bbr = cr * b_re - ci * b_im
    bbi = cr * b_im + ci * b_re
    eye = jnp.eye(SSM_GROUPS, dtype=F32)
    n_state = N_SLAB * LANES

    def b_windows(bb):
        full = jnp.einsum('gpc,gh->gchp', bb, eye).reshape(SSM_WIDTH, n_state)
        return jnp.stack([full[(k // SLABS_PER_WIN) * LANES:(k // SLABS_PER_WIN + 1) * LANES,
                               k * LANES:(k + 1) * LANES] for k in range(N_SLAB)])

    def c_windows(cc):
        full = jnp.einsum('gcp,gh->hpgc', cc, eye).reshape(n_state, SSM_WIDTH)
        return jnp.stack([full[k * LANES:(k + 1) * LANES,
                               (k // SLABS_PER_WIN) * LANES:(k // SLABS_PER_WIN + 1) * LANES]
                          for k in range(N_SLAB)])

    pr, pi = ar, ai
    for _ in range(int(math.log2(SSM_TC))):
        pr, pi = pr * pr - pi * pi, 2.0 * pr * pi
    return dict(a_re=ar.reshape(N_SLAB, LANES), a_im=ai.reshape(N_SLAB, LANES),
                a64_re=pr.reshape(N_SLAB, LANES), a64_im=pi.reshape(N_SLAB, LANES),
                bw_re=b_windows(bbr), bw_im=b_windows(bbi), cw_re=c_windows(c_re), cw_im=c_windows(c_im))


def _layer_fwd(x, p):
    z, h = _in_proj_fwd(x, p["g_mix"], p["w_in"])
    os_, ls_ = [], []
    for gi in range(N_GROUPS):
        o, l = _attn_fwd(z, p["g_q"], p["g_k"], gi)
        os_.append(o)
        ls_.append(l)
    pk = _ssm_pack(p["lambda_re"], p["lambda_im"], p["log_dt"], p["b_re"], p["b_im"], p["c_re"], p["c_im"])
    ypre, yact, sd_re, sd_im = _ssm_fwd(z, pk, p["d_skip"])
    x1, a, aout, sa, sb, mix = _mix_fwd(x, z, os_, ls_, yact, p["w_attn_proj"], p["w_glu_a"], p["w_glu_b"],
                                       p["w_out"])
    x2, h2, gate, up, act = _ffn_fwd(x1, p["g_ffn"], p["w_ffn_gate"], p["w_ffn_up"], p["w_ffn_down"])
    saved = dict(x=x, z=z, h=h, os=os_, ls=ls_, pk=pk, ypre=ypre, yact=yact, sd_re=sd_re, sd_im=sd_im,
                 x1=x1, a=a, aout=aout, sa=sa, sb=sb, mix=mix, h2=h2, gate=gate, up=up, act=act)
    return x2, saved


def _rms_bwd(xv, g, dh):
    r = lax.rsqrt(jnp.mean(xv * xv, axis=-1, keepdims=True) + EPS)
    xn = xv * r
    dxn = dh * g
    dx = r * (dxn - xn * jnp.mean(dxn * xn, axis=-1, keepdims=True))
    dg = jnp.sum((dh * xn).reshape(xv.shape[0] // SUBLANES, SUBLANES, xv.shape[1]), axis=0)
    return dx, dg


def _ffn_bwd(dx2, x1, g, gate, up, w_g, w_u, w_d):
    L = x1.shape[0]
    fs = D_FF // N_CHIPS

    def body(dx_ref, x_ref, g_ref, gate_ref, up_ref, wg_ref, wu_ref, wd_ref,
             dx1_ref, dgate_ref, dup_ref, dg_ref, acc, dgacc):
        i, s = pl.program_id(0), pl.program_id(1)

        @pl.when(s == 0)
        def _():
            acc[...] = jnp.zeros_like(acc)

        @pl.when(jnp.logical_and(i == 0, s == 0))
        def _():
            dgacc[...] = jnp.zeros_like(dgacc)

        dact = _dot_nt(dx_ref[...].astype(BF16), wd_ref[...])
        gt = gate_ref[...].astype(F32)
        sg = _sigmoid(gt)
        dgate = (dact * up_ref[...].astype(F32) * (sg * (1.0 + gt * (1.0 - sg)))).astype(BF16)
        dup = (dact * gt * sg).astype(BF16)
        dgate_ref[...] = dgate
        dup_ref[...] = dup
        acc[...] += _dot_nt(dgate, wg_ref[...]) + _dot_nt(dup, wu_ref[...])

        @pl.when(s == N_CHIPS - 1)
        def _():
            dx, dg = _rms_bwd(x_ref[...], g_ref[...], acc[...])
            dx1_ref[...] = dx_ref[...] + dx
            dgacc[...] += dg

        @pl.when(jnp.logical_and(i == pl.num_programs(0) - 1, s == N_CHIPS - 1))
        def _():
            dg_ref[...] = jnp.sum(dgacc[...], axis=0, keepdims=True)

    tok = pl.BlockSpec((TM, D_MODEL), lambda i, s: (i, 0))
    ffs = pl.BlockSpec((None, TM, fs), lambda i, s: (s, i, 0))
    vec = pl.BlockSpec((1, D_MODEL), lambda i, s: (0, 0))
    return _call(
        body, name="ffn_bwd", grid=(L // TM, N_CHIPS),
        in_specs=[tok, tok, vec, ffs, ffs,
                  pl.BlockSpec((None, D_MODEL, fs), lambda i, s: (s, 0, 0)),
                  pl.BlockSpec((None, D_MODEL, fs), lambda i, s: (s, 0, 0)),
                  pl.BlockSpec((None, fs, D_MODEL), lambda i, s: (s, 0, 0))],
        out_specs=[tok, ffs, ffs, vec],
        out_shape=[_sds((L, D_MODEL), F32), _sds((N_CHIPS, L, fs), BF16), _sds((N_CHIPS, L, fs), BF16),
                   _sds((1, D_MODEL), F32)],
        scratch=[pltpu.VMEM((TM, D_MODEL), F32), pltpu.VMEM((SUBLANES, D_MODEL), F32)],
        sem=("arbitrary", "arbitrary"))(dx2, x1, g, gate, up, w_g, w_u, w_d)


def _wgrad(a, b, *, name, grid_kn, a_spec, b_spec, out_shape, out_spec):
    L = a.shape[-2]
    nl = L // TM

    def body(a_ref, b_ref, o_ref):
        @pl.when(pl.program_id(2) == 0)
        def _():
            o_ref[...] = jnp.zeros_like(o_ref)
        o_ref[...] += _dot_tn(a_ref[...].astype(BF16), b_ref[...].astype(BF16))

    return _call(body, name=name, grid=(*grid_kn, nl), in_specs=[a_spec, b_spec], out_specs=out_spec,
                 out_shape=out_shape, sem=("parallel", "parallel", "arbitrary"))(a, b)


def _wgrad_cols(a, b, name):
    K, N = a.shape[1], b.shape[1]
    ns = N // N_CHIPS
    tn = ns // 2 if ns % (2 * LANES) == 0 else ns
    nj = ns // tn
    return _wgrad(a, b, name=name, grid_kn=(1, N_CHIPS * nj),
                  a_spec=pl.BlockSpec((TM, K), lambda i, j, t: (t, 0)),
                  b_spec=pl.BlockSpec((TM, tn), lambda i, j, t: (t, j)),
                  out_shape=_sds((N_CHIPS, K, ns), F32),
                  out_spec=pl.BlockSpec((None, K, tn), lambda i, j, t: (j // nj, 0, j % nj)))


def _wgrad_full(a, b, name):
    K, N = a.shape[1], b.shape[1]
    return _wgrad(a, b, name=name, grid_kn=(1, 1),
                  a_spec=pl.BlockSpec((TM, K), lambda i, j, t: (t, 0)),
                  b_spec=pl.BlockSpec((TM, N), lambda i, j, t: (t, 0)),
                  out_shape=_sds((K, N), F32), out_spec=pl.BlockSpec((K, N), lambda i, j, t: (0, 0)))


def _wgrad_ff_cols(a, b, name):
    K, fs = a.shape[1], b.shape[2]
    return _wgrad(a, b, name=name, grid_kn=(1, N_CHIPS),
                  a_spec=pl.BlockSpec((TM, K), lambda i, j, t: (t, 0)),
                  b_spec=pl.BlockSpec((None, TM, fs), lambda i, j, t: (j, t, 0)),
                  out_shape=_sds((N_CHIPS, K, fs), F32),
                  out_spec=pl.BlockSpec((None, K, fs), lambda i, j, t: (j, 0, 0)))


def _wgrad_ff_rows(a, b, name):
    fs, N = a.shape[2], b.shape[1]
    return _wgrad(a, b, name=name, grid_kn=(N_CHIPS, 1),
                  a_spec=pl.BlockSpec((None, TM, fs), lambda i, j, t: (i, t, 0)),
                  b_spec=pl.BlockSpec((TM, N), lambda i, j, t: (t, 0)),
                  out_shape=_sds((N_CHIPS, fs, N), F32),
                  out_spec=pl.BlockSpec((None, fs, N), lambda i, j, t: (i, 0, 0)))


def _head_sum_matrix():
    h = jnp.arange(ATTN_WIDTH) // HEAD_DIM
    return (h[:, None] == h[None, :]).astype(BF16)


def _mix_bwd(dx, z, aout, sa, sb, os_, ls_, ypre, w_ap, w_ga, w_gb, w_out):
    L = dx.shape[0]
    cs = D_MODEL // N_CHIPS
    ga_col = (3 * N_GROUPS * ATTN_WIDTH + SSM_WIDTH) // D_MODEL

    def body(dx_ref, ga_ref, gs_ref, aout_ref, sa_ref, sb_ref, o0, o1, o2, l0, l1, l2, ypre_ref,
             wap_ref, wga_ref, wgb_ref, wout_ref, hs_ref,
             dgates_ref, do0, do1, do2, dl0, dl1, dl2, gy_ref, daout_ref, dsa_ref, dsb_ref):
        dmix = _dot_nt(dx_ref[...].astype(BF16), wout_ref[...])
        sig_a = _sigmoid(ga_ref[...].astype(F32))
        sig_s = _sigmoid(gs_ref[...].astype(F32))
        a_out = aout_ref[...].astype(F32)
        s_a = sa_ref[...].astype(F32)
        sig_b = _sigmoid(sb_ref[...].astype(F32))
        s_out = s_a * sig_b
        daout = (dmix * sig_a).astype(BF16)
        daout_ref[...] = daout
        dgates_ref[:, :D_MODEL] = (dmix * a_out * sig_a * (1.0 - sig_a)).astype(BF16)
        dgates_ref[:, D_MODEL:] = (dmix * s_out * sig_s * (1.0 - sig_s)).astype(BF16)
        ds_out = dmix * sig_s
        dsa = (ds_out * sig_b).astype(BF16)
        dsb = (ds_out * s_a * sig_b * (1.0 - sig_b)).astype(BF16)
        dsa_ref[...] = dsa
        dsb_ref[...] = dsb
        da = jnp.zeros((TM_MIX, ATTN_WIDTH), F32)
        dy = jnp.zeros((TM_MIX, SSM_WIDTH), F32)
        for s in range(N_CHIPS):
            cols = slice(s * cs, (s + 1) * cs)
            da += _dot_nt(daout[:, cols], wap_ref[s])
            dy += _dot_nt(dsa[:, cols], wga_ref[s]) + _dot_nt(dsb[:, cols], wgb_ref[s])
        gy_ref[...] = dy * _gelu_grad(ypre_ref[...])
        w = _combine_weights(l0[...], l1[...], l2[...])
        hs = hs_ref[...]

        def head_sum(v):
            hi = v.astype(BF16)
            lo = (v - hi.astype(F32)).astype(BF16)
            return _dot(hi, hs) + _dot(lo, hs)

        t = [head_sum(da * o[...]) for o in (o0, o1, o2)]
        tbar = w[0] * t[0] + w[1] * t[1] + w[2] * t[2]
        for wg, tg, do_ref, dl_ref in zip(w, t, (do0, do1, do2), (dl0, dl1, dl2)):
            do_ref[...] = wg * da
            dl_ref[...] = wg * (tg - tbar)

    tok = lambda w: pl.BlockSpec((TM_MIX, w), lambda i: (i, 0))
    wsm = pl.BlockSpec((N_CHIPS, ATTN_WIDTH, cs), lambda i: (0, 0, 0))
    return _call(
        body, name="mix_bwd", grid=(L // TM_MIX,),
        in_specs=[tok(D_MODEL), pl.BlockSpec((TM_MIX, D_MODEL), lambda i: (i, ga_col)),
                  pl.BlockSpec((TM_MIX, D_MODEL), lambda i: (i, ga_col + 1)),
                  tok(D_MODEL), tok(D_MODEL), tok(D_MODEL)] + [tok(ATTN_WIDTH)] * 7
                 + [wsm, wsm, wsm, pl.BlockSpec((D_MODEL, D_MODEL), lambda i: (0, 0)),
                    pl.BlockSpec((ATTN_WIDTH, ATTN_WIDTH), lambda i: (0, 0))],
        out_specs=[tok(2 * D_MODEL)] + [tok(ATTN_WIDTH)] * 7 + [tok(D_MODEL)] * 3,
        out_shape=[_sds((L, 2 * D_MODEL), BF16)] + [_sds((L, ATTN_WIDTH), F32)] * 7 + [_sds((L, D_MODEL), BF16)] * 3,
        sem=("parallel",))(dx, z, z, aout, sa, sb, *os_, *ls_, ypre, w_ap, w_ga, w_gb,
                           w_out.reshape(D_MODEL, D_MODEL), _head_sum_matrix())


def _attn_bwd(z, o, l, do, dl, gq, gk, gi):
    L = z.shape[0]
    _, d = ATTN_PATTERN[gi]
    M = L // d
    nb = M // BLK
    zv = z.reshape(M, d * IN_COLS)
    cpb = IN_COLS // ATTN_WIDTH
    scale = HEAD_DIM ** -0.5
    tv = lambda t: t.reshape(M, d * ATTN_WIDTH)

    def body(q0_ref, q1_ref, k_ref, v_ref, o0_ref, o1_ref, l0_ref, l1_ref, do0_ref, do1_ref, dl0_ref, dl1_ref,
             gq_ref, gk_ref, dq_ref, dk_ref, dv_ref, dgq_ref, dgk_ref, carry):
        r, n = pl.program_id(0), pl.program_id(1)

        @pl.when(n == 0)
        def _():
            carry[...] = jnp.zeros_like(carry)

        @pl.when(jnp.logical_and(r == 0, n == 0))
        def _():
            dgq_ref[...] = jnp.zeros_like(dgq_ref)
            dgk_ref[...] = jnp.zeros_like(dgk_ref)

        mask_c, mask_p0 = _attn_masks()
        mask_p = jnp.logical_and(mask_p0, n < nb - 1)
        gqv = gq_ref[...]
        gkv = gk_ref[...]

        def rows8(t):
            return jnp.sum(t.reshape(BLK // SUBLANES, SUBLANES, HEAD_DIM), axis=0)

        for h in range(N_HEADS):
            sl = slice(h * HEAD_DIM, (h + 1) * HEAD_DIM)
            c1 = slice(h * HEAD_DIM, h * HEAD_DIM + 1)
            qh0, qn0, rq0 = _head_norm(q0_ref[:, sl].astype(F32), gqv)
            _, qn1, _ = _head_norm(q1_ref[:, sl].astype(F32), gqv)
            kh, kn, rk = _head_norm(k_ref[:, sl].astype(F32), gkv)
            qb0, qb1, kb = qn0.astype(BF16), qn1.astype(BF16), kn.astype(BF16)
            v = v_ref[:, sl]
            d0 = do0_ref[:, sl]
            d1 = do1_ref[:, sl]
            db0, db1 = d0.astype(BF16), d1.astype(BF16)
            delta0 = jnp.sum(d0 * o0_ref[:, sl], axis=-1, keepdims=True)
            delta1 = jnp.sum(d1 * o1_ref[:, sl], axis=-1, keepdims=True)
            p1 = jnp.where(mask_c, jnp.exp(_dot_nt(qb0, kb) * scale - l0_ref[:, c1]), 0.0)
            p2 = jnp.where(mask_p, jnp.exp(_dot_nt(qb1, kb) * scale - l1_ref[:, c1]), 0.0)
            ds1 = (p1 * (_dot_nt(db0, v) - delta0 + dl0_ref[:, c1])).astype(BF16)
            ds2 = (p2 * (_dot_nt(db1, v) - delta1 + dl1_ref[:, c1])).astype(BF16)
            dv_ref[:, sl] = (_dot_tn(p1.astype(BF16), db0) + _dot_tn(p2.astype(BF16), db1)).astype(BF16)
            dkn = (_dot_tn(ds1, qb0) + _dot_tn(ds2, qb1)) * scale
            dqn = _dot(ds1, kb) * scale + carry[:, sl]
            carry[:, sl] = _dot(ds2, kb) * scale
            dqh = dqn * gqv
            dq_ref[:, sl] = (rq0 * (dqh - qh0 * jnp.mean(dqh * qh0, axis=-1, keepdims=True))).astype(BF16)
            dkh = dkn * gkv
            dk_ref[:, sl] = (rk * (dkh - kh * jnp.mean(dkh * kh, axis=-1, keepdims=True))).astype(BF16)
            dgq_ref[...] += rows8(dqn * qh0)
            dgk_ref[...] += rows8(dkn * kh)

    nxt = lambda n: jnp.minimum(n + 1, nb - 1)
    blk = (BLK, ATTN_WIDTH)
    zc = lambda kind: pl.BlockSpec(blk, lambda r, n: (n, r * cpb + 3 * kind + gi))
    tc = pl.BlockSpec(blk, lambda r, n: (n, r))
    tn_ = pl.BlockSpec(blk, lambda r, n: (nxt(n), r))
    vec = pl.BlockSpec((1, HEAD_DIM), lambda r, n: (0, 0))
    acc = pl.BlockSpec((SUBLANES, HEAD_DIM), lambda r, n: (0, 0))
    dq, dk, dv, dgq, dgk = _call(
        body, name=f"attn_bwd_g{gi}", grid=(d, nb),
        in_specs=[zc(0), pl.BlockSpec(blk, lambda r, n: (nxt(n), r * cpb + gi)), zc(1), zc(2),
                  tc, tn_, tc, tn_, tc, tn_, tc, tn_, vec, vec],
        out_specs=[tc, tc, tc, acc, acc],
        out_shape=[_sds((M, d * ATTN_WIDTH), BF16)] * 3 + [_sds((SUBLANES, HEAD_DIM), F32)] * 2,
        scratch=[pltpu.VMEM((BLK, ATTN_WIDTH), F32)],
        sem=("arbitrary", "arbitrary"))(zv, zv, zv, zv, tv(o), tv(o), tv(l), tv(l), tv(do), tv(do), tv(dl), tv(dl),
                                        gq, gk)
    rs = lambda t: t.reshape(L, ATTN_WIDTH)
    return rs(dq), rs(dk), rs(dv), dgq, dgk


def _scan_rev_grad(sre, sim, rre, rim, are_ref, aim_ref, k0, init, seed_re, seed_im):
    ar = [_bcast(are_ref, k0 + kk) for kk in range(SCAN_GROUP)]
    ai = [-_bcast(aim_ref, k0 + kk) for kk in range(SCAN_GROUP)]

    def update(i, xprev, carry):
        out = []
        for kk in range(SCAN_GROUP):
            k = k0 + kk
            lr, li, dr, di = carry[4 * kk:4 * kk + 4]
            nr = ar[kk] * lr - ai[kk] * li + rre[k, _rows(i), :]
            ni = ar[kk] * li + ai[kk] * lr + rim[k, _rows(i), :]
            rre[k, _rows(i), :] = nr
            rim[k, _rows(i), :] = ni
            xr, xi = xprev(k)
            out += [nr, ni, dr + xr * nr + xi * ni, di + xr * ni - xi * nr]
        return tuple(out)

    def step(t, carry):
        i = SSM_TC - 1 - t
        return update(i, lambda k: (sre[k, _rows(i - 1), :], sim[k, _rows(i - 1), :]), carry)

    zero = jnp.zeros((SSM_SUB, LANES), F32)
    flat = []
    for re, im in init:
        flat += [re, im, zero, zero]
    res = lax.fori_loop(0, SSM_TC - 1, step, tuple(flat))
    res = update(0, lambda k: (seed_re[k], seed_im[k]), res)
    return [(res[4 * kk + 2], res[4 * kk + 3]) for kk in range(SCAN_GROUP)]


def _ssm_bwd(z, gy, pk, dskip, sd_re, sd_im):
    L = z.shape[0]
    nb = L // SSM_TB
    ucol = (3 * N_GROUPS * ATTN_WIDTH) // SSM_WIDTH
    nwin = N_SLAB // SLABS_PER_WIN

    def body(u_ref, gy_ref, are_ref, aim_ref, a64re_ref, a64im_ref, bwre_ref, bwim_ref, cwre_ref, cwim_ref, d_ref,
             sdre_ref, sdim_ref,
             du_ref, dare_ref, daim_ref, dbre_ref, dbim_ref, dcre_ref, dcim_ref, dd_ref,
             sre, sim, rre, rim, carry_re, carry_im, ends_re, ends_im, seed_re, seed_im):
        @pl.when(pl.program_id(0) == 0)
        def _():
            carry_re[...] = jnp.zeros_like(carry_re)
            carry_im[...] = jnp.zeros_like(carry_im)
            for ref in (dare_ref, daim_ref, dbre_ref, dbim_ref, dcre_ref, dcim_ref, dd_ref):
                ref[...] = jnp.zeros_like(ref)

        u = u_ref[...]
        gyv = gy_ref[...]
        gyb = gyv.astype(BF16)
        _ssm_fill(u, bwre_ref, bwim_ref, sre, sim)
        for k in range(N_SLAB):
            gw = gyb[:, (k // SLABS_PER_WIN) * LANES:(k // SLABS_PER_WIN + 1) * LANES]
            gr = _dot_nt(gw, cwre_ref[k])
            gi_ = -_dot_nt(gw, cwim_ref[k])
            for j in range(SSM_SUB):
                rre[k, j * SSM_PITCH:j * SSM_PITCH + SSM_TC, :] = gr[j * SSM_TC:(j + 1) * SSM_TC, :]
                rim[k, j * SSM_PITCH:j * SSM_PITCH + SSM_TC, :] = gi_[j * SSM_TC:(j + 1) * SSM_TC, :]
        zero = jnp.zeros((SSM_SUB, LANES), F32)
        for k0 in range(0, N_SLAB, SCAN_GROUP):
            grp = range(k0, k0 + SCAN_GROUP)
            _scan(sre, sim, are_ref, aim_ref, k0, [(sdre_ref[k], sdim_ref[k]) for k in grp],
                  reverse=False, store=True)
            ends = _scan(rre, rim, are_ref, aim_ref, k0, [(zero, zero)] * SCAN_GROUP, reverse=True, store=False,
                         sign=-1.0)
            for kk, k in enumerate(grp):
                ends_re[k] = ends[kk][0]
                ends_im[k] = ends[kk][1]
            for k in grp:
                _ssm_seeds(ends_re, ends_im, a64re_ref, a64im_ref, carry_re, carry_im, seed_re, seed_im, k,
                           reverse=True, sign=-1.0)
            das = _scan_rev_grad(sre, sim, rre, rim, are_ref, aim_ref, k0,
                                 [(seed_re[k], seed_im[k]) for k in grp], sdre_ref, sdim_ref)
            for kk, k in enumerate(grp):
                dare_ref[k] += das[kk][0]
                daim_ref[k] += das[kk][1]
        for w in range(nwin):
            cols = slice(w * LANES, (w + 1) * LANES)
            uw = u[:, cols]
            gw = gyb[:, cols]
            acc = gyv[:, cols] * d_ref[:, cols]
            for kk in range(SLABS_PER_WIN):
                k = w * SLABS_PER_WIN + kk
                lr = _slab_rows(rre, k).astype(BF16)
                li = _slab_rows(rim, k).astype(BF16)
                acc += _dot_nt(lr, bwre_ref[k]) + _dot_nt(li, bwim_ref[k])
                dbre_ref[k] += _dot_tn(uw, lr)
                dbim_ref[k] += _dot_tn(uw, li)
                dcre_ref[k] += _dot_tn(_slab_rows(sre, k).astype(BF16), gw)
                dcim_ref[k] -= _dot_tn(_slab_rows(sim, k).astype(BF16), gw)
            du_ref[:, cols] = acc.astype(BF16)
        dd_ref[...] += jnp.sum((gyv * u.astype(F32)).reshape(SSM_TB // SUBLANES, SUBLANES, SSM_WIDTH), axis=0)

    c2, c3 = _ssm_specs_consts()
    rev = lambda b: nb - 1 - b
    seed_spec = pl.BlockSpec((None, N_SLAB, SSM_SUB, LANES), lambda b: (rev(b), 0, 0, 0))
    tile_out = pl.BlockSpec((N_SLAB, SSM_SUB, LANES), lambda b: (0, 0, 0))
    small = pltpu.VMEM((N_SLAB, LANES), F32)
    tile = pltpu.VMEM((N_SLAB, SSM_SUB, LANES), F32)
    return _call(
        body, name="ssm_bwd", grid=(nb,),
        in_specs=[pl.BlockSpec((SSM_TB, SSM_WIDTH), lambda b: (rev(b), ucol)),
                  pl.BlockSpec((SSM_TB, SSM_WIDTH), lambda b: (rev(b), 0)),
                  c2, c2, c2, c2, c3, c3, c3, c3, pl.BlockSpec((1, SSM_WIDTH), lambda b: (0, 0)),
                  seed_spec, seed_spec],
        out_specs=[pl.BlockSpec((SSM_TB, SSM_WIDTH), lambda b: (rev(b), 0)), tile_out, tile_out, c3, c3, c3, c3,
                   pl.BlockSpec((SUBLANES, SSM_WIDTH), lambda b: (0, 0))],
        out_shape=[_sds((L, SSM_WIDTH), BF16), _sds((N_SLAB, SSM_SUB, LANES), F32),
                   _sds((N_SLAB, SSM_SUB, LANES), F32)] + [_sds((N_SLAB, LANES, LANES), F32)] * 4
                  + [_sds((SUBLANES, SSM_WIDTH), F32)],
        scratch=_ssm_scratch() + _ssm_scratch() + [small, small, tile, tile, tile, tile],
        sem=("arbitrary",))(z, gy, pk["a_re"], pk["a_im"], pk["a64_re"], pk["a64_im"],
                            pk["bw_re"].astype(BF16), pk["bw_im"].astype(BF16),
                            pk["cw_re"].astype(BF16), pk["cw_im"].astype(BF16), dskip, sd_re, sd_im)


def _in_proj_bwd(dz, w, x, g, dres):
    L = x.shape[0]
    ns = w.shape[2]
    tn = ns // 2
    nj = ns // tn
    nt = N_CHIPS * nj

    def body(dz_ref, w_ref, x_ref, g_ref, dres_ref, dx_ref, dg_ref, acc, dgacc):
        i, j = pl.program_id(0), pl.program_id(1)

        @pl.when(j == 0)
        def _():
            acc[...] = jnp.zeros_like(acc)

        @pl.when(jnp.logical_and(i == 0, j == 0))
        def _():
            dgacc[...] = jnp.zeros_like(dgacc)

        acc[...] += _dot_nt(dz_ref[...], w_ref[...])

        @pl.when(j == nt - 1)
        def _():
            dx, dg = _rms_bwd(x_ref[...], g_ref[...], acc[...])
            dx_ref[...] = dres_ref[...] + dx
            dgacc[...] += dg

        @pl.when(jnp.logical_and(i == pl.num_programs(0) - 1, j == nt - 1))
        def _():
            dg_ref[...] = jnp.sum(dgacc[...], axis=0, keepdims=True)

    tok = pl.BlockSpec((TM, D_MODEL), lambda i, j: (i, 0))
    vec = pl.BlockSpec((1, D_MODEL), lambda i, j: (0, 0))
    return _call(
        body, name="in_proj_bwd", grid=(L // TM, nt),
        in_specs=[pl.BlockSpec((TM, tn), lambda i, j: (i, j)),
                  pl.BlockSpec((None, D_MODEL, tn), lambda i, j: (j // nj, 0, j % nj)), tok, vec, tok],
        out_specs=[tok, vec],
        out_shape=[_sds((L, D_MODEL), F32), _sds((1, D_MODEL), F32)],
        scratch=[pltpu.VMEM((TM, D_MODEL), F32), pltpu.VMEM((SUBLANES, D_MODEL), F32)],
        sem=("arbitrary", "arbitrary"))(dz, w, x, g, dres)


SSM_PARAMS = ("lambda_re", "lambda_im", "log_dt", "b_re", "b_im", "c_re", "c_im")


def _layer_bwd(dx2, sv, p):
    g = {}
    dx1, dgate, dup, g["g_ffn"] = _ffn_bwd(dx2, sv["x1"], p["g_ffn"], sv["gate"], sv["up"],
                                            p["w_ffn_gate"], p["w_ffn_up"], p["w_ffn_down"])
    g["w_ffn_gate"] = _wgrad_ff_cols(sv["h2"], dgate, "wgrad_ffn_gate")
    g["w_ffn_up"] = _wgrad_ff_cols(sv["h2"], dup, "wgrad_ffn_up")
    g["w_ffn_down"] = _wgrad_ff_rows(sv["act"], dx2, "wgrad_ffn_down")

    (dgates, do0, do1, do2, dl0, dl1, dl2, gy, daout, dsa, dsb) = _mix_bwd(
        dx1, sv["z"], sv["aout"], sv["sa"], sv["sb"], sv["os"], sv["ls"], sv["ypre"],
        p["w_attn_proj"], p["w_glu_a"], p["w_glu_b"], p["w_out"])
    g["w_out"] = _wgrad_full(sv["mix"], dx1, "wgrad_out").reshape(N_CHIPS, D_MODEL // N_CHIPS, D_MODEL)
    g["w_attn_proj"] = _wgrad_cols(sv["a"], daout, "wgrad_attn_proj")
    g["w_glu_a"] = _wgrad_cols(sv["yact"], dsa, "wgrad_glu_a")
    g["w_glu_b"] = _wgrad_cols(sv["yact"], dsb, "wgrad_glu_b")

    du, da_re, da_im, dbw_re, dbw_im, dcw_re, dcw_im, dd = _ssm_bwd(
        sv["z"], gy, sv["pk"], p["d_skip"], sv["sd_re"], sv["sd_im"])
    g["d_skip"] = jnp.sum(dd, axis=0, keepdims=True)
    _, pull = jax.vjp(_ssm_pack, *[p[n] for n in SSM_PARAMS])
    zeros = jnp.zeros((N_SLAB, LANES), F32)
    ct = dict(a_re=jnp.sum(da_re, axis=1), a_im=jnp.sum(da_im, axis=1), a64_re=zeros, a64_im=zeros,
              bw_re=dbw_re, bw_im=dbw_im, cw_re=dcw_re, cw_im=dcw_im)
    for n, v in zip(SSM_PARAMS, pull(ct)):
        g[n] = v

    dqs, dks, dvs = [], [], []
    dgq = jnp.zeros((1, HEAD_DIM), F32)
    dgk = jnp.zeros((1, HEAD_DIM), F32)
    for gi, (do, dl) in enumerate(((do0, dl0), (do1, dl1), (do2, dl2))):
        dq, dk, dv, gq8, gk8 = _attn_bwd(sv["z"], sv["os"][gi], sv["ls"][gi], do, dl, p["g_q"], p["g_k"], gi)
        dqs.append(dq)
        dks.append(dk)
        dvs.append(dv)
        dgq = dgq + jnp.sum(gq8, axis=0, keepdims=True)
        dgk = dgk + jnp.sum(gk8, axis=0, keepdims=True)
    g["g_q"], g["g_k"] = dgq, dgk
    dz = jnp.concatenate(dqs + dks + dvs + [du, dgates], axis=1)
    g["w_in"] = _wgrad_cols(sv["h"], dz, "wgrad_in")
    dx, g["g_mix"] = _in_proj_bwd(dz, p["w_in"], sv["x"], p["g_mix"], dx1)
    return dx, g


ANY = pl.BlockSpec(memory_space=pl.ANY)


def _place():
    x, y, c = lax.axis_index("x"), lax.axis_index("y"), lax.axis_index("c")
    others = [(1 - x, y), (x, 1 - y), (1 - x, 1 - y)]
    return x, y, c, others


def _half(ref, hc):
    rows = ref.shape[-2] // 2
    idx = (slice(None),) * (len(ref.shape) - 2) + (pl.ds(hc * rows, rows), slice(None))
    return ref.at[idx]


def _comm_call(body, name, ins, out_shapes, n_remote, n_local=0):
    scratch = [pltpu.SemaphoreType.DMA((n_remote,)), pltpu.SemaphoreType.DMA((n_remote,))]
    if n_local:
        scratch.append(pltpu.SemaphoreType.DMA((n_local,)))
    return pl.pallas_call(
        body, name=name, in_specs=[ANY] * len(ins), out_specs=[ANY] * len(out_shapes), out_shape=out_shapes,
        scratch_shapes=scratch)(*ins)


def _gather_weights(shards):
    n = len(shards)

    def body(*refs):
        ins, outs = refs[:n], refs[n:2 * n]
        send, recv, lsem = refs[2 * n:]
        x, y, c, others = _place()
        me = 2 * x + y

        def remote(src, dst, k, to):
            return pltpu.make_async_remote_copy(src_ref=src, dst_ref=dst, send_sem=send.at[k], recv_sem=recv.at[k],
                                                device_id=to, device_id_type=MESH)

        locals_ = [pltpu.make_async_copy(ins[a], outs[a].at[me], lsem.at[a]) for a in range(n)]
        for cp in locals_:
            cp.start()
        started = []
        for a in range(n):
            for j, (cx, cy) in enumerate(others):
                cp = remote(_half(ins[a], c), _half(outs[a].at[me], c), 6 * a + j, (cx, cy, c))
                cp.start()
                started.append(cp)
        for a in range(n):
            for j, (cx, cy) in enumerate(others):
                landed = _half(outs[a].at[2 * cx + cy], c)
                remote(landed, landed, 6 * a + j, (cx, cy, c)).wait_recv()
                cp = remote(landed, landed, 6 * a + 3 + j, (x, y, 1 - c))
                cp.start()
                started.append(cp)
        for a in range(n):
            for j, (cx, cy) in enumerate(others):
                passed = _half(outs[a].at[2 * cx + cy], 1 - c)
                remote(passed, passed, 6 * a + 3 + j, (x, y, 1 - c)).wait_recv()
        for cp in started:
            cp.wait_send()
        for cp in locals_:
            cp.wait()

    outs = [_sds((N_CHIPS,) + s.shape, s.dtype) for s in shards]
    return _comm_call(body, "gather_weights", shards, outs, 6 * n, n)


def _swap_halves(gs):
    n = len(gs)

    def body(*refs):
        ins, outs = refs[:n], refs[n:2 * n]
        send, recv = refs[2 * n:]
        x, y, c, _ = _place()
        cps = [pltpu.make_async_remote_copy(src_ref=_half(ins[a], 1 - c), dst_ref=outs[a], send_sem=send.at[a],
                                            recv_sem=recv.at[a], device_id=(x, y, 1 - c), device_id_type=MESH)
               for a in range(n)]
        for cp in cps:
            cp.start()
        for cp in cps:
            cp.wait()

    outs = [_sds((g.shape[0], g.shape[1] // 2, g.shape[2]), g.dtype) for g in gs]
    return _comm_call(body, "swap_halves", gs, outs, n)


def _scatter_to_owners(ss):
    n = len(ss)

    def body(*refs):
        ins, outs = refs[:n], refs[n:2 * n]
        send, recv = refs[2 * n:]
        x, y, c, others = _place()
        cps = []
        for a in range(n):
            for j, (cx, cy) in enumerate(others):
                cps.append(pltpu.make_async_remote_copy(
                    src_ref=ins[a].at[2 * cx + cy], dst_ref=outs[a].at[j], send_sem=send.at[3 * a + j],
                    recv_sem=recv.at[3 * a + j], device_id=(cx, cy, c), device_id_type=MESH))
        for cp in cps:
            cp.start()
        for cp in cps:
            cp.wait()

    outs = [_sds((N_CHIPS - 1,) + s.shape[1:], s.dtype) for s in ss]
    return _comm_call(body, "scatter_to_owners", ss, outs, 3 * n)


def _join_halves(fs, depth):
    n = len(fs[0])
    flat = [f for layer in fs for f in layer]

    def body(*refs):
        ins, outs = refs[:depth * n], refs[depth * n:depth * n + n]
        send, recv, lsem = refs[depth * n + n:]
        x, y, c, _ = _place()
        cps, locals_ = [], []
        for l in range(depth):
            for a in range(n):
                k = l * n + a
                dst = _half(outs[a].at[l], c)
                locals_.append(pltpu.make_async_copy(ins[k], dst, lsem.at[k]))
                cps.append(pltpu.make_async_remote_copy(src_ref=ins[k], dst_ref=dst, send_sem=send.at[k],
                                                        recv_sem=recv.at[k], device_id=(x, y, 1 - c),
                                                        device_id_type=MESH))
        for cp in locals_ + cps:
            cp.start()
        for l in range(depth):
            for a in range(n):
                k = l * n + a
                got = _half(outs[a].at[l], 1 - c)
                pltpu.make_async_remote_copy(src_ref=got, dst_ref=got, send_sem=send.at[k], recv_sem=recv.at[k],
                                             device_id=(x, y, 1 - c), device_id_type=MESH).wait_recv()
        for cp in cps:
            cp.wait_send()
        for cp in locals_:
            cp.wait()

    outs = [_sds((depth, 2 * f.shape[0], f.shape[1]), f.dtype) for f in fs[0]]
    return _comm_call(body, "join_halves", flat, outs, depth * n, depth * n)


def _gather_small(v):
    rows, n = v.shape

    def body(v_ref, out_ref, send, recv, lsem):
        x, y, c, others = _place()
        me, sibling = (x, y, c), (x, y, 1 - c)

        def blk(px, py, pc):
            return out_ref.at[pl.ds((4 * px + 2 * py + pc) * rows, rows), :]

        def copy(k, block, to, src=None):
            return pltpu.make_async_remote_copy(src_ref=blk(*block) if src is None else src, dst_ref=blk(*block),
                                                send_sem=send.at[k], recv_sem=recv.at[k], device_id=to,
                                                device_id_type=MESH)

        mine = pltpu.make_async_copy(v_ref, blk(*me), lsem)
        mine.start()
        first = [copy(0, me, sibling, src=v_ref)]
        first += [copy(1 + j, me, (*chip, c), src=v_ref) for j, chip in enumerate(others)]
        for cp in first:
            cp.start()
        passed = [copy(4 + j, (*chip, c), sibling) for j, chip in enumerate(others)]
        for j, chip in enumerate(others):
            copy(1 + j, (*chip, c), me).wait_recv()
            passed[j].start()
        copy(0, sibling, me).wait_recv()
        for j, chip in enumerate(others):
            copy(4 + j, (*chip, 1 - c), me).wait_recv()
        for cp in first + passed:
            cp.wait_send()
        mine.wait()

    return pl.pallas_call(
        body, name="gather_small", out_shape=_sds((8 * rows, n), v.dtype),
        in_specs=[pl.BlockSpec(memory_space=pltpu.VMEM)], out_specs=pl.BlockSpec(memory_space=pltpu.VMEM),
        scratch_shapes=[pltpu.SemaphoreType.DMA((7,)), pltpu.SemaphoreType.DMA((7,)), pltpu.SemaphoreType.DMA],
        compiler_params=pltpu.CompilerParams(vmem_limit_bytes=VMEM_LIMIT))(v)


def _add_half(g, p, c):
    _, R, C = g.shape
    half = R // 2

    def body(c_ref, g_ref, p_ref, o_ref):
        o_ref[...] = g_ref[...] + p_ref[...]

    blk = (None, half, C)
    return _call(body, name="add_half", grid=(N_CHIPS,), prefetch=1,
                 in_specs=[pl.BlockSpec(blk, lambda s, c_ref: (s, c_ref[0], 0)),
                           pl.BlockSpec(blk, lambda s, c_ref: (s, 0, 0))],
                 out_specs=pl.BlockSpec(blk, lambda s, c_ref: (s, 0, 0)),
                 out_shape=_sds((N_CHIPS, half, C), F32), sem=("arbitrary",))(c, g, p)


def _sum_owner(s, q, me):
    _, half, C = s.shape
    tr = half // 2

    def body(me_ref, s_ref, q0, q1, q2, o_ref):
        o_ref[...] = ((s_ref[...] + q0[...]) + q1[...]) + q2[...]

    blk = (None, tr, C)
    qspec = lambda j: pl.BlockSpec(blk, lambda i, me_ref: (j, i, 0))
    return _call(body, name="sum_owner", grid=(half // tr,), prefetch=1,
                 in_specs=[pl.BlockSpec(blk, lambda i, me_ref: (me_ref[0], i, 0)), qspec(0), qspec(1), qspec(2)],
                 out_specs=pl.BlockSpec((tr, C), lambda i, me_ref: (i, 0)),
                 out_shape=_sds((half, C), F32), sem=("arbitrary",))(me, s, q, q, q)


def _adamw_math(w, g, m, v):
    m = ADAM_B1 * m + (1.0 - ADAM_B1) * g
    v = ADAM_B2 * v + (1.0 - ADAM_B2) * (g * g)
    m_hat = m / (1.0 - ADAM_B1 ** ADAM_STEP)
    v_hat = v / (1.0 - ADAM_B2 ** ADAM_STEP)
    delta = -ADAM_LR * (m_hat / (jnp.sqrt(v_hat) + ADAM_EPS) + ADAM_WD * w)
    return delta, m, v


def _adamw(w, g, m, v):
    rows, C = w.shape
    tr = next(t for t in (256, 128, 64) if rows % t == 0)

    def body(w_ref, g_ref, m_ref, v_ref, d_ref, nm_ref, nv_ref):
        d, nm, nv = _adamw_math(w_ref[...], g_ref[...], m_ref[...], v_ref[...])
        d_ref[...] = d
        nm_ref[...] = nm
        nv_ref[...] = nv

    spec = pl.BlockSpec((tr, C), lambda i: (i, 0))
    return _call(body, name="adamw", grid=(rows // tr,), in_specs=[spec] * 4, out_specs=[spec] * 3,
                 out_shape=[_sds((rows, C), F32)] * 3, sem=("parallel",))(w, g, m, v)


def _small_update(gathered, w, m, v):
    _, rows, n = gathered.shape
    tr = rows // 7

    def body(ga_ref, w_ref, m_ref, v_ref, g_ref, d_ref, nm_ref, nv_ref):
        g = ga_ref[0]
        for k in range(1, 8):
            g = g + ga_ref[k]
        d, nm, nv = _adamw_math(w_ref[...], g, m_ref[...], v_ref[...])
        g_ref[...] = g
        d_ref[...] = d
        nm_ref[...] = nm
        nv_ref[...] = nv

    spec = pl.BlockSpec((tr, n), lambda i: (i, 0))
    return _call(body, name="small_update", grid=(rows // tr,),
                 in_specs=[pl.BlockSpec((8, tr, n), lambda i: (0, i, 0)), spec, spec, spec], out_specs=[spec] * 4,
                 out_shape=[_sds((rows, n), F32)] * 4, sem=("parallel",))(gathered, w, m, v)


WEIGHTS = ("g_mix", "w_in", "g_q", "g_k", "w_attn_proj", "lambda_re", "lambda_im", "log_dt", "b_re", "b_im",
           "c_re", "c_im", "d_skip", "w_glu_a", "w_glu_b", "w_out", "g_ffn", "w_ffn_gate", "w_ffn_up", "w_ffn_down")
BIG = ("w_in", "w_attn_proj", "w_glu_a", "w_glu_b", "w_out", "w_ffn_gate", "w_ffn_up", "w_ffn_down")
SMALL = tuple(n for n in WEIGHTS if n not in BIG)
ROW_VECTORS = ("g_mix", "g_q", "g_k", "d_skip", "g_ffn")
PACK_QUANTUM = LANES * SUBLANES * 7


def _pack_small(parts, extra):
    flat = jnp.concatenate([parts[n].reshape(-1).astype(F32) for n in SMALL] + [extra.reshape(-1)])
    pad = -flat.shape[0] % PACK_QUANTUM
    return jnp.pad(flat, (0, pad)).reshape(-1, LANES)


def _unpack_small(packed, like):
    flat = packed.reshape(-1)
    out, at = {}, 0
    for n in SMALL:
        size = math.prod(like[n].shape)
        out[n] = flat[at:at + size].reshape(like[n].shape)
        at += size
    return out, flat[at]


def kernel(x, g_mix, w_in, g_q, g_k, w_attn_proj, lambda_re, lambda_im, log_dt, b_re, b_im, c_re, c_im, d_skip, w_glu_a, w_glu_b, w_out, g_ffn, w_ffn_gate, w_ffn_up, w_ffn_down, loss_target, m_g_mix, m_w_in, m_g_q, m_g_k, m_w_attn_proj, m_lambda_re, m_lambda_im, m_log_dt, m_b_re, m_b_im, m_c_re, m_c_im, m_d_skip, m_w_glu_a, m_w_glu_b, m_w_out, m_g_ffn, m_w_ffn_gate, m_w_ffn_up, m_w_ffn_down, v_g_mix, v_w_in, v_g_q, v_g_k, v_w_attn_proj, v_lambda_re, v_lambda_im, v_log_dt, v_b_re, v_b_im, v_c_re, v_c_im, v_d_skip, v_w_glu_a, v_w_glu_b, v_w_out, v_g_ffn, v_w_ffn_gate, v_w_ffn_up, v_w_ffn_down):
    given = dict(locals())
    W = {n: given[n] for n in WEIGHTS}
    M = {n: given["m_" + n] for n in WEIGHTS}
    V = {n: given["v_" + n] for n in WEIGHTS}
    depth = g_mix.shape[0]
    xl = x.reshape(x.shape[-2:])
    target = loss_target.reshape(loss_target.shape[-2:])
    c_idx = lax.axis_index("c").astype(jnp.int32).reshape(1)
    chip_idx = (2 * lax.axis_index("x") + lax.axis_index("y")).astype(jnp.int32).reshape(1)

    params = []
    for l in range(depth):
        full = _gather_weights([W[n][l].astype(BF16) for n in BIG])
        p = dict(zip(BIG, full))
        for n in SMALL:
            p[n] = W[n][l][None] if n in ROW_VECTORS else W[n][l]
        params.append(p)

    saved, h = [], xl
    for l in range(depth):
        h, sv = _layer_fwd(h, params[l])
        saved.append(sv)
    dx, loss_part = _loss_head(h, target)

    owned = [None] * depth
    small_grads = [None] * depth
    for l in reversed(range(depth)):
        dx, g = _layer_bwd(dx, saved[l], params[l])
        parts = [g[n] for n in BIG]
        sib = _swap_halves(parts)
        chip = [_add_half(a, b, c_idx) for a, b in zip(parts, sib)]
        recv = _scatter_to_owners(chip)
        owned[l] = [_sum_owner(s, q, chip_idx) for s, q in zip(chip, recv)]
        small_grads[l] = g
    reduced = dict(zip(BIG, _join_halves(owned, depth)))

    grads, delta, new_m, new_v = {}, {}, {}, {}
    for n in BIG:
        shape = W[n].shape
        two_d = (shape[0] * shape[1], shape[2])
        d, nm, nv = _adamw(W[n].reshape(two_d), reduced[n].reshape(two_d), M[n].reshape(two_d), V[n].reshape(two_d))
        grads[n], delta[n], new_m[n], new_v[n] = reduced[n], d.reshape(shape), nm.reshape(shape), nv.reshape(shape)

    stacked = {n: jnp.stack([small_grads[l][n] for l in range(depth)]) for n in SMALL}
    zero = jnp.zeros((1,), F32)
    packed = _pack_small(stacked, loss_part)
    gathered = _gather_small(packed).reshape(8, *packed.shape)
    gs, ds, nms, nvs = _small_update(gathered, _pack_small(W, zero), _pack_small(M, zero), _pack_small(V, zero))
    sg, loss = _unpack_small(gs, W)
    sd, _ = _unpack_small(ds, W)
    sm, _ = _unpack_small(nms, W)
    sv_, _ = _unpack_small(nvs, W)
    for n in SMALL:
        grads[n], delta[n], new_m[n], new_v[n] = sg[n], sd[n], sm[n], sv_[n]

    return (loss, dx.reshape(x.shape), *[grads[n] for n in WEIGHTS], *[delta[n] for n in WEIGHTS],
            *[new_m[n] for n in WEIGHTS], *[new_v[n] for n in WEIGHTS])
```

```python
import functools
import math

import jax
import jax.numpy as jnp
from jax import lax
from jax.experimental import pallas as pl
from jax.experimental.pallas import tpu as pltpu

F32 = jnp.float32
BF16 = jnp.bfloat16

D_MODEL = 1024
DEPTH = 4
HEAD_DIM = 64
N_HEADS = 8
ATTN_WIDTH = N_HEADS * HEAD_DIM
ATTN_PATTERN = ((128, 1), (512, 4), (2048, 16))
N_GROUPS = len(ATTN_PATTERN)
BLK = 128
SSM_WIDTH = 512
SSM_GROUP = 16
SSM_GROUPS = 32
SSM_STATE = 64
D_FF = 2816
IN_COLS = 7168
EPS = 1e-6
ADAM_LR, ADAM_B1, ADAM_B2, ADAM_EPS, ADAM_WD, ADAM_STEP = 0.001, 0.9, 0.999, 1e-08, 0.01, 10

N_CHIPS = 4
MESH = pl.DeviceIdType.MESH

LANES = 128
SUBLANES = 8
VMEM_LIMIT = 56 * 1024 * 1024

TM = 512
TM_MIX = 256

SSM_TB = 512
SSM_TC = 64
SSM_SUB = SUBLANES
SSM_PITCH = 72
N_SLAB = SSM_GROUPS * SSM_STATE // LANES
SLABS_PER_WIN = 4
SCAN_GROUP = 4


def _params(sem=None, collective=False):
    return pltpu.CompilerParams(dimension_semantics=sem, vmem_limit_bytes=VMEM_LIMIT)


def _call(body, *, name, grid, in_specs, out_specs, out_shape, scratch=(), sem=None, aliases=None,
          prefetch=0):
    kw = {}
    if aliases:
        kw["input_output_aliases"] = aliases
    if prefetch:
        gs = pltpu.PrefetchScalarGridSpec(num_scalar_prefetch=prefetch, grid=grid, in_specs=in_specs,
                                          out_specs=out_specs, scratch_shapes=list(scratch))
        return pl.pallas_call(body, name=name, grid_spec=gs, out_shape=out_shape,
                              compiler_params=_params(sem), **kw)
    return pl.pallas_call(body, name=name, grid=grid, in_specs=in_specs, out_specs=out_specs,
                          out_shape=out_shape, scratch_shapes=list(scratch),
                          compiler_params=_params(sem), **kw)


def _sds(shape, dtype):
    return jax.ShapeDtypeStruct(shape, dtype)


def _sigmoid(v):
    return 1.0 / (1.0 + jnp.exp(-v))


def _dot(a, b):
    return jnp.dot(a, b, preferred_element_type=F32)


def _dot_nt(a, b):
    return lax.dot_general(a, b, (((1,), (1,)), ((), ())), preferred_element_type=F32)


def _dot_tn(a, b):
    return lax.dot_general(a, b, (((0,), (0,)), ((), ())), preferred_element_type=F32)


def _in_proj_fwd(x, g, w):
    L = x.shape[0]
    ns = w.shape[2]
    tn = ns // 2
    nj = ns // tn

    def body(x_ref, g_ref, w_ref, z_ref, h_ref):
        @pl.when(pl.program_id(1) == 0)
        def _():
            xv = x_ref[...]
            r = lax.rsqrt(jnp.mean(xv * xv, axis=-1, keepdims=True) + EPS)
            h_ref[...] = (xv * r * g_ref[...]).astype(BF16)
        z_ref[...] = _dot(h_ref[...], w_ref[...]).astype(BF16)

    return _call(
        body, name="in_proj_fwd", grid=(L // TM, N_CHIPS * nj),
        in_specs=[pl.BlockSpec((TM, D_MODEL), lambda i, j: (i, 0)),
                  pl.BlockSpec((1, D_MODEL), lambda i, j: (0, 0)),
                  pl.BlockSpec((None, D_MODEL, tn), lambda i, j: (j // nj, 0, j % nj))],
        out_specs=[pl.BlockSpec((TM, tn), lambda i, j: (i, j)),
                   pl.BlockSpec((TM, D_MODEL), lambda i, j: (i, 0))],
        out_shape=[_sds((L, N_CHIPS * ns), BF16), _sds((L, D_MODEL), BF16)],
        sem=("parallel", "arbitrary"))(x, g, w)


def _head_norm(t, gain):
    r = lax.rsqrt(jnp.mean(t * t, axis=-1, keepdims=True) + EPS)
    th = t * r
    return th, th * gain, r


def _attn_masks():
    qi = lax.broadcasted_iota(jnp.int32, (BLK, BLK), 0)
    ki = lax.broadcasted_iota(jnp.int32, (BLK, BLK), 1)
    return qi >= ki, ki >= qi


def _attn_fwd(z, gq, gk, gi):
    L = z.shape[0]
    _, d = ATTN_PATTERN[gi]
    M = L // d
    nb = M // BLK
    zv = z.reshape(M, d * IN_COLS)
    cpb = IN_COLS // ATTN_WIDTH
    scale = HEAD_DIM ** -0.5

    def body(q_ref, kc_ref, kp_ref, vc_ref, vp_ref, gq_ref, gk_ref, o_ref, l_ref):
        n = pl.program_id(1)
        mask_c, mask_p0 = _attn_masks()
        mask_p = jnp.logical_and(mask_p0, n > 0)
        gqv = gq_ref[...]
        gkv = gk_ref[...]
        for h in range(N_HEADS):
            sl = slice(h * HEAD_DIM, (h + 1) * HEAD_DIM)
            _, qn, _ = _head_norm(q_ref[:, sl].astype(F32), gqv)
            _, kcn, _ = _head_norm(kc_ref[:, sl].astype(F32), gkv)
            _, kpn, _ = _head_norm(kp_ref[:, sl].astype(F32), gkv)
            qb = qn.astype(BF16)
            s_c = jnp.where(mask_c, _dot_nt(qb, kcn.astype(BF16)) * scale, -jnp.inf)
            s_p = jnp.where(mask_p, _dot_nt(qb, kpn.astype(BF16)) * scale, -jnp.inf)
            m = jnp.maximum(jnp.max(s_c, axis=-1, keepdims=True), jnp.max(s_p, axis=-1, keepdims=True))
            p_c = jnp.exp(s_c - m)
            p_p = jnp.exp(s_p - m)
            den = jnp.sum(p_c, axis=-1, keepdims=True) + jnp.sum(p_p, axis=-1, keepdims=True)
            acc = _dot(p_c.astype(BF16), vc_ref[:, sl]) + _dot(p_p.astype(BF16), vp_ref[:, sl])
            o_ref[:, sl] = acc / den
            l_ref[:, sl] = jnp.broadcast_to(m + jnp.log(den), (BLK, HEAD_DIM))

    def col(kind):
        return lambda r, n: (n, r * cpb + 3 * kind + gi)

    def colp(kind):
        return lambda r, n: (jnp.maximum(n - 1, 0), r * cpb + 3 * kind + gi)

    blk = (BLK, ATTN_WIDTH)
    o, l = _call(
        body, name=f"attn_fwd_g{gi}", grid=(d, nb),
        in_specs=[pl.BlockSpec(blk, col(0)), pl.BlockSpec(blk, col(1)), pl.BlockSpec(blk, colp(1)),
                  pl.BlockSpec(blk, col(2)), pl.BlockSpec(blk, colp(2)),
                  pl.BlockSpec((1, HEAD_DIM), lambda r, n: (0, 0)),
                  pl.BlockSpec((1, HEAD_DIM), lambda r, n: (0, 0))],
        out_specs=[pl.BlockSpec(blk, lambda r, n: (n, r)), pl.BlockSpec(blk, lambda r, n: (n, r))],
        out_shape=[_sds((M, d * ATTN_WIDTH), F32), _sds((M, d * ATTN_WIDTH), F32)],
        sem=("parallel", "parallel"))(zv, zv, zv, zv, zv, gq, gk)
    return o.reshape(L, ATTN_WIDTH), l.reshape(L, ATTN_WIDTH)


def _gelu(v):
    c = math.sqrt(2.0 / math.pi)
    return 0.5 * v * (1.0 + jnp.tanh(c * (v + 0.044715 * v * v * v)))


def _gelu_grad(v):
    c = math.sqrt(2.0 / math.pi)
    t = jnp.tanh(c * (v + 0.044715 * v * v * v))
    return 0.5 * (1.0 + t) + 0.5 * v * (1.0 - t * t) * c * (1.0 + 3.0 * 0.044715 * v * v)


def _ssm_fill(u, bwre_ref, bwim_ref, sre, sim):
    for k in range(N_SLAB):
        w = k // SLABS_PER_WIN
        uw = u[:, w * LANES:(w + 1) * LANES]
        br = _dot(uw, bwre_ref[k])
        bi = _dot(uw, bwim_ref[k])
        for j in range(SSM_SUB):
            sre[k, j * SSM_PITCH:j * SSM_PITCH + SSM_TC, :] = br[j * SSM_TC:(j + 1) * SSM_TC, :]
            sim[k, j * SSM_PITCH:j * SSM_PITCH + SSM_TC, :] = bi[j * SSM_TC:(j + 1) * SSM_TC, :]


def _rows(i):
    return pl.ds(i, SSM_SUB, stride=SSM_PITCH)


def _slab_rows(ref, k):
    return jnp.concatenate([ref[k, j * SSM_PITCH:j * SSM_PITCH + SSM_TC, :] for j in range(SSM_SUB)], axis=0)


def _bcast(ref, k):
    return jnp.broadcast_to(ref[pl.ds(k, 1), :], (SSM_SUB, LANES))


def _scan(sre, sim, are_ref, aim_ref, k0, init, *, reverse, store, sign=1.0):
    ar = [_bcast(are_ref, k0 + kk) for kk in range(SCAN_GROUP)]
    ai = [sign * _bcast(aim_ref, k0 + kk) for kk in range(SCAN_GROUP)]

    def step(t, carry):
        i = SSM_TC - 1 - t if reverse else t
        out = []
        for kk in range(SCAN_GROUP):
            k = k0 + kk
            xr, xi = carry[2 * kk], carry[2 * kk + 1]
            nr = ar[kk] * xr - ai[kk] * xi + sre[k, _rows(i), :]
            ni = ar[kk] * xi + ai[kk] * xr + sim[k, _rows(i), :]
            if store:
                sre[k, _rows(i), :] = nr
                sim[k, _rows(i), :] = ni
            out += [nr, ni]
        return tuple(out)

    flat = []
    for re, im in init:
        flat += [re, im]
    res = lax.fori_loop(0, SSM_TC, step, tuple(flat))
    return [(res[2 * kk], res[2 * kk + 1]) for kk in range(SCAN_GROUP)]


def _ssm_seeds(ends_re, ends_im, a64re_ref, a64im_ref, carry_re, carry_im, seed_re, seed_im, k,
               *, reverse, sign=1.0):
    ar = a64re_ref[pl.ds(k, 1), :]
    ai = sign * a64im_ref[pl.ds(k, 1), :]
    cr = carry_re[pl.ds(k, 1), :]
    ci = carry_im[pl.ds(k, 1), :]
    order = range(SSM_SUB - 1, -1, -1) if reverse else range(SSM_SUB)
    for j in order:
        seed_re[k, pl.ds(j, 1), :] = cr
        seed_im[k, pl.ds(j, 1), :] = ci
        er = ends_re[k, pl.ds(j, 1), :]
        ei = ends_im[k, pl.ds(j, 1), :]
        cr, ci = ar * cr - ai * ci + er, ar * ci + ai * cr + ei
    carry_re[pl.ds(k, 1), :] = cr
    carry_im[pl.ds(k, 1), :] = ci


def _ssm_specs_consts():
    c2 = pl.BlockSpec((N_SLAB, LANES), lambda b: (0, 0))
    c3 = pl.BlockSpec((N_SLAB, LANES, LANES), lambda b: (0, 0, 0))
    return c2, c3


def _ssm_scratch():
    rows = SSM_SUB * SSM_PITCH
    return [pltpu.VMEM((N_SLAB, rows, LANES), F32), pltpu.VMEM((N_SLAB, rows, LANES), F32)]


def _ssm_fwd(z, pk, dskip):
    L = z.shape[0]
    nb = L // SSM_TB
    ucol = (3 * N_GROUPS * ATTN_WIDTH) // SSM_WIDTH

    def body(u_ref, are_ref, aim_ref, a64re_ref, a64im_ref, bwre_ref, bwim_ref, cwre_ref, cwim_ref, d_ref,
             ypre_ref, yact_ref, sdre_ref, sdim_ref, sre, sim, carry_re, carry_im, ends_re, ends_im,
             seed_re, seed_im):
        @pl.when(pl.program_id(0) == 0)
        def _():
            carry_re[...] = jnp.zeros_like(carry_re)
            carry_im[...] = jnp.zeros_like(carry_im)

        u = u_ref[...]
        _ssm_fill(u, bwre_ref, bwim_ref, sre, sim)
        zero = jnp.zeros((SSM_SUB, LANES), F32)
        for k0 in range(0, N_SLAB, SCAN_GROUP):
            ends = _scan(sre, sim, are_ref, aim_ref, k0, [(zero, zero)] * SCAN_GROUP, reverse=False, store=False)
            for kk in range(SCAN_GROUP):
                ends_re[k0 + kk] = ends[kk][0]
                ends_im[k0 + kk] = ends[kk][1]
            for kk in range(SCAN_GROUP):
                _ssm_seeds(ends_re, ends_im, a64re_ref, a64im_ref, carry_re, carry_im, seed_re, seed_im,
                           k0 + kk, reverse=False)
            init = [(seed_re[k0 + kk], seed_im[k0 + kk]) for kk in range(SCAN_GROUP)]
            _scan(sre, sim, are_ref, aim_ref, k0, init, reverse=False, store=True)
        sdre_ref[...] = seed_re[...]
        sdim_ref[...] = seed_im[...]
        for w in range(N_SLAB // SLABS_PER_WIN):
            acc = jnp.zeros((SSM_TB, LANES), F32)
            for kk in range(SLABS_PER_WIN):
                k = w * SLABS_PER_WIN + kk
                acc += _dot(_slab_rows(sre, k).astype(BF16), cwre_ref[k])
                acc -= _dot(_slab_rows(sim, k).astype(BF16), cwim_ref[k])
            cols = slice(w * LANES, (w + 1) * LANES)
            ypre = acc + d_ref[:, cols] * u[:, cols].astype(F32)
            ypre_ref[:, cols] = ypre
            yact_ref[:, cols] = _gelu(ypre).astype(BF16)

    c2, c3 = _ssm_specs_consts()
    seed_spec = pl.BlockSpec((None, N_SLAB, SSM_SUB, LANES), lambda b: (b, 0, 0, 0))
    small = pltpu.VMEM((N_SLAB, LANES), F32)
    tile = pltpu.VMEM((N_SLAB, SSM_SUB, LANES), F32)
    return _call(
        body, name="ssm_fwd", grid=(nb,),
        in_specs=[pl.BlockSpec((SSM_TB, SSM_WIDTH), lambda b: (b, ucol)), c2, c2, c2, c2, c3, c3, c3, c3,
                  pl.BlockSpec((1, SSM_WIDTH), lambda b: (0, 0))],
        out_specs=[pl.BlockSpec((SSM_TB, SSM_WIDTH), lambda b: (b, 0)),
                   pl.BlockSpec((SSM_TB, SSM_WIDTH), lambda b: (b, 0)), seed_spec, seed_spec],
        out_shape=[_sds((L, SSM_WIDTH), F32), _sds((L, SSM_WIDTH), BF16),
                   _sds((nb, N_SLAB, SSM_SUB, LANES), F32), _sds((nb, N_SLAB, SSM_SUB, LANES), F32)],
        scratch=_ssm_scratch() + [small, small, tile, tile, tile, tile],
        sem=("arbitrary",))(z, pk["a_re"], pk["a_im"], pk["a64_re"], pk["a64_im"],
                            pk["bw_re"].astype(BF16), pk["bw_im"].astype(BF16),
                            pk["cw_re"].astype(BF16), pk["cw_im"].astype(BF16), dskip)


def _combine_weights(l0, l1, l2):
    m = jnp.maximum(jnp.maximum(l0, l1), l2)
    e0, e1, e2 = jnp.exp(l0 - m), jnp.exp(l1 - m), jnp.exp(l2 - m)
    inv = 1.0 / (e0 + e1 + e2)
    return e0 * inv, e1 * inv, e2 * inv


def _mix_fwd(x, z, os_, ls_, yact, w_ap, w_ga, w_gb, w_out):
    L = x.shape[0]
    cs = D_MODEL // N_CHIPS
    ga_col = (3 * N_GROUPS * ATTN_WIDTH + SSM_WIDTH) // D_MODEL

    def body(x_ref, ga_ref, gs_ref, o0, o1, o2, l0, l1, l2, y_ref, wap_ref, wga_ref, wgb_ref, wout_ref,
             x1_ref, a_ref, aout_ref, sa_ref, sb_ref, mix_ref):
        w0, w1, w2 = _combine_weights(l0[...], l1[...], l2[...])
        a = (w0 * o0[...] + w1 * o1[...] + w2 * o2[...]).astype(BF16)
        a_ref[...] = a
        y = y_ref[...]
        for s in range(N_CHIPS):
            cols = slice(s * cs, (s + 1) * cs)
            aout_ref[:, cols] = _dot(a, wap_ref[s]).astype(BF16)
            sa_ref[:, cols] = _dot(y, wga_ref[s]).astype(BF16)
            sb_ref[:, cols] = _dot(y, wgb_ref[s]).astype(BF16)
        s_out = sa_ref[...].astype(F32) * _sigmoid(sb_ref[...].astype(F32))
        mix = (_sigmoid(ga_ref[...].astype(F32)) * aout_ref[...].astype(F32)
               + _sigmoid(gs_ref[...].astype(F32)) * s_out).astype(BF16)
        mix_ref[...] = mix
        x1_ref[...] = x_ref[...] + _dot(mix, wout_ref[...])

    tok = lambda w: pl.BlockSpec((TM_MIX, w), lambda i: (i, 0))
    wsm = pl.BlockSpec((N_CHIPS, ATTN_WIDTH, cs), lambda i: (0, 0, 0))
    return _call(
        body, name="mix_fwd", grid=(L // TM_MIX,),
        in_specs=[tok(D_MODEL), pl.BlockSpec((TM_MIX, D_MODEL), lambda i: (i, ga_col)),
                  pl.BlockSpec((TM_MIX, D_MODEL), lambda i: (i, ga_col + 1))]
                 + [tok(ATTN_WIDTH)] * 7 + [wsm, wsm, wsm, pl.BlockSpec((D_MODEL, D_MODEL), lambda i: (0, 0))],
        out_specs=[tok(D_MODEL), tok(ATTN_WIDTH), tok(D_MODEL), tok(D_MODEL), tok(D_MODEL), tok(D_MODEL)],
        out_shape=[_sds((L, D_MODEL), F32), _sds((L, ATTN_WIDTH), BF16)] + [_sds((L, D_MODEL), BF16)] * 4,
        sem=("parallel",))(x, z, z, *os_, *ls_, yact, w_ap, w_ga, w_gb, w_out.reshape(D_MODEL, D_MODEL))


def _ffn_fwd(x1, g, w_g, w_u, w_d):
    L = x1.shape[0]
    fs = D_FF // N_CHIPS

    def body(x_ref, g_ref, wg_ref, wu_ref, wd_ref, x2_ref, h_ref, gate_ref, up_ref, act_ref, acc):
        s = pl.program_id(1)

        @pl.when(s == 0)
        def _():
            xv = x_ref[...]
            r = lax.rsqrt(jnp.mean(xv * xv, axis=-1, keepdims=True) + EPS)
            h_ref[...] = (xv * r * g_ref[...]).astype(BF16)
            acc[...] = jnp.zeros_like(acc)

        h = h_ref[...]
        gate = _dot(h, wg_ref[...])
        up = _dot(h, wu_ref[...])
        act = (gate * _sigmoid(gate) * up).astype(BF16)
        gate_ref[...] = gate.astype(BF16)
        up_ref[...] = up.astype(BF16)
        act_ref[...] = act
        acc[...] += _dot(act, wd_ref[...])

        @pl.when(s == N_CHIPS - 1)
        def _():
            x2_ref[...] = x_ref[...] + acc[...]

    tok = pl.BlockSpec((TM, D_MODEL), lambda i, s: (i, 0))
    ffs = pl.BlockSpec((None, TM, fs), lambda i, s: (s, i, 0))
    return _call(
        body, name="ffn_fwd", grid=(L // TM, N_CHIPS),
        in_specs=[tok, pl.BlockSpec((1, D_MODEL), lambda i, s: (0, 0)),
                  pl.BlockSpec((None, D_MODEL, fs), lambda i, s: (s, 0, 0)),
                  pl.BlockSpec((None, D_MODEL, fs), lambda i, s: (s, 0, 0)),
                  pl.BlockSpec((None, fs, D_MODEL), lambda i, s: (s, 0, 0))],
        out_specs=[tok, tok, ffs, ffs, ffs],
        out_shape=[_sds((L, D_MODEL), F32), _sds((L, D_MODEL), BF16)] + [_sds((N_CHIPS, L, fs), BF16)] * 3,
        scratch=[pltpu.VMEM((TM, D_MODEL), F32)],
        sem=("parallel", "arbitrary"))(x1, g, w_g, w_u, w_d)


def _loss_head(xl, target):
    L = xl.shape[0]

    def body(x_ref, t_ref, dx_ref, loss_ref, acc):
        i = pl.program_id(0)

        @pl.when(i == 0)
        def _():
            acc[...] = jnp.zeros_like(acc)

        e = x_ref[...] - t_ref[...]
        dx_ref[...] = e * (1.0 / D_MODEL)
        acc[...] += jnp.sum((e * e).reshape(TM // SUBLANES, SUBLANES, D_MODEL), axis=0)

        @pl.when(i == pl.num_programs(0) - 1)
        def _():
            loss_ref[...] = (0.5 / D_MODEL) * jnp.sum(acc[...]).reshape(1, 1)

    tok = pl.BlockSpec((TM, D_MODEL), lambda i: (i, 0))
    return _call(
        body, name="loss_head", grid=(L // TM,), in_specs=[tok, tok],
        out_specs=[tok, pl.BlockSpec((1, 1), lambda i: (0, 0))],
        out_shape=[_sds((L, D_MODEL), F32), _sds((1, 1), F32)],
        scratch=[pltpu.VMEM((SUBLANES, D_MODEL), F32)], sem=("arbitrary",))(xl, target)


def _ssm_pack(lam_re, lam_im, log_dt, b_re, b_im, c_re, c_im):
    dt = jnp.exp(log_dt)[:, None]
    mag = jnp.exp(lam_re * dt)
    ang = lam_im * dt
    ar = mag * jnp.cos(ang)
    ai = mag * jnp.sin(ang)
    nr = ar - 1.0
    ni = ai
    den = lam_re * lam_re + lam_im * lam_im
    cr = ((nr * lam_re + ni * lam_im) / den)[..., None]
    ci = ((ni * lam_re - nr * lam_im) / den)[..., None]
    bbr = cr * b_re - ci * b_im
    bbi = cr * b_im + ci * b_re
    eye = jnp.eye(SSM_GROUPS, dtype=F32)
    n_state = N_SLAB * LANES

    def b_windows(bb):
        full = jnp.einsum('gpc,gh->gchp', bb, eye).reshape(SSM_WIDTH, n_state)
        return jnp.stack([full[(k // SLABS_PER_WIN) * LANES:(k // SLABS_PER_WIN + 1) * LANES,
                               k * LANES:(k + 1) * LANES] for k in range(N_SLAB)])

    def c_windows(cc):
        full = jnp.einsum('gcp,gh->hpgc', cc, eye).reshape(n_state, SSM_WIDTH)
        return jnp.stack([full[k * LANES:(k + 1) * LANES,
                               (k // SLABS_PER_WIN) * LANES:(k // SLABS_PER_WIN + 1) * LANES]
                          for k in range(N_SLAB)])

    pr, pi = ar, ai
    for _ in range(int(math.log2(SSM_TC))):
        pr, pi = pr * pr - pi * pi, 2.0 * pr * pi
    return dict(a_re=ar.reshape(N_SLAB, LANES), a_im=ai.reshape(N_SLAB, LANES),
                a64_re=pr.reshape(N_SLAB, LANES), a64_im=pi.reshape(N_SLAB, LANES),
                bw_re=b_windows(bbr), bw_im=b_windows(bbi), cw_re=c_windows(c_re), cw_im=c_windows(c_im))


def _layer_fwd(x, p):
    z, h = _in_proj_fwd(x, p["g_mix"], p["w_in"])
    os_, ls_ = [], []
    for gi in range(N_GROUPS):
        o, l = _attn_fwd(z, p["g_q"], p["g_k"], gi)
        os_.append(o)
        ls_.append(l)
    pk = _ssm_pack(p["lambda_re"], p["lambda_im"], p["log_dt"], p["b_re"], p["b_im"], p["c_re"], p["c_im"])
    ypre, yact, sd_re, sd_im = _ssm_fwd(z, pk, p["d_skip"])
    x1, a, aout, sa, sb, mix = _mix_fwd(x, z, os_, ls_, yact, p["w_attn_proj"], p["w_glu_a"], p["w_glu_b"],
                                       p["w_out"])
    x2, h2, gate, up, act = _ffn_fwd(x1, p["g_ffn"], p["w_ffn_gate"], p["w_ffn_up"], p["w_ffn_down"])
    saved = dict(x=x, z=z, h=h, os=os_, ls=ls_, pk=pk, ypre=ypre, yact=yact, sd_re=sd_re, sd_im=sd_im,
                 x1=x1, a=a, aout=aout, sa=sa, sb=sb, mix=mix, h2=h2, gate=gate, up=up, act=act)
    return x2, saved


def _rms_bwd(xv, g, dh):
    r = lax.rsqrt(jnp.mean(xv * xv, axis=-1, keepdims=True) + EPS)
    xn = xv * r
    dxn = dh * g
    dx = r * (dxn - xn * jnp.mean(dxn * xn, axis=-1, keepdims=True))
    dg = jnp.sum((dh * xn).reshape(xv.shape[0] // SUBLANES, SUBLANES, xv.shape[1]), axis=0)
    return dx, dg


def _ffn_bwd(dx2, x1, g, gate, up, w_g, w_u, w_d):
    L = x1.shape[0]
    fs = D_FF // N_CHIPS

    def body(dx_ref, x_ref, g_ref, gate_ref, up_ref, wg_ref, wu_ref, wd_ref,
             dx1_ref, dgate_ref, dup_ref, dg_ref, acc, dgacc):
        i, s = pl.program_id(0), pl.program_id(1)

        @pl.when(s == 0)
        def _():
            acc[...] = jnp.zeros_like(acc)

        @pl.when(jnp.logical_and(i == 0, s == 0))
        def _():
            dgacc[...] = jnp.zeros_like(dgacc)

        dact = _dot_nt(dx_ref[...].astype(BF16), wd_ref[...])
        gt = gate_ref[...].astype(F32)
        sg = _sigmoid(gt)
        dgate = (dact * up_ref[...].astype(F32) * (sg * (1.0 + gt * (1.0 - sg)))).astype(BF16)
        dup = (dact * gt * sg).astype(BF16)
        dgate_ref[...] = dgate
        dup_ref[...] = dup
        acc[...] += _dot_nt(dgate, wg_ref[...]) + _dot_nt(dup, wu_ref[...])

        @pl.when(s == N_CHIPS - 1)
        def _():
            dx, dg = _rms_bwd(x_ref[...], g_ref[...], acc[...])
            dx1_ref[...] = dx_ref[...] + dx
            dgacc[...] += dg

        @pl.when(jnp.logical_and(i == pl.num_programs(0) - 1, s == N_CHIPS - 1))
        def _():
            dg_ref[...] = jnp.sum(dgacc[...], axis=0, keepdims=True)

    tok = pl.BlockSpec((TM, D_MODEL), lambda i, s: (i, 0))
    ffs = pl.BlockSpec((None, TM, fs), lambda i, s: (s, i, 0))
    vec = pl.BlockSpec((1, D_MODEL), lambda i, s: (0, 0))
    return _call(
        body, name="ffn_bwd", grid=(L // TM, N_CHIPS),
        in_specs=[tok, tok, vec, ffs, ffs,
                  pl.BlockSpec((None, D_MODEL, fs), lambda i, s: (s, 0, 0)),
                  pl.BlockSpec((None, D_MODEL, fs), lambda i, s: (s, 0, 0)),
                  pl.BlockSpec((None, fs, D_MODEL), lambda i, s: (s, 0, 0))],
        out_specs=[tok, ffs, ffs, vec],
        out_shape=[_sds((L, D_MODEL), F32), _sds((N_CHIPS, L, fs), BF16), _sds((N_CHIPS, L, fs), BF16),
                   _sds((1, D_MODEL), F32)],
        scratch=[pltpu.VMEM((TM, D_MODEL), F32), pltpu.VMEM((SUBLANES, D_MODEL), F32)],
        sem=("arbitrary", "arbitrary"))(dx2, x1, g, gate, up, w_g, w_u, w_d)


def _wgrad(a, b, *, name, grid_kn, a_spec, b_spec, out_shape, out_spec):
    L = a.shape[-2]
    nl = L // TM

    def body(a_ref, b_ref, o_ref):
        @pl.when(pl.program_id(2) == 0)
        def _():
            o_ref[...] = jnp.zeros_like(o_ref)
        o_ref[...] += _dot_tn(a_ref[...].astype(BF16), b_ref[...].astype(BF16))

    return _call(body, name=name, grid=(*grid_kn, nl), in_specs=[a_spec, b_spec], out_specs=out_spec,
                 out_shape=out_shape, sem=("parallel", "parallel", "arbitrary"))(a, b)


def _wgrad_cols(a, b, name):
    K, N = a.shape[1], b.shape[1]
    ns = N // N_CHIPS
    if N * K * 4 <= 4 * 1024 * 1024:
        L = a.shape[0]

        def body(a_ref, b_ref, o_ref):
            @pl.when(pl.program_id(0) == 0)
            def _():
                o_ref[...] = jnp.zeros_like(o_ref)
            av = a_ref[...].astype(BF16)
            for s in range(N_CHIPS):
                o_ref[s] += _dot_tn(av, b_ref[:, s * ns:(s + 1) * ns].astype(BF16))

        return _call(body, name=name, grid=(L // TM,),
                     in_specs=[pl.BlockSpec((TM, K), lambda t: (t, 0)), pl.BlockSpec((TM, N), lambda t: (t, 0))],
                     out_specs=pl.BlockSpec((N_CHIPS, K, ns), lambda t: (0, 0, 0)),
                     out_shape=_sds((N_CHIPS, K, ns), F32), sem=("arbitrary",))(a, b)
    tn = ns // 2 if ns % (2 * LANES) == 0 else ns
    nj = ns // tn
    return _wgrad(a, b, name=name, grid_kn=(1, N_CHIPS * nj),
                  a_spec=pl.BlockSpec((TM, K), lambda i, j, t: (t, 0)),
                  b_spec=pl.BlockSpec((TM, tn), lambda i, j, t: (t, j)),
                  out_shape=_sds((N_CHIPS, K, ns), F32),
                  out_spec=pl.BlockSpec((None, K, tn), lambda i, j, t: (j // nj, 0, j % nj)))


def _wgrad_full(a, b, name):
    K, N = a.shape[1], b.shape[1]
    return _wgrad(a, b, name=name, grid_kn=(1, 1),
                  a_spec=pl.BlockSpec((TM, K), lambda i, j, t: (t, 0)),
                  b_spec=pl.BlockSpec((TM, N), lambda i, j, t: (t, 0)),
                  out_shape=_sds((K, N), F32), out_spec=pl.BlockSpec((K, N), lambda i, j, t: (0, 0)))


def _wgrad_ff_cols(a, b, name):
    K, fs = a.shape[1], b.shape[2]
    return _wgrad(a, b, name=name, grid_kn=(1, N_CHIPS),
                  a_spec=pl.BlockSpec((TM, K), lambda i, j, t: (t, 0)),
                  b_spec=pl.BlockSpec((None, TM, fs), lambda i, j, t: (j, t, 0)),
                  out_shape=_sds((N_CHIPS, K, fs), F32),
                  out_spec=pl.BlockSpec((None, K, fs), lambda i, j, t: (j, 0, 0)))


def _wgrad_ff_rows(a, b, name):
    fs, N = a.shape[2], b.shape[1]
    return _wgrad(a, b, name=name, grid_kn=(N_CHIPS, 1),
                  a_spec=pl.BlockSpec((None, TM, fs), lambda i, j, t: (i, t, 0)),
                  b_spec=pl.BlockSpec((TM, N), lambda i, j, t: (t, 0)),
                  out_shape=_sds((N_CHIPS, fs, N), F32),
                  out_spec=pl.BlockSpec((None, fs, N), lambda i, j, t: (i, 0, 0)))


def _head_sum_matrix():
    h = jnp.arange(ATTN_WIDTH) // HEAD_DIM
    return (h[:, None] == h[None, :]).astype(BF16)


def _mix_bwd(dx, z, aout, sa, sb, os_, ls_, ypre, w_ap, w_ga, w_gb, w_out):
    L = dx.shape[0]
    cs = D_MODEL // N_CHIPS
    ga_col = (3 * N_GROUPS * ATTN_WIDTH + SSM_WIDTH) // D_MODEL

    def body(dx_ref, ga_ref, gs_ref, aout_ref, sa_ref, sb_ref, o0, o1, o2, l0, l1, l2, ypre_ref,
             wap_ref, wga_ref, wgb_ref, wout_ref, hs_ref,
             dgates_ref, do0, do1, do2, dl0, dl1, dl2, gy_ref, daout_ref, dsa_ref, dsb_ref):
        dmix = _dot_nt(dx_ref[...].astype(BF16), wout_ref[...])
        sig_a = _sigmoid(ga_ref[...].astype(F32))
        sig_s = _sigmoid(gs_ref[...].astype(F32))
        a_out = aout_ref[...].astype(F32)
        s_a = sa_ref[...].astype(F32)
        sig_b = _sigmoid(sb_ref[...].astype(F32))
        s_out = s_a * sig_b
        daout = (dmix * sig_a).astype(BF16)
        daout_ref[...] = daout
        dgates_ref[:, :D_MODEL] = (dmix * a_out * sig_a * (1.0 - sig_a)).astype(BF16)
        dgates_ref[:, D_MODEL:] = (dmix * s_out * sig_s * (1.0 - sig_s)).astype(BF16)
        ds_out = dmix * sig_s
        dsa = (ds_out * sig_b).astype(BF16)
        dsb = (ds_out * s_a * sig_b * (1.0 - sig_b)).astype(BF16)
        dsa_ref[...] = dsa
        dsb_ref[...] = dsb
        da = jnp.zeros((TM_MIX, ATTN_WIDTH), F32)
        dy = jnp.zeros((TM_MIX, SSM_WIDTH), F32)
        for s in range(N_CHIPS):
            cols = slice(s * cs, (s + 1) * cs)
            da += _dot_nt(daout[:, cols], wap_ref[s])
            dy += _dot_nt(dsa[:, cols], wga_ref[s]) + _dot_nt(dsb[:, cols], wgb_ref[s])
        gy_ref[...] = dy * _gelu_grad(ypre_ref[...])
        w = _combine_weights(l0[...], l1[...], l2[...])
        hs = hs_ref[...]

        def head_sum(v):
            hi = v.astype(BF16)
            lo = (v - hi.astype(F32)).astype(BF16)
            return _dot(hi, hs) + _dot(lo, hs)

        t = [head_sum(da * o[...]) for o in (o0, o1, o2)]
        tbar = w[0] * t[0] + w[1] * t[1] + w[2] * t[2]
        for wg, tg, do_ref, dl_ref in zip(w, t, (do0, do1, do2), (dl0, dl1, dl2)):
            do_ref[...] = wg * da
            dl_ref[...] = wg * (tg - tbar)

    tok = lambda w: pl.BlockSpec((TM_MIX, w), lambda i: (i, 0))
    wsm = pl.BlockSpec((N_CHIPS, ATTN_WIDTH, cs), lambda i: (0, 0, 0))
    return _call(
        body, name="mix_bwd", grid=(L // TM_MIX,),
        in_specs=[tok(D_MODEL), pl.BlockSpec((TM_MIX, D_MODEL), lambda i: (i, ga_col)),
                  pl.BlockSpec((TM_MIX, D_MODEL), lambda i: (i, ga_col + 1)),
                  tok(D_MODEL), tok(D_MODEL), tok(D_MODEL)] + [tok(ATTN_WIDTH)] * 7
                 + [wsm, wsm, wsm, pl.BlockSpec((D_MODEL, D_MODEL), lambda i: (0, 0)),
                    pl.BlockSpec((ATTN_WIDTH, ATTN_WIDTH), lambda i: (0, 0))],
        out_specs=[tok(2 * D_MODEL)] + [tok(ATTN_WIDTH)] * 7 + [tok(D_MODEL)] * 3,
        out_shape=[_sds((L, 2 * D_MODEL), BF16)] + [_sds((L, ATTN_WIDTH), F32)] * 7 + [_sds((L, D_MODEL), BF16)] * 3,
        sem=("parallel",))(dx, z, z, aout, sa, sb, *os_, *ls_, ypre, w_ap, w_ga, w_gb,
                           w_out.reshape(D_MODEL, D_MODEL), _head_sum_matrix())


def _attn_bwd(z, o, l, do, dl, gq, gk, gi):
    L = z.shape[0]
    _, d = ATTN_PATTERN[gi]
    M = L // d
    nb = M // BLK
    zv = z.reshape(M, d * IN_COLS)
    cpb = IN_COLS // ATTN_WIDTH
    scale = HEAD_DIM ** -0.5
    tv = lambda t: t.reshape(M, d * ATTN_WIDTH)

    def body(q0_ref, q1_ref, k_ref, v_ref, o0_ref, o1_ref, l0_ref, l1_ref, do0_ref, do1_ref, dl0_ref, dl1_ref,
             gq_ref, gk_ref, dq_ref, dk_ref, dv_ref, dgq_ref, dgk_ref, carry):
        r, n = pl.program_id(0), pl.program_id(1)

        @pl.when(n == 0)
        def _():
            carry[...] = jnp.zeros_like(carry)

        @pl.when(jnp.logical_and(r == 0, n == 0))
        def _():
            dgq_ref[...] = jnp.zeros_like(dgq_ref)
            dgk_ref[...] = jnp.zeros_like(dgk_ref)

        mask_c, mask_p0 = _attn_masks()
        mask_p = jnp.logical_and(mask_p0, n < nb - 1)
        gqv = gq_ref[...]
        gkv = gk_ref[...]

        def rows8(t):
            return jnp.sum(t.reshape(BLK // SUBLANES, SUBLANES, HEAD_DIM), axis=0)

        for h in range(N_HEADS):
            sl = slice(h * HEAD_DIM, (h + 1) * HEAD_DIM)
            c1 = slice(h * HEAD_DIM, h * HEAD_DIM + 1)
            qh0, qn0, rq0 = _head_norm(q0_ref[:, sl].astype(F32), gqv)
            _, qn1, _ = _head_norm(q1_ref[:, sl].astype(F32), gqv)
            kh, kn, rk = _head_norm(k_ref[:, sl].astype(F32), gkv)
            qb0, qb1, kb = qn0.astype(BF16), qn1.astype(BF16), kn.astype(BF16)
            v = v_ref[:, sl]
            d0 = do0_ref[:, sl]
            d1 = do1_ref[:, sl]
            db0, db1 = d0.astype(BF16), d1.astype(BF16)
            delta0 = jnp.sum(d0 * o0_ref[:, sl], axis=-1, keepdims=True)
            delta1 = jnp.sum(d1 * o1_ref[:, sl], axis=-1, keepdims=True)
            p1 = jnp.where(mask_c, jnp.exp(_dot_nt(qb0, kb) * scale - l0_ref[:, c1]), 0.0)
            p2 = jnp.where(mask_p, jnp.exp(_dot_nt(qb1, kb) * scale - l1_ref[:, c1]), 0.0)
            ds1 = (p1 * (_dot_nt(db0, v) - delta0 + dl0_ref[:, c1])).astype(BF16)
            ds2 = (p2 * (_dot_nt(db1, v) - delta1 + dl1_ref[:, c1])).astype(BF16)
            dv_ref[:, sl] = (_dot_tn(p1.astype(BF16), db0) + _dot_tn(p2.astype(BF16), db1)).astype(BF16)
            dkn = (_dot_tn(ds1, qb0) + _dot_tn(ds2, qb1)) * scale
            dqn = _dot(ds1, kb) * scale + carry[:, sl]
            carry[:, sl] = _dot(ds2, kb) * scale
            dqh = dqn * gqv
            dq_ref[:, sl] = (rq0 * (dqh - qh0 * jnp.mean(dqh * qh0, axis=-1, keepdims=True))).astype(BF16)
            dkh = dkn * gkv
            dk_ref[:, sl] = (rk * (dkh - kh * jnp.mean(dkh * kh, axis=-1, keepdims=True))).astype(BF16)
            dgq_ref[...] += rows8(dqn * qh0)
            dgk_ref[...] += rows8(dkn * kh)

    nxt = lambda n: jnp.minimum(n + 1, nb - 1)
    blk = (BLK, ATTN_WIDTH)
    zc = lambda kind: pl.BlockSpec(blk, lambda r, n: (n, r * cpb + 3 * kind + gi))
    tc = pl.BlockSpec(blk, lambda r, n: (n, r))
    tn_ = pl.BlockSpec(blk, lambda r, n: (nxt(n), r))
    vec = pl.BlockSpec((1, HEAD_DIM), lambda r, n: (0, 0))
    acc = pl.BlockSpec((SUBLANES, HEAD_DIM), lambda r, n: (0, 0))
    dq, dk, dv, dgq, dgk = _call(
        body, name=f"attn_bwd_g{gi}", grid=(d, nb),
        in_specs=[zc(0), pl.BlockSpec(blk, lambda r, n: (nxt(n), r * cpb + gi)), zc(1), zc(2),
                  tc, tn_, tc, tn_, tc, tn_, tc, tn_, vec, vec],
        out_specs=[tc, tc, tc, acc, acc],
        out_shape=[_sds((M, d * ATTN_WIDTH), BF16)] * 3 + [_sds((SUBLANES, HEAD_DIM), F32)] * 2,
        scratch=[pltpu.VMEM((BLK, ATTN_WIDTH), F32)],
        sem=("arbitrary", "arbitrary"))(zv, zv, zv, zv, tv(o), tv(o), tv(l), tv(l), tv(do), tv(do), tv(dl), tv(dl),
                                        gq, gk)
    rs = lambda t: t.reshape(L, ATTN_WIDTH)
    return rs(dq), rs(dk), rs(dv), dgq, dgk


def _scan_rev_grad(sre, sim, rre, rim, are_ref, aim_ref, k0, init, seed_re, seed_im):
    ar = [_bcast(are_ref, k0 + kk) for kk in range(SCAN_GROUP)]
    ai = [-_bcast(aim_ref, k0 + kk) for kk in range(SCAN_GROUP)]

    def update(i, xprev, carry):
        out = []
        for kk in range(SCAN_GROUP):
            k = k0 + kk
            lr, li, dr, di = carry[4 * kk:4 * kk + 4]
            nr = ar[kk] * lr - ai[kk] * li + rre[k, _rows(i), :]
            ni = ar[kk] * li + ai[kk] * lr + rim[k, _rows(i), :]
            rre[k, _rows(i), :] = nr
            rim[k, _rows(i), :] = ni
            xr, xi = xprev(k)
            out += [nr, ni, dr + xr * nr + xi * ni, di + xr * ni - xi * nr]
        return tuple(out)

    def step(t, carry):
        i = SSM_TC - 1 - t
        return update(i, lambda k: (sre[k, _rows(i - 1), :], sim[k, _rows(i - 1), :]), carry)

    zero = jnp.zeros((SSM_SUB, LANES), F32)
    flat = []
    for re, im in init:
        flat += [re, im, zero, zero]
    res = lax.fori_loop(0, SSM_TC - 1, step, tuple(flat))
    res = update(0, lambda k: (seed_re[k], seed_im[k]), res)
    return [(res[4 * kk + 2], res[4 * kk + 3]) for kk in range(SCAN_GROUP)]


def _ssm_bwd(z, gy, pk, dskip, sd_re, sd_im):
    L = z.shape[0]
    nb = L // SSM_TB
    ucol = (3 * N_GROUPS * ATTN_WIDTH) // SSM_WIDTH
    nwin = N_SLAB // SLABS_PER_WIN

    def body(u_ref, gy_ref, are_ref, aim_ref, a64re_ref, a64im_ref, bwre_ref, bwim_ref, cwre_ref, cwim_ref, d_ref,
             sdre_ref, sdim_ref,
             du_ref, dare_ref, daim_ref, dbre_ref, dbim_ref, dcre_ref, dcim_ref, dd_ref,
             sre, sim, rre, rim, carry_re, carry_im, ends_re, ends_im, seed_re, seed_im):
        @pl.when(pl.program_id(0) == 0)
        def _():
            carry_re[...] = jnp.zeros_like(carry_re)
            carry_im[...] = jnp.zeros_like(carry_im)
            for ref in (dare_ref, daim_ref, dbre_ref, dbim_ref, dcre_ref, dcim_ref, dd_ref):
                ref[...] = jnp.zeros_like(ref)

        u = u_ref[...]
        gyv = gy_ref[...]
        gyb = gyv.astype(BF16)
        _ssm_fill(u, bwre_ref, bwim_ref, sre, sim)
        for k in range(N_SLAB):
            gw = gyb[:, (k // SLABS_PER_WIN) * LANES:(k // SLABS_PER_WIN + 1) * LANES]
            gr = _dot_nt(gw, cwre_ref[k])
            gi_ = -_dot_nt(gw, cwim_ref[k])
            for j in range(SSM_SUB):
                rre[k, j * SSM_PITCH:j * SSM_PITCH + SSM_TC, :] = gr[j * SSM_TC:(j + 1) * SSM_TC, :]
                rim[k, j * SSM_PITCH:j * SSM_PITCH + SSM_TC, :] = gi_[j * SSM_TC:(j + 1) * SSM_TC, :]
        zero = jnp.zeros((SSM_SUB, LANES), F32)
        for k0 in range(0, N_SLAB, SCAN_GROUP):
            grp = range(k0, k0 + SCAN_GROUP)
            _scan(sre, sim, are_ref, aim_ref, k0, [(sdre_ref[k], sdim_ref[k]) for k in grp],
                  reverse=False, store=True)
            ends = _scan(rre, rim, are_ref, aim_ref, k0, [(zero, zero)] * SCAN_GROUP, reverse=True, store=False,
                         sign=-1.0)
            for kk, k in enumerate(grp):
                ends_re[k] = ends[kk][0]
                ends_im[k] = ends[kk][1]
            for k in grp:
                _ssm_seeds(ends_re, ends_im, a64re_ref, a64im_ref, carry_re, carry_im, seed_re, seed_im, k,
                           reverse=True, sign=-1.0)
            das = _scan_rev_grad(sre, sim, rre, rim, are_ref, aim_ref, k0,
                                 [(seed_re[k], seed_im[k]) for k in grp], sdre_ref, sdim_ref)
            for kk, k in enumerate(grp):
                dare_ref[k] += das[kk][0]
                daim_ref[k] += das[kk][1]
        for w in range(nwin):
            cols = slice(w * LANES, (w + 1) * LANES)
            uw = u[:, cols]
            gw = gyb[:, cols]
            acc = gyv[:, cols] * d_ref[:, cols]
            for kk in range(SLABS_PER_WIN):
                k = w * SLABS_PER_WIN + kk
                lr = _slab_rows(rre, k).astype(BF16)
                li = _slab_rows(rim, k).astype(BF16)
                acc += _dot_nt(lr, bwre_ref[k]) + _dot_nt(li, bwim_ref[k])
                dbre_ref[k] += _dot_tn(uw, lr)
                dbim_ref[k] += _dot_tn(uw, li)
                dcre_ref[k] += _dot_tn(_slab_rows(sre, k).astype(BF16), gw)
                dcim_ref[k] -= _dot_tn(_slab_rows(sim, k).astype(BF16), gw)
            du_ref[:, cols] = acc.astype(BF16)
        dd_ref[...] += jnp.sum((gyv * u.astype(F32)).reshape(SSM_TB // SUBLANES, SUBLANES, SSM_WIDTH), axis=0)

    c2, c3 = _ssm_specs_consts()
    rev = lambda b: nb - 1 - b
    seed_spec = pl.BlockSpec((None, N_SLAB, SSM_SUB, LANES), lambda b: (rev(b), 0, 0, 0))
    tile_out = pl.BlockSpec((N_SLAB, SSM_SUB, LANES), lambda b: (0, 0, 0))
    small = pltpu.VMEM((N_SLAB, LANES), F32)
    tile = pltpu.VMEM((N_SLAB, SSM_SUB, LANES), F32)
    return _call(
        body, name="ssm_bwd", grid=(nb,),
        in_specs=[pl.BlockSpec((SSM_TB, SSM_WIDTH), lambda b: (rev(b), ucol)),
                  pl.BlockSpec((SSM_TB, SSM_WIDTH), lambda b: (rev(b), 0)),
                  c2, c2, c2, c2, c3, c3, c3, c3, pl.BlockSpec((1, SSM_WIDTH), lambda b: (0, 0)),
                  seed_spec, seed_spec],
        out_specs=[pl.BlockSpec((SSM_TB, SSM_WIDTH), lambda b: (rev(b), 0)), tile_out, tile_out, c3, c3, c3, c3,
                   pl.BlockSpec((SUBLANES, SSM_WIDTH), lambda b: (0, 0))],
        out_shape=[_sds((L, SSM_WIDTH), BF16), _sds((N_SLAB, SSM_SUB, LANES), F32),
                   _sds((N_SLAB, SSM_SUB, LANES), F32)] + [_sds((N_SLAB, LANES, LANES), F32)] * 4
                  + [_sds((SUBLANES, SSM_WIDTH), F32)],
        scratch=_ssm_scratch() + _ssm_scratch() + [small, small, tile, tile, tile, tile],
        sem=("arbitrary",))(z, gy, pk["a_re"], pk["a_im"], pk["a64_re"], pk["a64_im"],
                            pk["bw_re"].astype(BF16), pk["bw_im"].astype(BF16),
                            pk["cw_re"].astype(BF16), pk["cw_im"].astype(BF16), dskip, sd_re, sd_im)


def _in_proj_bwd(dz, w, x, g, dres):
    L = x.shape[0]
    ns = w.shape[2]
    tn = ns // 2
    nj = ns // tn
    nt = N_CHIPS * nj

    def body(dz_ref, w_ref, x_ref, g_ref, dres_ref, dx_ref, dg_ref, acc, dgacc):
        i, j = pl.program_id(0), pl.program_id(1)

        @pl.when(j == 0)
        def _():
            acc[...] = jnp.zeros_like(acc)

        @pl.when(jnp.logical_and(i == 0, j == 0))
        def _():
            dgacc[...] = jnp.zeros_like(dgacc)

        acc[...] += _dot_nt(dz_ref[...], w_ref[...])

        @pl.when(j == nt - 1)
        def _():
            dx, dg = _rms_bwd(x_ref[...], g_ref[...], acc[...])
            dx_ref[...] = dres_ref[...] + dx
            dgacc[...] += dg

        @pl.when(jnp.logical_and(i == pl.num_programs(0) - 1, j == nt - 1))
        def _():
            dg_ref[...] = jnp.sum(dgacc[...], axis=0, keepdims=True)

    tok = pl.BlockSpec((TM, D_MODEL), lambda i, j: (i, 0))
    vec = pl.BlockSpec((1, D_MODEL), lambda i, j: (0, 0))
    return _call(
        body, name="in_proj_bwd", grid=(L // TM, nt),
        in_specs=[pl.BlockSpec((TM, tn), lambda i, j: (i, j)),
                  pl.BlockSpec((None, D_MODEL, tn), lambda i, j: (j // nj, 0, j % nj)), tok, vec, tok],
        out_specs=[tok, vec],
        out_shape=[_sds((L, D_MODEL), F32), _sds((1, D_MODEL), F32)],
        scratch=[pltpu.VMEM((TM, D_MODEL), F32), pltpu.VMEM((SUBLANES, D_MODEL), F32)],
        sem=("arbitrary", "arbitrary"))(dz, w, x, g, dres)


SSM_PARAMS = ("lambda_re", "lambda_im", "log_dt", "b_re", "b_im", "c_re", "c_im")


def _layer_bwd(dx2, sv, p):
    g = {}
    dx1, dgate, dup, g["g_ffn"] = _ffn_bwd(dx2, sv["x1"], p["g_ffn"], sv["gate"], sv["up"],
                                            p["w_ffn_gate"], p["w_ffn_up"], p["w_ffn_down"])
    g["w_ffn_gate"] = _wgrad_ff_cols(sv["h2"], dgate, "wgrad_ffn_gate")
    g["w_ffn_up"] = _wgrad_ff_cols(sv["h2"], dup, "wgrad_ffn_up")
    g["w_ffn_down"] = _wgrad_ff_rows(sv["act"], dx2, "wgrad_ffn_down")

    (dgates, do0, do1, do2, dl0, dl1, dl2, gy, daout, dsa, dsb) = _mix_bwd(
        dx1, sv["z"], sv["aout"], sv["sa"], sv["sb"], sv["os"], sv["ls"], sv["ypre"],
        p["w_attn_proj"], p["w_glu_a"], p["w_glu_b"], p["w_out"])
    g["w_out"] = _wgrad_full(sv["mix"], dx1, "wgrad_out").reshape(N_CHIPS, D_MODEL // N_CHIPS, D_MODEL)
    g["w_attn_proj"] = _wgrad_cols(sv["a"], daout, "wgrad_attn_proj")
    g["w_glu_a"] = _wgrad_cols(sv["yact"], dsa, "wgrad_glu_a")
    g["w_glu_b"] = _wgrad_cols(sv["yact"], dsb, "wgrad_glu_b")

    du, da_re, da_im, dbw_re, dbw_im, dcw_re, dcw_im, dd = _ssm_bwd(
        sv["z"], gy, sv["pk"], p["d_skip"], sv["sd_re"], sv["sd_im"])
    g["d_skip"] = jnp.sum(dd, axis=0, keepdims=True)
    _, pull = jax.vjp(_ssm_pack, *[p[n] for n in SSM_PARAMS])
    zeros = jnp.zeros((N_SLAB, LANES), F32)
    ct = dict(a_re=jnp.sum(da_re, axis=1), a_im=jnp.sum(da_im, axis=1), a64_re=zeros, a64_im=zeros,
              bw_re=dbw_re, bw_im=dbw_im, cw_re=dcw_re, cw_im=dcw_im)
    for n, v in zip(SSM_PARAMS, pull(ct)):
        g[n] = v

    dqs, dks, dvs = [], [], []
    dgq = jnp.zeros((1, HEAD_DIM), F32)
    dgk = jnp.zeros((1, HEAD_DIM), F32)
    for gi, (do, dl) in enumerate(((do0, dl0), (do1, dl1), (do2, dl2))):
        dq, dk, dv, gq8, gk8 = _attn_bwd(sv["z"], sv["os"][gi], sv["ls"][gi], do, dl, p["g_q"], p["g_k"], gi)
        dqs.append(dq)
        dks.append(dk)
        dvs.append(dv)
        dgq = dgq + jnp.sum(gq8, axis=0, keepdims=True)
        dgk = dgk + jnp.sum(gk8, axis=0, keepdims=True)
    g["g_q"], g["g_k"] = dgq, dgk
    dz = jnp.concatenate(dqs + dks + dvs + [du, dgates], axis=1)
    g["w_in"] = _wgrad_cols(sv["h"], dz, "wgrad_in")
    dx, g["g_mix"] = _in_proj_bwd(dz, p["w_in"], sv["x"], p["g_mix"], dx1)
    return dx, g


ANY = pl.BlockSpec(memory_space=pl.ANY)


def _place():
    x, y, c = lax.axis_index("x"), lax.axis_index("y"), lax.axis_index("c")
    others = [(1 - x, y), (x, 1 - y), (1 - x, 1 - y)]
    return x, y, c, others


def _half(ref, hc):
    rows = ref.shape[-2] // 2
    idx = (slice(None),) * (len(ref.shape) - 2) + (pl.ds(hc * rows, rows), slice(None))
    return ref.at[idx]


def _comm_call(body, name, ins, out_shapes, n_remote, aliases=None):
    scratch = [pltpu.SemaphoreType.DMA((n_remote,)), pltpu.SemaphoreType.DMA((n_remote,))]
    return pl.pallas_call(
        body, name=name, in_specs=[ANY] * len(ins), out_specs=[ANY] * len(out_shapes), out_shape=out_shapes,
        scratch_shapes=scratch, input_output_aliases=aliases or {})(*ins)


def _cast_place(w, l, chip_idx):
    _, R, C = w.shape
    tr = R // 2

    def body(me_ref, w_ref, o_ref):
        o_ref[...] = w_ref[...].astype(BF16)

    return _call(body, name=f"cast_place_l{l}", grid=(R // tr,), prefetch=1,
                 in_specs=[pl.BlockSpec((None, tr, C), lambda i, me_ref: (l, i, 0))],
                 out_specs=pl.BlockSpec((None, tr, C), lambda i, me_ref: (me_ref[0], i, 0)),
                 out_shape=_sds((N_CHIPS, R, C), BF16), sem=("arbitrary",))(chip_idx, w)


def _gather_weights(bufs):
    n = len(bufs)

    def body(*refs):
        outs = refs[n:2 * n]
        send, recv = refs[2 * n:]
        x, y, c, others = _place()
        me = 2 * x + y

        def remote(region, k, to):
            return pltpu.make_async_remote_copy(src_ref=region, dst_ref=region, send_sem=send.at[k],
                                                recv_sem=recv.at[k], device_id=to, device_id_type=MESH)

        started = []
        for a in range(n):
            for j, (cx, cy) in enumerate(others):
                cp = remote(_half(outs[a].at[me], c), 6 * a + j, (cx, cy, c))
                cp.start()
                started.append(cp)
        for a in range(n):
            for j, (cx, cy) in enumerate(others):
                landed = _half(outs[a].at[2 * cx + cy], c)
                remote(landed, 6 * a + j, (cx, cy, c)).wait_recv()
                cp = remote(landed, 6 * a + 3 + j, (x, y, 1 - c))
                cp.start()
                started.append(cp)
        for a in range(n):
            for j, (cx, cy) in enumerate(others):
                remote(_half(outs[a].at[2 * cx + cy], 1 - c), 6 * a + 3 + j, (x, y, 1 - c)).wait_recv()
        for cp in started:
            cp.wait_send()

    outs = [_sds(b.shape, b.dtype) for b in bufs]
    return _comm_call(body, "gather_weights", bufs, outs, 6 * n, aliases={a: a for a in range(n)})


def _swap_halves(gs):
    n = len(gs)

    def body(*refs):
        ins, outs = refs[:n], refs[n:2 * n]
        send, recv = refs[2 * n:]
        x, y, c, _ = _place()
        cps = [pltpu.make_async_remote_copy(src_ref=_half(ins[a], 1 - c), dst_ref=outs[a], send_sem=send.at[a],
                                            recv_sem=recv.at[a], device_id=(x, y, 1 - c), device_id_type=MESH)
               for a in range(n)]
        for cp in cps:
            cp.start()
        for cp in cps:
            cp.wait()

    outs = [_sds((g.shape[0], g.shape[1] // 2, g.shape[2]), g.dtype) for g in gs]
    return _comm_call(body, "swap_halves", gs, outs, n)


def _scatter_to_owners(ss):
    n = len(ss)

    def body(*refs):
        ins, outs = refs[:n], refs[n:2 * n]
        send, recv = refs[2 * n:]
        x, y, c, others = _place()
        cps = []
        for a in range(n):
            for j, (cx, cy) in enumerate(others):
                cps.append(pltpu.make_async_remote_copy(
                    src_ref=ins[a].at[2 * cx + cy], dst_ref=outs[a].at[j], send_sem=send.at[3 * a + j],
                    recv_sem=recv.at[3 * a + j], device_id=(cx, cy, c), device_id_type=MESH))
        for cp in cps:
            cp.start()
        for cp in cps:
            cp.wait()

    outs = [_sds((N_CHIPS - 1,) + s.shape[1:], s.dtype) for s in ss]
    return _comm_call(body, "scatter_to_owners", ss, outs, 3 * n)


def _join_halves(bufs):
    n = len(bufs)

    def body(*refs):
        outs = refs[n:2 * n]
        send, recv = refs[2 * n:]
        x, y, c, _ = _place()

        def swap(a, hc):
            region = _half(outs[a], hc)
            return pltpu.make_async_remote_copy(src_ref=region, dst_ref=region, send_sem=send.at[a],
                                                recv_sem=recv.at[a], device_id=(x, y, 1 - c), device_id_type=MESH)

        cps = [swap(a, c) for a in range(n)]
        for cp in cps:
            cp.start()
        for a in range(n):
            swap(a, 1 - c).wait_recv()
        for cp in cps:
            cp.wait_send()

    outs = [_sds(b.shape, b.dtype) for b in bufs]
    return _comm_call(body, "join_halves", bufs, outs, n, aliases={a: a for a in range(n)})


def _gather_small(v):
    rows, n = v.shape

    def body(v_ref, out_ref, send, recv, lsem):
        x, y, c, others = _place()
        me, sibling = (x, y, c), (x, y, 1 - c)

        def blk(px, py, pc):
            return out_ref.at[pl.ds((4 * px + 2 * py + pc) * rows, rows), :]

        def copy(k, block, to, src=None):
            return pltpu.make_async_remote_copy(src_ref=blk(*block) if src is None else src, dst_ref=blk(*block),
                                                send_sem=send.at[k], recv_sem=recv.at[k], device_id=to,
                                                device_id_type=MESH)

        mine = pltpu.make_async_copy(v_ref, blk(*me), lsem)
        mine.start()
        first = [copy(0, me, sibling, src=v_ref)]
        first += [copy(1 + j, me, (*chip, c), src=v_ref) for j, chip in enumerate(others)]
        for cp in first:
            cp.start()
        passed = [copy(4 + j, (*chip, c), sibling) for j, chip in enumerate(others)]
        for j, chip in enumerate(others):
            copy(1 + j, (*chip, c), me).wait_recv()
            passed[j].start()
        copy(0, sibling, me).wait_recv()
        for j, chip in enumerate(others):
            copy(4 + j, (*chip, 1 - c), me).wait_recv()
        for cp in first + passed:
            cp.wait_send()
        mine.wait()

    return pl.pallas_call(
        body, name="gather_small", out_shape=_sds((8 * rows, n), v.dtype),
        in_specs=[pl.BlockSpec(memory_space=pltpu.VMEM)], out_specs=pl.BlockSpec(memory_space=pltpu.VMEM),
        scratch_shapes=[pltpu.SemaphoreType.DMA((7,)), pltpu.SemaphoreType.DMA((7,)), pltpu.SemaphoreType.DMA],
        compiler_params=pltpu.CompilerParams(vmem_limit_bytes=VMEM_LIMIT))(v)


def _add_half(g, p, c):
    _, R, C = g.shape
    half = R // 2

    def body(c_ref, g_ref, p_ref, o_ref):
        o_ref[...] = g_ref[...] + p_ref[...]

    blk = (None, half, C)
    return _call(body, name="add_half", grid=(N_CHIPS,), prefetch=1,
                 in_specs=[pl.BlockSpec(blk, lambda s, c_ref: (s, c_ref[0], 0)),
                           pl.BlockSpec(blk, lambda s, c_ref: (s, 0, 0))],
                 out_specs=pl.BlockSpec(blk, lambda s, c_ref: (s, 0, 0)),
                 out_shape=_sds((N_CHIPS, half, C), F32), sem=("arbitrary",))(c, g, p)


def _sum_owner(s, q, buf, l, me, c):
    _, half, C = s.shape
    tr = half // 2

    def body(me_ref, c_ref, s_ref, q0, q1, q2, buf_ref, o_ref):
        o_ref[...] = ((s_ref[...] + q0[...]) + q1[...]) + q2[...]

    blk = (None, tr, C)
    qspec = lambda j: pl.BlockSpec(blk, lambda i, me_ref, c_ref: (j, i, 0))
    return _call(body, name=f"sum_owner_l{l}", grid=(half // tr,), prefetch=2,
                 in_specs=[pl.BlockSpec(blk, lambda i, me_ref, c_ref: (me_ref[0], i, 0)),
                           qspec(0), qspec(1), qspec(2), ANY],
                 out_specs=pl.BlockSpec(blk, lambda i, me_ref, c_ref: (l, 2 * c_ref[0] + i, 0)),
                 out_shape=_sds(buf.shape, F32), sem=("arbitrary",), aliases={6: 0})(me, c, s, q, q, q, buf)


def _adamw_math(w, g, m, v):
    m = ADAM_B1 * m + (1.0 - ADAM_B1) * g
    v = ADAM_B2 * v + (1.0 - ADAM_B2) * (g * g)
    m_hat = m / (1.0 - ADAM_B1 ** ADAM_STEP)
    v_hat = v / (1.0 - ADAM_B2 ** ADAM_STEP)
    delta = -ADAM_LR * (m_hat / (jnp.sqrt(v_hat) + ADAM_EPS) + ADAM_WD * w)
    return delta, m, v


def _adamw(w, g, m, v):
    rows, C = w.shape
    tr = next(t for t in (256, 128, 64) if rows % t == 0)

    def body(w_ref, g_ref, m_ref, v_ref, d_ref, nm_ref, nv_ref):
        d, nm, nv = _adamw_math(w_ref[...], g_ref[...], m_ref[...], v_ref[...])
        d_ref[...] = d
        nm_ref[...] = nm
        nv_ref[...] = nv

    spec = pl.BlockSpec((tr, C), lambda i: (i, 0))
    return _call(body, name="adamw", grid=(rows // tr,), in_specs=[spec] * 4, out_specs=[spec] * 3,
                 out_shape=[_sds((rows, C), F32)] * 3, sem=("parallel",))(w, g, m, v)


def _small_update(gathered, w, m, v):
    _, rows, n = gathered.shape
    tr = rows // 7

    def body(ga_ref, w_ref, m_ref, v_ref, g_ref, d_ref, nm_ref, nv_ref):
        g = ga_ref[0]
        for k in range(1, 8):
            g = g + ga_ref[k]
        d, nm, nv = _adamw_math(w_ref[...], g, m_ref[...], v_ref[...])
        g_ref[...] = g
        d_ref[...] = d
        nm_ref[...] = nm
        nv_ref[...] = nv

    spec = pl.BlockSpec((tr, n), lambda i: (i, 0))
    return _call(body, name="small_update", grid=(rows // tr,),
                 in_specs=[pl.BlockSpec((8, tr, n), lambda i: (0, i, 0)), spec, spec, spec], out_specs=[spec] * 4,
                 out_shape=[_sds((rows, n), F32)] * 4, sem=("parallel",))(gathered, w, m, v)


WEIGHTS = ("g_mix", "w_in", "g_q", "g_k", "w_attn_proj", "lambda_re", "lambda_im", "log_dt", "b_re", "b_im",
           "c_re", "c_im", "d_skip", "w_glu_a", "w_glu_b", "w_out", "g_ffn", "w_ffn_gate", "w_ffn_up", "w_ffn_down")
BIG = ("w_in", "w_attn_proj", "w_glu_a", "w_glu_b", "w_out", "w_ffn_gate", "w_ffn_up", "w_ffn_down")
SMALL = tuple(n for n in WEIGHTS if n not in BIG)
ROW_VECTORS = ("g_mix", "g_q", "g_k", "d_skip", "g_ffn")
PACK_QUANTUM = LANES * SUBLANES * 7


def _pack_small(parts, extra):
    flat = jnp.concatenate([parts[n].reshape(-1).astype(F32) for n in SMALL] + [extra.reshape(-1)])
    pad = -flat.shape[0] % PACK_QUANTUM
    return jnp.pad(flat, (0, pad)).reshape(-1, LANES)


def _unpack_small(packed, like):
    flat = packed.reshape(-1)
    out, at = {}, 0
    for n in SMALL:
        size = math.prod(like[n].shape)
        out[n] = flat[at:at + size].reshape(like[n].shape)
        at += size
    return out, flat[at]


def kernel(x, g_mix, w_in, g_q, g_k, w_attn_proj, lambda_re, lambda_im, log_dt, b_re, b_im, c_re, c_im, d_skip, w_glu_a, w_glu_b, w_out, g_ffn, w_ffn_gate, w_ffn_up, w_ffn_down, loss_target, m_g_mix, m_w_in, m_g_q, m_g_k, m_w_attn_proj, m_lambda_re, m_lambda_im, m_log_dt, m_b_re, m_b_im, m_c_re, m_c_im, m_d_skip, m_w_glu_a, m_w_glu_b, m_w_out, m_g_ffn, m_w_ffn_gate, m_w_ffn_up, m_w_ffn_down, v_g_mix, v_w_in, v_g_q, v_g_k, v_w_attn_proj, v_lambda_re, v_lambda_im, v_log_dt, v_b_re, v_b_im, v_c_re, v_c_im, v_d_skip, v_w_glu_a, v_w_glu_b, v_w_out, v_g_ffn, v_w_ffn_gate, v_w_ffn_up, v_w_ffn_down):
    given = dict(locals())
    W = {n: given[n] for n in WEIGHTS}
    M = {n: given["m_" + n] for n in WEIGHTS}
    V = {n: given["v_" + n] for n in WEIGHTS}
    depth = g_mix.shape[0]
    xl = x.reshape(x.shape[-2:])
    target = loss_target.reshape(loss_target.shape[-2:])
    c_idx = lax.axis_index("c").astype(jnp.int32).reshape(1)
    chip_idx = (2 * lax.axis_index("x") + lax.axis_index("y")).astype(jnp.int32).reshape(1)

    params = []
    for l in range(depth):
        full = _gather_weights([_cast_place(W[n], l, chip_idx) for n in BIG])
        p = dict(zip(BIG, full))
        for n in SMALL:
            p[n] = W[n][l][None] if n in ROW_VECTORS else W[n][l]
        params.append(p)

    saved, h = [], xl
    for l in range(depth):
        h, sv = _layer_fwd(h, params[l])
        saved.append(sv)
    dx, loss_part = _loss_head(h, target)

    owned = [lax.empty(W[n].shape, F32) for n in BIG]
    small_grads = [None] * depth
    for l in reversed(range(depth)):
        dx, g = _layer_bwd(dx, saved[l], params[l])
        parts = [g[n] for n in BIG]
        sib = _swap_halves(parts)
        chip = [_add_half(a, b, c_idx) for a, b in zip(parts, sib)]
        recv = _scatter_to_owners(chip)
        owned = [_sum_owner(s, q, buf, l, chip_idx, c_idx) for s, q, buf in zip(chip, recv, owned)]
        small_grads[l] = g
    reduced = dict(zip(BIG, _join_halves(owned)))

    grads, delta, new_m, new_v = {}, {}, {}, {}
    for n in BIG:
        shape = W[n].shape
        two_d = (shape[0] * shape[1], shape[2])
        d, nm, nv = _adamw(W[n].reshape(two_d), reduced[n].reshape(two_d), M[n].reshape(two_d), V[n].reshape(two_d))
        grads[n], delta[n], new_m[n], new_v[n] = reduced[n], d.reshape(shape), nm.reshape(shape), nv.reshape(shape)

    stacked = {n: jnp.stack([small_grads[l][n] for l in range(depth)]) for n in SMALL}
    zero = jnp.zeros((1,), F32)
    packed = _pack_small(stacked, loss_part)
    gathered = _gather_small(packed).reshape(8, *packed.shape)
    gs, ds, nms, nvs = _small_update(gathered, _pack_small(W, zero), _pack_small(M, zero), _pack_small(V, zero))
    sg, loss = _unpack_small(gs, W)
    sd, _ = _unpack_small(ds, W)
    sm, _ = _unpack_small(nms, W)
    sv_, _ = _unpack_small(nvs, W)
    for n in SMALL:
        grads[n], delta[n], new_m[n], new_v[n] = sg[n], sd[n], sm[n], sv_[n]

    return (loss, dx.reshape(x.shape), *[grads[n] for n in WEIGHTS], *[delta[n] for n in WEIGHTS],
            *[new_m[n] for n in WEIGHTS], *[new_v[n] for n in WEIGHTS])
```

```python
import functools
import math

import jax
import jax.numpy as jnp
from jax import lax
from jax.experimental import pallas as pl
from jax.experimental.pallas import tpu as pltpu

F32 = jnp.float32
BF16 = jnp.bfloat16

D_MODEL = 1024
DEPTH = 4
HEAD_DIM = 64
N_HEADS = 8
ATTN_WIDTH = N_HEADS * HEAD_DIM
ATTN_PATTERN = ((128, 1), (512, 4), (2048, 16))
N_GROUPS = len(ATTN_PATTERN)
BLK = 128
SSM_WIDTH = 512
SSM_GROUP = 16
SSM_GROUPS = 32
SSM_STATE = 64
D_FF = 2816
IN_COLS = 7168
EPS = 1e-6
ADAM_LR, ADAM_B1, ADAM_B2, ADAM_EPS, ADAM_WD, ADAM_STEP = 0.001, 0.9, 0.999, 1e-08, 0.01, 10

N_CHIPS = 4
MESH = pl.DeviceIdType.MESH

LANES = 128
SUBLANES = 8
VMEM_LIMIT = 56 * 1024 * 1024

TM = 512
TM_MIX = 256

SSM_TB = 512
SSM_TC = 64
SSM_SUB = SUBLANES
SSM_PITCH = 72
N_SLAB = SSM_GROUPS * SSM_STATE // LANES
SLABS_PER_WIN = 4
SCAN_GROUP = 4


def _params(sem=None, collective=False):
    return pltpu.CompilerParams(dimension_semantics=sem, vmem_limit_bytes=VMEM_LIMIT)


def _call(body, *, name, grid, in_specs, out_specs, out_shape, scratch=(), sem=None, aliases=None,
          prefetch=0):
    kw = {}
    if aliases:
        kw["input_output_aliases"] = aliases
    if prefetch:
        gs = pltpu.PrefetchScalarGridSpec(num_scalar_prefetch=prefetch, grid=grid, in_specs=in_specs,
                                          out_specs=out_specs, scratch_shapes=list(scratch))
        return pl.pallas_call(body, name=name, grid_spec=gs, out_shape=out_shape,
                              compiler_params=_params(sem), **kw)
    return pl.pallas_call(body, name=name, grid=grid, in_specs=in_specs, out_specs=out_specs,
                          out_shape=out_shape, scratch_shapes=list(scratch),
                          compiler_params=_params(sem), **kw)


def _sds(shape, dtype):
    return jax.ShapeDtypeStruct(shape, dtype)


def _sigmoid(v):
    return 1.0 / (1.0 + jnp.exp(-v))


def _dot(a, b):
    return jnp.dot(a, b, preferred_element_type=F32)


def _dot_nt(a, b):
    return lax.dot_general(a, b, (((1,), (1,)), ((), ())), preferred_element_type=F32)


def _dot_tn(a, b):
    return lax.dot_general(a, b, (((0,), (0,)), ((), ())), preferred_element_type=F32)


def _in_proj_fwd(x, g, w):
    L = x.shape[0]
    ns = w.shape[2]
    tn = ns // 2
    nj = ns // tn

    def body(x_ref, g_ref, w_ref, z_ref, h_ref):
        @pl.when(pl.program_id(1) == 0)
        def _():
            xv = x_ref[...]
            r = lax.rsqrt(jnp.mean(xv * xv, axis=-1, keepdims=True) + EPS)
            h_ref[...] = (xv * r * g_ref[...]).astype(BF16)
        z_ref[...] = _dot(h_ref[...], w_ref[...]).astype(BF16)

    return _call(
        body, name="in_proj_fwd", grid=(L // TM, N_CHIPS * nj),
        in_specs=[pl.BlockSpec((TM, D_MODEL), lambda i, j: (i, 0)),
                  pl.BlockSpec((1, D_MODEL), lambda i, j: (0, 0)),
                  pl.BlockSpec((None, D_MODEL, tn), lambda i, j: (j // nj, 0, j % nj))],
        out_specs=[pl.BlockSpec((TM, tn), lambda i, j: (i, j)),
                   pl.BlockSpec((TM, D_MODEL), lambda i, j: (i, 0))],
        out_shape=[_sds((L, N_CHIPS * ns), BF16), _sds((L, D_MODEL), BF16)],
        sem=("parallel", "arbitrary"))(x, g, w)


DL_TILE = 512
SCALE = HEAD_DIM ** -0.5


def _perm_matrix(d):
    rho = jnp.arange(DL_TILE)
    src = rho // (DL_TILE // d) + d * (rho % (DL_TILE // d))
    return (src[:, None] == jnp.arange(DL_TILE)[None, :]).astype(BF16)


def _head_sum_matrix():
    h = jnp.arange(ATTN_WIDTH) // HEAD_DIM
    return (h[:, None] == h[None, :]).astype(BF16)


def _split(v):
    hi = v.astype(BF16)
    return hi, (v - hi.astype(F32)).astype(BF16)


def _head_sum(v, hs):
    hi, lo = _split(v)
    return _dot(hi, hs) + _dot(lo, hs)


def _permute(pm, v):
    hi, lo = _split(v)
    return _dot(pm, hi) + _dot(pm, lo)


def _dl_view(t, d):
    if d * BLK <= DL_TILE:
        return t
    return t.reshape(t.shape[0] // DL_TILE, d, DL_TILE // d, t.shape[1])


def _dl_spec(d, width, which):
    if d * BLK <= DL_TILE:
        per_tile = DL_TILE // (d * BLK)
        return pl.BlockSpec((BLK, width), lambda r, n: ((which(n) // per_tile) * (DL_TILE // BLK)
                                                       + r * per_tile + which(n) % per_tile, 0))
    tiles = d * BLK // DL_TILE
    return pl.BlockSpec((tiles, None, DL_TILE // d, width), lambda r, n: (which(n), r, 0, 0))


def _dl_read(ref):
    v = ref[...]
    return v if v.ndim == 2 else v.reshape(BLK, v.shape[-1])


def _dl_write(ref, v):
    ref[...] = v if len(ref.shape) == 2 else v.reshape(ref.shape)


def _qkv_prep(z, gq_t, gk_t):
    L = z.shape[0]
    qkv_w = N_GROUPS * ATTN_WIDTH

    def body(zq_ref, zk_ref, zv_ref, gq_ref, gk_ref, hs_ref, p1_ref, p2_ref, *outs):
        hs = hs_ref[...]
        perms = (None, p1_ref[...], p2_ref[...])
        for g in range(N_GROUPS):
            cols = slice(g * ATTN_WIDTH, (g + 1) * ATTN_WIDTH)
            xq = zq_ref[:, cols].astype(F32)
            xk = zk_ref[:, cols].astype(F32)
            rq = lax.rsqrt(_head_sum(xq * xq, hs) * (1.0 / HEAD_DIM) + EPS)
            rk = lax.rsqrt(_head_sum(xk * xk, hs) * (1.0 / HEAD_DIM) + EPS)
            vals = [(xq * rq * (gq_ref[...] * SCALE)).astype(BF16), (xk * rk * gk_ref[...]).astype(BF16),
                    zv_ref[:, cols]]
            for j, t in enumerate(vals):
                if perms[g] is not None:
                    t = _dot(perms[g], t).astype(BF16)
                outs[3 * g + j][...] = t

    tile = pl.BlockSpec((DL_TILE, ATTN_WIDTH), lambda i: (i, 0))
    mat = pl.BlockSpec((DL_TILE, DL_TILE), lambda i: (0, 0))
    vec = pl.BlockSpec((1, ATTN_WIDTH), lambda i: (0, 0))
    outs = _call(
        body, name="qkv_prep", grid=(L // DL_TILE,),
        in_specs=[pl.BlockSpec((DL_TILE, qkv_w), lambda i: (i, 0)), pl.BlockSpec((DL_TILE, qkv_w), lambda i: (i, 1)),
                  pl.BlockSpec((DL_TILE, qkv_w), lambda i: (i, 2)), vec, vec, mat, mat, mat],
        out_specs=[tile] * 9, out_shape=[_sds((L, ATTN_WIDTH), BF16)] * 9,
        sem=("parallel",))(z, z, z, gq_t, gk_t, _head_sum_matrix(), _perm_matrix(ATTN_PATTERN[1][1]),
                           _perm_matrix(ATTN_PATTERN[2][1]))
    return [tuple(outs[3 * g:3 * g + 3]) for g in range(N_GROUPS)]


def _pair_masks():
    lane = lax.broadcasted_iota(jnp.int32, (1, LANES), 1)
    return lane < HEAD_DIM, lane >= HEAD_DIM


def _attn_fwd(qs, ks, v, gi):
    L = qs.shape[0]
    _, d = ATTN_PATTERN[gi]
    nb = L // (d * BLK)

    def body(q_ref, kc_ref, kp_ref, vc_ref, vp_ref, o_ref, l_ref):
        n = pl.program_id(1)
        qi = lax.broadcasted_iota(jnp.int32, (BLK, 2 * BLK), 0)
        kj = lax.broadcasted_iota(jnp.int32, (BLK, 2 * BLK), 1)
        prev = kj < BLK
        mask = jnp.logical_and(jnp.where(prev, kj, qi) >= jnp.where(prev, qi, kj - BLK),
                               kj >= jnp.where(n > 0, 0, BLK))
        q = _dl_read(q_ref)
        kw = jnp.concatenate([_dl_read(kp_ref), _dl_read(kc_ref)], axis=0)
        vw = jnp.concatenate([_dl_read(vp_ref), _dl_read(vc_ref)], axis=0)
        one = jnp.ones((2 * BLK, LANES), BF16)
        o_parts, l_parts = [], []
        for hp in range(N_HEADS // 2):
            ls = slice(hp * LANES, (hp + 1) * LANES)
            qp, kp_, vp_ = q[:, ls], kw[:, ls], vw[:, ls]
            num = jnp.zeros((BLK, LANES), F32)
            den = jnp.zeros((BLK, LANES), F32)
            mb = jnp.zeros((BLK, LANES), F32)
            for he in _pair_masks():
                s = jnp.where(mask, _dot_nt(jnp.where(he, qp, 0), kp_), -jnp.inf)
                m = jnp.max(s, axis=-1, keepdims=True)
                p = jnp.exp(s - m).astype(BF16)
                acc = _dot(p, jnp.concatenate([jnp.where(he, vp_, 0), jnp.where(he, one, 0)], axis=1))
                num += acc[:, :LANES]
                den += acc[:, LANES:]
                mb = jnp.where(he, m, mb)
            o_parts.append((num / den).astype(BF16))
            l_parts.append(mb + jnp.log(den))
        _dl_write(o_ref, jnp.concatenate(o_parts, axis=1))
        _dl_write(l_ref, jnp.concatenate(l_parts, axis=1))

    cur = _dl_spec(d, ATTN_WIDTH, lambda n: n)
    prev = _dl_spec(d, ATTN_WIDTH, lambda n: jnp.maximum(n - 1, 0))
    view = lambda t: _dl_view(t, d)
    o, l = _call(
        body, name=f"attn_fwd_g{gi}", grid=(d, nb), in_specs=[cur, cur, prev, cur, prev], out_specs=[cur, cur],
        out_shape=[_sds(view(qs).shape, BF16), _sds(view(qs).shape, F32)],
        sem=("parallel", "parallel"))(view(qs), view(ks), view(ks), view(v), view(v))
    return o.reshape(L, ATTN_WIDTH), l.reshape(L, ATTN_WIDTH)


def _to_token_order(os_, ls_, pts):
    o_tok, l_tok = [], []
    for o, l, pt in zip(os_, ls_, pts):
        if pt is None:
            o_tok.append(o.astype(F32))
            l_tok.append(l)
        else:
            o_tok.append(_dot(pt, o))
            l_tok.append(_permute(pt, l))
    return o_tok, l_tok


def _combine_fwd(os_, ls_):
    L = os_[0].shape[0]

    def body(o0, o1, o2, l0, l1, l2, pt1_ref, pt2_ref, a_ref):
        o_tok, l_tok = _to_token_order((o0[...], o1[...], o2[...]), (l0[...], l1[...], l2[...]),
                                       (None, pt1_ref[...], pt2_ref[...]))
        w = _combine_weights(*l_tok)
        a_ref[...] = (w[0] * o_tok[0] + w[1] * o_tok[1] + w[2] * o_tok[2]).astype(BF16)

    tile = pl.BlockSpec((DL_TILE, ATTN_WIDTH), lambda i: (i, 0))
    mat = pl.BlockSpec((DL_TILE, DL_TILE), lambda i: (0, 0))
    return _call(body, name="combine_fwd", grid=(L // DL_TILE,), in_specs=[tile] * 6 + [mat, mat], out_specs=tile,
                 out_shape=_sds((L, ATTN_WIDTH), BF16), sem=("parallel",))(
                     *os_, *ls_, _perm_matrix(ATTN_PATTERN[1][1]).T, _perm_matrix(ATTN_PATTERN[2][1]).T)


def _gelu(v):
    c = math.sqrt(2.0 / math.pi)
    return 0.5 * v * (1.0 + jnp.tanh(c * (v + 0.044715 * v * v * v)))


def _gelu_grad(v):
    c = math.sqrt(2.0 / math.pi)
    t = jnp.tanh(c * (v + 0.044715 * v * v * v))
    return 0.5 * (1.0 + t) + 0.5 * v * (1.0 - t * t) * c * (1.0 + 3.0 * 0.044715 * v * v)


def _ssm_fill(u, bwre_ref, bwim_ref, sre, sim):
    for k in range(N_SLAB):
        w = k // SLABS_PER_WIN
        uw = u[:, w * LANES:(w + 1) * LANES]
        br = _dot(uw, bwre_ref[k])
        bi = _dot(uw, bwim_ref[k])
        for j in range(SSM_SUB):
            sre[k, j * SSM_PITCH:j * SSM_PITCH + SSM_TC, :] = br[j * SSM_TC:(j + 1) * SSM_TC, :]
            sim[k, j * SSM_PITCH:j * SSM_PITCH + SSM_TC, :] = bi[j * SSM_TC:(j + 1) * SSM_TC, :]


def _rows(i):
    return pl.ds(i, SSM_SUB, stride=SSM_PITCH)


def _slab_rows(ref, k):
    return jnp.concatenate([ref[k, j * SSM_PITCH:j * SSM_PITCH + SSM_TC, :] for j in range(SSM_SUB)], axis=0)


def _bcast(ref, k):
    return jnp.broadcast_to(ref[pl.ds(k, 1), :], (SSM_SUB, LANES))


def _scan(sre, sim, are_ref, aim_ref, k0, init, *, reverse, store, sign=1.0):
    ar = [_bcast(are_ref, k0 + kk) for kk in range(SCAN_GROUP)]
    ai = [sign * _bcast(aim_ref, k0 + kk) for kk in range(SCAN_GROUP)]

    def step(t, carry):
        i = SSM_TC - 1 - t if reverse else t
        out = []
        for kk in range(SCAN_GROUP):
            k = k0 + kk
            xr, xi = carry[2 * kk], carry[2 * kk + 1]
            nr = ar[kk] * xr - ai[kk] * xi + sre[k, _rows(i), :]
            ni = ar[kk] * xi + ai[kk] * xr + sim[k, _rows(i), :]
            if store:
                sre[k, _rows(i), :] = nr
                sim[k, _rows(i), :] = ni
            out += [nr, ni]
        return tuple(out)

    flat = []
    for re, im in init:
        flat += [re, im]
    res = lax.fori_loop(0, SSM_TC, step, tuple(flat))
    return [(res[2 * kk], res[2 * kk + 1]) for kk in range(SCAN_GROUP)]


def _ssm_seeds(ends_re, ends_im, a64re_ref, a64im_ref, carry_re, carry_im, seed_re, seed_im, k,
               *, reverse, sign=1.0):
    ar = a64re_ref[pl.ds(k, 1), :]
    ai = sign * a64im_ref[pl.ds(k, 1), :]
    cr = carry_re[pl.ds(k, 1), :]
    ci = carry_im[pl.ds(k, 1), :]
    order = range(SSM_SUB - 1, -1, -1) if reverse else range(SSM_SUB)
    for j in order:
        seed_re[k, pl.ds(j, 1), :] = cr
        seed_im[k, pl.ds(j, 1), :] = ci
        er = ends_re[k, pl.ds(j, 1), :]
        ei = ends_im[k, pl.ds(j, 1), :]
        cr, ci = ar * cr - ai * ci + er, ar * ci + ai * cr + ei
    carry_re[pl.ds(k, 1), :] = cr
    carry_im[pl.ds(k, 1), :] = ci


def _ssm_specs_consts():
    c2 = pl.BlockSpec((N_SLAB, LANES), lambda b: (0, 0))
    c3 = pl.BlockSpec((N_SLAB, LANES, LANES), lambda b: (0, 0, 0))
    return c2, c3


def _ssm_scratch():
    rows = SSM_SUB * SSM_PITCH
    return [pltpu.VMEM((N_SLAB, rows, LANES), F32), pltpu.VMEM((N_SLAB, rows, LANES), F32)]


def _ssm_fwd(z, pk, dskip):
    L = z.shape[0]
    nb = L // SSM_TB
    ucol = (3 * N_GROUPS * ATTN_WIDTH) // SSM_WIDTH

    def body(u_ref, are_ref, aim_ref, a64re_ref, a64im_ref, bwre_ref, bwim_ref, cwre_ref, cwim_ref, d_ref,
             ypre_ref, yact_ref, sdre_ref, sdim_ref, sre, sim, carry_re, carry_im, ends_re, ends_im,
             seed_re, seed_im):
        @pl.when(pl.program_id(0) == 0)
        def _():
            carry_re[...] = jnp.zeros_like(carry_re)
            carry_im[...] = jnp.zeros_like(carry_im)

        u = u_ref[...]
        _ssm_fill(u, bwre_ref, bwim_ref, sre, sim)
        zero = jnp.zeros((SSM_SUB, LANES), F32)
        for k0 in range(0, N_SLAB, SCAN_GROUP):
            ends = _scan(sre, sim, are_ref, aim_ref, k0, [(zero, zero)] * SCAN_GROUP, reverse=False, store=False)
            for kk in range(SCAN_GROUP):
                ends_re[k0 + kk] = ends[kk][0]
                ends_im[k0 + kk] = ends[kk][1]
            for kk in range(SCAN_GROUP):
                _ssm_seeds(ends_re, ends_im, a64re_ref, a64im_ref, carry_re, carry_im, seed_re, seed_im,
                           k0 + kk, reverse=False)
            init = [(seed_re[k0 + kk], seed_im[k0 + kk]) for kk in range(SCAN_GROUP)]
            _scan(sre, sim, are_ref, aim_ref, k0, init, reverse=False, store=True)
        sdre_ref[...] = seed_re[...]
        sdim_ref[...] = seed_im[...]
        for w in range(N_SLAB // SLABS_PER_WIN):
            acc = jnp.zeros((SSM_TB, LANES), F32)
            for kk in range(SLABS_PER_WIN):
                k = w * SLABS_PER_WIN + kk
                acc += _dot(_slab_rows(sre, k).astype(BF16), cwre_ref[k])
                acc -= _dot(_slab_rows(sim, k).astype(BF16), cwim_ref[k])
            cols = slice(w * LANES, (w + 1) * LANES)
            ypre = acc + d_ref[:, cols] * u[:, cols].astype(F32)
            ypre_ref[:, cols] = ypre
            yact_ref[:, cols] = _gelu(ypre).astype(BF16)

    c2, c3 = _ssm_specs_consts()
    seed_spec = pl.BlockSpec((None, N_SLAB, SSM_SUB, LANES), lambda b: (b, 0, 0, 0))
    small = pltpu.VMEM((N_SLAB, LANES), F32)
    tile = pltpu.VMEM((N_SLAB, SSM_SUB, LANES), F32)
    return _call(
        body, name="ssm_fwd", grid=(nb,),
        in_specs=[pl.BlockSpec((SSM_TB, SSM_WIDTH), lambda b: (b, ucol)), c2, c2, c2, c2, c3, c3, c3, c3,
                  pl.BlockSpec((1, SSM_WIDTH), lambda b: (0, 0))],
        out_specs=[pl.BlockSpec((SSM_TB, SSM_WIDTH), lambda b: (b, 0)),
                   pl.BlockSpec((SSM_TB, SSM_WIDTH), lambda b: (b, 0)), seed_spec, seed_spec],
        out_shape=[_sds((L, SSM_WIDTH), F32), _sds((L, SSM_WIDTH), BF16),
                   _sds((nb, N_SLAB, SSM_SUB, LANES), F32), _sds((nb, N_SLAB, SSM_SUB, LANES), F32)],
        scratch=_ssm_scratch() + [small, small, tile, tile, tile, tile],
        sem=("arbitrary",))(z, pk["a_re"], pk["a_im"], pk["a64_re"], pk["a64_im"],
                            pk["bw_re"].astype(BF16), pk["bw_im"].astype(BF16),
                            pk["cw_re"].astype(BF16), pk["cw_im"].astype(BF16), dskip)


def _combine_weights(l0, l1, l2):
    m = jnp.maximum(jnp.maximum(l0, l1), l2)
    e0, e1, e2 = jnp.exp(l0 - m), jnp.exp(l1 - m), jnp.exp(l2 - m)
    inv = 1.0 / (e0 + e1 + e2)
    return e0 * inv, e1 * inv, e2 * inv


def _mix_fwd(x, z, a, yact, w_ap, w_ga, w_gb, w_out):
    L = x.shape[0]
    cs = D_MODEL // N_CHIPS
    ga_col = (3 * N_GROUPS * ATTN_WIDTH + SSM_WIDTH) // D_MODEL

    def body(x_ref, ga_ref, gs_ref, a_ref, y_ref, wap_ref, wga_ref, wgb_ref, wout_ref,
             x1_ref, aout_ref, sa_ref, sb_ref, mix_ref):
        a = a_ref[...]
        y = y_ref[...]
        for s in range(N_CHIPS):
            cols = slice(s * cs, (s + 1) * cs)
            aout_ref[:, cols] = _dot(a, wap_ref[s]).astype(BF16)
            sa_ref[:, cols] = _dot(y, wga_ref[s]).astype(BF16)
            sb_ref[:, cols] = _dot(y, wgb_ref[s]).astype(BF16)
        s_out = sa_ref[...].astype(F32) * _sigmoid(sb_ref[...].astype(F32))
        mix = (_sigmoid(ga_ref[...].astype(F32)) * aout_ref[...].astype(F32)
               + _sigmoid(gs_ref[...].astype(F32)) * s_out).astype(BF16)
        mix_ref[...] = mix
        x1_ref[...] = x_ref[...] + _dot(mix, wout_ref[...])

    tok = lambda w: pl.BlockSpec((TM_MIX, w), lambda i: (i, 0))
    wsm = pl.BlockSpec((N_CHIPS, ATTN_WIDTH, cs), lambda i: (0, 0, 0))
    return _call(
        body, name="mix_fwd", grid=(L // TM_MIX,),
        in_specs=[tok(D_MODEL), pl.BlockSpec((TM_MIX, D_MODEL), lambda i: (i, ga_col)),
                  pl.BlockSpec((TM_MIX, D_MODEL), lambda i: (i, ga_col + 1))]
                 + [tok(ATTN_WIDTH)] * 2 + [wsm, wsm, wsm, pl.BlockSpec((D_MODEL, D_MODEL), lambda i: (0, 0))],
        out_specs=[tok(D_MODEL), tok(D_MODEL), tok(D_MODEL), tok(D_MODEL), tok(D_MODEL)],
        out_shape=[_sds((L, D_MODEL), F32)] + [_sds((L, D_MODEL), BF16)] * 4,
        sem=("parallel",))(x, z, z, a, yact, w_ap, w_ga, w_gb, w_out.reshape(D_MODEL, D_MODEL))


def _ffn_fwd(x1, g, w_g, w_u, w_d):
    L = x1.shape[0]
    fs = D_FF // N_CHIPS

    def body(x_ref, g_ref, wg_ref, wu_ref, wd_ref, x2_ref, h_ref, gate_ref, up_ref, act_ref, acc):
        s = pl.program_id(1)

        @pl.when(s == 0)
        def _():
            xv = x_ref[...]
            r = lax.rsqrt(jnp.mean(xv * xv, axis=-1, keepdims=True) + EPS)
            h_ref[...] = (xv * r * g_ref[...]).astype(BF16)
            acc[...] = jnp.zeros_like(acc)

        h = h_ref[...]
        gate = _dot(h, wg_ref[...])
        up = _dot(h, wu_ref[...])
        act = (gate * _sigmoid(gate) * up).astype(BF16)
        gate_ref[...] = gate.astype(BF16)
        up_ref[...] = up.astype(BF16)
        act_ref[...] = act
        acc[...] += _dot(act, wd_ref[...])

        @pl.when(s == N_CHIPS - 1)
        def _():
            x2_ref[...] = x_ref[...] + acc[...]

    tok = pl.BlockSpec((TM, D_MODEL), lambda i, s: (i, 0))
    ffs = pl.BlockSpec((None, TM, fs), lambda i, s: (s, i, 0))
    return _call(
        body, name="ffn_fwd", grid=(L // TM, N_CHIPS),
        in_specs=[tok, pl.BlockSpec((1, D_MODEL), lambda i, s: (0, 0)),
                  pl.BlockSpec((None, D_MODEL, fs), lambda i, s: (s, 0, 0)),
                  pl.BlockSpec((None, D_MODEL, fs), lambda i, s: (s, 0, 0)),
                  pl.BlockSpec((None, fs, D_MODEL), lambda i, s: (s, 0, 0))],
        out_specs=[tok, tok, ffs, ffs, ffs],
        out_shape=[_sds((L, D_MODEL), F32), _sds((L, D_MODEL), BF16)] + [_sds((N_CHIPS, L, fs), BF16)] * 3,
        scratch=[pltpu.VMEM((TM, D_MODEL), F32)],
        sem=("parallel", "arbitrary"))(x1, g, w_g, w_u, w_d)


def _loss_head(xl, target):
    L = xl.shape[0]

    def body(x_ref, t_ref, dx_ref, loss_ref, acc):
        i = pl.program_id(0)

        @pl.when(i == 0)
        def _():
            acc[...] = jnp.zeros_like(acc)

        e = x_ref[...] - t_ref[...]
        dx_ref[...] = e * (1.0 / D_MODEL)
        acc[...] += jnp.sum((e * e).reshape(TM // SUBLANES, SUBLANES, D_MODEL), axis=0)

        @pl.when(i == pl.num_programs(0) - 1)
        def _():
            loss_ref[...] = (0.5 / D_MODEL) * jnp.sum(acc[...]).reshape(1, 1)

    tok = pl.BlockSpec((TM, D_MODEL), lambda i: (i, 0))
    return _call(
        body, name="loss_head", grid=(L // TM,), in_specs=[tok, tok],
        out_specs=[tok, pl.BlockSpec((1, 1), lambda i: (0, 0))],
        out_shape=[_sds((L, D_MODEL), F32), _sds((1, 1), F32)],
        scratch=[pltpu.VMEM((SUBLANES, D_MODEL), F32)], sem=("arbitrary",))(xl, target)


def _ssm_pack(lam_re, lam_im, log_dt, b_re, b_im, c_re, c_im):
    dt = jnp.exp(log_dt)[:, None]
    mag = jnp.exp(lam_re * dt)
    ang = lam_im * dt
    ar = mag * jnp.cos(ang)
    ai = mag * jnp.sin(ang)
    nr = ar - 1.0
    ni = ai
    den = lam_re * lam_re + lam_im * lam_im
    cr = ((nr * lam_re + ni * lam_im) / den)[..., None]
    ci = ((ni * lam_re - nr * lam_im) / den)[..., None]
    bbr = cr * b_re - ci * b_im
    bbi = cr * b_im + ci * b_re
    eye = jnp.eye(SSM_GROUPS, dtype=F32)
    n_state = N_SLAB * LANES

    def b_windows(bb):
        full = jnp.einsum('gpc,gh->gchp', bb, eye).reshape(SSM_WIDTH, n_state)
        return jnp.stack([full[(k // SLABS_PER_WIN) * LANES:(k // SLABS_PER_WIN + 1) * LANES,
                               k * LANES:(k + 1) * LANES] for k in range(N_SLAB)])

    def c_windows(cc):
        full = jnp.einsum('gcp,gh->hpgc', cc, eye).reshape(n_state, SSM_WIDTH)
        return jnp.stack([full[k * LANES:(k + 1) * LANES,
                               (k // SLABS_PER_WIN) * LANES:(k // SLABS_PER_WIN + 1) * LANES]
                          for k in range(N_SLAB)])

    pr, pi = ar, ai
    for _ in range(int(math.log2(SSM_TC))):
        pr, pi = pr * pr - pi * pi, 2.0 * pr * pi
    return dict(a_re=ar.reshape(N_SLAB, LANES), a_im=ai.reshape(N_SLAB, LANES),
                a64_re=pr.reshape(N_SLAB, LANES), a64_im=pi.reshape(N_SLAB, LANES),
                bw_re=b_windows(bbr), bw_im=b_windows(bbi), cw_re=c_windows(c_re), cw_im=c_windows(c_im))


def _layer_fwd(x, p):
    z, h = _in_proj_fwd(x, p["g_mix"], p["w_in"])
    qkv = _qkv_prep(z, jnp.tile(p["g_q"], (1, N_HEADS)), jnp.tile(p["g_k"], (1, N_HEADS)))
    os_, ls_ = [], []
    for gi in range(N_GROUPS):
        o, l = _attn_fwd(*qkv[gi], gi)
        os_.append(o)
        ls_.append(l)
    a = _combine_fwd(os_, ls_)
    pk = _ssm_pack(p["lambda_re"], p["lambda_im"], p["log_dt"], p["b_re"], p["b_im"], p["c_re"], p["c_im"])
    ypre, yact, sd_re, sd_im = _ssm_fwd(z, pk, p["d_skip"])
    x1, aout, sa, sb, mix = _mix_fwd(x, z, a, yact, p["w_attn_proj"], p["w_glu_a"], p["w_glu_b"], p["w_out"])
    x2, h2, gate, up, act = _ffn_fwd(x1, p["g_ffn"], p["w_ffn_gate"], p["w_ffn_up"], p["w_ffn_down"])
    saved = dict(x=x, z=z, h=h, qkv=qkv, os=os_, ls=ls_, pk=pk, ypre=ypre, yact=yact, sd_re=sd_re, sd_im=sd_im,
                 x1=x1, a=a, aout=aout, sa=sa, sb=sb, mix=mix, h2=h2, gate=gate, up=up, act=act)
    return x2, saved


def _rms_bwd(xv, g, dh):
    r = lax.rsqrt(jnp.mean(xv * xv, axis=-1, keepdims=True) + EPS)
    xn = xv * r
    dxn = dh * g
    dx = r * (dxn - xn * jnp.mean(dxn * xn, axis=-1, keepdims=True))
    dg = jnp.sum((dh * xn).reshape(xv.shape[0] // SUBLANES, SUBLANES, xv.shape[1]), axis=0)
    return dx, dg


def _ffn_bwd(dx2, x1, g, gate, up, w_g, w_u, w_d):
    L = x1.shape[0]
    fs = D_FF // N_CHIPS

    def body(dx_ref, x_ref, g_ref, gate_ref, up_ref, wg_ref, wu_ref, wd_ref,
             dx1_ref, dgate_ref, dup_ref, dg_ref, acc, dgacc):
        i, s = pl.program_id(0), pl.program_id(1)

        @pl.when(s == 0)
        def _():
            acc[...] = jnp.zeros_like(acc)

        @pl.when(jnp.logical_and(i == 0, s == 0))
        def _():
            dgacc[...] = jnp.zeros_like(dgacc)

        dact = _dot_nt(dx_ref[...].astype(BF16), wd_ref[...])
        gt = gate_ref[...].astype(F32)
        sg = _sigmoid(gt)
        dgate = (dact * up_ref[...].astype(F32) * (sg * (1.0 + gt * (1.0 - sg)))).astype(BF16)
        dup = (dact * gt * sg).astype(BF16)
        dgate_ref[...] = dgate
        dup_ref[...] = dup
        acc[...] += _dot_nt(dgate, wg_ref[...]) + _dot_nt(dup, wu_ref[...])

        @pl.when(s == N_CHIPS - 1)
        def _():
            dx, dg = _rms_bwd(x_ref[...], g_ref[...], acc[...])
            dx1_ref[...] = dx_ref[...] + dx
            dgacc[...] += dg

        @pl.when(jnp.logical_and(i == pl.num_programs(0) - 1, s == N_CHIPS - 1))
        def _():
            dg_ref[...] = jnp.sum(dgacc[...], axis=0, keepdims=True)

    tok = pl.BlockSpec((TM, D_MODEL), lambda i, s: (i, 0))
    ffs = pl.BlockSpec((None, TM, fs), lambda i, s: (s, i, 0))
    vec = pl.BlockSpec((1, D_MODEL), lambda i, s: (0, 0))
    return _call(
        body, name="ffn_bwd", grid=(L // TM, N_CHIPS),
        in_specs=[tok, tok, vec, ffs, ffs,
                  pl.BlockSpec((None, D_MODEL, fs), lambda i, s: (s, 0, 0)),
                  pl.BlockSpec((None, D_MODEL, fs), lambda i, s: (s, 0, 0)),
                  pl.BlockSpec((None, fs, D_MODEL), lambda i, s: (s, 0, 0))],
        out_specs=[tok, ffs, ffs, vec],
        out_shape=[_sds((L, D_MODEL), F32), _sds((N_CHIPS, L, fs), BF16), _sds((N_CHIPS, L, fs), BF16),
                   _sds((1, D_MODEL), F32)],
        scratch=[pltpu.VMEM((TM, D_MODEL), F32), pltpu.VMEM((SUBLANES, D_MODEL), F32)],
        sem=("arbitrary", "arbitrary"))(dx2, x1, g, gate, up, w_g, w_u, w_d)


def _wgrad(a, b, *, name, grid_kn, a_spec, b_spec, out_shape, out_spec):
    L = a.shape[-2]
    nl = L // TM

    def body(a_ref, b_ref, o_ref):
        @pl.when(pl.program_id(2) == 0)
        def _():
            o_ref[...] = jnp.zeros_like(o_ref)
        o_ref[...] += _dot_tn(a_ref[...].astype(BF16), b_ref[...].astype(BF16))

    return _call(body, name=name, grid=(*grid_kn, nl), in_specs=[a_spec, b_spec], out_specs=out_spec,
                 out_shape=out_shape, sem=("parallel", "parallel", "arbitrary"))(a, b)


def _wgrad_cols(a, b, name):
    K, N = a.shape[1], b.shape[1]
    ns = N // N_CHIPS
    if N * K * 4 <= 4 * 1024 * 1024:
        L = a.shape[0]

        def body(a_ref, b_ref, o_ref):
            @pl.when(pl.program_id(0) == 0)
            def _():
                o_ref[...] = jnp.zeros_like(o_ref)
            av = a_ref[...].astype(BF16)
            for s in range(N_CHIPS):
                o_ref[s] += _dot_tn(av, b_ref[:, s * ns:(s + 1) * ns].astype(BF16))

        return _call(body, name=name, grid=(L // TM,),
                     in_specs=[pl.BlockSpec((TM, K), lambda t: (t, 0)), pl.BlockSpec((TM, N), lambda t: (t, 0))],
                     out_specs=pl.BlockSpec((N_CHIPS, K, ns), lambda t: (0, 0, 0)),
                     out_shape=_sds((N_CHIPS, K, ns), F32), sem=("arbitrary",))(a, b)
    tn = ns // 2 if ns % (2 * LANES) == 0 else ns
    nj = ns // tn
    return _wgrad(a, b, name=name, grid_kn=(1, N_CHIPS * nj),
                  a_spec=pl.BlockSpec((TM, K), lambda i, j, t: (t, 0)),
                  b_spec=pl.BlockSpec((TM, tn), lambda i, j, t: (t, j)),
                  out_shape=_sds((N_CHIPS, K, ns), F32),
                  out_spec=pl.BlockSpec((None, K, tn), lambda i, j, t: (j // nj, 0, j % nj)))


def _wgrad_full(a, b, name):
    K, N = a.shape[1], b.shape[1]
    return _wgrad(a, b, name=name, grid_kn=(1, 1),
                  a_spec=pl.BlockSpec((TM, K), lambda i, j, t: (t, 0)),
                  b_spec=pl.BlockSpec((TM, N), lambda i, j, t: (t, 0)),
                  out_shape=_sds((K, N), F32), out_spec=pl.BlockSpec((K, N), lambda i, j, t: (0, 0)))


def _wgrad_ff_cols(a, b, name):
    K, fs = a.shape[1], b.shape[2]
    return _wgrad(a, b, name=name, grid_kn=(1, N_CHIPS),
                  a_spec=pl.BlockSpec((TM, K), lambda i, j, t: (t, 0)),
                  b_spec=pl.BlockSpec((None, TM, fs), lambda i, j, t: (j, t, 0)),
                  out_shape=_sds((N_CHIPS, K, fs), F32),
                  out_spec=pl.BlockSpec((None, K, fs), lambda i, j, t: (j, 0, 0)))


def _wgrad_ff_rows(a, b, name):
    fs, N = a.shape[2], b.shape[1]
    return _wgrad(a, b, name=name, grid_kn=(N_CHIPS, 1),
                  a_spec=pl.BlockSpec((None, TM, fs), lambda i, j, t: (i, t, 0)),
                  b_spec=pl.BlockSpec((TM, N), lambda i, j, t: (t, 0)),
                  out_shape=_sds((N_CHIPS, fs, N), F32),
                  out_spec=pl.BlockSpec((None, fs, N), lambda i, j, t: (i, 0, 0)))


def _mix_bwd(dx, z, aout, sa, sb, ypre, w_ap, w_ga, w_gb, w_out):
    L = dx.shape[0]
    cs = D_MODEL // N_CHIPS
    ga_col = (3 * N_GROUPS * ATTN_WIDTH + SSM_WIDTH) // D_MODEL

    def body(dx_ref, ga_ref, gs_ref, aout_ref, sa_ref, sb_ref, ypre_ref, wap_ref, wga_ref, wgb_ref, wout_ref,
             dgates_ref, da_ref, gy_ref, daout_ref, dsa_ref, dsb_ref):
        dmix = _dot_nt(dx_ref[...].astype(BF16), wout_ref[...])
        sig_a = _sigmoid(ga_ref[...].astype(F32))
        sig_s = _sigmoid(gs_ref[...].astype(F32))
        a_out = aout_ref[...].astype(F32)
        s_a = sa_ref[...].astype(F32)
        sig_b = _sigmoid(sb_ref[...].astype(F32))
        s_out = s_a * sig_b
        daout = (dmix * sig_a).astype(BF16)
        daout_ref[...] = daout
        dgates_ref[:, :D_MODEL] = (dmix * a_out * sig_a * (1.0 - sig_a)).astype(BF16)
        dgates_ref[:, D_MODEL:] = (dmix * s_out * sig_s * (1.0 - sig_s)).astype(BF16)
        ds_out = dmix * sig_s
        dsa = (ds_out * sig_b).astype(BF16)
        dsb = (ds_out * s_a * sig_b * (1.0 - sig_b)).astype(BF16)
        dsa_ref[...] = dsa
        dsb_ref[...] = dsb
        da = jnp.zeros((TM_MIX, ATTN_WIDTH), F32)
        dy = jnp.zeros((TM_MIX, SSM_WIDTH), F32)
        for s in range(N_CHIPS):
            cols = slice(s * cs, (s + 1) * cs)
            da += _dot_nt(daout[:, cols], wap_ref[s])
            dy += _dot_nt(dsa[:, cols], wga_ref[s]) + _dot_nt(dsb[:, cols], wgb_ref[s])
        gy_ref[...] = dy * _gelu_grad(ypre_ref[...])
        da_ref[...] = da

    tok = lambda w: pl.BlockSpec((TM_MIX, w), lambda i: (i, 0))
    wsm = pl.BlockSpec((N_CHIPS, ATTN_WIDTH, cs), lambda i: (0, 0, 0))
    return _call(
        body, name="mix_bwd", grid=(L // TM_MIX,),
        in_specs=[tok(D_MODEL), pl.BlockSpec((TM_MIX, D_MODEL), lambda i: (i, ga_col)),
                  pl.BlockSpec((TM_MIX, D_MODEL), lambda i: (i, ga_col + 1)),
                  tok(D_MODEL), tok(D_MODEL), tok(D_MODEL), tok(SSM_WIDTH),
                  wsm, wsm, wsm, pl.BlockSpec((D_MODEL, D_MODEL), lambda i: (0, 0))],
        out_specs=[tok(2 * D_MODEL), tok(ATTN_WIDTH), tok(SSM_WIDTH)] + [tok(D_MODEL)] * 3,
        out_shape=[_sds((L, 2 * D_MODEL), BF16), _sds((L, ATTN_WIDTH), F32), _sds((L, SSM_WIDTH), F32)]
                  + [_sds((L, D_MODEL), BF16)] * 3,
        sem=("parallel",))(dx, z, z, aout, sa, sb, ypre, w_ap, w_ga, w_gb, w_out.reshape(D_MODEL, D_MODEL))


def _combine_bwd(da, os_, ls_):
    L = da.shape[0]

    def body(da_ref, o0, o1, o2, l0, l1, l2, hs_ref, p1_ref, p2_ref, pt1_ref, pt2_ref,
             do0, do1, do2, c0, c1, c2):
        o_tok, l_tok = _to_token_order((o0[...], o1[...], o2[...]), (l0[...], l1[...], l2[...]),
                                       (None, pt1_ref[...], pt2_ref[...]))
        w = _combine_weights(*l_tok)
        dav = da_ref[...]
        hs = hs_ref[...]
        tbar = sum(wg * _head_sum(dav * og, hs) for wg, og in zip(w, o_tok))
        for wg, pm, do_ref, c_ref in zip(w, (None, p1_ref[...], p2_ref[...]), (do0, do1, do2), (c0, c1, c2)):
            dog = (wg * dav).astype(BF16)
            cg = -wg * tbar
            do_ref[...] = dog if pm is None else _dot(pm, dog).astype(BF16)
            c_ref[...] = cg if pm is None else _permute(pm, cg)

    tile = pl.BlockSpec((DL_TILE, ATTN_WIDTH), lambda i: (i, 0))
    mat = pl.BlockSpec((DL_TILE, DL_TILE), lambda i: (0, 0))
    p1, p2 = _perm_matrix(ATTN_PATTERN[1][1]), _perm_matrix(ATTN_PATTERN[2][1])
    outs = _call(body, name="combine_bwd", grid=(L // DL_TILE,), in_specs=[tile] * 7 + [mat] * 5,
                 out_specs=[tile] * 6,
                 out_shape=[_sds((L, ATTN_WIDTH), BF16)] * 3 + [_sds((L, ATTN_WIDTH), F32)] * 3,
                 sem=("parallel",))(da, *os_, *ls_, _head_sum_matrix(), p1, p2, p1.T, p2.T)
    return outs[:3], outs[3:]


def _attn_bwd(qs, ks, v, do, l, c, gi):
    L = qs.shape[0]
    _, d = ATTN_PATTERN[gi]
    nb = L // (d * BLK)

    def body(q0_ref, q1_ref, k_ref, v_ref, do0_ref, do1_ref, l0_ref, l1_ref, c0_ref, c1_ref,
             dq_ref, dk_ref, dv_ref, carry):
        n = pl.program_id(1)

        @pl.when(n == 0)
        def _():
            carry[...] = jnp.zeros_like(carry)

        qi = lax.broadcasted_iota(jnp.int32, (2 * BLK, BLK), 0)
        kj = lax.broadcasted_iota(jnp.int32, (2 * BLK, BLK), 1)
        first = qi < BLK
        mask = jnp.logical_and(jnp.where(first, qi, kj) >= jnp.where(first, kj, qi - BLK),
                               qi < jnp.where(n < nb - 1, 2 * BLK, BLK))
        q2 = jnp.concatenate([_dl_read(q0_ref), _dl_read(q1_ref)], axis=0)
        do2 = jnp.concatenate([_dl_read(do0_ref), _dl_read(do1_ref)], axis=0)
        l2 = jnp.concatenate([_dl_read(l0_ref), _dl_read(l1_ref)], axis=0)
        c2 = jnp.concatenate([_dl_read(c0_ref), _dl_read(c1_ref)], axis=0)
        k = _dl_read(k_ref)
        v_ = _dl_read(v_ref)
        dq_parts, dk_parts, dv_parts = [], [], []
        for hp in range(N_HEADS // 2):
            ls = slice(hp * LANES, (hp + 1) * LANES)
            qp, dop, kp_, vp_ = q2[:, ls], do2[:, ls], k[:, ls], v_[:, ls]
            dq2 = jnp.zeros((2 * BLK, LANES), F32)
            dkp = jnp.zeros((BLK, LANES), F32)
            dvp = jnp.zeros((BLK, LANES), F32)
            for e, he in enumerate(_pair_masks()):
                col = slice(hp * LANES + e * HEAD_DIM, hp * LANES + e * HEAD_DIM + 1)
                ke = jnp.where(he, kp_, 0)
                p = jnp.where(mask, jnp.exp(_dot_nt(qp, ke) - l2[:, col]), 0.0)
                ds = (p * (_dot_nt(dop, jnp.where(he, vp_, 0)) + c2[:, col])).astype(BF16)
                dvp += _dot_tn(p.astype(BF16), jnp.where(he, dop, 0))
                dkp += _dot_tn(ds, jnp.where(he, qp, 0))
                dq2 += _dot(ds, ke)
            dq_parts.append((dq2[:BLK] + carry[:, ls]).astype(BF16))
            carry[:, ls] = dq2[BLK:]
            dk_parts.append(dkp.astype(BF16))
            dv_parts.append(dvp.astype(BF16))
        _dl_write(dq_ref, jnp.concatenate(dq_parts, axis=1))
        _dl_write(dk_ref, jnp.concatenate(dk_parts, axis=1))
        _dl_write(dv_ref, jnp.concatenate(dv_parts, axis=1))

    cur = _dl_spec(d, ATTN_WIDTH, lambda n: n)
    nxt = _dl_spec(d, ATTN_WIDTH, lambda n: jnp.minimum(n + 1, nb - 1))
    view = lambda t: _dl_view(t, d)
    outs = _call(
        body, name=f"attn_bwd_g{gi}", grid=(d, nb),
        in_specs=[cur, nxt, cur, cur, cur, nxt, cur, nxt, cur, nxt], out_specs=[cur, cur, cur],
        out_shape=[_sds(view(qs).shape, BF16)] * 3, scratch=[pltpu.VMEM((BLK, ATTN_WIDTH), F32)],
        sem=("parallel", "arbitrary"))(view(qs), view(qs), view(ks), view(v), view(do), view(do), view(l), view(l),
                                       view(c), view(c))
    return [t.reshape(L, ATTN_WIDTH) for t in outs]


def _qkv_post(z, dqkv, du, dgates, gq_t, gk_t):
    L = z.shape[0]
    qkv_w = N_GROUPS * ATTN_WIDTH

    def body(zq_ref, zk_ref, gq_ref, gk_ref, hs_ref, pt1_ref, pt2_ref, du_ref, dgates_ref, *rest):
        dl_refs, (dz_ref, dgq_ref, dgk_ref) = rest[:9], rest[9:]

        @pl.when(pl.program_id(0) == 0)
        def _():
            dgq_ref[...] = jnp.zeros_like(dgq_ref)
            dgk_ref[...] = jnp.zeros_like(dgk_ref)

        hs = hs_ref[...]
        pts = (None, pt1_ref[...], pt2_ref[...])

        def rows8(t):
            return jnp.sum(t.reshape(DL_TILE // SUBLANES, SUBLANES, ATTN_WIDTH), axis=0)

        def norm_bwd(x, gain, dn):
            r = lax.rsqrt(_head_sum(x * x, hs) * (1.0 / HEAD_DIM) + EPS)
            xh = x * r
            dh = dn * gain
            return r * (dh - xh * (_head_sum(dh * xh, hs) * (1.0 / HEAD_DIM))), rows8(dn * xh)

        for g in range(N_GROUPS):
            tok = [t[...].astype(F32) if pts[g] is None else _dot(pts[g], t[...]) for t in dl_refs[3 * g:3 * g + 3]]
            cols = slice(g * ATTN_WIDTH, (g + 1) * ATTN_WIDTH)
            dq, pq = norm_bwd(zq_ref[:, cols].astype(F32), gq_ref[...] * SCALE, tok[0])
            dk, pk_ = norm_bwd(zk_ref[:, cols].astype(F32), gk_ref[...], tok[1])
            dgq_ref[...] += pq * SCALE
            dgk_ref[...] += pk_
            dz_ref[:, cols] = dq.astype(BF16)
            dz_ref[:, qkv_w + g * ATTN_WIDTH:qkv_w + (g + 1) * ATTN_WIDTH] = dk.astype(BF16)
            dz_ref[:, 2 * qkv_w + g * ATTN_WIDTH:2 * qkv_w + (g + 1) * ATTN_WIDTH] = tok[2].astype(BF16)
        dz_ref[:, 3 * qkv_w:3 * qkv_w + SSM_WIDTH] = du_ref[...]
        dz_ref[:, 3 * qkv_w + SSM_WIDTH:] = dgates_ref[...]

    tile = lambda w: pl.BlockSpec((DL_TILE, w), lambda i: (i, 0))
    mat = pl.BlockSpec((DL_TILE, DL_TILE), lambda i: (0, 0))
    vec = pl.BlockSpec((1, ATTN_WIDTH), lambda i: (0, 0))
    acc = pl.BlockSpec((SUBLANES, ATTN_WIDTH), lambda i: (0, 0))
    flat = [t for grp in dqkv for t in grp]
    return _call(
        body, name="qkv_post", grid=(L // DL_TILE,),
        in_specs=[tile(qkv_w), pl.BlockSpec((DL_TILE, qkv_w), lambda i: (i, 1)), vec, vec, mat, mat, mat,
                  tile(SSM_WIDTH), tile(2 * D_MODEL)] + [tile(ATTN_WIDTH)] * 9,
        out_specs=[tile(IN_COLS), acc, acc],
        out_shape=[_sds((L, IN_COLS), BF16), _sds((SUBLANES, ATTN_WIDTH), F32), _sds((SUBLANES, ATTN_WIDTH), F32)],
        sem=("arbitrary",))(z, z, gq_t, gk_t, _head_sum_matrix(), _perm_matrix(ATTN_PATTERN[1][1]).T,
                            _perm_matrix(ATTN_PATTERN[2][1]).T, du, dgates, *flat)


def _scan_rev_grad(sre, sim, rre, rim, are_ref, aim_ref, k0, init, seed_re, seed_im):
    ar = [_bcast(are_ref, k0 + kk) for kk in range(SCAN_GROUP)]
    ai = [-_bcast(aim_ref, k0 + kk) for kk in range(SCAN_GROUP)]

    def update(i, xprev, carry):
        out = []
        for kk in range(SCAN_GROUP):
            k = k0 + kk
            lr, li, dr, di = carry[4 * kk:4 * kk + 4]
            nr = ar[kk] * lr - ai[kk] * li + rre[k, _rows(i), :]
            ni = ar[kk] * li + ai[kk] * lr + rim[k, _rows(i), :]
            rre[k, _rows(i), :] = nr
            rim[k, _rows(i), :] = ni
            xr, xi = xprev(k)
            out += [nr, ni, dr + xr * nr + xi * ni, di + xr * ni - xi * nr]
        return tuple(out)

    def step(t, carry):
        i = SSM_TC - 1 - t
        return update(i, lambda k: (sre[k, _rows(i - 1), :], sim[k, _rows(i - 1), :]), carry)

    zero = jnp.zeros((SSM_SUB, LANES), F32)
    flat = []
    for re, im in init:
        flat += [re, im, zero, zero]
    res = lax.fori_loop(0, SSM_TC - 1, step, tuple(flat))
    res = update(0, lambda k: (seed_re[k], seed_im[k]), res)
    return [(res[4 * kk + 2], res[4 * kk + 3]) for kk in range(SCAN_GROUP)]


def _ssm_bwd(z, gy, pk, dskip, sd_re, sd_im):
    L = z.shape[0]
    nb = L // SSM_TB
    ucol = (3 * N_GROUPS * ATTN_WIDTH) // SSM_WIDTH
    nwin = N_SLAB // SLABS_PER_WIN

    def body(u_ref, gy_ref, are_ref, aim_ref, a64re_ref, a64im_ref, bwre_ref, bwim_ref, cwre_ref, cwim_ref, d_ref,
             sdre_ref, sdim_ref,
             du_ref, dare_ref, daim_ref, dbre_ref, dbim_ref, dcre_ref, dcim_ref, dd_ref,
             sre, sim, rre, rim, carry_re, carry_im, ends_re, ends_im, seed_re, seed_im):
        @pl.when(pl.program_id(0) == 0)
        def _():
            carry_re[...] = jnp.zeros_like(carry_re)
            carry_im[...] = jnp.zeros_like(carry_im)
            for ref in (dare_ref, daim_ref, dbre_ref, dbim_ref, dcre_ref, dcim_ref, dd_ref):
                ref[...] = jnp.zeros_like(ref)

        u = u_ref[...]
        gyv = gy_ref[...]
        gyb = gyv.astype(BF16)
        _ssm_fill(u, bwre_ref, bwim_ref, sre, sim)
        for k in range(N_SLAB):
            gw = gyb[:, (k // SLABS_PER_WIN) * LANES:(k // SLABS_PER_WIN + 1) * LANES]
            gr = _dot_nt(gw, cwre_ref[k])
            gi_ = -_dot_nt(gw, cwim_ref[k])
            for j in range(SSM_SUB):
                rre[k, j * SSM_PITCH:j * SSM_PITCH + SSM_TC, :] = gr[j * SSM_TC:(j + 1) * SSM_TC, :]
                rim[k, j * SSM_PITCH:j * SSM_PITCH + SSM_TC, :] = gi_[j * SSM_TC:(j + 1) * SSM_TC, :]
        zero = jnp.zeros((SSM_SUB, LANES), F32)
        for k0 in range(0, N_SLAB, SCAN_GROUP):
            grp = range(k0, k0 + SCAN_GROUP)
            _scan(sre, sim, are_ref, aim_ref, k0, [(sdre_ref[k], sdim_ref[k]) for k in grp],
                  reverse=False, store=True)
            ends = _scan(rre, rim, are_ref, aim_ref, k0, [(zero, zero)] * SCAN_GROUP, reverse=True, store=False,
                         sign=-1.0)
            for kk, k in enumerate(grp):
                ends_re[k] = ends[kk][0]
                ends_im[k] = ends[kk][1]
            for k in grp:
                _ssm_seeds(ends_re, ends_im, a64re_ref, a64im_ref, carry_re, carry_im, seed_re, seed_im, k,
                           reverse=True, sign=-1.0)
            das = _scan_rev_grad(sre, sim, rre, rim, are_ref, aim_ref, k0,
                                 [(seed_re[k], seed_im[k]) for k in grp], sdre_ref, sdim_ref)
            for kk, k in enumerate(grp):
                dare_ref[k] += das[kk][0]
                daim_ref[k] += das[kk][1]
        for w in range(nwin):
            cols = slice(w * LANES, (w + 1) * LANES)
            uw = u[:, cols]
            gw = gyb[:, cols]
            acc = gyv[:, cols] * d_ref[:, cols]
            for kk in range(SLABS_PER_WIN):
                k = w * SLABS_PER_WIN + kk
                lr = _slab_rows(rre, k).astype(BF16)
                li = _slab_rows(rim, k).astype(BF16)
                acc += _dot_nt(lr, bwre_ref[k]) + _dot_nt(li, bwim_ref[k])
                dbre_ref[k] += _dot_tn(uw, lr)
                dbim_ref[k] += _dot_tn(uw, li)
                dcre_ref[k] += _dot_tn(_slab_rows(sre, k).astype(BF16), gw)
                dcim_ref[k] -= _dot_tn(_slab_rows(sim, k).astype(BF16), gw)
            du_ref[:, cols] = acc.astype(BF16)
        dd_ref[...] += jnp.sum((gyv * u.astype(F32)).reshape(SSM_TB // SUBLANES, SUBLANES, SSM_WIDTH), axis=0)

    c2, c3 = _ssm_specs_consts()
    rev = lambda b: nb - 1 - b
    seed_spec = pl.BlockSpec((None, N_SLAB, SSM_SUB, LANES), lambda b: (rev(b), 0, 0, 0))
    tile_out = pl.BlockSpec((N_SLAB, SSM_SUB, LANES), lambda b: (0, 0, 0))
    small = pltpu.VMEM((N_SLAB, LANES), F32)
    tile = pltpu.VMEM((N_SLAB, SSM_SUB, LANES), F32)
    return _call(
        body, name="ssm_bwd", grid=(nb,),
        in_specs=[pl.BlockSpec((SSM_TB, SSM_WIDTH), lambda b: (rev(b), ucol)),
                  pl.BlockSpec((SSM_TB, SSM_WIDTH), lambda b: (rev(b), 0)),
                  c2, c2, c2, c2, c3, c3, c3, c3, pl.BlockSpec((1, SSM_WIDTH), lambda b: (0, 0)),
                  seed_spec, seed_spec],
        out_specs=[pl.BlockSpec((SSM_TB, SSM_WIDTH), lambda b: (rev(b), 0)), tile_out, tile_out, c3, c3, c3, c3,
                   pl.BlockSpec((SUBLANES, SSM_WIDTH), lambda b: (0, 0))],
        out_shape=[_sds((L, SSM_WIDTH), BF16), _sds((N_SLAB, SSM_SUB, LANES), F32),
                   _sds((N_SLAB, SSM_SUB, LANES), F32)] + [_sds((N_SLAB, LANES, LANES), F32)] * 4
                  + [_sds((SUBLANES, SSM_WIDTH), F32)],
        scratch=_ssm_scratch() + _ssm_scratch() + [small, small, tile, tile, tile, tile],
        sem=("arbitrary",))(z, gy, pk["a_re"], pk["a_im"], pk["a64_re"], pk["a64_im"],
                            pk["bw_re"].astype(BF16), pk["bw_im"].astype(BF16),
                            pk["cw_re"].astype(BF16), pk["cw_im"].astype(BF16), dskip, sd_re, sd_im)


def _in_proj_bwd(dz, w, x, g, dres):
    L = x.shape[0]
    ns = w.shape[2]
    tn = ns // 2
    nj = ns // tn
    nt = N_CHIPS * nj

    def body(dz_ref, w_ref, x_ref, g_ref, dres_ref, dx_ref, dg_ref, acc, dgacc):
        i, j = pl.program_id(0), pl.program_id(1)

        @pl.when(j == 0)
        def _():
            acc[...] = jnp.zeros_like(acc)

        @pl.when(jnp.logical_and(i == 0, j == 0))
        def _():
            dgacc[...] = jnp.zeros_like(dgacc)

        acc[...] += _dot_nt(dz_ref[...], w_ref[...])

        @pl.when(j == nt - 1)
        def _():
            dx, dg = _rms_bwd(x_ref[...], g_ref[...], acc[...])
            dx_ref[...] = dres_ref[...] + dx
            dgacc[...] += dg

        @pl.when(jnp.logical_and(i == pl.num_programs(0) - 1, j == nt - 1))
        def _():
            dg_ref[...] = jnp.sum(dgacc[...], axis=0, keepdims=True)

    tok = pl.BlockSpec((TM, D_MODEL), lambda i, j: (i, 0))
    vec = pl.BlockSpec((1, D_MODEL), lambda i, j: (0, 0))
    return _call(
        body, name="in_proj_bwd", grid=(L // TM, nt),
        in_specs=[pl.BlockSpec((TM, tn), lambda i, j: (i, j)),
                  pl.BlockSpec((None, D_MODEL, tn), lambda i, j: (j // nj, 0, j % nj)), tok, vec, tok],
        out_specs=[tok, vec],
        out_shape=[_sds((L, D_MODEL), F32), _sds((1, D_MODEL), F32)],
        scratch=[pltpu.VMEM((TM, D_MODEL), F32), pltpu.VMEM((SUBLANES, D_MODEL), F32)],
        sem=("arbitrary", "arbitrary"))(dz, w, x, g, dres)


SSM_PARAMS = ("lambda_re", "lambda_im", "log_dt", "b_re", "b_im", "c_re", "c_im")


def _layer_bwd(dx2, sv, p):
    g = {}
    dx1, dgate, dup, g["g_ffn"] = _ffn_bwd(dx2, sv["x1"], p["g_ffn"], sv["gate"], sv["up"],
                                            p["w_ffn_gate"], p["w_ffn_up"], p["w_ffn_down"])
    g["w_ffn_gate"] = _wgrad_ff_cols(sv["h2"], dgate, "wgrad_ffn_gate")
    g["w_ffn_up"] = _wgrad_ff_cols(sv["h2"], dup, "wgrad_ffn_up")
    g["w_ffn_down"] = _wgrad_ff_rows(sv["act"], dx2, "wgrad_ffn_down")

    dgates, da, gy, daout, dsa, dsb = _mix_bwd(dx1, sv["z"], sv["aout"], sv["sa"], sv["sb"], sv["ypre"],
                                               p["w_attn_proj"], p["w_glu_a"], p["w_glu_b"], p["w_out"])
    g["w_out"] = _wgrad_full(sv["mix"], dx1, "wgrad_out").reshape(N_CHIPS, D_MODEL // N_CHIPS, D_MODEL)
    g["w_attn_proj"] = _wgrad_cols(sv["a"], daout, "wgrad_attn_proj")
    g["w_glu_a"] = _wgrad_cols(sv["yact"], dsa, "wgrad_glu_a")
    g["w_glu_b"] = _wgrad_cols(sv["yact"], dsb, "wgrad_glu_b")

    du, da_re, da_im, dbw_re, dbw_im, dcw_re, dcw_im, dd = _ssm_bwd(
        sv["z"], gy, sv["pk"], p["d_skip"], sv["sd_re"], sv["sd_im"])
    g["d_skip"] = jnp.sum(dd, axis=0, keepdims=True)
    _, pull = jax.vjp(_ssm_pack, *[p[n] for n in SSM_PARAMS])
    zeros = jnp.zeros((N_SLAB, LANES), F32)
    ct = dict(a_re=jnp.sum(da_re, axis=1), a_im=jnp.sum(da_im, axis=1), a64_re=zeros, a64_im=zeros,
              bw_re=dbw_re, bw_im=dbw_im, cw_re=dcw_re, cw_im=dcw_im)
    for n, v in zip(SSM_PARAMS, pull(ct)):
        g[n] = v

    dos, cs = _combine_bwd(da, sv["os"], sv["ls"])
    dqkv = [_attn_bwd(*sv["qkv"][gi], dos[gi], sv["ls"][gi], cs[gi], gi) for gi in range(N_GROUPS)]
    dz, gq8, gk8 = _qkv_post(sv["z"], dqkv, du, dgates, jnp.tile(p["g_q"], (1, N_HEADS)),
                             jnp.tile(p["g_k"], (1, N_HEADS)))
    g["g_q"] = jnp.sum(gq8.reshape(SUBLANES * N_HEADS, HEAD_DIM), axis=0, keepdims=True)
    g["g_k"] = jnp.sum(gk8.reshape(SUBLANES * N_HEADS, HEAD_DIM), axis=0, keepdims=True)
    g["w_in"] = _wgrad_cols(sv["h"], dz, "wgrad_in")
    dx, g["g_mix"] = _in_proj_bwd(dz, p["w_in"], sv["x"], p["g_mix"], dx1)
    return dx, g


ANY = pl.BlockSpec(memory_space=pl.ANY)


def _place():
    x, y, c = lax.axis_index("x"), lax.axis_index("y"), lax.axis_index("c")
    others = [(1 - x, y), (x, 1 - y), (1 - x, 1 - y)]
    return x, y, c, others


def _half(ref, hc):
    rows = ref.shape[-2] // 2
    idx = (slice(None),) * (len(ref.shape) - 2) + (pl.ds(hc * rows, rows), slice(None))
    return ref.at[idx]


def _comm_call(body, name, ins, out_shapes, n_remote, aliases=None):
    scratch = [pltpu.SemaphoreType.DMA((n_remote,)), pltpu.SemaphoreType.DMA((n_remote,))]
    return pl.pallas_call(
        body, name=name, in_specs=[ANY] * len(ins), out_specs=[ANY] * len(out_shapes), out_shape=out_shapes,
        scratch_shapes=scratch, input_output_aliases=aliases or {})(*ins)


def _cast_place(w, l, chip_idx):
    _, R, C = w.shape
    tr = R // 2

    def body(me_ref, w_ref, o_ref):
        o_ref[...] = w_ref[...].astype(BF16)

    return _call(body, name=f"cast_place_l{l}", grid=(R // tr,), prefetch=1,
                 in_specs=[pl.BlockSpec((None, tr, C), lambda i, me_ref: (l, i, 0))],
                 out_specs=pl.BlockSpec((None, tr, C), lambda i, me_ref: (me_ref[0], i, 0)),
                 out_shape=_sds((N_CHIPS, R, C), BF16), sem=("arbitrary",))(chip_idx, w)


def _gather_weights(bufs):
    n = len(bufs)

    def body(*refs):
        outs = refs[n:2 * n]
        send, recv = refs[2 * n:]
        x, y, c, others = _place()
        me = 2 * x + y

        def remote(region, k, to):
            return pltpu.make_async_remote_copy(src_ref=region, dst_ref=region, send_sem=send.at[k],
                                                recv_sem=recv.at[k], device_id=to, device_id_type=MESH)

        started = []
        for a in range(n):
            for j, (cx, cy) in enumerate(others):
                cp = remote(_half(outs[a].at[me], c), 6 * a + j, (cx, cy, c))
                cp.start()
                started.append(cp)
        for a in range(n):
            for j, (cx, cy) in enumerate(others):
                landed = _half(outs[a].at[2 * cx + cy], c)
                remote(landed, 6 * a + j, (cx, cy, c)).wait_recv()
                cp = remote(landed, 6 * a + 3 + j, (x, y, 1 - c))
                cp.start()
                started.append(cp)
        for a in range(n):
            for j, (cx, cy) in enumerate(others):
                remote(_half(outs[a].at[2 * cx + cy], 1 - c), 6 * a + 3 + j, (x, y, 1 - c)).wait_recv()
        for cp in started:
            cp.wait_send()

    outs = [_sds(b.shape, b.dtype) for b in bufs]
    return _comm_call(body, "gather_weights", bufs, outs, 6 * n, aliases={a: a for a in range(n)})


def _swap_halves(gs):
    n = len(gs)

    def body(*refs):
        ins, outs = refs[:n], refs[n:2 * n]
        send, recv = refs[2 * n:]
        x, y, c, _ = _place()
        cps = [pltpu.make_async_remote_copy(src_ref=_half(ins[a], 1 - c), dst_ref=outs[a], send_sem=send.at[a],
                                            recv_sem=recv.at[a], device_id=(x, y, 1 - c), device_id_type=MESH)
               for a in range(n)]
        for cp in cps:
            cp.start()
        for cp in cps:
            cp.wait()

    outs = [_sds((g.shape[0], g.shape[1] // 2, g.shape[2]), g.dtype) for g in gs]
    return _comm_call(body, "swap_halves", gs, outs, n)


def _scatter_to_owners(ss):
    n = len(ss)

    def body(*refs):
        ins, outs = refs[:n], refs[n:2 * n]
        send, recv = refs[2 * n:]
        x, y, c, others = _place()
        cps = []
        for a in range(n):
            for j, (cx, cy) in enumerate(others):
                cps.append(pltpu.make_async_remote_copy(
                    src_ref=ins[a].at[2 * cx + cy], dst_ref=outs[a].at[j], send_sem=send.at[3 * a + j],
                    recv_sem=recv.at[3 * a + j], device_id=(cx, cy, c), device_id_type=MESH))
        for cp in cps:
            cp.start()
        for cp in cps:
            cp.wait()

    outs = [_sds((N_CHIPS - 1,) + s.shape[1:], s.dtype) for s in ss]
    return _comm_call(body, "scatter_to_owners", ss, outs, 3 * n)


def _join_halves(bufs):
    n = len(bufs)

    def body(*refs):
        outs = refs[n:2 * n]
        send, recv = refs[2 * n:]
        x, y, c, _ = _place()

        def swap(a, hc):
            region = _half(outs[a], hc)
            return pltpu.make_async_remote_copy(src_ref=region, dst_ref=region, send_sem=send.at[a],
                                                recv_sem=recv.at[a], device_id=(x, y, 1 - c), device_id_type=MESH)

        cps = [swap(a, c) for a in range(n)]
        for cp in cps:
            cp.start()
        for a in range(n):
            swap(a, 1 - c).wait_recv()
        for cp in cps:
            cp.wait_send()

    outs = [_sds(b.shape, b.dtype) for b in bufs]
    return _comm_call(body, "join_halves", bufs, outs, n, aliases={a: a for a in range(n)})


def _gather_small(v):
    rows, n = v.shape

    def body(v_ref, out_ref, send, recv, lsem):
        x, y, c, others = _place()
        me, sibling = (x, y, c), (x, y, 1 - c)

        def blk(px, py, pc):
            return out_ref.at[pl.ds((4 * px + 2 * py + pc) * rows, rows), :]

        def copy(k, block, to, src=None):
            return pltpu.make_async_remote_copy(src_ref=blk(*block) if src is None else src, dst_ref=blk(*block),
                                                send_sem=send.at[k], recv_sem=recv.at[k], device_id=to,
                                                device_id_type=MESH)

        mine = pltpu.make_async_copy(v_ref, blk(*me), lsem)
        mine.start()
        first = [copy(0, me, sibling, src=v_ref)]
        first += [copy(1 + j, me, (*chip, c), src=v_ref) for j, chip in enumerate(others)]
        for cp in first:
            cp.start()
        passed = [copy(4 + j, (*chip, c), sibling) for j, chip in enumerate(others)]
        for j, chip in enumerate(others):
            copy(1 + j, (*chip, c), me).wait_recv()
            passed[j].start()
        copy(0, sibling, me).wait_recv()
        for j, chip in enumerate(others):
            copy(4 + j, (*chip, 1 - c), me).wait_recv()
        for cp in first + passed:
            cp.wait_send()
        mine.wait()

    return pl.pallas_call(
        body, name="gather_small", out_shape=_sds((8 * rows, n), v.dtype),
        in_specs=[pl.BlockSpec(memory_space=pltpu.VMEM)], out_specs=pl.BlockSpec(memory_space=pltpu.VMEM),
        scratch_shapes=[pltpu.SemaphoreType.DMA((7,)), pltpu.SemaphoreType.DMA((7,)), pltpu.SemaphoreType.DMA],
        compiler_params=pltpu.CompilerParams(vmem_limit_bytes=VMEM_LIMIT))(v)


def _add_half(g, p, c):
    _, R, C = g.shape
    half = R // 2

    def body(c_ref, g_ref, p_ref, o_ref):
        o_ref[...] = g_ref[...] + p_ref[...]

    blk = (None, half, C)
    return _call(body, name="add_half", grid=(N_CHIPS,), prefetch=1,
                 in_specs=[pl.BlockSpec(blk, lambda s, c_ref: (s, c_ref[0], 0)),
                           pl.BlockSpec(blk, lambda s, c_ref: (s, 0, 0))],
                 out_specs=pl.BlockSpec(blk, lambda s, c_ref: (s, 0, 0)),
                 out_shape=_sds((N_CHIPS, half, C), F32), sem=("arbitrary",))(c, g, p)


def _sum_owner(s, q, buf, l, me, c):
    _, half, C = s.shape
    tr = half // 2

    def body(me_ref, c_ref, s_ref, q0, q1, q2, buf_ref, o_ref):
        o_ref[...] = ((s_ref[...] + q0[...]) + q1[...]) + q2[...]

    blk = (None, tr, C)
    qspec = lambda j: pl.BlockSpec(blk, lambda i, me_ref, c_ref: (j, i, 0))
    return _call(body, name=f"sum_owner_l{l}", grid=(half // tr,), prefetch=2,
                 in_specs=[pl.BlockSpec(blk, lambda i, me_ref, c_ref: (me_ref[0], i, 0)),
                           qspec(0), qspec(1), qspec(2), ANY],
                 out_specs=pl.BlockSpec(blk, lambda i, me_ref, c_ref: (l, 2 * c_ref[0] + i, 0)),
                 out_shape=_sds(buf.shape, F32), sem=("arbitrary",), aliases={6: 0})(me, c, s, q, q, q, buf)


def _adamw_math(w, g, m, v):
    m = ADAM_B1 * m + (1.0 - ADAM_B1) * g
    v = ADAM_B2 * v + (1.0 - ADAM_B2) * (g * g)
    m_hat = m / (1.0 - ADAM_B1 ** ADAM_STEP)
    v_hat = v / (1.0 - ADAM_B2 ** ADAM_STEP)
    delta = -ADAM_LR * (m_hat / (jnp.sqrt(v_hat) + ADAM_EPS) + ADAM_WD * w)
    return delta, m, v


def _adamw(w, g, m, v):
    rows, C = w.shape
    tr = next(t for t in (256, 128, 64) if rows % t == 0)

    def body(w_ref, g_ref, m_ref, v_ref, d_ref, nm_ref, nv_ref):
        d, nm, nv = _adamw_math(w_ref[...], g_ref[...], m_ref[...], v_ref[...])
        d_ref[...] = d
        nm_ref[...] = nm
        nv_ref[...] = nv

    spec = pl.BlockSpec((tr, C), lambda i: (i, 0))
    return _call(body, name="adamw", grid=(rows // tr,), in_specs=[spec] * 4, out_specs=[spec] * 3,
                 out_shape=[_sds((rows, C), F32)] * 3, sem=("parallel",))(w, g, m, v)


def _small_update(gathered, w, m, v):
    _, rows, n = gathered.shape
    tr = rows // 7

    def body(ga_ref, w_ref, m_ref, v_ref, g_ref, d_ref, nm_ref, nv_ref):
        g = ga_ref[0]
        for k in range(1, 8):
            g = g + ga_ref[k]
        d, nm, nv = _adamw_math(w_ref[...], g, m_ref[...], v_ref[...])
        g_ref[...] = g
        d_ref[...] = d
        nm_ref[...] = nm
        nv_ref[...] = nv

    spec = pl.BlockSpec((tr, n), lambda i: (i, 0))
    return _call(body, name="small_update", grid=(rows // tr,),
                 in_specs=[pl.BlockSpec((8, tr, n), lambda i: (0, i, 0)), spec, spec, spec], out_specs=[spec] * 4,
                 out_shape=[_sds((rows, n), F32)] * 4, sem=("parallel",))(gathered, w, m, v)


WEIGHTS = ("g_mix", "w_in", "g_q", "g_k", "w_attn_proj", "lambda_re", "lambda_im", "log_dt", "b_re", "b_im",
           "c_re", "c_im", "d_skip", "w_glu_a", "w_glu_b", "w_out", "g_ffn", "w_ffn_gate", "w_ffn_up", "w_ffn_down")
BIG = ("w_in", "w_attn_proj", "w_glu_a", "w_glu_b", "w_out", "w_ffn_gate", "w_ffn_up", "w_ffn_down")
SMALL = tuple(n for n in WEIGHTS if n not in BIG)
ROW_VECTORS = ("g_mix", "g_q", "g_k", "d_skip", "g_ffn")
PACK_QUANTUM = LANES * SUBLANES * 7


def _pack_small(parts, extra):
    flat = jnp.concatenate([parts[n].reshape(-1).astype(F32) for n in SMALL] + [extra.reshape(-1)])
    pad = -flat.shape[0] % PACK_QUANTUM
    return jnp.pad(flat, (0, pad)).reshape(-1, LANES)


def _unpack_small(packed, like):
    flat = packed.reshape(-1)
    out, at = {}, 0
    for n in SMALL:
        size = math.prod(like[n].shape)
        out[n] = flat[at:at + size].reshape(like[n].shape)
        at += size
    return out, flat[at]


def kernel(x, g_mix, w_in, g_q, g_k, w_attn_proj, lambda_re, lambda_im, log_dt, b_re, b_im, c_re, c_im, d_skip, w_glu_a, w_glu_b, w_out, g_ffn, w_ffn_gate, w_ffn_up, w_ffn_down, loss_target, m_g_mix, m_w_in, m_g_q, m_g_k, m_w_attn_proj, m_lambda_re, m_lambda_im, m_log_dt, m_b_re, m_b_im, m_c_re, m_c_im, m_d_skip, m_w_glu_a, m_w_glu_b, m_w_out, m_g_ffn, m_w_ffn_gate, m_w_ffn_up, m_w_ffn_down, v_g_mix, v_w_in, v_g_q, v_g_k, v_w_attn_proj, v_lambda_re, v_lambda_im, v_log_dt, v_b_re, v_b_im, v_c_re, v_c_im, v_d_skip, v_w_glu_a, v_w_glu_b, v_w_out, v_g_ffn, v_w_ffn_gate, v_w_ffn_up, v_w_ffn_down):
    given = dict(locals())
    W = {n: given[n] for n in WEIGHTS}
    M = {n: given["m_" + n] for n in WEIGHTS}
    V = {n: given["v_" + n] for n in WEIGHTS}
    depth = g_mix.shape[0]
    xl = x.reshape(x.shape[-2:])
    target = loss_target.reshape(loss_target.shape[-2:])
    c_idx = lax.axis_index("c").astype(jnp.int32).reshape(1)
    chip_idx = (2 * lax.axis_index("x") + lax.axis_index("y")).astype(jnp.int32).reshape(1)

    params = []
    for l in range(depth):
        full = _gather_weights([_cast_place(W[n], l, chip_idx) for n in BIG])
        p = dict(zip(BIG, full))
        for n in SMALL:
            p[n] = W[n][l][None] if n in ROW_VECTORS else W[n][l]
        params.append(p)

    saved, h = [], xl
    for l in range(depth):
        h, sv = _layer_fwd(h, params[l])
        saved.append(sv)
    dx, loss_part = _loss_head(h, target)

    owned = [lax.empty(W[n].shape, F32) for n in BIG]
    small_grads = [None] * depth
    for l in reversed(range(depth)):
        dx, g = _layer_bwd(dx, saved[l], params[l])
        parts = [g[n] for n in BIG]
        sib = _swap_halves(parts)
        chip = [_add_half(a, b, c_idx) for a, b in zip(parts, sib)]
        recv = _scatter_to_owners(chip)
        owned = [_sum_owner(s, q, buf, l, chip_idx, c_idx) for s, q, buf in zip(chip, recv, owned)]
        small_grads[l] = g
    reduced = dict(zip(BIG, _join_halves(owned)))

    grads, delta, new_m, new_v = {}, {}, {}, {}
    for n in BIG:
        shape = W[n].shape
        two_d = (shape[0] * shape[1], shape[2])
        d, nm, nv = _adamw(W[n].reshape(two_d), reduced[n].reshape(two_d), M[n].reshape(two_d), V[n].reshape(two_d))
        grads[n], delta[n], new_m[n], new_v[n] = reduced[n], d.reshape(shape), nm.reshape(shape), nv.reshape(shape)

    stacked = {n: jnp.stack([small_grads[l][n] for l in range(depth)]) for n in SMALL}
    zero = jnp.zeros((1,), F32)
    packed = _pack_small(stacked, loss_part)
    gathered = _gather_small(packed).reshape(8, *packed.shape)
    gs, ds, nms, nvs = _small_update(gathered, _pack_small(W, zero), _pack_small(M, zero), _pack_small(V, zero))
    sg, loss = _unpack_small(gs, W)
    sd, _ = _unpack_small(ds, W)
    sm, _ = _unpack_small(nms, W)
    sv_, _ = _unpack_small(nvs, W)
    for n in SMALL:
        grads[n], delta[n], new_m[n], new_v[n] = sg[n], sd[n], sm[n], sv_[n]

    return (loss, dx.reshape(x.shape), *[grads[n] for n in WEIGHTS], *[delta[n] for n in WEIGHTS],
            *[new_m[n] for n in WEIGHTS], *[new_v[n] for n in WEIGHTS])
```

```python
import collections
import functools
import math

import jax
import jax.numpy as jnp
from jax import lax
from jax.experimental import pallas as pl
from jax.experimental.pallas import tpu as pltpu

F32 = jnp.float32
BF16 = jnp.bfloat16

D_MODEL = 1024
DEPTH = 4
HEAD_DIM = 64
N_HEADS = 8
ATTN_WIDTH = N_HEADS * HEAD_DIM
ATTN_PATTERN = ((128, 1), (512, 4), (2048, 16))
N_GROUPS = len(ATTN_PATTERN)
BLK = 128
SSM_WIDTH = 512
SSM_GROUP = 16
SSM_GROUPS = 32
SSM_STATE = 64
D_FF = 2816
IN_COLS = 7168
EPS = 1e-6
ADAM_LR, ADAM_B1, ADAM_B2, ADAM_EPS, ADAM_WD, ADAM_STEP = 0.001, 0.9, 0.999, 1e-08, 0.01, 10

N_CHIPS = 4
MESH = pl.DeviceIdType.MESH

LANES = 128
SUBLANES = 8
VMEM_LIMIT = 56 * 1024 * 1024

TM = 512
TM_PROJ = 1024
TL_WGRAD = 2048
TM_MIX = 256

SSM_TB = 512
SSM_TC = 64
SSM_SUB = SUBLANES
SSM_PITCH = 72
N_SLAB = SSM_GROUPS * SSM_STATE // LANES
SLABS_PER_WIN = 4
SCAN_GROUP = 4


def _params(sem=None, collective=False):
    return pltpu.CompilerParams(dimension_semantics=sem, vmem_limit_bytes=VMEM_LIMIT)


ANY = pl.BlockSpec(memory_space=pl.ANY)

Rider = collections.namedtuple("Rider", "ins out_shapes n_sem start wait aliases")


def _with_rider(body, rider, grid, prefetch, n_in, n_out, n_scratch):
    n_rin, n_rout = len(rider.ins), len(rider.out_shapes)

    def hosted(*refs):
        pre, rest = refs[:prefetch], refs[prefetch:]
        ins, rin = rest[:n_in], rest[n_in:n_in + n_rin]
        o0 = n_in + n_rin
        outs, rout = rest[o0:o0 + n_out], rest[o0 + n_out:o0 + n_out + n_rout]
        s0 = o0 + n_out + n_rout
        scr, (send, recv) = rest[s0:s0 + n_scratch], rest[s0 + n_scratch:]
        first = functools.reduce(jnp.logical_and, [pl.program_id(k) == 0 for k in range(len(grid))])
        last = functools.reduce(jnp.logical_and, [pl.program_id(k) == grid[k] - 1 for k in range(len(grid))])

        @pl.when(first)
        def _():
            rider.start(rin, rout, send, recv)

        body(*pre, *ins, *outs, *scr)

        @pl.when(last)
        def _():
            rider.wait(rin, rout, send, recv)

    return hosted


def _call(body, *, name, grid, in_specs, out_specs, out_shape, scratch=(), sem=None, aliases=None,
          prefetch=0, rider=None):
    if rider is not None:
        single = not isinstance(out_specs, (list, tuple))
        out_specs = [out_specs] if single else list(out_specs)
        out_shape = [out_shape] if single else list(out_shape)
        body = _with_rider(body, rider, grid, prefetch, len(in_specs), len(out_specs), len(scratch))
        aliases = dict(aliases or {})
        aliases.update({prefetch + len(in_specs) + k: len(out_specs) + v for k, v in rider.aliases.items()})
        in_specs = list(in_specs) + [ANY] * len(rider.ins)
        out_specs = out_specs + [ANY] * len(rider.out_shapes)
        out_shape = out_shape + list(rider.out_shapes)
        scratch = list(scratch) + [pltpu.SemaphoreType.DMA((rider.n_sem,)), pltpu.SemaphoreType.DMA((rider.n_sem,))]
        sem = ("arbitrary",) * len(grid)
        fn = _call(body, name=name + "_host", grid=grid, in_specs=in_specs, out_specs=out_specs, out_shape=out_shape,
                   scratch=scratch, sem=sem, aliases=aliases, prefetch=prefetch)
        return lambda *args: fn(*args, *rider.ins)
    kw = {}
    if aliases:
        kw["input_output_aliases"] = aliases
    if prefetch:
        gs = pltpu.PrefetchScalarGridSpec(num_scalar_prefetch=prefetch, grid=grid, in_specs=in_specs,
                                          out_specs=out_specs, scratch_shapes=list(scratch))
        return pl.pallas_call(body, name=name, grid_spec=gs, out_shape=out_shape,
                              compiler_params=_params(sem), **kw)
    return pl.pallas_call(body, name=name, grid=grid, in_specs=in_specs, out_specs=out_specs,
                          out_shape=out_shape, scratch_shapes=list(scratch),
                          compiler_params=_params(sem), **kw)


def _sds(shape, dtype):
    return jax.ShapeDtypeStruct(shape, dtype)


def _sigmoid(v):
    return 1.0 / (1.0 + jnp.exp(-v))


def _dot(a, b):
    return jnp.dot(a, b, preferred_element_type=F32)


def _dot_nt(a, b):
    return lax.dot_general(a, b, (((1,), (1,)), ((), ())), preferred_element_type=F32)


def _dot_tn(a, b):
    return lax.dot_general(a, b, (((0,), (0,)), ((), ())), preferred_element_type=F32)


def _in_proj_fwd(x, g, w, rider=None):
    L = x.shape[0]
    ns = w.shape[2]
    tn = ns
    nj = ns // tn
    TM = TM_PROJ

    def body(x_ref, g_ref, w_ref, z_ref, h_ref):
        @pl.when(pl.program_id(1) == 0)
        def _():
            xv = x_ref[...]
            r = lax.rsqrt(jnp.mean(xv * xv, axis=-1, keepdims=True) + EPS)
            h_ref[...] = (xv * r * g_ref[...]).astype(BF16)
        z_ref[...] = _dot(h_ref[...], w_ref[...]).astype(BF16)

    return _call(
        body, name="in_proj_fwd", grid=(L // TM, N_CHIPS * nj),
        in_specs=[pl.BlockSpec((TM, D_MODEL), lambda i, j: (i, 0)),
                  pl.BlockSpec((1, D_MODEL), lambda i, j: (0, 0)),
                  pl.BlockSpec((None, D_MODEL, tn), lambda i, j: (j // nj, 0, j % nj))],
        out_specs=[pl.BlockSpec((TM, tn), lambda i, j: (i, j)),
                   pl.BlockSpec((TM, D_MODEL), lambda i, j: (i, 0))],
        out_shape=[_sds((L, N_CHIPS * ns), BF16), _sds((L, D_MODEL), BF16)],
        sem=("parallel", "arbitrary"), rider=rider)(x, g, w)


DL_TILE = 512
SCALE = HEAD_DIM ** -0.5


def _perm_matrix(d):
    rho = jnp.arange(DL_TILE)
    src = rho // (DL_TILE // d) + d * (rho % (DL_TILE // d))
    return (src[:, None] == jnp.arange(DL_TILE)[None, :]).astype(BF16)


def _head_sum_matrix():
    h = jnp.arange(ATTN_WIDTH) // HEAD_DIM
    return (h[:, None] == h[None, :]).astype(BF16)


def _split(v):
    hi = v.astype(BF16)
    return hi, (v - hi.astype(F32)).astype(BF16)


def _head_sum(v, hs):
    hi, lo = _split(v)
    return _dot(hi, hs) + _dot(lo, hs)


def _permute(pm, v):
    hi, lo = _split(v)
    return _dot(pm, hi) + _dot(pm, lo)


def _dl_view(t, d):
    if d * BLK <= DL_TILE:
        return t
    return t.reshape(t.shape[0] // DL_TILE, d, DL_TILE // d, t.shape[1])


def _dl_spec(d, width, which):
    if d * BLK <= DL_TILE:
        per_tile = DL_TILE // (d * BLK)
        return pl.BlockSpec((BLK, width), lambda r, n: ((which(n) // per_tile) * (DL_TILE // BLK)
                                                       + r * per_tile + which(n) % per_tile, 0))
    tiles = d * BLK // DL_TILE
    return pl.BlockSpec((tiles, None, DL_TILE // d, width), lambda r, n: (which(n), r, 0, 0))


def _dl_read(ref):
    v = ref[...]
    return v if v.ndim == 2 else v.reshape(BLK, v.shape[-1])


def _dl_write(ref, v):
    ref[...] = v if len(ref.shape) == 2 else v.reshape(ref.shape)


def _qkv_prep(z, gq_t, gk_t):
    L = z.shape[0]
    qkv_w = N_GROUPS * ATTN_WIDTH

    def body(zq_ref, zk_ref, zv_ref, gq_ref, gk_ref, hs_ref, p1_ref, p2_ref, *outs):
        hs = hs_ref[...]
        perms = (None, p1_ref[...], p2_ref[...])
        for g in range(N_GROUPS):
            cols = slice(g * ATTN_WIDTH, (g + 1) * ATTN_WIDTH)
            xq = zq_ref[:, cols].astype(F32)
            xk = zk_ref[:, cols].astype(F32)
            rq = lax.rsqrt(_head_sum(xq * xq, hs) * (1.0 / HEAD_DIM) + EPS)
            rk = lax.rsqrt(_head_sum(xk * xk, hs) * (1.0 / HEAD_DIM) + EPS)
            vals = [(xq * rq * (gq_ref[...] * SCALE)).astype(BF16), (xk * rk * gk_ref[...]).astype(BF16),
                    zv_ref[:, cols]]
            for j, t in enumerate(vals):
                if perms[g] is not None:
                    t = _dot(perms[g], t).astype(BF16)
                outs[3 * g + j][...] = t

    tile = pl.BlockSpec((DL_TILE, ATTN_WIDTH), lambda i: (i, 0))
    mat = pl.BlockSpec((DL_TILE, DL_TILE), lambda i: (0, 0))
    vec = pl.BlockSpec((1, ATTN_WIDTH), lambda i: (0, 0))
    outs = _call(
        body, name="qkv_prep", grid=(L // DL_TILE,),
        in_specs=[pl.BlockSpec((DL_TILE, qkv_w), lambda i: (i, 0)), pl.BlockSpec((DL_TILE, qkv_w), lambda i: (i, 1)),
                  pl.BlockSpec((DL_TILE, qkv_w), lambda i: (i, 2)), vec, vec, mat, mat, mat],
        out_specs=[tile] * 9, out_shape=[_sds((L, ATTN_WIDTH), BF16)] * 9,
        sem=("parallel",))(z, z, z, gq_t, gk_t, _head_sum_matrix(), _perm_matrix(ATTN_PATTERN[1][1]),
                           _perm_matrix(ATTN_PATTERN[2][1]))
    return [tuple(outs[3 * g:3 * g + 3]) for g in range(N_GROUPS)]


def _pair_masks():
    lane = lax.broadcasted_iota(jnp.int32, (1, LANES), 1)
    return lane < HEAD_DIM, lane >= HEAD_DIM


def _attn_fwd(qs, ks, v, gi):
    L = qs.shape[0]
    _, d = ATTN_PATTERN[gi]
    nb = L // (d * BLK)

    def body(q_ref, kc_ref, kp_ref, vc_ref, vp_ref, o_ref, l_ref):
        n = pl.program_id(1)
        qi = lax.broadcasted_iota(jnp.int32, (BLK, 2 * BLK), 0)
        kj = lax.broadcasted_iota(jnp.int32, (BLK, 2 * BLK), 1)
        prev = kj < BLK
        mask = jnp.logical_and(jnp.where(prev, kj, qi) >= jnp.where(prev, qi, kj - BLK),
                               kj >= jnp.where(n > 0, 0, BLK))
        q = _dl_read(q_ref)
        kw = jnp.concatenate([_dl_read(kp_ref), _dl_read(kc_ref)], axis=0)
        vw = jnp.concatenate([_dl_read(vp_ref), _dl_read(vc_ref)], axis=0)
        one = jnp.ones((2 * BLK, LANES), BF16)
        o_parts, l_parts = [], []
        for hp in range(N_HEADS // 2):
            ls = slice(hp * LANES, (hp + 1) * LANES)
            qp, kp_, vp_ = q[:, ls], kw[:, ls], vw[:, ls]
            num = jnp.zeros((BLK, LANES), F32)
            den = jnp.zeros((BLK, LANES), F32)
            mb = jnp.zeros((BLK, LANES), F32)
            for he in _pair_masks():
                s = jnp.where(mask, _dot_nt(jnp.where(he, qp, 0), kp_), -jnp.inf)
                m = jnp.max(s, axis=-1, keepdims=True)
                p = jnp.exp(s - m).astype(BF16)
                acc = _dot(p, jnp.concatenate([jnp.where(he, vp_, 0), jnp.where(he, one, 0)], axis=1))
                num += acc[:, :LANES]
                den += acc[:, LANES:]
                mb = jnp.where(he, m, mb)
            o_parts.append((num / den).astype(BF16))
            l_parts.append(mb + jnp.log(den))
        _dl_write(o_ref, jnp.concatenate(o_parts, axis=1))
        _dl_write(l_ref, jnp.concatenate(l_parts, axis=1))

    cur = _dl_spec(d, ATTN_WIDTH, lambda n: n)
    prev = _dl_spec(d, ATTN_WIDTH, lambda n: jnp.maximum(n - 1, 0))
    view = lambda t: _dl_view(t, d)
    o, l = _call(
        body, name=f"attn_fwd_g{gi}", grid=(d, nb), in_specs=[cur, cur, prev, cur, prev], out_specs=[cur, cur],
        out_shape=[_sds(view(qs).shape, BF16), _sds(view(qs).shape, F32)],
        sem=("parallel", "parallel"))(view(qs), view(ks), view(ks), view(v), view(v))
    return o.reshape(L, ATTN_WIDTH), l.reshape(L, ATTN_WIDTH)


def _to_token_order(os_, ls_, pts):
    o_tok, l_tok = [], []
    for o, l, pt in zip(os_, ls_, pts):
        if pt is None:
            o_tok.append(o.astype(F32))
            l_tok.append(l)
        else:
            o_tok.append(_dot(pt, o))
            l_tok.append(_permute(pt, l))
    return o_tok, l_tok


def _combine_fwd(os_, ls_):
    L = os_[0].shape[0]

    def body(o0, o1, o2, l0, l1, l2, pt1_ref, pt2_ref, a_ref):
        o_tok, l_tok = _to_token_order((o0[...], o1[...], o2[...]), (l0[...], l1[...], l2[...]),
                                       (None, pt1_ref[...], pt2_ref[...]))
        w = _combine_weights(*l_tok)
        a_ref[...] = (w[0] * o_tok[0] + w[1] * o_tok[1] + w[2] * o_tok[2]).astype(BF16)

    tile = pl.BlockSpec((DL_TILE, ATTN_WIDTH), lambda i: (i, 0))
    mat = pl.BlockSpec((DL_TILE, DL_TILE), lambda i: (0, 0))
    return _call(body, name="combine_fwd", grid=(L // DL_TILE,), in_specs=[tile] * 6 + [mat, mat], out_specs=tile,
                 out_shape=_sds((L, ATTN_WIDTH), BF16), sem=("parallel",))(
                     *os_, *ls_, _perm_matrix(ATTN_PATTERN[1][1]).T, _perm_matrix(ATTN_PATTERN[2][1]).T)


def _gelu(v):
    c = math.sqrt(2.0 / math.pi)
    return 0.5 * v * (1.0 + jnp.tanh(c * (v + 0.044715 * v * v * v)))


def _gelu_grad(v):
    c = math.sqrt(2.0 / math.pi)
    t = jnp.tanh(c * (v + 0.044715 * v * v * v))
    return 0.5 * (1.0 + t) + 0.5 * v * (1.0 - t * t) * c * (1.0 + 3.0 * 0.044715 * v * v)


def _ssm_fill(u, bwre_ref, bwim_ref, sre, sim):
    for k in range(N_SLAB):
        w = k // SLABS_PER_WIN
        uw = u[:, w * LANES:(w + 1) * LANES]
        br = _dot(uw, bwre_ref[k])
        bi = _dot(uw, bwim_ref[k])
        for j in range(SSM_SUB):
            sre[k, j * SSM_PITCH:j * SSM_PITCH + SSM_TC, :] = br[j * SSM_TC:(j + 1) * SSM_TC, :]
            sim[k, j * SSM_PITCH:j * SSM_PITCH + SSM_TC, :] = bi[j * SSM_TC:(j + 1) * SSM_TC, :]


def _rows(i):
    return pl.ds(i, SSM_SUB, stride=SSM_PITCH)


def _slab_rows(ref, k):
    return jnp.concatenate([ref[k, j * SSM_PITCH:j * SSM_PITCH + SSM_TC, :] for j in range(SSM_SUB)], axis=0)


def _bcast(ref, k):
    return jnp.broadcast_to(ref[pl.ds(k, 1), :], (SSM_SUB, LANES))


def _scan(sre, sim, are_ref, aim_ref, k0, init, *, reverse, store, sign=1.0):
    ar = [_bcast(are_ref, k0 + kk) for kk in range(SCAN_GROUP)]
    ai = [sign * _bcast(aim_ref, k0 + kk) for kk in range(SCAN_GROUP)]

    def step(t, carry):
        i = SSM_TC - 1 - t if reverse else t
        out = []
        for kk in range(SCAN_GROUP):
            k = k0 + kk
            xr, xi = carry[2 * kk], carry[2 * kk + 1]
            nr = ar[kk] * xr - ai[kk] * xi + sre[k, _rows(i), :]
            ni = ar[kk] * xi + ai[kk] * xr + sim[k, _rows(i), :]
            if store:
                sre[k, _rows(i), :] = nr
                sim[k, _rows(i), :] = ni
            out += [nr, ni]
        return tuple(out)

    flat = []
    for re, im in init:
        flat += [re, im]
    res = lax.fori_loop(0, SSM_TC, step, tuple(flat))
    return [(res[2 * kk], res[2 * kk + 1]) for kk in range(SCAN_GROUP)]


def _ssm_seeds(ends_re, ends_im, a64re_ref, a64im_ref, carry_re, carry_im, seed_re, seed_im, k,
               *, reverse, sign=1.0):
    ar = a64re_ref[pl.ds(k, 1), :]
    ai = sign * a64im_ref[pl.ds(k, 1), :]
    cr = carry_re[pl.ds(k, 1), :]
    ci = carry_im[pl.ds(k, 1), :]
    order = range(SSM_SUB - 1, -1, -1) if reverse else range(SSM_SUB)
    for j in order:
        seed_re[k, pl.ds(j, 1), :] = cr
        seed_im[k, pl.ds(j, 1), :] = ci
        er = ends_re[k, pl.ds(j, 1), :]
        ei = ends_im[k, pl.ds(j, 1), :]
        cr, ci = ar * cr - ai * ci + er, ar * ci + ai * cr + ei
    carry_re[pl.ds(k, 1), :] = cr
    carry_im[pl.ds(k, 1), :] = ci


def _ssm_specs_consts():
    c2 = pl.BlockSpec((N_SLAB, LANES), lambda b: (0, 0))
    c3 = pl.BlockSpec((N_SLAB, LANES, LANES), lambda b: (0, 0, 0))
    return c2, c3


def _ssm_scratch():
    rows = SSM_SUB * SSM_PITCH
    return [pltpu.VMEM((N_SLAB, rows, LANES), F32), pltpu.VMEM((N_SLAB, rows, LANES), F32)]


def _ssm_fwd(z, pk, dskip):
    L = z.shape[0]
    nb = L // SSM_TB
    ucol = (3 * N_GROUPS * ATTN_WIDTH) // SSM_WIDTH

    def body(u_ref, are_ref, aim_ref, a64re_ref, a64im_ref, bwre_ref, bwim_ref, cwre_ref, cwim_ref, d_ref,
             ypre_ref, yact_ref, sdre_ref, sdim_ref, sre, sim, carry_re, carry_im, ends_re, ends_im,
             seed_re, seed_im):
        @pl.when(pl.program_id(0) == 0)
        def _():
            carry_re[...] = jnp.zeros_like(carry_re)
            carry_im[...] = jnp.zeros_like(carry_im)

        u = u_ref[...]
        _ssm_fill(u, bwre_ref, bwim_ref, sre, sim)
        zero = jnp.zeros((SSM_SUB, LANES), F32)
        for k0 in range(0, N_SLAB, SCAN_GROUP):
            ends = _scan(sre, sim, are_ref, aim_ref, k0, [(zero, zero)] * SCAN_GROUP, reverse=False, store=False)
            for kk in range(SCAN_GROUP):
                ends_re[k0 + kk] = ends[kk][0]
                ends_im[k0 + kk] = ends[kk][1]
            for kk in range(SCAN_GROUP):
                _ssm_seeds(ends_re, ends_im, a64re_ref, a64im_ref, carry_re, carry_im, seed_re, seed_im,
                           k0 + kk, reverse=False)
            init = [(seed_re[k0 + kk], seed_im[k0 + kk]) for kk in range(SCAN_GROUP)]
            _scan(sre, sim, are_ref, aim_ref, k0, init, reverse=False, store=True)
        sdre_ref[...] = seed_re[...]
        sdim_ref[...] = seed_im[...]
        for w in range(N_SLAB // SLABS_PER_WIN):
            acc = jnp.zeros((SSM_TB, LANES), F32)
            for kk in range(SLABS_PER_WIN):
                k = w * SLABS_PER_WIN + kk
                acc += _dot(_slab_rows(sre, k).astype(BF16), cwre_ref[k])
                acc -= _dot(_slab_rows(sim, k).astype(BF16), cwim_ref[k])
            cols = slice(w * LANES, (w + 1) * LANES)
            ypre = acc + d_ref[:, cols] * u[:, cols].astype(F32)
            ypre_ref[:, cols] = ypre
            yact_ref[:, cols] = _gelu(ypre).astype(BF16)

    c2, c3 = _ssm_specs_consts()
    seed_spec = pl.BlockSpec((None, N_SLAB, SSM_SUB, LANES), lambda b: (b, 0, 0, 0))
    small = pltpu.VMEM((N_SLAB, LANES), F32)
    tile = pltpu.VMEM((N_SLAB, SSM_SUB, LANES), F32)
    return _call(
        body, name="ssm_fwd", grid=(nb,),
        in_specs=[pl.BlockSpec((SSM_TB, SSM_WIDTH), lambda b: (b, ucol)), c2, c2, c2, c2, c3, c3, c3, c3,
                  pl.BlockSpec((1, SSM_WIDTH), lambda b: (0, 0))],
        out_specs=[pl.BlockSpec((SSM_TB, SSM_WIDTH), lambda b: (b, 0)),
                   pl.BlockSpec((SSM_TB, SSM_WIDTH), lambda b: (b, 0)), seed_spec, seed_spec],
        out_shape=[_sds((L, SSM_WIDTH), F32), _sds((L, SSM_WIDTH), BF16),
                   _sds((nb, N_SLAB, SSM_SUB, LANES), F32), _sds((nb, N_SLAB, SSM_SUB, LANES), F32)],
        scratch=_ssm_scratch() + [small, small, tile, tile, tile, tile],
        sem=("arbitrary",))(z, pk["a_re"], pk["a_im"], pk["a64_re"], pk["a64_im"],
                            pk["bw_re"].astype(BF16), pk["bw_im"].astype(BF16),
                            pk["cw_re"].astype(BF16), pk["cw_im"].astype(BF16), dskip)


def _combine_weights(l0, l1, l2):
    m = jnp.maximum(jnp.maximum(l0, l1), l2)
    e0, e1, e2 = jnp.exp(l0 - m), jnp.exp(l1 - m), jnp.exp(l2 - m)
    inv = 1.0 / (e0 + e1 + e2)
    return e0 * inv, e1 * inv, e2 * inv


def _mix_fwd(x, z, a, yact, w_ap, w_ga, w_gb, w_out):
    L = x.shape[0]
    cs = D_MODEL // N_CHIPS
    ga_col = (3 * N_GROUPS * ATTN_WIDTH + SSM_WIDTH) // D_MODEL

    def body(x_ref, ga_ref, gs_ref, a_ref, y_ref, wap_ref, wga_ref, wgb_ref, wout_ref,
             x1_ref, aout_ref, sa_ref, sb_ref, mix_ref):
        a = a_ref[...]
        y = y_ref[...]
        for s in range(N_CHIPS):
            cols = slice(s * cs, (s + 1) * cs)
            aout_ref[:, cols] = _dot(a, wap_ref[s]).astype(BF16)
            sa_ref[:, cols] = _dot(y, wga_ref[s]).astype(BF16)
            sb_ref[:, cols] = _dot(y, wgb_ref[s]).astype(BF16)
        s_out = sa_ref[...].astype(F32) * _sigmoid(sb_ref[...].astype(F32))
        mix = (_sigmoid(ga_ref[...].astype(F32)) * aout_ref[...].astype(F32)
               + _sigmoid(gs_ref[...].astype(F32)) * s_out).astype(BF16)
        mix_ref[...] = mix
        x1_ref[...] = x_ref[...] + _dot(mix, wout_ref[...])

    tok = lambda w: pl.BlockSpec((TM_MIX, w), lambda i: (i, 0))
    wsm = pl.BlockSpec((N_CHIPS, ATTN_WIDTH, cs), lambda i: (0, 0, 0))
    return _call(
        body, name="mix_fwd", grid=(L // TM_MIX,),
        in_specs=[tok(D_MODEL), pl.BlockSpec((TM_MIX, D_MODEL), lambda i: (i, ga_col)),
                  pl.BlockSpec((TM_MIX, D_MODEL), lambda i: (i, ga_col + 1))]
                 + [tok(ATTN_WIDTH)] * 2 + [wsm, wsm, wsm, pl.BlockSpec((D_MODEL, D_MODEL), lambda i: (0, 0))],
        out_specs=[tok(D_MODEL), tok(D_MODEL), tok(D_MODEL), tok(D_MODEL), tok(D_MODEL)],
        out_shape=[_sds((L, D_MODEL), F32)] + [_sds((L, D_MODEL), BF16)] * 4,
        sem=("parallel",))(x, z, z, a, yact, w_ap, w_ga, w_gb, w_out.reshape(D_MODEL, D_MODEL))


def _ffn_fwd(x1, g, w_g, w_u, w_d, rider=None):
    L = x1.shape[0]
    fs = D_FF // N_CHIPS
    TM = TM_PROJ

    def body(x_ref, g_ref, wg_ref, wu_ref, wd_ref, x2_ref, h_ref, gate_ref, up_ref, act_ref, acc):
        s = pl.program_id(1)

        @pl.when(s == 0)
        def _():
            xv = x_ref[...]
            r = lax.rsqrt(jnp.mean(xv * xv, axis=-1, keepdims=True) + EPS)
            h_ref[...] = (xv * r * g_ref[...]).astype(BF16)
            acc[...] = jnp.zeros_like(acc)

        h = h_ref[...]
        gate = _dot(h, wg_ref[...])
        up = _dot(h, wu_ref[...])
        act = (gate * _sigmoid(gate) * up).astype(BF16)
        gate_ref[...] = gate.astype(BF16)
        up_ref[...] = up.astype(BF16)
        act_ref[...] = act
        acc[...] += _dot(act, wd_ref[...])

        @pl.when(s == N_CHIPS - 1)
        def _():
            x2_ref[...] = x_ref[...] + acc[...]

    tok = pl.BlockSpec((TM, D_MODEL), lambda i, s: (i, 0))
    ffs = pl.BlockSpec((None, TM, fs), lambda i, s: (s, i, 0))
    return _call(
        body, name="ffn_fwd", grid=(L // TM, N_CHIPS),
        in_specs=[tok, pl.BlockSpec((1, D_MODEL), lambda i, s: (0, 0)),
                  pl.BlockSpec((None, D_MODEL, fs), lambda i, s: (s, 0, 0)),
                  pl.BlockSpec((None, D_MODEL, fs), lambda i, s: (s, 0, 0)),
                  pl.BlockSpec((None, fs, D_MODEL), lambda i, s: (s, 0, 0))],
        out_specs=[tok, tok, ffs, ffs, ffs],
        out_shape=[_sds((L, D_MODEL), F32), _sds((L, D_MODEL), BF16)] + [_sds((N_CHIPS, L, fs), BF16)] * 3,
        scratch=[pltpu.VMEM((TM, D_MODEL), F32)],
        sem=("parallel", "arbitrary"), rider=rider)(x1, g, w_g, w_u, w_d)


def _loss_head(xl, target):
    L = xl.shape[0]

    def body(x_ref, t_ref, dx_ref, loss_ref, acc):
        i = pl.program_id(0)

        @pl.when(i == 0)
        def _():
            acc[...] = jnp.zeros_like(acc)

        e = x_ref[...] - t_ref[...]
        dx_ref[...] = e * (1.0 / D_MODEL)
        acc[...] += jnp.sum((e * e).reshape(TM // SUBLANES, SUBLANES, D_MODEL), axis=0)

        @pl.when(i == pl.num_programs(0) - 1)
        def _():
            loss_ref[...] = (0.5 / D_MODEL) * jnp.sum(acc[...]).reshape(1, 1)

    tok = pl.BlockSpec((TM, D_MODEL), lambda i: (i, 0))
    return _call(
        body, name="loss_head", grid=(L // TM,), in_specs=[tok, tok],
        out_specs=[tok, pl.BlockSpec((1, 1), lambda i: (0, 0))],
        out_shape=[_sds((L, D_MODEL), F32), _sds((1, 1), F32)],
        scratch=[pltpu.VMEM((SUBLANES, D_MODEL), F32)], sem=("arbitrary",))(xl, target)


def _ssm_pack(lam_re, lam_im, log_dt, b_re, b_im, c_re, c_im):
    dt = jnp.exp(log_dt)[:, None]
    mag = jnp.exp(lam_re * dt)
    ang = lam_im * dt
    ar = mag * jnp.cos(ang)
    ai = mag * jnp.sin(ang)
    nr = ar - 1.0
    ni = ai
    den = lam_re * lam_re + lam_im * lam_im
    cr = ((nr * lam_re + ni * lam_im) / den)[..., None]
    ci = ((ni * lam_re - nr * lam_im) / den)[..., None]
    bbr = cr * b_re - ci * b_im
    bbi = cr * b_im + ci * b_re
    eye = jnp.eye(SSM_GROUPS, dtype=F32)
    n_state = N_SLAB * LANES

    def b_windows(bb):
        full = jnp.einsum('gpc,gh->gchp', bb, eye).reshape(SSM_WIDTH, n_state)
        return jnp.stack([full[(k // SLABS_PER_WIN) * LANES:(k // SLABS_PER_WIN + 1) * LANES,
                               k * LANES:(k + 1) * LANES] for k in range(N_SLAB)])

    def c_windows(cc):
        full = jnp.einsum('gcp,gh->hpgc', cc, eye).reshape(n_state, SSM_WIDTH)
        return jnp.stack([full[k * LANES:(k + 1) * LANES,
                               (k // SLABS_PER_WIN) * LANES:(k // SLABS_PER_WIN + 1) * LANES]
                          for k in range(N_SLAB)])

    pr, pi = ar, ai
    for _ in range(int(math.log2(SSM_TC))):
        pr, pi = pr * pr - pi * pi, 2.0 * pr * pi
    return dict(a_re=ar.reshape(N_SLAB, LANES), a_im=ai.reshape(N_SLAB, LANES),
                a64_re=pr.reshape(N_SLAB, LANES), a64_im=pi.reshape(N_SLAB, LANES),
                bw_re=b_windows(bbr), bw_im=b_windows(bbi), cw_re=c_windows(c_re), cw_im=c_windows(c_im))


def _layer_fwd(x, p, next_bufs=None):
    outs = _in_proj_fwd(x, p["g_mix"], p["w_in"], _gather_ici_rider(next_bufs) if next_bufs else None)
    (z, h), next_bufs = outs[:2], list(outs[2:])
    qkv = _qkv_prep(z, jnp.tile(p["g_q"], (1, N_HEADS)), jnp.tile(p["g_k"], (1, N_HEADS)))
    os_, ls_ = [], []
    for gi in range(N_GROUPS):
        o, l = _attn_fwd(*qkv[gi], gi)
        os_.append(o)
        ls_.append(l)
    a = _combine_fwd(os_, ls_)
    pk = _ssm_pack(p["lambda_re"], p["lambda_im"], p["log_dt"], p["b_re"], p["b_im"], p["c_re"], p["c_im"])
    ypre, yact, sd_re, sd_im = _ssm_fwd(z, pk, p["d_skip"])
    x1, aout, sa, sb, mix = _mix_fwd(x, z, a, yact, p["w_attn_proj"], p["w_glu_a"], p["w_glu_b"], p["w_out"])
    outs = _ffn_fwd(x1, p["g_ffn"], p["w_ffn_gate"], p["w_ffn_up"], p["w_ffn_down"],
                    _gather_d2d_rider(next_bufs) if next_bufs else None)
    (x2, h2, gate, up, act), next_full = outs[:5], list(outs[5:])
    saved = dict(x=x, z=z, h=h, qkv=qkv, os=os_, ls=ls_, pk=pk, ypre=ypre, yact=yact, sd_re=sd_re, sd_im=sd_im,
                 x1=x1, a=a, aout=aout, sa=sa, sb=sb, mix=mix, h2=h2, gate=gate, up=up, act=act)
    return x2, saved, next_full


def _rms_bwd(xv, g, dh):
    r = lax.rsqrt(jnp.mean(xv * xv, axis=-1, keepdims=True) + EPS)
    xn = xv * r
    dxn = dh * g
    dx = r * (dxn - xn * jnp.mean(dxn * xn, axis=-1, keepdims=True))
    dg = jnp.sum((dh * xn).reshape(xv.shape[0] // SUBLANES, SUBLANES, xv.shape[1]), axis=0)
    return dx, dg


def _ffn_bwd(dx2, x1, g, gate, up, w_g, w_u, w_d, rider=None):
    L = x1.shape[0]
    fs = D_FF // N_CHIPS

    def body(dx_ref, x_ref, g_ref, gate_ref, up_ref, wg_ref, wu_ref, wd_ref,
             dx1_ref, dgate_ref, dup_ref, dg_ref, acc, dgacc):
        i, s = pl.program_id(0), pl.program_id(1)

        @pl.when(s == 0)
        def _():
            acc[...] = jnp.zeros_like(acc)

        @pl.when(jnp.logical_and(i == 0, s == 0))
        def _():
            dgacc[...] = jnp.zeros_like(dgacc)

        dact = _dot_nt(dx_ref[...].astype(BF16), wd_ref[...])
        gt = gate_ref[...].astype(F32)
        sg = _sigmoid(gt)
        dgate = (dact * up_ref[...].astype(F32) * (sg * (1.0 + gt * (1.0 - sg)))).astype(BF16)
        dup = (dact * gt * sg).astype(BF16)
        dgate_ref[...] = dgate
        dup_ref[...] = dup
        acc[...] += _dot_nt(dgate, wg_ref[...]) + _dot_nt(dup, wu_ref[...])

        @pl.when(s == N_CHIPS - 1)
        def _():
            dx, dg = _rms_bwd(x_ref[...], g_ref[...], acc[...])
            dx1_ref[...] = dx_ref[...] + dx
            dgacc[...] += dg

        @pl.when(jnp.logical_and(i == pl.num_programs(0) - 1, s == N_CHIPS - 1))
        def _():
            dg_ref[...] = jnp.sum(dgacc[...], axis=0, keepdims=True)

    tok = pl.BlockSpec((TM, D_MODEL), lambda i, s: (i, 0))
    ffs = pl.BlockSpec((None, TM, fs), lambda i, s: (s, i, 0))
    vec = pl.BlockSpec((1, D_MODEL), lambda i, s: (0, 0))
    return _call(
        body, name="ffn_bwd", grid=(L // TM, N_CHIPS),
        in_specs=[tok, tok, vec, ffs, ffs,
                  pl.BlockSpec((None, D_MODEL, fs), lambda i, s: (s, 0, 0)),
                  pl.BlockSpec((None, D_MODEL, fs), lambda i, s: (s, 0, 0)),
                  pl.BlockSpec((None, fs, D_MODEL), lambda i, s: (s, 0, 0))],
        out_specs=[tok, ffs, ffs, vec],
        out_shape=[_sds((L, D_MODEL), F32), _sds((N_CHIPS, L, fs), BF16), _sds((N_CHIPS, L, fs), BF16),
                   _sds((1, D_MODEL), F32)],
        scratch=[pltpu.VMEM((TM, D_MODEL), F32), pltpu.VMEM((SUBLANES, D_MODEL), F32)],
        sem=("arbitrary", "arbitrary"), rider=rider)(dx2, x1, g, gate, up, w_g, w_u, w_d)


def _wgrad(a, b, *, name, grid_kn, a_spec, b_spec, out_shape, out_spec):
    L = a.shape[-2]
    nl = L // TL_WGRAD

    def body(a_ref, b_ref, o_ref):
        @pl.when(pl.program_id(2) == 0)
        def _():
            o_ref[...] = jnp.zeros_like(o_ref)
        o_ref[...] += _dot_tn(a_ref[...].astype(BF16), b_ref[...].astype(BF16))

    return _call(body, name=name, grid=(*grid_kn, nl), in_specs=[a_spec, b_spec], out_specs=out_spec,
                 out_shape=out_shape, sem=("parallel", "parallel", "arbitrary"))(a, b)


def _wgrad_cols(a, b, name):
    K, N = a.shape[1], b.shape[1]
    ns = N // N_CHIPS
    if N * K * 4 <= 4 * 1024 * 1024:
        L = a.shape[0]

        def body(a_ref, b_ref, o_ref):
            @pl.when(pl.program_id(0) == 0)
            def _():
                o_ref[...] = jnp.zeros_like(o_ref)
            av = a_ref[...].astype(BF16)
            for s in range(N_CHIPS):
                o_ref[s] += _dot_tn(av, b_ref[:, s * ns:(s + 1) * ns].astype(BF16))

        return _call(body, name=name, grid=(L // TL_WGRAD,),
                     in_specs=[pl.BlockSpec((TL_WGRAD, K), lambda t: (t, 0)),
                               pl.BlockSpec((TL_WGRAD, N), lambda t: (t, 0))],
                     out_specs=pl.BlockSpec((N_CHIPS, K, ns), lambda t: (0, 0, 0)),
                     out_shape=_sds((N_CHIPS, K, ns), F32), sem=("arbitrary",))(a, b)
    tn = ns // 2 if ns % (2 * LANES) == 0 else ns
    nj = ns // tn
    return _wgrad(a, b, name=name, grid_kn=(1, N_CHIPS * nj),
                  a_spec=pl.BlockSpec((TL_WGRAD, K), lambda i, j, t: (t, 0)),
                  b_spec=pl.BlockSpec((TL_WGRAD, tn), lambda i, j, t: (t, j)),
                  out_shape=_sds((N_CHIPS, K, ns), F32),
                  out_spec=pl.BlockSpec((None, K, tn), lambda i, j, t: (j // nj, 0, j % nj)))


def _wgrad_full(a, b, name):
    K, N = a.shape[1], b.shape[1]
    return _wgrad(a, b, name=name, grid_kn=(1, 1),
                  a_spec=pl.BlockSpec((TL_WGRAD, K), lambda i, j, t: (t, 0)),
                  b_spec=pl.BlockSpec((TL_WGRAD, N), lambda i, j, t: (t, 0)),
                  out_shape=_sds((K, N), F32), out_spec=pl.BlockSpec((K, N), lambda i, j, t: (0, 0)))


def _wgrad_ff_cols(a, b, name):
    K, fs = a.shape[1], b.shape[2]
    return _wgrad(a, b, name=name, grid_kn=(1, N_CHIPS),
                  a_spec=pl.BlockSpec((TL_WGRAD, K), lambda i, j, t: (t, 0)),
                  b_spec=pl.BlockSpec((None, TL_WGRAD, fs), lambda i, j, t: (j, t, 0)),
                  out_shape=_sds((N_CHIPS, K, fs), F32),
                  out_spec=pl.BlockSpec((None, K, fs), lambda i, j, t: (j, 0, 0)))


def _wgrad_ff_rows(a, b, name):
    fs, N = a.shape[2], b.shape[1]
    return _wgrad(a, b, name=name, grid_kn=(N_CHIPS, 1),
                  a_spec=pl.BlockSpec((None, TL_WGRAD, fs), lambda i, j, t: (i, t, 0)),
                  b_spec=pl.BlockSpec((TL_WGRAD, N), lambda i, j, t: (t, 0)),
                  out_shape=_sds((N_CHIPS, fs, N), F32),
                  out_spec=pl.BlockSpec((None, fs, N), lambda i, j, t: (i, 0, 0)))


def _mix_bwd(dx, z, aout, sa, sb, ypre, w_ap, w_ga, w_gb, w_out):
    L = dx.shape[0]
    cs = D_MODEL // N_CHIPS
    ga_col = (3 * N_GROUPS * ATTN_WIDTH + SSM_WIDTH) // D_MODEL

    def body(dx_ref, ga_ref, gs_ref, aout_ref, sa_ref, sb_ref, ypre_ref, wap_ref, wga_ref, wgb_ref, wout_ref,
             dgates_ref, da_ref, gy_ref, daout_ref, dsa_ref, dsb_ref):
        dmix = _dot_nt(dx_ref[...].astype(BF16), wout_ref[...])
        sig_a = _sigmoid(ga_ref[...].astype(F32))
        sig_s = _sigmoid(gs_ref[...].astype(F32))
        a_out = aout_ref[...].astype(F32)
        s_a = sa_ref[...].astype(F32)
        sig_b = _sigmoid(sb_ref[...].astype(F32))
        s_out = s_a * sig_b
        daout = (dmix * sig_a).astype(BF16)
        daout_ref[...] = daout
        dgates_ref[:, :D_MODEL] = (dmix * a_out * sig_a * (1.0 - sig_a)).astype(BF16)
        dgates_ref[:, D_MODEL:] = (dmix * s_out * sig_s * (1.0 - sig_s)).astype(BF16)
        ds_out = dmix * sig_s
        dsa = (ds_out * sig_b).astype(BF16)
        dsb = (ds_out * s_a * sig_b * (1.0 - sig_b)).astype(BF16)
        dsa_ref[...] = dsa
        dsb_ref[...] = dsb
        da = jnp.zeros((TM_MIX, ATTN_WIDTH), F32)
        dy = jnp.zeros((TM_MIX, SSM_WIDTH), F32)
        for s in range(N_CHIPS):
            cols = slice(s * cs, (s + 1) * cs)
            da += _dot_nt(daout[:, cols], wap_ref[s])
            dy += _dot_nt(dsa[:, cols], wga_ref[s]) + _dot_nt(dsb[:, cols], wgb_ref[s])
        gy_ref[...] = dy * _gelu_grad(ypre_ref[...])
        da_ref[...] = da

    tok = lambda w: pl.BlockSpec((TM_MIX, w), lambda i: (i, 0))
    wsm = pl.BlockSpec((N_CHIPS, ATTN_WIDTH, cs), lambda i: (0, 0, 0))
    return _call(
        body, name="mix_bwd", grid=(L // TM_MIX,),
        in_specs=[tok(D_MODEL), pl.BlockSpec((TM_MIX, D_MODEL), lambda i: (i, ga_col)),
                  pl.BlockSpec((TM_MIX, D_MODEL), lambda i: (i, ga_col + 1)),
                  tok(D_MODEL), tok(D_MODEL), tok(D_MODEL), tok(SSM_WIDTH),
                  wsm, wsm, wsm, pl.BlockSpec((D_MODEL, D_MODEL), lambda i: (0, 0))],
        out_specs=[tok(2 * D_MODEL), tok(ATTN_WIDTH), tok(SSM_WIDTH)] + [tok(D_MODEL)] * 3,
        out_shape=[_sds((L, 2 * D_MODEL), BF16), _sds((L, ATTN_WIDTH), F32), _sds((L, SSM_WIDTH), F32)]
                  + [_sds((L, D_MODEL), BF16)] * 3,
        sem=("parallel",))(dx, z, z, aout, sa, sb, ypre, w_ap, w_ga, w_gb, w_out.reshape(D_MODEL, D_MODEL))


def _combine_bwd(da, os_, ls_):
    L = da.shape[0]

    def body(da_ref, o0, o1, o2, l0, l1, l2, hs_ref, p1_ref, p2_ref, pt1_ref, pt2_ref,
             do0, do1, do2, c0, c1, c2):
        o_tok, l_tok = _to_token_order((o0[...], o1[...], o2[...]), (l0[...], l1[...], l2[...]),
                                       (None, pt1_ref[...], pt2_ref[...]))
        w = _combine_weights(*l_tok)
        dav = da_ref[...]
        hs = hs_ref[...]
        tbar = sum(wg * _head_sum(dav * og, hs) for wg, og in zip(w, o_tok))
        for wg, pm, do_ref, c_ref in zip(w, (None, p1_ref[...], p2_ref[...]), (do0, do1, do2), (c0, c1, c2)):
            dog = (wg * dav).astype(BF16)
            cg = -wg * tbar
            do_ref[...] = dog if pm is None else _dot(pm, dog).astype(BF16)
            c_ref[...] = cg if pm is None else _permute(pm, cg)

    tile = pl.BlockSpec((DL_TILE, ATTN_WIDTH), lambda i: (i, 0))
    mat = pl.BlockSpec((DL_TILE, DL_TILE), lambda i: (0, 0))
    p1, p2 = _perm_matrix(ATTN_PATTERN[1][1]), _perm_matrix(ATTN_PATTERN[2][1])
    outs = _call(body, name="combine_bwd", grid=(L // DL_TILE,), in_specs=[tile] * 7 + [mat] * 5,
                 out_specs=[tile] * 6,
                 out_shape=[_sds((L, ATTN_WIDTH), BF16)] * 3 + [_sds((L, ATTN_WIDTH), F32)] * 3,
                 sem=("parallel",))(da, *os_, *ls_, _head_sum_matrix(), p1, p2, p1.T, p2.T)
    return outs[:3], outs[3:]


def _attn_bwd(qs, ks, v, do, l, c, gi):
    L = qs.shape[0]
    _, d = ATTN_PATTERN[gi]
    nb = L // (d * BLK)

    def body(q0_ref, q1_ref, k_ref, v_ref, do0_ref, do1_ref, l0_ref, l1_ref, c0_ref, c1_ref,
             dq_ref, dk_ref, dv_ref, carry):
        n = pl.program_id(1)

        @pl.when(n == 0)
        def _():
            carry[...] = jnp.zeros_like(carry)

        qi = lax.broadcasted_iota(jnp.int32, (2 * BLK, BLK), 0)
        kj = lax.broadcasted_iota(jnp.int32, (2 * BLK, BLK), 1)
        first = qi < BLK
        mask = jnp.logical_and(jnp.where(first, qi, kj) >= jnp.where(first, kj, qi - BLK),
                               qi < jnp.where(n < nb - 1, 2 * BLK, BLK))
        q2 = jnp.concatenate([_dl_read(q0_ref), _dl_read(q1_ref)], axis=0)
        do2 = jnp.concatenate([_dl_read(do0_ref), _dl_read(do1_ref)], axis=0)
        l2 = jnp.concatenate([_dl_read(l0_ref), _dl_read(l1_ref)], axis=0)
        c2 = jnp.concatenate([_dl_read(c0_ref), _dl_read(c1_ref)], axis=0)
        k = _dl_read(k_ref)
        v_ = _dl_read(v_ref)
        dq_parts, dk_parts, dv_parts = [], [], []
        for hp in range(N_HEADS // 2):
            ls = slice(hp * LANES, (hp + 1) * LANES)
            qp, dop, kp_, vp_ = q2[:, ls], do2[:, ls], k[:, ls], v_[:, ls]
            dq2 = jnp.zeros((2 * BLK, LANES), F32)
            dkp = jnp.zeros((BLK, LANES), F32)
            dvp = jnp.zeros((BLK, LANES), F32)
            for e, he in enumerate(_pair_masks()):
                col = slice(hp * LANES + e * HEAD_DIM, hp * LANES + e * HEAD_DIM + 1)
                ke = jnp.where(he, kp_, 0)
                p = jnp.where(mask, jnp.exp(_dot_nt(qp, ke) - l2[:, col]), 0.0)
                ds = (p * (_dot_nt(dop, jnp.where(he, vp_, 0)) + c2[:, col])).astype(BF16)
                dvp += _dot_tn(p.astype(BF16), jnp.where(he, dop, 0))
                dkp += _dot_tn(ds, jnp.where(he, qp, 0))
                dq2 += _dot(ds, ke)
            dq_parts.append((dq2[:BLK] + carry[:, ls]).astype(BF16))
            carry[:, ls] = dq2[BLK:]
            dk_parts.append(dkp.astype(BF16))
            dv_parts.append(dvp.astype(BF16))
        _dl_write(dq_ref, jnp.concatenate(dq_parts, axis=1))
        _dl_write(dk_ref, jnp.concatenate(dk_parts, axis=1))
        _dl_write(dv_ref, jnp.concatenate(dv_parts, axis=1))

    cur = _dl_spec(d, ATTN_WIDTH, lambda n: n)
    nxt = _dl_spec(d, ATTN_WIDTH, lambda n: jnp.minimum(n + 1, nb - 1))
    view = lambda t: _dl_view(t, d)
    outs = _call(
        body, name=f"attn_bwd_g{gi}", grid=(d, nb),
        in_specs=[cur, nxt, cur, cur, cur, nxt, cur, nxt, cur, nxt], out_specs=[cur, cur, cur],
        out_shape=[_sds(view(qs).shape, BF16)] * 3, scratch=[pltpu.VMEM((BLK, ATTN_WIDTH), F32)],
        sem=("parallel", "arbitrary"))(view(qs), view(qs), view(ks), view(v), view(do), view(do), view(l), view(l),
                                       view(c), view(c))
    return [t.reshape(L, ATTN_WIDTH) for t in outs]


def _qkv_post(z, dqkv, du, dgates, gq_t, gk_t):
    L = z.shape[0]
    qkv_w = N_GROUPS * ATTN_WIDTH

    def body(zq_ref, zk_ref, gq_ref, gk_ref, hs_ref, pt1_ref, pt2_ref, du_ref, dgates_ref, *rest):
        dl_refs, (dz_ref, dgq_ref, dgk_ref) = rest[:9], rest[9:]

        @pl.when(pl.program_id(0) == 0)
        def _():
            dgq_ref[...] = jnp.zeros_like(dgq_ref)
            dgk_ref[...] = jnp.zeros_like(dgk_ref)

        hs = hs_ref[...]
        pts = (None, pt1_ref[...], pt2_ref[...])

        def rows8(t):
            return jnp.sum(t.reshape(DL_TILE // SUBLANES, SUBLANES, ATTN_WIDTH), axis=0)

        def norm_bwd(x, gain, dn):
            r = lax.rsqrt(_head_sum(x * x, hs) * (1.0 / HEAD_DIM) + EPS)
            xh = x * r
            dh = dn * gain
            return r * (dh - xh * (_head_sum(dh * xh, hs) * (1.0 / HEAD_DIM))), rows8(dn * xh)

        for g in range(N_GROUPS):
            tok = [t[...].astype(F32) if pts[g] is None else _dot(pts[g], t[...]) for t in dl_refs[3 * g:3 * g + 3]]
            cols = slice(g * ATTN_WIDTH, (g + 1) * ATTN_WIDTH)
            dq, pq = norm_bwd(zq_ref[:, cols].astype(F32), gq_ref[...] * SCALE, tok[0])
            dk, pk_ = norm_bwd(zk_ref[:, cols].astype(F32), gk_ref[...], tok[1])
            dgq_ref[...] += pq * SCALE
            dgk_ref[...] += pk_
            dz_ref[:, cols] = dq.astype(BF16)
            dz_ref[:, qkv_w + g * ATTN_WIDTH:qkv_w + (g + 1) * ATTN_WIDTH] = dk.astype(BF16)
            dz_ref[:, 2 * qkv_w + g * ATTN_WIDTH:2 * qkv_w + (g + 1) * ATTN_WIDTH] = tok[2].astype(BF16)
        dz_ref[:, 3 * qkv_w:3 * qkv_w + SSM_WIDTH] = du_ref[...]
        dz_ref[:, 3 * qkv_w + SSM_WIDTH:] = dgates_ref[...]

    tile = lambda w: pl.BlockSpec((DL_TILE, w), lambda i: (i, 0))
    mat = pl.BlockSpec((DL_TILE, DL_TILE), lambda i: (0, 0))
    vec = pl.BlockSpec((1, ATTN_WIDTH), lambda i: (0, 0))
    acc = pl.BlockSpec((SUBLANES, ATTN_WIDTH), lambda i: (0, 0))
    flat = [t for grp in dqkv for t in grp]
    return _call(
        body, name="qkv_post", grid=(L // DL_TILE,),
        in_specs=[tile(qkv_w), pl.BlockSpec((DL_TILE, qkv_w), lambda i: (i, 1)), vec, vec, mat, mat, mat,
                  tile(SSM_WIDTH), tile(2 * D_MODEL)] + [tile(ATTN_WIDTH)] * 9,
        out_specs=[tile(IN_COLS), acc, acc],
        out_shape=[_sds((L, IN_COLS), BF16), _sds((SUBLANES, ATTN_WIDTH), F32), _sds((SUBLANES, ATTN_WIDTH), F32)],
        sem=("arbitrary",))(z, z, gq_t, gk_t, _head_sum_matrix(), _perm_matrix(ATTN_PATTERN[1][1]).T,
                            _perm_matrix(ATTN_PATTERN[2][1]).T, du, dgates, *flat)


def _scan_rev_grad(sre, sim, rre, rim, are_ref, aim_ref, k0, init, seed_re, seed_im):
    ar = [_bcast(are_ref, k0 + kk) for kk in range(SCAN_GROUP)]
    ai = [-_bcast(aim_ref, k0 + kk) for kk in range(SCAN_GROUP)]

    def update(i, xprev, carry):
        out = []
        for kk in range(SCAN_GROUP):
            k = k0 + kk
            lr, li, dr, di = carry[4 * kk:4 * kk + 4]
            nr = ar[kk] * lr - ai[kk] * li + rre[k, _rows(i), :]
            ni = ar[kk] * li + ai[kk] * lr + rim[k, _rows(i), :]
            rre[k, _rows(i), :] = nr
            rim[k, _rows(i), :] = ni
            xr, xi = xprev(k)
            out += [nr, ni, dr + xr * nr + xi * ni, di + xr * ni - xi * nr]
        return tuple(out)

    def step(t, carry):
        i = SSM_TC - 1 - t
        return update(i, lambda k: (sre[k, _rows(i - 1), :], sim[k, _rows(i - 1), :]), carry)

    zero = jnp.zeros((SSM_SUB, LANES), F32)
    flat = []
    for re, im in init:
        flat += [re, im, zero, zero]
    res = lax.fori_loop(0, SSM_TC - 1, step, tuple(flat))
    res = update(0, lambda k: (seed_re[k], seed_im[k]), res)
    return [(res[4 * kk + 2], res[4 * kk + 3]) for kk in range(SCAN_GROUP)]


def _ssm_bwd(z, gy, pk, dskip, sd_re, sd_im, rider=None):
    L = z.shape[0]
    nb = L // SSM_TB
    ucol = (3 * N_GROUPS * ATTN_WIDTH) // SSM_WIDTH
    nwin = N_SLAB // SLABS_PER_WIN

    def body(u_ref, gy_ref, are_ref, aim_ref, a64re_ref, a64im_ref, bwre_ref, bwim_ref, cwre_ref, cwim_ref, d_ref,
             sdre_ref, sdim_ref,
             du_ref, dare_ref, daim_ref, dbre_ref, dbim_ref, dcre_ref, dcim_ref, dd_ref,
             sre, sim, rre, rim, carry_re, carry_im, ends_re, ends_im, seed_re, seed_im):
        @pl.when(pl.program_id(0) == 0)
        def _():
            carry_re[...] = jnp.zeros_like(carry_re)
            carry_im[...] = jnp.zeros_like(carry_im)
            for ref in (dare_ref, daim_ref, dbre_ref, dbim_ref, dcre_ref, dcim_ref, dd_ref):
                ref[...] = jnp.zeros_like(ref)

        u = u_ref[...]
        gyv = gy_ref[...]
        gyb = gyv.astype(BF16)
        _ssm_fill(u, bwre_ref, bwim_ref, sre, sim)
        for k in range(N_SLAB):
            gw = gyb[:, (k // SLABS_PER_WIN) * LANES:(k // SLABS_PER_WIN + 1) * LANES]
            gr = _dot_nt(gw, cwre_ref[k])
            gi_ = -_dot_nt(gw, cwim_ref[k])
            for j in range(SSM_SUB):
                rre[k, j * SSM_PITCH:j * SSM_PITCH + SSM_TC, :] = gr[j * SSM_TC:(j + 1) * SSM_TC, :]
                rim[k, j * SSM_PITCH:j * SSM_PITCH + SSM_TC, :] = gi_[j * SSM_TC:(j + 1) * SSM_TC, :]
        zero = jnp.zeros((SSM_SUB, LANES), F32)
        for k0 in range(0, N_SLAB, SCAN_GROUP):
            grp = range(k0, k0 + SCAN_GROUP)
            _scan(sre, sim, are_ref, aim_ref, k0, [(sdre_ref[k], sdim_ref[k]) for k in grp],
                  reverse=False, store=True)
            ends = _scan(rre, rim, are_ref, aim_ref, k0, [(zero, zero)] * SCAN_GROUP, reverse=True, store=False,
                         sign=-1.0)
            for kk, k in enumerate(grp):
                ends_re[k] = ends[kk][0]
                ends_im[k] = ends[kk][1]
            for k in grp:
                _ssm_seeds(ends_re, ends_im, a64re_ref, a64im_ref, carry_re, carry_im, seed_re, seed_im, k,
                           reverse=True, sign=-1.0)
            das = _scan_rev_grad(sre, sim, rre, rim, are_ref, aim_ref, k0,
                                 [(seed_re[k], seed_im[k]) for k in grp], sdre_ref, sdim_ref)
            for kk, k in enumerate(grp):
                dare_ref[k] += das[kk][0]
                daim_ref[k] += das[kk][1]
        for w in range(nwin):
            cols = slice(w * LANES, (w + 1) * LANES)
            uw = u[:, cols]
            gw = gyb[:, cols]
            acc = gyv[:, cols] * d_ref[:, cols]
            for kk in range(SLABS_PER_WIN):
                k = w * SLABS_PER_WIN + kk
                lr = _slab_rows(rre, k).astype(BF16)
                li = _slab_rows(rim, k).astype(BF16)
                acc += _dot_nt(lr, bwre_ref[k]) + _dot_nt(li, bwim_ref[k])
                dbre_ref[k] += _dot_tn(uw, lr)
                dbim_ref[k] += _dot_tn(uw, li)
                dcre_ref[k] += _dot_tn(_slab_rows(sre, k).astype(BF16), gw)
                dcim_ref[k] -= _dot_tn(_slab_rows(sim, k).astype(BF16), gw)
            du_ref[:, cols] = acc.astype(BF16)
        dd_ref[...] += jnp.sum((gyv * u.astype(F32)).reshape(SSM_TB // SUBLANES, SUBLANES, SSM_WIDTH), axis=0)

    c2, c3 = _ssm_specs_consts()
    rev = lambda b: nb - 1 - b
    seed_spec = pl.BlockSpec((None, N_SLAB, SSM_SUB, LANES), lambda b: (rev(b), 0, 0, 0))
    tile_out = pl.BlockSpec((N_SLAB, SSM_SUB, LANES), lambda b: (0, 0, 0))
    small = pltpu.VMEM((N_SLAB, LANES), F32)
    tile = pltpu.VMEM((N_SLAB, SSM_SUB, LANES), F32)
    return _call(
        body, name="ssm_bwd", grid=(nb,),
        in_specs=[pl.BlockSpec((SSM_TB, SSM_WIDTH), lambda b: (rev(b), ucol)),
                  pl.BlockSpec((SSM_TB, SSM_WIDTH), lambda b: (rev(b), 0)),
                  c2, c2, c2, c2, c3, c3, c3, c3, pl.BlockSpec((1, SSM_WIDTH), lambda b: (0, 0)),
                  seed_spec, seed_spec],
        out_specs=[pl.BlockSpec((SSM_TB, SSM_WIDTH), lambda b: (rev(b), 0)), tile_out, tile_out, c3, c3, c3, c3,
                   pl.BlockSpec((SUBLANES, SSM_WIDTH), lambda b: (0, 0))],
        out_shape=[_sds((L, SSM_WIDTH), BF16), _sds((N_SLAB, SSM_SUB, LANES), F32),
                   _sds((N_SLAB, SSM_SUB, LANES), F32)] + [_sds((N_SLAB, LANES, LANES), F32)] * 4
                  + [_sds((SUBLANES, SSM_WIDTH), F32)],
        scratch=_ssm_scratch() + _ssm_scratch() + [small, small, tile, tile, tile, tile],
        sem=("arbitrary",), rider=rider)(z, gy, pk["a_re"], pk["a_im"], pk["a64_re"], pk["a64_im"],
                            pk["bw_re"].astype(BF16), pk["bw_im"].astype(BF16),
                            pk["cw_re"].astype(BF16), pk["cw_im"].astype(BF16), dskip, sd_re, sd_im)


def _in_proj_bwd(dz, w, x, g, dres, rider=None):
    L = x.shape[0]
    ns = w.shape[2]
    tn = ns
    nj = ns // tn
    nt = N_CHIPS * nj
    TM = TM_PROJ

    def body(dz_ref, w_ref, x_ref, g_ref, dres_ref, dx_ref, dg_ref, acc, dgacc):
        i, j = pl.program_id(0), pl.program_id(1)

        @pl.when(j == 0)
        def _():
            acc[...] = jnp.zeros_like(acc)

        @pl.when(jnp.logical_and(i == 0, j == 0))
        def _():
            dgacc[...] = jnp.zeros_like(dgacc)

        acc[...] += _dot_nt(dz_ref[...], w_ref[...])

        @pl.when(j == nt - 1)
        def _():
            dx, dg = _rms_bwd(x_ref[...], g_ref[...], acc[...])
            dx_ref[...] = dres_ref[...] + dx
            dgacc[...] += dg

        @pl.when(jnp.logical_and(i == pl.num_programs(0) - 1, j == nt - 1))
        def _():
            dg_ref[...] = jnp.sum(dgacc[...], axis=0, keepdims=True)

    tok = pl.BlockSpec((TM, D_MODEL), lambda i, j: (i, 0))
    vec = pl.BlockSpec((1, D_MODEL), lambda i, j: (0, 0))
    return _call(
        body, name="in_proj_bwd", grid=(L // TM, nt),
        in_specs=[pl.BlockSpec((TM, tn), lambda i, j: (i, j)),
                  pl.BlockSpec((None, D_MODEL, tn), lambda i, j: (j // nj, 0, j % nj)), tok, vec, tok],
        out_specs=[tok, vec],
        out_shape=[_sds((L, D_MODEL), F32), _sds((1, D_MODEL), F32)],
        scratch=[pltpu.VMEM((TM, D_MODEL), F32), pltpu.VMEM((SUBLANES, D_MODEL), F32)],
        sem=("arbitrary", "arbitrary"), rider=rider)(dz, w, x, g, dres)


SSM_PARAMS = ("lambda_re", "lambda_im", "log_dt", "b_re", "b_im", "c_re", "c_im")


def _layer_bwd(dx2, sv, p, pending=None):
    g = {}
    outs = _ffn_bwd(dx2, sv["x1"], p["g_ffn"], sv["gate"], sv["up"], p["w_ffn_gate"], p["w_ffn_up"],
                    p["w_ffn_down"], _scatter_rider(pending[:1]) if pending else None)
    (dx1, dgate, dup, g["g_ffn"]), arrived = outs[:4], list(outs[4:])
    g["w_ffn_gate"] = _wgrad_ff_cols(sv["h2"], dgate, "wgrad_ffn_gate")
    g["w_ffn_up"] = _wgrad_ff_cols(sv["h2"], dup, "wgrad_ffn_up")
    g["w_ffn_down"] = _wgrad_ff_rows(sv["act"], dx2, "wgrad_ffn_down")

    dgates, da, gy, daout, dsa, dsb = _mix_bwd(dx1, sv["z"], sv["aout"], sv["sa"], sv["sb"], sv["ypre"],
                                               p["w_attn_proj"], p["w_glu_a"], p["w_glu_b"], p["w_out"])
    g["w_out"] = _wgrad_full(sv["mix"], dx1, "wgrad_out").reshape(N_CHIPS, D_MODEL // N_CHIPS, D_MODEL)
    g["w_attn_proj"] = _wgrad_cols(sv["a"], daout, "wgrad_attn_proj")
    g["w_glu_a"] = _wgrad_cols(sv["yact"], dsa, "wgrad_glu_a")
    g["w_glu_b"] = _wgrad_cols(sv["yact"], dsb, "wgrad_glu_b")

    outs = _ssm_bwd(sv["z"], gy, sv["pk"], p["d_skip"], sv["sd_re"], sv["sd_im"],
                    _scatter_rider(pending[1:]) if pending else None)
    du, da_re, da_im, dbw_re, dbw_im, dcw_re, dcw_im, dd = outs[:8]
    arrived += list(outs[8:])
    g["d_skip"] = jnp.sum(dd, axis=0, keepdims=True)
    _, pull = jax.vjp(_ssm_pack, *[p[n] for n in SSM_PARAMS])
    zeros = jnp.zeros((N_SLAB, LANES), F32)
    ct = dict(a_re=jnp.sum(da_re, axis=1), a_im=jnp.sum(da_im, axis=1), a64_re=zeros, a64_im=zeros,
              bw_re=dbw_re, bw_im=dbw_im, cw_re=dcw_re, cw_im=dcw_im)
    for n, v in zip(SSM_PARAMS, pull(ct)):
        g[n] = v

    dos, cs = _combine_bwd(da, sv["os"], sv["ls"])
    dqkv = [_attn_bwd(*sv["qkv"][gi], dos[gi], sv["ls"][gi], cs[gi], gi) for gi in range(N_GROUPS)]
    dz, gq8, gk8 = _qkv_post(sv["z"], dqkv, du, dgates, jnp.tile(p["g_q"], (1, N_HEADS)),
                             jnp.tile(p["g_k"], (1, N_HEADS)))
    g["g_q"] = jnp.sum(gq8.reshape(SUBLANES * N_HEADS, HEAD_DIM), axis=0, keepdims=True)
    g["g_k"] = jnp.sum(gk8.reshape(SUBLANES * N_HEADS, HEAD_DIM), axis=0, keepdims=True)
    g["w_in"] = _wgrad_cols(sv["h"], dz, "wgrad_in")
    outs = _in_proj_bwd(dz, p["w_in"], sv["x"], p["g_mix"], dx1, _swap_rider([g[n] for n in BIG]))
    (dx, g["g_mix"]), sibling = outs[:2], list(outs[2:])
    return dx, g, sibling, arrived


def _place():
    x, y, c = lax.axis_index("x"), lax.axis_index("y"), lax.axis_index("c")
    others = [(1 - x, y), (x, 1 - y), (1 - x, 1 - y)]
    return x, y, c, others


def _half(ref, hc):
    rows = ref.shape[-2] // 2
    idx = (slice(None),) * (len(ref.shape) - 2) + (pl.ds(hc * rows, rows), slice(None))
    return ref.at[idx]


def _comm_call(body, name, ins, out_shapes, n_remote, aliases=None):
    scratch = [pltpu.SemaphoreType.DMA((n_remote,)), pltpu.SemaphoreType.DMA((n_remote,))]
    return pl.pallas_call(
        body, name=name, in_specs=[ANY] * len(ins), out_specs=[ANY] * len(out_shapes), out_shape=out_shapes,
        scratch_shapes=scratch, input_output_aliases=aliases or {})(*ins)


def _cast_place(w, l, chip_idx):
    _, R, C = w.shape
    tr = R // 2

    def body(me_ref, w_ref, o_ref):
        o_ref[...] = w_ref[...].astype(BF16)

    return _call(body, name=f"cast_place_l{l}", grid=(R // tr,), prefetch=1,
                 in_specs=[pl.BlockSpec((None, tr, C), lambda i, me_ref: (l, i, 0))],
                 out_specs=pl.BlockSpec((None, tr, C), lambda i, me_ref: (me_ref[0], i, 0)),
                 out_shape=_sds((N_CHIPS, R, C), BF16), sem=("arbitrary",))(chip_idx, w)


def _in_place_rider(bufs, pairs):
    n = len(bufs)

    def copies(outs, send, recv, side):
        return [pltpu.make_async_remote_copy(src_ref=pair[side][0], dst_ref=pair[side][0], send_sem=send.at[k],
                                             recv_sem=recv.at[k], device_id=pair[side][1], device_id_type=MESH)
                for k, pair in enumerate(pairs(outs))]

    def start(ins, outs, send, recv):
        for cp in copies(outs, send, recv, 0):
            cp.start()

    def wait(ins, outs, send, recv):
        for cp in copies(outs, send, recv, 1):
            cp.wait_recv()
        for cp in copies(outs, send, recv, 0):
            cp.wait_send()

    return Rider(list(bufs), [_sds(b.shape, b.dtype) for b in bufs], 3 * n, start, wait, {a: a for a in range(n)})


def _gather_ici_rider(bufs):
    def pairs(outs):
        x, y, c, others = _place()
        return [((_half(o.at[2 * x + y], c), (cx, cy, c)), (_half(o.at[2 * cx + cy], c), (cx, cy, c)))
                for o in outs for cx, cy in others]
    return _in_place_rider(bufs, pairs)


def _gather_d2d_rider(bufs):
    def pairs(outs):
        x, y, c, others = _place()
        sib = (x, y, 1 - c)
        return [((_half(o.at[2 * cx + cy], c), sib), (_half(o.at[2 * cx + cy], 1 - c), sib))
                for o in outs for cx, cy in others]
    return _in_place_rider(bufs, pairs)


def _swap_rider(gs):
    n = len(gs)

    def copies(ins, outs, send, recv):
        x, y, c, _ = _place()
        return [pltpu.make_async_remote_copy(src_ref=_half(ins[a], 1 - c), dst_ref=outs[a], send_sem=send.at[a],
                                             recv_sem=recv.at[a], device_id=(x, y, 1 - c), device_id_type=MESH)
                for a in range(n)]

    def start(ins, outs, send, recv):
        for cp in copies(ins, outs, send, recv):
            cp.start()

    def wait(ins, outs, send, recv):
        for cp in copies(ins, outs, send, recv):
            cp.wait()

    outs = [_sds((g.shape[0], g.shape[1] // 2, g.shape[2]), g.dtype) for g in gs]
    return Rider(list(gs), outs, n, start, wait, {})


def _scatter_rider(ss):
    n = len(ss)

    def copies(ins, outs, send, recv):
        x, y, c, others = _place()
        return [pltpu.make_async_remote_copy(
            src_ref=ins[a].at[2 * cx + cy], dst_ref=outs[a].at[j], send_sem=send.at[3 * a + j],
            recv_sem=recv.at[3 * a + j], device_id=(cx, cy, c), device_id_type=MESH)
            for a in range(n) for j, (cx, cy) in enumerate(others)]

    def start(ins, outs, send, recv):
        for cp in copies(ins, outs, send, recv):
            cp.start()

    def wait(ins, outs, send, recv):
        for cp in copies(ins, outs, send, recv):
            cp.wait()

    outs = [_sds((N_CHIPS - 1,) + s.shape[1:], s.dtype) for s in ss]
    return Rider(list(ss), outs, 3 * n, start, wait, {})


def _run_rider(rider, name):
    n_in = len(rider.ins)

    def body(*refs):
        ins, outs = refs[:n_in], refs[n_in:n_in + len(rider.out_shapes)]
        send, recv = refs[n_in + len(rider.out_shapes):]
        rider.start(ins, outs, send, recv)
        rider.wait(ins, outs, send, recv)

    return _comm_call(body, name, rider.ins, rider.out_shapes, rider.n_sem, aliases=rider.aliases)


def _join_halves(bufs):
    n = len(bufs)

    def body(*refs):
        outs = refs[n:2 * n]
        send, recv = refs[2 * n:]
        x, y, c, _ = _place()

        def swap(a, hc):
            region = _half(outs[a], hc)
            return pltpu.make_async_remote_copy(src_ref=region, dst_ref=region, send_sem=send.at[a],
                                                recv_sem=recv.at[a], device_id=(x, y, 1 - c), device_id_type=MESH)

        cps = [swap(a, c) for a in range(n)]
        for cp in cps:
            cp.start()
        for a in range(n):
            swap(a, 1 - c).wait_recv()
        for cp in cps:
            cp.wait_send()

    outs = [_sds(b.shape, b.dtype) for b in bufs]
    return _comm_call(body, "join_halves", bufs, outs, n, aliases={a: a for a in range(n)})


def _gather_small(v):
    rows, n = v.shape

    def body(v_ref, out_ref, send, recv, lsem):
        x, y, c, others = _place()
        me, sibling = (x, y, c), (x, y, 1 - c)

        def blk(px, py, pc):
            return out_ref.at[pl.ds((4 * px + 2 * py + pc) * rows, rows), :]

        def copy(k, block, to, src=None):
            return pltpu.make_async_remote_copy(src_ref=blk(*block) if src is None else src, dst_ref=blk(*block),
                                                send_sem=send.at[k], recv_sem=recv.at[k], device_id=to,
                                                device_id_type=MESH)

        mine = pltpu.make_async_copy(v_ref, blk(*me), lsem)
        mine.start()
        first = [copy(0, me, sibling, src=v_ref)]
        first += [copy(1 + j, me, (*chip, c), src=v_ref) for j, chip in enumerate(others)]
        for cp in first:
            cp.start()
        passed = [copy(4 + j, (*chip, c), sibling) for j, chip in enumerate(others)]
        for j, chip in enumerate(others):
            copy(1 + j, (*chip, c), me).wait_recv()
            passed[j].start()
        copy(0, sibling, me).wait_recv()
        for j, chip in enumerate(others):
            copy(4 + j, (*chip, 1 - c), me).wait_recv()
        for cp in first + passed:
            cp.wait_send()
        mine.wait()

    return pl.pallas_call(
        body, name="gather_small", out_shape=_sds((8 * rows, n), v.dtype),
        in_specs=[pl.BlockSpec(memory_space=pltpu.VMEM)], out_specs=pl.BlockSpec(memory_space=pltpu.VMEM),
        scratch_shapes=[pltpu.SemaphoreType.DMA((7,)), pltpu.SemaphoreType.DMA((7,)), pltpu.SemaphoreType.DMA],
        compiler_params=pltpu.CompilerParams(vmem_limit_bytes=VMEM_LIMIT))(v)


def _add_half(g, p, c):
    _, R, C = g.shape
    half = R // 2

    def body(c_ref, g_ref, p_ref, o_ref):
        o_ref[...] = g_ref[...] + p_ref[...]

    blk = (None, half, C)
    return _call(body, name="add_half", grid=(N_CHIPS,), prefetch=1,
                 in_specs=[pl.BlockSpec(blk, lambda s, c_ref: (s, c_ref[0], 0)),
                           pl.BlockSpec(blk, lambda s, c_ref: (s, 0, 0))],
                 out_specs=pl.BlockSpec(blk, lambda s, c_ref: (s, 0, 0)),
                 out_shape=_sds((N_CHIPS, half, C), F32), sem=("arbitrary",))(c, g, p)


def _sum_owner(s, q, buf, l, me, c):
    _, half, C = s.shape
    tr = half // 2

    def body(me_ref, c_ref, s_ref, q0, q1, q2, buf_ref, o_ref):
        o_ref[...] = ((s_ref[...] + q0[...]) + q1[...]) + q2[...]

    blk = (None, tr, C)
    qspec = lambda j: pl.BlockSpec(blk, lambda i, me_ref, c_ref: (j, i, 0))
    return _call(body, name=f"sum_owner_l{l}", grid=(half // tr,), prefetch=2,
                 in_specs=[pl.BlockSpec(blk, lambda i, me_ref, c_ref: (me_ref[0], i, 0)),
                           qspec(0), qspec(1), qspec(2), ANY],
                 out_specs=pl.BlockSpec(blk, lambda i, me_ref, c_ref: (l, 2 * c_ref[0] + i, 0)),
                 out_shape=_sds(buf.shape, F32), sem=("arbitrary",), aliases={6: 0})(me, c, s, q, q, q, buf)


def _adamw_math(w, g, m, v):
    m = ADAM_B1 * m + (1.0 - ADAM_B1) * g
    v = ADAM_B2 * v + (1.0 - ADAM_B2) * (g * g)
    m_hat = m / (1.0 - ADAM_B1 ** ADAM_STEP)
    v_hat = v / (1.0 - ADAM_B2 ** ADAM_STEP)
    delta = -ADAM_LR * (m_hat / (jnp.sqrt(v_hat) + ADAM_EPS) + ADAM_WD * w)
    return delta, m, v


def _adamw(w, g, m, v):
    rows, C = w.shape
    tr = next(t for t in (256, 128, 64) if rows % t == 0)

    def body(w_ref, g_ref, m_ref, v_ref, d_ref, nm_ref, nv_ref):
        d, nm, nv = _adamw_math(w_ref[...], g_ref[...], m_ref[...], v_ref[...])
        d_ref[...] = d
        nm_ref[...] = nm
        nv_ref[...] = nv

    spec = pl.BlockSpec((tr, C), lambda i: (i, 0))
    return _call(body, name="adamw", grid=(rows // tr,), in_specs=[spec] * 4, out_specs=[spec] * 3,
                 out_shape=[_sds((rows, C), F32)] * 3, sem=("parallel",))(w, g, m, v)


def _small_update(gathered, w, m, v):
    _, rows, n = gathered.shape
    tr = rows // 7

    def body(ga_ref, w_ref, m_ref, v_ref, g_ref, d_ref, nm_ref, nv_ref):
        g = ga_ref[0]
        for k in range(1, 8):
            g = g + ga_ref[k]
        d, nm, nv = _adamw_math(w_ref[...], g, m_ref[...], v_ref[...])
        g_ref[...] = g
        d_ref[...] = d
        nm_ref[...] = nm
        nv_ref[...] = nv

    spec = pl.BlockSpec((tr, n), lambda i: (i, 0))
    return _call(body, name="small_update", grid=(rows // tr,),
                 in_specs=[pl.BlockSpec((8, tr, n), lambda i: (0, i, 0)), spec, spec, spec], out_specs=[spec] * 4,
                 out_shape=[_sds((rows, n), F32)] * 4, sem=("parallel",))(gathered, w, m, v)


WEIGHTS = ("g_mix", "w_in", "g_q", "g_k", "w_attn_proj", "lambda_re", "lambda_im", "log_dt", "b_re", "b_im",
           "c_re", "c_im", "d_skip", "w_glu_a", "w_glu_b", "w_out", "g_ffn", "w_ffn_gate", "w_ffn_up", "w_ffn_down")
BIG = ("w_in", "w_attn_proj", "w_glu_a", "w_glu_b", "w_out", "w_ffn_gate", "w_ffn_up", "w_ffn_down")
SMALL = tuple(n for n in WEIGHTS if n not in BIG)
ROW_VECTORS = ("g_mix", "g_q", "g_k", "d_skip", "g_ffn")
PACK_QUANTUM = LANES * SUBLANES * 7


def _pack_small(parts, extra):
    flat = jnp.concatenate([parts[n].reshape(-1).astype(F32) for n in SMALL] + [extra.reshape(-1)])
    pad = -flat.shape[0] % PACK_QUANTUM
    return jnp.pad(flat, (0, pad)).reshape(-1, LANES)


def _unpack_small(packed, like):
    flat = packed.reshape(-1)
    out, at = {}, 0
    for n in SMALL:
        size = math.prod(like[n].shape)
        out[n] = flat[at:at + size].reshape(like[n].shape)
        at += size
    return out, flat[at]


def kernel(x, g_mix, w_in, g_q, g_k, w_attn_proj, lambda_re, lambda_im, log_dt, b_re, b_im, c_re, c_im, d_skip, w_glu_a, w_glu_b, w_out, g_ffn, w_ffn_gate, w_ffn_up, w_ffn_down, loss_target, m_g_mix, m_w_in, m_g_q, m_g_k, m_w_attn_proj, m_lambda_re, m_lambda_im, m_log_dt, m_b_re, m_b_im, m_c_re, m_c_im, m_d_skip, m_w_glu_a, m_w_glu_b, m_w_out, m_g_ffn, m_w_ffn_gate, m_w_ffn_up, m_w_ffn_down, v_g_mix, v_w_in, v_g_q, v_g_k, v_w_attn_proj, v_lambda_re, v_lambda_im, v_log_dt, v_b_re, v_b_im, v_c_re, v_c_im, v_d_skip, v_w_glu_a, v_w_glu_b, v_w_out, v_g_ffn, v_w_ffn_gate, v_w_ffn_up, v_w_ffn_down):
    given = dict(locals())
    W = {n: given[n] for n in WEIGHTS}
    M = {n: given["m_" + n] for n in WEIGHTS}
    V = {n: given["v_" + n] for n in WEIGHTS}
    depth = g_mix.shape[0]
    xl = x.reshape(x.shape[-2:])
    target = loss_target.reshape(loss_target.shape[-2:])
    c_idx = lax.axis_index("c").astype(jnp.int32).reshape(1)
    chip_idx = (2 * lax.axis_index("x") + lax.axis_index("y")).astype(jnp.int32).reshape(1)

    place = lambda l: [_cast_place(W[n], l, chip_idx) for n in BIG]
    full = _run_rider(_gather_d2d_rider(_run_rider(_gather_ici_rider(place(0)), "gather_ici")), "gather_d2d")
    params, saved, h = [], [], xl
    for l in range(depth):
        p = dict(zip(BIG, full))
        for n in SMALL:
            p[n] = W[n][l][None] if n in ROW_VECTORS else W[n][l]
        params.append(p)
        h, sv, full = _layer_fwd(h, p, place(l + 1) if l + 1 < depth else None)
        saved.append(sv)
    dx, loss_part = _loss_head(h, target)

    owned = [lax.empty(W[n].shape, F32) for n in BIG]
    small_grads = [None] * depth
    pending = None
    for l in reversed(range(depth)):
        dx, g, sibling, arrived = _layer_bwd(dx, saved[l], params[l], pending)
        if pending is not None:
            owned = [_sum_owner(s, q, buf, l + 1, chip_idx, c_idx) for s, q, buf in zip(pending, arrived, owned)]
        pending = [_add_half(g[n], s, c_idx) for n, s in zip(BIG, sibling)]
        small_grads[l] = g
    arrived = _run_rider(_scatter_rider(pending), "scatter_to_owners")
    owned = [_sum_owner(s, q, buf, 0, chip_idx, c_idx) for s, q, buf in zip(pending, arrived, owned)]
    reduced = dict(zip(BIG, _join_halves(owned)))

    grads, delta, new_m, new_v = {}, {}, {}, {}
    for n in BIG:
        shape = W[n].shape
        two_d = (shape[0] * shape[1], shape[2])
        d, nm, nv = _adamw(W[n].reshape(two_d), reduced[n].reshape(two_d), M[n].reshape(two_d), V[n].reshape(two_d))
        grads[n], delta[n], new_m[n], new_v[n] = reduced[n], d.reshape(shape), nm.reshape(shape), nv.reshape(shape)

    stacked = {n: jnp.stack([small_grads[l][n] for l in range(depth)]) for n in SMALL}
    zero = jnp.zeros((1,), F32)
    packed = _pack_small(stacked, loss_part)
    gathered = _gather_small(packed).reshape(8, *packed.shape)
    gs, ds, nms, nvs = _small_update(gathered, _pack_small(W, zero), _pack_small(M, zero), _pack_small(V, zero))
    sg, loss = _unpack_small(gs, W)
    sd, _ = _unpack_small(ds, W)
    sm, _ = _unpack_small(nms, W)
    sv_, _ = _unpack_small(nvs, W)
    for n in SMALL:
        grads[n], delta[n], new_m[n], new_v[n] = sg[n], sd[n], sm[n], sv_[n]

    return (loss, dx.reshape(x.shape), *[grads[n] for n in WEIGHTS], *[delta[n] for n in WEIGHTS],
            *[new_m[n] for n in WEIGHTS], *[new_v[n] for n in WEIGHTS])
```

```python
import collections
import functools
import math

import jax
import jax.numpy as jnp
from jax import lax
from jax.experimental import pallas as pl
from jax.experimental.pallas import tpu as pltpu

F32 = jnp.float32
BF16 = jnp.bfloat16

D_MODEL = 1024
DEPTH = 4
HEAD_DIM = 64
N_HEADS = 8
ATTN_WIDTH = N_HEADS * HEAD_DIM
ATTN_PATTERN = ((128, 1), (512, 4), (2048, 16))
N_GROUPS = len(ATTN_PATTERN)
BLK = 128
SSM_WIDTH = 512
SSM_GROUP = 16
SSM_GROUPS = 32
SSM_STATE = 64
D_FF = 2816
IN_COLS = 7168
EPS = 1e-6
ADAM_LR, ADAM_B1, ADAM_B2, ADAM_EPS, ADAM_WD, ADAM_STEP = 0.001, 0.9, 0.999, 1e-08, 0.01, 10

N_CHIPS = 4
MESH = pl.DeviceIdType.MESH

LANES = 128
SUBLANES = 8
VMEM_LIMIT = 56 * 1024 * 1024

TM = 512
TM_PROJ = 1024
TL_WGRAD = 2048
TM_MIX = 256

SSM_TB = 512
SSM_TC = 64
SSM_SUB = SUBLANES
SSM_PITCH = 72
N_SLAB = SSM_GROUPS * SSM_STATE // LANES
SSM_WIN = 256
N_PAIR = N_SLAB // 2
PAIRS_PER_WIN = 4
SCAN_GROUP = 4


def _params(sem=None, collective=False):
    return pltpu.CompilerParams(dimension_semantics=sem, vmem_limit_bytes=VMEM_LIMIT)


ANY = pl.BlockSpec(memory_space=pl.ANY)

Rider = collections.namedtuple("Rider", "ins out_shapes n_sem start wait aliases")


def _with_rider(body, rider, grid, prefetch, n_in, n_out, n_scratch):
    n_rin, n_rout = len(rider.ins), len(rider.out_shapes)

    def hosted(*refs):
        pre, rest = refs[:prefetch], refs[prefetch:]
        ins, rin = rest[:n_in], rest[n_in:n_in + n_rin]
        o0 = n_in + n_rin
        outs, rout = rest[o0:o0 + n_out], rest[o0 + n_out:o0 + n_out + n_rout]
        s0 = o0 + n_out + n_rout
        scr, (send, recv) = rest[s0:s0 + n_scratch], rest[s0 + n_scratch:]
        first = functools.reduce(jnp.logical_and, [pl.program_id(k) == 0 for k in range(len(grid))])
        last = functools.reduce(jnp.logical_and, [pl.program_id(k) == grid[k] - 1 for k in range(len(grid))])

        @pl.when(first)
        def _():
            rider.start(rin, rout, send, recv)

        body(*pre, *ins, *outs, *scr)

        @pl.when(last)
        def _():
            rider.wait(rin, rout, send, recv)

    return hosted


def _call(body, *, name, grid, in_specs, out_specs, out_shape, scratch=(), sem=None, aliases=None,
          prefetch=0, rider=None):
    if rider is not None:
        single = not isinstance(out_specs, (list, tuple))
        out_specs = [out_specs] if single else list(out_specs)
        out_shape = [out_shape] if single else list(out_shape)
        body = _with_rider(body, rider, grid, prefetch, len(in_specs), len(out_specs), len(scratch))
        aliases = dict(aliases or {})
        aliases.update({prefetch + len(in_specs) + k: len(out_specs) + v for k, v in rider.aliases.items()})
        in_specs = list(in_specs) + [ANY] * len(rider.ins)
        out_specs = out_specs + [ANY] * len(rider.out_shapes)
        out_shape = out_shape + list(rider.out_shapes)
        scratch = list(scratch) + [pltpu.SemaphoreType.DMA((rider.n_sem,)), pltpu.SemaphoreType.DMA((rider.n_sem,))]
        sem = ("arbitrary",) * len(grid)
        fn = _call(body, name=name + "_host", grid=grid, in_specs=in_specs, out_specs=out_specs, out_shape=out_shape,
                   scratch=scratch, sem=sem, aliases=aliases, prefetch=prefetch)
        return lambda *args: fn(*args, *rider.ins)
    kw = {}
    if aliases:
        kw["input_output_aliases"] = aliases
    if prefetch:
        gs = pltpu.PrefetchScalarGridSpec(num_scalar_prefetch=prefetch, grid=grid, in_specs=in_specs,
                                          out_specs=out_specs, scratch_shapes=list(scratch))
        return pl.pallas_call(body, name=name, grid_spec=gs, out_shape=out_shape,
                              compiler_params=_params(sem), **kw)
    return pl.pallas_call(body, name=name, grid=grid, in_specs=in_specs, out_specs=out_specs,
                          out_shape=out_shape, scratch_shapes=list(scratch),
                          compiler_params=_params(sem), **kw)


def _sds(shape, dtype):
    return jax.ShapeDtypeStruct(shape, dtype)


def _sigmoid(v):
    return 1.0 / (1.0 + jnp.exp(-v))


def _dot(a, b):
    return jnp.dot(a, b, preferred_element_type=F32)


def _dot_nt(a, b):
    return lax.dot_general(a, b, (((1,), (1,)), ((), ())), preferred_element_type=F32)


def _dot_tn(a, b):
    return lax.dot_general(a, b, (((0,), (0,)), ((), ())), preferred_element_type=F32)


def _in_proj_fwd(x, g, w, rider=None):
    L = x.shape[0]
    ns = w.shape[2]
    tn = ns
    nj = ns // tn
    TM = TM_PROJ

    def body(x_ref, g_ref, w_ref, z_ref, h_ref):
        @pl.when(pl.program_id(1) == 0)
        def _():
            xv = x_ref[...]
            r = lax.rsqrt(jnp.mean(xv * xv, axis=-1, keepdims=True) + EPS)
            h_ref[...] = (xv * r * g_ref[...]).astype(BF16)
        z_ref[...] = _dot(h_ref[...], w_ref[...]).astype(BF16)

    return _call(
        body, name="in_proj_fwd", grid=(L // TM, N_CHIPS * nj),
        in_specs=[pl.BlockSpec((TM, D_MODEL), lambda i, j: (i, 0)),
                  pl.BlockSpec((1, D_MODEL), lambda i, j: (0, 0)),
                  pl.BlockSpec((None, D_MODEL, tn), lambda i, j: (j // nj, 0, j % nj))],
        out_specs=[pl.BlockSpec((TM, tn), lambda i, j: (i, j)),
                   pl.BlockSpec((TM, D_MODEL), lambda i, j: (i, 0))],
        out_shape=[_sds((L, N_CHIPS * ns), BF16), _sds((L, D_MODEL), BF16)],
        sem=("parallel", "arbitrary"), rider=rider)(x, g, w)


DL_TILE = 512
SCALE = HEAD_DIM ** -0.5


def _perm_matrix(d):
    rho = jnp.arange(DL_TILE)
    src = rho // (DL_TILE // d) + d * (rho % (DL_TILE // d))
    return (src[:, None] == jnp.arange(DL_TILE)[None, :]).astype(BF16)


def _head_sum_matrix():
    h = jnp.arange(ATTN_WIDTH) // HEAD_DIM
    return (h[:, None] == h[None, :]).astype(BF16)


def _split(v):
    hi = v.astype(BF16)
    return hi, (v - hi.astype(F32)).astype(BF16)


def _head_sum(v, hs):
    vb = v.astype(BF16)
    half = ATTN_WIDTH // 2
    blk = hs[:half, :half]
    return jnp.concatenate([_dot(vb[:, :half], blk), _dot(vb[:, half:], blk)], axis=1)


def _permute(pm, v):
    hi, lo = _split(v)
    return _dot(pm, hi) + _dot(pm, lo)


def _dl_view(t, d):
    if d * BLK <= DL_TILE:
        return t
    return t.reshape(t.shape[0] // DL_TILE, d, DL_TILE // d, t.shape[1])


def _dl_spec(d, width, which):
    if d * BLK <= DL_TILE:
        per_tile = DL_TILE // (d * BLK)
        return pl.BlockSpec((BLK, width), lambda r, n: ((which(n) // per_tile) * (DL_TILE // BLK)
                                                       + r * per_tile + which(n) % per_tile, 0))
    tiles = d * BLK // DL_TILE
    return pl.BlockSpec((tiles, None, DL_TILE // d, width), lambda r, n: (which(n), r, 0, 0))


def _dl_read(ref):
    v = ref[...]
    return v if v.ndim == 2 else v.reshape(BLK, v.shape[-1])


def _dl_write(ref, v):
    ref[...] = v if len(ref.shape) == 2 else v.reshape(ref.shape)


def _qkv_prep(z, gq_t, gk_t):
    L = z.shape[0]
    qkv_w = N_GROUPS * ATTN_WIDTH

    def body(zq_ref, zk_ref, zv_ref, gq_ref, gk_ref, hs_ref, p1_ref, p2_ref, *outs):
        hs = hs_ref[...]
        perms = (None, p1_ref[...], p2_ref[...])
        for g in range(N_GROUPS):
            cols = slice(g * ATTN_WIDTH, (g + 1) * ATTN_WIDTH)
            xq = zq_ref[:, cols].astype(F32)
            xk = zk_ref[:, cols].astype(F32)
            rq = lax.rsqrt(_head_sum(xq * xq, hs) * (1.0 / HEAD_DIM) + EPS)
            rk = lax.rsqrt(_head_sum(xk * xk, hs) * (1.0 / HEAD_DIM) + EPS)
            vals = [(xq * rq * (gq_ref[...] * SCALE)).astype(BF16), (xk * rk * gk_ref[...]).astype(BF16),
                    zv_ref[:, cols]]
            for j, t in enumerate(vals):
                if perms[g] is not None:
                    t = _dot(perms[g], t).astype(BF16)
                outs[3 * g + j][...] = t

    tile = pl.BlockSpec((DL_TILE, ATTN_WIDTH), lambda i: (i, 0))
    mat = pl.BlockSpec((DL_TILE, DL_TILE), lambda i: (0, 0))
    vec = pl.BlockSpec((1, ATTN_WIDTH), lambda i: (0, 0))
    outs = _call(
        body, name="qkv_prep", grid=(L // DL_TILE,),
        in_specs=[pl.BlockSpec((DL_TILE, qkv_w), lambda i: (i, 0)), pl.BlockSpec((DL_TILE, qkv_w), lambda i: (i, 1)),
                  pl.BlockSpec((DL_TILE, qkv_w), lambda i: (i, 2)), vec, vec, mat, mat, mat],
        out_specs=[tile] * 9, out_shape=[_sds((L, ATTN_WIDTH), BF16)] * 9,
        sem=("parallel",))(z, z, z, gq_t, gk_t, _head_sum_matrix(), _perm_matrix(ATTN_PATTERN[1][1]),
                           _perm_matrix(ATTN_PATTERN[2][1]))
    return [tuple(outs[3 * g:3 * g + 3]) for g in range(N_GROUPS)]


def _pair_masks():
    lane = lax.broadcasted_iota(jnp.int32, (1, LANES), 1)
    return lane < HEAD_DIM, lane >= HEAD_DIM


def _attn_fwd(qs, ks, v, gi):
    L = qs.shape[0]
    _, d = ATTN_PATTERN[gi]
    nb = L // (d * BLK)

    def body(q_ref, kc_ref, kp_ref, vc_ref, vp_ref, o_ref, l_ref):
        n = pl.program_id(1)
        qi = lax.broadcasted_iota(jnp.int32, (BLK, 2 * BLK), 0)
        kj = lax.broadcasted_iota(jnp.int32, (BLK, 2 * BLK), 1)
        prev = kj < BLK
        mask = jnp.logical_and(jnp.where(prev, kj, qi) >= jnp.where(prev, qi, kj - BLK),
                               kj >= jnp.where(n > 0, 0, BLK))
        q = _dl_read(q_ref)
        kw = jnp.concatenate([_dl_read(kp_ref), _dl_read(kc_ref)], axis=0)
        vw = jnp.concatenate([_dl_read(vp_ref), _dl_read(vc_ref)], axis=0)
        one = jnp.ones((2 * BLK, LANES), BF16)
        o_parts, l_parts = [], []
        for hp in range(N_HEADS // 2):
            ls = slice(hp * LANES, (hp + 1) * LANES)
            qp, kp_, vp_ = q[:, ls], kw[:, ls], vw[:, ls]
            num = jnp.zeros((BLK, LANES), F32)
            den = jnp.zeros((BLK, LANES), F32)
            mb = jnp.zeros((BLK, LANES), F32)
            for he in _pair_masks():
                s = jnp.where(mask, _dot_nt(jnp.where(he, qp, 0), kp_), -jnp.inf)
                m = jnp.max(s, axis=-1, keepdims=True)
                p = jnp.exp(s - m).astype(BF16)
                acc = _dot(p, jnp.concatenate([jnp.where(he, vp_, 0), jnp.where(he, one, 0)], axis=1))
                num += acc[:, :LANES]
                den += acc[:, LANES:]
                mb = jnp.where(he, m, mb)
            o_parts.append((num / den).astype(BF16))
            l_parts.append(mb + jnp.log(den))
        _dl_write(o_ref, jnp.concatenate(o_parts, axis=1))
        _dl_write(l_ref, jnp.concatenate(l_parts, axis=1))

    cur = _dl_spec(d, ATTN_WIDTH, lambda n: n)
    prev = _dl_spec(d, ATTN_WIDTH, lambda n: jnp.maximum(n - 1, 0))
    view = lambda t: _dl_view(t, d)
    o, l = _call(
        body, name=f"attn_fwd_g{gi}", grid=(d, nb), in_specs=[cur, cur, prev, cur, prev], out_specs=[cur, cur],
        out_shape=[_sds(view(qs).shape, BF16), _sds(view(qs).shape, F32)],
        sem=("parallel", "parallel"))(view(qs), view(ks), view(ks), view(v), view(v))
    return o.reshape(L, ATTN_WIDTH), l.reshape(L, ATTN_WIDTH)


def _to_token_order(os_, ls_, pts):
    o_tok, l_tok = [], []
    for o, l, pt in zip(os_, ls_, pts):
        if pt is None:
            o_tok.append(o.astype(F32))
            l_tok.append(l)
        else:
            o_tok.append(_dot(pt, o))
            l_tok.append(_permute(pt, l))
    return o_tok, l_tok


def _combine_fwd(os_, ls_):
    L = os_[0].shape[0]

    def body(o0, o1, o2, l0, l1, l2, pt1_ref, pt2_ref, a_ref):
        o_tok, l_tok = _to_token_order((o0[...], o1[...], o2[...]), (l0[...], l1[...], l2[...]),
                                       (None, pt1_ref[...], pt2_ref[...]))
        w = _combine_weights(*l_tok)
        a_ref[...] = (w[0] * o_tok[0] + w[1] * o_tok[1] + w[2] * o_tok[2]).astype(BF16)

    tile = pl.BlockSpec((DL_TILE, ATTN_WIDTH), lambda i: (i, 0))
    mat = pl.BlockSpec((DL_TILE, DL_TILE), lambda i: (0, 0))
    return _call(body, name="combine_fwd", grid=(L // DL_TILE,), in_specs=[tile] * 6 + [mat, mat], out_specs=tile,
                 out_shape=_sds((L, ATTN_WIDTH), BF16), sem=("parallel",))(
                     *os_, *ls_, _perm_matrix(ATTN_PATTERN[1][1]).T, _perm_matrix(ATTN_PATTERN[2][1]).T)


def _gelu(v):
    c = math.sqrt(2.0 / math.pi)
    return 0.5 * v * (1.0 + jnp.tanh(c * (v + 0.044715 * v * v * v)))


def _gelu_grad(v):
    c = math.sqrt(2.0 / math.pi)
    t = jnp.tanh(c * (v + 0.044715 * v * v * v))
    return 0.5 * (1.0 + t) + 0.5 * v * (1.0 - t * t) * c * (1.0 + 3.0 * 0.044715 * v * v)


def _ssm_fill(u, bwre_ref, bwim_ref, sre, sim):
    for k2 in range(N_PAIR):
        uw = u[:, _win_cols(k2)]
        _to_slabs(sre, k2, _dot(uw, bwre_ref[k2]))
        _to_slabs(sim, k2, _dot(uw, bwim_ref[k2]))


def _win_cols(k2):
    w = k2 // PAIRS_PER_WIN
    return slice(w * SSM_WIN, (w + 1) * SSM_WIN)


def _to_slabs(ref, k2, v):
    for half in range(2):
        for j in range(SSM_SUB):
            ref[2 * k2 + half, j * SSM_PITCH:j * SSM_PITCH + SSM_TC, :] = (
                v[j * SSM_TC:(j + 1) * SSM_TC, half * LANES:(half + 1) * LANES])


def _rows(i):
    return pl.ds(i, SSM_SUB, stride=SSM_PITCH)


def _slab_rows(ref, k):
    return jnp.concatenate([ref[k, j * SSM_PITCH:j * SSM_PITCH + SSM_TC, :] for j in range(SSM_SUB)], axis=0)


def _pair_rows(ref, k2):
    return jnp.concatenate([_slab_rows(ref, 2 * k2), _slab_rows(ref, 2 * k2 + 1)], axis=1).astype(BF16)


def _bcast(ref, k):
    return jnp.broadcast_to(ref[pl.ds(k, 1), :], (SSM_SUB, LANES))


def _scan(sre, sim, are_ref, aim_ref, k0, init, *, reverse, store, sign=1.0):
    ar = [_bcast(are_ref, k0 + kk) for kk in range(SCAN_GROUP)]
    ai = [sign * _bcast(aim_ref, k0 + kk) for kk in range(SCAN_GROUP)]

    def step(t, carry):
        i = SSM_TC - 1 - t if reverse else t
        out = []
        for kk in range(SCAN_GROUP):
            k = k0 + kk
            xr, xi = carry[2 * kk], carry[2 * kk + 1]
            nr = ar[kk] * xr - ai[kk] * xi + sre[k, _rows(i), :]
            ni = ar[kk] * xi + ai[kk] * xr + sim[k, _rows(i), :]
            if store:
                sre[k, _rows(i), :] = nr
                sim[k, _rows(i), :] = ni
            out += [nr, ni]
        return tuple(out)

    flat = []
    for re, im in init:
        flat += [re, im]
    res = lax.fori_loop(0, SSM_TC, step, tuple(flat))
    return [(res[2 * kk], res[2 * kk + 1]) for kk in range(SCAN_GROUP)]


def _ssm_seeds(ends_re, ends_im, a64re_ref, a64im_ref, carry_re, carry_im, seed_re, seed_im, k,
               *, reverse, sign=1.0):
    ar = a64re_ref[pl.ds(k, 1), :]
    ai = sign * a64im_ref[pl.ds(k, 1), :]
    cr = carry_re[pl.ds(k, 1), :]
    ci = carry_im[pl.ds(k, 1), :]
    order = range(SSM_SUB - 1, -1, -1) if reverse else range(SSM_SUB)
    for j in order:
        seed_re[k, pl.ds(j, 1), :] = cr
        seed_im[k, pl.ds(j, 1), :] = ci
        er = ends_re[k, pl.ds(j, 1), :]
        ei = ends_im[k, pl.ds(j, 1), :]
        cr, ci = ar * cr - ai * ci + er, ar * ci + ai * cr + ei
    carry_re[pl.ds(k, 1), :] = cr
    carry_im[pl.ds(k, 1), :] = ci


def _ssm_specs_consts():
    c2 = pl.BlockSpec((N_SLAB, LANES), lambda b: (0, 0))
    c3 = pl.BlockSpec((N_PAIR, SSM_WIN, SSM_WIN), lambda b: (0, 0, 0))
    return c2, c3


def _ssm_scratch():
    rows = SSM_SUB * SSM_PITCH
    return [pltpu.VMEM((N_SLAB, rows, LANES), F32), pltpu.VMEM((N_SLAB, rows, LANES), F32)]


def _ssm_fwd(z, pk, dskip):
    L = z.shape[0]
    nb = L // SSM_TB
    ucol = (3 * N_GROUPS * ATTN_WIDTH) // SSM_WIDTH

    def body(u_ref, are_ref, aim_ref, a64re_ref, a64im_ref, bwre_ref, bwim_ref, cwre_ref, cwim_ref, d_ref,
             ypre_ref, yact_ref, sdre_ref, sdim_ref, sre, sim, carry_re, carry_im, ends_re, ends_im,
             seed_re, seed_im):
        @pl.when(pl.program_id(0) == 0)
        def _():
            carry_re[...] = jnp.zeros_like(carry_re)
            carry_im[...] = jnp.zeros_like(carry_im)

        u = u_ref[...]
        _ssm_fill(u, bwre_ref, bwim_ref, sre, sim)
        zero = jnp.zeros((SSM_SUB, LANES), F32)
        for k0 in range(0, N_SLAB, SCAN_GROUP):
            ends = _scan(sre, sim, are_ref, aim_ref, k0, [(zero, zero)] * SCAN_GROUP, reverse=False, store=False)
            for kk in range(SCAN_GROUP):
                ends_re[k0 + kk] = ends[kk][0]
                ends_im[k0 + kk] = ends[kk][1]
            for kk in range(SCAN_GROUP):
                _ssm_seeds(ends_re, ends_im, a64re_ref, a64im_ref, carry_re, carry_im, seed_re, seed_im,
                           k0 + kk, reverse=False)
            init = [(seed_re[k0 + kk], seed_im[k0 + kk]) for kk in range(SCAN_GROUP)]
            _scan(sre, sim, are_ref, aim_ref, k0, init, reverse=False, store=True)
        sdre_ref[...] = seed_re[...]
        sdim_ref[...] = seed_im[...]
        for w in range(N_PAIR // PAIRS_PER_WIN):
            acc = jnp.zeros((SSM_TB, SSM_WIN), F32)
            for kk in range(PAIRS_PER_WIN):
                k2 = w * PAIRS_PER_WIN + kk
                acc += _dot(_pair_rows(sre, k2), cwre_ref[k2])
                acc -= _dot(_pair_rows(sim, k2), cwim_ref[k2])
            cols = _win_cols(w * PAIRS_PER_WIN)
            ypre = acc + d_ref[:, cols] * u[:, cols].astype(F32)
            ypre_ref[:, cols] = ypre
            yact_ref[:, cols] = _gelu(ypre).astype(BF16)

    c2, c3 = _ssm_specs_consts()
    seed_spec = pl.BlockSpec((None, N_SLAB, SSM_SUB, LANES), lambda b: (b, 0, 0, 0))
    small = pltpu.VMEM((N_SLAB, LANES), F32)
    tile = pltpu.VMEM((N_SLAB, SSM_SUB, LANES), F32)
    return _call(
        body, name="ssm_fwd", grid=(nb,),
        in_specs=[pl.BlockSpec((SSM_TB, SSM_WIDTH), lambda b: (b, ucol)), c2, c2, c2, c2, c3, c3, c3, c3,
                  pl.BlockSpec((1, SSM_WIDTH), lambda b: (0, 0))],
        out_specs=[pl.BlockSpec((SSM_TB, SSM_WIDTH), lambda b: (b, 0)),
                   pl.BlockSpec((SSM_TB, SSM_WIDTH), lambda b: (b, 0)), seed_spec, seed_spec],
        out_shape=[_sds((L, SSM_WIDTH), F32), _sds((L, SSM_WIDTH), BF16),
                   _sds((nb, N_SLAB, SSM_SUB, LANES), F32), _sds((nb, N_SLAB, SSM_SUB, LANES), F32)],
        scratch=_ssm_scratch() + [small, small, tile, tile, tile, tile],
        sem=("arbitrary",))(z, pk["a_re"], pk["a_im"], pk["a64_re"], pk["a64_im"],
                            pk["bw_re"].astype(BF16), pk["bw_im"].astype(BF16),
                            pk["cw_re"].astype(BF16), pk["cw_im"].astype(BF16), dskip)


def _combine_weights(l0, l1, l2):
    m = jnp.maximum(jnp.maximum(l0, l1), l2)
    e0, e1, e2 = jnp.exp(l0 - m), jnp.exp(l1 - m), jnp.exp(l2 - m)
    inv = 1.0 / (e0 + e1 + e2)
    return e0 * inv, e1 * inv, e2 * inv


def _mix_fwd(x, z, a, yact, w_ap, w_ga, w_gb, w_out):
    L = x.shape[0]
    cs = D_MODEL // N_CHIPS
    ga_col = (3 * N_GROUPS * ATTN_WIDTH + SSM_WIDTH) // D_MODEL

    def body(x_ref, ga_ref, gs_ref, a_ref, y_ref, wap_ref, wga_ref, wgb_ref, wout_ref,
             x1_ref, aout_ref, sa_ref, sb_ref, mix_ref):
        a = a_ref[...]
        y = y_ref[...]
        for s in range(N_CHIPS):
            cols = slice(s * cs, (s + 1) * cs)
            aout_ref[:, cols] = _dot(a, wap_ref[s]).astype(BF16)
            sa_ref[:, cols] = _dot(y, wga_ref[s]).astype(BF16)
            sb_ref[:, cols] = _dot(y, wgb_ref[s]).astype(BF16)
        s_out = sa_ref[...].astype(F32) * _sigmoid(sb_ref[...].astype(F32))
        mix = (_sigmoid(ga_ref[...].astype(F32)) * aout_ref[...].astype(F32)
               + _sigmoid(gs_ref[...].astype(F32)) * s_out).astype(BF16)
        mix_ref[...] = mix
        x1_ref[...] = x_ref[...] + _dot(mix, wout_ref[...])

    tok = lambda w: pl.BlockSpec((TM_MIX, w), lambda i: (i, 0))
    wsm = pl.BlockSpec((N_CHIPS, ATTN_WIDTH, cs), lambda i: (0, 0, 0))
    return _call(
        body, name="mix_fwd", grid=(L // TM_MIX,),
        in_specs=[tok(D_MODEL), pl.BlockSpec((TM_MIX, D_MODEL), lambda i: (i, ga_col)),
                  pl.BlockSpec((TM_MIX, D_MODEL), lambda i: (i, ga_col + 1))]
                 + [tok(ATTN_WIDTH)] * 2 + [wsm, wsm, wsm, pl.BlockSpec((D_MODEL, D_MODEL), lambda i: (0, 0))],
        out_specs=[tok(D_MODEL), tok(D_MODEL), tok(D_MODEL), tok(D_MODEL), tok(D_MODEL)],
        out_shape=[_sds((L, D_MODEL), F32)] + [_sds((L, D_MODEL), BF16)] * 4,
        sem=("parallel",))(x, z, z, a, yact, w_ap, w_ga, w_gb, w_out.reshape(D_MODEL, D_MODEL))


def _ffn_fwd(x1, g, w_g, w_u, w_d, rider=None):
    L = x1.shape[0]
    fs = D_FF // N_CHIPS
    TM = TM_PROJ

    def body(x_ref, g_ref, wg_ref, wu_ref, wd_ref, x2_ref, h_ref, gate_ref, up_ref, act_ref, acc):
        s = pl.program_id(1)

        @pl.when(s == 0)
        def _():
            xv = x_ref[...]
            r = lax.rsqrt(jnp.mean(xv * xv, axis=-1, keepdims=True) + EPS)
            h_ref[...] = (xv * r * g_ref[...]).astype(BF16)
            acc[...] = jnp.zeros_like(acc)

        h = h_ref[...]
        gate = _dot(h, wg_ref[...])
        up = _dot(h, wu_ref[...])
        act = (gate * _sigmoid(gate) * up).astype(BF16)
        gate_ref[...] = gate.astype(BF16)
        up_ref[...] = up.astype(BF16)
        act_ref[...] = act
        acc[...] += _dot(act, wd_ref[...])

        @pl.when(s == N_CHIPS - 1)
        def _():
            x2_ref[...] = x_ref[...] + acc[...]

    tok = pl.BlockSpec((TM, D_MODEL), lambda i, s: (i, 0))
    ffs = pl.BlockSpec((None, TM, fs), lambda i, s: (s, i, 0))
    return _call(
        body, name="ffn_fwd", grid=(L // TM, N_CHIPS),
        in_specs=[tok, pl.BlockSpec((1, D_MODEL), lambda i, s: (0, 0)),
                  pl.BlockSpec((None, D_MODEL, fs), lambda i, s: (s, 0, 0)),
                  pl.BlockSpec((None, D_MODEL, fs), lambda i, s: (s, 0, 0)),
                  pl.BlockSpec((None, fs, D_MODEL), lambda i, s: (s, 0, 0))],
        out_specs=[tok, tok, ffs, ffs, ffs],
        out_shape=[_sds((L, D_MODEL), F32), _sds((L, D_MODEL), BF16)] + [_sds((N_CHIPS, L, fs), BF16)] * 3,
        scratch=[pltpu.VMEM((TM, D_MODEL), F32)],
        sem=("parallel", "arbitrary"), rider=rider)(x1, g, w_g, w_u, w_d)


def _loss_head(xl, target):
    L = xl.shape[0]

    def body(x_ref, t_ref, dx_ref, loss_ref, acc):
        i = pl.program_id(0)

        @pl.when(i == 0)
        def _():
            acc[...] = jnp.zeros_like(acc)

        e = x_ref[...] - t_ref[...]
        dx_ref[...] = e * (1.0 / D_MODEL)
        acc[...] += jnp.sum((e * e).reshape(TM // SUBLANES, SUBLANES, D_MODEL), axis=0)

        @pl.when(i == pl.num_programs(0) - 1)
        def _():
            loss_ref[...] = (0.5 / D_MODEL) * jnp.sum(acc[...]).reshape(1, 1)

    tok = pl.BlockSpec((TM, D_MODEL), lambda i: (i, 0))
    return _call(
        body, name="loss_head", grid=(L // TM,), in_specs=[tok, tok],
        out_specs=[tok, pl.BlockSpec((1, 1), lambda i: (0, 0))],
        out_shape=[_sds((L, D_MODEL), F32), _sds((1, 1), F32)],
        scratch=[pltpu.VMEM((SUBLANES, D_MODEL), F32)], sem=("arbitrary",))(xl, target)


def _ssm_pack(lam_re, lam_im, log_dt, b_re, b_im, c_re, c_im):
    dt = jnp.exp(log_dt)[:, None]
    mag = jnp.exp(lam_re * dt)
    ang = lam_im * dt
    ar = mag * jnp.cos(ang)
    ai = mag * jnp.sin(ang)
    nr = ar - 1.0
    ni = ai
    den = lam_re * lam_re + lam_im * lam_im
    cr = ((nr * lam_re + ni * lam_im) / den)[..., None]
    ci = ((ni * lam_re - nr * lam_im) / den)[..., None]
    bbr = cr * b_re - ci * b_im
    bbi = cr * b_im + ci * b_re
    gpp = SSM_WIN // SSM_STATE
    gpw = SSM_WIN // SSM_GROUP
    k2 = jnp.arange(N_PAIR)[:, None, None]
    gs = jnp.arange(gpp)[None, :, None]
    gl = jnp.arange(gpw)[None, None, :]
    same = (gl == gpp * (k2 % PAIRS_PER_WIN) + gs).astype(F32)

    def b_windows(bb):
        return jnp.einsum('kgl,kgpc->klcgp', same, bb.reshape(N_PAIR, gpp, SSM_STATE, SSM_GROUP)).reshape(
            N_PAIR, SSM_WIN, SSM_WIN)

    def c_windows(cc):
        return jnp.einsum('kgl,kgcp->kgplc', same, cc.reshape(N_PAIR, gpp, SSM_GROUP, SSM_STATE)).reshape(
            N_PAIR, SSM_WIN, SSM_WIN)

    pr, pi = ar, ai
    for _ in range(int(math.log2(SSM_TC))):
        pr, pi = pr * pr - pi * pi, 2.0 * pr * pi
    return dict(a_re=ar.reshape(N_SLAB, LANES), a_im=ai.reshape(N_SLAB, LANES),
                a64_re=pr.reshape(N_SLAB, LANES), a64_im=pi.reshape(N_SLAB, LANES),
                bw_re=b_windows(bbr), bw_im=b_windows(bbi), cw_re=c_windows(c_re), cw_im=c_windows(c_im))


def _layer_fwd(x, p, next_bufs=None):
    outs = _in_proj_fwd(x, p["g_mix"], p["w_in"], _gather_ici_rider(next_bufs) if next_bufs else None)
    (z, h), next_bufs = outs[:2], list(outs[2:])
    qkv = _qkv_prep(z, jnp.tile(p["g_q"], (1, N_HEADS)), jnp.tile(p["g_k"], (1, N_HEADS)))
    os_, ls_ = [], []
    for gi in range(N_GROUPS):
        o, l = _attn_fwd(*qkv[gi], gi)
        os_.append(o)
        ls_.append(l)
    a = _combine_fwd(os_, ls_)
    pk = _ssm_pack(p["lambda_re"], p["lambda_im"], p["log_dt"], p["b_re"], p["b_im"], p["c_re"], p["c_im"])
    ypre, yact, sd_re, sd_im = _ssm_fwd(z, pk, p["d_skip"])
    x1, aout, sa, sb, mix = _mix_fwd(x, z, a, yact, p["w_attn_proj"], p["w_glu_a"], p["w_glu_b"], p["w_out"])
    outs = _ffn_fwd(x1, p["g_ffn"], p["w_ffn_gate"], p["w_ffn_up"], p["w_ffn_down"],
                    _gather_d2d_rider(next_bufs) if next_bufs else None)
    (x2, h2, gate, up, act), next_full = outs[:5], list(outs[5:])
    saved = dict(x=x, z=z, h=h, qkv=qkv, os=os_, ls=ls_, pk=pk, ypre=ypre, yact=yact, sd_re=sd_re, sd_im=sd_im,
                 x1=x1, a=a, aout=aout, sa=sa, sb=sb, mix=mix, h2=h2, gate=gate, up=up, act=act)
    return x2, saved, next_full


def _rms_bwd(xv, g, dh):
    r = lax.rsqrt(jnp.mean(xv * xv, axis=-1, keepdims=True) + EPS)
    xn = xv * r
    dxn = dh * g
    dx = r * (dxn - xn * jnp.mean(dxn * xn, axis=-1, keepdims=True))
    dg = jnp.sum((dh * xn).reshape(xv.shape[0] // SUBLANES, SUBLANES, xv.shape[1]), axis=0)
    return dx, dg


def _ffn_bwd(dx2, x1, g, gate, up, w_g, w_u, w_d, rider=None):
    L = x1.shape[0]
    fs = D_FF // N_CHIPS

    def body(dx_ref, x_ref, g_ref, gate_ref, up_ref, wg_ref, wu_ref, wd_ref,
             dx1_ref, dgate_ref, dup_ref, dg_ref, acc, dgacc):
        i, s = pl.program_id(0), pl.program_id(1)

        @pl.when(s == 0)
        def _():
            acc[...] = jnp.zeros_like(acc)

        @pl.when(jnp.logical_and(i == 0, s == 0))
        def _():
            dgacc[...] = jnp.zeros_like(dgacc)

        dact = _dot_nt(dx_ref[...].astype(BF16), wd_ref[...])
        gt = gate_ref[...].astype(F32)
        sg = _sigmoid(gt)
        dgate = (dact * up_ref[...].astype(F32) * (sg * (1.0 + gt * (1.0 - sg)))).astype(BF16)
        dup = (dact * gt * sg).astype(BF16)
        dgate_ref[...] = dgate
        dup_ref[...] = dup
        acc[...] += _dot_nt(dgate, wg_ref[...]) + _dot_nt(dup, wu_ref[...])

        @pl.when(s == N_CHIPS - 1)
        def _():
            dx, dg = _rms_bwd(x_ref[...], g_ref[...], acc[...])
            dx1_ref[...] = dx_ref[...] + dx
            dgacc[...] += dg

        @pl.when(jnp.logical_and(i == pl.num_programs(0) - 1, s == N_CHIPS - 1))
        def _():
            dg_ref[...] = jnp.sum(dgacc[...], axis=0, keepdims=True)

    tok = pl.BlockSpec((TM, D_MODEL), lambda i, s: (i, 0))
    ffs = pl.BlockSpec((None, TM, fs), lambda i, s: (s, i, 0))
    vec = pl.BlockSpec((1, D_MODEL), lambda i, s: (0, 0))
    return _call(
        body, name="ffn_bwd", grid=(L // TM, N_CHIPS),
        in_specs=[tok, tok, vec, ffs, ffs,
                  pl.BlockSpec((None, D_MODEL, fs), lambda i, s: (s, 0, 0)),
                  pl.BlockSpec((None, D_MODEL, fs), lambda i, s: (s, 0, 0)),
                  pl.BlockSpec((None, fs, D_MODEL), lambda i, s: (s, 0, 0))],
        out_specs=[tok, ffs, ffs, vec],
        out_shape=[_sds((L, D_MODEL), F32), _sds((N_CHIPS, L, fs), BF16), _sds((N_CHIPS, L, fs), BF16),
                   _sds((1, D_MODEL), F32)],
        scratch=[pltpu.VMEM((TM, D_MODEL), F32), pltpu.VMEM((SUBLANES, D_MODEL), F32)],
        sem=("arbitrary", "arbitrary"), rider=rider)(dx2, x1, g, gate, up, w_g, w_u, w_d)


def _wgrad(a, b, *, name, grid_kn, a_spec, b_spec, out_shape, out_spec):
    L = a.shape[-2]
    nl = L // TL_WGRAD

    def body(a_ref, b_ref, o_ref):
        @pl.when(pl.program_id(2) == 0)
        def _():
            o_ref[...] = jnp.zeros_like(o_ref)
        o_ref[...] += _dot_tn(a_ref[...].astype(BF16), b_ref[...].astype(BF16))

    return _call(body, name=name, grid=(*grid_kn, nl), in_specs=[a_spec, b_spec], out_specs=out_spec,
                 out_shape=out_shape, sem=("parallel", "parallel", "arbitrary"))(a, b)


def _wgrad_cols(a, b, name):
    K, N = a.shape[1], b.shape[1]
    ns = N // N_CHIPS
    if N * K * 4 <= 4 * 1024 * 1024:
        L = a.shape[0]

        def body(a_ref, b_ref, o_ref):
            @pl.when(pl.program_id(0) == 0)
            def _():
                o_ref[...] = jnp.zeros_like(o_ref)
            av = a_ref[...].astype(BF16)
            for s in range(N_CHIPS):
                o_ref[s] += _dot_tn(av, b_ref[:, s * ns:(s + 1) * ns].astype(BF16))

        return _call(body, name=name, grid=(L // TL_WGRAD,),
                     in_specs=[pl.BlockSpec((TL_WGRAD, K), lambda t: (t, 0)),
                               pl.BlockSpec((TL_WGRAD, N), lambda t: (t, 0))],
                     out_specs=pl.BlockSpec((N_CHIPS, K, ns), lambda t: (0, 0, 0)),
                     out_shape=_sds((N_CHIPS, K, ns), F32), sem=("arbitrary",))(a, b)
    tn = ns // 2 if ns % (2 * LANES) == 0 else ns
    nj = ns // tn
    return _wgrad(a, b, name=name, grid_kn=(1, N_CHIPS * nj),
                  a_spec=pl.BlockSpec((TL_WGRAD, K), lambda i, j, t: (t, 0)),
                  b_spec=pl.BlockSpec((TL_WGRAD, tn), lambda i, j, t: (t, j)),
                  out_shape=_sds((N_CHIPS, K, ns), F32),
                  out_spec=pl.BlockSpec((None, K, tn), lambda i, j, t: (j // nj, 0, j % nj)))


def _wgrad_full(a, b, name):
    K, N = a.shape[1], b.shape[1]
    return _wgrad(a, b, name=name, grid_kn=(1, 1),
                  a_spec=pl.BlockSpec((TL_WGRAD, K), lambda i, j, t: (t, 0)),
                  b_spec=pl.BlockSpec((TL_WGRAD, N), lambda i, j, t: (t, 0)),
                  out_shape=_sds((K, N), F32), out_spec=pl.BlockSpec((K, N), lambda i, j, t: (0, 0)))


def _wgrad_ff_cols(a, b, name):
    K, fs = a.shape[1], b.shape[2]
    return _wgrad(a, b, name=name, grid_kn=(1, N_CHIPS),
                  a_spec=pl.BlockSpec((TL_WGRAD, K), lambda i, j, t: (t, 0)),
                  b_spec=pl.BlockSpec((None, TL_WGRAD, fs), lambda i, j, t: (j, t, 0)),
                  out_shape=_sds((N_CHIPS, K, fs), F32),
                  out_spec=pl.BlockSpec((None, K, fs), lambda i, j, t: (j, 0, 0)))


def _wgrad_ff_rows(a, b, name):
    fs, N = a.shape[2], b.shape[1]
    return _wgrad(a, b, name=name, grid_kn=(N_CHIPS, 1),
                  a_spec=pl.BlockSpec((None, TL_WGRAD, fs), lambda i, j, t: (i, t, 0)),
                  b_spec=pl.BlockSpec((TL_WGRAD, N), lambda i, j, t: (t, 0)),
                  out_shape=_sds((N_CHIPS, fs, N), F32),
                  out_spec=pl.BlockSpec((None, fs, N), lambda i, j, t: (i, 0, 0)))


def _mix_bwd(dx, z, aout, sa, sb, ypre, w_ap, w_ga, w_gb, w_out):
    L = dx.shape[0]
    cs = D_MODEL // N_CHIPS
    ga_col = (3 * N_GROUPS * ATTN_WIDTH + SSM_WIDTH) // D_MODEL

    def body(dx_ref, ga_ref, gs_ref, aout_ref, sa_ref, sb_ref, ypre_ref, wap_ref, wga_ref, wgb_ref, wout_ref,
             dgates_ref, da_ref, gy_ref, daout_ref, dsa_ref, dsb_ref):
        dmix = _dot_nt(dx_ref[...].astype(BF16), wout_ref[...])
        sig_a = _sigmoid(ga_ref[...].astype(F32))
        sig_s = _sigmoid(gs_ref[...].astype(F32))
        a_out = aout_ref[...].astype(F32)
        s_a = sa_ref[...].astype(F32)
        sig_b = _sigmoid(sb_ref[...].astype(F32))
        s_out = s_a * sig_b
        daout = (dmix * sig_a).astype(BF16)
        daout_ref[...] = daout
        dgates_ref[:, :D_MODEL] = (dmix * a_out * sig_a * (1.0 - sig_a)).astype(BF16)
        dgates_ref[:, D_MODEL:] = (dmix * s_out * sig_s * (1.0 - sig_s)).astype(BF16)
        ds_out = dmix * sig_s
        dsa = (ds_out * sig_b).astype(BF16)
        dsb = (ds_out * s_a * sig_b * (1.0 - sig_b)).astype(BF16)
        dsa_ref[...] = dsa
        dsb_ref[...] = dsb
        da = jnp.zeros((TM_MIX, ATTN_WIDTH), F32)
        dy = jnp.zeros((TM_MIX, SSM_WIDTH), F32)
        for s in range(N_CHIPS):
            cols = slice(s * cs, (s + 1) * cs)
            da += _dot_nt(daout[:, cols], wap_ref[s])
            dy += _dot_nt(dsa[:, cols], wga_ref[s]) + _dot_nt(dsb[:, cols], wgb_ref[s])
        gy_ref[...] = dy * _gelu_grad(ypre_ref[...])
        da_ref[...] = da

    tok = lambda w: pl.BlockSpec((TM_MIX, w), lambda i: (i, 0))
    wsm = pl.BlockSpec((N_CHIPS, ATTN_WIDTH, cs), lambda i: (0, 0, 0))
    return _call(
        body, name="mix_bwd", grid=(L // TM_MIX,),
        in_specs=[tok(D_MODEL), pl.BlockSpec((TM_MIX, D_MODEL), lambda i: (i, ga_col)),
                  pl.BlockSpec((TM_MIX, D_MODEL), lambda i: (i, ga_col + 1)),
                  tok(D_MODEL), tok(D_MODEL), tok(D_MODEL), tok(SSM_WIDTH),
                  wsm, wsm, wsm, pl.BlockSpec((D_MODEL, D_MODEL), lambda i: (0, 0))],
        out_specs=[tok(2 * D_MODEL), tok(ATTN_WIDTH), tok(SSM_WIDTH)] + [tok(D_MODEL)] * 3,
        out_shape=[_sds((L, 2 * D_MODEL), BF16), _sds((L, ATTN_WIDTH), F32), _sds((L, SSM_WIDTH), F32)]
                  + [_sds((L, D_MODEL), BF16)] * 3,
        sem=("parallel",))(dx, z, z, aout, sa, sb, ypre, w_ap, w_ga, w_gb, w_out.reshape(D_MODEL, D_MODEL))


def _combine_bwd(da, os_, ls_):
    L = da.shape[0]

    def body(da_ref, o0, o1, o2, l0, l1, l2, hs_ref, p1_ref, p2_ref, pt1_ref, pt2_ref,
             do0, do1, do2, c0, c1, c2):
        o_tok, l_tok = _to_token_order((o0[...], o1[...], o2[...]), (l0[...], l1[...], l2[...]),
                                       (None, pt1_ref[...], pt2_ref[...]))
        w = _combine_weights(*l_tok)
        dav = da_ref[...]
        hs = hs_ref[...]
        tbar = sum(wg * _head_sum(dav * og, hs) for wg, og in zip(w, o_tok))
        for wg, pm, do_ref, c_ref in zip(w, (None, p1_ref[...], p2_ref[...]), (do0, do1, do2), (c0, c1, c2)):
            dog = (wg * dav).astype(BF16)
            cg = -wg * tbar
            do_ref[...] = dog if pm is None else _dot(pm, dog).astype(BF16)
            c_ref[...] = cg if pm is None else _dot(pm, cg.astype(BF16))

    tile = pl.BlockSpec((DL_TILE, ATTN_WIDTH), lambda i: (i, 0))
    mat = pl.BlockSpec((DL_TILE, DL_TILE), lambda i: (0, 0))
    p1, p2 = _perm_matrix(ATTN_PATTERN[1][1]), _perm_matrix(ATTN_PATTERN[2][1])
    outs = _call(body, name="combine_bwd", grid=(L // DL_TILE,), in_specs=[tile] * 7 + [mat] * 5,
                 out_specs=[tile] * 6,
                 out_shape=[_sds((L, ATTN_WIDTH), BF16)] * 3 + [_sds((L, ATTN_WIDTH), F32)] * 3,
                 sem=("parallel",))(da, *os_, *ls_, _head_sum_matrix(), p1, p2, p1.T, p2.T)
    return outs[:3], outs[3:]


def _attn_bwd(qs, ks, v, do, l, c, gi):
    L = qs.shape[0]
    _, d = ATTN_PATTERN[gi]
    nb = L // (d * BLK)

    def body(q0_ref, q1_ref, k_ref, v_ref, do0_ref, do1_ref, l0_ref, l1_ref, c0_ref, c1_ref,
             dq_ref, dk_ref, dv_ref, carry):
        n = pl.program_id(1)

        @pl.when(n == 0)
        def _():
            carry[...] = jnp.zeros_like(carry)

        qi = lax.broadcasted_iota(jnp.int32, (2 * BLK, BLK), 0)
        kj = lax.broadcasted_iota(jnp.int32, (2 * BLK, BLK), 1)
        first = qi < BLK
        mask = jnp.logical_and(jnp.where(first, qi, kj) >= jnp.where(first, kj, qi - BLK),
                               qi < jnp.where(n < nb - 1, 2 * BLK, BLK))
        q2 = jnp.concatenate([_dl_read(q0_ref), _dl_read(q1_ref)], axis=0)
        do2 = jnp.concatenate([_dl_read(do0_ref), _dl_read(do1_ref)], axis=0)
        l2 = jnp.concatenate([_dl_read(l0_ref), _dl_read(l1_ref)], axis=0)
        c2 = jnp.concatenate([_dl_read(c0_ref), _dl_read(c1_ref)], axis=0)
        k = _dl_read(k_ref)
        v_ = _dl_read(v_ref)
        dq_parts, dk_parts, dv_parts = [], [], []
        for hp in range(N_HEADS // 2):
            ls = slice(hp * LANES, (hp + 1) * LANES)
            qp, dop, kp_, vp_ = q2[:, ls], do2[:, ls], k[:, ls], v_[:, ls]
            dq2 = jnp.zeros((2 * BLK, LANES), F32)
            dkp = jnp.zeros((BLK, LANES), F32)
            dvp = jnp.zeros((BLK, LANES), F32)
            for e, he in enumerate(_pair_masks()):
                col = slice(hp * LANES + e * HEAD_DIM, hp * LANES + e * HEAD_DIM + 1)
                ke = jnp.where(he, kp_, 0)
                p = jnp.where(mask, jnp.exp(_dot_nt(qp, ke) - l2[:, col]), 0.0)
                ds = (p * (_dot_nt(dop, jnp.where(he, vp_, 0)) + c2[:, col])).astype(BF16)
                dvp += _dot_tn(p.astype(BF16), jnp.where(he, dop, 0))
                dkp += _dot_tn(ds, jnp.where(he, qp, 0))
                dq2 += _dot(ds, ke)
            dq_parts.append((dq2[:BLK] + carry[:, ls]).astype(BF16))
            carry[:, ls] = dq2[BLK:]
            dk_parts.append(dkp.astype(BF16))
            dv_parts.append(dvp.astype(BF16))
        _dl_write(dq_ref, jnp.concatenate(dq_parts, axis=1))
        _dl_write(dk_ref, jnp.concatenate(dk_parts, axis=1))
        _dl_write(dv_ref, jnp.concatenate(dv_parts, axis=1))

    cur = _dl_spec(d, ATTN_WIDTH, lambda n: n)
    nxt = _dl_spec(d, ATTN_WIDTH, lambda n: jnp.minimum(n + 1, nb - 1))
    view = lambda t: _dl_view(t, d)
    outs = _call(
        body, name=f"attn_bwd_g{gi}", grid=(d, nb),
        in_specs=[cur, nxt, cur, cur, cur, nxt, cur, nxt, cur, nxt], out_specs=[cur, cur, cur],
        out_shape=[_sds(view(qs).shape, BF16)] * 3, scratch=[pltpu.VMEM((BLK, ATTN_WIDTH), F32)],
        sem=("parallel", "arbitrary"))(view(qs), view(qs), view(ks), view(v), view(do), view(do), view(l), view(l),
                                       view(c), view(c))
    return [t.reshape(L, ATTN_WIDTH) for t in outs]


def _qkv_post(z, dqkv, du, dgates, gq_t, gk_t):
    L = z.shape[0]
    qkv_w = N_GROUPS * ATTN_WIDTH

    def body(zq_ref, zk_ref, gq_ref, gk_ref, hs_ref, pt1_ref, pt2_ref, du_ref, dgates_ref, *rest):
        dl_refs, (dz_ref, dgq_ref, dgk_ref) = rest[:9], rest[9:]

        @pl.when(pl.program_id(0) == 0)
        def _():
            dgq_ref[...] = jnp.zeros_like(dgq_ref)
            dgk_ref[...] = jnp.zeros_like(dgk_ref)

        hs = hs_ref[...]
        pts = (None, pt1_ref[...], pt2_ref[...])

        def rows8(t):
            return jnp.sum(t.reshape(DL_TILE // SUBLANES, SUBLANES, ATTN_WIDTH), axis=0)

        def norm_bwd(x, gain, dn):
            r = lax.rsqrt(_head_sum(x * x, hs) * (1.0 / HEAD_DIM) + EPS)
            xh = x * r
            dh = dn * gain
            return r * (dh - xh * (_head_sum(dh * xh, hs) * (1.0 / HEAD_DIM))), rows8(dn * xh)

        for g in range(N_GROUPS):
            tok = [t[...].astype(F32) if pts[g] is None else _dot(pts[g], t[...]) for t in dl_refs[3 * g:3 * g + 3]]
            cols = slice(g * ATTN_WIDTH, (g + 1) * ATTN_WIDTH)
            dq, pq = norm_bwd(zq_ref[:, cols].astype(F32), gq_ref[...] * SCALE, tok[0])
            dk, pk_ = norm_bwd(zk_ref[:, cols].astype(F32), gk_ref[...], tok[1])
            dgq_ref[...] += pq * SCALE
            dgk_ref[...] += pk_
            dz_ref[:, cols] = dq.astype(BF16)
            dz_ref[:, qkv_w + g * ATTN_WIDTH:qkv_w + (g + 1) * ATTN_WIDTH] = dk.astype(BF16)
            dz_ref[:, 2 * qkv_w + g * ATTN_WIDTH:2 * qkv_w + (g + 1) * ATTN_WIDTH] = tok[2].astype(BF16)
        dz_ref[:, 3 * qkv_w:3 * qkv_w + SSM_WIDTH] = du_ref[...]
        dz_ref[:, 3 * qkv_w + SSM_WIDTH:] = dgates_ref[...]

    tile = lambda w: pl.BlockSpec((DL_TILE, w), lambda i: (i, 0))
    mat = pl.BlockSpec((DL_TILE, DL_TILE), lambda i: (0, 0))
    vec = pl.BlockSpec((1, ATTN_WIDTH), lambda i: (0, 0))
    acc = pl.BlockSpec((SUBLANES, ATTN_WIDTH), lambda i: (0, 0))
    flat = [t for grp in dqkv for t in grp]
    return _call(
        body, name="qkv_post", grid=(L // DL_TILE,),
        in_specs=[tile(qkv_w), pl.BlockSpec((DL_TILE, qkv_w), lambda i: (i, 1)), vec, vec, mat, mat, mat,
                  tile(SSM_WIDTH), tile(2 * D_MODEL)] + [tile(ATTN_WIDTH)] * 9,
        out_specs=[tile(IN_COLS), acc, acc],
        out_shape=[_sds((L, IN_COLS), BF16), _sds((SUBLANES, ATTN_WIDTH), F32), _sds((SUBLANES, ATTN_WIDTH), F32)],
        sem=("arbitrary",))(z, z, gq_t, gk_t, _head_sum_matrix(), _perm_matrix(ATTN_PATTERN[1][1]).T,
                            _perm_matrix(ATTN_PATTERN[2][1]).T, du, dgates, *flat)


def _scan_rev_grad(sre, sim, rre, rim, are_ref, aim_ref, k0, init, seed_re, seed_im):
    ar = [_bcast(are_ref, k0 + kk) for kk in range(SCAN_GROUP)]
    ai = [-_bcast(aim_ref, k0 + kk) for kk in range(SCAN_GROUP)]

    def update(i, xprev, carry):
        out = []
        for kk in range(SCAN_GROUP):
            k = k0 + kk
            lr, li, dr, di = carry[4 * kk:4 * kk + 4]
            nr = ar[kk] * lr - ai[kk] * li + rre[k, _rows(i), :]
            ni = ar[kk] * li + ai[kk] * lr + rim[k, _rows(i), :]
            rre[k, _rows(i), :] = nr
            rim[k, _rows(i), :] = ni
            xr, xi = xprev(k)
            out += [nr, ni, dr + xr * nr + xi * ni, di + xr * ni - xi * nr]
        return tuple(out)

    def step(t, carry):
        i = SSM_TC - 1 - t
        return update(i, lambda k: (sre[k, _rows(i - 1), :], sim[k, _rows(i - 1), :]), carry)

    zero = jnp.zeros((SSM_SUB, LANES), F32)
    flat = []
    for re, im in init:
        flat += [re, im, zero, zero]
    res = lax.fori_loop(0, SSM_TC - 1, step, tuple(flat))
    res = update(0, lambda k: (seed_re[k], seed_im[k]), res)
    return [(res[4 * kk + 2], res[4 * kk + 3]) for kk in range(SCAN_GROUP)]


def _ssm_bwd(z, gy, pk, dskip, sd_re, sd_im, rider=None):
    L = z.shape[0]
    nb = L // SSM_TB
    ucol = (3 * N_GROUPS * ATTN_WIDTH) // SSM_WIDTH
    nwin = N_PAIR // PAIRS_PER_WIN

    def body(u_ref, gy_ref, are_ref, aim_ref, a64re_ref, a64im_ref, bwre_ref, bwim_ref, cwre_ref, cwim_ref, d_ref,
             sdre_ref, sdim_ref,
             du_ref, dare_ref, daim_ref, dbre_ref, dbim_ref, dcre_ref, dcim_ref, dd_ref,
             sre, sim, rre, rim, carry_re, carry_im, ends_re, ends_im, seed_re, seed_im):
        @pl.when(pl.program_id(0) == 0)
        def _():
            carry_re[...] = jnp.zeros_like(carry_re)
            carry_im[...] = jnp.zeros_like(carry_im)
            for ref in (dare_ref, daim_ref, dbre_ref, dbim_ref, dcre_ref, dcim_ref, dd_ref):
                ref[...] = jnp.zeros_like(ref)

        u = u_ref[...]
        gyv = gy_ref[...]
        gyb = gyv.astype(BF16)
        _ssm_fill(u, bwre_ref, bwim_ref, sre, sim)
        for k2 in range(N_PAIR):
            gw = gyb[:, _win_cols(k2)]
            _to_slabs(rre, k2, _dot_nt(gw, cwre_ref[k2]))
            _to_slabs(rim, k2, -_dot_nt(gw, cwim_ref[k2]))
        zero = jnp.zeros((SSM_SUB, LANES), F32)
        for k0 in range(0, N_SLAB, SCAN_GROUP):
            grp = range(k0, k0 + SCAN_GROUP)
            _scan(sre, sim, are_ref, aim_ref, k0, [(sdre_ref[k], sdim_ref[k]) for k in grp],
                  reverse=False, store=True)
            ends = _scan(rre, rim, are_ref, aim_ref, k0, [(zero, zero)] * SCAN_GROUP, reverse=True, store=False,
                         sign=-1.0)
            for kk, k in enumerate(grp):
                ends_re[k] = ends[kk][0]
                ends_im[k] = ends[kk][1]
            for k in grp:
                _ssm_seeds(ends_re, ends_im, a64re_ref, a64im_ref, carry_re, carry_im, seed_re, seed_im, k,
                           reverse=True, sign=-1.0)
            das = _scan_rev_grad(sre, sim, rre, rim, are_ref, aim_ref, k0,
                                 [(seed_re[k], seed_im[k]) for k in grp], sdre_ref, sdim_ref)
            for kk, k in enumerate(grp):
                dare_ref[k] += das[kk][0]
                daim_ref[k] += das[kk][1]
        for w in range(nwin):
            cols = _win_cols(w * PAIRS_PER_WIN)
            uw = u[:, cols]
            gw = gyb[:, cols]
            acc = gyv[:, cols] * d_ref[:, cols]
            for kk in range(PAIRS_PER_WIN):
                k2 = w * PAIRS_PER_WIN + kk
                lr = _pair_rows(rre, k2)
                li = _pair_rows(rim, k2)
                acc += _dot_nt(lr, bwre_ref[k2]) + _dot_nt(li, bwim_ref[k2])
                dbre_ref[k2] += _dot_tn(uw, lr)
                dbim_ref[k2] += _dot_tn(uw, li)
                dcre_ref[k2] += _dot_tn(_pair_rows(sre, k2), gw)
                dcim_ref[k2] -= _dot_tn(_pair_rows(sim, k2), gw)
            du_ref[:, cols] = acc.astype(BF16)
        dd_ref[...] += jnp.sum((gyv * u.astype(F32)).reshape(SSM_TB // SUBLANES, SUBLANES, SSM_WIDTH), axis=0)

    c2, c3 = _ssm_specs_consts()
    rev = lambda b: nb - 1 - b
    seed_spec = pl.BlockSpec((None, N_SLAB, SSM_SUB, LANES), lambda b: (rev(b), 0, 0, 0))
    tile_out = pl.BlockSpec((N_SLAB, SSM_SUB, LANES), lambda b: (0, 0, 0))
    small = pltpu.VMEM((N_SLAB, LANES), F32)
    tile = pltpu.VMEM((N_SLAB, SSM_SUB, LANES), F32)
    return _call(
        body, name="ssm_bwd", grid=(nb,),
        in_specs=[pl.BlockSpec((SSM_TB, SSM_WIDTH), lambda b: (rev(b), ucol)),
                  pl.BlockSpec((SSM_TB, SSM_WIDTH), lambda b: (rev(b), 0)),
                  c2, c2, c2, c2, c3, c3, c3, c3, pl.BlockSpec((1, SSM_WIDTH), lambda b: (0, 0)),
                  seed_spec, seed_spec],
        out_specs=[pl.BlockSpec((SSM_TB, SSM_WIDTH), lambda b: (rev(b), 0)), tile_out, tile_out, c3, c3, c3, c3,
                   pl.BlockSpec((SUBLANES, SSM_WIDTH), lambda b: (0, 0))],
        out_shape=[_sds((L, SSM_WIDTH), BF16), _sds((N_SLAB, SSM_SUB, LANES), F32),
                   _sds((N_SLAB, SSM_SUB, LANES), F32)] + [_sds((N_PAIR, SSM_WIN, SSM_WIN), F32)] * 4
                  + [_sds((SUBLANES, SSM_WIDTH), F32)],
        scratch=_ssm_scratch() + _ssm_scratch() + [small, small, tile, tile, tile, tile],
        sem=("arbitrary",), rider=rider)(z, gy, pk["a_re"], pk["a_im"], pk["a64_re"], pk["a64_im"],
                            pk["bw_re"].astype(BF16), pk["bw_im"].astype(BF16),
                            pk["cw_re"].astype(BF16), pk["cw_im"].astype(BF16), dskip, sd_re, sd_im)


def _in_proj_bwd(dz, w, x, g, dres, rider=None):
    L = x.shape[0]
    ns = w.shape[2]
    tn = ns
    nj = ns // tn
    nt = N_CHIPS * nj
    TM = TM_PROJ

    def body(dz_ref, w_ref, x_ref, g_ref, dres_ref, dx_ref, dg_ref, acc, dgacc):
        i, j = pl.program_id(0), pl.program_id(1)

        @pl.when(j == 0)
        def _():
            acc[...] = jnp.zeros_like(acc)

        @pl.when(jnp.logical_and(i == 0, j == 0))
        def _():
            dgacc[...] = jnp.zeros_like(dgacc)

        acc[...] += _dot_nt(dz_ref[...], w_ref[...])

        @pl.when(j == nt - 1)
        def _():
            dx, dg = _rms_bwd(x_ref[...], g_ref[...], acc[...])
            dx_ref[...] = dres_ref[...] + dx
            dgacc[...] += dg

        @pl.when(jnp.logical_and(i == pl.num_programs(0) - 1, j == nt - 1))
        def _():
            dg_ref[...] = jnp.sum(dgacc[...], axis=0, keepdims=True)

    tok = pl.BlockSpec((TM, D_MODEL), lambda i, j: (i, 0))
    vec = pl.BlockSpec((1, D_MODEL), lambda i, j: (0, 0))
    return _call(
        body, name="in_proj_bwd", grid=(L // TM, nt),
        in_specs=[pl.BlockSpec((TM, tn), lambda i, j: (i, j)),
                  pl.BlockSpec((None, D_MODEL, tn), lambda i, j: (j // nj, 0, j % nj)), tok, vec, tok],
        out_specs=[tok, vec],
        out_shape=[_sds((L, D_MODEL), F32), _sds((1, D_MODEL), F32)],
        scratch=[pltpu.VMEM((TM, D_MODEL), F32), pltpu.VMEM((SUBLANES, D_MODEL), F32)],
        sem=("arbitrary", "arbitrary"), rider=rider)(dz, w, x, g, dres)


SSM_PARAMS = ("lambda_re", "lambda_im", "log_dt", "b_re", "b_im", "c_re", "c_im")


def _layer_bwd(dx2, sv, p, pending=None):
    g = {}
    outs = _ffn_bwd(dx2, sv["x1"], p["g_ffn"], sv["gate"], sv["up"], p["w_ffn_gate"], p["w_ffn_up"],
                    p["w_ffn_down"], _scatter_rider(pending[:1]) if pending else None)
    (dx1, dgate, dup, g["g_ffn"]), arrived = outs[:4], list(outs[4:])
    g["w_ffn_gate"] = _wgrad_ff_cols(sv["h2"], dgate, "wgrad_ffn_gate")
    g["w_ffn_up"] = _wgrad_ff_cols(sv["h2"], dup, "wgrad_ffn_up")
    g["w_ffn_down"] = _wgrad_ff_rows(sv["act"], dx2, "wgrad_ffn_down")

    dgates, da, gy, daout, dsa, dsb = _mix_bwd(dx1, sv["z"], sv["aout"], sv["sa"], sv["sb"], sv["ypre"],
                                               p["w_attn_proj"], p["w_glu_a"], p["w_glu_b"], p["w_out"])
    g["w_out"] = _wgrad_full(sv["mix"], dx1, "wgrad_out").reshape(N_CHIPS, D_MODEL // N_CHIPS, D_MODEL)
    g["w_attn_proj"] = _wgrad_cols(sv["a"], daout, "wgrad_attn_proj")
    g["w_glu_a"] = _wgrad_cols(sv["yact"], dsa, "wgrad_glu_a")
    g["w_glu_b"] = _wgrad_cols(sv["yact"], dsb, "wgrad_glu_b")

    outs = _ssm_bwd(sv["z"], gy, sv["pk"], p["d_skip"], sv["sd_re"], sv["sd_im"],
                    _scatter_rider(pending[1:]) if pending else None)
    du, da_re, da_im, dbw_re, dbw_im, dcw_re, dcw_im, dd = outs[:8]
    arrived += list(outs[8:])
    g["d_skip"] = jnp.sum(dd, axis=0, keepdims=True)
    _, pull = jax.vjp(_ssm_pack, *[p[n] for n in SSM_PARAMS])
    zeros = jnp.zeros((N_SLAB, LANES), F32)
    ct = dict(a_re=jnp.sum(da_re, axis=1), a_im=jnp.sum(da_im, axis=1), a64_re=zeros, a64_im=zeros,
              bw_re=dbw_re, bw_im=dbw_im, cw_re=dcw_re, cw_im=dcw_im)
    for n, v in zip(SSM_PARAMS, pull(ct)):
        g[n] = v

    dos, cs = _combine_bwd(da, sv["os"], sv["ls"])
    dqkv = [_attn_bwd(*sv["qkv"][gi], dos[gi], sv["ls"][gi], cs[gi], gi) for gi in range(N_GROUPS)]
    dz, gq8, gk8 = _qkv_post(sv["z"], dqkv, du, dgates, jnp.tile(p["g_q"], (1, N_HEADS)),
                             jnp.tile(p["g_k"], (1, N_HEADS)))
    g["g_q"] = jnp.sum(gq8.reshape(SUBLANES * N_HEADS, HEAD_DIM), axis=0, keepdims=True)
    g["g_k"] = jnp.sum(gk8.reshape(SUBLANES * N_HEADS, HEAD_DIM), axis=0, keepdims=True)
    g["w_in"] = _wgrad_cols(sv["h"], dz, "wgrad_in")
    outs = _in_proj_bwd(dz, p["w_in"], sv["x"], p["g_mix"], dx1, _swap_rider([g[n] for n in BIG]))
    (dx, g["g_mix"]), sibling = outs[:2], list(outs[2:])
    return dx, g, sibling, arrived


def _place():
    x, y, c = lax.axis_index("x"), lax.axis_index("y"), lax.axis_index("c")
    others = [(1 - x, y), (x, 1 - y), (1 - x, 1 - y)]
    return x, y, c, others


def _half(ref, hc):
    rows = ref.shape[-2] // 2
    idx = (slice(None),) * (len(ref.shape) - 2) + (pl.ds(hc * rows, rows), slice(None))
    return ref.at[idx]


def _comm_call(body, name, ins, out_shapes, n_remote, aliases=None):
    scratch = [pltpu.SemaphoreType.DMA((n_remote,)), pltpu.SemaphoreType.DMA((n_remote,))]
    return pl.pallas_call(
        body, name=name, in_specs=[ANY] * len(ins), out_specs=[ANY] * len(out_shapes), out_shape=out_shapes,
        scratch_shapes=scratch, input_output_aliases=aliases or {})(*ins)


def _cast_place(w, l, chip_idx):
    _, R, C = w.shape
    tr = R // 2

    def body(me_ref, w_ref, o_ref):
        o_ref[...] = w_ref[...].astype(BF16)

    return _call(body, name=f"cast_place_l{l}", grid=(R // tr,), prefetch=1,
                 in_specs=[pl.BlockSpec((None, tr, C), lambda i, me_ref: (l, i, 0))],
                 out_specs=pl.BlockSpec((None, tr, C), lambda i, me_ref: (me_ref[0], i, 0)),
                 out_shape=_sds((N_CHIPS, R, C), BF16), sem=("arbitrary",))(chip_idx, w)


def _in_place_rider(bufs, pairs):
    n = len(bufs)

    def copies(outs, send, recv, side):
        return [pltpu.make_async_remote_copy(src_ref=pair[side][0], dst_ref=pair[side][0], send_sem=send.at[k],
                                             recv_sem=recv.at[k], device_id=pair[side][1], device_id_type=MESH)
                for k, pair in enumerate(pairs(outs))]

    def start(ins, outs, send, recv):
        for cp in copies(outs, send, recv, 0):
            cp.start()

    def wait(ins, outs, send, recv):
        for cp in copies(outs, send, recv, 1):
            cp.wait_recv()
        for cp in copies(outs, send, recv, 0):
            cp.wait_send()

    return Rider(list(bufs), [_sds(b.shape, b.dtype) for b in bufs], 3 * n, start, wait, {a: a for a in range(n)})


def _gather_ici_rider(bufs):
    def pairs(outs):
        x, y, c, others = _place()
        return [((_half(o.at[2 * x + y], c), (cx, cy, c)), (_half(o.at[2 * cx + cy], c), (cx, cy, c)))
                for o in outs for cx, cy in others]
    return _in_place_rider(bufs, pairs)


def _gather_d2d_rider(bufs):
    def pairs(outs):
        x, y, c, others = _place()
        sib = (x, y, 1 - c)
        return [((_half(o.at[2 * cx + cy], c), sib), (_half(o.at[2 * cx + cy], 1 - c), sib))
                for o in outs for cx, cy in others]
    return _in_place_rider(bufs, pairs)


def _swap_rider(gs):
    n = len(gs)

    def copies(ins, outs, send, recv):
        x, y, c, _ = _place()
        return [pltpu.make_async_remote_copy(src_ref=_half(ins[a], 1 - c), dst_ref=outs[a], send_sem=send.at[a],
                                             recv_sem=recv.at[a], device_id=(x, y, 1 - c), device_id_type=MESH)
                for a in range(n)]

    def start(ins, outs, send, recv):
        for cp in copies(ins, outs, send, recv):
            cp.start()

    def wait(ins, outs, send, recv):
        for cp in copies(ins, outs, send, recv):
            cp.wait()

    outs = [_sds((g.shape[0], g.shape[1] // 2, g.shape[2]), g.dtype) for g in gs]
    return Rider(list(gs), outs, n, start, wait, {})


def _scatter_rider(ss):
    n = len(ss)

    def copies(ins, outs, send, recv):
        x, y, c, others = _place()
        return [pltpu.make_async_remote_copy(
            src_ref=ins[a].at[2 * cx + cy], dst_ref=outs[a].at[j], send_sem=send.at[3 * a + j],
            recv_sem=recv.at[3 * a + j], device_id=(cx, cy, c), device_id_type=MESH)
            for a in range(n) for j, (cx, cy) in enumerate(others)]

    def start(ins, outs, send, recv):
        for cp in copies(ins, outs, send, recv):
            cp.start()

    def wait(ins, outs, send, recv):
        for cp in copies(ins, outs, send, recv):
            cp.wait()

    outs = [_sds((N_CHIPS - 1,) + s.shape[1:], s.dtype) for s in ss]
    return Rider(list(ss), outs, 3 * n, start, wait, {})


def _run_rider(rider, name):
    n_in = len(rider.ins)

    def body(*refs):
        ins, outs = refs[:n_in], refs[n_in:n_in + len(rider.out_shapes)]
        send, recv = refs[n_in + len(rider.out_shapes):]
        rider.start(ins, outs, send, recv)
        rider.wait(ins, outs, send, recv)

    return _comm_call(body, name, rider.ins, rider.out_shapes, rider.n_sem, aliases=rider.aliases)


def _join_halves(bufs):
    n = len(bufs)

    def body(*refs):
        outs = refs[n:2 * n]
        send, recv = refs[2 * n:]
        x, y, c, _ = _place()

        def swap(a, hc):
            region = _half(outs[a], hc)
            return pltpu.make_async_remote_copy(src_ref=region, dst_ref=region, send_sem=send.at[a],
                                                recv_sem=recv.at[a], device_id=(x, y, 1 - c), device_id_type=MESH)

        cps = [swap(a, c) for a in range(n)]
        for cp in cps:
            cp.start()
        for a in range(n):
            swap(a, 1 - c).wait_recv()
        for cp in cps:
            cp.wait_send()

    outs = [_sds(b.shape, b.dtype) for b in bufs]
    return _comm_call(body, "join_halves", bufs, outs, n, aliases={a: a for a in range(n)})


def _gather_small(v):
    rows, n = v.shape

    def body(v_ref, out_ref, send, recv, lsem):
        x, y, c, others = _place()
        me, sibling = (x, y, c), (x, y, 1 - c)

        def blk(px, py, pc):
            return out_ref.at[pl.ds((4 * px + 2 * py + pc) * rows, rows), :]

        def copy(k, block, to, src=None):
            return pltpu.make_async_remote_copy(src_ref=blk(*block) if src is None else src, dst_ref=blk(*block),
                                                send_sem=send.at[k], recv_sem=recv.at[k], device_id=to,
                                                device_id_type=MESH)

        mine = pltpu.make_async_copy(v_ref, blk(*me), lsem)
        mine.start()
        first = [copy(0, me, sibling, src=v_ref)]
        first += [copy(1 + j, me, (*chip, c), src=v_ref) for j, chip in enumerate(others)]
        for cp in first:
            cp.start()
        passed = [copy(4 + j, (*chip, c), sibling) for j, chip in enumerate(others)]
        for j, chip in enumerate(others):
            copy(1 + j, (*chip, c), me).wait_recv()
            passed[j].start()
        copy(0, sibling, me).wait_recv()
        for j, chip in enumerate(others):
            copy(4 + j, (*chip, 1 - c), me).wait_recv()
        for cp in first + passed:
            cp.wait_send()
        mine.wait()

    return pl.pallas_call(
        body, name="gather_small", out_shape=_sds((8 * rows, n), v.dtype),
        in_specs=[pl.BlockSpec(memory_space=pltpu.VMEM)], out_specs=pl.BlockSpec(memory_space=pltpu.VMEM),
        scratch_shapes=[pltpu.SemaphoreType.DMA((7,)), pltpu.SemaphoreType.DMA((7,)), pltpu.SemaphoreType.DMA],
        compiler_params=pltpu.CompilerParams(vmem_limit_bytes=VMEM_LIMIT))(v)


def _add_half(g, p, c):
    _, R, C = g.shape
    half = R // 2

    def body(c_ref, g_ref, p_ref, o_ref):
        o_ref[...] = g_ref[...] + p_ref[...]

    blk = (None, half, C)
    return _call(body, name="add_half", grid=(N_CHIPS,), prefetch=1,
                 in_specs=[pl.BlockSpec(blk, lambda s, c_ref: (s, c_ref[0], 0)),
                           pl.BlockSpec(blk, lambda s, c_ref: (s, 0, 0))],
                 out_specs=pl.BlockSpec(blk, lambda s, c_ref: (s, 0, 0)),
                 out_shape=_sds((N_CHIPS, half, C), F32), sem=("arbitrary",))(c, g, p)


def _sum_owner(s, q, buf, l, me, c):
    _, half, C = s.shape
    tr = half // 2

    def body(me_ref, c_ref, s_ref, q0, q1, q2, buf_ref, o_ref):
        o_ref[...] = ((s_ref[...] + q0[...]) + q1[...]) + q2[...]

    blk = (None, tr, C)
    qspec = lambda j: pl.BlockSpec(blk, lambda i, me_ref, c_ref: (j, i, 0))
    return _call(body, name=f"sum_owner_l{l}", grid=(half // tr,), prefetch=2,
                 in_specs=[pl.BlockSpec(blk, lambda i, me_ref, c_ref: (me_ref[0], i, 0)),
                           qspec(0), qspec(1), qspec(2), ANY],
                 out_specs=pl.BlockSpec(blk, lambda i, me_ref, c_ref: (l, 2 * c_ref[0] + i, 0)),
                 out_shape=_sds(buf.shape, F32), sem=("arbitrary",), aliases={6: 0})(me, c, s, q, q, q, buf)


def _adamw_math(w, g, m, v):
    m = ADAM_B1 * m + (1.0 - ADAM_B1) * g
    v = ADAM_B2 * v + (1.0 - ADAM_B2) * (g * g)
    m_hat = m / (1.0 - ADAM_B1 ** ADAM_STEP)
    v_hat = v / (1.0 - ADAM_B2 ** ADAM_STEP)
    delta = -ADAM_LR * (m_hat / (jnp.sqrt(v_hat) + ADAM_EPS) + ADAM_WD * w)
    return delta, m, v


def _adamw(w, g, m, v):
    rows, C = w.shape
    tr = next(t for t in (256, 128, 64) if rows % t == 0)

    def body(w_ref, g_ref, m_ref, v_ref, d_ref, nm_ref, nv_ref):
        d, nm, nv = _adamw_math(w_ref[...], g_ref[...], m_ref[...], v_ref[...])
        d_ref[...] = d
        nm_ref[...] = nm
        nv_ref[...] = nv

    spec = pl.BlockSpec((tr, C), lambda i: (i, 0))
    return _call(body, name="adamw", grid=(rows // tr,), in_specs=[spec] * 4, out_specs=[spec] * 3,
                 out_shape=[_sds((rows, C), F32)] * 3, sem=("parallel",))(w, g, m, v)


def _small_update(gathered, w, m, v):
    _, rows, n = gathered.shape
    tr = rows // 7

    def body(ga_ref, w_ref, m_ref, v_ref, g_ref, d_ref, nm_ref, nv_ref):
        g = ga_ref[0]
        for k in range(1, 8):
            g = g + ga_ref[k]
        d, nm, nv = _adamw_math(w_ref[...], g, m_ref[...], v_ref[...])
        g_ref[...] = g
        d_ref[...] = d
        nm_ref[...] = nm
        nv_ref[...] = nv

    spec = pl.BlockSpec((tr, n), lambda i: (i, 0))
    return _call(body, name="small_update", grid=(rows // tr,),
                 in_specs=[pl.BlockSpec((8, tr, n), lambda i: (0, i, 0)), spec, spec, spec], out_specs=[spec] * 4,
                 out_shape=[_sds((rows, n), F32)] * 4, sem=("parallel",))(gathered, w, m, v)


WEIGHTS = ("g_mix", "w_in", "g_q", "g_k", "w_attn_proj", "lambda_re", "lambda_im", "log_dt", "b_re", "b_im",
           "c_re", "c_im", "d_skip", "w_glu_a", "w_glu_b", "w_out", "g_ffn", "w_ffn_gate", "w_ffn_up", "w_ffn_down")
BIG = ("w_in", "w_attn_proj", "w_glu_a", "w_glu_b", "w_out", "w_ffn_gate", "w_ffn_up", "w_ffn_down")
SMALL = tuple(n for n in WEIGHTS if n not in BIG)
ROW_VECTORS = ("g_mix", "g_q", "g_k", "d_skip", "g_ffn")
PACK_QUANTUM = LANES * SUBLANES * 7


def _pack_small(parts, extra):
    flat = jnp.concatenate([parts[n].reshape(-1).astype(F32) for n in SMALL] + [extra.reshape(-1)])
    pad = -flat.shape[0] % PACK_QUANTUM
    return jnp.pad(flat, (0, pad)).reshape(-1, LANES)


def _unpack_small(packed, like):
    flat = packed.reshape(-1)
    out, at = {}, 0
    for n in SMALL:
        size = math.prod(like[n].shape)
        out[n] = flat[at:at + size].reshape(like[n].shape)
        at += size
    return out, flat[at]


def kernel(x, g_mix, w_in, g_q, g_k, w_attn_proj, lambda_re, lambda_im, log_dt, b_re, b_im, c_re, c_im, d_skip, w_glu_a, w_glu_b, w_out, g_ffn, w_ffn_gate, w_ffn_up, w_ffn_down, loss_target, m_g_mix, m_w_in, m_g_q, m_g_k, m_w_attn_proj, m_lambda_re, m_lambda_im, m_log_dt, m_b_re, m_b_im, m_c_re, m_c_im, m_d_skip, m_w_glu_a, m_w_glu_b, m_w_out, m_g_ffn, m_w_ffn_gate, m_w_ffn_up, m_w_ffn_down, v_g_mix, v_w_in, v_g_q, v_g_k, v_w_attn_proj, v_lambda_re, v_lambda_im, v_log_dt, v_b_re, v_b_im, v_c_re, v_c_im, v_d_skip, v_w_glu_a, v_w_glu_b, v_w_out, v_g_ffn, v_w_ffn_gate, v_w_ffn_up, v_w_ffn_down):
    given = dict(locals())
    W = {n: given[n] for n in WEIGHTS}
    M = {n: given["m_" + n] for n in WEIGHTS}
    V = {n: given["v_" + n] for n in WEIGHTS}
    depth = g_mix.shape[0]
    xl = x.reshape(x.shape[-2:])
    target = loss_target.reshape(loss_target.shape[-2:])
    c_idx = lax.axis_index("c").astype(jnp.int32).reshape(1)
    chip_idx = (2 * lax.axis_index("x") + lax.axis_index("y")).astype(jnp.int32).reshape(1)

    place = lambda l: [_cast_place(W[n], l, chip_idx) for n in BIG]
    full = _run_rider(_gather_d2d_rider(_run_rider(_gather_ici_rider(place(0)), "gather_ici")), "gather_d2d")
    params, saved, h = [], [], xl
    for l in range(depth):
        p = dict(zip(BIG, full))
        for n in SMALL:
            p[n] = W[n][l][None] if n in ROW_VECTORS else W[n][l]
        params.append(p)
        h, sv, full = _layer_fwd(h, p, place(l + 1) if l + 1 < depth else None)
        saved.append(sv)
    dx, loss_part = _loss_head(h, target)

    owned = [lax.empty(W[n].shape, F32) for n in BIG]
    small_grads = [None] * depth
    pending = None
    for l in reversed(range(depth)):
        dx, g, sibling, arrived = _layer_bwd(dx, saved[l], params[l], pending)
        if pending is not None:
            owned = [_sum_owner(s, q, buf, l + 1, chip_idx, c_idx) for s, q, buf in zip(pending, arrived, owned)]
        pending = [_add_half(g[n], s, c_idx) for n, s in zip(BIG, sibling)]
        small_grads[l] = g
    arrived = _run_rider(_scatter_rider(pending), "scatter_to_owners")
    owned = [_sum_owner(s, q, buf, 0, chip_idx, c_idx) for s, q, buf in zip(pending, arrived, owned)]
    reduced = dict(zip(BIG, _join_halves(owned)))

    grads, delta, new_m, new_v = {}, {}, {}, {}
    for n in BIG:
        shape = W[n].shape
        two_d = (shape[0] * shape[1], shape[2])
        d, nm, nv = _adamw(W[n].reshape(two_d), reduced[n].reshape(two_d), M[n].reshape(two_d), V[n].reshape(two_d))
        grads[n], delta[n], new_m[n], new_v[n] = reduced[n], d.reshape(shape), nm.reshape(shape), nv.reshape(shape)

    stacked = {n: jnp.stack([small_grads[l][n] for l in range(depth)]) for n in SMALL}
    zero = jnp.zeros((1,), F32)
    packed = _pack_small(stacked, loss_part)
    gathered = _gather_small(packed).reshape(8, *packed.shape)
    gs, ds, nms, nvs = _small_update(gathered, _pack_small(W, zero), _pack_small(M, zero), _pack_small(V, zero))
    sg, loss = _unpack_small(gs, W)
    sd, _ = _unpack_small(ds, W)
    sm, _ = _unpack_small(nms, W)
    sv_, _ = _unpack_small(nvs, W)
    for n in SMALL:
        grads[n], delta[n], new_m[n], new_v[n] = sg[n], sd[n], sm[n], sv_[n]

    return (loss, dx.reshape(x.shape), *[grads[n] for n in WEIGHTS], *[delta[n] for n in WEIGHTS],
            *[new_m[n] for n in WEIGHTS], *[new_v[n] for n in WEIGHTS])
```

```python
import collections
import functools
import math

import jax
import jax.numpy as jnp
from jax import lax
from jax.experimental import pallas as pl
from jax.experimental.pallas import tpu as pltpu

F32 = jnp.float32
BF16 = jnp.bfloat16

D_MODEL = 1024
DEPTH = 4
HEAD_DIM = 64
N_HEADS = 8
ATTN_WIDTH = N_HEADS * HEAD_DIM
ATTN_PATTERN = ((128, 1), (512, 4), (2048, 16))
N_GROUPS = len(ATTN_PATTERN)
BLK = 128
SSM_WIDTH = 512
SSM_GROUP = 16
SSM_GROUPS = 32
SSM_STATE = 64
D_FF = 2816
IN_COLS = 7168
EPS = 1e-6
ADAM_LR, ADAM_B1, ADAM_B2, ADAM_EPS, ADAM_WD, ADAM_STEP = 0.001, 0.9, 0.999, 1e-08, 0.01, 10

N_CHIPS = 4
MESH = pl.DeviceIdType.MESH

LANES = 128
SUBLANES = 8
VMEM_LIMIT = 56 * 1024 * 1024

TM = 512
TM_PROJ = 1024
TL_WGRAD = 2048
TM_MIX = 256

SSM_TB = 512
SSM_TC = 64
SSM_SUB = SUBLANES
SSM_PITCH = 72
N_SLAB = SSM_GROUPS * SSM_STATE // LANES
SSM_WIN = 256
N_PAIR = N_SLAB // 2
PAIRS_PER_WIN = 4
SCAN_GROUP = 4


def _params(sem=None, collective=False):
    return pltpu.CompilerParams(dimension_semantics=sem, vmem_limit_bytes=VMEM_LIMIT)


ANY = pl.BlockSpec(memory_space=pl.ANY)

Rider = collections.namedtuple("Rider", "ins out_shapes n_sem start wait aliases")


def _with_rider(body, rider, grid, prefetch, n_in, n_out, n_scratch):
    n_rin, n_rout = len(rider.ins), len(rider.out_shapes)

    def hosted(*refs):
        pre, rest = refs[:prefetch], refs[prefetch:]
        ins, rin = rest[:n_in], rest[n_in:n_in + n_rin]
        o0 = n_in + n_rin
        outs, rout = rest[o0:o0 + n_out], rest[o0 + n_out:o0 + n_out + n_rout]
        s0 = o0 + n_out + n_rout
        scr, (send, recv) = rest[s0:s0 + n_scratch], rest[s0 + n_scratch:]
        first = functools.reduce(jnp.logical_and, [pl.program_id(k) == 0 for k in range(len(grid))])
        last = functools.reduce(jnp.logical_and, [pl.program_id(k) == grid[k] - 1 for k in range(len(grid))])

        @pl.when(first)
        def _():
            rider.start(rin, rout, send, recv)

        body(*pre, *ins, *outs, *scr)

        @pl.when(last)
        def _():
            rider.wait(rin, rout, send, recv)

    return hosted


def _call(body, *, name, grid, in_specs, out_specs, out_shape, scratch=(), sem=None, aliases=None,
          prefetch=0, rider=None):
    if rider is not None:
        single = not isinstance(out_specs, (list, tuple))
        out_specs = [out_specs] if single else list(out_specs)
        out_shape = [out_shape] if single else list(out_shape)
        body = _with_rider(body, rider, grid, prefetch, len(in_specs), len(out_specs), len(scratch))
        aliases = dict(aliases or {})
        aliases.update({prefetch + len(in_specs) + k: len(out_specs) + v for k, v in rider.aliases.items()})
        in_specs = list(in_specs) + [ANY] * len(rider.ins)
        out_specs = out_specs + [ANY] * len(rider.out_shapes)
        out_shape = out_shape + list(rider.out_shapes)
        scratch = list(scratch) + [pltpu.SemaphoreType.DMA((rider.n_sem,)), pltpu.SemaphoreType.DMA((rider.n_sem,))]
        sem = ("arbitrary",) * len(grid)
        fn = _call(body, name=name + "_host", grid=grid, in_specs=in_specs, out_specs=out_specs, out_shape=out_shape,
                   scratch=scratch, sem=sem, aliases=aliases, prefetch=prefetch)
        return lambda *args: fn(*args, *rider.ins)
    kw = {}
    if aliases:
        kw["input_output_aliases"] = aliases
    if prefetch:
        gs = pltpu.PrefetchScalarGridSpec(num_scalar_prefetch=prefetch, grid=grid, in_specs=in_specs,
                                          out_specs=out_specs, scratch_shapes=list(scratch))
        return pl.pallas_call(body, name=name, grid_spec=gs, out_shape=out_shape,
                              compiler_params=_params(sem), **kw)
    return pl.pallas_call(body, name=name, grid=grid, in_specs=in_specs, out_specs=out_specs,
                          out_shape=out_shape, scratch_shapes=list(scratch),
                          compiler_params=_params(sem), **kw)


def _sds(shape, dtype):
    return jax.ShapeDtypeStruct(shape, dtype)


def _sigmoid(v):
    return 1.0 / (1.0 + jnp.exp(-v))


def _dot(a, b):
    return jnp.dot(a, b, preferred_element_type=F32)


def _dot_nt(a, b):
    return lax.dot_general(a, b, (((1,), (1,)), ((), ())), preferred_element_type=F32)


def _dot_tn(a, b):
    return lax.dot_general(a, b, (((0,), (0,)), ((), ())), preferred_element_type=F32)


def _in_proj_fwd(x, g, w, rider=None):
    L = x.shape[0]
    ns = w.shape[2]
    tn = ns
    nj = ns // tn
    TM = TM_PROJ

    def body(x_ref, g_ref, w_ref, z_ref, h_ref):
        @pl.when(pl.program_id(1) == 0)
        def _():
            xv = x_ref[...]
            r = lax.rsqrt(jnp.mean(xv * xv, axis=-1, keepdims=True) + EPS)
            h_ref[...] = (xv * r * g_ref[...]).astype(BF16)
        z_ref[...] = _dot(h_ref[...], w_ref[...]).astype(BF16)

    return _call(
        body, name="in_proj_fwd", grid=(L // TM, N_CHIPS * nj),
        in_specs=[pl.BlockSpec((TM, D_MODEL), lambda i, j: (i, 0)),
                  pl.BlockSpec((1, D_MODEL), lambda i, j: (0, 0)),
                  pl.BlockSpec((None, D_MODEL, tn), lambda i, j: (j // nj, 0, j % nj))],
        out_specs=[pl.BlockSpec((TM, tn), lambda i, j: (i, j)),
                   pl.BlockSpec((TM, D_MODEL), lambda i, j: (i, 0))],
        out_shape=[_sds((L, N_CHIPS * ns), BF16), _sds((L, D_MODEL), BF16)],
        sem=("parallel", "arbitrary"), rider=rider)(x, g, w)


DL_TILE = 512
SCALE = HEAD_DIM ** -0.5


def _perm_matrix(d):
    rho = jnp.arange(DL_TILE)
    src = rho // (DL_TILE // d) + d * (rho % (DL_TILE // d))
    return (src[:, None] == jnp.arange(DL_TILE)[None, :]).astype(BF16)


def _head_sum_matrix():
    h = jnp.arange(ATTN_WIDTH) // HEAD_DIM
    return (h[:, None] == h[None, :]).astype(BF16)


def _split(v):
    hi = v.astype(BF16)
    return hi, (v - hi.astype(F32)).astype(BF16)


def _head_sum(v, hs):
    vb = v.astype(BF16)
    half = ATTN_WIDTH // 2
    blk = hs[:half, :half]
    return jnp.concatenate([_dot(vb[:, :half], blk), _dot(vb[:, half:], blk)], axis=1)


def _permute(pm, v):
    hi, lo = _split(v)
    return _dot(pm, hi) + _dot(pm, lo)


def _dl_view(t, d):
    if d * BLK <= DL_TILE:
        return t
    return t.reshape(t.shape[0] // DL_TILE, d, DL_TILE // d, t.shape[1])


def _dl_spec(d, width, which):
    if d * BLK <= DL_TILE:
        per_tile = DL_TILE // (d * BLK)
        return pl.BlockSpec((BLK, width), lambda r, n: ((which(n) // per_tile) * (DL_TILE // BLK)
                                                       + r * per_tile + which(n) % per_tile, 0))
    tiles = d * BLK // DL_TILE
    return pl.BlockSpec((tiles, None, DL_TILE // d, width), lambda r, n: (which(n), r, 0, 0))


def _dl_read(ref):
    v = ref[...]
    return v if v.ndim == 2 else v.reshape(BLK, v.shape[-1])


def _dl_write(ref, v):
    ref[...] = v if len(ref.shape) == 2 else v.reshape(ref.shape)


def _qkv_prep(z, gq_t, gk_t):
    L = z.shape[0]
    qkv_w = N_GROUPS * ATTN_WIDTH

    def body(zq_ref, zk_ref, zv_ref, gq_ref, gk_ref, hs_ref, p1_ref, p2_ref, *outs):
        hs = hs_ref[...]
        perms = (None, p1_ref[...], p2_ref[...])
        for g in range(N_GROUPS):
            cols = slice(g * ATTN_WIDTH, (g + 1) * ATTN_WIDTH)
            xq = zq_ref[:, cols].astype(F32)
            xk = zk_ref[:, cols].astype(F32)
            rq = lax.rsqrt(_head_sum(xq * xq, hs) * (1.0 / HEAD_DIM) + EPS)
            rk = lax.rsqrt(_head_sum(xk * xk, hs) * (1.0 / HEAD_DIM) + EPS)
            vals = [(xq * rq * (gq_ref[...] * SCALE)).astype(BF16), (xk * rk * gk_ref[...]).astype(BF16),
                    zv_ref[:, cols]]
            for j, t in enumerate(vals):
                if perms[g] is not None:
                    t = _dot(perms[g], t).astype(BF16)
                outs[3 * g + j][...] = t

    tile = pl.BlockSpec((DL_TILE, ATTN_WIDTH), lambda i: (i, 0))
    mat = pl.BlockSpec((DL_TILE, DL_TILE), lambda i: (0, 0))
    vec = pl.BlockSpec((1, ATTN_WIDTH), lambda i: (0, 0))
    outs = _call(
        body, name="qkv_prep", grid=(L // DL_TILE,),
        in_specs=[pl.BlockSpec((DL_TILE, qkv_w), lambda i: (i, 0)), pl.BlockSpec((DL_TILE, qkv_w), lambda i: (i, 1)),
                  pl.BlockSpec((DL_TILE, qkv_w), lambda i: (i, 2)), vec, vec, mat, mat, mat],
        out_specs=[tile] * 9, out_shape=[_sds((L, ATTN_WIDTH), BF16)] * 9,
        sem=("parallel",))(z, z, z, gq_t, gk_t, _head_sum_matrix(), _perm_matrix(ATTN_PATTERN[1][1]),
                           _perm_matrix(ATTN_PATTERN[2][1]))
    return [tuple(outs[3 * g:3 * g + 3]) for g in range(N_GROUPS)]


def _pair_masks():
    lane = lax.broadcasted_iota(jnp.int32, (1, LANES), 1)
    return lane < HEAD_DIM, lane >= HEAD_DIM


def _attn_fwd(qs, ks, v, gi):
    L = qs.shape[0]
    _, d = ATTN_PATTERN[gi]
    nb = L // (d * BLK)

    def body(q_ref, kc_ref, kp_ref, vc_ref, vp_ref, o_ref, l_ref):
        n = pl.program_id(1)
        qi = lax.broadcasted_iota(jnp.int32, (BLK, 2 * BLK), 0)
        kj = lax.broadcasted_iota(jnp.int32, (BLK, 2 * BLK), 1)
        prev = kj < BLK
        mask = jnp.logical_and(jnp.where(prev, kj, qi) >= jnp.where(prev, qi, kj - BLK),
                               kj >= jnp.where(n > 0, 0, BLK))
        q = _dl_read(q_ref)
        kw = jnp.concatenate([_dl_read(kp_ref), _dl_read(kc_ref)], axis=0)
        vw = jnp.concatenate([_dl_read(vp_ref), _dl_read(vc_ref)], axis=0)
        one = jnp.ones((2 * BLK, LANES), BF16)
        o_parts, l_parts = [], []
        for hp in range(N_HEADS // 2):
            ls = slice(hp * LANES, (hp + 1) * LANES)
            qp, kp_, vp_ = q[:, ls], kw[:, ls], vw[:, ls]
            num = jnp.zeros((BLK, LANES), F32)
            den = jnp.zeros((BLK, LANES), F32)
            mb = jnp.zeros((BLK, LANES), F32)
            for he in _pair_masks():
                s = jnp.where(mask, _dot_nt(jnp.where(he, qp, 0), kp_), -jnp.inf)
                m = jnp.max(s, axis=-1, keepdims=True)
                p = jnp.exp(s - m).astype(BF16)
                acc = _dot(p, jnp.concatenate([jnp.where(he, vp_, 0), jnp.where(he, one, 0)], axis=1))
                num += acc[:, :LANES]
                den += acc[:, LANES:]
                mb = jnp.where(he, m, mb)
            o_parts.append((num / den).astype(BF16))
            l_parts.append(mb + jnp.log(den))
        _dl_write(o_ref, jnp.concatenate(o_parts, axis=1))
        _dl_write(l_ref, jnp.concatenate(l_parts, axis=1))

    cur = _dl_spec(d, ATTN_WIDTH, lambda n: n)
    prev = _dl_spec(d, ATTN_WIDTH, lambda n: jnp.maximum(n - 1, 0))
    view = lambda t: _dl_view(t, d)
    o, l = _call(
        body, name=f"attn_fwd_g{gi}", grid=(d, nb), in_specs=[cur, cur, prev, cur, prev], out_specs=[cur, cur],
        out_shape=[_sds(view(qs).shape, BF16), _sds(view(qs).shape, F32)],
        sem=("parallel", "parallel"))(view(qs), view(ks), view(ks), view(v), view(v))
    return o.reshape(L, ATTN_WIDTH), l.reshape(L, ATTN_WIDTH)


def _to_token_order(os_, ls_, pts):
    o_tok, l_tok = [], []
    for o, l, pt in zip(os_, ls_, pts):
        if pt is None:
            o_tok.append(o.astype(F32))
            l_tok.append(l)
        else:
            o_tok.append(_dot(pt, o))
            l_tok.append(_permute(pt, l))
    return o_tok, l_tok


def _combine_fwd(os_, ls_):
    L = os_[0].shape[0]

    def body(o0, o1, o2, l0, l1, l2, pt1_ref, pt2_ref, a_ref):
        o_tok, l_tok = _to_token_order((o0[...], o1[...], o2[...]), (l0[...], l1[...], l2[...]),
                                       (None, pt1_ref[...], pt2_ref[...]))
        w = _combine_weights(*l_tok)
        a_ref[...] = (w[0] * o_tok[0] + w[1] * o_tok[1] + w[2] * o_tok[2]).astype(BF16)

    tile = pl.BlockSpec((DL_TILE, ATTN_WIDTH), lambda i: (i, 0))
    mat = pl.BlockSpec((DL_TILE, DL_TILE), lambda i: (0, 0))
    return _call(body, name="combine_fwd", grid=(L // DL_TILE,), in_specs=[tile] * 6 + [mat, mat], out_specs=tile,
                 out_shape=_sds((L, ATTN_WIDTH), BF16), sem=("parallel",))(
                     *os_, *ls_, _perm_matrix(ATTN_PATTERN[1][1]).T, _perm_matrix(ATTN_PATTERN[2][1]).T)


def _gelu(v):
    c = math.sqrt(2.0 / math.pi)
    return 0.5 * v * (1.0 + jnp.tanh(c * (v + 0.044715 * v * v * v)))


def _gelu_grad(v):
    c = math.sqrt(2.0 / math.pi)
    t = jnp.tanh(c * (v + 0.044715 * v * v * v))
    return 0.5 * (1.0 + t) + 0.5 * v * (1.0 - t * t) * c * (1.0 + 3.0 * 0.044715 * v * v)


def _ssm_fill(u, bwre_ref, bwim_ref, sre, sim):
    for k2 in range(N_PAIR):
        uw = u[:, _win_cols(k2)]
        _to_slabs(sre, k2, _dot(uw, bwre_ref[k2]))
        _to_slabs(sim, k2, _dot(uw, bwim_ref[k2]))


def _win_cols(k2):
    w = k2 // PAIRS_PER_WIN
    return slice(w * SSM_WIN, (w + 1) * SSM_WIN)


def _to_slabs(ref, k2, v):
    for half in range(2):
        for j in range(SSM_SUB):
            ref[2 * k2 + half, j * SSM_PITCH:j * SSM_PITCH + SSM_TC, :] = (
                v[j * SSM_TC:(j + 1) * SSM_TC, half * LANES:(half + 1) * LANES])


def _rows(i):
    return pl.ds(i, SSM_SUB, stride=SSM_PITCH)


def _slab_rows(ref, k):
    return jnp.concatenate([ref[k, j * SSM_PITCH:j * SSM_PITCH + SSM_TC, :] for j in range(SSM_SUB)], axis=0)


def _pair_rows(ref, k2):
    return jnp.concatenate([_slab_rows(ref, 2 * k2), _slab_rows(ref, 2 * k2 + 1)], axis=1).astype(BF16)


def _bcast(ref, k):
    return jnp.broadcast_to(ref[pl.ds(k, 1), :], (SSM_SUB, LANES))


def _scan(sre, sim, are_ref, aim_ref, k0, init, *, reverse, store, sign=1.0):
    ar = [_bcast(are_ref, k0 + kk) for kk in range(SCAN_GROUP)]
    ai = [sign * _bcast(aim_ref, k0 + kk) for kk in range(SCAN_GROUP)]

    def step(t, carry):
        i = SSM_TC - 1 - t if reverse else t
        out = []
        for kk in range(SCAN_GROUP):
            k = k0 + kk
            xr, xi = carry[2 * kk], carry[2 * kk + 1]
            nr = ar[kk] * xr - ai[kk] * xi + sre[k, _rows(i), :]
            ni = ar[kk] * xi + ai[kk] * xr + sim[k, _rows(i), :]
            if store:
                sre[k, _rows(i), :] = nr
                sim[k, _rows(i), :] = ni
            out += [nr, ni]
        return tuple(out)

    flat = []
    for re, im in init:
        flat += [re, im]
    res = lax.fori_loop(0, SSM_TC, step, tuple(flat), unroll=2)
    return [(res[2 * kk], res[2 * kk + 1]) for kk in range(SCAN_GROUP)]


def _ssm_seeds(ends_re, ends_im, a64re_ref, a64im_ref, carry_re, carry_im, seed_re, seed_im, k,
               *, reverse, sign=1.0):
    ar = a64re_ref[pl.ds(k, 1), :]
    ai = sign * a64im_ref[pl.ds(k, 1), :]
    cr = carry_re[pl.ds(k, 1), :]
    ci = carry_im[pl.ds(k, 1), :]
    order = range(SSM_SUB - 1, -1, -1) if reverse else range(SSM_SUB)
    for j in order:
        seed_re[k, pl.ds(j, 1), :] = cr
        seed_im[k, pl.ds(j, 1), :] = ci
        er = ends_re[k, pl.ds(j, 1), :]
        ei = ends_im[k, pl.ds(j, 1), :]
        cr, ci = ar * cr - ai * ci + er, ar * ci + ai * cr + ei
    carry_re[pl.ds(k, 1), :] = cr
    carry_im[pl.ds(k, 1), :] = ci


def _ssm_specs_consts():
    c2 = pl.BlockSpec((N_SLAB, LANES), lambda b: (0, 0))
    c3 = pl.BlockSpec((N_PAIR, SSM_WIN, SSM_WIN), lambda b: (0, 0, 0))
    return c2, c3


def _ssm_scratch():
    rows = SSM_SUB * SSM_PITCH
    return [pltpu.VMEM((N_SLAB, rows, LANES), F32), pltpu.VMEM((N_SLAB, rows, LANES), F32)]


def _ssm_fwd(z, pk, dskip):
    L = z.shape[0]
    nb = L // SSM_TB
    ucol = (3 * N_GROUPS * ATTN_WIDTH) // SSM_WIDTH

    def body(u_ref, are_ref, aim_ref, a64re_ref, a64im_ref, bwre_ref, bwim_ref, cwre_ref, cwim_ref, d_ref,
             ypre_ref, yact_ref, sdre_ref, sdim_ref, sre, sim, carry_re, carry_im, ends_re, ends_im,
             seed_re, seed_im):
        @pl.when(pl.program_id(0) == 0)
        def _():
            carry_re[...] = jnp.zeros_like(carry_re)
            carry_im[...] = jnp.zeros_like(carry_im)

        u = u_ref[...]
        _ssm_fill(u, bwre_ref, bwim_ref, sre, sim)
        zero = jnp.zeros((SSM_SUB, LANES), F32)
        for k0 in range(0, N_SLAB, SCAN_GROUP):
            ends = _scan(sre, sim, are_ref, aim_ref, k0, [(zero, zero)] * SCAN_GROUP, reverse=False, store=False)
            for kk in range(SCAN_GROUP):
                ends_re[k0 + kk] = ends[kk][0]
                ends_im[k0 + kk] = ends[kk][1]
            for kk in range(SCAN_GROUP):
                _ssm_seeds(ends_re, ends_im, a64re_ref, a64im_ref, carry_re, carry_im, seed_re, seed_im,
                           k0 + kk, reverse=False)
            init = [(seed_re[k0 + kk], seed_im[k0 + kk]) for kk in range(SCAN_GROUP)]
            _scan(sre, sim, are_ref, aim_ref, k0, init, reverse=False, store=True)
        sdre_ref[...] = seed_re[...]
        sdim_ref[...] = seed_im[...]
        for w in range(N_PAIR // PAIRS_PER_WIN):
            acc = jnp.zeros((SSM_TB, SSM_WIN), F32)
            for kk in range(PAIRS_PER_WIN):
                k2 = w * PAIRS_PER_WIN + kk
                acc += _dot(_pair_rows(sre, k2), cwre_ref[k2])
                acc -= _dot(_pair_rows(sim, k2), cwim_ref[k2])
            cols = _win_cols(w * PAIRS_PER_WIN)
            ypre = acc + d_ref[:, cols] * u[:, cols].astype(F32)
            ypre_ref[:, cols] = ypre
            yact_ref[:, cols] = _gelu(ypre).astype(BF16)

    c2, c3 = _ssm_specs_consts()
    seed_spec = pl.BlockSpec((None, N_SLAB, SSM_SUB, LANES), lambda b: (b, 0, 0, 0))
    small = pltpu.VMEM((N_SLAB, LANES), F32)
    tile = pltpu.VMEM((N_SLAB, SSM_SUB, LANES), F32)
    return _call(
        body, name="ssm_fwd", grid=(nb,),
        in_specs=[pl.BlockSpec((SSM_TB, SSM_WIDTH), lambda b: (b, ucol)), c2, c2, c2, c2, c3, c3, c3, c3,
                  pl.BlockSpec((1, SSM_WIDTH), lambda b: (0, 0))],
        out_specs=[pl.BlockSpec((SSM_TB, SSM_WIDTH), lambda b: (b, 0)),
                   pl.BlockSpec((SSM_TB, SSM_WIDTH), lambda b: (b, 0)), seed_spec, seed_spec],
        out_shape=[_sds((L, SSM_WIDTH), F32), _sds((L, SSM_WIDTH), BF16),
                   _sds((nb, N_SLAB, SSM_SUB, LANES), F32), _sds((nb, N_SLAB, SSM_SUB, LANES), F32)],
        scratch=_ssm_scratch() + [small, small, tile, tile, tile, tile],
        sem=("arbitrary",))(z, pk["a_re"], pk["a_im"], pk["a64_re"], pk["a64_im"],
                            pk["bw_re"].astype(BF16), pk["bw_im"].astype(BF16),
                            pk["cw_re"].astype(BF16), pk["cw_im"].astype(BF16), dskip)


def _combine_weights(l0, l1, l2):
    m = jnp.maximum(jnp.maximum(l0, l1), l2)
    e0, e1, e2 = jnp.exp(l0 - m), jnp.exp(l1 - m), jnp.exp(l2 - m)
    inv = 1.0 / (e0 + e1 + e2)
    return e0 * inv, e1 * inv, e2 * inv


def _mix_fwd(x, z, a, yact, w_ap, w_ga, w_gb, w_out):
    L = x.shape[0]
    cs = D_MODEL // N_CHIPS
    ga_col = (3 * N_GROUPS * ATTN_WIDTH + SSM_WIDTH) // D_MODEL

    def body(x_ref, ga_ref, gs_ref, a_ref, y_ref, wap_ref, wga_ref, wgb_ref, wout_ref,
             x1_ref, aout_ref, sa_ref, sb_ref, mix_ref):
        a = a_ref[...]
        y = y_ref[...]
        for s in range(N_CHIPS):
            cols = slice(s * cs, (s + 1) * cs)
            aout_ref[:, cols] = _dot(a, wap_ref[s]).astype(BF16)
            sa_ref[:, cols] = _dot(y, wga_ref[s]).astype(BF16)
            sb_ref[:, cols] = _dot(y, wgb_ref[s]).astype(BF16)
        s_out = sa_ref[...].astype(F32) * _sigmoid(sb_ref[...].astype(F32))
        mix = (_sigmoid(ga_ref[...].astype(F32)) * aout_ref[...].astype(F32)
               + _sigmoid(gs_ref[...].astype(F32)) * s_out).astype(BF16)
        mix_ref[...] = mix
        x1_ref[...] = x_ref[...] + _dot(mix, wout_ref[...])

    tok = lambda w: pl.BlockSpec((TM_MIX, w), lambda i: (i, 0))
    wsm = pl.BlockSpec((N_CHIPS, ATTN_WIDTH, cs), lambda i: (0, 0, 0))
    return _call(
        body, name="mix_fwd", grid=(L // TM_MIX,),
        in_specs=[tok(D_MODEL), pl.BlockSpec((TM_MIX, D_MODEL), lambda i: (i, ga_col)),
                  pl.BlockSpec((TM_MIX, D_MODEL), lambda i: (i, ga_col + 1))]
                 + [tok(ATTN_WIDTH)] * 2 + [wsm, wsm, wsm, pl.BlockSpec((D_MODEL, D_MODEL), lambda i: (0, 0))],
        out_specs=[tok(D_MODEL), tok(D_MODEL), tok(D_MODEL), tok(D_MODEL), tok(D_MODEL)],
        out_shape=[_sds((L, D_MODEL), F32)] + [_sds((L, D_MODEL), BF16)] * 4,
        sem=("parallel",))(x, z, z, a, yact, w_ap, w_ga, w_gb, w_out.reshape(D_MODEL, D_MODEL))


def _ffn_fwd(x1, g, w_g, w_u, w_d, rider=None):
    L = x1.shape[0]
    fs = D_FF // N_CHIPS
    TM = TM_PROJ

    def body(x_ref, g_ref, wg_ref, wu_ref, wd_ref, x2_ref, h_ref, gate_ref, up_ref, act_ref, acc):
        s = pl.program_id(1)

        @pl.when(s == 0)
        def _():
            xv = x_ref[...]
            r = lax.rsqrt(jnp.mean(xv * xv, axis=-1, keepdims=True) + EPS)
            h_ref[...] = (xv * r * g_ref[...]).astype(BF16)
            acc[...] = jnp.zeros_like(acc)

        h = h_ref[...]
        gate = _dot(h, wg_ref[...])
        up = _dot(h, wu_ref[...])
        act = (gate * _sigmoid(gate) * up).astype(BF16)
        gate_ref[...] = gate.astype(BF16)
        up_ref[...] = up.astype(BF16)
        act_ref[...] = act
        acc[...] += _dot(act, wd_ref[...])

        @pl.when(s == N_CHIPS - 1)
        def _():
            x2_ref[...] = x_ref[...] + acc[...]

    tok = pl.BlockSpec((TM, D_MODEL), lambda i, s: (i, 0))
    ffs = pl.BlockSpec((None, TM, fs), lambda i, s: (s, i, 0))
    return _call(
        body, name="ffn_fwd", grid=(L // TM, N_CHIPS),
        in_specs=[tok, pl.BlockSpec((1, D_MODEL), lambda i, s: (0, 0)),
                  pl.BlockSpec((None, D_MODEL, fs), lambda i, s: (s, 0, 0)),
                  pl.BlockSpec((None, D_MODEL, fs), lambda i, s: (s, 0, 0)),
                  pl.BlockSpec((None, fs, D_MODEL), lambda i, s: (s, 0, 0))],
        out_specs=[tok, tok, ffs, ffs, ffs],
        out_shape=[_sds((L, D_MODEL), F32), _sds((L, D_MODEL), BF16)] + [_sds((N_CHIPS, L, fs), BF16)] * 3,
        scratch=[pltpu.VMEM((TM, D_MODEL), F32)],
        sem=("parallel", "arbitrary"), rider=rider)(x1, g, w_g, w_u, w_d)


def _loss_head(xl, target):
    L = xl.shape[0]

    def body(x_ref, t_ref, dx_ref, loss_ref, acc):
        i = pl.program_id(0)

        @pl.when(i == 0)
        def _():
            acc[...] = jnp.zeros_like(acc)

        e = x_ref[...] - t_ref[...]
        dx_ref[...] = e * (1.0 / D_MODEL)
        acc[...] += jnp.sum((e * e).reshape(TM // SUBLANES, SUBLANES, D_MODEL), axis=0)

        @pl.when(i == pl.num_programs(0) - 1)
        def _():
            loss_ref[...] = (0.5 / D_MODEL) * jnp.sum(acc[...]).reshape(1, 1)

    tok = pl.BlockSpec((TM, D_MODEL), lambda i: (i, 0))
    return _call(
        body, name="loss_head", grid=(L // TM,), in_specs=[tok, tok],
        out_specs=[tok, pl.BlockSpec((1, 1), lambda i: (0, 0))],
        out_shape=[_sds((L, D_MODEL), F32), _sds((1, 1), F32)],
        scratch=[pltpu.VMEM((SUBLANES, D_MODEL), F32)], sem=("arbitrary",))(xl, target)


def _ssm_pack(lam_re, lam_im, log_dt, b_re, b_im, c_re, c_im):
    dt = jnp.exp(log_dt)[:, None]
    mag = jnp.exp(lam_re * dt)
    ang = lam_im * dt
    ar = mag * jnp.cos(ang)
    ai = mag * jnp.sin(ang)
    nr = ar - 1.0
    ni = ai
    den = lam_re * lam_re + lam_im * lam_im
    cr = ((nr * lam_re + ni * lam_im) / den)[..., None]
    ci = ((ni * lam_re - nr * lam_im) / den)[..., None]
    bbr = cr * b_re - ci * b_im
    bbi = cr * b_im + ci * b_re
    gpp = SSM_WIN // SSM_STATE
    gpw = SSM_WIN // SSM_GROUP
    k2 = jnp.arange(N_PAIR)[:, None, None]
    gs = jnp.arange(gpp)[None, :, None]
    gl = jnp.arange(gpw)[None, None, :]
    same = (gl == gpp * (k2 % PAIRS_PER_WIN) + gs).astype(F32)

    def b_windows(bb):
        return jnp.einsum('kgl,kgpc->klcgp', same, bb.reshape(N_PAIR, gpp, SSM_STATE, SSM_GROUP)).reshape(
            N_PAIR, SSM_WIN, SSM_WIN)

    def c_windows(cc):
        return jnp.einsum('kgl,kgcp->kgplc', same, cc.reshape(N_PAIR, gpp, SSM_GROUP, SSM_STATE)).reshape(
            N_PAIR, SSM_WIN, SSM_WIN)

    pr, pi = ar, ai
    for _ in range(int(math.log2(SSM_TC))):
        pr, pi = pr * pr - pi * pi, 2.0 * pr * pi
    return dict(a_re=ar.reshape(N_SLAB, LANES), a_im=ai.reshape(N_SLAB, LANES),
                a64_re=pr.reshape(N_SLAB, LANES), a64_im=pi.reshape(N_SLAB, LANES),
                bw_re=b_windows(bbr), bw_im=b_windows(bbi), cw_re=c_windows(c_re), cw_im=c_windows(c_im))


def _layer_fwd(x, p, next_bufs=None):
    outs = _in_proj_fwd(x, p["g_mix"], p["w_in"], _gather_ici_rider(next_bufs) if next_bufs else None)
    (z, h), next_bufs = outs[:2], list(outs[2:])
    qkv = _qkv_prep(z, jnp.tile(p["g_q"], (1, N_HEADS)), jnp.tile(p["g_k"], (1, N_HEADS)))
    os_, ls_ = [], []
    for gi in range(N_GROUPS):
        o, l = _attn_fwd(*qkv[gi], gi)
        os_.append(o)
        ls_.append(l)
    a = _combine_fwd(os_, ls_)
    pk = _ssm_pack(p["lambda_re"], p["lambda_im"], p["log_dt"], p["b_re"], p["b_im"], p["c_re"], p["c_im"])
    ypre, yact, sd_re, sd_im = _ssm_fwd(z, pk, p["d_skip"])
    x1, aout, sa, sb, mix = _mix_fwd(x, z, a, yact, p["w_attn_proj"], p["w_glu_a"], p["w_glu_b"], p["w_out"])
    outs = _ffn_fwd(x1, p["g_ffn"], p["w_ffn_gate"], p["w_ffn_up"], p["w_ffn_down"],
                    _gather_d2d_rider(next_bufs) if next_bufs else None)
    (x2, h2, gate, up, act), next_full = outs[:5], list(outs[5:])
    saved = dict(x=x, z=z, h=h, qkv=qkv, os=os_, ls=ls_, pk=pk, ypre=ypre, yact=yact, sd_re=sd_re, sd_im=sd_im,
                 x1=x1, a=a, aout=aout, sa=sa, sb=sb, mix=mix, h2=h2, gate=gate, up=up, act=act)
    return x2, saved, next_full


def _rms_bwd(xv, g, dh):
    r = lax.rsqrt(jnp.mean(xv * xv, axis=-1, keepdims=True) + EPS)
    xn = xv * r
    dxn = dh * g
    dx = r * (dxn - xn * jnp.mean(dxn * xn, axis=-1, keepdims=True))
    dg = jnp.sum((dh * xn).reshape(xv.shape[0] // SUBLANES, SUBLANES, xv.shape[1]), axis=0)
    return dx, dg


def _ffn_bwd_act(dx2, gate, up, w_d):
    L = dx2.shape[0]
    fs = D_FF // N_CHIPS
    TM = TM_PROJ

    def body(dx_ref, gate_ref, up_ref, wd_ref, dgate_ref, dup_ref):
        dact = _dot_nt(dx_ref[...].astype(BF16), wd_ref[...])
        gt = gate_ref[...].astype(F32)
        sg = _sigmoid(gt)
        dgate_ref[...] = (dact * up_ref[...].astype(F32) * (sg * (1.0 + gt * (1.0 - sg)))).astype(BF16)
        dup_ref[...] = (dact * gt * sg).astype(BF16)

    ffs = pl.BlockSpec((None, TM, fs), lambda i, s: (s, i, 0))
    return _call(
        body, name="ffn_bwd_act", grid=(L // TM, N_CHIPS),
        in_specs=[pl.BlockSpec((TM, D_MODEL), lambda i, s: (i, 0)), ffs, ffs,
                  pl.BlockSpec((None, fs, D_MODEL), lambda i, s: (s, 0, 0))],
        out_specs=[ffs, ffs], out_shape=[_sds((N_CHIPS, L, fs), BF16)] * 2,
        sem=("parallel", "parallel"))(dx2, gate, up, w_d)


def _ffn_bwd_in(dx2, x1, g, dgate, dup, w_g, w_u, rider=None):
    L = x1.shape[0]
    fs = D_FF // N_CHIPS
    TM = TM_PROJ

    def body(dx_ref, x_ref, g_ref, dgate_ref, dup_ref, wg_ref, wu_ref, dx1_ref, dg_ref, acc, dgacc):
        i, s = pl.program_id(0), pl.program_id(1)

        @pl.when(s == 0)
        def _():
            acc[...] = jnp.zeros_like(acc)

        @pl.when(jnp.logical_and(i == 0, s == 0))
        def _():
            dgacc[...] = jnp.zeros_like(dgacc)

        acc[...] += _dot_nt(dgate_ref[...], wg_ref[...]) + _dot_nt(dup_ref[...], wu_ref[...])

        @pl.when(s == N_CHIPS - 1)
        def _():
            dx, dg = _rms_bwd(x_ref[...], g_ref[...], acc[...])
            dx1_ref[...] = dx_ref[...] + dx
            dgacc[...] += dg

        @pl.when(jnp.logical_and(i == pl.num_programs(0) - 1, s == N_CHIPS - 1))
        def _():
            dg_ref[...] = jnp.sum(dgacc[...], axis=0, keepdims=True)

    tok = pl.BlockSpec((TM, D_MODEL), lambda i, s: (i, 0))
    ffs = pl.BlockSpec((None, TM, fs), lambda i, s: (s, i, 0))
    vec = pl.BlockSpec((1, D_MODEL), lambda i, s: (0, 0))
    return _call(
        body, name="ffn_bwd_in", grid=(L // TM, N_CHIPS),
        in_specs=[tok, tok, vec, ffs, ffs,
                  pl.BlockSpec((None, D_MODEL, fs), lambda i, s: (s, 0, 0)),
                  pl.BlockSpec((None, D_MODEL, fs), lambda i, s: (s, 0, 0))],
        out_specs=[tok, vec],
        out_shape=[_sds((L, D_MODEL), F32), _sds((1, D_MODEL), F32)],
        scratch=[pltpu.VMEM((TM, D_MODEL), F32), pltpu.VMEM((SUBLANES, D_MODEL), F32)],
        sem=("arbitrary", "arbitrary"), rider=rider)(dx2, x1, g, dgate, dup, w_g, w_u)


def _wgrad(a, b, *, name, grid_kn, a_spec, b_spec, out_shape, out_spec):
    L = a.shape[-2]
    nl = L // TL_WGRAD

    def body(a_ref, b_ref, o_ref):
        @pl.when(pl.program_id(2) == 0)
        def _():
            o_ref[...] = jnp.zeros_like(o_ref)
        o_ref[...] += _dot_tn(a_ref[...].astype(BF16), b_ref[...].astype(BF16))

    return _call(body, name=name, grid=(*grid_kn, nl), in_specs=[a_spec, b_spec], out_specs=out_spec,
                 out_shape=out_shape, sem=("parallel", "parallel", "arbitrary"))(a, b)


def _wgrad_cols(a, b, name):
    K, N = a.shape[1], b.shape[1]
    ns = N // N_CHIPS
    if N * K * 4 <= 4 * 1024 * 1024:
        L = a.shape[0]

        def body(a_ref, b_ref, o_ref):
            @pl.when(pl.program_id(0) == 0)
            def _():
                o_ref[...] = jnp.zeros_like(o_ref)
            av = a_ref[...].astype(BF16)
            for s in range(N_CHIPS):
                o_ref[s] += _dot_tn(av, b_ref[:, s * ns:(s + 1) * ns].astype(BF16))

        return _call(body, name=name, grid=(L // TL_WGRAD,),
                     in_specs=[pl.BlockSpec((TL_WGRAD, K), lambda t: (t, 0)),
                               pl.BlockSpec((TL_WGRAD, N), lambda t: (t, 0))],
                     out_specs=pl.BlockSpec((N_CHIPS, K, ns), lambda t: (0, 0, 0)),
                     out_shape=_sds((N_CHIPS, K, ns), F32), sem=("arbitrary",))(a, b)
    tn = ns // 2 if ns % (2 * LANES) == 0 else ns
    nj = ns // tn
    return _wgrad(a, b, name=name, grid_kn=(1, N_CHIPS * nj),
                  a_spec=pl.BlockSpec((TL_WGRAD, K), lambda i, j, t: (t, 0)),
                  b_spec=pl.BlockSpec((TL_WGRAD, tn), lambda i, j, t: (t, j)),
                  out_shape=_sds((N_CHIPS, K, ns), F32),
                  out_spec=pl.BlockSpec((None, K, tn), lambda i, j, t: (j // nj, 0, j % nj)))


def _wgrad_full(a, b, name):
    K, N = a.shape[1], b.shape[1]
    return _wgrad(a, b, name=name, grid_kn=(1, 1),
                  a_spec=pl.BlockSpec((TL_WGRAD, K), lambda i, j, t: (t, 0)),
                  b_spec=pl.BlockSpec((TL_WGRAD, N), lambda i, j, t: (t, 0)),
                  out_shape=_sds((K, N), F32), out_spec=pl.BlockSpec((K, N), lambda i, j, t: (0, 0)))


def _wgrad_ff_cols(a, b, name):
    K, fs = a.shape[1], b.shape[2]
    return _wgrad(a, b, name=name, grid_kn=(1, N_CHIPS),
                  a_spec=pl.BlockSpec((TL_WGRAD, K), lambda i, j, t: (t, 0)),
                  b_spec=pl.BlockSpec((None, TL_WGRAD, fs), lambda i, j, t: (j, t, 0)),
                  out_shape=_sds((N_CHIPS, K, fs), F32),
                  out_spec=pl.BlockSpec((None, K, fs), lambda i, j, t: (j, 0, 0)))


def _wgrad_ff_rows(a, b, name):
    fs, N = a.shape[2], b.shape[1]
    return _wgrad(a, b, name=name, grid_kn=(N_CHIPS, 1),
                  a_spec=pl.BlockSpec((None, TL_WGRAD, fs), lambda i, j, t: (i, t, 0)),
                  b_spec=pl.BlockSpec((TL_WGRAD, N), lambda i, j, t: (t, 0)),
                  out_shape=_sds((N_CHIPS, fs, N), F32),
                  out_spec=pl.BlockSpec((None, fs, N), lambda i, j, t: (i, 0, 0)))


def _mix_bwd(dx, z, aout, sa, sb, ypre, w_ap, w_ga, w_gb, w_out, rider=None):
    L = dx.shape[0]
    cs = D_MODEL // N_CHIPS
    ga_col = (3 * N_GROUPS * ATTN_WIDTH + SSM_WIDTH) // D_MODEL

    def body(dx_ref, ga_ref, gs_ref, aout_ref, sa_ref, sb_ref, ypre_ref, wap_ref, wga_ref, wgb_ref, wout_ref,
             dgates_ref, da_ref, gy_ref, daout_ref, dsa_ref, dsb_ref):
        dmix = _dot_nt(dx_ref[...].astype(BF16), wout_ref[...])
        sig_a = _sigmoid(ga_ref[...].astype(F32))
        sig_s = _sigmoid(gs_ref[...].astype(F32))
        a_out = aout_ref[...].astype(F32)
        s_a = sa_ref[...].astype(F32)
        sig_b = _sigmoid(sb_ref[...].astype(F32))
        s_out = s_a * sig_b
        daout = (dmix * sig_a).astype(BF16)
        daout_ref[...] = daout
        dgates_ref[:, :D_MODEL] = (dmix * a_out * sig_a * (1.0 - sig_a)).astype(BF16)
        dgates_ref[:, D_MODEL:] = (dmix * s_out * sig_s * (1.0 - sig_s)).astype(BF16)
        ds_out = dmix * sig_s
        dsa = (ds_out * sig_b).astype(BF16)
        dsb = (ds_out * s_a * sig_b * (1.0 - sig_b)).astype(BF16)
        dsa_ref[...] = dsa
        dsb_ref[...] = dsb
        da = jnp.zeros((TM_MIX, ATTN_WIDTH), F32)
        dy = jnp.zeros((TM_MIX, SSM_WIDTH), F32)
        for s in range(N_CHIPS):
            cols = slice(s * cs, (s + 1) * cs)
            da += _dot_nt(daout[:, cols], wap_ref[s])
            dy += _dot_nt(dsa[:, cols], wga_ref[s]) + _dot_nt(dsb[:, cols], wgb_ref[s])
        gy_ref[...] = dy * _gelu_grad(ypre_ref[...])
        da_ref[...] = da

    tok = lambda w: pl.BlockSpec((TM_MIX, w), lambda i: (i, 0))
    wsm = pl.BlockSpec((N_CHIPS, ATTN_WIDTH, cs), lambda i: (0, 0, 0))
    return _call(
        body, name="mix_bwd", grid=(L // TM_MIX,),
        in_specs=[tok(D_MODEL), pl.BlockSpec((TM_MIX, D_MODEL), lambda i: (i, ga_col)),
                  pl.BlockSpec((TM_MIX, D_MODEL), lambda i: (i, ga_col + 1)),
                  tok(D_MODEL), tok(D_MODEL), tok(D_MODEL), tok(SSM_WIDTH),
                  wsm, wsm, wsm, pl.BlockSpec((D_MODEL, D_MODEL), lambda i: (0, 0))],
        out_specs=[tok(2 * D_MODEL), tok(ATTN_WIDTH), tok(SSM_WIDTH)] + [tok(D_MODEL)] * 3,
        out_shape=[_sds((L, 2 * D_MODEL), BF16), _sds((L, ATTN_WIDTH), F32), _sds((L, SSM_WIDTH), F32)]
                  + [_sds((L, D_MODEL), BF16)] * 3,
        sem=("parallel",), rider=rider)(dx, z, z, aout, sa, sb, ypre, w_ap, w_ga, w_gb,
                                        w_out.reshape(D_MODEL, D_MODEL))


def _combine_bwd(da, os_, ls_):
    L = da.shape[0]

    def body(da_ref, o0, o1, o2, l0, l1, l2, hs_ref, p1_ref, p2_ref, pt1_ref, pt2_ref,
             do0, do1, do2, c0, c1, c2):
        o_tok, l_tok = _to_token_order((o0[...], o1[...], o2[...]), (l0[...], l1[...], l2[...]),
                                       (None, pt1_ref[...], pt2_ref[...]))
        w = _combine_weights(*l_tok)
        dav = da_ref[...]
        hs = hs_ref[...]
        tbar = sum(wg * _head_sum(dav * og, hs) for wg, og in zip(w, o_tok))
        for wg, pm, do_ref, c_ref in zip(w, (None, p1_ref[...], p2_ref[...]), (do0, do1, do2), (c0, c1, c2)):
            dog = (wg * dav).astype(BF16)
            cg = -wg * tbar
            do_ref[...] = dog if pm is None else _dot(pm, dog).astype(BF16)
            c_ref[...] = cg if pm is None else _dot(pm, cg.astype(BF16))

    tile = pl.BlockSpec((DL_TILE, ATTN_WIDTH), lambda i: (i, 0))
    mat = pl.BlockSpec((DL_TILE, DL_TILE), lambda i: (0, 0))
    p1, p2 = _perm_matrix(ATTN_PATTERN[1][1]), _perm_matrix(ATTN_PATTERN[2][1])
    outs = _call(body, name="combine_bwd", grid=(L // DL_TILE,), in_specs=[tile] * 7 + [mat] * 5,
                 out_specs=[tile] * 6,
                 out_shape=[_sds((L, ATTN_WIDTH), BF16)] * 3 + [_sds((L, ATTN_WIDTH), F32)] * 3,
                 sem=("parallel",))(da, *os_, *ls_, _head_sum_matrix(), p1, p2, p1.T, p2.T)
    return outs[:3], outs[3:]


def _attn_bwd(qs, ks, v, do, l, c, gi, rider=None):
    L = qs.shape[0]
    _, d = ATTN_PATTERN[gi]
    nb = L // (d * BLK)

    def body(q0_ref, q1_ref, k_ref, v_ref, do0_ref, do1_ref, l0_ref, l1_ref, c0_ref, c1_ref,
             dq_ref, dk_ref, dv_ref, carry):
        n = pl.program_id(1)

        @pl.when(n == 0)
        def _():
            carry[...] = jnp.zeros_like(carry)

        qi = lax.broadcasted_iota(jnp.int32, (2 * BLK, BLK), 0)
        kj = lax.broadcasted_iota(jnp.int32, (2 * BLK, BLK), 1)
        first = qi < BLK
        mask = jnp.logical_and(jnp.where(first, qi, kj) >= jnp.where(first, kj, qi - BLK),
                               qi < jnp.where(n < nb - 1, 2 * BLK, BLK))
        q2 = jnp.concatenate([_dl_read(q0_ref), _dl_read(q1_ref)], axis=0)
        do2 = jnp.concatenate([_dl_read(do0_ref), _dl_read(do1_ref)], axis=0)
        l2 = jnp.concatenate([_dl_read(l0_ref), _dl_read(l1_ref)], axis=0)
        c2 = jnp.concatenate([_dl_read(c0_ref), _dl_read(c1_ref)], axis=0)
        k = _dl_read(k_ref)
        v_ = _dl_read(v_ref)
        dq_parts, dk_parts, dv_parts = [], [], []
        for hp in range(N_HEADS // 2):
            ls = slice(hp * LANES, (hp + 1) * LANES)
            qp, dop, kp_, vp_ = q2[:, ls], do2[:, ls], k[:, ls], v_[:, ls]
            dq2 = jnp.zeros((2 * BLK, LANES), F32)
            dkp = jnp.zeros((BLK, LANES), F32)
            dvp = jnp.zeros((BLK, LANES), F32)
            for e, he in enumerate(_pair_masks()):
                col = slice(hp * LANES + e * HEAD_DIM, hp * LANES + e * HEAD_DIM + 1)
                ke = jnp.where(he, kp_, 0)
                p = jnp.where(mask, jnp.exp(_dot_nt(qp, ke) - l2[:, col]), 0.0)
                ds = (p * (_dot_nt(dop, jnp.where(he, vp_, 0)) + c2[:, col])).astype(BF16)
                dvp += _dot_tn(p.astype(BF16), jnp.where(he, dop, 0))
                dkp += _dot_tn(ds, jnp.where(he, qp, 0))
                dq2 += _dot(ds, ke)
            dq_parts.append((dq2[:BLK] + carry[:, ls]).astype(BF16))
            carry[:, ls] = dq2[BLK:]
            dk_parts.append(dkp.astype(BF16))
            dv_parts.append(dvp.astype(BF16))
        _dl_write(dq_ref, jnp.concatenate(dq_parts, axis=1))
        _dl_write(dk_ref, jnp.concatenate(dk_parts, axis=1))
        _dl_write(dv_ref, jnp.concatenate(dv_parts, axis=1))

    cur = _dl_spec(d, ATTN_WIDTH, lambda n: n)
    nxt = _dl_spec(d, ATTN_WIDTH, lambda n: jnp.minimum(n + 1, nb - 1))
    view = lambda t: _dl_view(t, d)
    outs = _call(
        body, name=f"attn_bwd_g{gi}", grid=(d, nb),
        in_specs=[cur, nxt, cur, cur, cur, nxt, cur, nxt, cur, nxt], out_specs=[cur, cur, cur],
        out_shape=[_sds(view(qs).shape, BF16)] * 3, scratch=[pltpu.VMEM((BLK, ATTN_WIDTH), F32)],
        sem=("parallel", "arbitrary"), rider=rider)(view(qs), view(qs), view(ks), view(v), view(do), view(do),
                                                    view(l), view(l), view(c), view(c))
    return [t.reshape(L, ATTN_WIDTH) for t in outs[:3]], list(outs[3:])


def _qkv_post(z, dqkv, du, dgates, gq_t, gk_t):
    L = z.shape[0]
    qkv_w = N_GROUPS * ATTN_WIDTH

    def body(zq_ref, zk_ref, gq_ref, gk_ref, hs_ref, pt1_ref, pt2_ref, du_ref, dgates_ref, *rest):
        dl_refs, (dz_ref, dgq_ref, dgk_ref) = rest[:9], rest[9:]

        @pl.when(pl.program_id(0) == 0)
        def _():
            dgq_ref[...] = jnp.zeros_like(dgq_ref)
            dgk_ref[...] = jnp.zeros_like(dgk_ref)

        hs = hs_ref[...]
        pts = (None, pt1_ref[...], pt2_ref[...])

        def rows8(t):
            return jnp.sum(t.reshape(DL_TILE // SUBLANES, SUBLANES, ATTN_WIDTH), axis=0)

        def norm_bwd(x, gain, dn):
            r = lax.rsqrt(_head_sum(x * x, hs) * (1.0 / HEAD_DIM) + EPS)
            xh = x * r
            dh = dn * gain
            return r * (dh - xh * (_head_sum(dh * xh, hs) * (1.0 / HEAD_DIM))), rows8(dn * xh)

        for g in range(N_GROUPS):
            tok = [t[...].astype(F32) if pts[g] is None else _dot(pts[g], t[...]) for t in dl_refs[3 * g:3 * g + 3]]
            cols = slice(g * ATTN_WIDTH, (g + 1) * ATTN_WIDTH)
            dq, pq = norm_bwd(zq_ref[:, cols].astype(F32), gq_ref[...] * SCALE, tok[0])
            dk, pk_ = norm_bwd(zk_ref[:, cols].astype(F32), gk_ref[...], tok[1])
            dgq_ref[...] += pq * SCALE
            dgk_ref[...] += pk_
            dz_ref[:, cols] = dq.astype(BF16)
            dz_ref[:, qkv_w + g * ATTN_WIDTH:qkv_w + (g + 1) * ATTN_WIDTH] = dk.astype(BF16)
            dz_ref[:, 2 * qkv_w + g * ATTN_WIDTH:2 * qkv_w + (g + 1) * ATTN_WIDTH] = tok[2].astype(BF16)
        dz_ref[:, 3 * qkv_w:3 * qkv_w + SSM_WIDTH] = du_ref[...]
        dz_ref[:, 3 * qkv_w + SSM_WIDTH:] = dgates_ref[...]

    tile = lambda w: pl.BlockSpec((DL_TILE, w), lambda i: (i, 0))
    mat = pl.BlockSpec((DL_TILE, DL_TILE), lambda i: (0, 0))
    vec = pl.BlockSpec((1, ATTN_WIDTH), lambda i: (0, 0))
    acc = pl.BlockSpec((SUBLANES, ATTN_WIDTH), lambda i: (0, 0))
    flat = [t for grp in dqkv for t in grp]
    return _call(
        body, name="qkv_post", grid=(L // DL_TILE,),
        in_specs=[tile(qkv_w), pl.BlockSpec((DL_TILE, qkv_w), lambda i: (i, 1)), vec, vec, mat, mat, mat,
                  tile(SSM_WIDTH), tile(2 * D_MODEL)] + [tile(ATTN_WIDTH)] * 9,
        out_specs=[tile(IN_COLS), acc, acc],
        out_shape=[_sds((L, IN_COLS), BF16), _sds((SUBLANES, ATTN_WIDTH), F32), _sds((SUBLANES, ATTN_WIDTH), F32)],
        sem=("arbitrary",))(z, z, gq_t, gk_t, _head_sum_matrix(), _perm_matrix(ATTN_PATTERN[1][1]).T,
                            _perm_matrix(ATTN_PATTERN[2][1]).T, du, dgates, *flat)


def _scan_rev_grad(sre, sim, rre, rim, are_ref, aim_ref, k0, init, seed_re, seed_im):
    ar = [_bcast(are_ref, k0 + kk) for kk in range(SCAN_GROUP)]
    ai = [-_bcast(aim_ref, k0 + kk) for kk in range(SCAN_GROUP)]

    def update(i, xprev, carry):
        out = []
        for kk in range(SCAN_GROUP):
            k = k0 + kk
            lr, li, dr, di = carry[4 * kk:4 * kk + 4]
            nr = ar[kk] * lr - ai[kk] * li + rre[k, _rows(i), :]
            ni = ar[kk] * li + ai[kk] * lr + rim[k, _rows(i), :]
            rre[k, _rows(i), :] = nr
            rim[k, _rows(i), :] = ni
            xr, xi = xprev(k)
            out += [nr, ni, dr + xr * nr + xi * ni, di + xr * ni - xi * nr]
        return tuple(out)

    def step(t, carry):
        i = SSM_TC - 1 - t
        return update(i, lambda k: (sre[k, _rows(i - 1), :], sim[k, _rows(i - 1), :]), carry)

    zero = jnp.zeros((SSM_SUB, LANES), F32)
    flat = []
    for re, im in init:
        flat += [re, im, zero, zero]
    res = lax.fori_loop(0, SSM_TC - 1, step, tuple(flat), unroll=3)
    res = update(0, lambda k: (seed_re[k], seed_im[k]), res)
    return [(res[4 * kk + 2], res[4 * kk + 3]) for kk in range(SCAN_GROUP)]


def _ssm_bwd(z, gy, pk, dskip, sd_re, sd_im, rider=None):
    L = z.shape[0]
    nb = L // SSM_TB
    ucol = (3 * N_GROUPS * ATTN_WIDTH) // SSM_WIDTH
    nwin = N_PAIR // PAIRS_PER_WIN

    def body(u_ref, gy_ref, are_ref, aim_ref, a64re_ref, a64im_ref, bwre_ref, bwim_ref, cwre_ref, cwim_ref, d_ref,
             sdre_ref, sdim_ref,
             du_ref, dare_ref, daim_ref, dbre_ref, dbim_ref, dcre_ref, dcim_ref, dd_ref,
             sre, sim, rre, rim, carry_re, carry_im, ends_re, ends_im, seed_re, seed_im):
        @pl.when(pl.program_id(0) == 0)
        def _():
            carry_re[...] = jnp.zeros_like(carry_re)
            carry_im[...] = jnp.zeros_like(carry_im)
            for ref in (dare_ref, daim_ref, dbre_ref, dbim_ref, dcre_ref, dcim_ref, dd_ref):
                ref[...] = jnp.zeros_like(ref)

        u = u_ref[...]
        gyv = gy_ref[...]
        gyb = gyv.astype(BF16)
        _ssm_fill(u, bwre_ref, bwim_ref, sre, sim)
        for k2 in range(N_PAIR):
            gw = gyb[:, _win_cols(k2)]
            _to_slabs(rre, k2, _dot_nt(gw, cwre_ref[k2]))
            _to_slabs(rim, k2, -_dot_nt(gw, cwim_ref[k2]))
        zero = jnp.zeros((SSM_SUB, LANES), F32)
        for k0 in range(0, N_SLAB, SCAN_GROUP):
            grp = range(k0, k0 + SCAN_GROUP)
            _scan(sre, sim, are_ref, aim_ref, k0, [(sdre_ref[k], sdim_ref[k]) for k in grp],
                  reverse=False, store=True)
            ends = _scan(rre, rim, are_ref, aim_ref, k0, [(zero, zero)] * SCAN_GROUP, reverse=True, store=False,
                         sign=-1.0)
            for kk, k in enumerate(grp):
                ends_re[k] = ends[kk][0]
                ends_im[k] = ends[kk][1]
            for k in grp:
                _ssm_seeds(ends_re, ends_im, a64re_ref, a64im_ref, carry_re, carry_im, seed_re, seed_im, k,
                           reverse=True, sign=-1.0)
            das = _scan_rev_grad(sre, sim, rre, rim, are_ref, aim_ref, k0,
                                 [(seed_re[k], seed_im[k]) for k in grp], sdre_ref, sdim_ref)
            for kk, k in enumerate(grp):
                dare_ref[k] += das[kk][0]
                daim_ref[k] += das[kk][1]
        for w in range(nwin):
            cols = _win_cols(w * PAIRS_PER_WIN)
            uw = u[:, cols]
            gw = gyb[:, cols]
            acc = gyv[:, cols] * d_ref[:, cols]
            for kk in range(PAIRS_PER_WIN):
                k2 = w * PAIRS_PER_WIN + kk
                lr = _pair_rows(rre, k2)
                li = _pair_rows(rim, k2)
                acc += _dot_nt(lr, bwre_ref[k2]) + _dot_nt(li, bwim_ref[k2])
                dbre_ref[k2] += _dot_tn(uw, lr)
                dbim_ref[k2] += _dot_tn(uw, li)
                dcre_ref[k2] += _dot_tn(_pair_rows(sre, k2), gw)
                dcim_ref[k2] -= _dot_tn(_pair_rows(sim, k2), gw)
            du_ref[:, cols] = acc.astype(BF16)
        dd_ref[...] += jnp.sum((gyv * u.astype(F32)).reshape(SSM_TB // SUBLANES, SUBLANES, SSM_WIDTH), axis=0)

    c2, c3 = _ssm_specs_consts()
    rev = lambda b: nb - 1 - b
    seed_spec = pl.BlockSpec((None, N_SLAB, SSM_SUB, LANES), lambda b: (rev(b), 0, 0, 0))
    tile_out = pl.BlockSpec((N_SLAB, SSM_SUB, LANES), lambda b: (0, 0, 0))
    small = pltpu.VMEM((N_SLAB, LANES), F32)
    tile = pltpu.VMEM((N_SLAB, SSM_SUB, LANES), F32)
    return _call(
        body, name="ssm_bwd", grid=(nb,),
        in_specs=[pl.BlockSpec((SSM_TB, SSM_WIDTH), lambda b: (rev(b), ucol)),
                  pl.BlockSpec((SSM_TB, SSM_WIDTH), lambda b: (rev(b), 0)),
                  c2, c2, c2, c2, c3, c3, c3, c3, pl.BlockSpec((1, SSM_WIDTH), lambda b: (0, 0)),
                  seed_spec, seed_spec],
        out_specs=[pl.BlockSpec((SSM_TB, SSM_WIDTH), lambda b: (rev(b), 0)), tile_out, tile_out, c3, c3, c3, c3,
                   pl.BlockSpec((SUBLANES, SSM_WIDTH), lambda b: (0, 0))],
        out_shape=[_sds((L, SSM_WIDTH), BF16), _sds((N_SLAB, SSM_SUB, LANES), F32),
                   _sds((N_SLAB, SSM_SUB, LANES), F32)] + [_sds((N_PAIR, SSM_WIN, SSM_WIN), F32)] * 4
                  + [_sds((SUBLANES, SSM_WIDTH), F32)],
        scratch=_ssm_scratch() + _ssm_scratch() + [small, small, tile, tile, tile, tile],
        sem=("arbitrary",), rider=rider)(z, gy, pk["a_re"], pk["a_im"], pk["a64_re"], pk["a64_im"],
                            pk["bw_re"].astype(BF16), pk["bw_im"].astype(BF16),
                            pk["cw_re"].astype(BF16), pk["cw_im"].astype(BF16), dskip, sd_re, sd_im)


def _in_proj_bwd(dz, w, x, g, dres, rider=None):
    L = x.shape[0]
    ns = w.shape[2]
    tn = ns
    nj = ns // tn
    nt = N_CHIPS * nj
    TM = TM_PROJ

    def body(dz_ref, w_ref, x_ref, g_ref, dres_ref, dx_ref, dg_ref, acc, dgacc):
        i, j = pl.program_id(0), pl.program_id(1)

        @pl.when(j == 0)
        def _():
            acc[...] = jnp.zeros_like(acc)

        @pl.when(jnp.logical_and(i == 0, j == 0))
        def _():
            dgacc[...] = jnp.zeros_like(dgacc)

        acc[...] += _dot_nt(dz_ref[...], w_ref[...])

        @pl.when(j == nt - 1)
        def _():
            dx, dg = _rms_bwd(x_ref[...], g_ref[...], acc[...])
            dx_ref[...] = dres_ref[...] + dx
            dgacc[...] += dg

        @pl.when(jnp.logical_and(i == pl.num_programs(0) - 1, j == nt - 1))
        def _():
            dg_ref[...] = jnp.sum(dgacc[...], axis=0, keepdims=True)

    tok = pl.BlockSpec((TM, D_MODEL), lambda i, j: (i, 0))
    vec = pl.BlockSpec((1, D_MODEL), lambda i, j: (0, 0))
    return _call(
        body, name="in_proj_bwd", grid=(L // TM, nt),
        in_specs=[pl.BlockSpec((TM, tn), lambda i, j: (i, j)),
                  pl.BlockSpec((None, D_MODEL, tn), lambda i, j: (j // nj, 0, j % nj)), tok, vec, tok],
        out_specs=[tok, vec],
        out_shape=[_sds((L, D_MODEL), F32), _sds((1, D_MODEL), F32)],
        scratch=[pltpu.VMEM((TM, D_MODEL), F32), pltpu.VMEM((SUBLANES, D_MODEL), F32)],
        sem=("arbitrary", "arbitrary"), rider=rider)(dz, w, x, g, dres)


SSM_PARAMS = ("lambda_re", "lambda_im", "log_dt", "b_re", "b_im", "c_re", "c_im")
EARLY = ("w_ffn_gate", "w_ffn_up", "w_ffn_down")
LATE = ("w_in", "w_attn_proj", "w_glu_a", "w_glu_b", "w_out")


def _layer_bwd(dx2, sv, p, pending, owned, l, idx):
    chip_idx, c_idx = idx
    g = {}
    owned = dict(owned)

    def settle(name, partial, arrived, layer):
        owned[name] = _sum_owner(partial, arrived, owned[name], layer, chip_idx, c_idx)

    dgate, dup = _ffn_bwd_act(dx2, sv["gate"], sv["up"], p["w_ffn_down"])
    outs = _ffn_bwd_in(dx2, sv["x1"], p["g_ffn"], dgate, dup, p["w_ffn_gate"], p["w_ffn_up"],
                       _scatter_rider([pending[n] for n in LATE[1:]]) if pending else None)
    dx1, g["g_ffn"] = outs[:2]
    for n, t in zip(LATE[1:], outs[2:]):
        settle(n, pending[n], t, l + 1)
    g["w_ffn_gate"] = _wgrad_ff_cols(sv["h2"], dgate, "wgrad_ffn_gate")
    g["w_ffn_up"] = _wgrad_ff_cols(sv["h2"], dup, "wgrad_ffn_up")
    g["w_ffn_down"] = _wgrad_ff_rows(sv["act"], dx2, "wgrad_ffn_down")

    outs = _mix_bwd(dx1, sv["z"], sv["aout"], sv["sa"], sv["sb"], sv["ypre"], p["w_attn_proj"], p["w_glu_a"],
                    p["w_glu_b"], p["w_out"], _swap_rider([g[n] for n in EARLY]))
    dgates, da, gy, daout, dsa, dsb = outs[:6]
    early = [_add_half(g[n], s, c_idx) for n, s in zip(EARLY, outs[6:])]
    g["w_out"] = _wgrad_full(sv["mix"], dx1, "wgrad_out").reshape(N_CHIPS, D_MODEL // N_CHIPS, D_MODEL)
    g["w_attn_proj"] = _wgrad_cols(sv["a"], daout, "wgrad_attn_proj")
    g["w_glu_a"] = _wgrad_cols(sv["yact"], dsa, "wgrad_glu_a")
    g["w_glu_b"] = _wgrad_cols(sv["yact"], dsb, "wgrad_glu_b")

    outs = _ssm_bwd(sv["z"], gy, sv["pk"], p["d_skip"], sv["sd_re"], sv["sd_im"],
                    _scatter_rider([pending[LATE[0]]]) if pending else None)
    du, da_re, da_im, dbw_re, dbw_im, dcw_re, dcw_im, dd = outs[:8]
    if pending:
        settle(LATE[0], pending[LATE[0]], outs[8], l + 1)
    g["d_skip"] = jnp.sum(dd, axis=0, keepdims=True)
    _, pull = jax.vjp(_ssm_pack, *[p[n] for n in SSM_PARAMS])
    zeros = jnp.zeros((N_SLAB, LANES), F32)
    ct = dict(a_re=jnp.sum(da_re, axis=1), a_im=jnp.sum(da_im, axis=1), a64_re=zeros, a64_im=zeros,
              bw_re=dbw_re, bw_im=dbw_im, cw_re=dcw_re, cw_im=dcw_im)
    for n, v in zip(SSM_PARAMS, pull(ct)):
        g[n] = v

    dos, cs = _combine_bwd(da, sv["os"], sv["ls"])
    dqkv = []
    for gi in range(N_GROUPS):
        grads, arrived = _attn_bwd(*sv["qkv"][gi], dos[gi], sv["ls"][gi], cs[gi], gi, _scatter_rider([early[gi]]))
        settle(EARLY[gi], early[gi], arrived[0], l)
        dqkv.append(grads)
    dz, gq8, gk8 = _qkv_post(sv["z"], dqkv, du, dgates, jnp.tile(p["g_q"], (1, N_HEADS)),
                             jnp.tile(p["g_k"], (1, N_HEADS)))
    g["g_q"] = jnp.sum(gq8.reshape(SUBLANES * N_HEADS, HEAD_DIM), axis=0, keepdims=True)
    g["g_k"] = jnp.sum(gk8.reshape(SUBLANES * N_HEADS, HEAD_DIM), axis=0, keepdims=True)
    g["w_in"] = _wgrad_cols(sv["h"], dz, "wgrad_in")
    outs = _in_proj_bwd(dz, p["w_in"], sv["x"], p["g_mix"], dx1, _swap_rider([g[n] for n in LATE]))
    dx, g["g_mix"] = outs[:2]
    late = {n: _add_half(g[n], s, c_idx) for n, s in zip(LATE, outs[2:])}
    return dx, g, late, owned


def _place():
    x, y, c = lax.axis_index("x"), lax.axis_index("y"), lax.axis_index("c")
    others = [(1 - x, y), (x, 1 - y), (1 - x, 1 - y)]
    return x, y, c, others


def _half(ref, hc):
    rows = ref.shape[-2] // 2
    idx = (slice(None),) * (len(ref.shape) - 2) + (pl.ds(hc * rows, rows), slice(None))
    return ref.at[idx]


def _comm_call(body, name, ins, out_shapes, n_remote, aliases=None):
    scratch = [pltpu.SemaphoreType.DMA((n_remote,)), pltpu.SemaphoreType.DMA((n_remote,))]
    return pl.pallas_call(
        body, name=name, in_specs=[ANY] * len(ins), out_specs=[ANY] * len(out_shapes), out_shape=out_shapes,
        scratch_shapes=scratch, input_output_aliases=aliases or {})(*ins)


def _cast_place(w, l, chip_idx):
    _, R, C = w.shape
    tr = R // 2

    def body(me_ref, w_ref, o_ref):
        o_ref[...] = w_ref[...].astype(BF16)

    return _call(body, name=f"cast_place_l{l}", grid=(R // tr,), prefetch=1,
                 in_specs=[pl.BlockSpec((None, tr, C), lambda i, me_ref: (l, i, 0))],
                 out_specs=pl.BlockSpec((None, tr, C), lambda i, me_ref: (me_ref[0], i, 0)),
                 out_shape=_sds((N_CHIPS, R, C), BF16), sem=("arbitrary",))(chip_idx, w)


def _in_place_rider(bufs, pairs):
    n = len(bufs)

    def copies(outs, send, recv, side):
        return [pltpu.make_async_remote_copy(src_ref=pair[side][0], dst_ref=pair[side][0], send_sem=send.at[k],
                                             recv_sem=recv.at[k], device_id=pair[side][1], device_id_type=MESH)
                for k, pair in enumerate(pairs(outs))]

    def start(ins, outs, send, recv):
        for cp in copies(outs, send, recv, 0):
            cp.start()

    def wait(ins, outs, send, recv):
        for cp in copies(outs, send, recv, 1):
            cp.wait_recv()
        for cp in copies(outs, send, recv, 0):
            cp.wait_send()

    return Rider(list(bufs), [_sds(b.shape, b.dtype) for b in bufs], 3 * n, start, wait, {a: a for a in range(n)})


def _gather_ici_rider(bufs):
    def pairs(outs):
        x, y, c, others = _place()
        return [((_half(o.at[2 * x + y], c), (cx, cy, c)), (_half(o.at[2 * cx + cy], c), (cx, cy, c)))
                for o in outs for cx, cy in others]
    return _in_place_rider(bufs, pairs)


def _gather_d2d_rider(bufs):
    def pairs(outs):
        x, y, c, others = _place()
        sib = (x, y, 1 - c)
        return [((_half(o.at[2 * cx + cy], c), sib), (_half(o.at[2 * cx + cy], 1 - c), sib))
                for o in outs for cx, cy in others]
    return _in_place_rider(bufs, pairs)


def _swap_rider(gs):
    n = len(gs)

    def copies(ins, outs, send, recv):
        x, y, c, _ = _place()
        return [pltpu.make_async_remote_copy(src_ref=_half(ins[a], 1 - c), dst_ref=outs[a], send_sem=send.at[a],
                                             recv_sem=recv.at[a], device_id=(x, y, 1 - c), device_id_type=MESH)
                for a in range(n)]

    def start(ins, outs, send, recv):
        for cp in copies(ins, outs, send, recv):
            cp.start()

    def wait(ins, outs, send, recv):
        for cp in copies(ins, outs, send, recv):
            cp.wait()

    outs = [_sds((g.shape[0], g.shape[1] // 2, g.shape[2]), g.dtype) for g in gs]
    return Rider(list(gs), outs, n, start, wait, {})


def _scatter_rider(ss):
    n = len(ss)

    def copies(ins, outs, send, recv):
        x, y, c, others = _place()
        return [pltpu.make_async_remote_copy(
            src_ref=ins[a].at[2 * cx + cy], dst_ref=outs[a].at[j], send_sem=send.at[3 * a + j],
            recv_sem=recv.at[3 * a + j], device_id=(cx, cy, c), device_id_type=MESH)
            for a in range(n) for j, (cx, cy) in enumerate(others)]

    def start(ins, outs, send, recv):
        for cp in copies(ins, outs, send, recv):
            cp.start()

    def wait(ins, outs, send, recv):
        for cp in copies(ins, outs, send, recv):
            cp.wait()

    outs = [_sds((N_CHIPS - 1,) + s.shape[1:], s.dtype) for s in ss]
    return Rider(list(ss), outs, 3 * n, start, wait, {})


def _run_rider(rider, name):
    n_in = len(rider.ins)

    def body(*refs):
        ins, outs = refs[:n_in], refs[n_in:n_in + len(rider.out_shapes)]
        send, recv = refs[n_in + len(rider.out_shapes):]
        rider.start(ins, outs, send, recv)
        rider.wait(ins, outs, send, recv)

    return _comm_call(body, name, rider.ins, rider.out_shapes, rider.n_sem, aliases=rider.aliases)


def _join_halves(bufs):
    n = len(bufs)

    def body(*refs):
        outs = refs[n:2 * n]
        send, recv = refs[2 * n:]
        x, y, c, _ = _place()

        def swap(a, hc):
            region = _half(outs[a], hc)
            return pltpu.make_async_remote_copy(src_ref=region, dst_ref=region, send_sem=send.at[a],
                                                recv_sem=recv.at[a], device_id=(x, y, 1 - c), device_id_type=MESH)

        cps = [swap(a, c) for a in range(n)]
        for cp in cps:
            cp.start()
        for a in range(n):
            swap(a, 1 - c).wait_recv()
        for cp in cps:
            cp.wait_send()

    outs = [_sds(b.shape, b.dtype) for b in bufs]
    return _comm_call(body, "join_halves", bufs, outs, n, aliases={a: a for a in range(n)})


def _gather_small(v):
    rows, n = v.shape

    def body(v_ref, out_ref, send, recv, lsem):
        x, y, c, others = _place()
        me, sibling = (x, y, c), (x, y, 1 - c)

        def blk(px, py, pc):
            return out_ref.at[pl.ds((4 * px + 2 * py + pc) * rows, rows), :]

        def copy(k, block, to, src=None):
            return pltpu.make_async_remote_copy(src_ref=blk(*block) if src is None else src, dst_ref=blk(*block),
                                                send_sem=send.at[k], recv_sem=recv.at[k], device_id=to,
                                                device_id_type=MESH)

        mine = pltpu.make_async_copy(v_ref, blk(*me), lsem)
        mine.start()
        first = [copy(0, me, sibling, src=v_ref)]
        first += [copy(1 + j, me, (*chip, c), src=v_ref) for j, chip in enumerate(others)]
        for cp in first:
            cp.start()
        passed = [copy(4 + j, (*chip, c), sibling) for j, chip in enumerate(others)]
        for j, chip in enumerate(others):
            copy(1 + j, (*chip, c), me).wait_recv()
            passed[j].start()
        copy(0, sibling, me).wait_recv()
        for j, chip in enumerate(others):
            copy(4 + j, (*chip, 1 - c), me).wait_recv()
        for cp in first + passed:
            cp.wait_send()
        mine.wait()

    return pl.pallas_call(
        body, name="gather_small", out_shape=_sds((8 * rows, n), v.dtype),
        in_specs=[pl.BlockSpec(memory_space=pltpu.VMEM)], out_specs=pl.BlockSpec(memory_space=pltpu.VMEM),
        scratch_shapes=[pltpu.SemaphoreType.DMA((7,)), pltpu.SemaphoreType.DMA((7,)), pltpu.SemaphoreType.DMA],
        compiler_params=pltpu.CompilerParams(vmem_limit_bytes=VMEM_LIMIT))(v)


def _add_half(g, p, c):
    _, R, C = g.shape
    half = R // 2

    def body(c_ref, g_ref, p_ref, o_ref):
        o_ref[...] = g_ref[...] + p_ref[...]

    blk = (None, half, C)
    return _call(body, name="add_half", grid=(N_CHIPS,), prefetch=1,
                 in_specs=[pl.BlockSpec(blk, lambda s, c_ref: (s, c_ref[0], 0)),
                           pl.BlockSpec(blk, lambda s, c_ref: (s, 0, 0))],
                 out_specs=pl.BlockSpec(blk, lambda s, c_ref: (s, 0, 0)),
                 out_shape=_sds((N_CHIPS, half, C), F32), sem=("arbitrary",))(c, g, p)


def _sum_owner(s, q, buf, l, me, c):
    _, half, C = s.shape
    tr = half // 2

    def body(me_ref, c_ref, s_ref, q0, q1, q2, buf_ref, o_ref):
        o_ref[...] = ((s_ref[...] + q0[...]) + q1[...]) + q2[...]

    blk = (None, tr, C)
    qspec = lambda j: pl.BlockSpec(blk, lambda i, me_ref, c_ref: (j, i, 0))
    return _call(body, name=f"sum_owner_l{l}", grid=(half // tr,), prefetch=2,
                 in_specs=[pl.BlockSpec(blk, lambda i, me_ref, c_ref: (me_ref[0], i, 0)),
                           qspec(0), qspec(1), qspec(2), ANY],
                 out_specs=pl.BlockSpec(blk, lambda i, me_ref, c_ref: (l, 2 * c_ref[0] + i, 0)),
                 out_shape=_sds(buf.shape, F32), sem=("arbitrary",), aliases={6: 0})(me, c, s, q, q, q, buf)


def _adamw_math(w, g, m, v):
    m = ADAM_B1 * m + (1.0 - ADAM_B1) * g
    v = ADAM_B2 * v + (1.0 - ADAM_B2) * (g * g)
    m_hat = m / (1.0 - ADAM_B1 ** ADAM_STEP)
    v_hat = v / (1.0 - ADAM_B2 ** ADAM_STEP)
    delta = -ADAM_LR * (m_hat / (jnp.sqrt(v_hat) + ADAM_EPS) + ADAM_WD * w)
    return delta, m, v


def _adamw(w, g, m, v):
    rows, C = w.shape
    tr = next(t for t in (256, 128, 64) if rows % t == 0)

    def body(w_ref, g_ref, m_ref, v_ref, d_ref, nm_ref, nv_ref):
        d, nm, nv = _adamw_math(w_ref[...], g_ref[...], m_ref[...], v_ref[...])
        d_ref[...] = d
        nm_ref[...] = nm
        nv_ref[...] = nv

    spec = pl.BlockSpec((tr, C), lambda i: (i, 0))
    return _call(body, name="adamw", grid=(rows // tr,), in_specs=[spec] * 4, out_specs=[spec] * 3,
                 out_shape=[_sds((rows, C), F32)] * 3, sem=("parallel",))(w, g, m, v)


def _small_update(gathered, w, m, v):
    _, rows, n = gathered.shape
    tr = rows // 7

    def body(ga_ref, w_ref, m_ref, v_ref, g_ref, d_ref, nm_ref, nv_ref):
        g = ga_ref[0]
        for k in range(1, 8):
            g = g + ga_ref[k]
        d, nm, nv = _adamw_math(w_ref[...], g, m_ref[...], v_ref[...])
        g_ref[...] = g
        d_ref[...] = d
        nm_ref[...] = nm
        nv_ref[...] = nv

    spec = pl.BlockSpec((tr, n), lambda i: (i, 0))
    return _call(body, name="small_update", grid=(rows // tr,),
                 in_specs=[pl.BlockSpec((8, tr, n), lambda i: (0, i, 0)), spec, spec, spec], out_specs=[spec] * 4,
                 out_shape=[_sds((rows, n), F32)] * 4, sem=("parallel",))(gathered, w, m, v)


WEIGHTS = ("g_mix", "w_in", "g_q", "g_k", "w_attn_proj", "lambda_re", "lambda_im", "log_dt", "b_re", "b_im",
           "c_re", "c_im", "d_skip", "w_glu_a", "w_glu_b", "w_out", "g_ffn", "w_ffn_gate", "w_ffn_up", "w_ffn_down")
BIG = ("w_in", "w_attn_proj", "w_glu_a", "w_glu_b", "w_out", "w_ffn_gate", "w_ffn_up", "w_ffn_down")
SMALL = tuple(n for n in WEIGHTS if n not in BIG)
ROW_VECTORS = ("g_mix", "g_q", "g_k", "d_skip", "g_ffn")
PACK_QUANTUM = LANES * SUBLANES * 7


def _pack_small(parts, extra):
    flat = jnp.concatenate([parts[n].reshape(-1).astype(F32) for n in SMALL] + [extra.reshape(-1)])
    pad = -flat.shape[0] % PACK_QUANTUM
    return jnp.pad(flat, (0, pad)).reshape(-1, LANES)


def _unpack_small(packed, like):
    flat = packed.reshape(-1)
    out, at = {}, 0
    for n in SMALL:
        size = math.prod(like[n].shape)
        out[n] = flat[at:at + size].reshape(like[n].shape)
        at += size
    return out, flat[at]


def kernel(x, g_mix, w_in, g_q, g_k, w_attn_proj, lambda_re, lambda_im, log_dt, b_re, b_im, c_re, c_im, d_skip, w_glu_a, w_glu_b, w_out, g_ffn, w_ffn_gate, w_ffn_up, w_ffn_down, loss_target, m_g_mix, m_w_in, m_g_q, m_g_k, m_w_attn_proj, m_lambda_re, m_lambda_im, m_log_dt, m_b_re, m_b_im, m_c_re, m_c_im, m_d_skip, m_w_glu_a, m_w_glu_b, m_w_out, m_g_ffn, m_w_ffn_gate, m_w_ffn_up, m_w_ffn_down, v_g_mix, v_w_in, v_g_q, v_g_k, v_w_attn_proj, v_lambda_re, v_lambda_im, v_log_dt, v_b_re, v_b_im, v_c_re, v_c_im, v_d_skip, v_w_glu_a, v_w_glu_b, v_w_out, v_g_ffn, v_w_ffn_gate, v_w_ffn_up, v_w_ffn_down):
    given = dict(locals())
    W = {n: given[n] for n in WEIGHTS}
    M = {n: given["m_" + n] for n in WEIGHTS}
    V = {n: given["v_" + n] for n in WEIGHTS}
    depth = g_mix.shape[0]
    xl = x.reshape(x.shape[-2:])
    target = loss_target.reshape(loss_target.shape[-2:])
    c_idx = lax.axis_index("c").astype(jnp.int32).reshape(1)
    chip_idx = (2 * lax.axis_index("x") + lax.axis_index("y")).astype(jnp.int32).reshape(1)

    place = lambda l: [_cast_place(W[n], l, chip_idx) for n in BIG]
    full = _run_rider(_gather_d2d_rider(_run_rider(_gather_ici_rider(place(0)), "gather_ici")), "gather_d2d")
    params, saved, h = [], [], xl
    for l in range(depth):
        p = dict(zip(BIG, full))
        for n in SMALL:
            p[n] = W[n][l][None] if n in ROW_VECTORS else W[n][l]
        params.append(p)
        h, sv, full = _layer_fwd(h, p, place(l + 1) if l + 1 < depth else None)
        saved.append(sv)
    dx, loss_part = _loss_head(h, target)

    owned = {n: lax.empty(W[n].shape, F32) for n in BIG}
    small_grads = [None] * depth
    pending = None
    for l in reversed(range(depth)):
        dx, small_grads[l], pending, owned = _layer_bwd(dx, saved[l], params[l], pending, owned, l,
                                                        (chip_idx, c_idx))
    arrived = _run_rider(_scatter_rider([pending[n] for n in LATE]), "scatter_to_owners")
    for n, q in zip(LATE, arrived):
        owned[n] = _sum_owner(pending[n], q, owned[n], 0, chip_idx, c_idx)
    reduced = dict(zip(BIG, _join_halves([owned[n] for n in BIG])))

    grads, delta, new_m, new_v = {}, {}, {}, {}
    for n in BIG:
        shape = W[n].shape
        two_d = (shape[0] * shape[1], shape[2])
        d, nm, nv = _adamw(W[n].reshape(two_d), reduced[n].reshape(two_d), M[n].reshape(two_d), V[n].reshape(two_d))
        grads[n], delta[n], new_m[n], new_v[n] = reduced[n], d.reshape(shape), nm.reshape(shape), nv.reshape(shape)

    stacked = {n: jnp.stack([small_grads[l][n] for l in range(depth)]) for n in SMALL}
    zero = jnp.zeros((1,), F32)
    packed = _pack_small(stacked, loss_part)
    gathered = _gather_small(packed).reshape(8, *packed.shape)
    gs, ds, nms, nvs = _small_update(gathered, _pack_small(W, zero), _pack_small(M, zero), _pack_small(V, zero))
    sg, loss = _unpack_small(gs, W)
    sd, _ = _unpack_small(ds, W)
    sm, _ = _unpack_small(nms, W)
    sv_, _ = _unpack_small(nvs, W)
    for n in SMALL:
        grads[n], delta[n], new_m[n], new_v[n] = sg[n], sd[n], sm[n], sv_[n]

    return (loss, dx.reshape(x.shape), *[grads[n] for n in WEIGHTS], *[delta[n] for n in WEIGHTS],
            *[new_m[n] for n in WEIGHTS], *[new_v[n] for n in WEIGHTS])
```

```python
import collections
import functools
import math

import jax
import jax.numpy as jnp
from jax import lax
from jax.experimental import pallas as pl
from jax.experimental.pallas import tpu as pltpu

F32 = jnp.float32
BF16 = jnp.bfloat16

D_MODEL = 1024
DEPTH = 4
HEAD_DIM = 64
N_HEADS = 8
ATTN_WIDTH = N_HEADS * HEAD_DIM
ATTN_PATTERN = ((128, 1), (512, 4), (2048, 16))
N_GROUPS = len(ATTN_PATTERN)
BLK = 128
SSM_WIDTH = 512
SSM_GROUP = 16
SSM_GROUPS = 32
SSM_STATE = 64
D_FF = 2816
IN_COLS = 7168
EPS = 1e-6
ADAM_LR, ADAM_B1, ADAM_B2, ADAM_EPS, ADAM_WD, ADAM_STEP = 0.001, 0.9, 0.999, 1e-08, 0.01, 10

N_CHIPS = 4
MESH = pl.DeviceIdType.MESH

LANES = 128
SUBLANES = 8
VMEM_LIMIT = 56 * 1024 * 1024

TM = 512
TM_PROJ = 1024
TL_WGRAD = 2048
TM_MIX = 512

SSM_TB = 512
SSM_TC = 64
SSM_SUB = SUBLANES
SSM_PITCH = 72
N_SLAB = SSM_GROUPS * SSM_STATE // LANES
SSM_WIN = 256
N_PAIR = N_SLAB // 2
PAIRS_PER_WIN = 4
SCAN_GROUP = 4


def _params(sem=None, collective=False):
    return pltpu.CompilerParams(dimension_semantics=sem, vmem_limit_bytes=VMEM_LIMIT)


ANY = pl.BlockSpec(memory_space=pl.ANY)

Rider = collections.namedtuple("Rider", "ins out_shapes n_sem start wait aliases")


class _SemWindow:
    def __init__(self, ref, offset):
        self.ref, self.offset = ref, offset

    @property
    def at(self):
        return self

    def __getitem__(self, k):
        return self.ref.at[self.offset + k]


def _join_riders(*riders):
    riders = [r for r in riders if r is not None]
    if len(riders) <= 1:
        return riders[0] if riders else None

    def each(fn_name):
        def run(ins, outs, send, recv):
            i = o = s = 0
            for r in riders:
                getattr(r, fn_name)(ins[i:i + len(r.ins)], outs[o:o + len(r.out_shapes)],
                                    _SemWindow(send, s), _SemWindow(recv, s))
                i, o, s = i + len(r.ins), o + len(r.out_shapes), s + r.n_sem
        return run

    aliases, i, o = {}, 0, 0
    for r in riders:
        aliases.update({i + a: o + b for a, b in r.aliases.items()})
        i, o = i + len(r.ins), o + len(r.out_shapes)
    return Rider([t for r in riders for t in r.ins], [t for r in riders for t in r.out_shapes],
                 sum(r.n_sem for r in riders), each("start"), each("wait"), aliases)


def _with_rider(body, rider, grid, prefetch, n_in, n_out, n_scratch):
    n_rin, n_rout = len(rider.ins), len(rider.out_shapes)

    def hosted(*refs):
        pre, rest = refs[:prefetch], refs[prefetch:]
        ins, rin = rest[:n_in], rest[n_in:n_in + n_rin]
        o0 = n_in + n_rin
        outs, rout = rest[o0:o0 + n_out], rest[o0 + n_out:o0 + n_out + n_rout]
        s0 = o0 + n_out + n_rout
        scr, (send, recv) = rest[s0:s0 + n_scratch], rest[s0 + n_scratch:]
        first = functools.reduce(jnp.logical_and, [pl.program_id(k) == 0 for k in range(len(grid))])
        last = functools.reduce(jnp.logical_and, [pl.program_id(k) == grid[k] - 1 for k in range(len(grid))])

        @pl.when(first)
        def _():
            rider.start(rin, rout, send, recv)

        body(*pre, *ins, *outs, *scr)

        @pl.when(last)
        def _():
            rider.wait(rin, rout, send, recv)

    return hosted


def _call(body, *, name, grid, in_specs, out_specs, out_shape, scratch=(), sem=None, aliases=None,
          prefetch=0, rider=None):
    if rider is not None:
        single = not isinstance(out_specs, (list, tuple))
        out_specs = [out_specs] if single else list(out_specs)
        out_shape = [out_shape] if single else list(out_shape)
        body = _with_rider(body, rider, grid, prefetch, len(in_specs), len(out_specs), len(scratch))
        aliases = dict(aliases or {})
        aliases.update({prefetch + len(in_specs) + k: len(out_specs) + v for k, v in rider.aliases.items()})
        in_specs = list(in_specs) + [ANY] * len(rider.ins)
        out_specs = out_specs + [ANY] * len(rider.out_shapes)
        out_shape = out_shape + list(rider.out_shapes)
        scratch = list(scratch) + [pltpu.SemaphoreType.DMA((rider.n_sem,)), pltpu.SemaphoreType.DMA((rider.n_sem,))]
        sem = ("arbitrary",) * len(grid)
        fn = _call(body, name=name + "_host", grid=grid, in_specs=in_specs, out_specs=out_specs, out_shape=out_shape,
                   scratch=scratch, sem=sem, aliases=aliases, prefetch=prefetch)
        return lambda *args: fn(*args, *rider.ins)
    kw = {}
    if aliases:
        kw["input_output_aliases"] = aliases
    if prefetch:
        gs = pltpu.PrefetchScalarGridSpec(num_scalar_prefetch=prefetch, grid=grid, in_specs=in_specs,
                                          out_specs=out_specs, scratch_shapes=list(scratch))
        return pl.pallas_call(body, name=name, grid_spec=gs, out_shape=out_shape,
                              compiler_params=_params(sem), **kw)
    return pl.pallas_call(body, name=name, grid=grid, in_specs=in_specs, out_specs=out_specs,
                          out_shape=out_shape, scratch_shapes=list(scratch),
                          compiler_params=_params(sem), **kw)


def _sds(shape, dtype):
    return jax.ShapeDtypeStruct(shape, dtype)


def _sigmoid(v):
    return 1.0 / (1.0 + jnp.exp(-v))


def _dot(a, b):
    return jnp.dot(a, b, preferred_element_type=F32)


def _dot_nt(a, b):
    return lax.dot_general(a, b, (((1,), (1,)), ((), ())), preferred_element_type=F32)


def _dot_tn(a, b):
    return lax.dot_general(a, b, (((0,), (0,)), ((), ())), preferred_element_type=F32)


def _in_proj_fwd(x, g, w, rider=None):
    L = x.shape[0]
    ns = w.shape[2]
    tn = ns
    nj = ns // tn
    TM = TM_PROJ

    def body(x_ref, g_ref, w_ref, z_ref, h_ref):
        @pl.when(pl.program_id(1) == 0)
        def _():
            xv = x_ref[...]
            r = lax.rsqrt(jnp.mean(xv * xv, axis=-1, keepdims=True) + EPS)
            h_ref[...] = (xv * r * g_ref[...]).astype(BF16)
        z_ref[...] = _dot(h_ref[...], w_ref[...]).astype(BF16)

    return _call(
        body, name="in_proj_fwd", grid=(L // TM, N_CHIPS * nj),
        in_specs=[pl.BlockSpec((TM, D_MODEL), lambda i, j: (i, 0)),
                  pl.BlockSpec((1, D_MODEL), lambda i, j: (0, 0)),
                  pl.BlockSpec((None, D_MODEL, tn), lambda i, j: (j // nj, 0, j % nj))],
        out_specs=[pl.BlockSpec((TM, tn), lambda i, j: (i, j)),
                   pl.BlockSpec((TM, D_MODEL), lambda i, j: (i, 0))],
        out_shape=[_sds((L, N_CHIPS * ns), BF16), _sds((L, D_MODEL), BF16)],
        sem=("parallel", "arbitrary"), rider=rider)(x, g, w)


DL_TILE = 512
SCALE = HEAD_DIM ** -0.5


def _perm_matrix(d):
    rho = jnp.arange(DL_TILE)
    src = rho // (DL_TILE // d) + d * (rho % (DL_TILE // d))
    return (src[:, None] == jnp.arange(DL_TILE)[None, :]).astype(BF16)


def _head_sum_matrix():
    h = jnp.arange(ATTN_WIDTH) // HEAD_DIM
    return (h[:, None] == h[None, :]).astype(BF16)


def _split(v):
    hi = v.astype(BF16)
    return hi, (v - hi.astype(F32)).astype(BF16)


def _head_sum(v, hs):
    vb = v.astype(BF16)
    half = ATTN_WIDTH // 2
    blk = hs[:half, :half]
    return jnp.concatenate([_dot(vb[:, :half], blk), _dot(vb[:, half:], blk)], axis=1)


def _permute(pm, v):
    hi, lo = _split(v)
    return _dot(pm, hi) + _dot(pm, lo)


def _dl_view(t, d):
    if d * BLK <= DL_TILE:
        return t
    return t.reshape(t.shape[0] // DL_TILE, d, DL_TILE // d, t.shape[1])


def _dl_spec(d, width, which):
    if d * BLK <= DL_TILE:
        per_tile = DL_TILE // (d * BLK)
        return pl.BlockSpec((BLK, width), lambda r, n: ((which(n) // per_tile) * (DL_TILE // BLK)
                                                       + r * per_tile + which(n) % per_tile, 0))
    tiles = d * BLK // DL_TILE
    return pl.BlockSpec((tiles, None, DL_TILE // d, width), lambda r, n: (which(n), r, 0, 0))


def _dl_read(ref):
    v = ref[...]
    return v if v.ndim == 2 else v.reshape(BLK, v.shape[-1])


def _dl_write(ref, v):
    ref[...] = v if len(ref.shape) == 2 else v.reshape(ref.shape)


def _qkv_prep(z, gq_t, gk_t, rider=None):
    L = z.shape[0]
    qkv_w = N_GROUPS * ATTN_WIDTH

    def body(zq_ref, zk_ref, zv_ref, gq_ref, gk_ref, hs_ref, p1_ref, p2_ref, *outs):
        hs = hs_ref[...]
        perms = (None, p1_ref[...], p2_ref[...])
        for g in range(N_GROUPS):
            cols = slice(g * ATTN_WIDTH, (g + 1) * ATTN_WIDTH)
            xq = zq_ref[:, cols].astype(F32)
            xk = zk_ref[:, cols].astype(F32)
            rq = lax.rsqrt(_head_sum(xq * xq, hs) * (1.0 / HEAD_DIM) + EPS)
            rk = lax.rsqrt(_head_sum(xk * xk, hs) * (1.0 / HEAD_DIM) + EPS)
            vals = [(xq * rq * (gq_ref[...] * SCALE)).astype(BF16), (xk * rk * gk_ref[...]).astype(BF16),
                    zv_ref[:, cols]]
            for j, t in enumerate(vals):
                if perms[g] is not None:
                    t = _dot(perms[g], t).astype(BF16)
                outs[3 * g + j][...] = t

    tile = pl.BlockSpec((DL_TILE, ATTN_WIDTH), lambda i: (i, 0))
    mat = pl.BlockSpec((DL_TILE, DL_TILE), lambda i: (0, 0))
    vec = pl.BlockSpec((1, ATTN_WIDTH), lambda i: (0, 0))
    outs = _call(
        body, name="qkv_prep", grid=(L // DL_TILE,),
        in_specs=[pl.BlockSpec((DL_TILE, qkv_w), lambda i: (i, 0)), pl.BlockSpec((DL_TILE, qkv_w), lambda i: (i, 1)),
                  pl.BlockSpec((DL_TILE, qkv_w), lambda i: (i, 2)), vec, vec, mat, mat, mat],
        out_specs=[tile] * 9, out_shape=[_sds((L, ATTN_WIDTH), BF16)] * 9,
        sem=("parallel",), rider=rider)(z, z, z, gq_t, gk_t, _head_sum_matrix(), _perm_matrix(ATTN_PATTERN[1][1]),
                                        _perm_matrix(ATTN_PATTERN[2][1]))
    return [tuple(outs[3 * g:3 * g + 3]) for g in range(N_GROUPS)], list(outs[3 * N_GROUPS:])


def _pair_masks():
    lane = lax.broadcasted_iota(jnp.int32, (1, LANES), 1)
    return lane < HEAD_DIM, lane >= HEAD_DIM


def _attn_fwd(qs, ks, v, gi):
    L = qs.shape[0]
    _, d = ATTN_PATTERN[gi]
    nb = L // (d * BLK)

    def body(q_ref, kc_ref, kp_ref, vc_ref, vp_ref, o_ref, l_ref):
        n = pl.program_id(1)
        qi = lax.broadcasted_iota(jnp.int32, (BLK, 2 * BLK), 0)
        kj = lax.broadcasted_iota(jnp.int32, (BLK, 2 * BLK), 1)
        prev = kj < BLK
        mask = jnp.logical_and(jnp.where(prev, kj, qi) >= jnp.where(prev, qi, kj - BLK),
                               kj >= jnp.where(n > 0, 0, BLK))
        q = _dl_read(q_ref)
        kw = jnp.concatenate([_dl_read(kp_ref), _dl_read(kc_ref)], axis=0)
        vw = jnp.concatenate([_dl_read(vp_ref), _dl_read(vc_ref)], axis=0)
        one = jnp.ones((2 * BLK, LANES), BF16)
        o_parts, l_parts = [], []
        for hp in range(N_HEADS // 2):
            ls = slice(hp * LANES, (hp + 1) * LANES)
            qp, kp_, vp_ = q[:, ls], kw[:, ls], vw[:, ls]
            num = jnp.zeros((BLK, LANES), F32)
            den = jnp.zeros((BLK, LANES), F32)
            mb = jnp.zeros((BLK, LANES), F32)
            for he in _pair_masks():
                s = jnp.where(mask, _dot_nt(jnp.where(he, qp, 0), kp_), -jnp.inf)
                m = jnp.max(s, axis=-1, keepdims=True)
                p = jnp.exp(s - m).astype(BF16)
                acc = _dot(p, jnp.concatenate([jnp.where(he, vp_, 0), jnp.where(he, one, 0)], axis=1))
                num += acc[:, :LANES]
                den += acc[:, LANES:]
                mb = jnp.where(he, m, mb)
            o_parts.append((num / den).astype(BF16))
            l_parts.append(mb + jnp.log(den))
        _dl_write(o_ref, jnp.concatenate(o_parts, axis=1))
        _dl_write(l_ref, jnp.concatenate(l_parts, axis=1))

    cur = _dl_spec(d, ATTN_WIDTH, lambda n: n)
    prev = _dl_spec(d, ATTN_WIDTH, lambda n: jnp.maximum(n - 1, 0))
    view = lambda t: _dl_view(t, d)
    o, l = _call(
        body, name=f"attn_fwd_g{gi}", grid=(d, nb), in_specs=[cur, cur, prev, cur, prev], out_specs=[cur, cur],
        out_shape=[_sds(view(qs).shape, BF16), _sds(view(qs).shape, F32)],
        sem=("parallel", "parallel"))(view(qs), view(ks), view(ks), view(v), view(v))
    return o.reshape(L, ATTN_WIDTH), l.reshape(L, ATTN_WIDTH)


def _to_token_order(os_, ls_, pts):
    o_tok, l_tok = [], []
    for o, l, pt in zip(os_, ls_, pts):
        if pt is None:
            o_tok.append(o.astype(F32))
            l_tok.append(l)
        else:
            o_tok.append(_dot(pt, o))
            l_tok.append(_permute(pt, l))
    return o_tok, l_tok


def _combine_fwd(os_, ls_):
    L = os_[0].shape[0]

    def body(o0, o1, o2, l0, l1, l2, pt1_ref, pt2_ref, a_ref):
        o_tok, l_tok = _to_token_order((o0[...], o1[...], o2[...]), (l0[...], l1[...], l2[...]),
                                       (None, pt1_ref[...], pt2_ref[...]))
        w = _combine_weights(*l_tok)
        a_ref[...] = (w[0] * o_tok[0] + w[1] * o_tok[1] + w[2] * o_tok[2]).astype(BF16)

    tile = pl.BlockSpec((DL_TILE, ATTN_WIDTH), lambda i: (i, 0))
    mat = pl.BlockSpec((DL_TILE, DL_TILE), lambda i: (0, 0))
    return _call(body, name="combine_fwd", grid=(L // DL_TILE,), in_specs=[tile] * 6 + [mat, mat], out_specs=tile,
                 out_shape=_sds((L, ATTN_WIDTH), BF16), sem=("parallel",))(
                     *os_, *ls_, _perm_matrix(ATTN_PATTERN[1][1]).T, _perm_matrix(ATTN_PATTERN[2][1]).T)


def _gelu(v):
    c = math.sqrt(2.0 / math.pi)
    return 0.5 * v * (1.0 + jnp.tanh(c * (v + 0.044715 * v * v * v)))


def _gelu_grad(v):
    c = math.sqrt(2.0 / math.pi)
    t = jnp.tanh(c * (v + 0.044715 * v * v * v))
    return 0.5 * (1.0 + t) + 0.5 * v * (1.0 - t * t) * c * (1.0 + 3.0 * 0.044715 * v * v)


def _ssm_fill(u, bwre_ref, bwim_ref, sre, sim):
    for k2 in range(N_PAIR):
        uw = u[:, _win_cols(k2)]
        _to_slabs(sre, k2, _dot(uw, bwre_ref[k2]))
        _to_slabs(sim, k2, _dot(uw, bwim_ref[k2]))


def _win_cols(k2):
    w = k2 // PAIRS_PER_WIN
    return slice(w * SSM_WIN, (w + 1) * SSM_WIN)


def _to_slabs(ref, k2, v):
    for half in range(2):
        for j in range(SSM_SUB):
            ref[2 * k2 + half, j * SSM_PITCH:j * SSM_PITCH + SSM_TC, :] = (
                v[j * SSM_TC:(j + 1) * SSM_TC, half * LANES:(half + 1) * LANES])


def _rows(i):
    return pl.ds(i, SSM_SUB, stride=SSM_PITCH)


def _slab_rows(ref, k):
    return jnp.concatenate([ref[k, j * SSM_PITCH:j * SSM_PITCH + SSM_TC, :] for j in range(SSM_SUB)], axis=0)


def _pair_rows(ref, k2):
    return jnp.concatenate([_slab_rows(ref, 2 * k2), _slab_rows(ref, 2 * k2 + 1)], axis=1).astype(BF16)


def _bcast(ref, k):
    return jnp.broadcast_to(ref[pl.ds(k, 1), :], (SSM_SUB, LANES))


def _scan(sre, sim, are_ref, aim_ref, k0, init, *, reverse, store, sign=1.0):
    ar = [_bcast(are_ref, k0 + kk) for kk in range(SCAN_GROUP)]
    ai = [sign * _bcast(aim_ref, k0 + kk) for kk in range(SCAN_GROUP)]

    def step(t, carry):
        i = SSM_TC - 1 - t if reverse else t
        out = []
        for kk in range(SCAN_GROUP):
            k = k0 + kk
            xr, xi = carry[2 * kk], carry[2 * kk + 1]
            nr = ar[kk] * xr - ai[kk] * xi + sre[k, _rows(i), :]
            ni = ar[kk] * xi + ai[kk] * xr + sim[k, _rows(i), :]
            if store:
                sre[k, _rows(i), :] = nr
                sim[k, _rows(i), :] = ni
            out += [nr, ni]
        return tuple(out)

    flat = []
    for re, im in init:
        flat += [re, im]
    res = lax.fori_loop(0, SSM_TC, step, tuple(flat), unroll=2)
    return [(res[2 * kk], res[2 * kk + 1]) for kk in range(SCAN_GROUP)]


def _ssm_seeds(ends_re, ends_im, a64re_ref, a64im_ref, carry_re, carry_im, seed_re, seed_im, k,
               *, reverse, sign=1.0):
    ar = a64re_ref[pl.ds(k, 1), :]
    ai = sign * a64im_ref[pl.ds(k, 1), :]
    cr = carry_re[pl.ds(k, 1), :]
    ci = carry_im[pl.ds(k, 1), :]
    order = range(SSM_SUB - 1, -1, -1) if reverse else range(SSM_SUB)
    for j in order:
        seed_re[k, pl.ds(j, 1), :] = cr
        seed_im[k, pl.ds(j, 1), :] = ci
        er = ends_re[k, pl.ds(j, 1), :]
        ei = ends_im[k, pl.ds(j, 1), :]
        cr, ci = ar * cr - ai * ci + er, ar * ci + ai * cr + ei
    carry_re[pl.ds(k, 1), :] = cr
    carry_im[pl.ds(k, 1), :] = ci


def _ssm_specs_consts():
    c2 = pl.BlockSpec((N_SLAB, LANES), lambda b: (0, 0))
    c3 = pl.BlockSpec((N_PAIR, SSM_WIN, SSM_WIN), lambda b: (0, 0, 0))
    return c2, c3


def _ssm_scratch():
    rows = SSM_SUB * SSM_PITCH
    return [pltpu.VMEM((N_SLAB, rows, LANES), F32), pltpu.VMEM((N_SLAB, rows, LANES), F32)]


def _ssm_fwd(z, pk, dskip, rider=None):
    L = z.shape[0]
    nb = L // SSM_TB
    ucol = (3 * N_GROUPS * ATTN_WIDTH) // SSM_WIDTH

    def body(u_ref, are_ref, aim_ref, a64re_ref, a64im_ref, bwre_ref, bwim_ref, cwre_ref, cwim_ref, d_ref,
             ypre_ref, yact_ref, sdre_ref, sdim_ref, sre, sim, carry_re, carry_im, ends_re, ends_im,
             seed_re, seed_im):
        @pl.when(pl.program_id(0) == 0)
        def _():
            carry_re[...] = jnp.zeros_like(carry_re)
            carry_im[...] = jnp.zeros_like(carry_im)

        u = u_ref[...]
        _ssm_fill(u, bwre_ref, bwim_ref, sre, sim)
        zero = jnp.zeros((SSM_SUB, LANES), F32)
        for k0 in range(0, N_SLAB, SCAN_GROUP):
            ends = _scan(sre, sim, are_ref, aim_ref, k0, [(zero, zero)] * SCAN_GROUP, reverse=False, store=False)
            for kk in range(SCAN_GROUP):
                ends_re[k0 + kk] = ends[kk][0]
                ends_im[k0 + kk] = ends[kk][1]
            for kk in range(SCAN_GROUP):
                _ssm_seeds(ends_re, ends_im, a64re_ref, a64im_ref, carry_re, carry_im, seed_re, seed_im,
                           k0 + kk, reverse=False)
            init = [(seed_re[k0 + kk], seed_im[k0 + kk]) for kk in range(SCAN_GROUP)]
            _scan(sre, sim, are_ref, aim_ref, k0, init, reverse=False, store=True)
        sdre_ref[...] = seed_re[...]
        sdim_ref[...] = seed_im[...]
        for w in range(N_PAIR // PAIRS_PER_WIN):
            acc = jnp.zeros((SSM_TB, SSM_WIN), F32)
            for kk in range(PAIRS_PER_WIN):
                k2 = w * PAIRS_PER_WIN + kk
                acc += _dot(_pair_rows(sre, k2), cwre_ref[k2])
                acc -= _dot(_pair_rows(sim, k2), cwim_ref[k2])
            cols = _win_cols(w * PAIRS_PER_WIN)
            ypre = acc + d_ref[:, cols] * u[:, cols].astype(F32)
            ypre_ref[:, cols] = ypre
            yact_ref[:, cols] = _gelu(ypre).astype(BF16)

    c2, c3 = _ssm_specs_consts()
    seed_spec = pl.BlockSpec((None, N_SLAB, SSM_SUB, LANES), lambda b: (b, 0, 0, 0))
    small = pltpu.VMEM((N_SLAB, LANES), F32)
    tile = pltpu.VMEM((N_SLAB, SSM_SUB, LANES), F32)
    return _call(
        body, name="ssm_fwd", grid=(nb,),
        in_specs=[pl.BlockSpec((SSM_TB, SSM_WIDTH), lambda b: (b, ucol)), c2, c2, c2, c2, c3, c3, c3, c3,
                  pl.BlockSpec((1, SSM_WIDTH), lambda b: (0, 0))],
        out_specs=[pl.BlockSpec((SSM_TB, SSM_WIDTH), lambda b: (b, 0)),
                   pl.BlockSpec((SSM_TB, SSM_WIDTH), lambda b: (b, 0)), seed_spec, seed_spec],
        out_shape=[_sds((L, SSM_WIDTH), F32), _sds((L, SSM_WIDTH), BF16),
                   _sds((nb, N_SLAB, SSM_SUB, LANES), F32), _sds((nb, N_SLAB, SSM_SUB, LANES), F32)],
        scratch=_ssm_scratch() + [small, small, tile, tile, tile, tile],
        sem=("arbitrary",), rider=rider)(z, pk["a_re"], pk["a_im"], pk["a64_re"], pk["a64_im"],
                                         pk["bw_re"].astype(BF16), pk["bw_im"].astype(BF16),
                                         pk["cw_re"].astype(BF16), pk["cw_im"].astype(BF16), dskip)


def _combine_weights(l0, l1, l2):
    m = jnp.maximum(jnp.maximum(l0, l1), l2)
    e0, e1, e2 = jnp.exp(l0 - m), jnp.exp(l1 - m), jnp.exp(l2 - m)
    inv = 1.0 / (e0 + e1 + e2)
    return e0 * inv, e1 * inv, e2 * inv


def _mix_fwd(x, z, a, yact, w_ap, w_ga, w_gb, w_out):
    L = x.shape[0]
    cs = D_MODEL // N_CHIPS
    ga_col = (3 * N_GROUPS * ATTN_WIDTH + SSM_WIDTH) // D_MODEL

    def body(x_ref, ga_ref, gs_ref, a_ref, y_ref, wap_ref, wga_ref, wgb_ref, wout_ref,
             x1_ref, aout_ref, sa_ref, sb_ref, mix_ref):
        a = a_ref[...]
        y = y_ref[...]
        for s in range(N_CHIPS):
            cols = slice(s * cs, (s + 1) * cs)
            aout_ref[:, cols] = _dot(a, wap_ref[s]).astype(BF16)
            sa_ref[:, cols] = _dot(y, wga_ref[s]).astype(BF16)
            sb_ref[:, cols] = _dot(y, wgb_ref[s]).astype(BF16)
        s_out = sa_ref[...].astype(F32) * _sigmoid(sb_ref[...].astype(F32))
        mix = (_sigmoid(ga_ref[...].astype(F32)) * aout_ref[...].astype(F32)
               + _sigmoid(gs_ref[...].astype(F32)) * s_out).astype(BF16)
        mix_ref[...] = mix
        x1_ref[...] = x_ref[...] + _dot(mix, wout_ref[...])

    tok = lambda w: pl.BlockSpec((TM_MIX, w), lambda i: (i, 0))
    wsm = pl.BlockSpec((N_CHIPS, ATTN_WIDTH, cs), lambda i: (0, 0, 0))
    return _call(
        body, name="mix_fwd", grid=(L // TM_MIX,),
        in_specs=[tok(D_MODEL), pl.BlockSpec((TM_MIX, D_MODEL), lambda i: (i, ga_col)),
                  pl.BlockSpec((TM_MIX, D_MODEL), lambda i: (i, ga_col + 1))]
                 + [tok(ATTN_WIDTH)] * 2 + [wsm, wsm, wsm, pl.BlockSpec((D_MODEL, D_MODEL), lambda i: (0, 0))],
        out_specs=[tok(D_MODEL), tok(D_MODEL), tok(D_MODEL), tok(D_MODEL), tok(D_MODEL)],
        out_shape=[_sds((L, D_MODEL), F32)] + [_sds((L, D_MODEL), BF16)] * 4,
        sem=("parallel",))(x, z, z, a, yact, w_ap, w_ga, w_gb, w_out.reshape(D_MODEL, D_MODEL))


def _ffn_fwd(x1, g, w_g, w_u, w_d, rider=None):
    L = x1.shape[0]
    fs = D_FF // N_CHIPS
    TM = TM_PROJ

    def body(x_ref, g_ref, wg_ref, wu_ref, wd_ref, x2_ref, h_ref, gate_ref, up_ref, act_ref, acc):
        s = pl.program_id(1)

        @pl.when(s == 0)
        def _():
            xv = x_ref[...]
            r = lax.rsqrt(jnp.mean(xv * xv, axis=-1, keepdims=True) + EPS)
            h_ref[...] = (xv * r * g_ref[...]).astype(BF16)
            acc[...] = jnp.zeros_like(acc)

        h = h_ref[...]
        gate = _dot(h, wg_ref[...])
        up = _dot(h, wu_ref[...])
        act = (gate * _sigmoid(gate) * up).astype(BF16)
        gate_ref[...] = gate.astype(BF16)
        up_ref[...] = up.astype(BF16)
        act_ref[...] = act
        acc[...] += _dot(act, wd_ref[...])

        @pl.when(s == N_CHIPS - 1)
        def _():
            x2_ref[...] = x_ref[...] + acc[...]

    tok = pl.BlockSpec((TM, D_MODEL), lambda i, s: (i, 0))
    ffs = pl.BlockSpec((None, TM, fs), lambda i, s: (s, i, 0))
    return _call(
        body, name="ffn_fwd", grid=(L // TM, N_CHIPS),
        in_specs=[tok, pl.BlockSpec((1, D_MODEL), lambda i, s: (0, 0)),
                  pl.BlockSpec((None, D_MODEL, fs), lambda i, s: (s, 0, 0)),
                  pl.BlockSpec((None, D_MODEL, fs), lambda i, s: (s, 0, 0)),
                  pl.BlockSpec((None, fs, D_MODEL), lambda i, s: (s, 0, 0))],
        out_specs=[tok, tok, ffs, ffs, ffs],
        out_shape=[_sds((L, D_MODEL), F32), _sds((L, D_MODEL), BF16)] + [_sds((N_CHIPS, L, fs), BF16)] * 3,
        scratch=[pltpu.VMEM((TM, D_MODEL), F32)],
        sem=("parallel", "arbitrary"), rider=rider)(x1, g, w_g, w_u, w_d)


def _loss_head(xl, target):
    L = xl.shape[0]

    def body(x_ref, t_ref, dx_ref, loss_ref, acc):
        i = pl.program_id(0)

        @pl.when(i == 0)
        def _():
            acc[...] = jnp.zeros_like(acc)

        e = x_ref[...] - t_ref[...]
        dx_ref[...] = e * (1.0 / D_MODEL)
        acc[...] += jnp.sum((e * e).reshape(TM // SUBLANES, SUBLANES, D_MODEL), axis=0)

        @pl.when(i == pl.num_programs(0) - 1)
        def _():
            loss_ref[...] = (0.5 / D_MODEL) * jnp.sum(acc[...]).reshape(1, 1)

    tok = pl.BlockSpec((TM, D_MODEL), lambda i: (i, 0))
    return _call(
        body, name="loss_head", grid=(L // TM,), in_specs=[tok, tok],
        out_specs=[tok, pl.BlockSpec((1, 1), lambda i: (0, 0))],
        out_shape=[_sds((L, D_MODEL), F32), _sds((1, 1), F32)],
        scratch=[pltpu.VMEM((SUBLANES, D_MODEL), F32)], sem=("arbitrary",))(xl, target)


def _ssm_pack(lam_re, lam_im, log_dt, b_re, b_im, c_re, c_im):
    dt = jnp.exp(log_dt)[:, None]
    mag = jnp.exp(lam_re * dt)
    ang = lam_im * dt
    ar = mag * jnp.cos(ang)
    ai = mag * jnp.sin(ang)
    nr = ar - 1.0
    ni = ai
    den = lam_re * lam_re + lam_im * lam_im
    cr = ((nr * lam_re + ni * lam_im) / den)[..., None]
    ci = ((ni * lam_re - nr * lam_im) / den)[..., None]
    bbr = cr * b_re - ci * b_im
    bbi = cr * b_im + ci * b_re
    gpp = SSM_WIN // SSM_STATE
    gpw = SSM_WIN // SSM_GROUP
    k2 = jnp.arange(N_PAIR)[:, None, None]
    gs = jnp.arange(gpp)[None, :, None]
    gl = jnp.arange(gpw)[None, None, :]
    same = (gl == gpp * (k2 % PAIRS_PER_WIN) + gs).astype(F32)

    def b_windows(bb):
        return jnp.einsum('kgl,kgpc->klcgp', same, bb.reshape(N_PAIR, gpp, SSM_STATE, SSM_GROUP)).reshape(
            N_PAIR, SSM_WIN, SSM_WIN)

    def c_windows(cc):
        return jnp.einsum('kgl,kgcp->kgplc', same, cc.reshape(N_PAIR, gpp, SSM_GROUP, SSM_STATE)).reshape(
            N_PAIR, SSM_WIN, SSM_WIN)

    pr, pi = ar, ai
    for _ in range(int(math.log2(SSM_TC))):
        pr, pi = pr * pr - pi * pi, 2.0 * pr * pi
    return dict(a_re=ar.reshape(N_SLAB, LANES), a_im=ai.reshape(N_SLAB, LANES),
                a64_re=pr.reshape(N_SLAB, LANES), a64_im=pi.reshape(N_SLAB, LANES),
                bw_re=b_windows(bbr), bw_im=b_windows(bbi), cw_re=c_windows(c_re), cw_im=c_windows(c_im))


def _layer_fwd(x, p, rest, rest_stage, next_bufs=None):
    first = {"ici": _gather_ici_rider, "d2d": _gather_d2d_rider}[rest_stage]
    outs = _in_proj_fwd(x, p["g_mix"], p["w_in"], first(rest))
    (z, h), rest = outs[:2], list(outs[2:])
    qkv, got = _qkv_prep(z, jnp.tile(p["g_q"], (1, N_HEADS)), jnp.tile(p["g_k"], (1, N_HEADS)),
                         _gather_d2d_rider(rest) if rest_stage == "ici" else None)
    p = {**p, **dict(zip(BIG[1:], got if rest_stage == "ici" else rest))}
    os_, ls_ = [], []
    for gi in range(N_GROUPS):
        o, l = _attn_fwd(*qkv[gi], gi)
        os_.append(o)
        ls_.append(l)
    a = _combine_fwd(os_, ls_)
    pk = _ssm_pack(p["lambda_re"], p["lambda_im"], p["log_dt"], p["b_re"], p["b_im"], p["c_re"], p["c_im"])
    outs = _ssm_fwd(z, pk, p["d_skip"], _gather_ici_rider(next_bufs[:1]) if next_bufs else None)
    (ypre, yact, sd_re, sd_im), next_in = outs[:4], list(outs[4:])
    x1, aout, sa, sb, mix = _mix_fwd(x, z, a, yact, p["w_attn_proj"], p["w_glu_a"], p["w_glu_b"], p["w_out"])
    outs = _ffn_fwd(x1, p["g_ffn"], p["w_ffn_gate"], p["w_ffn_up"], p["w_ffn_down"],
                    _join_riders(_gather_ici_rider(next_bufs[1:]), _gather_d2d_rider(next_in)) if next_bufs else None)
    x2, h2, gate, up, act = outs[:5]
    nxt = (outs[-1], list(outs[5:-1])) if next_bufs else None
    saved = dict(x=x, z=z, h=h, qkv=qkv, os=os_, ls=ls_, pk=pk, ypre=ypre, yact=yact, sd_re=sd_re, sd_im=sd_im,
                 x1=x1, a=a, aout=aout, sa=sa, sb=sb, mix=mix, h2=h2, gate=gate, up=up, act=act)
    return x2, saved, p, nxt


def _rms_bwd(xv, g, dh):
    r = lax.rsqrt(jnp.mean(xv * xv, axis=-1, keepdims=True) + EPS)
    xn = xv * r
    dxn = dh * g
    dx = r * (dxn - xn * jnp.mean(dxn * xn, axis=-1, keepdims=True))
    dg = jnp.sum((dh * xn).reshape(xv.shape[0] // SUBLANES, SUBLANES, xv.shape[1]), axis=0)
    return dx, dg


def _ffn_bwd_act(dx2, gate, up, w_d):
    L = dx2.shape[0]
    fs = D_FF // N_CHIPS
    TM = TM_PROJ

    def body(dx_ref, gate_ref, up_ref, wd_ref, dgate_ref, dup_ref):
        dact = _dot_nt(dx_ref[...].astype(BF16), wd_ref[...])
        gt = gate_ref[...].astype(F32)
        sg = _sigmoid(gt)
        dgate_ref[...] = (dact * up_ref[...].astype(F32) * (sg * (1.0 + gt * (1.0 - sg)))).astype(BF16)
        dup_ref[...] = (dact * gt * sg).astype(BF16)

    ffs = pl.BlockSpec((None, TM, fs), lambda i, s: (s, i, 0))
    return _call(
        body, name="ffn_bwd_act", grid=(L // TM, N_CHIPS),
        in_specs=[pl.BlockSpec((TM, D_MODEL), lambda i, s: (i, 0)), ffs, ffs,
                  pl.BlockSpec((None, fs, D_MODEL), lambda i, s: (s, 0, 0))],
        out_specs=[ffs, ffs], out_shape=[_sds((N_CHIPS, L, fs), BF16)] * 2,
        sem=("parallel", "parallel"))(dx2, gate, up, w_d)


def _ffn_bwd_in(dx2, x1, g, dgate, dup, w_g, w_u, rider=None):
    L = x1.shape[0]
    fs = D_FF // N_CHIPS
    TM = TM_PROJ

    def body(dx_ref, x_ref, g_ref, dgate_ref, dup_ref, wg_ref, wu_ref, dx1_ref, dg_ref, acc, dgacc):
        i, s = pl.program_id(0), pl.program_id(1)

        @pl.when(s == 0)
        def _():
            acc[...] = jnp.zeros_like(acc)

        @pl.when(jnp.logical_and(i == 0, s == 0))
        def _():
            dgacc[...] = jnp.zeros_like(dgacc)

        acc[...] += _dot_nt(dgate_ref[...], wg_ref[...]) + _dot_nt(dup_ref[...], wu_ref[...])

        @pl.when(s == N_CHIPS - 1)
        def _():
            dx, dg = _rms_bwd(x_ref[...], g_ref[...], acc[...])
            dx1_ref[...] = dx_ref[...] + dx
            dgacc[...] += dg

        @pl.when(jnp.logical_and(i == pl.num_programs(0) - 1, s == N_CHIPS - 1))
        def _():
            dg_ref[...] = jnp.sum(dgacc[...], axis=0, keepdims=True)

    tok = pl.BlockSpec((TM, D_MODEL), lambda i, s: (i, 0))
    ffs = pl.BlockSpec((None, TM, fs), lambda i, s: (s, i, 0))
    vec = pl.BlockSpec((1, D_MODEL), lambda i, s: (0, 0))
    return _call(
        body, name="ffn_bwd_in", grid=(L // TM, N_CHIPS),
        in_specs=[tok, tok, vec, ffs, ffs,
                  pl.BlockSpec((None, D_MODEL, fs), lambda i, s: (s, 0, 0)),
                  pl.BlockSpec((None, D_MODEL, fs), lambda i, s: (s, 0, 0))],
        out_specs=[tok, vec],
        out_shape=[_sds((L, D_MODEL), F32), _sds((1, D_MODEL), F32)],
        scratch=[pltpu.VMEM((TM, D_MODEL), F32), pltpu.VMEM((SUBLANES, D_MODEL), F32)],
        sem=("arbitrary", "arbitrary"), rider=rider)(dx2, x1, g, dgate, dup, w_g, w_u)


def _wgrad(a, b, *, name, grid_kn, a_spec, b_spec, out_shape, out_spec):
    L = a.shape[-2]
    nl = L // TL_WGRAD

    def body(a_ref, b_ref, o_ref):
        @pl.when(pl.program_id(2) == 0)
        def _():
            o_ref[...] = jnp.zeros_like(o_ref)
        o_ref[...] += _dot_tn(a_ref[...].astype(BF16), b_ref[...].astype(BF16))

    return _call(body, name=name, grid=(*grid_kn, nl), in_specs=[a_spec, b_spec], out_specs=out_spec,
                 out_shape=out_shape, sem=("parallel", "parallel", "arbitrary"))(a, b)


def _wgrad_cols(a, b, name):
    K, N = a.shape[1], b.shape[1]
    ns = N // N_CHIPS
    if N * K * 4 <= 4 * 1024 * 1024:
        L = a.shape[0]

        def body(a_ref, b_ref, o_ref):
            @pl.when(pl.program_id(0) == 0)
            def _():
                o_ref[...] = jnp.zeros_like(o_ref)
            av = a_ref[...].astype(BF16)
            for s in range(N_CHIPS):
                o_ref[s] += _dot_tn(av, b_ref[:, s * ns:(s + 1) * ns].astype(BF16))

        return _call(body, name=name, grid=(L // TL_WGRAD,),
                     in_specs=[pl.BlockSpec((TL_WGRAD, K), lambda t: (t, 0)),
                               pl.BlockSpec((TL_WGRAD, N), lambda t: (t, 0))],
                     out_specs=pl.BlockSpec((N_CHIPS, K, ns), lambda t: (0, 0, 0)),
                     out_shape=_sds((N_CHIPS, K, ns), F32), sem=("arbitrary",))(a, b)
    tn = ns // 2 if ns % (2 * LANES) == 0 else ns
    nj = ns // tn
    return _wgrad(a, b, name=name, grid_kn=(1, N_CHIPS * nj),
                  a_spec=pl.BlockSpec((TL_WGRAD, K), lambda i, j, t: (t, 0)),
                  b_spec=pl.BlockSpec((TL_WGRAD, tn), lambda i, j, t: (t, j)),
                  out_shape=_sds((N_CHIPS, K, ns), F32),
                  out_spec=pl.BlockSpec((None, K, tn), lambda i, j, t: (j // nj, 0, j % nj)))


def _wgrad_full(a, b, name):
    K, N = a.shape[1], b.shape[1]
    return _wgrad(a, b, name=name, grid_kn=(1, 1),
                  a_spec=pl.BlockSpec((TL_WGRAD, K), lambda i, j, t: (t, 0)),
                  b_spec=pl.BlockSpec((TL_WGRAD, N), lambda i, j, t: (t, 0)),
                  out_shape=_sds((K, N), F32), out_spec=pl.BlockSpec((K, N), lambda i, j, t: (0, 0)))


def _wgrad_ff_cols(a, b, name):
    K, fs = a.shape[1], b.shape[2]
    return _wgrad(a, b, name=name, grid_kn=(1, N_CHIPS),
                  a_spec=pl.BlockSpec((TL_WGRAD, K), lambda i, j, t: (t, 0)),
                  b_spec=pl.BlockSpec((None, TL_WGRAD, fs), lambda i, j, t: (j, t, 0)),
                  out_shape=_sds((N_CHIPS, K, fs), F32),
                  out_spec=pl.BlockSpec((None, K, fs), lambda i, j, t: (j, 0, 0)))


def _wgrad_ff_rows(a, b, name):
    fs, N = a.shape[2], b.shape[1]
    return _wgrad(a, b, name=name, grid_kn=(N_CHIPS, 1),
                  a_spec=pl.BlockSpec((None, TL_WGRAD, fs), lambda i, j, t: (i, t, 0)),
                  b_spec=pl.BlockSpec((TL_WGRAD, N), lambda i, j, t: (t, 0)),
                  out_shape=_sds((N_CHIPS, fs, N), F32),
                  out_spec=pl.BlockSpec((None, fs, N), lambda i, j, t: (i, 0, 0)))


def _mix_bwd(dx, z, aout, sa, sb, ypre, w_ap, w_ga, w_gb, w_out, rider=None):
    L = dx.shape[0]
    cs = D_MODEL // N_CHIPS
    ga_col = (3 * N_GROUPS * ATTN_WIDTH + SSM_WIDTH) // D_MODEL

    def body(dx_ref, ga_ref, gs_ref, aout_ref, sa_ref, sb_ref, ypre_ref, wap_ref, wga_ref, wgb_ref, wout_ref,
             dgates_ref, da_ref, gy_ref, daout_ref, dsa_ref, dsb_ref):
        dmix = _dot_nt(dx_ref[...].astype(BF16), wout_ref[...])
        sig_a = _sigmoid(ga_ref[...].astype(F32))
        sig_s = _sigmoid(gs_ref[...].astype(F32))
        a_out = aout_ref[...].astype(F32)
        s_a = sa_ref[...].astype(F32)
        sig_b = _sigmoid(sb_ref[...].astype(F32))
        s_out = s_a * sig_b
        daout = (dmix * sig_a).astype(BF16)
        daout_ref[...] = daout
        dgates_ref[:, :D_MODEL] = (dmix * a_out * sig_a * (1.0 - sig_a)).astype(BF16)
        dgates_ref[:, D_MODEL:] = (dmix * s_out * sig_s * (1.0 - sig_s)).astype(BF16)
        ds_out = dmix * sig_s
        dsa = (ds_out * sig_b).astype(BF16)
        dsb = (ds_out * s_a * sig_b * (1.0 - sig_b)).astype(BF16)
        dsa_ref[...] = dsa
        dsb_ref[...] = dsb
        da = jnp.zeros((TM_MIX, ATTN_WIDTH), F32)
        dy = jnp.zeros((TM_MIX, SSM_WIDTH), F32)
        for s in range(N_CHIPS):
            cols = slice(s * cs, (s + 1) * cs)
            da += _dot_nt(daout[:, cols], wap_ref[s])
            dy += _dot_nt(dsa[:, cols], wga_ref[s]) + _dot_nt(dsb[:, cols], wgb_ref[s])
        gy_ref[...] = dy * _gelu_grad(ypre_ref[...])
        da_ref[...] = da

    tok = lambda w: pl.BlockSpec((TM_MIX, w), lambda i: (i, 0))
    wsm = pl.BlockSpec((N_CHIPS, ATTN_WIDTH, cs), lambda i: (0, 0, 0))
    return _call(
        body, name="mix_bwd", grid=(L // TM_MIX,),
        in_specs=[tok(D_MODEL), pl.BlockSpec((TM_MIX, D_MODEL), lambda i: (i, ga_col)),
                  pl.BlockSpec((TM_MIX, D_MODEL), lambda i: (i, ga_col + 1)),
                  tok(D_MODEL), tok(D_MODEL), tok(D_MODEL), tok(SSM_WIDTH),
                  wsm, wsm, wsm, pl.BlockSpec((D_MODEL, D_MODEL), lambda i: (0, 0))],
        out_specs=[tok(2 * D_MODEL), tok(ATTN_WIDTH), tok(SSM_WIDTH)] + [tok(D_MODEL)] * 3,
        out_shape=[_sds((L, 2 * D_MODEL), BF16), _sds((L, ATTN_WIDTH), F32), _sds((L, SSM_WIDTH), F32)]
                  + [_sds((L, D_MODEL), BF16)] * 3,
        sem=("parallel",), rider=rider)(dx, z, z, aout, sa, sb, ypre, w_ap, w_ga, w_gb,
                                        w_out.reshape(D_MODEL, D_MODEL))


def _combine_bwd(da, os_, ls_):
    L = da.shape[0]

    def body(da_ref, o0, o1, o2, l0, l1, l2, hs_ref, p1_ref, p2_ref, pt1_ref, pt2_ref,
             do0, do1, do2, c0, c1, c2):
        o_tok, l_tok = _to_token_order((o0[...], o1[...], o2[...]), (l0[...], l1[...], l2[...]),
                                       (None, pt1_ref[...], pt2_ref[...]))
        w = _combine_weights(*l_tok)
        dav = da_ref[...]
        hs = hs_ref[...]
        tbar = sum(wg * _head_sum(dav * og, hs) for wg, og in zip(w, o_tok))
        for wg, pm, do_ref, c_ref in zip(w, (None, p1_ref[...], p2_ref[...]), (do0, do1, do2), (c0, c1, c2)):
            dog = (wg * dav).astype(BF16)
            cg = -wg * tbar
            do_ref[...] = dog if pm is None else _dot(pm, dog).astype(BF16)
            c_ref[...] = cg if pm is None else _dot(pm, cg.astype(BF16))

    tile = pl.BlockSpec((DL_TILE, ATTN_WIDTH), lambda i: (i, 0))
    mat = pl.BlockSpec((DL_TILE, DL_TILE), lambda i: (0, 0))
    p1, p2 = _perm_matrix(ATTN_PATTERN[1][1]), _perm_matrix(ATTN_PATTERN[2][1])
    outs = _call(body, name="combine_bwd", grid=(L // DL_TILE,), in_specs=[tile] * 7 + [mat] * 5,
                 out_specs=[tile] * 6,
                 out_shape=[_sds((L, ATTN_WIDTH), BF16)] * 3 + [_sds((L, ATTN_WIDTH), F32)] * 3,
                 sem=("parallel",))(da, *os_, *ls_, _head_sum_matrix(), p1, p2, p1.T, p2.T)
    return outs[:3], outs[3:]


def _attn_bwd(qs, ks, v, do, l, c, gi, rider=None):
    L = qs.shape[0]
    _, d = ATTN_PATTERN[gi]
    nb = L // (d * BLK)

    def body(q0_ref, q1_ref, k_ref, v_ref, do0_ref, do1_ref, l0_ref, l1_ref, c0_ref, c1_ref,
             dq_ref, dk_ref, dv_ref, carry):
        n = pl.program_id(1)

        @pl.when(n == 0)
        def _():
            carry[...] = jnp.zeros_like(carry)

        qi = lax.broadcasted_iota(jnp.int32, (2 * BLK, BLK), 0)
        kj = lax.broadcasted_iota(jnp.int32, (2 * BLK, BLK), 1)
        first = qi < BLK
        mask = jnp.logical_and(jnp.where(first, qi, kj) >= jnp.where(first, kj, qi - BLK),
                               qi < jnp.where(n < nb - 1, 2 * BLK, BLK))
        q2 = jnp.concatenate([_dl_read(q0_ref), _dl_read(q1_ref)], axis=0)
        do2 = jnp.concatenate([_dl_read(do0_ref), _dl_read(do1_ref)], axis=0)
        l2 = jnp.concatenate([_dl_read(l0_ref), _dl_read(l1_ref)], axis=0)
        c2 = jnp.concatenate([_dl_read(c0_ref), _dl_read(c1_ref)], axis=0)
        k = _dl_read(k_ref)
        v_ = _dl_read(v_ref)
        h0, h1 = _pair_masks()
        mask2 = jnp.concatenate([mask, mask], axis=1)
        dq_parts, dk_parts, dv_parts = [], [], []
        for hp in range(N_HEADS // 2):
            ls = slice(hp * LANES, (hp + 1) * LANES)
            qp, dop, kp_, vp_ = q2[:, ls], do2[:, ls], k[:, ls], v_[:, ls]
            kk = jnp.concatenate([jnp.where(h0, kp_, 0), jnp.where(h1, kp_, 0)], axis=0)
            vv = jnp.concatenate([jnp.where(h0, vp_, 0), jnp.where(h1, vp_, 0)], axis=0)

            def per_head(t):
                a = jnp.broadcast_to(t[:, hp * LANES:hp * LANES + 1], (2 * BLK, BLK))
                b = jnp.broadcast_to(t[:, hp * LANES + HEAD_DIM:hp * LANES + HEAD_DIM + 1], (2 * BLK, BLK))
                return jnp.concatenate([a, b], axis=1)

            p = jnp.where(mask2, jnp.exp(_dot_nt(qp, kk) - per_head(l2)), 0.0)
            ds = (p * (_dot_nt(dop, vv) + per_head(c2))).astype(BF16)
            dv2 = _dot_tn(p.astype(BF16), dop)
            dk2 = _dot_tn(ds, qp)
            dq2 = _dot(ds, kk)
            dq_parts.append((dq2[:BLK] + carry[:, ls]).astype(BF16))
            carry[:, ls] = dq2[BLK:]
            dk_parts.append(jnp.where(h0, dk2[:BLK], dk2[BLK:]).astype(BF16))
            dv_parts.append(jnp.where(h0, dv2[:BLK], dv2[BLK:]).astype(BF16))
        _dl_write(dq_ref, jnp.concatenate(dq_parts, axis=1))
        _dl_write(dk_ref, jnp.concatenate(dk_parts, axis=1))
        _dl_write(dv_ref, jnp.concatenate(dv_parts, axis=1))

    cur = _dl_spec(d, ATTN_WIDTH, lambda n: n)
    nxt = _dl_spec(d, ATTN_WIDTH, lambda n: jnp.minimum(n + 1, nb - 1))
    view = lambda t: _dl_view(t, d)
    outs = _call(
        body, name=f"attn_bwd_g{gi}", grid=(d, nb),
        in_specs=[cur, nxt, cur, cur, cur, nxt, cur, nxt, cur, nxt], out_specs=[cur, cur, cur],
        out_shape=[_sds(view(qs).shape, BF16)] * 3, scratch=[pltpu.VMEM((BLK, ATTN_WIDTH), F32)],
        sem=("parallel", "arbitrary"), rider=rider)(view(qs), view(qs), view(ks), view(v), view(do), view(do),
                                                    view(l), view(l), view(c), view(c))
    return [t.reshape(L, ATTN_WIDTH) for t in outs[:3]], list(outs[3:])


def _qkv_post(z, dqkv, du, dgates, gq_t, gk_t):
    L = z.shape[0]
    qkv_w = N_GROUPS * ATTN_WIDTH

    def body(zq_ref, zk_ref, gq_ref, gk_ref, hs_ref, pt1_ref, pt2_ref, du_ref, dgates_ref, *rest):
        dl_refs, (dz_ref, dgq_ref, dgk_ref) = rest[:9], rest[9:]

        @pl.when(pl.program_id(0) == 0)
        def _():
            dgq_ref[...] = jnp.zeros_like(dgq_ref)
            dgk_ref[...] = jnp.zeros_like(dgk_ref)

        hs = hs_ref[...]
        pts = (None, pt1_ref[...], pt2_ref[...])

        def rows8(t):
            return jnp.sum(t.reshape(DL_TILE // SUBLANES, SUBLANES, ATTN_WIDTH), axis=0)

        def norm_bwd(x, gain, dn):
            r = lax.rsqrt(_head_sum(x * x, hs) * (1.0 / HEAD_DIM) + EPS)
            xh = x * r
            dh = dn * gain
            return r * (dh - xh * (_head_sum(dh * xh, hs) * (1.0 / HEAD_DIM))), rows8(dn * xh)

        for g in range(N_GROUPS):
            tok = [t[...].astype(F32) if pts[g] is None else _dot(pts[g], t[...]) for t in dl_refs[3 * g:3 * g + 3]]
            cols = slice(g * ATTN_WIDTH, (g + 1) * ATTN_WIDTH)
            dq, pq = norm_bwd(zq_ref[:, cols].astype(F32), gq_ref[...] * SCALE, tok[0])
            dk, pk_ = norm_bwd(zk_ref[:, cols].astype(F32), gk_ref[...], tok[1])
            dgq_ref[...] += pq * SCALE
            dgk_ref[...] += pk_
            dz_ref[:, cols] = dq.astype(BF16)
            dz_ref[:, qkv_w + g * ATTN_WIDTH:qkv_w + (g + 1) * ATTN_WIDTH] = dk.astype(BF16)
            dz_ref[:, 2 * qkv_w + g * ATTN_WIDTH:2 * qkv_w + (g + 1) * ATTN_WIDTH] = tok[2].astype(BF16)
        dz_ref[:, 3 * qkv_w:3 * qkv_w + SSM_WIDTH] = du_ref[...]
        dz_ref[:, 3 * qkv_w + SSM_WIDTH:] = dgates_ref[...]

    tile = lambda w: pl.BlockSpec((DL_TILE, w), lambda i: (i, 0))
    mat = pl.BlockSpec((DL_TILE, DL_TILE), lambda i: (0, 0))
    vec = pl.BlockSpec((1, ATTN_WIDTH), lambda i: (0, 0))
    acc = pl.BlockSpec((SUBLANES, ATTN_WIDTH), lambda i: (0, 0))
    flat = [t for grp in dqkv for t in grp]
    return _call(
        body, name="qkv_post", grid=(L // DL_TILE,),
        in_specs=[tile(qkv_w), pl.BlockSpec((DL_TILE, qkv_w), lambda i: (i, 1)), vec, vec, mat, mat, mat,
                  tile(SSM_WIDTH), tile(2 * D_MODEL)] + [tile(ATTN_WIDTH)] * 9,
        out_specs=[tile(IN_COLS), acc, acc],
        out_shape=[_sds((L, IN_COLS), BF16), _sds((SUBLANES, ATTN_WIDTH), F32), _sds((SUBLANES, ATTN_WIDTH), F32)],
        sem=("arbitrary",))(z, z, gq_t, gk_t, _head_sum_matrix(), _perm_matrix(ATTN_PATTERN[1][1]).T,
                            _perm_matrix(ATTN_PATTERN[2][1]).T, du, dgates, *flat)


def _scan_rev_grad(sre, sim, rre, rim, are_ref, aim_ref, k0, init, seed_re, seed_im):
    ar = [_bcast(are_ref, k0 + kk) for kk in range(SCAN_GROUP)]
    ai = [-_bcast(aim_ref, k0 + kk) for kk in range(SCAN_GROUP)]

    def update(i, xprev, carry):
        out = []
        for kk in range(SCAN_GROUP):
            k = k0 + kk
            lr, li, dr, di = carry[4 * kk:4 * kk + 4]
            nr = ar[kk] * lr - ai[kk] * li + rre[k, _rows(i), :]
            ni = ar[kk] * li + ai[kk] * lr + rim[k, _rows(i), :]
            rre[k, _rows(i), :] = nr
            rim[k, _rows(i), :] = ni
            xr, xi = xprev(k)
            out += [nr, ni, dr + xr * nr + xi * ni, di + xr * ni - xi * nr]
        return tuple(out)

    def step(t, carry):
        i = SSM_TC - 1 - t
        return update(i, lambda k: (sre[k, _rows(i - 1), :], sim[k, _rows(i - 1), :]), carry)

    zero = jnp.zeros((SSM_SUB, LANES), F32)
    flat = []
    for re, im in init:
        flat += [re, im, zero, zero]
    res = lax.fori_loop(0, SSM_TC - 1, step, tuple(flat), unroll=3)
    res = update(0, lambda k: (seed_re[k], seed_im[k]), res)
    return [(res[4 * kk + 2], res[4 * kk + 3]) for kk in range(SCAN_GROUP)]


def _ssm_bwd(z, gy, pk, dskip, sd_re, sd_im, rider=None):
    L = z.shape[0]
    nb = L // SSM_TB
    ucol = (3 * N_GROUPS * ATTN_WIDTH) // SSM_WIDTH
    nwin = N_PAIR // PAIRS_PER_WIN

    def body(u_ref, gy_ref, are_ref, aim_ref, a64re_ref, a64im_ref, bwre_ref, bwim_ref, cwre_ref, cwim_ref, d_ref,
             sdre_ref, sdim_ref,
             du_ref, dare_ref, daim_ref, dbre_ref, dbim_ref, dcre_ref, dcim_ref, dd_ref,
             sre, sim, rre, rim, carry_re, carry_im, ends_re, ends_im, seed_re, seed_im):
        @pl.when(pl.program_id(0) == 0)
        def _():
            carry_re[...] = jnp.zeros_like(carry_re)
            carry_im[...] = jnp.zeros_like(carry_im)
            for ref in (dare_ref, daim_ref, dbre_ref, dbim_ref, dcre_ref, dcim_ref, dd_ref):
                ref[...] = jnp.zeros_like(ref)

        u = u_ref[...]
        gyv = gy_ref[...]
        gyb = gyv.astype(BF16)
        _ssm_fill(u, bwre_ref, bwim_ref, sre, sim)
        for k2 in range(N_PAIR):
            gw = gyb[:, _win_cols(k2)]
            _to_slabs(rre, k2, _dot_nt(gw, cwre_ref[k2]))
            _to_slabs(rim, k2, -_dot_nt(gw, cwim_ref[k2]))
        zero = jnp.zeros((SSM_SUB, LANES), F32)
        for k0 in range(0, N_SLAB, SCAN_GROUP):
            grp = range(k0, k0 + SCAN_GROUP)
            _scan(sre, sim, are_ref, aim_ref, k0, [(sdre_ref[k], sdim_ref[k]) for k in grp],
                  reverse=False, store=True)
            ends = _scan(rre, rim, are_ref, aim_ref, k0, [(zero, zero)] * SCAN_GROUP, reverse=True, store=False,
                         sign=-1.0)
            for kk, k in enumerate(grp):
                ends_re[k] = ends[kk][0]
                ends_im[k] = ends[kk][1]
            for k in grp:
                _ssm_seeds(ends_re, ends_im, a64re_ref, a64im_ref, carry_re, carry_im, seed_re, seed_im, k,
                           reverse=True, sign=-1.0)
            das = _scan_rev_grad(sre, sim, rre, rim, are_ref, aim_ref, k0,
                                 [(seed_re[k], seed_im[k]) for k in grp], sdre_ref, sdim_ref)
            for kk, k in enumerate(grp):
                dare_ref[k] += das[kk][0]
                daim_ref[k] += das[kk][1]
        for w in range(nwin):
            cols = _win_cols(w * PAIRS_PER_WIN)
            uw = u[:, cols]
            gw = gyb[:, cols]
            acc = gyv[:, cols] * d_ref[:, cols]
            for kk in range(PAIRS_PER_WIN):
                k2 = w * PAIRS_PER_WIN + kk
                lr = _pair_rows(rre, k2)
                li = _pair_rows(rim, k2)
                acc += _dot_nt(lr, bwre_ref[k2]) + _dot_nt(li, bwim_ref[k2])
                dbre_ref[k2] += _dot_tn(uw, lr)
                dbim_ref[k2] += _dot_tn(uw, li)
                dcre_ref[k2] += _dot_tn(_pair_rows(sre, k2), gw)
                dcim_ref[k2] -= _dot_tn(_pair_rows(sim, k2), gw)
            du_ref[:, cols] = acc.astype(BF16)
        dd_ref[...] += jnp.sum((gyv * u.astype(F32)).reshape(SSM_TB // SUBLANES, SUBLANES, SSM_WIDTH), axis=0)

    c2, c3 = _ssm_specs_consts()
    rev = lambda b: nb - 1 - b
    seed_spec = pl.BlockSpec((None, N_SLAB, SSM_SUB, LANES), lambda b: (rev(b), 0, 0, 0))
    tile_out = pl.BlockSpec((N_SLAB, SSM_SUB, LANES), lambda b: (0, 0, 0))
    small = pltpu.VMEM((N_SLAB, LANES), F32)
    tile = pltpu.VMEM((N_SLAB, SSM_SUB, LANES), F32)
    return _call(
        body, name="ssm_bwd", grid=(nb,),
        in_specs=[pl.BlockSpec((SSM_TB, SSM_WIDTH), lambda b: (rev(b), ucol)),
                  pl.BlockSpec((SSM_TB, SSM_WIDTH), lambda b: (rev(b), 0)),
                  c2, c2, c2, c2, c3, c3, c3, c3, pl.BlockSpec((1, SSM_WIDTH), lambda b: (0, 0)),
                  seed_spec, seed_spec],
        out_specs=[pl.BlockSpec((SSM_TB, SSM_WIDTH), lambda b: (rev(b), 0)), tile_out, tile_out, c3, c3, c3, c3,
                   pl.BlockSpec((SUBLANES, SSM_WIDTH), lambda b: (0, 0))],
        out_shape=[_sds((L, SSM_WIDTH), BF16), _sds((N_SLAB, SSM_SUB, LANES), F32),
                   _sds((N_SLAB, SSM_SUB, LANES), F32)] + [_sds((N_PAIR, SSM_WIN, SSM_WIN), F32)] * 4
                  + [_sds((SUBLANES, SSM_WIDTH), F32)],
        scratch=_ssm_scratch() + _ssm_scratch() + [small, small, tile, tile, tile, tile],
        sem=("arbitrary",), rider=rider)(z, gy, pk["a_re"], pk["a_im"], pk["a64_re"], pk["a64_im"],
                            pk["bw_re"].astype(BF16), pk["bw_im"].astype(BF16),
                            pk["cw_re"].astype(BF16), pk["cw_im"].astype(BF16), dskip, sd_re, sd_im)


def _in_proj_bwd(dz, w, x, g, dres, rider=None):
    L = x.shape[0]
    ns = w.shape[2]
    tn = ns
    nj = ns // tn
    nt = N_CHIPS * nj
    TM = TM_PROJ

    def body(dz_ref, w_ref, x_ref, g_ref, dres_ref, dx_ref, dg_ref, acc, dgacc):
        i, j = pl.program_id(0), pl.program_id(1)

        @pl.when(j == 0)
        def _():
            acc[...] = jnp.zeros_like(acc)

        @pl.when(jnp.logical_and(i == 0, j == 0))
        def _():
            dgacc[...] = jnp.zeros_like(dgacc)

        acc[...] += _dot_nt(dz_ref[...], w_ref[...])

        @pl.when(j == nt - 1)
        def _():
            dx, dg = _rms_bwd(x_ref[...], g_ref[...], acc[...])
            dx_ref[...] = dres_ref[...] + dx
            dgacc[...] += dg

        @pl.when(jnp.logical_and(i == pl.num_programs(0) - 1, j == nt - 1))
        def _():
            dg_ref[...] = jnp.sum(dgacc[...], axis=0, keepdims=True)

    tok = pl.BlockSpec((TM, D_MODEL), lambda i, j: (i, 0))
    vec = pl.BlockSpec((1, D_MODEL), lambda i, j: (0, 0))
    return _call(
        body, name="in_proj_bwd", grid=(L // TM, nt),
        in_specs=[pl.BlockSpec((TM, tn), lambda i, j: (i, j)),
                  pl.BlockSpec((None, D_MODEL, tn), lambda i, j: (j // nj, 0, j % nj)), tok, vec, tok],
        out_specs=[tok, vec],
        out_shape=[_sds((L, D_MODEL), F32), _sds((1, D_MODEL), F32)],
        scratch=[pltpu.VMEM((TM, D_MODEL), F32), pltpu.VMEM((SUBLANES, D_MODEL), F32)],
        sem=("arbitrary", "arbitrary"), rider=rider)(dz, w, x, g, dres)


SSM_PARAMS = ("lambda_re", "lambda_im", "log_dt", "b_re", "b_im", "c_re", "c_im")
EARLY = ("w_ffn_gate", "w_ffn_up", "w_ffn_down")
LATE = ("w_in", "w_attn_proj", "w_glu_a", "w_glu_b", "w_out")


def _layer_bwd(dx2, sv, p, pending, owned, l, idx):
    chip_idx, c_idx = idx
    g = {}
    owned = dict(owned)

    def settle(name, partial, arrived, layer):
        owned[name] = _sum_owner(partial, arrived, owned[name], layer, chip_idx, c_idx)

    dgate, dup = _ffn_bwd_act(dx2, sv["gate"], sv["up"], p["w_ffn_down"])
    outs = _ffn_bwd_in(dx2, sv["x1"], p["g_ffn"], dgate, dup, p["w_ffn_gate"], p["w_ffn_up"],
                       _scatter_rider([pending[n] for n in LATE[1:]]) if pending else None)
    dx1, g["g_ffn"] = outs[:2]
    for n, t in zip(LATE[1:], outs[2:]):
        settle(n, pending[n], t, l + 1)
    g["w_ffn_gate"] = _wgrad_ff_cols(sv["h2"], dgate, "wgrad_ffn_gate")
    g["w_ffn_up"] = _wgrad_ff_cols(sv["h2"], dup, "wgrad_ffn_up")
    g["w_ffn_down"] = _wgrad_ff_rows(sv["act"], dx2, "wgrad_ffn_down")

    outs = _mix_bwd(dx1, sv["z"], sv["aout"], sv["sa"], sv["sb"], sv["ypre"], p["w_attn_proj"], p["w_glu_a"],
                    p["w_glu_b"], p["w_out"], _swap_rider([g[n] for n in EARLY]))
    dgates, da, gy, daout, dsa, dsb = outs[:6]
    early = [_add_half(g[n], s, c_idx) for n, s in zip(EARLY, outs[6:])]
    g["w_out"] = _wgrad_full(sv["mix"], dx1, "wgrad_out").reshape(N_CHIPS, D_MODEL // N_CHIPS, D_MODEL)
    g["w_attn_proj"] = _wgrad_cols(sv["a"], daout, "wgrad_attn_proj")
    g["w_glu_a"] = _wgrad_cols(sv["yact"], dsa, "wgrad_glu_a")
    g["w_glu_b"] = _wgrad_cols(sv["yact"], dsb, "wgrad_glu_b")

    outs = _ssm_bwd(sv["z"], gy, sv["pk"], p["d_skip"], sv["sd_re"], sv["sd_im"],
                    _scatter_rider([pending[LATE[0]]]) if pending else None)
    du, da_re, da_im, dbw_re, dbw_im, dcw_re, dcw_im, dd = outs[:8]
    if pending:
        settle(LATE[0], pending[LATE[0]], outs[8], l + 1)
    g["d_skip"] = jnp.sum(dd, axis=0, keepdims=True)
    _, pull = jax.vjp(_ssm_pack, *[p[n] for n in SSM_PARAMS])
    zeros = jnp.zeros((N_SLAB, LANES), F32)
    ct = dict(a_re=jnp.sum(da_re, axis=1), a_im=jnp.sum(da_im, axis=1), a64_re=zeros, a64_im=zeros,
              bw_re=dbw_re, bw_im=dbw_im, cw_re=dcw_re, cw_im=dcw_im)
    for n, v in zip(SSM_PARAMS, pull(ct)):
        g[n] = v

    dos, cs = _combine_bwd(da, sv["os"], sv["ls"])
    dqkv = []
    for gi in range(N_GROUPS):
        grads, arrived = _attn_bwd(*sv["qkv"][gi], dos[gi], sv["ls"][gi], cs[gi], gi, _scatter_rider([early[gi]]))
        settle(EARLY[gi], early[gi], arrived[0], l)
        dqkv.append(grads)
    dz, gq8, gk8 = _qkv_post(sv["z"], dqkv, du, dgates, jnp.tile(p["g_q"], (1, N_HEADS)),
                             jnp.tile(p["g_k"], (1, N_HEADS)))
    g["g_q"] = jnp.sum(gq8.reshape(SUBLANES * N_HEADS, HEAD_DIM), axis=0, keepdims=True)
    g["g_k"] = jnp.sum(gk8.reshape(SUBLANES * N_HEADS, HEAD_DIM), axis=0, keepdims=True)
    g["w_in"] = _wgrad_cols(sv["h"], dz, "wgrad_in")
    outs = _in_proj_bwd(dz, p["w_in"], sv["x"], p["g_mix"], dx1, _swap_rider([g[n] for n in LATE]))
    dx, g["g_mix"] = outs[:2]
    late = {n: _add_half(g[n], s, c_idx) for n, s in zip(LATE, outs[2:])}
    return dx, g, late, owned


def _place():
    x, y, c = lax.axis_index("x"), lax.axis_index("y"), lax.axis_index("c")
    others = [(1 - x, y), (x, 1 - y), (1 - x, 1 - y)]
    return x, y, c, others


def _half(ref, hc):
    rows = ref.shape[-2] // 2
    idx = (slice(None),) * (len(ref.shape) - 2) + (pl.ds(hc * rows, rows), slice(None))
    return ref.at[idx]


def _comm_call(body, name, ins, out_shapes, n_remote, aliases=None):
    scratch = [pltpu.SemaphoreType.DMA((n_remote,)), pltpu.SemaphoreType.DMA((n_remote,))]
    return pl.pallas_call(
        body, name=name, in_specs=[ANY] * len(ins), out_specs=[ANY] * len(out_shapes), out_shape=out_shapes,
        scratch_shapes=scratch, input_output_aliases=aliases or {})(*ins)


def _cast_place(w, l, chip_idx):
    _, R, C = w.shape
    tr = R // 2

    def body(me_ref, w_ref, o_ref):
        o_ref[...] = w_ref[...].astype(BF16)

    return _call(body, name=f"cast_place_l{l}", grid=(R // tr,), prefetch=1,
                 in_specs=[pl.BlockSpec((None, tr, C), lambda i, me_ref: (l, i, 0))],
                 out_specs=pl.BlockSpec((None, tr, C), lambda i, me_ref: (me_ref[0], i, 0)),
                 out_shape=_sds((N_CHIPS, R, C), BF16), sem=("arbitrary",))(chip_idx, w)


def _in_place_rider(bufs, pairs):
    n = len(bufs)

    def copies(outs, send, recv, side):
        return [pltpu.make_async_remote_copy(src_ref=pair[side][0], dst_ref=pair[side][0], send_sem=send.at[k],
                                             recv_sem=recv.at[k], device_id=pair[side][1], device_id_type=MESH)
                for k, pair in enumerate(pairs(outs))]

    def start(ins, outs, send, recv):
        for cp in copies(outs, send, recv, 0):
            cp.start()

    def wait(ins, outs, send, recv):
        for cp in copies(outs, send, recv, 1):
            cp.wait_recv()
        for cp in copies(outs, send, recv, 0):
            cp.wait_send()

    return Rider(list(bufs), [_sds(b.shape, b.dtype) for b in bufs], 3 * n, start, wait, {a: a for a in range(n)})


def _gather_ici_rider(bufs):
    def pairs(outs):
        x, y, c, others = _place()
        return [((_half(o.at[2 * x + y], c), (cx, cy, c)), (_half(o.at[2 * cx + cy], c), (cx, cy, c)))
                for o in outs for cx, cy in others]
    return _in_place_rider(bufs, pairs)


def _gather_d2d_rider(bufs):
    def pairs(outs):
        x, y, c, others = _place()
        sib = (x, y, 1 - c)
        return [((_half(o.at[2 * cx + cy], c), sib), (_half(o.at[2 * cx + cy], 1 - c), sib))
                for o in outs for cx, cy in others]
    return _in_place_rider(bufs, pairs)


def _swap_rider(gs):
    n = len(gs)

    def copies(ins, outs, send, recv):
        x, y, c, _ = _place()
        return [pltpu.make_async_remote_copy(src_ref=_half(ins[a], 1 - c), dst_ref=outs[a], send_sem=send.at[a],
                                             recv_sem=recv.at[a], device_id=(x, y, 1 - c), device_id_type=MESH)
                for a in range(n)]

    def start(ins, outs, send, recv):
        for cp in copies(ins, outs, send, recv):
            cp.start()

    def wait(ins, outs, send, recv):
        for cp in copies(ins, outs, send, recv):
            cp.wait()

    outs = [_sds((g.shape[0], g.shape[1] // 2, g.shape[2]), g.dtype) for g in gs]
    return Rider(list(gs), outs, n, start, wait, {})


def _scatter_rider(ss):
    n = len(ss)

    def copies(ins, outs, send, recv):
        x, y, c, others = _place()
        return [pltpu.make_async_remote_copy(
            src_ref=ins[a].at[2 * cx + cy], dst_ref=outs[a].at[j], send_sem=send.at[3 * a + j],
            recv_sem=recv.at[3 * a + j], device_id=(cx, cy, c), device_id_type=MESH)
            for a in range(n) for j, (cx, cy) in enumerate(others)]

    def start(ins, outs, send, recv):
        for cp in copies(ins, outs, send, recv):
            cp.start()

    def wait(ins, outs, send, recv):
        for cp in copies(ins, outs, send, recv):
            cp.wait()

    outs = [_sds((N_CHIPS - 1,) + s.shape[1:], s.dtype) for s in ss]
    return Rider(list(ss), outs, 3 * n, start, wait, {})


def _run_rider(rider, name):
    n_in = len(rider.ins)

    def body(*refs):
        ins, outs = refs[:n_in], refs[n_in:n_in + len(rider.out_shapes)]
        send, recv = refs[n_in + len(rider.out_shapes):]
        rider.start(ins, outs, send, recv)
        rider.wait(ins, outs, send, recv)

    return _comm_call(body, name, rider.ins, rider.out_shapes, rider.n_sem, aliases=rider.aliases)


def _join_halves(bufs):
    n = len(bufs)

    def body(*refs):
        outs = refs[n:2 * n]
        send, recv = refs[2 * n:]
        x, y, c, _ = _place()

        def swap(a, hc):
            region = _half(outs[a], hc)
            return pltpu.make_async_remote_copy(src_ref=region, dst_ref=region, send_sem=send.at[a],
                                                recv_sem=recv.at[a], device_id=(x, y, 1 - c), device_id_type=MESH)

        cps = [swap(a, c) for a in range(n)]
        for cp in cps:
            cp.start()
        for a in range(n):
            swap(a, 1 - c).wait_recv()
        for cp in cps:
            cp.wait_send()

    outs = [_sds(b.shape, b.dtype) for b in bufs]
    return _comm_call(body, "join_halves", bufs, outs, n, aliases={a: a for a in range(n)})


def _gather_small(v):
    rows, n = v.shape

    def body(v_ref, out_ref, send, recv, lsem):
        x, y, c, others = _place()
        me, sibling = (x, y, c), (x, y, 1 - c)

        def blk(px, py, pc):
            return out_ref.at[pl.ds((4 * px + 2 * py + pc) * rows, rows), :]

        def copy(k, block, to, src=None):
            return pltpu.make_async_remote_copy(src_ref=blk(*block) if src is None else src, dst_ref=blk(*block),
                                                send_sem=send.at[k], recv_sem=recv.at[k], device_id=to,
                                                device_id_type=MESH)

        mine = pltpu.make_async_copy(v_ref, blk(*me), lsem)
        mine.start()
        first = [copy(0, me, sibling, src=v_ref)]
        first += [copy(1 + j, me, (*chip, c), src=v_ref) for j, chip in enumerate(others)]
        for cp in first:
            cp.start()
        passed = [copy(4 + j, (*chip, c), sibling) for j, chip in enumerate(others)]
        for j, chip in enumerate(others):
            copy(1 + j, (*chip, c), me).wait_recv()
            passed[j].start()
        copy(0, sibling, me).wait_recv()
        for j, chip in enumerate(others):
            copy(4 + j, (*chip, 1 - c), me).wait_recv()
        for cp in first + passed:
            cp.wait_send()
        mine.wait()

    return pl.pallas_call(
        body, name="gather_small", out_shape=_sds((8 * rows, n), v.dtype),
        in_specs=[pl.BlockSpec(memory_space=pltpu.VMEM)], out_specs=pl.BlockSpec(memory_space=pltpu.VMEM),
        scratch_shapes=[pltpu.SemaphoreType.DMA((7,)), pltpu.SemaphoreType.DMA((7,)), pltpu.SemaphoreType.DMA],
        compiler_params=pltpu.CompilerParams(vmem_limit_bytes=VMEM_LIMIT))(v)


def _add_half(g, p, c):
    _, R, C = g.shape
    half = R // 2

    def body(c_ref, g_ref, p_ref, o_ref):
        o_ref[...] = g_ref[...] + p_ref[...]

    blk = (None, half, C)
    return _call(body, name="add_half", grid=(N_CHIPS,), prefetch=1,
                 in_specs=[pl.BlockSpec(blk, lambda s, c_ref: (s, c_ref[0], 0)),
                           pl.BlockSpec(blk, lambda s, c_ref: (s, 0, 0))],
                 out_specs=pl.BlockSpec(blk, lambda s, c_ref: (s, 0, 0)),
                 out_shape=_sds((N_CHIPS, half, C), F32), sem=("arbitrary",))(c, g, p)


def _sum_owner(s, q, buf, l, me, c):
    _, half, C = s.shape
    tr = half // 2

    def body(me_ref, c_ref, s_ref, q0, q1, q2, buf_ref, o_ref):
        o_ref[...] = ((s_ref[...] + q0[...]) + q1[...]) + q2[...]

    blk = (None, tr, C)
    qspec = lambda j: pl.BlockSpec(blk, lambda i, me_ref, c_ref: (j, i, 0))
    return _call(body, name=f"sum_owner_l{l}", grid=(half // tr,), prefetch=2,
                 in_specs=[pl.BlockSpec(blk, lambda i, me_ref, c_ref: (me_ref[0], i, 0)),
                           qspec(0), qspec(1), qspec(2), ANY],
                 out_specs=pl.BlockSpec(blk, lambda i, me_ref, c_ref: (l, 2 * c_ref[0] + i, 0)),
                 out_shape=_sds(buf.shape, F32), sem=("arbitrary",), aliases={6: 0})(me, c, s, q, q, q, buf)


def _adamw_math(w, g, m, v):
    m = ADAM_B1 * m + (1.0 - ADAM_B1) * g
    v = ADAM_B2 * v + (1.0 - ADAM_B2) * (g * g)
    m_hat = m / (1.0 - ADAM_B1 ** ADAM_STEP)
    v_hat = v / (1.0 - ADAM_B2 ** ADAM_STEP)
    delta = -ADAM_LR * (m_hat / (jnp.sqrt(v_hat) + ADAM_EPS) + ADAM_WD * w)
    return delta, m, v


def _adamw(w, g, m, v):
    rows, C = w.shape
    tr = next(t for t in (256, 128, 64) if rows % t == 0)

    def body(w_ref, g_ref, m_ref, v_ref, d_ref, nm_ref, nv_ref):
        d, nm, nv = _adamw_math(w_ref[...], g_ref[...], m_ref[...], v_ref[...])
        d_ref[...] = d
        nm_ref[...] = nm
        nv_ref[...] = nv

    spec = pl.BlockSpec((tr, C), lambda i: (i, 0))
    return _call(body, name="adamw", grid=(rows // tr,), in_specs=[spec] * 4, out_specs=[spec] * 3,
                 out_shape=[_sds((rows, C), F32)] * 3, sem=("parallel",))(w, g, m, v)


def _small_update(gathered, w, m, v):
    _, rows, n = gathered.shape
    tr = rows // 7

    def body(ga_ref, w_ref, m_ref, v_ref, g_ref, d_ref, nm_ref, nv_ref):
        g = ga_ref[0]
        for k in range(1, 8):
            g = g + ga_ref[k]
        d, nm, nv = _adamw_math(w_ref[...], g, m_ref[...], v_ref[...])
        g_ref[...] = g
        d_ref[...] = d
        nm_ref[...] = nm
        nv_ref[...] = nv

    spec = pl.BlockSpec((tr, n), lambda i: (i, 0))
    return _call(body, name="small_update", grid=(rows // tr,),
                 in_specs=[pl.BlockSpec((8, tr, n), lambda i: (0, i, 0)), spec, spec, spec], out_specs=[spec] * 4,
                 out_shape=[_sds((rows, n), F32)] * 4, sem=("parallel",))(gathered, w, m, v)


WEIGHTS = ("g_mix", "w_in", "g_q", "g_k", "w_attn_proj", "lambda_re", "lambda_im", "log_dt", "b_re", "b_im",
           "c_re", "c_im", "d_skip", "w_glu_a", "w_glu_b", "w_out", "g_ffn", "w_ffn_gate", "w_ffn_up", "w_ffn_down")
BIG = ("w_in", "w_attn_proj", "w_glu_a", "w_glu_b", "w_out", "w_ffn_gate", "w_ffn_up", "w_ffn_down")
SMALL = tuple(n for n in WEIGHTS if n not in BIG)
ROW_VECTORS = ("g_mix", "g_q", "g_k", "d_skip", "g_ffn")
PACK_QUANTUM = LANES * SUBLANES * 7


def _pack_small(parts, extra):
    flat = jnp.concatenate([parts[n].reshape(-1).astype(F32) for n in SMALL] + [extra.reshape(-1)])
    pad = -flat.shape[0] % PACK_QUANTUM
    return jnp.pad(flat, (0, pad)).reshape(-1, LANES)


def _unpack_small(packed, like):
    flat = packed.reshape(-1)
    out, at = {}, 0
    for n in SMALL:
        size = math.prod(like[n].shape)
        out[n] = flat[at:at + size].reshape(like[n].shape)
        at += size
    return out, flat[at]


def kernel(x, g_mix, w_in, g_q, g_k, w_attn_proj, lambda_re, lambda_im, log_dt, b_re, b_im, c_re, c_im, d_skip, w_glu_a, w_glu_b, w_out, g_ffn, w_ffn_gate, w_ffn_up, w_ffn_down, loss_target, m_g_mix, m_w_in, m_g_q, m_g_k, m_w_attn_proj, m_lambda_re, m_lambda_im, m_log_dt, m_b_re, m_b_im, m_c_re, m_c_im, m_d_skip, m_w_glu_a, m_w_glu_b, m_w_out, m_g_ffn, m_w_ffn_gate, m_w_ffn_up, m_w_ffn_down, v_g_mix, v_w_in, v_g_q, v_g_k, v_w_attn_proj, v_lambda_re, v_lambda_im, v_log_dt, v_b_re, v_b_im, v_c_re, v_c_im, v_d_skip, v_w_glu_a, v_w_glu_b, v_w_out, v_g_ffn, v_w_ffn_gate, v_w_ffn_up, v_w_ffn_down):
    given = dict(locals())
    W = {n: given[n] for n in WEIGHTS}
    M = {n: given["m_" + n] for n in WEIGHTS}
    V = {n: given["v_" + n] for n in WEIGHTS}
    depth = g_mix.shape[0]
    xl = x.reshape(x.shape[-2:])
    target = loss_target.reshape(loss_target.shape[-2:])
    c_idx = lax.axis_index("c").astype(jnp.int32).reshape(1)
    chip_idx = (2 * lax.axis_index("x") + lax.axis_index("y")).astype(jnp.int32).reshape(1)

    place = lambda l: [_cast_place(W[n], l, chip_idx) for n in BIG]
    bufs = place(0)
    w_in = _run_rider(_gather_d2d_rider(_run_rider(_gather_ici_rider(bufs[:1]), "gather_ici")), "gather_d2d")[0]
    rest, stage = bufs[1:], "ici"
    params, saved, h = [], [], xl
    for l in range(depth):
        p = {"w_in": w_in}
        for n in SMALL:
            p[n] = W[n][l][None] if n in ROW_VECTORS else W[n][l]
        h, sv, p, nxt = _layer_fwd(h, p, rest, stage, place(l + 1) if l + 1 < depth else None)
        params.append(p)
        saved.append(sv)
        if nxt:
            (w_in, rest), stage = nxt, "d2d"
    dx, loss_part = _loss_head(h, target)

    owned = {n: lax.empty(W[n].shape, F32) for n in BIG}
    small_grads = [None] * depth
    pending = None
    for l in reversed(range(depth)):
        dx, small_grads[l], pending, owned = _layer_bwd(dx, saved[l], params[l], pending, owned, l,
                                                        (chip_idx, c_idx))
    arrived = _run_rider(_scatter_rider([pending[n] for n in LATE]), "scatter_to_owners")
    for n, q in zip(LATE, arrived):
        owned[n] = _sum_owner(pending[n], q, owned[n], 0, chip_idx, c_idx)
    reduced = dict(zip(BIG, _join_halves([owned[n] for n in BIG])))

    grads, delta, new_m, new_v = {}, {}, {}, {}
    for n in BIG:
        shape = W[n].shape
        two_d = (shape[0] * shape[1], shape[2])
        d, nm, nv = _adamw(W[n].reshape(two_d), reduced[n].reshape(two_d), M[n].reshape(two_d), V[n].reshape(two_d))
        grads[n], delta[n], new_m[n], new_v[n] = reduced[n], d.reshape(shape), nm.reshape(shape), nv.reshape(shape)

    stacked = {n: jnp.stack([small_grads[l][n] for l in range(depth)]) for n in SMALL}
    zero = jnp.zeros((1,), F32)
    packed = _pack_small(stacked, loss_part)
    gathered = _gather_small(packed).reshape(8, *packed.shape)
    gs, ds, nms, nvs = _small_update(gathered, _pack_small(W, zero), _pack_small(M, zero), _pack_small(V, zero))
    sg, loss = _unpack_small(gs, W)
    sd, _ = _unpack_small(ds, W)
    sm, _ = _unpack_small(nms, W)
    sv_, _ = _unpack_small(nvs, W)
    for n in SMALL:
        grads[n], delta[n], new_m[n], new_v[n] = sg[n], sd[n], sm[n], sv_[n]

    return (loss, dx.reshape(x.shape), *[grads[n] for n in WEIGHTS], *[delta[n] for n in WEIGHTS],
            *[new_m[n] for n in WEIGHTS], *[new_v[n] for n in WEIGHTS])
```

```python
import collections
import functools
import math

import jax
import jax.numpy as jnp
from jax import lax
from jax.experimental import pallas as pl
from jax.experimental.pallas import tpu as pltpu

F32 = jnp.float32
BF16 = jnp.bfloat16

D_MODEL = 1024
DEPTH = 4
HEAD_DIM = 64
N_HEADS = 8
ATTN_WIDTH = N_HEADS * HEAD_DIM
ATTN_PATTERN = ((128, 1), (512, 4), (2048, 16))
N_GROUPS = len(ATTN_PATTERN)
BLK = 128
SSM_WIDTH = 512
SSM_GROUP = 16
SSM_GROUPS = 32
SSM_STATE = 64
D_FF = 2816
IN_COLS = 7168
EPS = 1e-6
ADAM_LR, ADAM_B1, ADAM_B2, ADAM_EPS, ADAM_WD, ADAM_STEP = 0.001, 0.9, 0.999, 1e-08, 0.01, 10

N_CHIPS = 4
MESH = pl.DeviceIdType.MESH

LANES = 128
SUBLANES = 8
VMEM_LIMIT = 56 * 1024 * 1024

TM = 512
TM_PROJ = 1024
TL_WGRAD = 2048
TM_MIX = 512

SSM_TB = 512
SSM_TC = 64
SSM_SUB = SUBLANES
SSM_PITCH = 72
N_SLAB = SSM_GROUPS * SSM_STATE // LANES
SSM_WIN = 256
N_PAIR = N_SLAB // 2
PAIRS_PER_WIN = 4
SCAN_GROUP = 4


def _params(sem=None, collective=False):
    return pltpu.CompilerParams(dimension_semantics=sem, vmem_limit_bytes=VMEM_LIMIT)


ANY = pl.BlockSpec(memory_space=pl.ANY)

Rider = collections.namedtuple("Rider", "ins out_shapes n_sem start wait aliases")


class _SemWindow:
    def __init__(self, ref, offset):
        self.ref, self.offset = ref, offset

    @property
    def at(self):
        return self

    def __getitem__(self, k):
        return self.ref.at[self.offset + k]


def _join_riders(*riders):
    riders = [r for r in riders if r is not None]
    if len(riders) <= 1:
        return riders[0] if riders else None

    def each(fn_name):
        def run(ins, outs, send, recv):
            i = o = s = 0
            for r in riders:
                getattr(r, fn_name)(ins[i:i + len(r.ins)], outs[o:o + len(r.out_shapes)],
                                    _SemWindow(send, s), _SemWindow(recv, s))
                i, o, s = i + len(r.ins), o + len(r.out_shapes), s + r.n_sem
        return run

    aliases, i, o = {}, 0, 0
    for r in riders:
        aliases.update({i + a: o + b for a, b in r.aliases.items()})
        i, o = i + len(r.ins), o + len(r.out_shapes)
    return Rider([t for r in riders for t in r.ins], [t for r in riders for t in r.out_shapes],
                 sum(r.n_sem for r in riders), each("start"), each("wait"), aliases)


def _with_rider(body, rider, grid, prefetch, n_in, n_out, n_scratch):
    n_rin, n_rout = len(rider.ins), len(rider.out_shapes)

    def hosted(*refs):
        pre, rest = refs[:prefetch], refs[prefetch:]
        ins, rin = rest[:n_in], rest[n_in:n_in + n_rin]
        o0 = n_in + n_rin
        outs, rout = rest[o0:o0 + n_out], rest[o0 + n_out:o0 + n_out + n_rout]
        s0 = o0 + n_out + n_rout
        scr, (send, recv) = rest[s0:s0 + n_scratch], rest[s0 + n_scratch:]
        first = functools.reduce(jnp.logical_and, [pl.program_id(k) == 0 for k in range(len(grid))])
        last = functools.reduce(jnp.logical_and, [pl.program_id(k) == grid[k] - 1 for k in range(len(grid))])

        @pl.when(first)
        def _():
            rider.start(rin, rout, send, recv)

        body(*pre, *ins, *outs, *scr)

        @pl.when(last)
        def _():
            rider.wait(rin, rout, send, recv)

    return hosted


def _call(body, *, name, grid, in_specs, out_specs, out_shape, scratch=(), sem=None, aliases=None,
          prefetch=0, rider=None):
    if rider is not None:
        single = not isinstance(out_specs, (list, tuple))
        out_specs = [out_specs] if single else list(out_specs)
        out_shape = [out_shape] if single else list(out_shape)
        body = _with_rider(body, rider, grid, prefetch, len(in_specs), len(out_specs), len(scratch))
        aliases = dict(aliases or {})
        aliases.update({prefetch + len(in_specs) + k: len(out_specs) + v for k, v in rider.aliases.items()})
        in_specs = list(in_specs) + [ANY] * len(rider.ins)
        out_specs = out_specs + [ANY] * len(rider.out_shapes)
        out_shape = out_shape + list(rider.out_shapes)
        scratch = list(scratch) + [pltpu.SemaphoreType.DMA((rider.n_sem,)), pltpu.SemaphoreType.DMA((rider.n_sem,))]
        sem = ("arbitrary",) * len(grid)
        fn = _call(body, name=name + "_host", grid=grid, in_specs=in_specs, out_specs=out_specs, out_shape=out_shape,
                   scratch=scratch, sem=sem, aliases=aliases, prefetch=prefetch)
        return lambda *args: fn(*args, *rider.ins)
    kw = {}
    if aliases:
        kw["input_output_aliases"] = aliases
    if prefetch:
        gs = pltpu.PrefetchScalarGridSpec(num_scalar_prefetch=prefetch, grid=grid, in_specs=in_specs,
                                          out_specs=out_specs, scratch_shapes=list(scratch))
        return pl.pallas_call(body, name=name, grid_spec=gs, out_shape=out_shape,
                              compiler_params=_params(sem), **kw)
    return pl.pallas_call(body, name=name, grid=grid, in_specs=in_specs, out_specs=out_specs,
                          out_shape=out_shape, scratch_shapes=list(scratch),
                          compiler_params=_params(sem), **kw)


def _sds(shape, dtype):
    return jax.ShapeDtypeStruct(shape, dtype)


def _sigmoid(v):
    return 1.0 / (1.0 + jnp.exp(-v))


def _dot(a, b):
    return jnp.dot(a, b, preferred_element_type=F32)


def _dot_nt(a, b):
    return lax.dot_general(a, b, (((1,), (1,)), ((), ())), preferred_element_type=F32)


def _dot_tn(a, b):
    return lax.dot_general(a, b, (((0,), (0,)), ((), ())), preferred_element_type=F32)


def _in_proj_fwd(x, g, w, rider=None):
    L = x.shape[0]
    ns = w.shape[2]
    tn = ns
    nj = ns // tn
    TM = TM_PROJ

    def body(x_ref, g_ref, w_ref, z_ref, h_ref):
        @pl.when(pl.program_id(1) == 0)
        def _():
            xv = x_ref[...]
            r = lax.rsqrt(jnp.mean(xv * xv, axis=-1, keepdims=True) + EPS)
            h_ref[...] = (xv * r * g_ref[...]).astype(BF16)
        z_ref[...] = _dot(h_ref[...], w_ref[...]).astype(BF16)

    return _call(
        body, name="in_proj_fwd", grid=(L // TM, N_CHIPS * nj),
        in_specs=[pl.BlockSpec((TM, D_MODEL), lambda i, j: (i, 0)),
                  pl.BlockSpec((1, D_MODEL), lambda i, j: (0, 0)),
                  pl.BlockSpec((None, D_MODEL, tn), lambda i, j: (j // nj, 0, j % nj))],
        out_specs=[pl.BlockSpec((TM, tn), lambda i, j: (i, j)),
                   pl.BlockSpec((TM, D_MODEL), lambda i, j: (i, 0))],
        out_shape=[_sds((L, N_CHIPS * ns), BF16), _sds((L, D_MODEL), BF16)],
        sem=("parallel", "arbitrary"), rider=rider)(x, g, w)


DL_TILE = 512
SCALE = HEAD_DIM ** -0.5


def _perm_matrix(d):
    rho = jnp.arange(DL_TILE)
    src = rho // (DL_TILE // d) + d * (rho % (DL_TILE // d))
    return (src[:, None] == jnp.arange(DL_TILE)[None, :]).astype(BF16)


def _head_sum_matrix():
    h = jnp.arange(ATTN_WIDTH) // HEAD_DIM
    return (h[:, None] == h[None, :]).astype(BF16)


def _split(v):
    hi = v.astype(BF16)
    return hi, (v - hi.astype(F32)).astype(BF16)


def _head_sum(v, hs):
    vb = v.astype(BF16)
    half = ATTN_WIDTH // 2
    blk = hs[:half, :half]
    return jnp.concatenate([_dot(vb[:, :half], blk), _dot(vb[:, half:], blk)], axis=1)


def _permute(pm, v):
    hi, lo = _split(v)
    return _dot(pm, hi) + _dot(pm, lo)


def _dl_view(t, d):
    if d * BLK <= DL_TILE:
        return t
    return t.reshape(t.shape[0] // DL_TILE, d, DL_TILE // d, t.shape[1])


def _dl_spec(d, width, which):
    if d * BLK <= DL_TILE:
        per_tile = DL_TILE // (d * BLK)
        return pl.BlockSpec((BLK, width), lambda r, n: ((which(n) // per_tile) * (DL_TILE // BLK)
                                                       + r * per_tile + which(n) % per_tile, 0))
    tiles = d * BLK // DL_TILE
    return pl.BlockSpec((tiles, None, DL_TILE // d, width), lambda r, n: (which(n), r, 0, 0))


def _dl_read(ref):
    v = ref[...]
    return v if v.ndim == 2 else v.reshape(BLK, v.shape[-1])


def _dl_write(ref, v):
    ref[...] = v if len(ref.shape) == 2 else v.reshape(ref.shape)


def _qkv_prep(z, gq_t, gk_t, rider=None):
    L = z.shape[0]
    qkv_w = N_GROUPS * ATTN_WIDTH

    def body(zq_ref, zk_ref, zv_ref, gq_ref, gk_ref, hs_ref, p1_ref, p2_ref, *outs):
        hs = hs_ref[...]
        perms = (None, p1_ref[...], p2_ref[...])
        for g in range(N_GROUPS):
            cols = slice(g * ATTN_WIDTH, (g + 1) * ATTN_WIDTH)
            xq = zq_ref[:, cols].astype(F32)
            xk = zk_ref[:, cols].astype(F32)
            rq = lax.rsqrt(_head_sum(xq * xq, hs) * (1.0 / HEAD_DIM) + EPS)
            rk = lax.rsqrt(_head_sum(xk * xk, hs) * (1.0 / HEAD_DIM) + EPS)
            vals = [(xq * rq * (gq_ref[...] * SCALE)).astype(BF16), (xk * rk * gk_ref[...]).astype(BF16),
                    zv_ref[:, cols]]
            for j, t in enumerate(vals):
                if perms[g] is not None:
                    t = _dot(perms[g], t).astype(BF16)
                outs[3 * g + j][...] = t

    tile = pl.BlockSpec((DL_TILE, ATTN_WIDTH), lambda i: (i, 0))
    mat = pl.BlockSpec((DL_TILE, DL_TILE), lambda i: (0, 0))
    vec = pl.BlockSpec((1, ATTN_WIDTH), lambda i: (0, 0))
    outs = _call(
        body, name="qkv_prep", grid=(L // DL_TILE,),
        in_specs=[pl.BlockSpec((DL_TILE, qkv_w), lambda i: (i, 0)), pl.BlockSpec((DL_TILE, qkv_w), lambda i: (i, 1)),
                  pl.BlockSpec((DL_TILE, qkv_w), lambda i: (i, 2)), vec, vec, mat, mat, mat],
        out_specs=[tile] * 9, out_shape=[_sds((L, ATTN_WIDTH), BF16)] * 9,
        sem=("parallel",), rider=rider)(z, z, z, gq_t, gk_t, _head_sum_matrix(), _perm_matrix(ATTN_PATTERN[1][1]),
                                        _perm_matrix(ATTN_PATTERN[2][1]))
    return [tuple(outs[3 * g:3 * g + 3]) for g in range(N_GROUPS)], list(outs[3 * N_GROUPS:])


def _pair_masks():
    lane = lax.broadcasted_iota(jnp.int32, (1, LANES), 1)
    return lane < HEAD_DIM, lane >= HEAD_DIM


def _attn_fwd(qs, ks, v, gi):
    L = qs.shape[0]
    _, d = ATTN_PATTERN[gi]
    nb = L // (d * BLK)

    def body(q_ref, kc_ref, kp_ref, vc_ref, vp_ref, o_ref, l_ref):
        n = pl.program_id(1)
        qi = lax.broadcasted_iota(jnp.int32, (BLK, 2 * BLK), 0)
        kj = lax.broadcasted_iota(jnp.int32, (BLK, 2 * BLK), 1)
        prev = kj < BLK
        mask = jnp.logical_and(jnp.where(prev, kj, qi) >= jnp.where(prev, qi, kj - BLK),
                               kj >= jnp.where(n > 0, 0, BLK))
        q = _dl_read(q_ref)
        kw = jnp.concatenate([_dl_read(kp_ref), _dl_read(kc_ref)], axis=0)
        vw = jnp.concatenate([_dl_read(vp_ref), _dl_read(vc_ref)], axis=0)
        one = jnp.ones((2 * BLK, LANES), BF16)
        o_parts, l_parts = [], []
        for hp in range(N_HEADS // 2):
            ls = slice(hp * LANES, (hp + 1) * LANES)
            qp, kp_, vp_ = q[:, ls], kw[:, ls], vw[:, ls]
            num = jnp.zeros((BLK, LANES), F32)
            den = jnp.zeros((BLK, LANES), F32)
            mb = jnp.zeros((BLK, LANES), F32)
            for he in _pair_masks():
                s = jnp.where(mask, _dot_nt(jnp.where(he, qp, 0), kp_), -jnp.inf)
                m = jnp.max(s, axis=-1, keepdims=True)
                p = jnp.exp(s - m).astype(BF16)
                acc = _dot(p, jnp.concatenate([jnp.where(he, vp_, 0), jnp.where(he, one, 0)], axis=1))
                num += acc[:, :LANES]
                den += acc[:, LANES:]
                mb = jnp.where(he, m, mb)
            o_parts.append((num / den).astype(BF16))
            l_parts.append(mb + jnp.log(den))
        _dl_write(o_ref, jnp.concatenate(o_parts, axis=1))
        _dl_write(l_ref, jnp.concatenate(l_parts, axis=1))

    cur = _dl_spec(d, ATTN_WIDTH, lambda n: n)
    prev = _dl_spec(d, ATTN_WIDTH, lambda n: jnp.maximum(n - 1, 0))
    view = lambda t: _dl_view(t, d)
    o, l = _call(
        body, name=f"attn_fwd_g{gi}", grid=(d, nb), in_specs=[cur, cur, prev, cur, prev], out_specs=[cur, cur],
        out_shape=[_sds(view(qs).shape, BF16), _sds(view(qs).shape, F32)],
        sem=("parallel", "parallel"))(view(qs), view(ks), view(ks), view(v), view(v))
    return o.reshape(L, ATTN_WIDTH), l.reshape(L, ATTN_WIDTH)


def _to_token_order(os_, ls_, pts):
    o_tok, l_tok = [], []
    for o, l, pt in zip(os_, ls_, pts):
        if pt is None:
            o_tok.append(o.astype(F32))
            l_tok.append(l)
        else:
            o_tok.append(_dot(pt, o))
            l_tok.append(_permute(pt, l))
    return o_tok, l_tok


def _combine_fwd(os_, ls_):
    L = os_[0].shape[0]

    def body(o0, o1, o2, l0, l1, l2, pt1_ref, pt2_ref, a_ref):
        o_tok, l_tok = _to_token_order((o0[...], o1[...], o2[...]), (l0[...], l1[...], l2[...]),
                                       (None, pt1_ref[...], pt2_ref[...]))
        w = _combine_weights(*l_tok)
        a_ref[...] = (w[0] * o_tok[0] + w[1] * o_tok[1] + w[2] * o_tok[2]).astype(BF16)

    tile = pl.BlockSpec((DL_TILE, ATTN_WIDTH), lambda i: (i, 0))
    mat = pl.BlockSpec((DL_TILE, DL_TILE), lambda i: (0, 0))
    return _call(body, name="combine_fwd", grid=(L // DL_TILE,), in_specs=[tile] * 6 + [mat, mat], out_specs=tile,
                 out_shape=_sds((L, ATTN_WIDTH), BF16), sem=("parallel",))(
                     *os_, *ls_, _perm_matrix(ATTN_PATTERN[1][1]).T, _perm_matrix(ATTN_PATTERN[2][1]).T)


def _gelu(v):
    c = math.sqrt(2.0 / math.pi)
    return 0.5 * v * (1.0 + jnp.tanh(c * (v + 0.044715 * v * v * v)))


def _gelu_grad(v):
    c = math.sqrt(2.0 / math.pi)
    t = jnp.tanh(c * (v + 0.044715 * v * v * v))
    return 0.5 * (1.0 + t) + 0.5 * v * (1.0 - t * t) * c * (1.0 + 3.0 * 0.044715 * v * v)


def _ssm_fill(u, bwre_ref, bwim_ref, sre, sim):
    for k2 in range(N_PAIR):
        uw = u[:, _win_cols(k2)]
        _to_slabs(sre, k2, _dot(uw, bwre_ref[k2]))
        _to_slabs(sim, k2, _dot(uw, bwim_ref[k2]))


def _win_cols(k2):
    w = k2 // PAIRS_PER_WIN
    return slice(w * SSM_WIN, (w + 1) * SSM_WIN)


def _to_slabs(ref, k2, v):
    for half in range(2):
        for j in range(SSM_SUB):
            ref[2 * k2 + half, j * SSM_PITCH:j * SSM_PITCH + SSM_TC, :] = (
                v[j * SSM_TC:(j + 1) * SSM_TC, half * LANES:(half + 1) * LANES])


def _rows(i):
    return pl.ds(i, SSM_SUB, stride=SSM_PITCH)


def _slab_rows(ref, k):
    return jnp.concatenate([ref[k, j * SSM_PITCH:j * SSM_PITCH + SSM_TC, :] for j in range(SSM_SUB)], axis=0)


def _pair_rows(ref, k2):
    return jnp.concatenate([_slab_rows(ref, 2 * k2), _slab_rows(ref, 2 * k2 + 1)], axis=1).astype(BF16)


def _bcast(ref, k):
    return jnp.broadcast_to(ref[pl.ds(k, 1), :], (SSM_SUB, LANES))


def _scan(sre, sim, are_ref, aim_ref, k0, init, *, reverse, store, sign=1.0):
    ar = [_bcast(are_ref, k0 + kk) for kk in range(SCAN_GROUP)]
    ai = [sign * _bcast(aim_ref, k0 + kk) for kk in range(SCAN_GROUP)]

    def step(t, carry):
        i = SSM_TC - 1 - t if reverse else t
        out = []
        for kk in range(SCAN_GROUP):
            k = k0 + kk
            xr, xi = carry[2 * kk], carry[2 * kk + 1]
            nr = ar[kk] * xr - ai[kk] * xi + sre[k, _rows(i), :]
            ni = ar[kk] * xi + ai[kk] * xr + sim[k, _rows(i), :]
            if store:
                sre[k, _rows(i), :] = nr
                sim[k, _rows(i), :] = ni
            out += [nr, ni]
        return tuple(out)

    flat = []
    for re, im in init:
        flat += [re, im]
    res = lax.fori_loop(0, SSM_TC, step, tuple(flat), unroll=2)
    return [(res[2 * kk], res[2 * kk + 1]) for kk in range(SCAN_GROUP)]


def _ssm_seeds(ends_re, ends_im, a64re_ref, a64im_ref, carry_re, carry_im, seed_re, seed_im, k,
               *, reverse, sign=1.0):
    ar = a64re_ref[pl.ds(k, 1), :]
    ai = sign * a64im_ref[pl.ds(k, 1), :]
    cr = carry_re[pl.ds(k, 1), :]
    ci = carry_im[pl.ds(k, 1), :]
    order = range(SSM_SUB - 1, -1, -1) if reverse else range(SSM_SUB)
    for j in order:
        seed_re[k, pl.ds(j, 1), :] = cr
        seed_im[k, pl.ds(j, 1), :] = ci
        er = ends_re[k, pl.ds(j, 1), :]
        ei = ends_im[k, pl.ds(j, 1), :]
        cr, ci = ar * cr - ai * ci + er, ar * ci + ai * cr + ei
    carry_re[pl.ds(k, 1), :] = cr
    carry_im[pl.ds(k, 1), :] = ci


def _ssm_specs_consts():
    c2 = pl.BlockSpec((N_SLAB, LANES), lambda b: (0, 0))
    c3 = pl.BlockSpec((N_PAIR, SSM_WIN, SSM_WIN), lambda b: (0, 0, 0))
    return c2, c3


def _ssm_scratch():
    rows = SSM_SUB * SSM_PITCH
    return [pltpu.VMEM((N_SLAB, rows, LANES), F32), pltpu.VMEM((N_SLAB, rows, LANES), F32)]


def _ssm_fwd(z, pk, dskip, rider=None):
    L = z.shape[0]
    nb = L // SSM_TB
    ucol = (3 * N_GROUPS * ATTN_WIDTH) // SSM_WIDTH

    def body(u_ref, are_ref, aim_ref, a64re_ref, a64im_ref, bwre_ref, bwim_ref, cwre_ref, cwim_ref, d_ref,
             ypre_ref, yact_ref, sdre_ref, sdim_ref, sre, sim, carry_re, carry_im, ends_re, ends_im,
             seed_re, seed_im):
        @pl.when(pl.program_id(0) == 0)
        def _():
            carry_re[...] = jnp.zeros_like(carry_re)
            carry_im[...] = jnp.zeros_like(carry_im)

        u = u_ref[...]
        _ssm_fill(u, bwre_ref, bwim_ref, sre, sim)
        zero = jnp.zeros((SSM_SUB, LANES), F32)
        for k0 in range(0, N_SLAB, SCAN_GROUP):
            ends = _scan(sre, sim, are_ref, aim_ref, k0, [(zero, zero)] * SCAN_GROUP, reverse=False, store=False)
            for kk in range(SCAN_GROUP):
                ends_re[k0 + kk] = ends[kk][0]
                ends_im[k0 + kk] = ends[kk][1]
            for kk in range(SCAN_GROUP):
                _ssm_seeds(ends_re, ends_im, a64re_ref, a64im_ref, carry_re, carry_im, seed_re, seed_im,
                           k0 + kk, reverse=False)
            init = [(seed_re[k0 + kk], seed_im[k0 + kk]) for kk in range(SCAN_GROUP)]
            _scan(sre, sim, are_ref, aim_ref, k0, init, reverse=False, store=True)
        sdre_ref[...] = seed_re[...]
        sdim_ref[...] = seed_im[...]
        for w in range(N_PAIR // PAIRS_PER_WIN):
            acc = jnp.zeros((SSM_TB, SSM_WIN), F32)
            for kk in range(PAIRS_PER_WIN):
                k2 = w * PAIRS_PER_WIN + kk
                acc += _dot(_pair_rows(sre, k2), cwre_ref[k2])
                acc -= _dot(_pair_rows(sim, k2), cwim_ref[k2])
            cols = _win_cols(w * PAIRS_PER_WIN)
            ypre = acc + d_ref[:, cols] * u[:, cols].astype(F32)
            ypre_ref[:, cols] = ypre
            yact_ref[:, cols] = _gelu(ypre).astype(BF16)

    c2, c3 = _ssm_specs_consts()
    seed_spec = pl.BlockSpec((None, N_SLAB, SSM_SUB, LANES), lambda b: (b, 0, 0, 0))
    small = pltpu.VMEM((N_SLAB, LANES), F32)
    tile = pltpu.VMEM((N_SLAB, SSM_SUB, LANES), F32)
    return _call(
        body, name="ssm_fwd", grid=(nb,),
        in_specs=[pl.BlockSpec((SSM_TB, SSM_WIDTH), lambda b: (b, ucol)), c2, c2, c2, c2, c3, c3, c3, c3,
                  pl.BlockSpec((1, SSM_WIDTH), lambda b: (0, 0))],
        out_specs=[pl.BlockSpec((SSM_TB, SSM_WIDTH), lambda b: (b, 0)),
                   pl.BlockSpec((SSM_TB, SSM_WIDTH), lambda b: (b, 0)), seed_spec, seed_spec],
        out_shape=[_sds((L, SSM_WIDTH), F32), _sds((L, SSM_WIDTH), BF16),
                   _sds((nb, N_SLAB, SSM_SUB, LANES), F32), _sds((nb, N_SLAB, SSM_SUB, LANES), F32)],
        scratch=_ssm_scratch() + [small, small, tile, tile, tile, tile],
        sem=("arbitrary",), rider=rider)(z, pk["a_re"], pk["a_im"], pk["a64_re"], pk["a64_im"],
                                         pk["bw_re"].astype(BF16), pk["bw_im"].astype(BF16),
                                         pk["cw_re"].astype(BF16), pk["cw_im"].astype(BF16), dskip)


def _combine_weights(l0, l1, l2):
    m = jnp.maximum(jnp.maximum(l0, l1), l2)
    e0, e1, e2 = jnp.exp(l0 - m), jnp.exp(l1 - m), jnp.exp(l2 - m)
    inv = 1.0 / (e0 + e1 + e2)
    return e0 * inv, e1 * inv, e2 * inv


def _mix_fwd(x, z, a, yact, w_ap, w_ga, w_gb, w_out):
    L = x.shape[0]
    cs = D_MODEL // N_CHIPS
    ga_col = (3 * N_GROUPS * ATTN_WIDTH + SSM_WIDTH) // D_MODEL

    def body(x_ref, ga_ref, gs_ref, a_ref, y_ref, wap_ref, wga_ref, wgb_ref, wout_ref,
             x1_ref, aout_ref, sa_ref, sb_ref, mix_ref):
        a = a_ref[...]
        y = y_ref[...]
        for s in range(N_CHIPS):
            cols = slice(s * cs, (s + 1) * cs)
            aout_ref[:, cols] = _dot(a, wap_ref[s]).astype(BF16)
            sa_ref[:, cols] = _dot(y, wga_ref[s]).astype(BF16)
            sb_ref[:, cols] = _dot(y, wgb_ref[s]).astype(BF16)
        s_out = sa_ref[...].astype(F32) * _sigmoid(sb_ref[...].astype(F32))
        mix = (_sigmoid(ga_ref[...].astype(F32)) * aout_ref[...].astype(F32)
               + _sigmoid(gs_ref[...].astype(F32)) * s_out).astype(BF16)
        mix_ref[...] = mix
        x1_ref[...] = x_ref[...] + _dot(mix, wout_ref[...])

    tok = lambda w: pl.BlockSpec((TM_MIX, w), lambda i: (i, 0))
    wsm = pl.BlockSpec((N_CHIPS, ATTN_WIDTH, cs), lambda i: (0, 0, 0))
    return _call(
        body, name="mix_fwd", grid=(L // TM_MIX,),
        in_specs=[tok(D_MODEL), pl.BlockSpec((TM_MIX, D_MODEL), lambda i: (i, ga_col)),
                  pl.BlockSpec((TM_MIX, D_MODEL), lambda i: (i, ga_col + 1))]
                 + [tok(ATTN_WIDTH)] * 2 + [wsm, wsm, wsm, pl.BlockSpec((D_MODEL, D_MODEL), lambda i: (0, 0))],
        out_specs=[tok(D_MODEL), tok(D_MODEL), tok(D_MODEL), tok(D_MODEL), tok(D_MODEL)],
        out_shape=[_sds((L, D_MODEL), F32)] + [_sds((L, D_MODEL), BF16)] * 4,
        sem=("parallel",))(x, z, z, a, yact, w_ap, w_ga, w_gb, w_out.reshape(D_MODEL, D_MODEL))


def _ffn_fwd(x1, g, w_g, w_u, w_d, rider=None):
    L = x1.shape[0]
    fs = D_FF // N_CHIPS
    TM = TM_PROJ

    def body(x_ref, g_ref, wg_ref, wu_ref, wd_ref, x2_ref, h_ref, gate_ref, up_ref, act_ref, acc):
        s = pl.program_id(1)

        @pl.when(s == 0)
        def _():
            xv = x_ref[...]
            r = lax.rsqrt(jnp.mean(xv * xv, axis=-1, keepdims=True) + EPS)
            h_ref[...] = (xv * r * g_ref[...]).astype(BF16)
            acc[...] = jnp.zeros_like(acc)

        h = h_ref[...]
        gate = _dot(h, wg_ref[...])
        up = _dot(h, wu_ref[...])
        act = (gate * _sigmoid(gate) * up).astype(BF16)
        gate_ref[...] = gate.astype(BF16)
        up_ref[...] = up.astype(BF16)
        act_ref[...] = act
        acc[...] += _dot(act, wd_ref[...])

        @pl.when(s == N_CHIPS - 1)
        def _():
            x2_ref[...] = x_ref[...] + acc[...]

    tok = pl.BlockSpec((TM, D_MODEL), lambda i, s: (i, 0))
    ffs = pl.BlockSpec((None, TM, fs), lambda i, s: (s, i, 0))
    return _call(
        body, name="ffn_fwd", grid=(L // TM, N_CHIPS),
        in_specs=[tok, pl.BlockSpec((1, D_MODEL), lambda i, s: (0, 0)),
                  pl.BlockSpec((None, D_MODEL, fs), lambda i, s: (s, 0, 0)),
                  pl.BlockSpec((None, D_MODEL, fs), lambda i, s: (s, 0, 0)),
                  pl.BlockSpec((None, fs, D_MODEL), lambda i, s: (s, 0, 0))],
        out_specs=[tok, tok, ffs, ffs, ffs],
        out_shape=[_sds((L, D_MODEL), F32), _sds((L, D_MODEL), BF16)] + [_sds((N_CHIPS, L, fs), BF16)] * 3,
        scratch=[pltpu.VMEM((TM, D_MODEL), F32)],
        sem=("parallel", "arbitrary"), rider=rider)(x1, g, w_g, w_u, w_d)


def _loss_head(xl, target):
    L = xl.shape[0]

    def body(x_ref, t_ref, dx_ref, loss_ref, acc):
        i = pl.program_id(0)

        @pl.when(i == 0)
        def _():
            acc[...] = jnp.zeros_like(acc)

        e = x_ref[...] - t_ref[...]
        dx_ref[...] = e * (1.0 / D_MODEL)
        acc[...] += jnp.sum((e * e).reshape(TM // SUBLANES, SUBLANES, D_MODEL), axis=0)

        @pl.when(i == pl.num_programs(0) - 1)
        def _():
            loss_ref[...] = (0.5 / D_MODEL) * jnp.sum(acc[...]).reshape(1, 1)

    tok = pl.BlockSpec((TM, D_MODEL), lambda i: (i, 0))
    return _call(
        body, name="loss_head", grid=(L // TM,), in_specs=[tok, tok],
        out_specs=[tok, pl.BlockSpec((1, 1), lambda i: (0, 0))],
        out_shape=[_sds((L, D_MODEL), F32), _sds((1, 1), F32)],
        scratch=[pltpu.VMEM((SUBLANES, D_MODEL), F32)], sem=("arbitrary",))(xl, target)


def _ssm_pack(lam_re, lam_im, log_dt, b_re, b_im, c_re, c_im):
    dt = jnp.exp(log_dt)[:, None]
    mag = jnp.exp(lam_re * dt)
    ang = lam_im * dt
    ar = mag * jnp.cos(ang)
    ai = mag * jnp.sin(ang)
    nr = ar - 1.0
    ni = ai
    den = lam_re * lam_re + lam_im * lam_im
    cr = ((nr * lam_re + ni * lam_im) / den)[..., None]
    ci = ((ni * lam_re - nr * lam_im) / den)[..., None]
    bbr = cr * b_re - ci * b_im
    bbi = cr * b_im + ci * b_re
    gpp = SSM_WIN // SSM_STATE
    gpw = SSM_WIN // SSM_GROUP
    k2 = jnp.arange(N_PAIR)[:, None, None]
    gs = jnp.arange(gpp)[None, :, None]
    gl = jnp.arange(gpw)[None, None, :]
    same = (gl == gpp * (k2 % PAIRS_PER_WIN) + gs).astype(F32)

    def b_windows(bb):
        return jnp.einsum('kgl,kgpc->klcgp', same, bb.reshape(N_PAIR, gpp, SSM_STATE, SSM_GROUP)).reshape(
            N_PAIR, SSM_WIN, SSM_WIN)

    def c_windows(cc):
        return jnp.einsum('kgl,kgcp->kgplc', same, cc.reshape(N_PAIR, gpp, SSM_GROUP, SSM_STATE)).reshape(
            N_PAIR, SSM_WIN, SSM_WIN)

    pr, pi = ar, ai
    for _ in range(int(math.log2(SSM_TC))):
        pr, pi = pr * pr - pi * pi, 2.0 * pr * pi
    return dict(a_re=ar.reshape(N_SLAB, LANES), a_im=ai.reshape(N_SLAB, LANES),
                a64_re=pr.reshape(N_SLAB, LANES), a64_im=pi.reshape(N_SLAB, LANES),
                bw_re=b_windows(bbr), bw_im=b_windows(bbi), cw_re=c_windows(c_re), cw_im=c_windows(c_im))


def _layer_fwd(x, p, rest, rest_stage, next_bufs=None):
    first = {"ici": _gather_ici_rider, "d2d": _gather_d2d_rider}[rest_stage]
    outs = _in_proj_fwd(x, p["g_mix"], p["w_in"], first(rest))
    (z, h), rest = outs[:2], list(outs[2:])
    qkv, got = _qkv_prep(z, jnp.tile(p["g_q"], (1, N_HEADS)), jnp.tile(p["g_k"], (1, N_HEADS)),
                         _gather_d2d_rider(rest) if rest_stage == "ici" else None)
    p = {**p, **dict(zip(BIG[1:], got if rest_stage == "ici" else rest))}
    os_, ls_ = [], []
    for gi in range(N_GROUPS):
        o, l = _attn_fwd(*qkv[gi], gi)
        os_.append(o)
        ls_.append(l)
    a = _combine_fwd(os_, ls_)
    pk = _ssm_pack(p["lambda_re"], p["lambda_im"], p["log_dt"], p["b_re"], p["b_im"], p["c_re"], p["c_im"])
    outs = _ssm_fwd(z, pk, p["d_skip"], _gather_ici_rider(next_bufs[:1]) if next_bufs else None)
    (ypre, yact, sd_re, sd_im), next_in = outs[:4], list(outs[4:])
    x1, aout, sa, sb, mix = _mix_fwd(x, z, a, yact, p["w_attn_proj"], p["w_glu_a"], p["w_glu_b"], p["w_out"])
    outs = _ffn_fwd(x1, p["g_ffn"], p["w_ffn_gate"], p["w_ffn_up"], p["w_ffn_down"],
                    _join_riders(_gather_ici_rider(next_bufs[1:]), _gather_d2d_rider(next_in)) if next_bufs else None)
    x2, h2, gate, up, act = outs[:5]
    nxt = (outs[-1], list(outs[5:-1])) if next_bufs else None
    saved = dict(x=x, z=z, h=h, qkv=qkv, os=os_, ls=ls_, pk=pk, ypre=ypre, yact=yact, sd_re=sd_re, sd_im=sd_im,
                 x1=x1, a=a, aout=aout, sa=sa, sb=sb, mix=mix, h2=h2, gate=gate, up=up, act=act)
    return x2, saved, p, nxt


def _rms_bwd(xv, g, dh):
    r = lax.rsqrt(jnp.mean(xv * xv, axis=-1, keepdims=True) + EPS)
    xn = xv * r
    dxn = dh * g
    dx = r * (dxn - xn * jnp.mean(dxn * xn, axis=-1, keepdims=True))
    dg = jnp.sum((dh * xn).reshape(xv.shape[0] // SUBLANES, SUBLANES, xv.shape[1]), axis=0)
    return dx, dg


def _ffn_bwd_act(dx2, gate, up, w_d):
    L = dx2.shape[0]
    fs = D_FF // N_CHIPS
    TM = TM_PROJ

    def body(dx_ref, gate_ref, up_ref, wd_ref, dgate_ref, dup_ref):
        dact = _dot_nt(dx_ref[...].astype(BF16), wd_ref[...])
        gt = gate_ref[...].astype(F32)
        sg = _sigmoid(gt)
        dgate_ref[...] = (dact * up_ref[...].astype(F32) * (sg * (1.0 + gt * (1.0 - sg)))).astype(BF16)
        dup_ref[...] = (dact * gt * sg).astype(BF16)

    ffs = pl.BlockSpec((None, TM, fs), lambda i, s: (s, i, 0))
    return _call(
        body, name="ffn_bwd_act", grid=(L // TM, N_CHIPS),
        in_specs=[pl.BlockSpec((TM, D_MODEL), lambda i, s: (i, 0)), ffs, ffs,
                  pl.BlockSpec((None, fs, D_MODEL), lambda i, s: (s, 0, 0))],
        out_specs=[ffs, ffs], out_shape=[_sds((N_CHIPS, L, fs), BF16)] * 2,
        sem=("parallel", "parallel"))(dx2, gate, up, w_d)


def _ffn_bwd_in(dx2, x1, g, dgate, dup, w_g, w_u, rider=None):
    L = x1.shape[0]
    fs = D_FF // N_CHIPS
    TM = TM_PROJ

    def body(dx_ref, x_ref, g_ref, dgate_ref, dup_ref, wg_ref, wu_ref, dx1_ref, dg_ref, acc, dgacc):
        i, s = pl.program_id(0), pl.program_id(1)

        @pl.when(s == 0)
        def _():
            acc[...] = jnp.zeros_like(acc)

        @pl.when(jnp.logical_and(i == 0, s == 0))
        def _():
            dgacc[...] = jnp.zeros_like(dgacc)

        acc[...] += _dot_nt(dgate_ref[...], wg_ref[...]) + _dot_nt(dup_ref[...], wu_ref[...])

        @pl.when(s == N_CHIPS - 1)
        def _():
            dx, dg = _rms_bwd(x_ref[...], g_ref[...], acc[...])
            dx1_ref[...] = dx_ref[...] + dx
            dgacc[...] += dg

        @pl.when(jnp.logical_and(i == pl.num_programs(0) - 1, s == N_CHIPS - 1))
        def _():
            dg_ref[...] = jnp.sum(dgacc[...], axis=0, keepdims=True)

    tok = pl.BlockSpec((TM, D_MODEL), lambda i, s: (i, 0))
    ffs = pl.BlockSpec((None, TM, fs), lambda i, s: (s, i, 0))
    vec = pl.BlockSpec((1, D_MODEL), lambda i, s: (0, 0))
    return _call(
        body, name="ffn_bwd_in", grid=(L // TM, N_CHIPS),
        in_specs=[tok, tok, vec, ffs, ffs,
                  pl.BlockSpec((None, D_MODEL, fs), lambda i, s: (s, 0, 0)),
                  pl.BlockSpec((None, D_MODEL, fs), lambda i, s: (s, 0, 0))],
        out_specs=[tok, vec],
        out_shape=[_sds((L, D_MODEL), F32), _sds((1, D_MODEL), F32)],
        scratch=[pltpu.VMEM((TM, D_MODEL), F32), pltpu.VMEM((SUBLANES, D_MODEL), F32)],
        sem=("arbitrary", "arbitrary"), rider=rider)(dx2, x1, g, dgate, dup, w_g, w_u)


def _wgrad(a, b, *, name, grid_kn, a_spec, b_spec, out_shape, out_spec):
    L = a.shape[-2]
    nl = L // TL_WGRAD

    def body(a_ref, b_ref, o_ref):
        @pl.when(pl.program_id(2) == 0)
        def _():
            o_ref[...] = jnp.zeros_like(o_ref)
        o_ref[...] += _dot_tn(a_ref[...].astype(BF16), b_ref[...].astype(BF16))

    return _call(body, name=name, grid=(*grid_kn, nl), in_specs=[a_spec, b_spec], out_specs=out_spec,
                 out_shape=out_shape, sem=("parallel", "parallel", "arbitrary"))(a, b)


def _wgrad_cols(a, b, name):
    K, N = a.shape[1], b.shape[1]
    ns = N // N_CHIPS
    if N * K * 4 <= 4 * 1024 * 1024:
        L = a.shape[0]

        def body(a_ref, b_ref, o_ref):
            @pl.when(pl.program_id(0) == 0)
            def _():
                o_ref[...] = jnp.zeros_like(o_ref)
            av = a_ref[...].astype(BF16)
            for s in range(N_CHIPS):
                o_ref[s] += _dot_tn(av, b_ref[:, s * ns:(s + 1) * ns].astype(BF16))

        return _call(body, name=name, grid=(L // TL_WGRAD,),
                     in_specs=[pl.BlockSpec((TL_WGRAD, K), lambda t: (t, 0)),
                               pl.BlockSpec((TL_WGRAD, N), lambda t: (t, 0))],
                     out_specs=pl.BlockSpec((N_CHIPS, K, ns), lambda t: (0, 0, 0)),
                     out_shape=_sds((N_CHIPS, K, ns), F32), sem=("arbitrary",))(a, b)
    tn = ns // 2 if ns % (2 * LANES) == 0 else ns
    nj = ns // tn
    return _wgrad(a, b, name=name, grid_kn=(1, N_CHIPS * nj),
                  a_spec=pl.BlockSpec((TL_WGRAD, K), lambda i, j, t: (t, 0)),
                  b_spec=pl.BlockSpec((TL_WGRAD, tn), lambda i, j, t: (t, j)),
                  out_shape=_sds((N_CHIPS, K, ns), F32),
                  out_spec=pl.BlockSpec((None, K, tn), lambda i, j, t: (j // nj, 0, j % nj)))


def _wgrad_full(a, b, name):
    K, N = a.shape[1], b.shape[1]
    return _wgrad(a, b, name=name, grid_kn=(1, 1),
                  a_spec=pl.BlockSpec((TL_WGRAD, K), lambda i, j, t: (t, 0)),
                  b_spec=pl.BlockSpec((TL_WGRAD, N), lambda i, j, t: (t, 0)),
                  out_shape=_sds((K, N), F32), out_spec=pl.BlockSpec((K, N), lambda i, j, t: (0, 0)))


def _wgrad_ff_cols(a, b, name):
    K, fs = a.shape[1], b.shape[2]
    return _wgrad(a, b, name=name, grid_kn=(1, N_CHIPS),
                  a_spec=pl.BlockSpec((TL_WGRAD, K), lambda i, j, t: (t, 0)),
                  b_spec=pl.BlockSpec((None, TL_WGRAD, fs), lambda i, j, t: (j, t, 0)),
                  out_shape=_sds((N_CHIPS, K, fs), F32),
                  out_spec=pl.BlockSpec((None, K, fs), lambda i, j, t: (j, 0, 0)))


def _wgrad_ff_rows(a, b, name):
    fs, N = a.shape[2], b.shape[1]
    return _wgrad(a, b, name=name, grid_kn=(N_CHIPS, 1),
                  a_spec=pl.BlockSpec((None, TL_WGRAD, fs), lambda i, j, t: (i, t, 0)),
                  b_spec=pl.BlockSpec((TL_WGRAD, N), lambda i, j, t: (t, 0)),
                  out_shape=_sds((N_CHIPS, fs, N), F32),
                  out_spec=pl.BlockSpec((None, fs, N), lambda i, j, t: (i, 0, 0)))


def _mix_bwd(dx, z, aout, sa, sb, ypre, w_ap, w_ga, w_gb, w_out, rider=None):
    L = dx.shape[0]
    cs = D_MODEL // N_CHIPS
    ga_col = (3 * N_GROUPS * ATTN_WIDTH + SSM_WIDTH) // D_MODEL

    def body(dx_ref, ga_ref, gs_ref, aout_ref, sa_ref, sb_ref, ypre_ref, wap_ref, wga_ref, wgb_ref, wout_ref,
             dgates_ref, da_ref, gy_ref, daout_ref, dsa_ref, dsb_ref):
        dmix = _dot_nt(dx_ref[...].astype(BF16), wout_ref[...])
        sig_a = _sigmoid(ga_ref[...].astype(F32))
        sig_s = _sigmoid(gs_ref[...].astype(F32))
        a_out = aout_ref[...].astype(F32)
        s_a = sa_ref[...].astype(F32)
        sig_b = _sigmoid(sb_ref[...].astype(F32))
        s_out = s_a * sig_b
        daout = (dmix * sig_a).astype(BF16)
        daout_ref[...] = daout
        dgates_ref[:, :D_MODEL] = (dmix * a_out * sig_a * (1.0 - sig_a)).astype(BF16)
        dgates_ref[:, D_MODEL:] = (dmix * s_out * sig_s * (1.0 - sig_s)).astype(BF16)
        ds_out = dmix * sig_s
        dsa = (ds_out * sig_b).astype(BF16)
        dsb = (ds_out * s_a * sig_b * (1.0 - sig_b)).astype(BF16)
        dsa_ref[...] = dsa
        dsb_ref[...] = dsb
        da = jnp.zeros((TM_MIX, ATTN_WIDTH), F32)
        dy = jnp.zeros((TM_MIX, SSM_WIDTH), F32)
        for s in range(N_CHIPS):
            cols = slice(s * cs, (s + 1) * cs)
            da += _dot_nt(daout[:, cols], wap_ref[s])
            dy += _dot_nt(dsa[:, cols], wga_ref[s]) + _dot_nt(dsb[:, cols], wgb_ref[s])
        gy_ref[...] = dy * _gelu_grad(ypre_ref[...])
        da_ref[...] = da

    tok = lambda w: pl.BlockSpec((TM_MIX, w), lambda i: (i, 0))
    wsm = pl.BlockSpec((N_CHIPS, ATTN_WIDTH, cs), lambda i: (0, 0, 0))
    return _call(
        body, name="mix_bwd", grid=(L // TM_MIX,),
        in_specs=[tok(D_MODEL), pl.BlockSpec((TM_MIX, D_MODEL), lambda i: (i, ga_col)),
                  pl.BlockSpec((TM_MIX, D_MODEL), lambda i: (i, ga_col + 1)),
                  tok(D_MODEL), tok(D_MODEL), tok(D_MODEL), tok(SSM_WIDTH),
                  wsm, wsm, wsm, pl.BlockSpec((D_MODEL, D_MODEL), lambda i: (0, 0))],
        out_specs=[tok(2 * D_MODEL), tok(ATTN_WIDTH), tok(SSM_WIDTH)] + [tok(D_MODEL)] * 3,
        out_shape=[_sds((L, 2 * D_MODEL), BF16), _sds((L, ATTN_WIDTH), F32), _sds((L, SSM_WIDTH), F32)]
                  + [_sds((L, D_MODEL), BF16)] * 3,
        sem=("parallel",), rider=rider)(dx, z, z, aout, sa, sb, ypre, w_ap, w_ga, w_gb,
                                        w_out.reshape(D_MODEL, D_MODEL))


def _combine_bwd(da, os_, ls_):
    L = da.shape[0]

    def body(da_ref, o0, o1, o2, l0, l1, l2, hs_ref, p1_ref, p2_ref, pt1_ref, pt2_ref,
             do0, do1, do2, c0, c1, c2):
        o_tok, l_tok = _to_token_order((o0[...], o1[...], o2[...]), (l0[...], l1[...], l2[...]),
                                       (None, pt1_ref[...], pt2_ref[...]))
        w = _combine_weights(*l_tok)
        dav = da_ref[...]
        hs = hs_ref[...]
        tbar = sum(wg * _head_sum(dav * og, hs) for wg, og in zip(w, o_tok))
        for wg, pm, do_ref, c_ref in zip(w, (None, p1_ref[...], p2_ref[...]), (do0, do1, do2), (c0, c1, c2)):
            dog = (wg * dav).astype(BF16)
            cg = -wg * tbar
            do_ref[...] = dog if pm is None else _dot(pm, dog).astype(BF16)
            c_ref[...] = cg if pm is None else _dot(pm, cg.astype(BF16))

    tile = pl.BlockSpec((DL_TILE, ATTN_WIDTH), lambda i: (i, 0))
    mat = pl.BlockSpec((DL_TILE, DL_TILE), lambda i: (0, 0))
    p1, p2 = _perm_matrix(ATTN_PATTERN[1][1]), _perm_matrix(ATTN_PATTERN[2][1])
    outs = _call(body, name="combine_bwd", grid=(L // DL_TILE,), in_specs=[tile] * 7 + [mat] * 5,
                 out_specs=[tile] * 6,
                 out_shape=[_sds((L, ATTN_WIDTH), BF16)] * 3 + [_sds((L, ATTN_WIDTH), F32)] * 3,
                 sem=("parallel",))(da, *os_, *ls_, _head_sum_matrix(), p1, p2, p1.T, p2.T)
    return outs[:3], outs[3:]


def _attn_bwd(qs, ks, v, do, l, c, gi, rider=None):
    L = qs.shape[0]
    _, d = ATTN_PATTERN[gi]
    nb = L // (d * BLK)

    def body(q0_ref, q1_ref, k_ref, v_ref, do0_ref, do1_ref, l0_ref, l1_ref, c0_ref, c1_ref,
             dq_ref, dk_ref, dv_ref, carry):
        n = pl.program_id(1)

        @pl.when(n == 0)
        def _():
            carry[...] = jnp.zeros_like(carry)

        qi = lax.broadcasted_iota(jnp.int32, (2 * BLK, BLK), 0)
        kj = lax.broadcasted_iota(jnp.int32, (2 * BLK, BLK), 1)
        first = qi < BLK
        mask = jnp.logical_and(jnp.where(first, qi, kj) >= jnp.where(first, kj, qi - BLK),
                               qi < jnp.where(n < nb - 1, 2 * BLK, BLK))
        q2 = jnp.concatenate([_dl_read(q0_ref), _dl_read(q1_ref)], axis=0)
        do2 = jnp.concatenate([_dl_read(do0_ref), _dl_read(do1_ref)], axis=0)
        l2 = jnp.concatenate([_dl_read(l0_ref), _dl_read(l1_ref)], axis=0)
        c2 = jnp.concatenate([_dl_read(c0_ref), _dl_read(c1_ref)], axis=0)
        k = _dl_read(k_ref)
        v_ = _dl_read(v_ref)
        h0, h1 = _pair_masks()
        mask2 = jnp.concatenate([mask, mask], axis=1)
        dq_parts, dk_parts, dv_parts = [], [], []
        for hp in range(N_HEADS // 2):
            ls = slice(hp * LANES, (hp + 1) * LANES)
            qp, dop, kp_, vp_ = q2[:, ls], do2[:, ls], k[:, ls], v_[:, ls]
            kk = jnp.concatenate([jnp.where(h0, kp_, 0), jnp.where(h1, kp_, 0)], axis=0)
            vv = jnp.concatenate([jnp.where(h0, vp_, 0), jnp.where(h1, vp_, 0)], axis=0)

            def per_head(t):
                a = jnp.broadcast_to(t[:, hp * LANES:hp * LANES + 1], (2 * BLK, BLK))
                b = jnp.broadcast_to(t[:, hp * LANES + HEAD_DIM:hp * LANES + HEAD_DIM + 1], (2 * BLK, BLK))
                return jnp.concatenate([a, b], axis=1)

            p = jnp.where(mask2, jnp.exp(_dot_nt(qp, kk) - per_head(l2)), 0.0)
            ds = (p * (_dot_nt(dop, vv) + per_head(c2))).astype(BF16)
            dv2 = _dot_tn(p.astype(BF16), dop)
            dk2 = _dot_tn(ds, qp)
            dq2 = _dot(ds, kk)
            dq_parts.append((dq2[:BLK] + carry[:, ls]).astype(BF16))
            carry[:, ls] = dq2[BLK:]
            dk_parts.append(jnp.where(h0, dk2[:BLK], dk2[BLK:]).astype(BF16))
            dv_parts.append(jnp.where(h0, dv2[:BLK], dv2[BLK:]).astype(BF16))
        _dl_write(dq_ref, jnp.concatenate(dq_parts, axis=1))
        _dl_write(dk_ref, jnp.concatenate(dk_parts, axis=1))
        _dl_write(dv_ref, jnp.concatenate(dv_parts, axis=1))

    cur = _dl_spec(d, ATTN_WIDTH, lambda n: n)
    nxt = _dl_spec(d, ATTN_WIDTH, lambda n: jnp.minimum(n + 1, nb - 1))
    view = lambda t: _dl_view(t, d)
    outs = _call(
        body, name=f"attn_bwd_g{gi}", grid=(d, nb),
        in_specs=[cur, nxt, cur, cur, cur, nxt, cur, nxt, cur, nxt], out_specs=[cur, cur, cur],
        out_shape=[_sds(view(qs).shape, BF16)] * 3, scratch=[pltpu.VMEM((BLK, ATTN_WIDTH), F32)],
        sem=("parallel", "arbitrary"), rider=rider)(view(qs), view(qs), view(ks), view(v), view(do), view(do),
                                                    view(l), view(l), view(c), view(c))
    return [t.reshape(L, ATTN_WIDTH) for t in outs[:3]], list(outs[3:])


def _qkv_post(z, dqkv, du, dgates, gq_t, gk_t):
    L = z.shape[0]
    qkv_w = N_GROUPS * ATTN_WIDTH

    def body(zq_ref, zk_ref, gq_ref, gk_ref, hs_ref, pt1_ref, pt2_ref, du_ref, dgates_ref, *rest):
        dl_refs, (dz_ref, dgq_ref, dgk_ref) = rest[:9], rest[9:]

        @pl.when(pl.program_id(0) == 0)
        def _():
            dgq_ref[...] = jnp.zeros_like(dgq_ref)
            dgk_ref[...] = jnp.zeros_like(dgk_ref)

        hs = hs_ref[...]
        pts = (None, pt1_ref[...], pt2_ref[...])

        def rows8(t):
            return jnp.sum(t.reshape(DL_TILE // SUBLANES, SUBLANES, ATTN_WIDTH), axis=0)

        def norm_bwd(x, gain, dn):
            r = lax.rsqrt(_head_sum(x * x, hs) * (1.0 / HEAD_DIM) + EPS)
            xh = x * r
            dh = dn * gain
            return r * (dh - xh * (_head_sum(dh * xh, hs) * (1.0 / HEAD_DIM))), rows8(dn * xh)

        for g in range(N_GROUPS):
            tok = [t[...].astype(F32) if pts[g] is None else _dot(pts[g], t[...]) for t in dl_refs[3 * g:3 * g + 3]]
            cols = slice(g * ATTN_WIDTH, (g + 1) * ATTN_WIDTH)
            dq, pq = norm_bwd(zq_ref[:, cols].astype(F32), gq_ref[...] * SCALE, tok[0])
            dk, pk_ = norm_bwd(zk_ref[:, cols].astype(F32), gk_ref[...], tok[1])
            dgq_ref[...] += pq * SCALE
            dgk_ref[...] += pk_
            dz_ref[:, cols] = dq.astype(BF16)
            dz_ref[:, qkv_w + g * ATTN_WIDTH:qkv_w + (g + 1) * ATTN_WIDTH] = dk.astype(BF16)
            dz_ref[:, 2 * qkv_w + g * ATTN_WIDTH:2 * qkv_w + (g + 1) * ATTN_WIDTH] = tok[2].astype(BF16)
        dz_ref[:, 3 * qkv_w:3 * qkv_w + SSM_WIDTH] = du_ref[...]
        dz_ref[:, 3 * qkv_w + SSM_WIDTH:] = dgates_ref[...]

    tile = lambda w: pl.BlockSpec((DL_TILE, w), lambda i: (i, 0))
    mat = pl.BlockSpec((DL_TILE, DL_TILE), lambda i: (0, 0))
    vec = pl.BlockSpec((1, ATTN_WIDTH), lambda i: (0, 0))
    acc = pl.BlockSpec((SUBLANES, ATTN_WIDTH), lambda i: (0, 0))
    flat = [t for grp in dqkv for t in grp]
    return _call(
        body, name="qkv_post", grid=(L // DL_TILE,),
        in_specs=[tile(qkv_w), pl.BlockSpec((DL_TILE, qkv_w), lambda i: (i, 1)), vec, vec, mat, mat, mat,
                  tile(SSM_WIDTH), tile(2 * D_MODEL)] + [tile(ATTN_WIDTH)] * 9,
        out_specs=[tile(IN_COLS), acc, acc],
        out_shape=[_sds((L, IN_COLS), BF16), _sds((SUBLANES, ATTN_WIDTH), F32), _sds((SUBLANES, ATTN_WIDTH), F32)],
        sem=("arbitrary",))(z, z, gq_t, gk_t, _head_sum_matrix(), _perm_matrix(ATTN_PATTERN[1][1]).T,
                            _perm_matrix(ATTN_PATTERN[2][1]).T, du, dgates, *flat)


def _scan_rev_grad(sre, sim, rre, rim, are_ref, aim_ref, k0, init, seed_re, seed_im):
    ar = [_bcast(are_ref, k0 + kk) for kk in range(SCAN_GROUP)]
    ai = [-_bcast(aim_ref, k0 + kk) for kk in range(SCAN_GROUP)]

    def update(i, xprev, carry):
        out = []
        for kk in range(SCAN_GROUP):
            k = k0 + kk
            lr, li, dr, di = carry[4 * kk:4 * kk + 4]
            nr = ar[kk] * lr - ai[kk] * li + rre[k, _rows(i), :]
            ni = ar[kk] * li + ai[kk] * lr + rim[k, _rows(i), :]
            rre[k, _rows(i), :] = nr
            rim[k, _rows(i), :] = ni
            xr, xi = xprev(k)
            out += [nr, ni, dr + xr * nr + xi * ni, di + xr * ni - xi * nr]
        return tuple(out)

    def step(t, carry):
        i = SSM_TC - 1 - t
        return update(i, lambda k: (sre[k, _rows(i - 1), :], sim[k, _rows(i - 1), :]), carry)

    zero = jnp.zeros((SSM_SUB, LANES), F32)
    flat = []
    for re, im in init:
        flat += [re, im, zero, zero]
    res = lax.fori_loop(0, SSM_TC - 1, step, tuple(flat), unroll=3)
    res = update(0, lambda k: (seed_re[k], seed_im[k]), res)
    return [(res[4 * kk + 2], res[4 * kk + 3]) for kk in range(SCAN_GROUP)]


def _ssm_bwd(z, gy, pk, dskip, sd_re, sd_im, rider=None):
    L = z.shape[0]
    nb = L // SSM_TB
    ucol = (3 * N_GROUPS * ATTN_WIDTH) // SSM_WIDTH
    nwin = N_PAIR // PAIRS_PER_WIN

    def body(u_ref, gy_ref, are_ref, aim_ref, a64re_ref, a64im_ref, bwre_ref, bwim_ref, cwre_ref, cwim_ref, d_ref,
             sdre_ref, sdim_ref,
             du_ref, dare_ref, daim_ref, dbre_ref, dbim_ref, dcre_ref, dcim_ref, dd_ref,
             sre, sim, rre, rim, carry_re, carry_im, ends_re, ends_im, seed_re, seed_im):
        @pl.when(pl.program_id(0) == 0)
        def _():
            carry_re[...] = jnp.zeros_like(carry_re)
            carry_im[...] = jnp.zeros_like(carry_im)
            for ref in (dare_ref, daim_ref, dbre_ref, dbim_ref, dcre_ref, dcim_ref, dd_ref):
                ref[...] = jnp.zeros_like(ref)

        u = u_ref[...]
        gyv = gy_ref[...]
        gyb = gyv.astype(BF16)
        _ssm_fill(u, bwre_ref, bwim_ref, sre, sim)
        for k2 in range(N_PAIR):
            gw = gyb[:, _win_cols(k2)]
            _to_slabs(rre, k2, _dot_nt(gw, cwre_ref[k2]))
            _to_slabs(rim, k2, -_dot_nt(gw, cwim_ref[k2]))
        zero = jnp.zeros((SSM_SUB, LANES), F32)
        for k0 in range(0, N_SLAB, SCAN_GROUP):
            grp = range(k0, k0 + SCAN_GROUP)
            _scan(sre, sim, are_ref, aim_ref, k0, [(sdre_ref[k], sdim_ref[k]) for k in grp],
                  reverse=False, store=True)
            ends = _scan(rre, rim, are_ref, aim_ref, k0, [(zero, zero)] * SCAN_GROUP, reverse=True, store=False,
                         sign=-1.0)
            for kk, k in enumerate(grp):
                ends_re[k] = ends[kk][0]
                ends_im[k] = ends[kk][1]
            for k in grp:
                _ssm_seeds(ends_re, ends_im, a64re_ref, a64im_ref, carry_re, carry_im, seed_re, seed_im, k,
                           reverse=True, sign=-1.0)
            das = _scan_rev_grad(sre, sim, rre, rim, are_ref, aim_ref, k0,
                                 [(seed_re[k], seed_im[k]) for k in grp], sdre_ref, sdim_ref)
            for kk, k in enumerate(grp):
                dare_ref[k] += das[kk][0]
                daim_ref[k] += das[kk][1]
        for w in range(nwin):
            cols = _win_cols(w * PAIRS_PER_WIN)
            uw = u[:, cols]
            gw = gyb[:, cols]
            acc = gyv[:, cols] * d_ref[:, cols]
            for kk in range(PAIRS_PER_WIN):
                k2 = w * PAIRS_PER_WIN + kk
                lr = _pair_rows(rre, k2)
                li = _pair_rows(rim, k2)
                acc += _dot_nt(lr, bwre_ref[k2]) + _dot_nt(li, bwim_ref[k2])
                dbre_ref[k2] += _dot_tn(uw, lr)
                dbim_ref[k2] += _dot_tn(uw, li)
                dcre_ref[k2] += _dot_tn(_pair_rows(sre, k2), gw)
                dcim_ref[k2] -= _dot_tn(_pair_rows(sim, k2), gw)
            du_ref[:, cols] = acc.astype(BF16)
        dd_ref[...] += jnp.sum((gyv * u.astype(F32)).reshape(SSM_TB // SUBLANES, SUBLANES, SSM_WIDTH), axis=0)

    c2, c3 = _ssm_specs_consts()
    rev = lambda b: nb - 1 - b
    seed_spec = pl.BlockSpec((None, N_SLAB, SSM_SUB, LANES), lambda b: (rev(b), 0, 0, 0))
    tile_out = pl.BlockSpec((N_SLAB, SSM_SUB, LANES), lambda b: (0, 0, 0))
    small = pltpu.VMEM((N_SLAB, LANES), F32)
    tile = pltpu.VMEM((N_SLAB, SSM_SUB, LANES), F32)
    return _call(
        body, name="ssm_bwd", grid=(nb,),
        in_specs=[pl.BlockSpec((SSM_TB, SSM_WIDTH), lambda b: (rev(b), ucol)),
                  pl.BlockSpec((SSM_TB, SSM_WIDTH), lambda b: (rev(b), 0)),
                  c2, c2, c2, c2, c3, c3, c3, c3, pl.BlockSpec((1, SSM_WIDTH), lambda b: (0, 0)),
                  seed_spec, seed_spec],
        out_specs=[pl.BlockSpec((SSM_TB, SSM_WIDTH), lambda b: (rev(b), 0)), tile_out, tile_out, c3, c3, c3, c3,
                   pl.BlockSpec((SUBLANES, SSM_WIDTH), lambda b: (0, 0))],
        out_shape=[_sds((L, SSM_WIDTH), BF16), _sds((N_SLAB, SSM_SUB, LANES), F32),
                   _sds((N_SLAB, SSM_SUB, LANES), F32)] + [_sds((N_PAIR, SSM_WIN, SSM_WIN), F32)] * 4
                  + [_sds((SUBLANES, SSM_WIDTH), F32)],
        scratch=_ssm_scratch() + _ssm_scratch() + [small, small, tile, tile, tile, tile],
        sem=("arbitrary",), rider=rider)(z, gy, pk["a_re"], pk["a_im"], pk["a64_re"], pk["a64_im"],
                            pk["bw_re"].astype(BF16), pk["bw_im"].astype(BF16),
                            pk["cw_re"].astype(BF16), pk["cw_im"].astype(BF16), dskip, sd_re, sd_im)


def _in_proj_bwd(dz, w, x, g, dres, rider=None):
    L = x.shape[0]
    ns = w.shape[2]
    tn = ns
    nj = ns // tn
    nt = N_CHIPS * nj
    TM = TM_PROJ

    def body(dz_ref, w_ref, x_ref, g_ref, dres_ref, dx_ref, dg_ref, acc, dgacc):
        i, j = pl.program_id(0), pl.program_id(1)

        @pl.when(j == 0)
        def _():
            acc[...] = jnp.zeros_like(acc)

        @pl.when(jnp.logical_and(i == 0, j == 0))
        def _():
            dgacc[...] = jnp.zeros_like(dgacc)

        acc[...] += _dot_nt(dz_ref[...], w_ref[...])

        @pl.when(j == nt - 1)
        def _():
            dx, dg = _rms_bwd(x_ref[...], g_ref[...], acc[...])
            dx_ref[...] = dres_ref[...] + dx
            dgacc[...] += dg

        @pl.when(jnp.logical_and(i == pl.num_programs(0) - 1, j == nt - 1))
        def _():
            dg_ref[...] = jnp.sum(dgacc[...], axis=0, keepdims=True)

    tok = pl.BlockSpec((TM, D_MODEL), lambda i, j: (i, 0))
    vec = pl.BlockSpec((1, D_MODEL), lambda i, j: (0, 0))
    return _call(
        body, name="in_proj_bwd", grid=(L // TM, nt),
        in_specs=[pl.BlockSpec((TM, tn), lambda i, j: (i, j)),
                  pl.BlockSpec((None, D_MODEL, tn), lambda i, j: (j // nj, 0, j % nj)), tok, vec, tok],
        out_specs=[tok, vec],
        out_shape=[_sds((L, D_MODEL), F32), _sds((1, D_MODEL), F32)],
        scratch=[pltpu.VMEM((TM, D_MODEL), F32), pltpu.VMEM((SUBLANES, D_MODEL), F32)],
        sem=("arbitrary", "arbitrary"), rider=rider)(dz, w, x, g, dres)


SSM_PARAMS = ("lambda_re", "lambda_im", "log_dt", "b_re", "b_im", "c_re", "c_im")
EARLY = ("w_ffn_gate", "w_ffn_up", "w_ffn_down")
LATE = ("w_in", "w_attn_proj", "w_glu_a", "w_glu_b", "w_out")


def _layer_bwd(dx2, sv, p, pending, owned, l, idx):
    chip_idx, c_idx = idx
    g = {}
    owned = dict(owned)

    def settle(name, partial, arrived, layer):
        owned[name] = _sum_owner(partial, arrived, owned[name], layer, chip_idx, c_idx)

    dgate, dup = _ffn_bwd_act(dx2, sv["gate"], sv["up"], p["w_ffn_down"])
    outs = _ffn_bwd_in(dx2, sv["x1"], p["g_ffn"], dgate, dup, p["w_ffn_gate"], p["w_ffn_up"],
                       _scatter_rider([pending[n] for n in LATE[1:]]) if pending else None)
    dx1, g["g_ffn"] = outs[:2]
    for n, t in zip(LATE[1:], outs[2:]):
        settle(n, pending[n], t, l + 1)
    g["w_ffn_gate"] = _wgrad_ff_cols(sv["h2"], dgate, "wgrad_ffn_gate")
    g["w_ffn_up"] = _wgrad_ff_cols(sv["h2"], dup, "wgrad_ffn_up")
    g["w_ffn_down"] = _wgrad_ff_rows(sv["act"], dx2, "wgrad_ffn_down")

    outs = _mix_bwd(dx1, sv["z"], sv["aout"], sv["sa"], sv["sb"], sv["ypre"], p["w_attn_proj"], p["w_glu_a"],
                    p["w_glu_b"], p["w_out"], _swap_rider([g[n] for n in EARLY]))
    dgates, da, gy, daout, dsa, dsb = outs[:6]
    early = [_add_half(g[n], s, c_idx) for n, s in zip(EARLY, outs[6:])]
    g["w_out"] = _wgrad_full(sv["mix"], dx1, "wgrad_out").reshape(N_CHIPS, D_MODEL // N_CHIPS, D_MODEL)
    g["w_attn_proj"] = _wgrad_cols(sv["a"], daout, "wgrad_attn_proj")
    g["w_glu_a"] = _wgrad_cols(sv["yact"], dsa, "wgrad_glu_a")
    g["w_glu_b"] = _wgrad_cols(sv["yact"], dsb, "wgrad_glu_b")

    outs = _ssm_bwd(sv["z"], gy, sv["pk"], p["d_skip"], sv["sd_re"], sv["sd_im"],
                    _scatter_rider([pending[LATE[0]]]) if pending else None)
    du, da_re, da_im, dbw_re, dbw_im, dcw_re, dcw_im, dd = outs[:8]
    if pending:
        settle(LATE[0], pending[LATE[0]], outs[8], l + 1)
    g["d_skip"] = jnp.sum(dd, axis=0, keepdims=True)
    g["ssm_pack_ct"] = dict(a_re=da_re, a_im=da_im, bw_re=dbw_re, bw_im=dbw_im, cw_re=dcw_re, cw_im=dcw_im)

    dos, cs = _combine_bwd(da, sv["os"], sv["ls"])
    dqkv = []
    for gi in range(N_GROUPS):
        grads, arrived = _attn_bwd(*sv["qkv"][gi], dos[gi], sv["ls"][gi], cs[gi], gi, _scatter_rider([early[gi]]))
        settle(EARLY[gi], early[gi], arrived[0], l)
        dqkv.append(grads)
    dz, gq8, gk8 = _qkv_post(sv["z"], dqkv, du, dgates, jnp.tile(p["g_q"], (1, N_HEADS)),
                             jnp.tile(p["g_k"], (1, N_HEADS)))
    g["g_q"] = jnp.sum(gq8.reshape(SUBLANES * N_HEADS, HEAD_DIM), axis=0, keepdims=True)
    g["g_k"] = jnp.sum(gk8.reshape(SUBLANES * N_HEADS, HEAD_DIM), axis=0, keepdims=True)
    g["w_in"] = _wgrad_cols(sv["h"], dz, "wgrad_in")
    outs = _in_proj_bwd(dz, p["w_in"], sv["x"], p["g_mix"], dx1, _swap_rider([g[n] for n in LATE]))
    dx, g["g_mix"] = outs[:2]
    late = {n: _add_half(g[n], s, c_idx) for n, s in zip(LATE, outs[2:])}
    return dx, g, late, owned


def _place():
    x, y, c = lax.axis_index("x"), lax.axis_index("y"), lax.axis_index("c")
    others = [(1 - x, y), (x, 1 - y), (1 - x, 1 - y)]
    return x, y, c, others


def _half(ref, hc):
    rows = ref.shape[-2] // 2
    idx = (slice(None),) * (len(ref.shape) - 2) + (pl.ds(hc * rows, rows), slice(None))
    return ref.at[idx]


def _comm_call(body, name, ins, out_shapes, n_remote, aliases=None):
    scratch = [pltpu.SemaphoreType.DMA((n_remote,)), pltpu.SemaphoreType.DMA((n_remote,))]
    return pl.pallas_call(
        body, name=name, in_specs=[ANY] * len(ins), out_specs=[ANY] * len(out_shapes), out_shape=out_shapes,
        scratch_shapes=scratch, input_output_aliases=aliases or {})(*ins)


def _cast_place(w, l, chip_idx):
    _, R, C = w.shape
    tr = R // 2

    def body(me_ref, w_ref, o_ref):
        o_ref[...] = w_ref[...].astype(BF16)

    return _call(body, name=f"cast_place_l{l}", grid=(R // tr,), prefetch=1,
                 in_specs=[pl.BlockSpec((None, tr, C), lambda i, me_ref: (l, i, 0))],
                 out_specs=pl.BlockSpec((None, tr, C), lambda i, me_ref: (me_ref[0], i, 0)),
                 out_shape=_sds((N_CHIPS, R, C), BF16), sem=("arbitrary",))(chip_idx, w)


def _in_place_rider(bufs, pairs, per_buf=3):
    n = len(bufs)

    def copies(outs, send, recv, side):
        return [pltpu.make_async_remote_copy(src_ref=pair[side][0], dst_ref=pair[side][0], send_sem=send.at[k],
                                             recv_sem=recv.at[k], device_id=pair[side][1], device_id_type=MESH)
                for k, pair in enumerate(pairs(outs))]

    def start(ins, outs, send, recv):
        for cp in copies(outs, send, recv, 0):
            cp.start()

    def wait(ins, outs, send, recv):
        for cp in copies(outs, send, recv, 1):
            cp.wait_recv()
        for cp in copies(outs, send, recv, 0):
            cp.wait_send()

    return Rider(list(bufs), [_sds(b.shape, b.dtype) for b in bufs], per_buf * n, start, wait,
                 {a: a for a in range(n)})


def _gather_ici_rider(bufs):
    def pairs(outs):
        x, y, c, others = _place()
        return [((_half(o.at[2 * x + y], c), (cx, cy, c)), (_half(o.at[2 * cx + cy], c), (cx, cy, c)))
                for o in outs for cx, cy in others]
    return _in_place_rider(bufs, pairs)


def _gather_d2d_rider(bufs):
    def pairs(outs):
        x, y, c, others = _place()
        sib = (x, y, 1 - c)
        return [((_half(o.at[2 * cx + cy], c), sib), (_half(o.at[2 * cx + cy], 1 - c), sib))
                for o in outs for cx, cy in others]
    return _in_place_rider(bufs, pairs)


def _swap_rider(gs):
    n = len(gs)

    def copies(ins, outs, send, recv):
        x, y, c, _ = _place()
        return [pltpu.make_async_remote_copy(src_ref=_half(ins[a], 1 - c), dst_ref=outs[a], send_sem=send.at[a],
                                             recv_sem=recv.at[a], device_id=(x, y, 1 - c), device_id_type=MESH)
                for a in range(n)]

    def start(ins, outs, send, recv):
        for cp in copies(ins, outs, send, recv):
            cp.start()

    def wait(ins, outs, send, recv):
        for cp in copies(ins, outs, send, recv):
            cp.wait()

    outs = [_sds((g.shape[0], g.shape[1] // 2, g.shape[2]), g.dtype) for g in gs]
    return Rider(list(gs), outs, n, start, wait, {})


def _scatter_rider(ss):
    n = len(ss)

    def copies(ins, outs, send, recv):
        x, y, c, others = _place()
        return [pltpu.make_async_remote_copy(
            src_ref=ins[a].at[2 * cx + cy], dst_ref=outs[a].at[j], send_sem=send.at[3 * a + j],
            recv_sem=recv.at[3 * a + j], device_id=(cx, cy, c), device_id_type=MESH)
            for a in range(n) for j, (cx, cy) in enumerate(others)]

    def start(ins, outs, send, recv):
        for cp in copies(ins, outs, send, recv):
            cp.start()

    def wait(ins, outs, send, recv):
        for cp in copies(ins, outs, send, recv):
            cp.wait()

    outs = [_sds((N_CHIPS - 1,) + s.shape[1:], s.dtype) for s in ss]
    return Rider(list(ss), outs, 3 * n, start, wait, {})


def _run_rider(rider, name):
    n_in = len(rider.ins)

    def body(*refs):
        ins, outs = refs[:n_in], refs[n_in:n_in + len(rider.out_shapes)]
        send, recv = refs[n_in + len(rider.out_shapes):]
        rider.start(ins, outs, send, recv)
        rider.wait(ins, outs, send, recv)

    return _comm_call(body, name, rider.ins, rider.out_shapes, rider.n_sem, aliases=rider.aliases)


def _join_halves(bufs):
    n = len(bufs)

    def body(*refs):
        outs = refs[n:2 * n]
        send, recv = refs[2 * n:]
        x, y, c, _ = _place()

        def swap(a, hc):
            region = _half(outs[a], hc)
            return pltpu.make_async_remote_copy(src_ref=region, dst_ref=region, send_sem=send.at[a],
                                                recv_sem=recv.at[a], device_id=(x, y, 1 - c), device_id_type=MESH)

        cps = [swap(a, c) for a in range(n)]
        for cp in cps:
            cp.start()
        for a in range(n):
            swap(a, 1 - c).wait_recv()
        for cp in cps:
            cp.wait_send()

    outs = [_sds(b.shape, b.dtype) for b in bufs]
    return _comm_call(body, "join_halves", bufs, outs, n, aliases={a: a for a in range(n)})


def _place_small(v, dev_idx):
    rows, n = v.shape

    def body(idx_ref, v_ref, o_ref):
        o_ref[...] = v_ref[...]

    return _call(body, name="place_small", grid=(1,), prefetch=1,
                 in_specs=[pl.BlockSpec((rows, n), lambda i, idx_ref: (0, 0))],
                 out_specs=pl.BlockSpec((None, rows, n), lambda i, idx_ref: (idx_ref[0], 0, 0)),
                 out_shape=_sds((8, rows, n), v.dtype), sem=("arbitrary",))(dev_idx, v)


def _small_ici_rider(buf):
    def pairs(outs):
        x, y, c, others = _place()
        peers = [(x, y, 1 - c)] + [(cx, cy, c) for cx, cy in others]
        return [((outs[0].at[4 * x + 2 * y + c], peer), (outs[0].at[4 * peer[0] + 2 * peer[1] + peer[2]], peer))
                for peer in peers]
    return _in_place_rider([buf], pairs, per_buf=4)


def _small_d2d_rider(buf):
    def pairs(outs):
        x, y, c, others = _place()
        sib = (x, y, 1 - c)
        return [((outs[0].at[4 * cx + 2 * cy + c], sib), (outs[0].at[4 * cx + 2 * cy + 1 - c], sib))
                for cx, cy in others]
    return _in_place_rider([buf], pairs)


def _add_half(g, p, c):
    _, R, C = g.shape
    half = R // 2

    def body(c_ref, g_ref, p_ref, o_ref):
        o_ref[...] = g_ref[...] + p_ref[...]

    blk = (None, half, C)
    return _call(body, name="add_half", grid=(N_CHIPS,), prefetch=1,
                 in_specs=[pl.BlockSpec(blk, lambda s, c_ref: (s, c_ref[0], 0)),
                           pl.BlockSpec(blk, lambda s, c_ref: (s, 0, 0))],
                 out_specs=pl.BlockSpec(blk, lambda s, c_ref: (s, 0, 0)),
                 out_shape=_sds((N_CHIPS, half, C), F32), sem=("arbitrary",))(c, g, p)


def _sum_owner(s, q, buf, l, me, c):
    _, half, C = s.shape
    tr = half // 2

    def body(me_ref, c_ref, s_ref, q0, q1, q2, buf_ref, o_ref):
        o_ref[...] = ((s_ref[...] + q0[...]) + q1[...]) + q2[...]

    blk = (None, tr, C)
    qspec = lambda j: pl.BlockSpec(blk, lambda i, me_ref, c_ref: (j, i, 0))
    return _call(body, name=f"sum_owner_l{l}", grid=(half // tr,), prefetch=2,
                 in_specs=[pl.BlockSpec(blk, lambda i, me_ref, c_ref: (me_ref[0], i, 0)),
                           qspec(0), qspec(1), qspec(2), ANY],
                 out_specs=pl.BlockSpec(blk, lambda i, me_ref, c_ref: (l, 2 * c_ref[0] + i, 0)),
                 out_shape=_sds(buf.shape, F32), sem=("arbitrary",), aliases={6: 0})(me, c, s, q, q, q, buf)


def _adamw_math(w, g, m, v):
    m = ADAM_B1 * m + (1.0 - ADAM_B1) * g
    v = ADAM_B2 * v + (1.0 - ADAM_B2) * (g * g)
    m_hat = m / (1.0 - ADAM_B1 ** ADAM_STEP)
    v_hat = v / (1.0 - ADAM_B2 ** ADAM_STEP)
    delta = -ADAM_LR * (m_hat / (jnp.sqrt(v_hat) + ADAM_EPS) + ADAM_WD * w)
    return delta, m, v


def _adamw(w, g, m, v, rider=None):
    depth, R, C = w.shape
    tr = next(t for t in (256, 128, 64, 32) if R % t == 0)

    def body(w_ref, g_ref, m_ref, v_ref, d_ref, nm_ref, nv_ref):
        d, nm, nv = _adamw_math(w_ref[...], g_ref[...], m_ref[...], v_ref[...])
        d_ref[...] = d
        nm_ref[...] = nm
        nv_ref[...] = nv

    spec = pl.BlockSpec((None, tr, C), lambda l, i: (l, i, 0))
    return _call(body, name="adamw", grid=(depth, R // tr), in_specs=[spec] * 4, out_specs=[spec] * 3,
                 out_shape=[_sds(w.shape, F32)] * 3, sem=("parallel", "parallel"), rider=rider)(w, g, m, v)


def _small_update(gathered, w, m, v):
    _, rows, n = gathered.shape
    tr = rows // 7

    def body(ga_ref, w_ref, m_ref, v_ref, g_ref, d_ref, nm_ref, nv_ref):
        g = ga_ref[0]
        for k in range(1, 8):
            g = g + ga_ref[k]
        d, nm, nv = _adamw_math(w_ref[...], g, m_ref[...], v_ref[...])
        g_ref[...] = g
        d_ref[...] = d
        nm_ref[...] = nm
        nv_ref[...] = nv

    spec = pl.BlockSpec((tr, n), lambda i: (i, 0))
    return _call(body, name="small_update", grid=(rows // tr,),
                 in_specs=[pl.BlockSpec((8, tr, n), lambda i: (0, i, 0)), spec, spec, spec], out_specs=[spec] * 4,
                 out_shape=[_sds((rows, n), F32)] * 4, sem=("parallel",))(gathered, w, m, v)


WEIGHTS = ("g_mix", "w_in", "g_q", "g_k", "w_attn_proj", "lambda_re", "lambda_im", "log_dt", "b_re", "b_im",
           "c_re", "c_im", "d_skip", "w_glu_a", "w_glu_b", "w_out", "g_ffn", "w_ffn_gate", "w_ffn_up", "w_ffn_down")
BIG = ("w_in", "w_attn_proj", "w_glu_a", "w_glu_b", "w_out", "w_ffn_gate", "w_ffn_up", "w_ffn_down")
SMALL = tuple(n for n in WEIGHTS if n not in BIG)
ROW_VECTORS = ("g_mix", "g_q", "g_k", "d_skip", "g_ffn")
PACK_QUANTUM = LANES * SUBLANES * 7


def _pack_small(parts, extra):
    flat = jnp.concatenate([parts[n].reshape(-1).astype(F32) for n in SMALL] + [extra.reshape(-1)])
    pad = -flat.shape[0] % PACK_QUANTUM
    return jnp.pad(flat, (0, pad)).reshape(-1, LANES)


def _unpack_small(packed, like):
    flat = packed.reshape(-1)
    out, at = {}, 0
    for n in SMALL:
        size = math.prod(like[n].shape)
        out[n] = flat[at:at + size].reshape(like[n].shape)
        at += size
    return out, flat[at]


def kernel(x, g_mix, w_in, g_q, g_k, w_attn_proj, lambda_re, lambda_im, log_dt, b_re, b_im, c_re, c_im, d_skip, w_glu_a, w_glu_b, w_out, g_ffn, w_ffn_gate, w_ffn_up, w_ffn_down, loss_target, m_g_mix, m_w_in, m_g_q, m_g_k, m_w_attn_proj, m_lambda_re, m_lambda_im, m_log_dt, m_b_re, m_b_im, m_c_re, m_c_im, m_d_skip, m_w_glu_a, m_w_glu_b, m_w_out, m_g_ffn, m_w_ffn_gate, m_w_ffn_up, m_w_ffn_down, v_g_mix, v_w_in, v_g_q, v_g_k, v_w_attn_proj, v_lambda_re, v_lambda_im, v_log_dt, v_b_re, v_b_im, v_c_re, v_c_im, v_d_skip, v_w_glu_a, v_w_glu_b, v_w_out, v_g_ffn, v_w_ffn_gate, v_w_ffn_up, v_w_ffn_down):
    given = dict(locals())
    W = {n: given[n] for n in WEIGHTS}
    M = {n: given["m_" + n] for n in WEIGHTS}
    V = {n: given["v_" + n] for n in WEIGHTS}
    depth = g_mix.shape[0]
    xl = x.reshape(x.shape[-2:])
    target = loss_target.reshape(loss_target.shape[-2:])
    c_idx = lax.axis_index("c").astype(jnp.int32).reshape(1)
    chip_idx = (2 * lax.axis_index("x") + lax.axis_index("y")).astype(jnp.int32).reshape(1)

    place = lambda l: [_cast_place(W[n], l, chip_idx) for n in BIG]
    bufs = place(0)
    w_in = _run_rider(_gather_d2d_rider(_run_rider(_gather_ici_rider(bufs[:1]), "gather_ici")), "gather_d2d")[0]
    rest, stage = bufs[1:], "ici"
    params, saved, h = [], [], xl
    for l in range(depth):
        p = {"w_in": w_in}
        for n in SMALL:
            p[n] = W[n][l][None] if n in ROW_VECTORS else W[n][l]
        h, sv, p, nxt = _layer_fwd(h, p, rest, stage, place(l + 1) if l + 1 < depth else None)
        params.append(p)
        saved.append(sv)
        if nxt:
            (w_in, rest), stage = nxt, "d2d"
    dx, loss_part = _loss_head(h, target)

    owned = {n: lax.empty(W[n].shape, F32) for n in BIG}
    small_grads = [None] * depth
    pending = None
    for l in reversed(range(depth)):
        dx, small_grads[l], pending, owned = _layer_bwd(dx, saved[l], params[l], pending, owned, l,
                                                        (chip_idx, c_idx))
    arrived = _run_rider(_scatter_rider([pending[n] for n in LATE]), "scatter_to_owners")
    for n, q in zip(LATE, arrived):
        owned[n] = _sum_owner(pending[n], q, owned[n], 0, chip_idx, c_idx)
    reduced = dict(zip(BIG, _join_halves([owned[n] for n in BIG])))

    ct = {k: jnp.stack([small_grads[l]["ssm_pack_ct"][k] for l in range(depth)])
          for k in small_grads[0]["ssm_pack_ct"]}
    ct["a_re"], ct["a_im"] = jnp.sum(ct["a_re"], axis=2), jnp.sum(ct["a_im"], axis=2)
    ct["a64_re"] = ct["a64_im"] = jnp.zeros_like(ct["a_re"])
    _, pull = jax.vjp(jax.vmap(_ssm_pack), *[W[n] for n in SSM_PARAMS])
    stacked = dict(zip(SSM_PARAMS, pull(ct)))
    for n in SMALL:
        if n not in stacked:
            stacked[n] = jnp.stack([small_grads[l][n] for l in range(depth)])
    zero = jnp.zeros((1,), F32)
    dev_idx = (4 * lax.axis_index("x") + 2 * lax.axis_index("y") + lax.axis_index("c")).astype(jnp.int32).reshape(1)
    gathered = _place_small(_pack_small(stacked, loss_part), dev_idx)
    riders = {BIG[0]: _small_ici_rider, BIG[5]: _small_d2d_rider}
    grads, delta, new_m, new_v = {}, {}, {}, {}
    for n in BIG:
        grads[n] = reduced[n]
        outs = _adamw(W[n], reduced[n], M[n], V[n], riders[n](gathered) if n in riders else None)
        delta[n], new_m[n], new_v[n] = outs[:3]
        if n in riders:
            gathered = outs[3]
    gs, ds, nms, nvs = _small_update(gathered, _pack_small(W, zero), _pack_small(M, zero), _pack_small(V, zero))
    sg, loss = _unpack_small(gs, W)
    sd, _ = _unpack_small(ds, W)
    sm, _ = _unpack_small(nms, W)
    sv_, _ = _unpack_small(nvs, W)
    for n in SMALL:
        grads[n], delta[n], new_m[n], new_v[n] = sg[n], sd[n], sm[n], sv_[n]

    return (loss, dx.reshape(x.shape), *[grads[n] for n in WEIGHTS], *[delta[n] for n in WEIGHTS],
            *[new_m[n] for n in WEIGHTS], *[new_v[n] for n in WEIGHTS])
```

```python
import collections
import functools
import math

import jax
import jax.numpy as jnp
from jax import lax
from jax.experimental import pallas as pl
from jax.experimental.pallas import tpu as pltpu

F32 = jnp.float32
BF16 = jnp.bfloat16

D_MODEL = 1024
DEPTH = 4
HEAD_DIM = 64
N_HEADS = 8
ATTN_WIDTH = N_HEADS * HEAD_DIM
ATTN_PATTERN = ((128, 1), (512, 4), (2048, 16))
N_GROUPS = len(ATTN_PATTERN)
BLK = 128
SSM_WIDTH = 512
SSM_GROUP = 16
SSM_GROUPS = 32
SSM_STATE = 64
D_FF = 2816
IN_COLS = 7168
EPS = 1e-6
ADAM_LR, ADAM_B1, ADAM_B2, ADAM_EPS, ADAM_WD, ADAM_STEP = 0.001, 0.9, 0.999, 1e-08, 0.01, 10

N_CHIPS = 4
MESH = pl.DeviceIdType.MESH

LANES = 128
SUBLANES = 8
VMEM_LIMIT = 56 * 1024 * 1024

TM = 512
TM_PROJ = 1024
TL_WGRAD = 2048
TM_MIX = 512

SSM_TB = 512
SSM_TC = 64
SSM_SUB = SUBLANES
SSM_PITCH = 72
N_SLAB = SSM_GROUPS * SSM_STATE // LANES
SSM_WIN = 256
N_PAIR = N_SLAB // 2
PAIRS_PER_WIN = 4
SCAN_GROUP = 4


def _params(sem=None, collective=False):
    return pltpu.CompilerParams(dimension_semantics=sem, vmem_limit_bytes=VMEM_LIMIT)


ANY = pl.BlockSpec(memory_space=pl.ANY)

Rider = collections.namedtuple("Rider", "ins out_shapes n_sem start wait aliases")


class _SemWindow:
    def __init__(self, ref, offset):
        self.ref, self.offset = ref, offset

    @property
    def at(self):
        return self

    def __getitem__(self, k):
        return self.ref.at[self.offset + k]


def _join_riders(*riders):
    riders = [r for r in riders if r is not None]
    if len(riders) <= 1:
        return riders[0] if riders else None

    def each(fn_name):
        def run(ins, outs, send, recv):
            i = o = s = 0
            for r in riders:
                getattr(r, fn_name)(ins[i:i + len(r.ins)], outs[o:o + len(r.out_shapes)],
                                    _SemWindow(send, s), _SemWindow(recv, s))
                i, o, s = i + len(r.ins), o + len(r.out_shapes), s + r.n_sem
        return run

    aliases, i, o = {}, 0, 0
    for r in riders:
        aliases.update({i + a: o + b for a, b in r.aliases.items()})
        i, o = i + len(r.ins), o + len(r.out_shapes)
    return Rider([t for r in riders for t in r.ins], [t for r in riders for t in r.out_shapes],
                 sum(r.n_sem for r in riders), each("start"), each("wait"), aliases)


def _with_rider(body, rider, grid, prefetch, n_in, n_out, n_scratch):
    n_rin, n_rout = len(rider.ins), len(rider.out_shapes)

    def hosted(*refs):
        pre, rest = refs[:prefetch], refs[prefetch:]
        ins, rin = rest[:n_in], rest[n_in:n_in + n_rin]
        o0 = n_in + n_rin
        outs, rout = rest[o0:o0 + n_out], rest[o0 + n_out:o0 + n_out + n_rout]
        s0 = o0 + n_out + n_rout
        scr, (send, recv) = rest[s0:s0 + n_scratch], rest[s0 + n_scratch:]
        first = functools.reduce(jnp.logical_and, [pl.program_id(k) == 0 for k in range(len(grid))])
        last = functools.reduce(jnp.logical_and, [pl.program_id(k) == grid[k] - 1 for k in range(len(grid))])

        @pl.when(first)
        def _():
            rider.start(rin, rout, send, recv)

        body(*pre, *ins, *outs, *scr)

        @pl.when(last)
        def _():
            rider.wait(rin, rout, send, recv)

    return hosted


def _call(body, *, name, grid, in_specs, out_specs, out_shape, scratch=(), sem=None, aliases=None,
          prefetch=0, rider=None):
    if rider is not None:
        single = not isinstance(out_specs, (list, tuple))
        out_specs = [out_specs] if single else list(out_specs)
        out_shape = [out_shape] if single else list(out_shape)
        body = _with_rider(body, rider, grid, prefetch, len(in_specs), len(out_specs), len(scratch))
        aliases = dict(aliases or {})
        aliases.update({prefetch + len(in_specs) + k: len(out_specs) + v for k, v in rider.aliases.items()})
        in_specs = list(in_specs) + [ANY] * len(rider.ins)
        out_specs = out_specs + [ANY] * len(rider.out_shapes)
        out_shape = out_shape + list(rider.out_shapes)
        scratch = list(scratch) + [pltpu.SemaphoreType.DMA((rider.n_sem,)), pltpu.SemaphoreType.DMA((rider.n_sem,))]
        sem = ("arbitrary",) * len(grid)
        fn = _call(body, name=name + "_host", grid=grid, in_specs=in_specs, out_specs=out_specs, out_shape=out_shape,
                   scratch=scratch, sem=sem, aliases=aliases, prefetch=prefetch)
        return lambda *args: fn(*args, *rider.ins)
    kw = {}
    if aliases:
        kw["input_output_aliases"] = aliases
    if prefetch:
        gs = pltpu.PrefetchScalarGridSpec(num_scalar_prefetch=prefetch, grid=grid, in_specs=in_specs,
                                          out_specs=out_specs, scratch_shapes=list(scratch))
        return pl.pallas_call(body, name=name, grid_spec=gs, out_shape=out_shape,
                              compiler_params=_params(sem), **kw)
    return pl.pallas_call(body, name=name, grid=grid, in_specs=in_specs, out_specs=out_specs,
                          out_shape=out_shape, scratch_shapes=list(scratch),
                          compiler_params=_params(sem), **kw)


def _sds(shape, dtype):
    return jax.ShapeDtypeStruct(shape, dtype)


def _sigmoid(v):
    return 1.0 / (1.0 + jnp.exp(-v))


def _dot(a, b):
    return jnp.dot(a, b, preferred_element_type=F32)


def _dot_nt(a, b):
    return lax.dot_general(a, b, (((1,), (1,)), ((), ())), preferred_element_type=F32)


def _dot_tn(a, b):
    return lax.dot_general(a, b, (((0,), (0,)), ((), ())), preferred_element_type=F32)


def _in_proj_fwd(x, g, w, rider=None):
    L = x.shape[0]
    ns = w.shape[2]
    tn = ns
    nj = ns // tn
    TM = TM_PROJ

    def body(x_ref, g_ref, w_ref, z_ref, h_ref):
        @pl.when(pl.program_id(1) == 0)
        def _():
            xv = x_ref[...]
            r = lax.rsqrt(jnp.mean(xv * xv, axis=-1, keepdims=True) + EPS)
            h_ref[...] = (xv * r * g_ref[...]).astype(BF16)
        z_ref[...] = _dot(h_ref[...], w_ref[...]).astype(BF16)

    return _call(
        body, name="in_proj_fwd", grid=(L // TM, N_CHIPS * nj),
        in_specs=[pl.BlockSpec((TM, D_MODEL), lambda i, j: (i, 0)),
                  pl.BlockSpec((1, D_MODEL), lambda i, j: (0, 0)),
                  pl.BlockSpec((None, D_MODEL, tn), lambda i, j: (j // nj, 0, j % nj))],
        out_specs=[pl.BlockSpec((TM, tn), lambda i, j: (i, j)),
                   pl.BlockSpec((TM, D_MODEL), lambda i, j: (i, 0))],
        out_shape=[_sds((L, N_CHIPS * ns), BF16), _sds((L, D_MODEL), BF16)],
        sem=("parallel", "arbitrary"), rider=rider)(x, g, w)


DL_TILE = 512
SCALE = HEAD_DIM ** -0.5


def _perm_matrix(d):
    rho = jnp.arange(DL_TILE)
    src = rho // (DL_TILE // d) + d * (rho % (DL_TILE // d))
    return (src[:, None] == jnp.arange(DL_TILE)[None, :]).astype(BF16)


def _head_sum_matrix():
    h = jnp.arange(ATTN_WIDTH) // HEAD_DIM
    return (h[:, None] == h[None, :]).astype(BF16)


def _split(v):
    hi = v.astype(BF16)
    return hi, (v - hi.astype(F32)).astype(BF16)


def _head_sum(v, hs):
    vb = v.astype(BF16)
    half = ATTN_WIDTH // 2
    blk = hs[:half, :half]
    return jnp.concatenate([_dot(vb[:, :half], blk), _dot(vb[:, half:], blk)], axis=1)


def _permute(pm, v):
    hi, lo = _split(v)
    return _dot(pm, hi) + _dot(pm, lo)


def _dl_view(t, d):
    if d * BLK <= DL_TILE:
        return t
    return t.reshape(t.shape[0] // DL_TILE, d, DL_TILE // d, t.shape[1])


def _dl_spec(d, width, which):
    if d * BLK <= DL_TILE:
        per_tile = DL_TILE // (d * BLK)
        return pl.BlockSpec((BLK, width), lambda r, n: ((which(n) // per_tile) * (DL_TILE // BLK)
                                                       + r * per_tile + which(n) % per_tile, 0))
    tiles = d * BLK // DL_TILE
    return pl.BlockSpec((tiles, None, DL_TILE // d, width), lambda r, n: (which(n), r, 0, 0))


def _dl_read(ref):
    v = ref[...]
    return v if v.ndim == 2 else v.reshape(BLK, v.shape[-1])


def _dl_write(ref, v):
    ref[...] = v if len(ref.shape) == 2 else v.reshape(ref.shape)


def _qkv_prep(z, gq_t, gk_t, rider=None):
    L = z.shape[0]
    qkv_w = N_GROUPS * ATTN_WIDTH

    def body(zq_ref, zk_ref, zv_ref, gq_ref, gk_ref, hs_ref, p1_ref, p2_ref, *outs):
        hs = hs_ref[...]
        perms = (None, p1_ref[...], p2_ref[...])
        for g in range(N_GROUPS):
            cols = slice(g * ATTN_WIDTH, (g + 1) * ATTN_WIDTH)
            xq = zq_ref[:, cols].astype(F32)
            xk = zk_ref[:, cols].astype(F32)
            rq = lax.rsqrt(_head_sum(xq * xq, hs) * (1.0 / HEAD_DIM) + EPS)
            rk = lax.rsqrt(_head_sum(xk * xk, hs) * (1.0 / HEAD_DIM) + EPS)
            vals = [(xq * rq * (gq_ref[...] * SCALE)).astype(BF16), (xk * rk * gk_ref[...]).astype(BF16),
                    zv_ref[:, cols]]
            for j, t in enumerate(vals):
                if perms[g] is not None:
                    t = _dot(perms[g], t).astype(BF16)
                outs[3 * g + j][...] = t

    tile = pl.BlockSpec((DL_TILE, ATTN_WIDTH), lambda i: (i, 0))
    mat = pl.BlockSpec((DL_TILE, DL_TILE), lambda i: (0, 0))
    vec = pl.BlockSpec((1, ATTN_WIDTH), lambda i: (0, 0))
    outs = _call(
        body, name="qkv_prep", grid=(L // DL_TILE,),
        in_specs=[pl.BlockSpec((DL_TILE, qkv_w), lambda i: (i, 0)), pl.BlockSpec((DL_TILE, qkv_w), lambda i: (i, 1)),
                  pl.BlockSpec((DL_TILE, qkv_w), lambda i: (i, 2)), vec, vec, mat, mat, mat],
        out_specs=[tile] * 9, out_shape=[_sds((L, ATTN_WIDTH), BF16)] * 9,
        sem=("parallel",), rider=rider)(z, z, z, gq_t, gk_t, _head_sum_matrix(), _perm_matrix(ATTN_PATTERN[1][1]),
                                        _perm_matrix(ATTN_PATTERN[2][1]))
    return [tuple(outs[3 * g:3 * g + 3]) for g in range(N_GROUPS)], list(outs[3 * N_GROUPS:])


def _pair_masks():
    lane = lax.broadcasted_iota(jnp.int32, (1, LANES), 1)
    return lane < HEAD_DIM, lane >= HEAD_DIM


def _attn_fwd(qs, ks, v, gi):
    L = qs.shape[0]
    _, d = ATTN_PATTERN[gi]
    nb = L // (d * BLK)

    def body(q_ref, kc_ref, kp_ref, vc_ref, vp_ref, o_ref, l_ref):
        n = pl.program_id(1)
        qi = lax.broadcasted_iota(jnp.int32, (BLK, 2 * BLK), 0)
        kj = lax.broadcasted_iota(jnp.int32, (BLK, 2 * BLK), 1)
        prev = kj < BLK
        mask = jnp.logical_and(jnp.where(prev, kj, qi) >= jnp.where(prev, qi, kj - BLK),
                               kj >= jnp.where(n > 0, 0, BLK))
        q = _dl_read(q_ref)
        kw = jnp.concatenate([_dl_read(kp_ref), _dl_read(kc_ref)], axis=0)
        vw = jnp.concatenate([_dl_read(vp_ref), _dl_read(vc_ref)], axis=0)
        one = jnp.ones((2 * BLK, LANES), BF16)
        o_parts, l_parts = [], []
        for hp in range(N_HEADS // 2):
            ls = slice(hp * LANES, (hp + 1) * LANES)
            qp, kp_, vp_ = q[:, ls], kw[:, ls], vw[:, ls]
            num = jnp.zeros((BLK, LANES), F32)
            den = jnp.zeros((BLK, LANES), F32)
            mb = jnp.zeros((BLK, LANES), F32)
            for he in _pair_masks():
                s = jnp.where(mask, _dot_nt(jnp.where(he, qp, 0), kp_), -jnp.inf)
                m = jnp.max(s, axis=-1, keepdims=True)
                p = jnp.exp(s - m).astype(BF16)
                acc = _dot(p, jnp.concatenate([jnp.where(he, vp_, 0), jnp.where(he, one, 0)], axis=1))
                num += acc[:, :LANES]
                den += acc[:, LANES:]
                mb = jnp.where(he, m, mb)
            o_parts.append((num / den).astype(BF16))
            l_parts.append(mb + jnp.log(den))
        _dl_write(o_ref, jnp.concatenate(o_parts, axis=1))
        _dl_write(l_ref, jnp.concatenate(l_parts, axis=1))

    cur = _dl_spec(d, ATTN_WIDTH, lambda n: n)
    prev = _dl_spec(d, ATTN_WIDTH, lambda n: jnp.maximum(n - 1, 0))
    view = lambda t: _dl_view(t, d)
    o, l = _call(
        body, name=f"attn_fwd_g{gi}", grid=(d, nb), in_specs=[cur, cur, prev, cur, prev], out_specs=[cur, cur],
        out_shape=[_sds(view(qs).shape, BF16), _sds(view(qs).shape, F32)],
        sem=("parallel", "parallel"))(view(qs), view(ks), view(ks), view(v), view(v))
    return o.reshape(L, ATTN_WIDTH), l.reshape(L, ATTN_WIDTH)


def _to_token_order(os_, ls_, pts):
    o_tok, l_tok = [], []
    for o, l, pt in zip(os_, ls_, pts):
        if pt is None:
            o_tok.append(o.astype(F32))
            l_tok.append(l)
        else:
            o_tok.append(_dot(pt, o))
            l_tok.append(_permute(pt, l))
    return o_tok, l_tok


def _combine_fwd(os_, ls_):
    L = os_[0].shape[0]

    def body(o0, o1, o2, l0, l1, l2, pt1_ref, pt2_ref, a_ref):
        o_tok, l_tok = _to_token_order((o0[...], o1[...], o2[...]), (l0[...], l1[...], l2[...]),
                                       (None, pt1_ref[...], pt2_ref[...]))
        w = _combine_weights(*l_tok)
        a_ref[...] = (w[0] * o_tok[0] + w[1] * o_tok[1] + w[2] * o_tok[2]).astype(BF16)

    tile = pl.BlockSpec((DL_TILE, ATTN_WIDTH), lambda i: (i, 0))
    mat = pl.BlockSpec((DL_TILE, DL_TILE), lambda i: (0, 0))
    return _call(body, name="combine_fwd", grid=(L // DL_TILE,), in_specs=[tile] * 6 + [mat, mat], out_specs=tile,
                 out_shape=_sds((L, ATTN_WIDTH), BF16), sem=("parallel",))(
                     *os_, *ls_, _perm_matrix(ATTN_PATTERN[1][1]).T, _perm_matrix(ATTN_PATTERN[2][1]).T)


def _gelu(v):
    c = math.sqrt(2.0 / math.pi)
    return 0.5 * v * (1.0 + jnp.tanh(c * (v + 0.044715 * v * v * v)))


def _gelu_grad(v):
    c = math.sqrt(2.0 / math.pi)
    t = jnp.tanh(c * (v + 0.044715 * v * v * v))
    return 0.5 * (1.0 + t) + 0.5 * v * (1.0 - t * t) * c * (1.0 + 3.0 * 0.044715 * v * v)


def _ssm_fill(u, bwre_ref, bwim_ref, sre, sim):
    for k2 in range(N_PAIR):
        uw = u[:, _win_cols(k2)]
        _to_slabs(sre, k2, _dot(uw, bwre_ref[k2]))
        _to_slabs(sim, k2, _dot(uw, bwim_ref[k2]))


def _win_cols(k2):
    w = k2 // PAIRS_PER_WIN
    return slice(w * SSM_WIN, (w + 1) * SSM_WIN)


def _to_slabs(ref, k2, v):
    for half in range(2):
        for j in range(SSM_SUB):
            ref[2 * k2 + half, j * SSM_PITCH:j * SSM_PITCH + SSM_TC, :] = (
                v[j * SSM_TC:(j + 1) * SSM_TC, half * LANES:(half + 1) * LANES])


def _rows(i):
    return pl.ds(i, SSM_SUB, stride=SSM_PITCH)


def _slab_rows(ref, k):
    return jnp.concatenate([ref[k, j * SSM_PITCH:j * SSM_PITCH + SSM_TC, :] for j in range(SSM_SUB)], axis=0)


def _pair_rows(ref, k2):
    return jnp.concatenate([_slab_rows(ref, 2 * k2), _slab_rows(ref, 2 * k2 + 1)], axis=1).astype(BF16)


def _bcast(ref, k):
    return jnp.broadcast_to(ref[pl.ds(k, 1), :], (SSM_SUB, LANES))


def _scan(sre, sim, are_ref, aim_ref, k0, init, *, reverse, store, sign=1.0):
    ar = [_bcast(are_ref, k0 + kk) for kk in range(SCAN_GROUP)]
    ai = [sign * _bcast(aim_ref, k0 + kk) for kk in range(SCAN_GROUP)]

    def step(t, carry):
        i = SSM_TC - 1 - t if reverse else t
        out = []
        for kk in range(SCAN_GROUP):
            k = k0 + kk
            xr, xi = carry[2 * kk], carry[2 * kk + 1]
            nr = ar[kk] * xr - ai[kk] * xi + sre[k, _rows(i), :]
            ni = ar[kk] * xi + ai[kk] * xr + sim[k, _rows(i), :]
            if store:
                sre[k, _rows(i), :] = nr
                sim[k, _rows(i), :] = ni
            out += [nr, ni]
        return tuple(out)

    flat = []
    for re, im in init:
        flat += [re, im]
    res = lax.fori_loop(0, SSM_TC, step, tuple(flat), unroll=2)
    return [(res[2 * kk], res[2 * kk + 1]) for kk in range(SCAN_GROUP)]


def _ssm_seeds(ends_re, ends_im, a64re_ref, a64im_ref, carry_re, carry_im, seed_re, seed_im, k,
               *, reverse, sign=1.0):
    ar = a64re_ref[pl.ds(k, 1), :]
    ai = sign * a64im_ref[pl.ds(k, 1), :]
    cr = carry_re[pl.ds(k, 1), :]
    ci = carry_im[pl.ds(k, 1), :]
    order = range(SSM_SUB - 1, -1, -1) if reverse else range(SSM_SUB)
    for j in order:
        seed_re[k, pl.ds(j, 1), :] = cr
        seed_im[k, pl.ds(j, 1), :] = ci
        er = ends_re[k, pl.ds(j, 1), :]
        ei = ends_im[k, pl.ds(j, 1), :]
        cr, ci = ar * cr - ai * ci + er, ar * ci + ai * cr + ei
    carry_re[pl.ds(k, 1), :] = cr
    carry_im[pl.ds(k, 1), :] = ci


def _ssm_specs_consts():
    c2 = pl.BlockSpec((N_SLAB, LANES), lambda b: (0, 0))
    c3 = pl.BlockSpec((N_PAIR, SSM_WIN, SSM_WIN), lambda b: (0, 0, 0))
    return c2, c3


def _ssm_scratch():
    rows = SSM_SUB * SSM_PITCH
    return [pltpu.VMEM((N_SLAB, rows, LANES), F32), pltpu.VMEM((N_SLAB, rows, LANES), F32)]


def _ssm_fwd(z, pk, dskip, rider=None):
    L = z.shape[0]
    nb = L // SSM_TB
    ucol = (3 * N_GROUPS * ATTN_WIDTH) // SSM_WIDTH

    def body(u_ref, are_ref, aim_ref, a64re_ref, a64im_ref, bwre_ref, bwim_ref, cwre_ref, cwim_ref, d_ref,
             ypre_ref, yact_ref, sdre_ref, sdim_ref, sre, sim, carry_re, carry_im, ends_re, ends_im,
             seed_re, seed_im):
        @pl.when(pl.program_id(0) == 0)
        def _():
            carry_re[...] = jnp.zeros_like(carry_re)
            carry_im[...] = jnp.zeros_like(carry_im)

        u = u_ref[...]
        _ssm_fill(u, bwre_ref, bwim_ref, sre, sim)
        zero = jnp.zeros((SSM_SUB, LANES), F32)
        for k0 in range(0, N_SLAB, SCAN_GROUP):
            ends = _scan(sre, sim, are_ref, aim_ref, k0, [(zero, zero)] * SCAN_GROUP, reverse=False, store=False)
            for kk in range(SCAN_GROUP):
                ends_re[k0 + kk] = ends[kk][0]
                ends_im[k0 + kk] = ends[kk][1]
            for kk in range(SCAN_GROUP):
                _ssm_seeds(ends_re, ends_im, a64re_ref, a64im_ref, carry_re, carry_im, seed_re, seed_im,
                           k0 + kk, reverse=False)
            init = [(seed_re[k0 + kk], seed_im[k0 + kk]) for kk in range(SCAN_GROUP)]
            _scan(sre, sim, are_ref, aim_ref, k0, init, reverse=False, store=True)
        sdre_ref[...] = seed_re[...]
        sdim_ref[...] = seed_im[...]
        for w in range(N_PAIR // PAIRS_PER_WIN):
            acc = jnp.zeros((SSM_TB, SSM_WIN), F32)
            for kk in range(PAIRS_PER_WIN):
                k2 = w * PAIRS_PER_WIN + kk
                acc += _dot(_pair_rows(sre, k2), cwre_ref[k2])
                acc -= _dot(_pair_rows(sim, k2), cwim_ref[k2])
            cols = _win_cols(w * PAIRS_PER_WIN)
            ypre = acc + d_ref[:, cols] * u[:, cols].astype(F32)
            ypre_ref[:, cols] = ypre
            yact_ref[:, cols] = _gelu(ypre).astype(BF16)

    c2, c3 = _ssm_specs_consts()
    seed_spec = pl.BlockSpec((None, N_SLAB, SSM_SUB, LANES), lambda b: (b, 0, 0, 0))
    small = pltpu.VMEM((N_SLAB, LANES), F32)
    tile = pltpu.VMEM((N_SLAB, SSM_SUB, LANES), F32)
    return _call(
        body, name="ssm_fwd", grid=(nb,),
        in_specs=[pl.BlockSpec((SSM_TB, SSM_WIDTH), lambda b: (b, ucol)), c2, c2, c2, c2, c3, c3, c3, c3,
                  pl.BlockSpec((1, SSM_WIDTH), lambda b: (0, 0))],
        out_specs=[pl.BlockSpec((SSM_TB, SSM_WIDTH), lambda b: (b, 0)),
                   pl.BlockSpec((SSM_TB, SSM_WIDTH), lambda b: (b, 0)), seed_spec, seed_spec],
        out_shape=[_sds((L, SSM_WIDTH), F32), _sds((L, SSM_WIDTH), BF16),
                   _sds((nb, N_SLAB, SSM_SUB, LANES), F32), _sds((nb, N_SLAB, SSM_SUB, LANES), F32)],
        scratch=_ssm_scratch() + [small, small, tile, tile, tile, tile],
        sem=("arbitrary",), rider=rider)(z, pk["a_re"], pk["a_im"], pk["a64_re"], pk["a64_im"],
                                         pk["bw_re"].astype(BF16), pk["bw_im"].astype(BF16),
                                         pk["cw_re"].astype(BF16), pk["cw_im"].astype(BF16), dskip)


def _combine_weights(l0, l1, l2):
    m = jnp.maximum(jnp.maximum(l0, l1), l2)
    e0, e1, e2 = jnp.exp(l0 - m), jnp.exp(l1 - m), jnp.exp(l2 - m)
    inv = 1.0 / (e0 + e1 + e2)
    return e0 * inv, e1 * inv, e2 * inv


def _mix_fwd(x, z, a, yact, w_ap, w_ga, w_gb, w_out):
    L = x.shape[0]
    cs = D_MODEL // N_CHIPS
    ga_col = (3 * N_GROUPS * ATTN_WIDTH + SSM_WIDTH) // D_MODEL

    def body(x_ref, ga_ref, gs_ref, a_ref, y_ref, wap_ref, wga_ref, wgb_ref, wout_ref,
             x1_ref, aout_ref, sa_ref, sb_ref, mix_ref):
        a = a_ref[...]
        y = y_ref[...]
        for s in range(N_CHIPS):
            cols = slice(s * cs, (s + 1) * cs)
            aout_ref[:, cols] = _dot(a, wap_ref[s]).astype(BF16)
            sa_ref[:, cols] = _dot(y, wga_ref[s]).astype(BF16)
            sb_ref[:, cols] = _dot(y, wgb_ref[s]).astype(BF16)
        s_out = sa_ref[...].astype(F32) * _sigmoid(sb_ref[...].astype(F32))
        mix = (_sigmoid(ga_ref[...].astype(F32)) * aout_ref[...].astype(F32)
               + _sigmoid(gs_ref[...].astype(F32)) * s_out).astype(BF16)
        mix_ref[...] = mix
        x1_ref[...] = x_ref[...] + _dot(mix, wout_ref[...])

    tok = lambda w: pl.BlockSpec((TM_MIX, w), lambda i: (i, 0))
    wsm = pl.BlockSpec((N_CHIPS, ATTN_WIDTH, cs), lambda i: (0, 0, 0))
    return _call(
        body, name="mix_fwd", grid=(L // TM_MIX,),
        in_specs=[tok(D_MODEL), pl.BlockSpec((TM_MIX, D_MODEL), lambda i: (i, ga_col)),
                  pl.BlockSpec((TM_MIX, D_MODEL), lambda i: (i, ga_col + 1))]
                 + [tok(ATTN_WIDTH)] * 2 + [wsm, wsm, wsm, pl.BlockSpec((D_MODEL, D_MODEL), lambda i: (0, 0))],
        out_specs=[tok(D_MODEL), tok(D_MODEL), tok(D_MODEL), tok(D_MODEL), tok(D_MODEL)],
        out_shape=[_sds((L, D_MODEL), F32)] + [_sds((L, D_MODEL), BF16)] * 4,
        sem=("parallel",))(x, z, z, a, yact, w_ap, w_ga, w_gb, w_out.reshape(D_MODEL, D_MODEL))


def _ffn_fwd(x1, g, w_g, w_u, w_d, rider=None):
    L = x1.shape[0]
    fs = D_FF // N_CHIPS
    TM = TM_PROJ

    def body(x_ref, g_ref, wg_ref, wu_ref, wd_ref, x2_ref, h_ref, gate_ref, up_ref, act_ref, acc):
        s = pl.program_id(1)

        @pl.when(s == 0)
        def _():
            xv = x_ref[...]
            r = lax.rsqrt(jnp.mean(xv * xv, axis=-1, keepdims=True) + EPS)
            h_ref[...] = (xv * r * g_ref[...]).astype(BF16)
            acc[...] = jnp.zeros_like(acc)

        h = h_ref[...]
        gate = _dot(h, wg_ref[...])
        up = _dot(h, wu_ref[...])
        act = (gate * _sigmoid(gate) * up).astype(BF16)
        gate_ref[...] = gate.astype(BF16)
        up_ref[...] = up.astype(BF16)
        act_ref[...] = act
        acc[...] += _dot(act, wd_ref[...])

        @pl.when(s == N_CHIPS - 1)
        def _():
            x2_ref[...] = x_ref[...] + acc[...]

    tok = pl.BlockSpec((TM, D_MODEL), lambda i, s: (i, 0))
    ffs = pl.BlockSpec((None, TM, fs), lambda i, s: (s, i, 0))
    return _call(
        body, name="ffn_fwd", grid=(L // TM, N_CHIPS),
        in_specs=[tok, pl.BlockSpec((1, D_MODEL), lambda i, s: (0, 0)),
                  pl.BlockSpec((None, D_MODEL, fs), lambda i, s: (s, 0, 0)),
                  pl.BlockSpec((None, D_MODEL, fs), lambda i, s: (s, 0, 0)),
                  pl.BlockSpec((None, fs, D_MODEL), lambda i, s: (s, 0, 0))],
        out_specs=[tok, tok, ffs, ffs, ffs],
        out_shape=[_sds((L, D_MODEL), F32), _sds((L, D_MODEL), BF16)] + [_sds((N_CHIPS, L, fs), BF16)] * 3,
        scratch=[pltpu.VMEM((TM, D_MODEL), F32)],
        sem=("parallel", "arbitrary"), rider=rider)(x1, g, w_g, w_u, w_d)


def _loss_head(xl, target):
    L = xl.shape[0]

    def body(x_ref, t_ref, dx_ref, loss_ref, acc):
        i = pl.program_id(0)

        @pl.when(i == 0)
        def _():
            acc[...] = jnp.zeros_like(acc)

        e = x_ref[...] - t_ref[...]
        dx_ref[...] = e * (1.0 / D_MODEL)
        acc[...] += jnp.sum((e * e).reshape(TM // SUBLANES, SUBLANES, D_MODEL), axis=0)

        @pl.when(i == pl.num_programs(0) - 1)
        def _():
            loss_ref[...] = (0.5 / D_MODEL) * jnp.sum(acc[...]).reshape(1, 1)

    tok = pl.BlockSpec((TM, D_MODEL), lambda i: (i, 0))
    return _call(
        body, name="loss_head", grid=(L // TM,), in_specs=[tok, tok],
        out_specs=[tok, pl.BlockSpec((1, 1), lambda i: (0, 0))],
        out_shape=[_sds((L, D_MODEL), F32), _sds((1, 1), F32)],
        scratch=[pltpu.VMEM((SUBLANES, D_MODEL), F32)], sem=("arbitrary",))(xl, target)


def _ssm_pack(lam_re, lam_im, log_dt, b_re, b_im, c_re, c_im):
    dt = jnp.exp(log_dt)[:, None]
    mag = jnp.exp(lam_re * dt)
    ang = lam_im * dt
    ar = mag * jnp.cos(ang)
    ai = mag * jnp.sin(ang)
    nr = ar - 1.0
    ni = ai
    den = lam_re * lam_re + lam_im * lam_im
    cr = ((nr * lam_re + ni * lam_im) / den)[..., None]
    ci = ((ni * lam_re - nr * lam_im) / den)[..., None]
    bbr = cr * b_re - ci * b_im
    bbi = cr * b_im + ci * b_re
    gpp = SSM_WIN // SSM_STATE
    gpw = SSM_WIN // SSM_GROUP
    k2 = jnp.arange(N_PAIR)[:, None, None]
    gs = jnp.arange(gpp)[None, :, None]
    gl = jnp.arange(gpw)[None, None, :]
    same = (gl == gpp * (k2 % PAIRS_PER_WIN) + gs).astype(F32)

    def b_windows(bb):
        return jnp.einsum('kgl,kgpc->klcgp', same, bb.reshape(N_PAIR, gpp, SSM_STATE, SSM_GROUP)).reshape(
            N_PAIR, SSM_WIN, SSM_WIN)

    def c_windows(cc):
        return jnp.einsum('kgl,kgcp->kgplc', same, cc.reshape(N_PAIR, gpp, SSM_GROUP, SSM_STATE)).reshape(
            N_PAIR, SSM_WIN, SSM_WIN)

    pr, pi = ar, ai
    for _ in range(int(math.log2(SSM_TC))):
        pr, pi = pr * pr - pi * pi, 2.0 * pr * pi
    return dict(a_re=ar.reshape(N_SLAB, LANES), a_im=ai.reshape(N_SLAB, LANES),
                a64_re=pr.reshape(N_SLAB, LANES), a64_im=pi.reshape(N_SLAB, LANES),
                bw_re=b_windows(bbr), bw_im=b_windows(bbi), cw_re=c_windows(c_re), cw_im=c_windows(c_im))


def _layer_fwd(x, p, rest, rest_stage, next_bufs=None):
    first = {"ici": _gather_ici_rider, "d2d": _gather_d2d_rider}[rest_stage]
    outs = _in_proj_fwd(x, p["g_mix"], p["w_in"], first(rest))
    (z, h), rest = outs[:2], list(outs[2:])
    qkv, got = _qkv_prep(z, jnp.tile(p["g_q"], (1, N_HEADS)), jnp.tile(p["g_k"], (1, N_HEADS)),
                         _gather_d2d_rider(rest) if rest_stage == "ici" else None)
    p = {**p, **dict(zip(BIG[1:], got if rest_stage == "ici" else rest))}
    os_, ls_ = [], []
    for gi in range(N_GROUPS):
        o, l = _attn_fwd(*qkv[gi], gi)
        os_.append(o)
        ls_.append(l)
    a = _combine_fwd(os_, ls_)
    pk = _ssm_pack(p["lambda_re"], p["lambda_im"], p["log_dt"], p["b_re"], p["b_im"], p["c_re"], p["c_im"])
    outs = _ssm_fwd(z, pk, p["d_skip"], _gather_ici_rider(next_bufs[:1]) if next_bufs else None)
    (ypre, yact, sd_re, sd_im), next_in = outs[:4], list(outs[4:])
    x1, aout, sa, sb, mix = _mix_fwd(x, z, a, yact, p["w_attn_proj"], p["w_glu_a"], p["w_glu_b"], p["w_out"])
    outs = _ffn_fwd(x1, p["g_ffn"], p["w_ffn_gate"], p["w_ffn_up"], p["w_ffn_down"],
                    _join_riders(_gather_ici_rider(next_bufs[1:]), _gather_d2d_rider(next_in)) if next_bufs else None)
    x2, h2, gate, up, act = outs[:5]
    nxt = (outs[-1], list(outs[5:-1])) if next_bufs else None
    saved = dict(x=x, z=z, h=h, qkv=qkv, os=os_, ls=ls_, pk=pk, ypre=ypre, yact=yact, sd_re=sd_re, sd_im=sd_im,
                 x1=x1, a=a, aout=aout, sa=sa, sb=sb, mix=mix, h2=h2, gate=gate, up=up, act=act)
    return x2, saved, p, nxt


def _rms_bwd(xv, g, dh):
    r = lax.rsqrt(jnp.mean(xv * xv, axis=-1, keepdims=True) + EPS)
    xn = xv * r
    dxn = dh * g
    dx = r * (dxn - xn * jnp.mean(dxn * xn, axis=-1, keepdims=True))
    dg = jnp.sum((dh * xn).reshape(xv.shape[0] // SUBLANES, SUBLANES, xv.shape[1]), axis=0)
    return dx, dg


def _ffn_bwd_act(dx2, gate, up, w_d):
    L = dx2.shape[0]
    fs = D_FF // N_CHIPS
    TM = TM_PROJ

    def body(dx_ref, gate_ref, up_ref, wd_ref, dgate_ref, dup_ref):
        dact = _dot_nt(dx_ref[...].astype(BF16), wd_ref[...])
        gt = gate_ref[...].astype(F32)
        sg = _sigmoid(gt)
        dgate_ref[...] = (dact * up_ref[...].astype(F32) * (sg * (1.0 + gt * (1.0 - sg)))).astype(BF16)
        dup_ref[...] = (dact * gt * sg).astype(BF16)

    ffs = pl.BlockSpec((None, TM, fs), lambda i, s: (s, i, 0))
    return _call(
        body, name="ffn_bwd_act", grid=(L // TM, N_CHIPS),
        in_specs=[pl.BlockSpec((TM, D_MODEL), lambda i, s: (i, 0)), ffs, ffs,
                  pl.BlockSpec((None, fs, D_MODEL), lambda i, s: (s, 0, 0))],
        out_specs=[ffs, ffs], out_shape=[_sds((N_CHIPS, L, fs), BF16)] * 2,
        sem=("parallel", "parallel"))(dx2, gate, up, w_d)


def _ffn_bwd_in(dx2, x1, g, dgate, dup, w_g, w_u, rider=None):
    L = x1.shape[0]
    fs = D_FF // N_CHIPS
    TM = TM_PROJ

    def body(dx_ref, x_ref, g_ref, dgate_ref, dup_ref, wg_ref, wu_ref, dx1_ref, dg_ref, acc, dgacc):
        i, s = pl.program_id(0), pl.program_id(1)

        @pl.when(s == 0)
        def _():
            acc[...] = jnp.zeros_like(acc)

        @pl.when(jnp.logical_and(i == 0, s == 0))
        def _():
            dgacc[...] = jnp.zeros_like(dgacc)

        acc[...] += _dot_nt(dgate_ref[...], wg_ref[...]) + _dot_nt(dup_ref[...], wu_ref[...])

        @pl.when(s == N_CHIPS - 1)
        def _():
            dx, dg = _rms_bwd(x_ref[...], g_ref[...], acc[...])
            dx1_ref[...] = dx_ref[...] + dx
            dgacc[...] += dg

        @pl.when(jnp.logical_and(i == pl.num_programs(0) - 1, s == N_CHIPS - 1))
        def _():
            dg_ref[...] = jnp.sum(dgacc[...], axis=0, keepdims=True)

    tok = pl.BlockSpec((TM, D_MODEL), lambda i, s: (i, 0))
    ffs = pl.BlockSpec((None, TM, fs), lambda i, s: (s, i, 0))
    vec = pl.BlockSpec((1, D_MODEL), lambda i, s: (0, 0))
    return _call(
        body, name="ffn_bwd_in", grid=(L // TM, N_CHIPS),
        in_specs=[tok, tok, vec, ffs, ffs,
                  pl.BlockSpec((None, D_MODEL, fs), lambda i, s: (s, 0, 0)),
                  pl.BlockSpec((None, D_MODEL, fs), lambda i, s: (s, 0, 0))],
        out_specs=[tok, vec],
        out_shape=[_sds((L, D_MODEL), F32), _sds((1, D_MODEL), F32)],
        scratch=[pltpu.VMEM((TM, D_MODEL), F32), pltpu.VMEM((SUBLANES, D_MODEL), F32)],
        sem=("arbitrary", "arbitrary"), rider=rider)(dx2, x1, g, dgate, dup, w_g, w_u)


def _wgrad(a, b, *, name, grid_kn, a_spec, b_spec, out_shape, out_spec):
    L = a.shape[-2]
    nl = L // TL_WGRAD

    def body(a_ref, b_ref, o_ref):
        @pl.when(pl.program_id(2) == 0)
        def _():
            o_ref[...] = jnp.zeros_like(o_ref)
        o_ref[...] += _dot_tn(a_ref[...].astype(BF16), b_ref[...].astype(BF16))

    return _call(body, name=name, grid=(*grid_kn, nl), in_specs=[a_spec, b_spec], out_specs=out_spec,
                 out_shape=out_shape, sem=("parallel", "parallel", "arbitrary"))(a, b)


def _wgrad_cols(a, b, name):
    K, N = a.shape[1], b.shape[1]
    ns = N // N_CHIPS
    if N * K * 4 <= 4 * 1024 * 1024:
        L = a.shape[0]

        def body(a_ref, b_ref, o_ref):
            @pl.when(pl.program_id(0) == 0)
            def _():
                o_ref[...] = jnp.zeros_like(o_ref)
            av = a_ref[...].astype(BF16)
            for s in range(N_CHIPS):
                o_ref[s] += _dot_tn(av, b_ref[:, s * ns:(s + 1) * ns].astype(BF16))

        return _call(body, name=name, grid=(L // TL_WGRAD,),
                     in_specs=[pl.BlockSpec((TL_WGRAD, K), lambda t: (t, 0)),
                               pl.BlockSpec((TL_WGRAD, N), lambda t: (t, 0))],
                     out_specs=pl.BlockSpec((N_CHIPS, K, ns), lambda t: (0, 0, 0)),
                     out_shape=_sds((N_CHIPS, K, ns), F32), sem=("arbitrary",))(a, b)
    tn = ns // 2 if ns % (2 * LANES) == 0 else ns
    nj = ns // tn
    return _wgrad(a, b, name=name, grid_kn=(1, N_CHIPS * nj),
                  a_spec=pl.BlockSpec((TL_WGRAD, K), lambda i, j, t: (t, 0)),
                  b_spec=pl.BlockSpec((TL_WGRAD, tn), lambda i, j, t: (t, j)),
                  out_shape=_sds((N_CHIPS, K, ns), F32),
                  out_spec=pl.BlockSpec((None, K, tn), lambda i, j, t: (j // nj, 0, j % nj)))


def _wgrad_full(a, b, name):
    K, N = a.shape[1], b.shape[1]
    return _wgrad(a, b, name=name, grid_kn=(1, 1),
                  a_spec=pl.BlockSpec((TL_WGRAD, K), lambda i, j, t: (t, 0)),
                  b_spec=pl.BlockSpec((TL_WGRAD, N), lambda i, j, t: (t, 0)),
                  out_shape=_sds((K, N), F32), out_spec=pl.BlockSpec((K, N), lambda i, j, t: (0, 0)))


def _wgrad_ff_cols(a, b, name):
    K, fs = a.shape[1], b.shape[2]
    return _wgrad(a, b, name=name, grid_kn=(1, N_CHIPS),
                  a_spec=pl.BlockSpec((TL_WGRAD, K), lambda i, j, t: (t, 0)),
                  b_spec=pl.BlockSpec((None, TL_WGRAD, fs), lambda i, j, t: (j, t, 0)),
                  out_shape=_sds((N_CHIPS, K, fs), F32),
                  out_spec=pl.BlockSpec((None, K, fs), lambda i, j, t: (j, 0, 0)))


def _wgrad_ff_rows(a, b, name):
    fs, N = a.shape[2], b.shape[1]
    return _wgrad(a, b, name=name, grid_kn=(N_CHIPS, 1),
                  a_spec=pl.BlockSpec((None, TL_WGRAD, fs), lambda i, j, t: (i, t, 0)),
                  b_spec=pl.BlockSpec((TL_WGRAD, N), lambda i, j, t: (t, 0)),
                  out_shape=_sds((N_CHIPS, fs, N), F32),
                  out_spec=pl.BlockSpec((None, fs, N), lambda i, j, t: (i, 0, 0)))


def _mix_bwd(dx, z, aout, sa, sb, ypre, w_ap, w_ga, w_gb, w_out, rider=None):
    L = dx.shape[0]
    cs = D_MODEL // N_CHIPS
    ga_col = (3 * N_GROUPS * ATTN_WIDTH + SSM_WIDTH) // D_MODEL

    def body(dx_ref, ga_ref, gs_ref, aout_ref, sa_ref, sb_ref, ypre_ref, wap_ref, wga_ref, wgb_ref, wout_ref,
             dgates_ref, da_ref, gy_ref, daout_ref, dsa_ref, dsb_ref):
        dmix = _dot_nt(dx_ref[...].astype(BF16), wout_ref[...])
        sig_a = _sigmoid(ga_ref[...].astype(F32))
        sig_s = _sigmoid(gs_ref[...].astype(F32))
        a_out = aout_ref[...].astype(F32)
        s_a = sa_ref[...].astype(F32)
        sig_b = _sigmoid(sb_ref[...].astype(F32))
        s_out = s_a * sig_b
        daout = (dmix * sig_a).astype(BF16)
        daout_ref[...] = daout
        dgates_ref[:, :D_MODEL] = (dmix * a_out * sig_a * (1.0 - sig_a)).astype(BF16)
        dgates_ref[:, D_MODEL:] = (dmix * s_out * sig_s * (1.0 - sig_s)).astype(BF16)
        ds_out = dmix * sig_s
        dsa = (ds_out * sig_b).astype(BF16)
        dsb = (ds_out * s_a * sig_b * (1.0 - sig_b)).astype(BF16)
        dsa_ref[...] = dsa
        dsb_ref[...] = dsb
        da = jnp.zeros((TM_MIX, ATTN_WIDTH), F32)
        dy = jnp.zeros((TM_MIX, SSM_WIDTH), F32)
        for s in range(N_CHIPS):
            cols = slice(s * cs, (s + 1) * cs)
            da += _dot_nt(daout[:, cols], wap_ref[s])
            dy += _dot_nt(dsa[:, cols], wga_ref[s]) + _dot_nt(dsb[:, cols], wgb_ref[s])
        gy_ref[...] = dy * _gelu_grad(ypre_ref[...])
        da_ref[...] = da

    tok = lambda w: pl.BlockSpec((TM_MIX, w), lambda i: (i, 0))
    wsm = pl.BlockSpec((N_CHIPS, ATTN_WIDTH, cs), lambda i: (0, 0, 0))
    return _call(
        body, name="mix_bwd", grid=(L // TM_MIX,),
        in_specs=[tok(D_MODEL), pl.BlockSpec((TM_MIX, D_MODEL), lambda i: (i, ga_col)),
                  pl.BlockSpec((TM_MIX, D_MODEL), lambda i: (i, ga_col + 1)),
                  tok(D_MODEL), tok(D_MODEL), tok(D_MODEL), tok(SSM_WIDTH),
                  wsm, wsm, wsm, pl.BlockSpec((D_MODEL, D_MODEL), lambda i: (0, 0))],
        out_specs=[tok(2 * D_MODEL), tok(ATTN_WIDTH), tok(SSM_WIDTH)] + [tok(D_MODEL)] * 3,
        out_shape=[_sds((L, 2 * D_MODEL), BF16), _sds((L, ATTN_WIDTH), F32), _sds((L, SSM_WIDTH), F32)]
                  + [_sds((L, D_MODEL), BF16)] * 3,
        sem=("parallel",), rider=rider)(dx, z, z, aout, sa, sb, ypre, w_ap, w_ga, w_gb,
                                        w_out.reshape(D_MODEL, D_MODEL))


def _combine_bwd(da, os_, ls_):
    L = da.shape[0]

    def body(da_ref, o0, o1, o2, l0, l1, l2, hs_ref, p1_ref, p2_ref, pt1_ref, pt2_ref,
             do0, do1, do2, c0, c1, c2):
        o_tok, l_tok = _to_token_order((o0[...], o1[...], o2[...]), (l0[...], l1[...], l2[...]),
                                       (None, pt1_ref[...], pt2_ref[...]))
        w = _combine_weights(*l_tok)
        dav = da_ref[...]
        hs = hs_ref[...]
        tbar = sum(wg * _head_sum(dav * og, hs) for wg, og in zip(w, o_tok))
        for wg, pm, do_ref, c_ref in zip(w, (None, p1_ref[...], p2_ref[...]), (do0, do1, do2), (c0, c1, c2)):
            dog = (wg * dav).astype(BF16)
            cg = -wg * tbar
            do_ref[...] = dog if pm is None else _dot(pm, dog).astype(BF16)
            c_ref[...] = cg if pm is None else _dot(pm, cg.astype(BF16))

    tile = pl.BlockSpec((DL_TILE, ATTN_WIDTH), lambda i: (i, 0))
    mat = pl.BlockSpec((DL_TILE, DL_TILE), lambda i: (0, 0))
    p1, p2 = _perm_matrix(ATTN_PATTERN[1][1]), _perm_matrix(ATTN_PATTERN[2][1])
    outs = _call(body, name="combine_bwd", grid=(L // DL_TILE,), in_specs=[tile] * 7 + [mat] * 5,
                 out_specs=[tile] * 6,
                 out_shape=[_sds((L, ATTN_WIDTH), BF16)] * 3 + [_sds((L, ATTN_WIDTH), F32)] * 3,
                 sem=("parallel",))(da, *os_, *ls_, _head_sum_matrix(), p1, p2, p1.T, p2.T)
    return outs[:3], outs[3:]


def _attn_bwd(qs, ks, v, do, l, c, gi, rider=None):
    L = qs.shape[0]
    _, d = ATTN_PATTERN[gi]
    nb = L // (d * BLK)

    def body(q0_ref, q1_ref, k_ref, v_ref, do0_ref, do1_ref, l0_ref, l1_ref, c0_ref, c1_ref,
             dq_ref, dk_ref, dv_ref, carry):
        n = pl.program_id(1)

        @pl.when(n == 0)
        def _():
            carry[...] = jnp.zeros_like(carry)

        qi = lax.broadcasted_iota(jnp.int32, (2 * BLK, BLK), 0)
        kj = lax.broadcasted_iota(jnp.int32, (2 * BLK, BLK), 1)
        first = qi < BLK
        mask = jnp.logical_and(jnp.where(first, qi, kj) >= jnp.where(first, kj, qi - BLK),
                               qi < jnp.where(n < nb - 1, 2 * BLK, BLK))
        q2 = jnp.concatenate([_dl_read(q0_ref), _dl_read(q1_ref)], axis=0)
        do2 = jnp.concatenate([_dl_read(do0_ref), _dl_read(do1_ref)], axis=0)
        l2 = jnp.concatenate([_dl_read(l0_ref), _dl_read(l1_ref)], axis=0)
        c2 = jnp.concatenate([_dl_read(c0_ref), _dl_read(c1_ref)], axis=0)
        k = _dl_read(k_ref)
        v_ = _dl_read(v_ref)
        h0, h1 = _pair_masks()
        mask2 = jnp.concatenate([mask, mask], axis=1)
        dq_parts, dk_parts, dv_parts = [], [], []
        for hp in range(N_HEADS // 2):
            ls = slice(hp * LANES, (hp + 1) * LANES)
            qp, dop, kp_, vp_ = q2[:, ls], do2[:, ls], k[:, ls], v_[:, ls]
            kk = jnp.concatenate([jnp.where(h0, kp_, 0), jnp.where(h1, kp_, 0)], axis=0)
            vv = jnp.concatenate([jnp.where(h0, vp_, 0), jnp.where(h1, vp_, 0)], axis=0)

            def per_head(t):
                a = jnp.broadcast_to(t[:, hp * LANES:hp * LANES + 1], (2 * BLK, BLK))
                b = jnp.broadcast_to(t[:, hp * LANES + HEAD_DIM:hp * LANES + HEAD_DIM + 1], (2 * BLK, BLK))
                return jnp.concatenate([a, b], axis=1)

            p = jnp.where(mask2, jnp.exp(_dot_nt(qp, kk) - per_head(l2)), 0.0)
            ds = (p * (_dot_nt(dop, vv) + per_head(c2))).astype(BF16)
            dv2 = _dot_tn(p.astype(BF16), dop)
            dk2 = _dot_tn(ds, qp)
            dq2 = _dot(ds, kk)
            dq_parts.append((dq2[:BLK] + carry[:, ls]).astype(BF16))
            carry[:, ls] = dq2[BLK:]
            dk_parts.append(jnp.where(h0, dk2[:BLK], dk2[BLK:]).astype(BF16))
            dv_parts.append(jnp.where(h0, dv2[:BLK], dv2[BLK:]).astype(BF16))
        _dl_write(dq_ref, jnp.concatenate(dq_parts, axis=1))
        _dl_write(dk_ref, jnp.concatenate(dk_parts, axis=1))
        _dl_write(dv_ref, jnp.concatenate(dv_parts, axis=1))

    cur = _dl_spec(d, ATTN_WIDTH, lambda n: n)
    nxt = _dl_spec(d, ATTN_WIDTH, lambda n: jnp.minimum(n + 1, nb - 1))
    view = lambda t: _dl_view(t, d)
    outs = _call(
        body, name=f"attn_bwd_g{gi}", grid=(d, nb),
        in_specs=[cur, nxt, cur, cur, cur, nxt, cur, nxt, cur, nxt], out_specs=[cur, cur, cur],
        out_shape=[_sds(view(qs).shape, BF16)] * 3, scratch=[pltpu.VMEM((BLK, ATTN_WIDTH), F32)],
        sem=("parallel", "arbitrary"), rider=rider)(view(qs), view(qs), view(ks), view(v), view(do), view(do),
                                                    view(l), view(l), view(c), view(c))
    return [t.reshape(L, ATTN_WIDTH) for t in outs[:3]], list(outs[3:])


def _qkv_post(z, dqkv, du, dgates, gq_t, gk_t):
    L = z.shape[0]
    qkv_w = N_GROUPS * ATTN_WIDTH

    def body(zq_ref, zk_ref, gq_ref, gk_ref, hs_ref, pt1_ref, pt2_ref, du_ref, dgates_ref, *rest):
        dl_refs, (dz_ref, dgq_ref, dgk_ref) = rest[:9], rest[9:]

        @pl.when(pl.program_id(0) == 0)
        def _():
            dgq_ref[...] = jnp.zeros_like(dgq_ref)
            dgk_ref[...] = jnp.zeros_like(dgk_ref)

        hs = hs_ref[...]
        pts = (None, pt1_ref[...], pt2_ref[...])

        def rows8(t):
            return jnp.sum(t.reshape(DL_TILE // SUBLANES, SUBLANES, ATTN_WIDTH), axis=0)

        def norm_bwd(x, gain, dn):
            r = lax.rsqrt(_head_sum(x * x, hs) * (1.0 / HEAD_DIM) + EPS)
            xh = x * r
            dh = dn * gain
            return r * (dh - xh * (_head_sum(dh * xh, hs) * (1.0 / HEAD_DIM))), rows8(dn * xh)

        for g in range(N_GROUPS):
            tok = [t[...].astype(F32) if pts[g] is None else _dot(pts[g], t[...]) for t in dl_refs[3 * g:3 * g + 3]]
            cols = slice(g * ATTN_WIDTH, (g + 1) * ATTN_WIDTH)
            dq, pq = norm_bwd(zq_ref[:, cols].astype(F32), gq_ref[...] * SCALE, tok[0])
            dk, pk_ = norm_bwd(zk_ref[:, cols].astype(F32), gk_ref[...], tok[1])
            dgq_ref[...] += pq * SCALE
            dgk_ref[...] += pk_
            dz_ref[:, cols] = dq.astype(BF16)
            dz_ref[:, qkv_w + g * ATTN_WIDTH:qkv_w + (g + 1) * ATTN_WIDTH] = dk.astype(BF16)
            dz_ref[:, 2 * qkv_w + g * ATTN_WIDTH:2 * qkv_w + (g + 1) * ATTN_WIDTH] = tok[2].astype(BF16)
        dz_ref[:, 3 * qkv_w:3 * qkv_w + SSM_WIDTH] = du_ref[...]
        dz_ref[:, 3 * qkv_w + SSM_WIDTH:] = dgates_ref[...]

    tile = lambda w: pl.BlockSpec((DL_TILE, w), lambda i: (i, 0))
    mat = pl.BlockSpec((DL_TILE, DL_TILE), lambda i: (0, 0))
    vec = pl.BlockSpec((1, ATTN_WIDTH), lambda i: (0, 0))
    acc = pl.BlockSpec((SUBLANES, ATTN_WIDTH), lambda i: (0, 0))
    flat = [t for grp in dqkv for t in grp]
    return _call(
        body, name="qkv_post", grid=(L // DL_TILE,),
        in_specs=[tile(qkv_w), pl.BlockSpec((DL_TILE, qkv_w), lambda i: (i, 1)), vec, vec, mat, mat, mat,
                  tile(SSM_WIDTH), tile(2 * D_MODEL)] + [tile(ATTN_WIDTH)] * 9,
        out_specs=[tile(IN_COLS), acc, acc],
        out_shape=[_sds((L, IN_COLS), BF16), _sds((SUBLANES, ATTN_WIDTH), F32), _sds((SUBLANES, ATTN_WIDTH), F32)],
        sem=("arbitrary",))(z, z, gq_t, gk_t, _head_sum_matrix(), _perm_matrix(ATTN_PATTERN[1][1]).T,
                            _perm_matrix(ATTN_PATTERN[2][1]).T, du, dgates, *flat)


def _scan_rev_grad(sre, sim, rre, rim, are_ref, aim_ref, k0, init, seed_re, seed_im):
    ar = [_bcast(are_ref, k0 + kk) for kk in range(SCAN_GROUP)]
    ai = [-_bcast(aim_ref, k0 + kk) for kk in range(SCAN_GROUP)]

    def update(i, xprev, carry):
        out = []
        for kk in range(SCAN_GROUP):
            k = k0 + kk
            lr, li, dr, di = carry[4 * kk:4 * kk + 4]
            nr = ar[kk] * lr - ai[kk] * li + rre[k, _rows(i), :]
            ni = ar[kk] * li + ai[kk] * lr + rim[k, _rows(i), :]
            rre[k, _rows(i), :] = nr
            rim[k, _rows(i), :] = ni
            xr, xi = xprev(k)
            out += [nr, ni, dr + xr * nr + xi * ni, di + xr * ni - xi * nr]
        return tuple(out)

    def step(t, carry):
        i = SSM_TC - 1 - t
        return update(i, lambda k: (sre[k, _rows(i - 1), :], sim[k, _rows(i - 1), :]), carry)

    zero = jnp.zeros((SSM_SUB, LANES), F32)
    flat = []
    for re, im in init:
        flat += [re, im, zero, zero]
    res = lax.fori_loop(0, SSM_TC - 1, step, tuple(flat), unroll=3)
    res = update(0, lambda k: (seed_re[k], seed_im[k]), res)
    return [(res[4 * kk + 2], res[4 * kk + 3]) for kk in range(SCAN_GROUP)]


def _ssm_bwd(z, gy, pk, dskip, sd_re, sd_im, rider=None):
    L = z.shape[0]
    nb = L // SSM_TB
    ucol = (3 * N_GROUPS * ATTN_WIDTH) // SSM_WIDTH
    nwin = N_PAIR // PAIRS_PER_WIN

    def body(u_ref, gy_ref, are_ref, aim_ref, a64re_ref, a64im_ref, bwre_ref, bwim_ref, cwre_ref, cwim_ref, d_ref,
             sdre_ref, sdim_ref,
             du_ref, dare_ref, daim_ref, dbre_ref, dbim_ref, dcre_ref, dcim_ref, dd_ref,
             sre, sim, rre, rim, carry_re, carry_im, ends_re, ends_im, seed_re, seed_im):
        @pl.when(pl.program_id(0) == 0)
        def _():
            carry_re[...] = jnp.zeros_like(carry_re)
            carry_im[...] = jnp.zeros_like(carry_im)
            for ref in (dare_ref, daim_ref, dbre_ref, dbim_ref, dcre_ref, dcim_ref, dd_ref):
                ref[...] = jnp.zeros_like(ref)

        u = u_ref[...]
        gyv = gy_ref[...]
        gyb = gyv.astype(BF16)
        _ssm_fill(u, bwre_ref, bwim_ref, sre, sim)
        for k2 in range(N_PAIR):
            gw = gyb[:, _win_cols(k2)]
            _to_slabs(rre, k2, _dot_nt(gw, cwre_ref[k2]))
            _to_slabs(rim, k2, -_dot_nt(gw, cwim_ref[k2]))
        zero = jnp.zeros((SSM_SUB, LANES), F32)
        for k0 in range(0, N_SLAB, SCAN_GROUP):
            grp = range(k0, k0 + SCAN_GROUP)
            _scan(sre, sim, are_ref, aim_ref, k0, [(sdre_ref[k], sdim_ref[k]) for k in grp],
                  reverse=False, store=True)
            ends = _scan(rre, rim, are_ref, aim_ref, k0, [(zero, zero)] * SCAN_GROUP, reverse=True, store=False,
                         sign=-1.0)
            for kk, k in enumerate(grp):
                ends_re[k] = ends[kk][0]
                ends_im[k] = ends[kk][1]
            for k in grp:
                _ssm_seeds(ends_re, ends_im, a64re_ref, a64im_ref, carry_re, carry_im, seed_re, seed_im, k,
                           reverse=True, sign=-1.0)
            das = _scan_rev_grad(sre, sim, rre, rim, are_ref, aim_ref, k0,
                                 [(seed_re[k], seed_im[k]) for k in grp], sdre_ref, sdim_ref)
            for kk, k in enumerate(grp):
                dare_ref[k] += das[kk][0]
                daim_ref[k] += das[kk][1]
        for w in range(nwin):
            cols = _win_cols(w * PAIRS_PER_WIN)
            uw = u[:, cols]
            gw = gyb[:, cols]
            acc = gyv[:, cols] * d_ref[:, cols]
            for kk in range(PAIRS_PER_WIN):
                k2 = w * PAIRS_PER_WIN + kk
                lr = _pair_rows(rre, k2)
                li = _pair_rows(rim, k2)
                acc += _dot_nt(lr, bwre_ref[k2]) + _dot_nt(li, bwim_ref[k2])
                dbre_ref[k2] += _dot_tn(uw, lr)
                dbim_ref[k2] += _dot_tn(uw, li)
                dcre_ref[k2] += _dot_tn(_pair_rows(sre, k2), gw)
                dcim_ref[k2] -= _dot_tn(_pair_rows(sim, k2), gw)
            du_ref[:, cols] = acc.astype(BF16)
        dd_ref[...] += jnp.sum((gyv * u.astype(F32)).reshape(SSM_TB // SUBLANES, SUBLANES, SSM_WIDTH), axis=0)

    c2, c3 = _ssm_specs_consts()
    rev = lambda b: nb - 1 - b
    seed_spec = pl.BlockSpec((None, N_SLAB, SSM_SUB, LANES), lambda b: (rev(b), 0, 0, 0))
    tile_out = pl.BlockSpec((N_SLAB, SSM_SUB, LANES), lambda b: (0, 0, 0))
    small = pltpu.VMEM((N_SLAB, LANES), F32)
    tile = pltpu.VMEM((N_SLAB, SSM_SUB, LANES), F32)
    return _call(
        body, name="ssm_bwd", grid=(nb,),
        in_specs=[pl.BlockSpec((SSM_TB, SSM_WIDTH), lambda b: (rev(b), ucol)),
                  pl.BlockSpec((SSM_TB, SSM_WIDTH), lambda b: (rev(b), 0)),
                  c2, c2, c2, c2, c3, c3, c3, c3, pl.BlockSpec((1, SSM_WIDTH), lambda b: (0, 0)),
                  seed_spec, seed_spec],
        out_specs=[pl.BlockSpec((SSM_TB, SSM_WIDTH), lambda b: (rev(b), 0)), tile_out, tile_out, c3, c3, c3, c3,
                   pl.BlockSpec((SUBLANES, SSM_WIDTH), lambda b: (0, 0))],
        out_shape=[_sds((L, SSM_WIDTH), BF16), _sds((N_SLAB, SSM_SUB, LANES), F32),
                   _sds((N_SLAB, SSM_SUB, LANES), F32)] + [_sds((N_PAIR, SSM_WIN, SSM_WIN), F32)] * 4
                  + [_sds((SUBLANES, SSM_WIDTH), F32)],
        scratch=_ssm_scratch() + _ssm_scratch() + [small, small, tile, tile, tile, tile],
        sem=("arbitrary",), rider=rider)(z, gy, pk["a_re"], pk["a_im"], pk["a64_re"], pk["a64_im"],
                            pk["bw_re"].astype(BF16), pk["bw_im"].astype(BF16),
                            pk["cw_re"].astype(BF16), pk["cw_im"].astype(BF16), dskip, sd_re, sd_im)


def _in_proj_bwd(dz, w, x, g, dres, rider=None):
    L = x.shape[0]
    ns = w.shape[2]
    tn = ns
    nj = ns // tn
    nt = N_CHIPS * nj
    TM = TM_PROJ

    def body(dz_ref, w_ref, x_ref, g_ref, dres_ref, dx_ref, dg_ref, acc, dgacc):
        i, j = pl.program_id(0), pl.program_id(1)

        @pl.when(j == 0)
        def _():
            acc[...] = jnp.zeros_like(acc)

        @pl.when(jnp.logical_and(i == 0, j == 0))
        def _():
            dgacc[...] = jnp.zeros_like(dgacc)

        acc[...] += _dot_nt(dz_ref[...], w_ref[...])

        @pl.when(j == nt - 1)
        def _():
            dx, dg = _rms_bwd(x_ref[...], g_ref[...], acc[...])
            dx_ref[...] = dres_ref[...] + dx
            dgacc[...] += dg

        @pl.when(jnp.logical_and(i == pl.num_programs(0) - 1, j == nt - 1))
        def _():
            dg_ref[...] = jnp.sum(dgacc[...], axis=0, keepdims=True)

    tok = pl.BlockSpec((TM, D_MODEL), lambda i, j: (i, 0))
    vec = pl.BlockSpec((1, D_MODEL), lambda i, j: (0, 0))
    return _call(
        body, name="in_proj_bwd", grid=(L // TM, nt),
        in_specs=[pl.BlockSpec((TM, tn), lambda i, j: (i, j)),
                  pl.BlockSpec((None, D_MODEL, tn), lambda i, j: (j // nj, 0, j % nj)), tok, vec, tok],
        out_specs=[tok, vec],
        out_shape=[_sds((L, D_MODEL), F32), _sds((1, D_MODEL), F32)],
        scratch=[pltpu.VMEM((TM, D_MODEL), F32), pltpu.VMEM((SUBLANES, D_MODEL), F32)],
        sem=("arbitrary", "arbitrary"), rider=rider)(dz, w, x, g, dres)


SSM_PARAMS = ("lambda_re", "lambda_im", "log_dt", "b_re", "b_im", "c_re", "c_im")
EARLY = ("w_ffn_gate", "w_ffn_up", "w_ffn_down")
LATE = ("w_in", "w_attn_proj", "w_glu_a", "w_glu_b", "w_out")


def _layer_bwd(dx2, sv, p, pending, owned, l, idx):
    chip_idx, c_idx = idx
    g = {}
    owned = dict(owned)

    def settle(name, partial, arrived, layer):
        owned[name] = _sum_owner(partial, arrived, owned[name], layer, chip_idx, c_idx)

    dgate, dup = _ffn_bwd_act(dx2, sv["gate"], sv["up"], p["w_ffn_down"])
    outs = _ffn_bwd_in(dx2, sv["x1"], p["g_ffn"], dgate, dup, p["w_ffn_gate"], p["w_ffn_up"],
                       _scatter_rider([pending[n] for n in LATE[1:]]) if pending else None)
    dx1, g["g_ffn"] = outs[:2]
    for n, t in zip(LATE[1:], outs[2:]):
        settle(n, pending[n], t, l + 1)
    g["w_ffn_gate"] = _wgrad_ff_cols(sv["h2"], dgate, "wgrad_ffn_gate")
    g["w_ffn_up"] = _wgrad_ff_cols(sv["h2"], dup, "wgrad_ffn_up")
    g["w_ffn_down"] = _wgrad_ff_rows(sv["act"], dx2, "wgrad_ffn_down")

    outs = _mix_bwd(dx1, sv["z"], sv["aout"], sv["sa"], sv["sb"], sv["ypre"], p["w_attn_proj"], p["w_glu_a"],
                    p["w_glu_b"], p["w_out"], _swap_rider([g[n] for n in EARLY]))
    dgates, da, gy, daout, dsa, dsb = outs[:6]
    early = [_add_half(g[n], s, c_idx) for n, s in zip(EARLY, outs[6:])]
    g["w_out"] = _wgrad_full(sv["mix"], dx1, "wgrad_out").reshape(N_CHIPS, D_MODEL // N_CHIPS, D_MODEL)
    g["w_attn_proj"] = _wgrad_cols(sv["a"], daout, "wgrad_attn_proj")
    g["w_glu_a"] = _wgrad_cols(sv["yact"], dsa, "wgrad_glu_a")
    g["w_glu_b"] = _wgrad_cols(sv["yact"], dsb, "wgrad_glu_b")

    outs = _ssm_bwd(sv["z"], gy, sv["pk"], p["d_skip"], sv["sd_re"], sv["sd_im"],
                    _scatter_rider([pending[LATE[0]]]) if pending else None)
    du, da_re, da_im, dbw_re, dbw_im, dcw_re, dcw_im, dd = outs[:8]
    if pending:
        settle(LATE[0], pending[LATE[0]], outs[8], l + 1)
    g["d_skip"] = jnp.sum(dd, axis=0, keepdims=True)
    g["ssm_pack_ct"] = dict(a_re=da_re, a_im=da_im, bw_re=dbw_re, bw_im=dbw_im, cw_re=dcw_re, cw_im=dcw_im)

    dos, cs = _combine_bwd(da, sv["os"], sv["ls"])
    dqkv = []
    for gi in range(N_GROUPS):
        grads, arrived = _attn_bwd(*sv["qkv"][gi], dos[gi], sv["ls"][gi], cs[gi], gi, _scatter_rider([early[gi]]))
        settle(EARLY[gi], early[gi], arrived[0], l)
        dqkv.append(grads)
    dz, gq8, gk8 = _qkv_post(sv["z"], dqkv, du, dgates, jnp.tile(p["g_q"], (1, N_HEADS)),
                             jnp.tile(p["g_k"], (1, N_HEADS)))
    g["g_q"] = jnp.sum(gq8.reshape(SUBLANES * N_HEADS, HEAD_DIM), axis=0, keepdims=True)
    g["g_k"] = jnp.sum(gk8.reshape(SUBLANES * N_HEADS, HEAD_DIM), axis=0, keepdims=True)
    g["w_in"] = _wgrad_cols(sv["h"], dz, "wgrad_in")
    outs = _in_proj_bwd(dz, p["w_in"], sv["x"], p["g_mix"], dx1, _swap_rider([g[n] for n in LATE]))
    dx, g["g_mix"] = outs[:2]
    late = {n: _add_half(g[n], s, c_idx) for n, s in zip(LATE, outs[2:])}
    return dx, g, late, owned


def _place():
    x, y, c = lax.axis_index("x"), lax.axis_index("y"), lax.axis_index("c")
    others = [(1 - x, y), (x, 1 - y), (1 - x, 1 - y)]
    return x, y, c, others


def _half(ref, hc):
    rows = ref.shape[-2] // 2
    idx = (slice(None),) * (len(ref.shape) - 2) + (pl.ds(hc * rows, rows), slice(None))
    return ref.at[idx]


def _comm_call(body, name, ins, out_shapes, n_remote, aliases=None):
    scratch = [pltpu.SemaphoreType.DMA((n_remote,)), pltpu.SemaphoreType.DMA((n_remote,))]
    return pl.pallas_call(
        body, name=name, in_specs=[ANY] * len(ins), out_specs=[ANY] * len(out_shapes), out_shape=out_shapes,
        scratch_shapes=scratch, input_output_aliases=aliases or {})(*ins)


def _cast_place(w, l, chip_idx):
    _, R, C = w.shape
    tr = R // 2

    def body(me_ref, w_ref, o_ref):
        o_ref[...] = w_ref[...].astype(BF16)

    return _call(body, name=f"cast_place_l{l}", grid=(R // tr,), prefetch=1,
                 in_specs=[pl.BlockSpec((None, tr, C), lambda i, me_ref: (l, i, 0))],
                 out_specs=pl.BlockSpec((None, tr, C), lambda i, me_ref: (me_ref[0], i, 0)),
                 out_shape=_sds((N_CHIPS, R, C), BF16), sem=("arbitrary",))(chip_idx, w)


def _in_place_rider(bufs, pairs, per_buf=3):
    n = len(bufs)

    def copies(outs, send, recv, side):
        return [pltpu.make_async_remote_copy(src_ref=pair[side][0], dst_ref=pair[side][0], send_sem=send.at[k],
                                             recv_sem=recv.at[k], device_id=pair[side][1], device_id_type=MESH)
                for k, pair in enumerate(pairs(outs))]

    def start(ins, outs, send, recv):
        for cp in copies(outs, send, recv, 0):
            cp.start()

    def wait(ins, outs, send, recv):
        for cp in copies(outs, send, recv, 1):
            cp.wait_recv()
        for cp in copies(outs, send, recv, 0):
            cp.wait_send()

    return Rider(list(bufs), [_sds(b.shape, b.dtype) for b in bufs], per_buf * n, start, wait,
                 {a: a for a in range(n)})


def _gather_ici_rider(bufs):
    def pairs(outs):
        x, y, c, others = _place()
        return [((_half(o.at[2 * x + y], c), (cx, cy, c)), (_half(o.at[2 * cx + cy], c), (cx, cy, c)))
                for o in outs for cx, cy in others]
    return _in_place_rider(bufs, pairs)


def _gather_d2d_rider(bufs):
    def pairs(outs):
        x, y, c, others = _place()
        sib = (x, y, 1 - c)
        return [((_half(o.at[2 * cx + cy], c), sib), (_half(o.at[2 * cx + cy], 1 - c), sib))
                for o in outs for cx, cy in others]
    return _in_place_rider(bufs, pairs)


def _swap_rider(gs):
    n = len(gs)

    def copies(ins, outs, send, recv):
        x, y, c, _ = _place()
        return [pltpu.make_async_remote_copy(src_ref=_half(ins[a], 1 - c), dst_ref=outs[a], send_sem=send.at[a],
                                             recv_sem=recv.at[a], device_id=(x, y, 1 - c), device_id_type=MESH)
                for a in range(n)]

    def start(ins, outs, send, recv):
        for cp in copies(ins, outs, send, recv):
            cp.start()

    def wait(ins, outs, send, recv):
        for cp in copies(ins, outs, send, recv):
            cp.wait()

    outs = [_sds((g.shape[0], g.shape[1] // 2, g.shape[2]), g.dtype) for g in gs]
    return Rider(list(gs), outs, n, start, wait, {})


def _scatter_rider(ss):
    n = len(ss)

    def copies(ins, outs, send, recv):
        x, y, c, others = _place()
        return [pltpu.make_async_remote_copy(
            src_ref=ins[a].at[2 * cx + cy], dst_ref=outs[a].at[j], send_sem=send.at[3 * a + j],
            recv_sem=recv.at[3 * a + j], device_id=(cx, cy, c), device_id_type=MESH)
            for a in range(n) for j, (cx, cy) in enumerate(others)]

    def start(ins, outs, send, recv):
        for cp in copies(ins, outs, send, recv):
            cp.start()

    def wait(ins, outs, send, recv):
        for cp in copies(ins, outs, send, recv):
            cp.wait()

    outs = [_sds((N_CHIPS - 1,) + s.shape[1:], s.dtype) for s in ss]
    return Rider(list(ss), outs, 3 * n, start, wait, {})


def _run_rider(rider, name):
    n_in = len(rider.ins)

    def body(*refs):
        ins, outs = refs[:n_in], refs[n_in:n_in + len(rider.out_shapes)]
        send, recv = refs[n_in + len(rider.out_shapes):]
        rider.start(ins, outs, send, recv)
        rider.wait(ins, outs, send, recv)

    return _comm_call(body, name, rider.ins, rider.out_shapes, rider.n_sem, aliases=rider.aliases)


def _join_rider(bufs):
    def pairs(outs):
        x, y, c, _ = _place()
        sib = (x, y, 1 - c)
        return [((_half(o, c), sib), (_half(o, 1 - c), sib)) for o in outs]
    return _in_place_rider(bufs, pairs, per_buf=1)


def _place_small(v, dev_idx):
    rows, n = v.shape

    def body(idx_ref, v_ref, o_ref):
        o_ref[...] = v_ref[...]

    return _call(body, name="place_small", grid=(1,), prefetch=1,
                 in_specs=[pl.BlockSpec((rows, n), lambda i, idx_ref: (0, 0))],
                 out_specs=pl.BlockSpec((None, rows, n), lambda i, idx_ref: (idx_ref[0], 0, 0)),
                 out_shape=_sds((8, rows, n), v.dtype), sem=("arbitrary",))(dev_idx, v)


def _small_ici_rider(buf):
    def pairs(outs):
        x, y, c, others = _place()
        peers = [(x, y, 1 - c)] + [(cx, cy, c) for cx, cy in others]
        return [((outs[0].at[4 * x + 2 * y + c], peer), (outs[0].at[4 * peer[0] + 2 * peer[1] + peer[2]], peer))
                for peer in peers]
    return _in_place_rider([buf], pairs, per_buf=4)


def _small_d2d_rider(buf):
    def pairs(outs):
        x, y, c, others = _place()
        sib = (x, y, 1 - c)
        return [((outs[0].at[4 * cx + 2 * cy + c], sib), (outs[0].at[4 * cx + 2 * cy + 1 - c], sib))
                for cx, cy in others]
    return _in_place_rider([buf], pairs)


def _add_half(g, p, c):
    _, R, C = g.shape
    half = R // 2

    def body(c_ref, g_ref, p_ref, o_ref):
        o_ref[...] = g_ref[...] + p_ref[...]

    blk = (None, half, C)
    return _call(body, name="add_half", grid=(N_CHIPS,), prefetch=1,
                 in_specs=[pl.BlockSpec(blk, lambda s, c_ref: (s, c_ref[0], 0)),
                           pl.BlockSpec(blk, lambda s, c_ref: (s, 0, 0))],
                 out_specs=pl.BlockSpec(blk, lambda s, c_ref: (s, 0, 0)),
                 out_shape=_sds((N_CHIPS, half, C), F32), sem=("arbitrary",))(c, g, p)


def _sum_owner(s, q, buf, l, me, c):
    _, half, C = s.shape
    tr = half // 2

    def body(me_ref, c_ref, s_ref, q0, q1, q2, buf_ref, o_ref):
        o_ref[...] = ((s_ref[...] + q0[...]) + q1[...]) + q2[...]

    blk = (None, tr, C)
    qspec = lambda j: pl.BlockSpec(blk, lambda i, me_ref, c_ref: (j, i, 0))
    return _call(body, name=f"sum_owner_l{l}", grid=(half // tr,), prefetch=2,
                 in_specs=[pl.BlockSpec(blk, lambda i, me_ref, c_ref: (me_ref[0], i, 0)),
                           qspec(0), qspec(1), qspec(2), ANY],
                 out_specs=pl.BlockSpec(blk, lambda i, me_ref, c_ref: (l, 2 * c_ref[0] + i, 0)),
                 out_shape=_sds(buf.shape, F32), sem=("arbitrary",), aliases={6: 0})(me, c, s, q, q, q, buf)


def _adamw_math(w, g, m, v):
    m = ADAM_B1 * m + (1.0 - ADAM_B1) * g
    v = ADAM_B2 * v + (1.0 - ADAM_B2) * (g * g)
    m_hat = m / (1.0 - ADAM_B1 ** ADAM_STEP)
    v_hat = v / (1.0 - ADAM_B2 ** ADAM_STEP)
    delta = -ADAM_LR * (m_hat / (jnp.sqrt(v_hat) + ADAM_EPS) + ADAM_WD * w)
    return delta, m, v


def _adamw(w, g, m, v, rider=None):
    depth, R, C = w.shape
    tr = next(t for t in (256, 128, 64, 32) if R % t == 0)

    def body(w_ref, g_ref, m_ref, v_ref, d_ref, nm_ref, nv_ref):
        d, nm, nv = _adamw_math(w_ref[...], g_ref[...], m_ref[...], v_ref[...])
        d_ref[...] = d
        nm_ref[...] = nm
        nv_ref[...] = nv

    spec = pl.BlockSpec((None, tr, C), lambda l, i: (l, i, 0))
    return _call(body, name="adamw", grid=(depth, R // tr), in_specs=[spec] * 4, out_specs=[spec] * 3,
                 out_shape=[_sds(w.shape, F32)] * 3, sem=("parallel", "parallel"), rider=rider)(w, g, m, v)


def _small_update(gathered, w, m, v):
    _, rows, n = gathered.shape
    tr = rows // 7

    def body(ga_ref, w_ref, m_ref, v_ref, g_ref, d_ref, nm_ref, nv_ref):
        g = ga_ref[0]
        for k in range(1, 8):
            g = g + ga_ref[k]
        d, nm, nv = _adamw_math(w_ref[...], g, m_ref[...], v_ref[...])
        g_ref[...] = g
        d_ref[...] = d
        nm_ref[...] = nm
        nv_ref[...] = nv

    spec = pl.BlockSpec((tr, n), lambda i: (i, 0))
    return _call(body, name="small_update", grid=(rows // tr,),
                 in_specs=[pl.BlockSpec((8, tr, n), lambda i: (0, i, 0)), spec, spec, spec], out_specs=[spec] * 4,
                 out_shape=[_sds((rows, n), F32)] * 4, sem=("parallel",))(gathered, w, m, v)


WEIGHTS = ("g_mix", "w_in", "g_q", "g_k", "w_attn_proj", "lambda_re", "lambda_im", "log_dt", "b_re", "b_im",
           "c_re", "c_im", "d_skip", "w_glu_a", "w_glu_b", "w_out", "g_ffn", "w_ffn_gate", "w_ffn_up", "w_ffn_down")
BIG = ("w_in", "w_attn_proj", "w_glu_a", "w_glu_b", "w_out", "w_ffn_gate", "w_ffn_up", "w_ffn_down")
SMALL = tuple(n for n in WEIGHTS if n not in BIG)
ROW_VECTORS = ("g_mix", "g_q", "g_k", "d_skip", "g_ffn")
PACK_QUANTUM = LANES * SUBLANES * 7


def _pack_small(parts, extra):
    flat = jnp.concatenate([parts[n].reshape(-1).astype(F32) for n in SMALL] + [extra.reshape(-1)])
    pad = -flat.shape[0] % PACK_QUANTUM
    return jnp.pad(flat, (0, pad)).reshape(-1, LANES)


def _unpack_small(packed, like):
    flat = packed.reshape(-1)
    out, at = {}, 0
    for n in SMALL:
        size = math.prod(like[n].shape)
        out[n] = flat[at:at + size].reshape(like[n].shape)
        at += size
    return out, flat[at]


def kernel(x, g_mix, w_in, g_q, g_k, w_attn_proj, lambda_re, lambda_im, log_dt, b_re, b_im, c_re, c_im, d_skip, w_glu_a, w_glu_b, w_out, g_ffn, w_ffn_gate, w_ffn_up, w_ffn_down, loss_target, m_g_mix, m_w_in, m_g_q, m_g_k, m_w_attn_proj, m_lambda_re, m_lambda_im, m_log_dt, m_b_re, m_b_im, m_c_re, m_c_im, m_d_skip, m_w_glu_a, m_w_glu_b, m_w_out, m_g_ffn, m_w_ffn_gate, m_w_ffn_up, m_w_ffn_down, v_g_mix, v_w_in, v_g_q, v_g_k, v_w_attn_proj, v_lambda_re, v_lambda_im, v_log_dt, v_b_re, v_b_im, v_c_re, v_c_im, v_d_skip, v_w_glu_a, v_w_glu_b, v_w_out, v_g_ffn, v_w_ffn_gate, v_w_ffn_up, v_w_ffn_down):
    given = dict(locals())
    W = {n: given[n] for n in WEIGHTS}
    M = {n: given["m_" + n] for n in WEIGHTS}
    V = {n: given["v_" + n] for n in WEIGHTS}
    depth = g_mix.shape[0]
    xl = x.reshape(x.shape[-2:])
    target = loss_target.reshape(loss_target.shape[-2:])
    c_idx = lax.axis_index("c").astype(jnp.int32).reshape(1)
    chip_idx = (2 * lax.axis_index("x") + lax.axis_index("y")).astype(jnp.int32).reshape(1)

    place = lambda l: [_cast_place(W[n], l, chip_idx) for n in BIG]
    bufs = place(0)
    w_in = _run_rider(_gather_d2d_rider(_run_rider(_gather_ici_rider(bufs[:1]), "gather_ici")), "gather_d2d")[0]
    rest, stage = bufs[1:], "ici"
    params, saved, h = [], [], xl
    for l in range(depth):
        p = {"w_in": w_in}
        for n in SMALL:
            p[n] = W[n][l][None] if n in ROW_VECTORS else W[n][l]
        h, sv, p, nxt = _layer_fwd(h, p, rest, stage, place(l + 1) if l + 1 < depth else None)
        params.append(p)
        saved.append(sv)
        if nxt:
            (w_in, rest), stage = nxt, "d2d"
    dx, loss_part = _loss_head(h, target)

    owned = {n: lax.empty(W[n].shape, F32) for n in BIG}
    small_grads = [None] * depth
    pending = None
    for l in reversed(range(depth)):
        dx, small_grads[l], pending, owned = _layer_bwd(dx, saved[l], params[l], pending, owned, l,
                                                        (chip_idx, c_idx))

    ct = {k: jnp.stack([small_grads[l]["ssm_pack_ct"][k] for l in range(depth)])
          for k in small_grads[0]["ssm_pack_ct"]}
    ct["a_re"], ct["a_im"] = jnp.sum(ct["a_re"], axis=2), jnp.sum(ct["a_im"], axis=2)
    ct["a64_re"] = ct["a64_im"] = jnp.zeros_like(ct["a_re"])
    _, pull = jax.vjp(jax.vmap(_ssm_pack), *[W[n] for n in SSM_PARAMS])
    stacked = dict(zip(SSM_PARAMS, pull(ct)))
    for n in SMALL:
        if n not in stacked:
            stacked[n] = jnp.stack([small_grads[l][n] for l in range(depth)])
    zero = jnp.zeros((1,), F32)
    dev_idx = (4 * lax.axis_index("x") + 2 * lax.axis_index("y") + lax.axis_index("c")).astype(jnp.int32).reshape(1)
    gathered = _place_small(_pack_small(stacked, loss_part), dev_idx)
    outs = _run_rider(_join_riders(_scatter_rider([pending[n] for n in LATE]), _small_ici_rider(gathered)),
                      "scatter_to_owners")
    for n, q in zip(LATE, outs[:len(LATE)]):
        owned[n] = _sum_owner(pending[n], q, owned[n], 0, chip_idx, c_idx)
    outs = _run_rider(_join_riders(_join_rider([owned[n] for n in BIG]), _small_d2d_rider(outs[len(LATE)])),
                      "join_halves")
    reduced, gathered = dict(zip(BIG, outs[:len(BIG)])), outs[len(BIG)]
    grads, delta, new_m, new_v = {}, {}, {}, {}
    for n in BIG:
        grads[n] = reduced[n]
        delta[n], new_m[n], new_v[n] = _adamw(W[n], reduced[n], M[n], V[n])
    gs, ds, nms, nvs = _small_update(gathered, _pack_small(W, zero), _pack_small(M, zero), _pack_small(V, zero))
    sg, loss = _unpack_small(gs, W)
    sd, _ = _unpack_small(ds, W)
    sm, _ = _unpack_small(nms, W)
    sv_, _ = _unpack_small(nvs, W)
    for n in SMALL:
        grads[n], delta[n], new_m[n], new_v[n] = sg[n], sd[n], sm[n], sv_[n]

    return (loss, dx.reshape(x.shape), *[grads[n] for n in WEIGHTS], *[delta[n] for n in WEIGHTS],
            *[new_m[n] for n in WEIGHTS], *[new_v[n] for n in WEIGHTS])
```

```python
import collections
import functools
import math

import jax
import jax.numpy as jnp
from jax import lax
from jax.experimental import pallas as pl
from jax.experimental.pallas import tpu as pltpu

F32 = jnp.float32
BF16 = jnp.bfloat16

D_MODEL = 1024
DEPTH = 4
HEAD_DIM = 64
N_HEADS = 8
ATTN_WIDTH = N_HEADS * HEAD_DIM
ATTN_PATTERN = ((128, 1), (512, 4), (2048, 16))
N_GROUPS = len(ATTN_PATTERN)
BLK = 128
SSM_WIDTH = 512
SSM_GROUP = 16
SSM_GROUPS = 32
SSM_STATE = 64
D_FF = 2816
IN_COLS = 7168
EPS = 1e-6
ADAM_LR, ADAM_B1, ADAM_B2, ADAM_EPS, ADAM_WD, ADAM_STEP = 0.001, 0.9, 0.999, 1e-08, 0.01, 10

N_CHIPS = 4
MESH = pl.DeviceIdType.MESH

LANES = 128
SUBLANES = 8
VMEM_LIMIT = 56 * 1024 * 1024

TM = 512
TM_PROJ = 1024
TL_WGRAD = 2048
TM_MIX = 512

SSM_TB = 512
SSM_TC = 64
SSM_SUB = SUBLANES
SSM_PITCH = 72
N_SLAB = SSM_GROUPS * SSM_STATE // LANES
SSM_WIN = 256
N_PAIR = N_SLAB // 2
PAIRS_PER_WIN = 4
SCAN_GROUP = 4


def _params(sem=None, collective=False):
    return pltpu.CompilerParams(dimension_semantics=sem, vmem_limit_bytes=VMEM_LIMIT)


ANY = pl.BlockSpec(memory_space=pl.ANY)

Rider = collections.namedtuple("Rider", "ins out_shapes n_sem start wait aliases")


class _SemWindow:
    def __init__(self, ref, offset):
        self.ref, self.offset = ref, offset

    @property
    def at(self):
        return self

    def __getitem__(self, k):
        return self.ref.at[self.offset + k]


def _join_riders(*riders):
    riders = [r for r in riders if r is not None]
    if len(riders) <= 1:
        return riders[0] if riders else None

    def each(fn_name):
        def run(ins, outs, send, recv):
            i = o = s = 0
            for r in riders:
                getattr(r, fn_name)(ins[i:i + len(r.ins)], outs[o:o + len(r.out_shapes)],
                                    _SemWindow(send, s), _SemWindow(recv, s))
                i, o, s = i + len(r.ins), o + len(r.out_shapes), s + r.n_sem
        return run

    aliases, i, o = {}, 0, 0
    for r in riders:
        aliases.update({i + a: o + b for a, b in r.aliases.items()})
        i, o = i + len(r.ins), o + len(r.out_shapes)
    return Rider([t for r in riders for t in r.ins], [t for r in riders for t in r.out_shapes],
                 sum(r.n_sem for r in riders), each("start"), each("wait"), aliases)


def _with_rider(body, rider, grid, prefetch, n_in, n_out, n_scratch):
    n_rin, n_rout = len(rider.ins), len(rider.out_shapes)

    def hosted(*refs):
        pre, rest = refs[:prefetch], refs[prefetch:]
        ins, rin = rest[:n_in], rest[n_in:n_in + n_rin]
        o0 = n_in + n_rin
        outs, rout = rest[o0:o0 + n_out], rest[o0 + n_out:o0 + n_out + n_rout]
        s0 = o0 + n_out + n_rout
        scr, (send, recv) = rest[s0:s0 + n_scratch], rest[s0 + n_scratch:]
        first = functools.reduce(jnp.logical_and, [pl.program_id(k) == 0 for k in range(len(grid))])
        last = functools.reduce(jnp.logical_and, [pl.program_id(k) == grid[k] - 1 for k in range(len(grid))])

        @pl.when(first)
        def _():
            rider.start(rin, rout, send, recv)

        body(*pre, *ins, *outs, *scr)

        @pl.when(last)
        def _():
            rider.wait(rin, rout, send, recv)

    return hosted


def _call(body, *, name, grid, in_specs, out_specs, out_shape, scratch=(), sem=None, aliases=None,
          prefetch=0, rider=None):
    if rider is not None:
        single = not isinstance(out_specs, (list, tuple))
        out_specs = [out_specs] if single else list(out_specs)
        out_shape = [out_shape] if single else list(out_shape)
        body = _with_rider(body, rider, grid, prefetch, len(in_specs), len(out_specs), len(scratch))
        aliases = dict(aliases or {})
        aliases.update({prefetch + len(in_specs) + k: len(out_specs) + v for k, v in rider.aliases.items()})
        in_specs = list(in_specs) + [ANY] * len(rider.ins)
        out_specs = out_specs + [ANY] * len(rider.out_shapes)
        out_shape = out_shape + list(rider.out_shapes)
        scratch = list(scratch) + [pltpu.SemaphoreType.DMA((rider.n_sem,)), pltpu.SemaphoreType.DMA((rider.n_sem,))]
        sem = ("arbitrary",) * len(grid)
        fn = _call(body, name=name + "_host", grid=grid, in_specs=in_specs, out_specs=out_specs, out_shape=out_shape,
                   scratch=scratch, sem=sem, aliases=aliases, prefetch=prefetch)
        return lambda *args: fn(*args, *rider.ins)
    kw = {}
    if aliases:
        kw["input_output_aliases"] = aliases
    if prefetch:
        gs = pltpu.PrefetchScalarGridSpec(num_scalar_prefetch=prefetch, grid=grid, in_specs=in_specs,
                                          out_specs=out_specs, scratch_shapes=list(scratch))
        return pl.pallas_call(body, name=name, grid_spec=gs, out_shape=out_shape,
                              compiler_params=_params(sem), **kw)
    return pl.pallas_call(body, name=name, grid=grid, in_specs=in_specs, out_specs=out_specs,
                          out_shape=out_shape, scratch_shapes=list(scratch),
                          compiler_params=_params(sem), **kw)


def _sds(shape, dtype):
    return jax.ShapeDtypeStruct(shape, dtype)


def _sigmoid(v):
    return 1.0 / (1.0 + jnp.exp(-v))


def _dot(a, b):
    return jnp.dot(a, b, preferred_element_type=F32)


def _dot_nt(a, b):
    return lax.dot_general(a, b, (((1,), (1,)), ((), ())), preferred_element_type=F32)


def _dot_tn(a, b):
    return lax.dot_general(a, b, (((0,), (0,)), ((), ())), preferred_element_type=F32)


def _in_proj_fwd(x, g, w, rider=None):
    L = x.shape[0]
    ns = w.shape[2]
    tn = ns
    nj = ns // tn
    TM = TM_PROJ

    def body(x_ref, g_ref, w_ref, z_ref, h_ref):
        @pl.when(pl.program_id(1) == 0)
        def _():
            xv = x_ref[...]
            r = lax.rsqrt(jnp.mean(xv * xv, axis=-1, keepdims=True) + EPS)
            h_ref[...] = (xv * r * g_ref[...]).astype(BF16)
        z_ref[...] = _dot(h_ref[...], w_ref[...]).astype(BF16)

    return _call(
        body, name="in_proj_fwd", grid=(L // TM, N_CHIPS * nj),
        in_specs=[pl.BlockSpec((TM, D_MODEL), lambda i, j: (i, 0)),
                  pl.BlockSpec((1, D_MODEL), lambda i, j: (0, 0)),
                  pl.BlockSpec((None, D_MODEL, tn), lambda i, j: (j // nj, 0, j % nj))],
        out_specs=[pl.BlockSpec((TM, tn), lambda i, j: (i, j)),
                   pl.BlockSpec((TM, D_MODEL), lambda i, j: (i, 0))],
        out_shape=[_sds((L, N_CHIPS * ns), BF16), _sds((L, D_MODEL), BF16)],
        sem=("parallel", "arbitrary"), rider=rider)(x, g, w)


DL_TILE = 512
SCALE = HEAD_DIM ** -0.5


def _perm_matrix(d):
    rho = jnp.arange(DL_TILE)
    src = rho // (DL_TILE // d) + d * (rho % (DL_TILE // d))
    return (src[:, None] == jnp.arange(DL_TILE)[None, :]).astype(BF16)


def _head_sum_matrix():
    h = jnp.arange(ATTN_WIDTH) // HEAD_DIM
    return (h[:, None] == h[None, :]).astype(BF16)


def _split(v):
    hi = v.astype(BF16)
    return hi, (v - hi.astype(F32)).astype(BF16)


def _head_sum(v, hs):
    vb = v.astype(BF16)
    half = ATTN_WIDTH // 2
    blk = hs[:half, :half]
    return jnp.concatenate([_dot(vb[:, :half], blk), _dot(vb[:, half:], blk)], axis=1)


def _permute(pm, v):
    hi, lo = _split(v)
    return _dot(pm, hi) + _dot(pm, lo)


def _dl_view(t, d):
    if d * BLK <= DL_TILE:
        return t
    return t.reshape(t.shape[0] // DL_TILE, d, DL_TILE // d, t.shape[1])


def _dl_spec(d, width, which):
    if d * BLK <= DL_TILE:
        per_tile = DL_TILE // (d * BLK)
        return pl.BlockSpec((BLK, width), lambda r, n: ((which(n) // per_tile) * (DL_TILE // BLK)
                                                       + r * per_tile + which(n) % per_tile, 0))
    tiles = d * BLK // DL_TILE
    return pl.BlockSpec((tiles, None, DL_TILE // d, width), lambda r, n: (which(n), r, 0, 0))


def _dl_read(ref):
    v = ref[...]
    return v if v.ndim == 2 else v.reshape(BLK, v.shape[-1])


def _dl_write(ref, v):
    ref[...] = v if len(ref.shape) == 2 else v.reshape(ref.shape)


def _qkv_prep(z, gq_t, gk_t, rider=None):
    L = z.shape[0]
    qkv_w = N_GROUPS * ATTN_WIDTH

    def body(zq_ref, zk_ref, zv_ref, gq_ref, gk_ref, hs_ref, p1_ref, p2_ref, *outs):
        hs = hs_ref[...]
        perms = (None, p1_ref[...], p2_ref[...])
        for g in range(N_GROUPS):
            cols = slice(g * ATTN_WIDTH, (g + 1) * ATTN_WIDTH)
            xq = zq_ref[:, cols].astype(F32)
            xk = zk_ref[:, cols].astype(F32)
            rq = lax.rsqrt(_head_sum(xq * xq, hs) * (1.0 / HEAD_DIM) + EPS)
            rk = lax.rsqrt(_head_sum(xk * xk, hs) * (1.0 / HEAD_DIM) + EPS)
            vals = [(xq * rq * (gq_ref[...] * SCALE)).astype(BF16), (xk * rk * gk_ref[...]).astype(BF16),
                    zv_ref[:, cols]]
            for j, t in enumerate(vals):
                if perms[g] is not None:
                    t = _dot(perms[g], t).astype(BF16)
                outs[3 * g + j][...] = t

    tile = pl.BlockSpec((DL_TILE, ATTN_WIDTH), lambda i: (i, 0))
    mat = pl.BlockSpec((DL_TILE, DL_TILE), lambda i: (0, 0))
    vec = pl.BlockSpec((1, ATTN_WIDTH), lambda i: (0, 0))
    outs = _call(
        body, name="qkv_prep", grid=(L // DL_TILE,),
        in_specs=[pl.BlockSpec((DL_TILE, qkv_w), lambda i: (i, 0)), pl.BlockSpec((DL_TILE, qkv_w), lambda i: (i, 1)),
                  pl.BlockSpec((DL_TILE, qkv_w), lambda i: (i, 2)), vec, vec, mat, mat, mat],
        out_specs=[tile] * 9, out_shape=[_sds((L, ATTN_WIDTH), BF16)] * 9,
        sem=("parallel",), rider=rider)(z, z, z, gq_t, gk_t, _head_sum_matrix(), _perm_matrix(ATTN_PATTERN[1][1]),
                                        _perm_matrix(ATTN_PATTERN[2][1]))
    return [tuple(outs[3 * g:3 * g + 3]) for g in range(N_GROUPS)], list(outs[3 * N_GROUPS:])


def _pair_masks():
    lane = lax.broadcasted_iota(jnp.int32, (1, LANES), 1)
    return lane < HEAD_DIM, lane >= HEAD_DIM


def _attn_fwd(qs, ks, v, gi):
    L = qs.shape[0]
    _, d = ATTN_PATTERN[gi]
    nb = L // (d * BLK)

    def body(q_ref, kc_ref, kp_ref, vc_ref, vp_ref, o_ref, l_ref):
        n = pl.program_id(1)
        qi = lax.broadcasted_iota(jnp.int32, (BLK, 2 * BLK), 0)
        kj = lax.broadcasted_iota(jnp.int32, (BLK, 2 * BLK), 1)
        prev = kj < BLK
        mask = jnp.logical_and(jnp.where(prev, kj, qi) >= jnp.where(prev, qi, kj - BLK),
                               kj >= jnp.where(n > 0, 0, BLK))
        q = _dl_read(q_ref)
        kw = jnp.concatenate([_dl_read(kp_ref), _dl_read(kc_ref)], axis=0)
        vw = jnp.concatenate([_dl_read(vp_ref), _dl_read(vc_ref)], axis=0)
        one = jnp.ones((2 * BLK, LANES), BF16)
        o_parts, l_parts = [], []
        for hp in range(N_HEADS // 2):
            ls = slice(hp * LANES, (hp + 1) * LANES)
            qp, kp_, vp_ = q[:, ls], kw[:, ls], vw[:, ls]
            num = jnp.zeros((BLK, LANES), F32)
            den = jnp.zeros((BLK, LANES), F32)
            mb = jnp.zeros((BLK, LANES), F32)
            for he in _pair_masks():
                s = jnp.where(mask, _dot_nt(jnp.where(he, qp, 0), kp_), -jnp.inf)
                m = jnp.max(s, axis=-1, keepdims=True)
                p = jnp.exp(s - m).astype(BF16)
                acc = _dot(p, jnp.concatenate([jnp.where(he, vp_, 0), jnp.where(he, one, 0)], axis=1))
                num += acc[:, :LANES]
                den += acc[:, LANES:]
                mb = jnp.where(he, m, mb)
            o_parts.append((num / den).astype(BF16))
            l_parts.append(mb + jnp.log(den))
        _dl_write(o_ref, jnp.concatenate(o_parts, axis=1))
        _dl_write(l_ref, jnp.concatenate(l_parts, axis=1))

    cur = _dl_spec(d, ATTN_WIDTH, lambda n: n)
    prev = _dl_spec(d, ATTN_WIDTH, lambda n: jnp.maximum(n - 1, 0))
    view = lambda t: _dl_view(t, d)
    o, l = _call(
        body, name=f"attn_fwd_g{gi}", grid=(d, nb), in_specs=[cur, cur, prev, cur, prev], out_specs=[cur, cur],
        out_shape=[_sds(view(qs).shape, BF16), _sds(view(qs).shape, F32)],
        sem=("parallel", "parallel"))(view(qs), view(ks), view(ks), view(v), view(v))
    return o.reshape(L, ATTN_WIDTH), l.reshape(L, ATTN_WIDTH)


def _to_token_order(os_, ls_, pts):
    o_tok, l_tok = [], []
    for o, l, pt in zip(os_, ls_, pts):
        if pt is None:
            o_tok.append(o.astype(F32))
            l_tok.append(l)
        else:
            o_tok.append(_dot(pt, o))
            l_tok.append(_permute(pt, l))
    return o_tok, l_tok


def _combine_fwd(os_, ls_):
    L = os_[0].shape[0]

    def body(o0, o1, o2, l0, l1, l2, pt1_ref, pt2_ref, a_ref):
        o_tok, l_tok = _to_token_order((o0[...], o1[...], o2[...]), (l0[...], l1[...], l2[...]),
                                       (None, pt1_ref[...], pt2_ref[...]))
        w = _combine_weights(*l_tok)
        a_ref[...] = (w[0] * o_tok[0] + w[1] * o_tok[1] + w[2] * o_tok[2]).astype(BF16)

    tile = pl.BlockSpec((DL_TILE, ATTN_WIDTH), lambda i: (i, 0))
    mat = pl.BlockSpec((DL_TILE, DL_TILE), lambda i: (0, 0))
    return _call(body, name="combine_fwd", grid=(L // DL_TILE,), in_specs=[tile] * 6 + [mat, mat], out_specs=tile,
                 out_shape=_sds((L, ATTN_WIDTH), BF16), sem=("parallel",))(
                     *os_, *ls_, _perm_matrix(ATTN_PATTERN[1][1]).T, _perm_matrix(ATTN_PATTERN[2][1]).T)


def _gelu(v):
    c = math.sqrt(2.0 / math.pi)
    return 0.5 * v * (1.0 + jnp.tanh(c * (v + 0.044715 * v * v * v)))


def _gelu_grad(v):
    c = math.sqrt(2.0 / math.pi)
    t = jnp.tanh(c * (v + 0.044715 * v * v * v))
    return 0.5 * (1.0 + t) + 0.5 * v * (1.0 - t * t) * c * (1.0 + 3.0 * 0.044715 * v * v)


def _ssm_fill(u, bwre_ref, bwim_ref, sre, sim):
    for k2 in range(N_PAIR):
        uw = u[:, _win_cols(k2)]
        _to_slabs(sre, k2, _dot(uw, bwre_ref[k2]))
        _to_slabs(sim, k2, _dot(uw, bwim_ref[k2]))


def _win_cols(k2):
    w = k2 // PAIRS_PER_WIN
    return slice(w * SSM_WIN, (w + 1) * SSM_WIN)


def _to_slabs(ref, k2, v):
    for half in range(2):
        for j in range(SSM_SUB):
            ref[2 * k2 + half, j * SSM_PITCH:j * SSM_PITCH + SSM_TC, :] = (
                v[j * SSM_TC:(j + 1) * SSM_TC, half * LANES:(half + 1) * LANES])


def _rows(i):
    return pl.ds(i, SSM_SUB, stride=SSM_PITCH)


def _slab_rows(ref, k):
    return jnp.concatenate([ref[k, j * SSM_PITCH:j * SSM_PITCH + SSM_TC, :] for j in range(SSM_SUB)], axis=0)


def _pair_rows(ref, k2):
    return jnp.concatenate([_slab_rows(ref, 2 * k2), _slab_rows(ref, 2 * k2 + 1)], axis=1).astype(BF16)


def _bcast(ref, k):
    return jnp.broadcast_to(ref[pl.ds(k, 1), :], (SSM_SUB, LANES))


def _scan(sre, sim, are_ref, aim_ref, k0, init, *, reverse, store, sign=1.0):
    ar = [_bcast(are_ref, k0 + kk) for kk in range(SCAN_GROUP)]
    ai = [sign * _bcast(aim_ref, k0 + kk) for kk in range(SCAN_GROUP)]

    def step(t, carry):
        i = SSM_TC - 1 - t if reverse else t
        out = []
        for kk in range(SCAN_GROUP):
            k = k0 + kk
            xr, xi = carry[2 * kk], carry[2 * kk + 1]
            nr = ar[kk] * xr - ai[kk] * xi + sre[k, _rows(i), :]
            ni = ar[kk] * xi + ai[kk] * xr + sim[k, _rows(i), :]
            if store:
                sre[k, _rows(i), :] = nr
                sim[k, _rows(i), :] = ni
            out += [nr, ni]
        return tuple(out)

    flat = []
    for re, im in init:
        flat += [re, im]
    res = lax.fori_loop(0, SSM_TC, step, tuple(flat), unroll=2)
    return [(res[2 * kk], res[2 * kk + 1]) for kk in range(SCAN_GROUP)]


def _ssm_seeds(ends_re, ends_im, a64re_ref, a64im_ref, carry_re, carry_im, seed_re, seed_im, k,
               *, reverse, sign=1.0):
    ar = a64re_ref[pl.ds(k, 1), :]
    ai = sign * a64im_ref[pl.ds(k, 1), :]
    cr = carry_re[pl.ds(k, 1), :]
    ci = carry_im[pl.ds(k, 1), :]
    order = range(SSM_SUB - 1, -1, -1) if reverse else range(SSM_SUB)
    for j in order:
        seed_re[k, pl.ds(j, 1), :] = cr
        seed_im[k, pl.ds(j, 1), :] = ci
        er = ends_re[k, pl.ds(j, 1), :]
        ei = ends_im[k, pl.ds(j, 1), :]
        cr, ci = ar * cr - ai * ci + er, ar * ci + ai * cr + ei
    carry_re[pl.ds(k, 1), :] = cr
    carry_im[pl.ds(k, 1), :] = ci


def _ssm_specs_consts():
    c2 = pl.BlockSpec((N_SLAB, LANES), lambda b: (0, 0))
    c3 = pl.BlockSpec((N_PAIR, SSM_WIN, SSM_WIN), lambda b: (0, 0, 0))
    return c2, c3


def _ssm_scratch():
    rows = SSM_SUB * SSM_PITCH
    return [pltpu.VMEM((N_SLAB, rows, LANES), F32), pltpu.VMEM((N_SLAB, rows, LANES), F32)]


def _ssm_fwd(z, pk, dskip, rider=None):
    L = z.shape[0]
    nb = L // SSM_TB
    ucol = (3 * N_GROUPS * ATTN_WIDTH) // SSM_WIDTH

    def body(u_ref, are_ref, aim_ref, a64re_ref, a64im_ref, bwre_ref, bwim_ref, cwre_ref, cwim_ref, d_ref,
             ypre_ref, yact_ref, sdre_ref, sdim_ref, sre, sim, carry_re, carry_im, ends_re, ends_im,
             seed_re, seed_im):
        @pl.when(pl.program_id(0) == 0)
        def _():
            carry_re[...] = jnp.zeros_like(carry_re)
            carry_im[...] = jnp.zeros_like(carry_im)

        u = u_ref[...]
        _ssm_fill(u, bwre_ref, bwim_ref, sre, sim)
        zero = jnp.zeros((SSM_SUB, LANES), F32)
        for k0 in range(0, N_SLAB, SCAN_GROUP):
            ends = _scan(sre, sim, are_ref, aim_ref, k0, [(zero, zero)] * SCAN_GROUP, reverse=False, store=False)
            for kk in range(SCAN_GROUP):
                ends_re[k0 + kk] = ends[kk][0]
                ends_im[k0 + kk] = ends[kk][1]
            for kk in range(SCAN_GROUP):
                _ssm_seeds(ends_re, ends_im, a64re_ref, a64im_ref, carry_re, carry_im, seed_re, seed_im,
                           k0 + kk, reverse=False)
            init = [(seed_re[k0 + kk], seed_im[k0 + kk]) for kk in range(SCAN_GROUP)]
            _scan(sre, sim, are_ref, aim_ref, k0, init, reverse=False, store=True)
        sdre_ref[...] = seed_re[...]
        sdim_ref[...] = seed_im[...]
        for w in range(N_PAIR // PAIRS_PER_WIN):
            acc = jnp.zeros((SSM_TB, SSM_WIN), F32)
            for kk in range(PAIRS_PER_WIN):
                k2 = w * PAIRS_PER_WIN + kk
                acc += _dot(_pair_rows(sre, k2), cwre_ref[k2])
                acc -= _dot(_pair_rows(sim, k2), cwim_ref[k2])
            cols = _win_cols(w * PAIRS_PER_WIN)
            ypre = acc + d_ref[:, cols] * u[:, cols].astype(F32)
            ypre_ref[:, cols] = ypre
            yact_ref[:, cols] = _gelu(ypre).astype(BF16)

    c2, c3 = _ssm_specs_consts()
    seed_spec = pl.BlockSpec((None, N_SLAB, SSM_SUB, LANES), lambda b: (b, 0, 0, 0))
    small = pltpu.VMEM((N_SLAB, LANES), F32)
    tile = pltpu.VMEM((N_SLAB, SSM_SUB, LANES), F32)
    return _call(
        body, name="ssm_fwd", grid=(nb,),
        in_specs=[pl.BlockSpec((SSM_TB, SSM_WIDTH), lambda b: (b, ucol)), c2, c2, c2, c2, c3, c3, c3, c3,
                  pl.BlockSpec((1, SSM_WIDTH), lambda b: (0, 0))],
        out_specs=[pl.BlockSpec((SSM_TB, SSM_WIDTH), lambda b: (b, 0)),
                   pl.BlockSpec((SSM_TB, SSM_WIDTH), lambda b: (b, 0)), seed_spec, seed_spec],
        out_shape=[_sds((L, SSM_WIDTH), F32), _sds((L, SSM_WIDTH), BF16),
                   _sds((nb, N_SLAB, SSM_SUB, LANES), F32), _sds((nb, N_SLAB, SSM_SUB, LANES), F32)],
        scratch=_ssm_scratch() + [small, small, tile, tile, tile, tile],
        sem=("arbitrary",), rider=rider)(z, pk["a_re"], pk["a_im"], pk["a64_re"], pk["a64_im"],
                                         pk["bw_re"].astype(BF16), pk["bw_im"].astype(BF16),
                                         pk["cw_re"].astype(BF16), pk["cw_im"].astype(BF16), dskip)


def _combine_weights(l0, l1, l2):
    m = jnp.maximum(jnp.maximum(l0, l1), l2)
    e0, e1, e2 = jnp.exp(l0 - m), jnp.exp(l1 - m), jnp.exp(l2 - m)
    inv = 1.0 / (e0 + e1 + e2)
    return e0 * inv, e1 * inv, e2 * inv


def _mix_fwd(x, z, a, yact, w_ap, w_ga, w_gb, w_out):
    L = x.shape[0]
    cs = D_MODEL // N_CHIPS
    ga_col = (3 * N_GROUPS * ATTN_WIDTH + SSM_WIDTH) // D_MODEL

    def body(x_ref, ga_ref, gs_ref, a_ref, y_ref, wap_ref, wga_ref, wgb_ref, wout_ref,
             x1_ref, aout_ref, sa_ref, sb_ref, mix_ref):
        a = a_ref[...]
        y = y_ref[...]
        for s in range(N_CHIPS):
            cols = slice(s * cs, (s + 1) * cs)
            aout_ref[:, cols] = _dot(a, wap_ref[s]).astype(BF16)
            sa_ref[:, cols] = _dot(y, wga_ref[s]).astype(BF16)
            sb_ref[:, cols] = _dot(y, wgb_ref[s]).astype(BF16)
        s_out = sa_ref[...].astype(F32) * _sigmoid(sb_ref[...].astype(F32))
        mix = (_sigmoid(ga_ref[...].astype(F32)) * aout_ref[...].astype(F32)
               + _sigmoid(gs_ref[...].astype(F32)) * s_out).astype(BF16)
        mix_ref[...] = mix
        x1_ref[...] = x_ref[...] + _dot(mix, wout_ref[...])

    tok = lambda w: pl.BlockSpec((TM_MIX, w), lambda i: (i, 0))
    wsm = pl.BlockSpec((N_CHIPS, ATTN_WIDTH, cs), lambda i: (0, 0, 0))
    return _call(
        body, name="mix_fwd", grid=(L // TM_MIX,),
        in_specs=[tok(D_MODEL), pl.BlockSpec((TM_MIX, D_MODEL), lambda i: (i, ga_col)),
                  pl.BlockSpec((TM_MIX, D_MODEL), lambda i: (i, ga_col + 1))]
                 + [tok(ATTN_WIDTH)] * 2 + [wsm, wsm, wsm, pl.BlockSpec((D_MODEL, D_MODEL), lambda i: (0, 0))],
        out_specs=[tok(D_MODEL), tok(D_MODEL), tok(D_MODEL), tok(D_MODEL), tok(D_MODEL)],
        out_shape=[_sds((L, D_MODEL), F32)] + [_sds((L, D_MODEL), BF16)] * 4,
        sem=("parallel",))(x, z, z, a, yact, w_ap, w_ga, w_gb, w_out.reshape(D_MODEL, D_MODEL))


def _ffn_fwd(x1, g, w_g, w_u, w_d, rider=None):
    L = x1.shape[0]
    fs = D_FF // N_CHIPS
    TM = TM_PROJ

    def body(x_ref, g_ref, wg_ref, wu_ref, wd_ref, x2_ref, h_ref, gate_ref, up_ref, act_ref, acc):
        s = pl.program_id(1)

        @pl.when(s == 0)
        def _():
            xv = x_ref[...]
            r = lax.rsqrt(jnp.mean(xv * xv, axis=-1, keepdims=True) + EPS)
            h_ref[...] = (xv * r * g_ref[...]).astype(BF16)
            acc[...] = jnp.zeros_like(acc)

        h = h_ref[...]
        gate = _dot_nt(h, wg_ref[...])
        up = _dot_nt(h, wu_ref[...])
        act = (gate * _sigmoid(gate) * up).astype(BF16)
        gate_ref[...] = gate.astype(BF16)
        up_ref[...] = up.astype(BF16)
        act_ref[...] = act
        acc[...] += _dot(act, wd_ref[...])

        @pl.when(s == N_CHIPS - 1)
        def _():
            x2_ref[...] = x_ref[...] + acc[...]

    tok = pl.BlockSpec((TM, D_MODEL), lambda i, s: (i, 0))
    ffs = pl.BlockSpec((None, TM, fs), lambda i, s: (s, i, 0))
    return _call(
        body, name="ffn_fwd", grid=(L // TM, N_CHIPS),
        in_specs=[tok, pl.BlockSpec((1, D_MODEL), lambda i, s: (0, 0))]
                 + [pl.BlockSpec((None, fs, D_MODEL), lambda i, s: (s, 0, 0))] * 3,
        out_specs=[tok, tok, ffs, ffs, ffs],
        out_shape=[_sds((L, D_MODEL), F32), _sds((L, D_MODEL), BF16)] + [_sds((N_CHIPS, L, fs), BF16)] * 3,
        scratch=[pltpu.VMEM((TM, D_MODEL), F32)],
        sem=("parallel", "arbitrary"), rider=rider)(x1, g, w_g, w_u, w_d)


def _loss_head(xl, target):
    L = xl.shape[0]

    def body(x_ref, t_ref, dx_ref, loss_ref, acc):
        i = pl.program_id(0)

        @pl.when(i == 0)
        def _():
            acc[...] = jnp.zeros_like(acc)

        e = x_ref[...] - t_ref[...]
        dx_ref[...] = e * (1.0 / D_MODEL)
        acc[...] += jnp.sum((e * e).reshape(TM // SUBLANES, SUBLANES, D_MODEL), axis=0)

        @pl.when(i == pl.num_programs(0) - 1)
        def _():
            loss_ref[...] = (0.5 / D_MODEL) * jnp.sum(acc[...]).reshape(1, 1)

    tok = pl.BlockSpec((TM, D_MODEL), lambda i: (i, 0))
    return _call(
        body, name="loss_head", grid=(L // TM,), in_specs=[tok, tok],
        out_specs=[tok, pl.BlockSpec((1, 1), lambda i: (0, 0))],
        out_shape=[_sds((L, D_MODEL), F32), _sds((1, 1), F32)],
        scratch=[pltpu.VMEM((SUBLANES, D_MODEL), F32)], sem=("arbitrary",))(xl, target)


def _ssm_pack(lam_re, lam_im, log_dt, b_re, b_im, c_re, c_im):
    dt = jnp.exp(log_dt)[:, None]
    mag = jnp.exp(lam_re * dt)
    ang = lam_im * dt
    ar = mag * jnp.cos(ang)
    ai = mag * jnp.sin(ang)
    nr = ar - 1.0
    ni = ai
    den = lam_re * lam_re + lam_im * lam_im
    cr = ((nr * lam_re + ni * lam_im) / den)[..., None]
    ci = ((ni * lam_re - nr * lam_im) / den)[..., None]
    bbr = cr * b_re - ci * b_im
    bbi = cr * b_im + ci * b_re
    gpp = SSM_WIN // SSM_STATE
    gpw = SSM_WIN // SSM_GROUP
    k2 = jnp.arange(N_PAIR)[:, None, None]
    gs = jnp.arange(gpp)[None, :, None]
    gl = jnp.arange(gpw)[None, None, :]
    same = (gl == gpp * (k2 % PAIRS_PER_WIN) + gs).astype(F32)

    def b_windows(bb):
        return jnp.einsum('kgl,kgpc->klcgp', same, bb.reshape(N_PAIR, gpp, SSM_STATE, SSM_GROUP)).reshape(
            N_PAIR, SSM_WIN, SSM_WIN)

    def c_windows(cc):
        return jnp.einsum('kgl,kgcp->kgplc', same, cc.reshape(N_PAIR, gpp, SSM_GROUP, SSM_STATE)).reshape(
            N_PAIR, SSM_WIN, SSM_WIN)

    pr, pi = ar, ai
    for _ in range(int(math.log2(SSM_TC))):
        pr, pi = pr * pr - pi * pi, 2.0 * pr * pi
    return dict(a_re=ar.reshape(N_SLAB, LANES), a_im=ai.reshape(N_SLAB, LANES),
                a64_re=pr.reshape(N_SLAB, LANES), a64_im=pi.reshape(N_SLAB, LANES),
                bw_re=b_windows(bbr), bw_im=b_windows(bbi), cw_re=c_windows(c_re), cw_im=c_windows(c_im))


def _layer_fwd(x, p, rest, rest_stage, next_bufs=None):
    first = {"ici": _gather_ici_rider, "d2d": _gather_d2d_rider}[rest_stage]
    outs = _in_proj_fwd(x, p["g_mix"], p["w_in"], first(rest))
    (z, h), rest = outs[:2], list(outs[2:])
    qkv, got = _qkv_prep(z, jnp.tile(p["g_q"], (1, N_HEADS)), jnp.tile(p["g_k"], (1, N_HEADS)),
                         _gather_d2d_rider(rest) if rest_stage == "ici" else None)
    p = {**p, **dict(zip(BIG[1:], got if rest_stage == "ici" else rest))}
    os_, ls_ = [], []
    for gi in range(N_GROUPS):
        o, l = _attn_fwd(*qkv[gi], gi)
        os_.append(o)
        ls_.append(l)
    a = _combine_fwd(os_, ls_)
    pk = _ssm_pack(p["lambda_re"], p["lambda_im"], p["log_dt"], p["b_re"], p["b_im"], p["c_re"], p["c_im"])
    outs = _ssm_fwd(z, pk, p["d_skip"], _gather_ici_rider(next_bufs[:1]) if next_bufs else None)
    (ypre, yact, sd_re, sd_im), next_in = outs[:4], list(outs[4:])
    x1, aout, sa, sb, mix = _mix_fwd(x, z, a, yact, p["w_attn_proj"], p["w_glu_a"], p["w_glu_b"], p["w_out"])
    outs = _ffn_fwd(x1, p["g_ffn"], p["w_ffn_gate"], p["w_ffn_up"], p["w_ffn_down"],
                    _join_riders(_gather_ici_rider(next_bufs[1:]), _gather_d2d_rider(next_in)) if next_bufs else None)
    x2, h2, gate, up, act = outs[:5]
    nxt = (outs[-1], list(outs[5:-1])) if next_bufs else None
    saved = dict(x=x, z=z, h=h, qkv=qkv, os=os_, ls=ls_, pk=pk, ypre=ypre, yact=yact, sd_re=sd_re, sd_im=sd_im,
                 x1=x1, a=a, aout=aout, sa=sa, sb=sb, mix=mix, h2=h2, gate=gate, up=up, act=act)
    return x2, saved, p, nxt


def _rms_bwd(xv, g, dh):
    r = lax.rsqrt(jnp.mean(xv * xv, axis=-1, keepdims=True) + EPS)
    xn = xv * r
    dxn = dh * g
    dx = r * (dxn - xn * jnp.mean(dxn * xn, axis=-1, keepdims=True))
    dg = jnp.sum((dh * xn).reshape(xv.shape[0] // SUBLANES, SUBLANES, xv.shape[1]), axis=0)
    return dx, dg


def _ffn_bwd_act(dx2, gate, up, w_d):
    L = dx2.shape[0]
    fs = D_FF // N_CHIPS
    TM = TM_PROJ

    def body(dx_ref, gate_ref, up_ref, wd_ref, dgate_ref, dup_ref):
        dact = _dot_nt(dx_ref[...].astype(BF16), wd_ref[...])
        gt = gate_ref[...].astype(F32)
        sg = _sigmoid(gt)
        dgate_ref[...] = (dact * up_ref[...].astype(F32) * (sg * (1.0 + gt * (1.0 - sg)))).astype(BF16)
        dup_ref[...] = (dact * gt * sg).astype(BF16)

    ffs = pl.BlockSpec((None, TM, fs), lambda i, s: (s, i, 0))
    return _call(
        body, name="ffn_bwd_act", grid=(L // TM, N_CHIPS),
        in_specs=[pl.BlockSpec((TM, D_MODEL), lambda i, s: (i, 0)), ffs, ffs,
                  pl.BlockSpec((None, fs, D_MODEL), lambda i, s: (s, 0, 0))],
        out_specs=[ffs, ffs], out_shape=[_sds((N_CHIPS, L, fs), BF16)] * 2,
        sem=("parallel", "parallel"))(dx2, gate, up, w_d)


def _ffn_bwd_in(dx2, x1, g, dgate, dup, w_g, w_u, rider=None):
    L = x1.shape[0]
    fs = D_FF // N_CHIPS
    TM = TM_PROJ

    def body(dx_ref, x_ref, g_ref, dgate_ref, dup_ref, wg_ref, wu_ref, dx1_ref, dg_ref, acc, dgacc):
        i, s = pl.program_id(0), pl.program_id(1)

        @pl.when(s == 0)
        def _():
            acc[...] = jnp.zeros_like(acc)

        @pl.when(jnp.logical_and(i == 0, s == 0))
        def _():
            dgacc[...] = jnp.zeros_like(dgacc)

        acc[...] += _dot(dgate_ref[...], wg_ref[...]) + _dot(dup_ref[...], wu_ref[...])

        @pl.when(s == N_CHIPS - 1)
        def _():
            dx, dg = _rms_bwd(x_ref[...], g_ref[...], acc[...])
            dx1_ref[...] = dx_ref[...] + dx
            dgacc[...] += dg

        @pl.when(jnp.logical_and(i == pl.num_programs(0) - 1, s == N_CHIPS - 1))
        def _():
            dg_ref[...] = jnp.sum(dgacc[...], axis=0, keepdims=True)

    tok = pl.BlockSpec((TM, D_MODEL), lambda i, s: (i, 0))
    ffs = pl.BlockSpec((None, TM, fs), lambda i, s: (s, i, 0))
    vec = pl.BlockSpec((1, D_MODEL), lambda i, s: (0, 0))
    return _call(
        body, name="ffn_bwd_in", grid=(L // TM, N_CHIPS),
        in_specs=[tok, tok, vec, ffs, ffs,
                  pl.BlockSpec((None, fs, D_MODEL), lambda i, s: (s, 0, 0)),
                  pl.BlockSpec((None, fs, D_MODEL), lambda i, s: (s, 0, 0))],
        out_specs=[tok, vec],
        out_shape=[_sds((L, D_MODEL), F32), _sds((1, D_MODEL), F32)],
        scratch=[pltpu.VMEM((TM, D_MODEL), F32), pltpu.VMEM((SUBLANES, D_MODEL), F32)],
        sem=("arbitrary", "arbitrary"), rider=rider)(dx2, x1, g, dgate, dup, w_g, w_u)


def _wgrad(a, b, *, name, grid_kn, a_spec, b_spec, out_shape, out_spec):
    L = a.shape[-2]
    nl = L // TL_WGRAD

    def body(a_ref, b_ref, o_ref):
        @pl.when(pl.program_id(2) == 0)
        def _():
            o_ref[...] = jnp.zeros_like(o_ref)
        o_ref[...] += _dot_tn(a_ref[...].astype(BF16), b_ref[...].astype(BF16))

    return _call(body, name=name, grid=(*grid_kn, nl), in_specs=[a_spec, b_spec], out_specs=out_spec,
                 out_shape=out_shape, sem=("parallel", "parallel", "arbitrary"))(a, b)


def _wgrad_cols(a, b, name):
    K, N = a.shape[1], b.shape[1]
    ns = N // N_CHIPS
    if N * K * 4 <= 4 * 1024 * 1024:
        L = a.shape[0]

        def body(a_ref, b_ref, o_ref):
            @pl.when(pl.program_id(0) == 0)
            def _():
                o_ref[...] = jnp.zeros_like(o_ref)
            av = a_ref[...].astype(BF16)
            for s in range(N_CHIPS):
                o_ref[s] += _dot_tn(av, b_ref[:, s * ns:(s + 1) * ns].astype(BF16))

        return _call(body, name=name, grid=(L // TL_WGRAD,),
                     in_specs=[pl.BlockSpec((TL_WGRAD, K), lambda t: (t, 0)),
                               pl.BlockSpec((TL_WGRAD, N), lambda t: (t, 0))],
                     out_specs=pl.BlockSpec((N_CHIPS, K, ns), lambda t: (0, 0, 0)),
                     out_shape=_sds((N_CHIPS, K, ns), F32), sem=("arbitrary",))(a, b)
    tn = ns // 2 if ns % (2 * LANES) == 0 else ns
    nj = ns // tn
    return _wgrad(a, b, name=name, grid_kn=(1, N_CHIPS * nj),
                  a_spec=pl.BlockSpec((TL_WGRAD, K), lambda i, j, t: (t, 0)),
                  b_spec=pl.BlockSpec((TL_WGRAD, tn), lambda i, j, t: (t, j)),
                  out_shape=_sds((N_CHIPS, K, ns), F32),
                  out_spec=pl.BlockSpec((None, K, tn), lambda i, j, t: (j // nj, 0, j % nj)))


def _wgrad_full(a, b, name):
    K, N = a.shape[1], b.shape[1]
    return _wgrad(a, b, name=name, grid_kn=(1, 1),
                  a_spec=pl.BlockSpec((TL_WGRAD, K), lambda i, j, t: (t, 0)),
                  b_spec=pl.BlockSpec((TL_WGRAD, N), lambda i, j, t: (t, 0)),
                  out_shape=_sds((K, N), F32), out_spec=pl.BlockSpec((K, N), lambda i, j, t: (0, 0)))


def _wgrad_ff_cols(a, b, name):
    K, fs = a.shape[1], b.shape[2]
    return _wgrad(a, b, name=name, grid_kn=(1, N_CHIPS),
                  a_spec=pl.BlockSpec((TL_WGRAD, K), lambda i, j, t: (t, 0)),
                  b_spec=pl.BlockSpec((None, TL_WGRAD, fs), lambda i, j, t: (j, t, 0)),
                  out_shape=_sds((N_CHIPS, K, fs), F32),
                  out_spec=pl.BlockSpec((None, K, fs), lambda i, j, t: (j, 0, 0)))


def _wgrad_ff_rows(a, b, name):
    fs, N = a.shape[2], b.shape[1]
    return _wgrad(a, b, name=name, grid_kn=(N_CHIPS, 1),
                  a_spec=pl.BlockSpec((None, TL_WGRAD, fs), lambda i, j, t: (i, t, 0)),
                  b_spec=pl.BlockSpec((TL_WGRAD, N), lambda i, j, t: (t, 0)),
                  out_shape=_sds((N_CHIPS, fs, N), F32),
                  out_spec=pl.BlockSpec((None, fs, N), lambda i, j, t: (i, 0, 0)))


def _mix_bwd(dx, z, aout, sa, sb, ypre, w_ap, w_ga, w_gb, w_out, rider=None):
    L = dx.shape[0]
    cs = D_MODEL // N_CHIPS
    ga_col = (3 * N_GROUPS * ATTN_WIDTH + SSM_WIDTH) // D_MODEL

    def body(dx_ref, ga_ref, gs_ref, aout_ref, sa_ref, sb_ref, ypre_ref, wap_ref, wga_ref, wgb_ref, wout_ref,
             dgates_ref, da_ref, gy_ref, daout_ref, dsa_ref, dsb_ref):
        dmix = _dot_nt(dx_ref[...].astype(BF16), wout_ref[...])
        sig_a = _sigmoid(ga_ref[...].astype(F32))
        sig_s = _sigmoid(gs_ref[...].astype(F32))
        a_out = aout_ref[...].astype(F32)
        s_a = sa_ref[...].astype(F32)
        sig_b = _sigmoid(sb_ref[...].astype(F32))
        s_out = s_a * sig_b
        daout = (dmix * sig_a).astype(BF16)
        daout_ref[...] = daout
        dgates_ref[:, :D_MODEL] = (dmix * a_out * sig_a * (1.0 - sig_a)).astype(BF16)
        dgates_ref[:, D_MODEL:] = (dmix * s_out * sig_s * (1.0 - sig_s)).astype(BF16)
        ds_out = dmix * sig_s
        dsa = (ds_out * sig_b).astype(BF16)
        dsb = (ds_out * s_a * sig_b * (1.0 - sig_b)).astype(BF16)
        dsa_ref[...] = dsa
        dsb_ref[...] = dsb
        da = jnp.zeros((TM_MIX, ATTN_WIDTH), F32)
        dy = jnp.zeros((TM_MIX, SSM_WIDTH), F32)
        for s in range(N_CHIPS):
            cols = slice(s * cs, (s + 1) * cs)
            da += _dot_nt(daout[:, cols], wap_ref[s])
            dy += _dot_nt(dsa[:, cols], wga_ref[s]) + _dot_nt(dsb[:, cols], wgb_ref[s])
        gy_ref[...] = dy * _gelu_grad(ypre_ref[...])
        da_ref[...] = da

    tok = lambda w: pl.BlockSpec((TM_MIX, w), lambda i: (i, 0))
    wsm = pl.BlockSpec((N_CHIPS, ATTN_WIDTH, cs), lambda i: (0, 0, 0))
    return _call(
        body, name="mix_bwd", grid=(L // TM_MIX,),
        in_specs=[tok(D_MODEL), pl.BlockSpec((TM_MIX, D_MODEL), lambda i: (i, ga_col)),
                  pl.BlockSpec((TM_MIX, D_MODEL), lambda i: (i, ga_col + 1)),
                  tok(D_MODEL), tok(D_MODEL), tok(D_MODEL), tok(SSM_WIDTH),
                  wsm, wsm, wsm, pl.BlockSpec((D_MODEL, D_MODEL), lambda i: (0, 0))],
        out_specs=[tok(2 * D_MODEL), tok(ATTN_WIDTH), tok(SSM_WIDTH)] + [tok(D_MODEL)] * 3,
        out_shape=[_sds((L, 2 * D_MODEL), BF16), _sds((L, ATTN_WIDTH), F32), _sds((L, SSM_WIDTH), F32)]
                  + [_sds((L, D_MODEL), BF16)] * 3,
        sem=("parallel",), rider=rider)(dx, z, z, aout, sa, sb, ypre, w_ap, w_ga, w_gb,
                                        w_out.reshape(D_MODEL, D_MODEL))


def _combine_bwd(da, os_, ls_):
    L = da.shape[0]

    def body(da_ref, o0, o1, o2, l0, l1, l2, hs_ref, p1_ref, p2_ref, pt1_ref, pt2_ref,
             do0, do1, do2, c0, c1, c2):
        o_tok, l_tok = _to_token_order((o0[...], o1[...], o2[...]), (l0[...], l1[...], l2[...]),
                                       (None, pt1_ref[...], pt2_ref[...]))
        w = _combine_weights(*l_tok)
        dav = da_ref[...]
        hs = hs_ref[...]
        tbar = sum(wg * _head_sum(dav * og, hs) for wg, og in zip(w, o_tok))
        for wg, pm, do_ref, c_ref in zip(w, (None, p1_ref[...], p2_ref[...]), (do0, do1, do2), (c0, c1, c2)):
            dog = (wg * dav).astype(BF16)
            cg = -wg * tbar
            do_ref[...] = dog if pm is None else _dot(pm, dog).astype(BF16)
            c_ref[...] = cg if pm is None else _dot(pm, cg.astype(BF16))

    tile = pl.BlockSpec((DL_TILE, ATTN_WIDTH), lambda i: (i, 0))
    mat = pl.BlockSpec((DL_TILE, DL_TILE), lambda i: (0, 0))
    p1, p2 = _perm_matrix(ATTN_PATTERN[1][1]), _perm_matrix(ATTN_PATTERN[2][1])
    outs = _call(body, name="combine_bwd", grid=(L // DL_TILE,), in_specs=[tile] * 7 + [mat] * 5,
                 out_specs=[tile] * 6,
                 out_shape=[_sds((L, ATTN_WIDTH), BF16)] * 3 + [_sds((L, ATTN_WIDTH), F32)] * 3,
                 sem=("parallel",))(da, *os_, *ls_, _head_sum_matrix(), p1, p2, p1.T, p2.T)
    return outs[:3], outs[3:]


def _attn_bwd(qs, ks, v, do, l, c, gi, rider=None):
    L = qs.shape[0]
    _, d = ATTN_PATTERN[gi]
    nb = L // (d * BLK)

    def body(q0_ref, q1_ref, k_ref, v_ref, do0_ref, do1_ref, l0_ref, l1_ref, c0_ref, c1_ref,
             dq_ref, dk_ref, dv_ref, carry):
        n = pl.program_id(1)

        @pl.when(n == 0)
        def _():
            carry[...] = jnp.zeros_like(carry)

        qi = lax.broadcasted_iota(jnp.int32, (2 * BLK, BLK), 0)
        kj = lax.broadcasted_iota(jnp.int32, (2 * BLK, BLK), 1)
        first = qi < BLK
        mask = jnp.logical_and(jnp.where(first, qi, kj) >= jnp.where(first, kj, qi - BLK),
                               qi < jnp.where(n < nb - 1, 2 * BLK, BLK))
        q2 = jnp.concatenate([_dl_read(q0_ref), _dl_read(q1_ref)], axis=0)
        do2 = jnp.concatenate([_dl_read(do0_ref), _dl_read(do1_ref)], axis=0)
        l2 = jnp.concatenate([_dl_read(l0_ref), _dl_read(l1_ref)], axis=0)
        c2 = jnp.concatenate([_dl_read(c0_ref), _dl_read(c1_ref)], axis=0)
        k = _dl_read(k_ref)
        v_ = _dl_read(v_ref)
        h0, h1 = _pair_masks()
        mask2 = jnp.concatenate([mask, mask], axis=1)
        dq_parts, dk_parts, dv_parts = [], [], []
        for hp in range(N_HEADS // 2):
            ls = slice(hp * LANES, (hp + 1) * LANES)
            qp, dop, kp_, vp_ = q2[:, ls], do2[:, ls], k[:, ls], v_[:, ls]
            kk = jnp.concatenate([jnp.where(h0, kp_, 0), jnp.where(h1, kp_, 0)], axis=0)
            vv = jnp.concatenate([jnp.where(h0, vp_, 0), jnp.where(h1, vp_, 0)], axis=0)

            def per_head(t):
                a = jnp.broadcast_to(t[:, hp * LANES:hp * LANES + 1], (2 * BLK, BLK))
                b = jnp.broadcast_to(t[:, hp * LANES + HEAD_DIM:hp * LANES + HEAD_DIM + 1], (2 * BLK, BLK))
                return jnp.concatenate([a, b], axis=1)

            p = jnp.where(mask2, jnp.exp(_dot_nt(qp, kk) - per_head(l2)), 0.0)
            ds = (p * (_dot_nt(dop, vv) + per_head(c2))).astype(BF16)
            dv2 = _dot_tn(p.astype(BF16), dop)
            dk2 = _dot_tn(ds, qp)
            dq2 = _dot(ds, kk)
            dq_parts.append((dq2[:BLK] + carry[:, ls]).astype(BF16))
            carry[:, ls] = dq2[BLK:]
            dk_parts.append(jnp.where(h0, dk2[:BLK], dk2[BLK:]).astype(BF16))
            dv_parts.append(jnp.where(h0, dv2[:BLK], dv2[BLK:]).astype(BF16))
        _dl_write(dq_ref, jnp.concatenate(dq_parts, axis=1))
        _dl_write(dk_ref, jnp.concatenate(dk_parts, axis=1))
        _dl_write(dv_ref, jnp.concatenate(dv_parts, axis=1))

    cur = _dl_spec(d, ATTN_WIDTH, lambda n: n)
    nxt = _dl_spec(d, ATTN_WIDTH, lambda n: jnp.minimum(n + 1, nb - 1))
    view = lambda t: _dl_view(t, d)
    outs = _call(
        body, name=f"attn_bwd_g{gi}", grid=(d, nb),
        in_specs=[cur, nxt, cur, cur, cur, nxt, cur, nxt, cur, nxt], out_specs=[cur, cur, cur],
        out_shape=[_sds(view(qs).shape, BF16)] * 3, scratch=[pltpu.VMEM((BLK, ATTN_WIDTH), F32)],
        sem=("parallel", "arbitrary"), rider=rider)(view(qs), view(qs), view(ks), view(v), view(do), view(do),
                                                    view(l), view(l), view(c), view(c))
    return [t.reshape(L, ATTN_WIDTH) for t in outs[:3]], list(outs[3:])


def _qkv_post(z, dqkv, du, dgates, gq_t, gk_t):
    L = z.shape[0]
    qkv_w = N_GROUPS * ATTN_WIDTH

    def body(zq_ref, zk_ref, gq_ref, gk_ref, hs_ref, pt1_ref, pt2_ref, du_ref, dgates_ref, *rest):
        dl_refs, (dz_ref, dgq_ref, dgk_ref) = rest[:9], rest[9:]

        @pl.when(pl.program_id(0) == 0)
        def _():
            dgq_ref[...] = jnp.zeros_like(dgq_ref)
            dgk_ref[...] = jnp.zeros_like(dgk_ref)

        hs = hs_ref[...]
        pts = (None, pt1_ref[...], pt2_ref[...])

        def rows8(t):
            return jnp.sum(t.reshape(DL_TILE // SUBLANES, SUBLANES, ATTN_WIDTH), axis=0)

        def norm_bwd(x, gain, dn):
            r = lax.rsqrt(_head_sum(x * x, hs) * (1.0 / HEAD_DIM) + EPS)
            xh = x * r
            dh = dn * gain
            return r * (dh - xh * (_head_sum(dh * xh, hs) * (1.0 / HEAD_DIM))), rows8(dn * xh)

        for g in range(N_GROUPS):
            tok = [t[...].astype(F32) if pts[g] is None else _dot(pts[g], t[...]) for t in dl_refs[3 * g:3 * g + 3]]
            cols = slice(g * ATTN_WIDTH, (g + 1) * ATTN_WIDTH)
            dq, pq = norm_bwd(zq_ref[:, cols].astype(F32), gq_ref[...] * SCALE, tok[0])
            dk, pk_ = norm_bwd(zk_ref[:, cols].astype(F32), gk_ref[...], tok[1])
            dgq_ref[...] += pq * SCALE
            dgk_ref[...] += pk_
            dz_ref[:, cols] = dq.astype(BF16)
            dz_ref[:, qkv_w + g * ATTN_WIDTH:qkv_w + (g + 1) * ATTN_WIDTH] = dk.astype(BF16)
            dz_ref[:, 2 * qkv_w + g * ATTN_WIDTH:2 * qkv_w + (g + 1) * ATTN_WIDTH] = tok[2].astype(BF16)
        dz_ref[:, 3 * qkv_w:3 * qkv_w + SSM_WIDTH] = du_ref[...]
        dz_ref[:, 3 * qkv_w + SSM_WIDTH:] = dgates_ref[...]

    tile = lambda w: pl.BlockSpec((DL_TILE, w), lambda i: (i, 0))
    mat = pl.BlockSpec((DL_TILE, DL_TILE), lambda i: (0, 0))
    vec = pl.BlockSpec((1, ATTN_WIDTH), lambda i: (0, 0))
    acc = pl.BlockSpec((SUBLANES, ATTN_WIDTH), lambda i: (0, 0))
    flat = [t for grp in dqkv for t in grp]
    return _call(
        body, name="qkv_post", grid=(L // DL_TILE,),
        in_specs=[tile(qkv_w), pl.BlockSpec((DL_TILE, qkv_w), lambda i: (i, 1)), vec, vec, mat, mat, mat,
                  tile(SSM_WIDTH), tile(2 * D_MODEL)] + [tile(ATTN_WIDTH)] * 9,
        out_specs=[tile(IN_COLS), acc, acc],
        out_shape=[_sds((L, IN_COLS), BF16), _sds((SUBLANES, ATTN_WIDTH), F32), _sds((SUBLANES, ATTN_WIDTH), F32)],
        sem=("arbitrary",))(z, z, gq_t, gk_t, _head_sum_matrix(), _perm_matrix(ATTN_PATTERN[1][1]).T,
                            _perm_matrix(ATTN_PATTERN[2][1]).T, du, dgates, *flat)


def _scan_rev_grad(sre, sim, rre, rim, are_ref, aim_ref, k0, init, seed_re, seed_im):
    ar = [_bcast(are_ref, k0 + kk) for kk in range(SCAN_GROUP)]
    ai = [-_bcast(aim_ref, k0 + kk) for kk in range(SCAN_GROUP)]

    def update(i, xprev, carry):
        out = []
        for kk in range(SCAN_GROUP):
            k = k0 + kk
            lr, li, dr, di = carry[4 * kk:4 * kk + 4]
            nr = ar[kk] * lr - ai[kk] * li + rre[k, _rows(i), :]
            ni = ar[kk] * li + ai[kk] * lr + rim[k, _rows(i), :]
            rre[k, _rows(i), :] = nr
            rim[k, _rows(i), :] = ni
            xr, xi = xprev(k)
            out += [nr, ni, dr + xr * nr + xi * ni, di + xr * ni - xi * nr]
        return tuple(out)

    def step(t, carry):
        i = SSM_TC - 1 - t
        return update(i, lambda k: (sre[k, _rows(i - 1), :], sim[k, _rows(i - 1), :]), carry)

    zero = jnp.zeros((SSM_SUB, LANES), F32)
    flat = []
    for re, im in init:
        flat += [re, im, zero, zero]
    res = lax.fori_loop(0, SSM_TC - 1, step, tuple(flat), unroll=3)
    res = update(0, lambda k: (seed_re[k], seed_im[k]), res)
    return [(res[4 * kk + 2], res[4 * kk + 3]) for kk in range(SCAN_GROUP)]


def _ssm_bwd(z, gy, pk, dskip, sd_re, sd_im, rider=None):
    L = z.shape[0]
    nb = L // SSM_TB
    ucol = (3 * N_GROUPS * ATTN_WIDTH) // SSM_WIDTH
    nwin = N_PAIR // PAIRS_PER_WIN

    def body(u_ref, gy_ref, are_ref, aim_ref, a64re_ref, a64im_ref, bwre_ref, bwim_ref, cwre_ref, cwim_ref, d_ref,
             sdre_ref, sdim_ref,
             du_ref, dare_ref, daim_ref, dbre_ref, dbim_ref, dcre_ref, dcim_ref, dd_ref,
             sre, sim, rre, rim, carry_re, carry_im, ends_re, ends_im, seed_re, seed_im):
        @pl.when(pl.program_id(0) == 0)
        def _():
            carry_re[...] = jnp.zeros_like(carry_re)
            carry_im[...] = jnp.zeros_like(carry_im)
            for ref in (dare_ref, daim_ref, dbre_ref, dbim_ref, dcre_ref, dcim_ref, dd_ref):
                ref[...] = jnp.zeros_like(ref)

        u = u_ref[...]
        gyv = gy_ref[...]
        gyb = gyv.astype(BF16)
        _ssm_fill(u, bwre_ref, bwim_ref, sre, sim)
        for k2 in range(N_PAIR):
            gw = gyb[:, _win_cols(k2)]
            _to_slabs(rre, k2, _dot_nt(gw, cwre_ref[k2]))
            _to_slabs(rim, k2, -_dot_nt(gw, cwim_ref[k2]))
        zero = jnp.zeros((SSM_SUB, LANES), F32)
        for k0 in range(0, N_SLAB, SCAN_GROUP):
            grp = range(k0, k0 + SCAN_GROUP)
            _scan(sre, sim, are_ref, aim_ref, k0, [(sdre_ref[k], sdim_ref[k]) for k in grp],
                  reverse=False, store=True)
            ends = _scan(rre, rim, are_ref, aim_ref, k0, [(zero, zero)] * SCAN_GROUP, reverse=True, store=False,
                         sign=-1.0)
            for kk, k in enumerate(grp):
                ends_re[k] = ends[kk][0]
                ends_im[k] = ends[kk][1]
            for k in grp:
                _ssm_seeds(ends_re, ends_im, a64re_ref, a64im_ref, carry_re, carry_im, seed_re, seed_im, k,
                           reverse=True, sign=-1.0)
            das = _scan_rev_grad(sre, sim, rre, rim, are_ref, aim_ref, k0,
                                 [(seed_re[k], seed_im[k]) for k in grp], sdre_ref, sdim_ref)
            for kk, k in enumerate(grp):
                dare_ref[k] += das[kk][0]
                daim_ref[k] += das[kk][1]
        for w in range(nwin):
            cols = _win_cols(w * PAIRS_PER_WIN)
            uw = u[:, cols]
            gw = gyb[:, cols]
            acc = gyv[:, cols] * d_ref[:, cols]
            for kk in range(PAIRS_PER_WIN):
                k2 = w * PAIRS_PER_WIN + kk
                lr = _pair_rows(rre, k2)
                li = _pair_rows(rim, k2)
                acc += _dot_nt(lr, bwre_ref[k2]) + _dot_nt(li, bwim_ref[k2])
                dbre_ref[k2] += _dot_tn(uw, lr)
                dbim_ref[k2] += _dot_tn(uw, li)
                dcre_ref[k2] += _dot_tn(_pair_rows(sre, k2), gw)
                dcim_ref[k2] -= _dot_tn(_pair_rows(sim, k2), gw)
            du_ref[:, cols] = acc.astype(BF16)
        dd_ref[...] += jnp.sum((gyv * u.astype(F32)).reshape(SSM_TB // SUBLANES, SUBLANES, SSM_WIDTH), axis=0)

    c2, c3 = _ssm_specs_consts()
    rev = lambda b: nb - 1 - b
    seed_spec = pl.BlockSpec((None, N_SLAB, SSM_SUB, LANES), lambda b: (rev(b), 0, 0, 0))
    tile_out = pl.BlockSpec((N_SLAB, SSM_SUB, LANES), lambda b: (0, 0, 0))
    small = pltpu.VMEM((N_SLAB, LANES), F32)
    tile = pltpu.VMEM((N_SLAB, SSM_SUB, LANES), F32)
    return _call(
        body, name="ssm_bwd", grid=(nb,),
        in_specs=[pl.BlockSpec((SSM_TB, SSM_WIDTH), lambda b: (rev(b), ucol)),
                  pl.BlockSpec((SSM_TB, SSM_WIDTH), lambda b: (rev(b), 0)),
                  c2, c2, c2, c2, c3, c3, c3, c3, pl.BlockSpec((1, SSM_WIDTH), lambda b: (0, 0)),
                  seed_spec, seed_spec],
        out_specs=[pl.BlockSpec((SSM_TB, SSM_WIDTH), lambda b: (rev(b), 0)), tile_out, tile_out, c3, c3, c3, c3,
                   pl.BlockSpec((SUBLANES, SSM_WIDTH), lambda b: (0, 0))],
        out_shape=[_sds((L, SSM_WIDTH), BF16), _sds((N_SLAB, SSM_SUB, LANES), F32),
                   _sds((N_SLAB, SSM_SUB, LANES), F32)] + [_sds((N_PAIR, SSM_WIN, SSM_WIN), F32)] * 4
                  + [_sds((SUBLANES, SSM_WIDTH), F32)],
        scratch=_ssm_scratch() + _ssm_scratch() + [small, small, tile, tile, tile, tile],
        sem=("arbitrary",), rider=rider)(z, gy, pk["a_re"], pk["a_im"], pk["a64_re"], pk["a64_im"],
                            pk["bw_re"].astype(BF16), pk["bw_im"].astype(BF16),
                            pk["cw_re"].astype(BF16), pk["cw_im"].astype(BF16), dskip, sd_re, sd_im)


def _in_proj_bwd(dz, w, x, g, dres, rider=None):
    L = x.shape[0]
    ns = w.shape[2]
    tn = ns
    nj = ns // tn
    nt = N_CHIPS * nj
    TM = TM_PROJ

    def body(dz_ref, w_ref, x_ref, g_ref, dres_ref, dx_ref, dg_ref, acc, dgacc):
        i, j = pl.program_id(0), pl.program_id(1)

        @pl.when(j == 0)
        def _():
            acc[...] = jnp.zeros_like(acc)

        @pl.when(jnp.logical_and(i == 0, j == 0))
        def _():
            dgacc[...] = jnp.zeros_like(dgacc)

        acc[...] += _dot_nt(dz_ref[...], w_ref[...])

        @pl.when(j == nt - 1)
        def _():
            dx, dg = _rms_bwd(x_ref[...], g_ref[...], acc[...])
            dx_ref[...] = dres_ref[...] + dx
            dgacc[...] += dg

        @pl.when(jnp.logical_and(i == pl.num_programs(0) - 1, j == nt - 1))
        def _():
            dg_ref[...] = jnp.sum(dgacc[...], axis=0, keepdims=True)

    tok = pl.BlockSpec((TM, D_MODEL), lambda i, j: (i, 0))
    vec = pl.BlockSpec((1, D_MODEL), lambda i, j: (0, 0))
    return _call(
        body, name="in_proj_bwd", grid=(L // TM, nt),
        in_specs=[pl.BlockSpec((TM, tn), lambda i, j: (i, j)),
                  pl.BlockSpec((None, D_MODEL, tn), lambda i, j: (j // nj, 0, j % nj)), tok, vec, tok],
        out_specs=[tok, vec],
        out_shape=[_sds((L, D_MODEL), F32), _sds((1, D_MODEL), F32)],
        scratch=[pltpu.VMEM((TM, D_MODEL), F32), pltpu.VMEM((SUBLANES, D_MODEL), F32)],
        sem=("arbitrary", "arbitrary"), rider=rider)(dz, w, x, g, dres)


SSM_PARAMS = ("lambda_re", "lambda_im", "log_dt", "b_re", "b_im", "c_re", "c_im")
EARLY = ("w_ffn_gate", "w_ffn_up", "w_ffn_down")
LATE = ("w_in", "w_attn_proj", "w_glu_a", "w_glu_b", "w_out")


def _layer_bwd(dx2, sv, p, pending, owned, l, idx):
    chip_idx, c_idx = idx
    g = {}
    owned = dict(owned)

    def settle(name, partial, arrived, layer):
        owned[name] = _sum_owner(partial, arrived, owned[name], layer, chip_idx, c_idx)

    dgate, dup = _ffn_bwd_act(dx2, sv["gate"], sv["up"], p["w_ffn_down"])
    outs = _ffn_bwd_in(dx2, sv["x1"], p["g_ffn"], dgate, dup, p["w_ffn_gate"], p["w_ffn_up"],
                       _scatter_rider([pending[n] for n in LATE[1:]]) if pending else None)
    dx1, g["g_ffn"] = outs[:2]
    for n, t in zip(LATE[1:], outs[2:]):
        settle(n, pending[n], t, l + 1)
    g["w_ffn_gate"] = _wgrad_ff_rows(dgate, sv["h2"], "wgrad_ffn_gate")
    g["w_ffn_up"] = _wgrad_ff_rows(dup, sv["h2"], "wgrad_ffn_up")
    g["w_ffn_down"] = _wgrad_ff_rows(sv["act"], dx2, "wgrad_ffn_down")

    outs = _mix_bwd(dx1, sv["z"], sv["aout"], sv["sa"], sv["sb"], sv["ypre"], p["w_attn_proj"], p["w_glu_a"],
                    p["w_glu_b"], p["w_out"], _swap_rider([g[n] for n in EARLY]))
    dgates, da, gy, daout, dsa, dsb = outs[:6]
    early = [_add_half(g[n], s, c_idx) for n, s in zip(EARLY, outs[6:])]
    g["w_out"] = _wgrad_full(sv["mix"], dx1, "wgrad_out").reshape(N_CHIPS, D_MODEL // N_CHIPS, D_MODEL)
    g["w_attn_proj"] = _wgrad_cols(sv["a"], daout, "wgrad_attn_proj")
    g["w_glu_a"] = _wgrad_cols(sv["yact"], dsa, "wgrad_glu_a")
    g["w_glu_b"] = _wgrad_cols(sv["yact"], dsb, "wgrad_glu_b")

    outs = _ssm_bwd(sv["z"], gy, sv["pk"], p["d_skip"], sv["sd_re"], sv["sd_im"],
                    _scatter_rider([pending[LATE[0]]]) if pending else None)
    du, da_re, da_im, dbw_re, dbw_im, dcw_re, dcw_im, dd = outs[:8]
    if pending:
        settle(LATE[0], pending[LATE[0]], outs[8], l + 1)
    g["d_skip"] = jnp.sum(dd, axis=0, keepdims=True)
    g["ssm_pack_ct"] = dict(a_re=da_re, a_im=da_im, bw_re=dbw_re, bw_im=dbw_im, cw_re=dcw_re, cw_im=dcw_im)

    dos, cs = _combine_bwd(da, sv["os"], sv["ls"])
    dqkv = []
    for gi in range(N_GROUPS):
        grads, arrived = _attn_bwd(*sv["qkv"][gi], dos[gi], sv["ls"][gi], cs[gi], gi, _scatter_rider([early[gi]]))
        settle(EARLY[gi], early[gi], arrived[0], l)
        dqkv.append(grads)
    dz, gq8, gk8 = _qkv_post(sv["z"], dqkv, du, dgates, jnp.tile(p["g_q"], (1, N_HEADS)),
                             jnp.tile(p["g_k"], (1, N_HEADS)))
    g["g_q"] = jnp.sum(gq8.reshape(SUBLANES * N_HEADS, HEAD_DIM), axis=0, keepdims=True)
    g["g_k"] = jnp.sum(gk8.reshape(SUBLANES * N_HEADS, HEAD_DIM), axis=0, keepdims=True)
    g["w_in"] = _wgrad_cols(sv["h"], dz, "wgrad_in")
    outs = _in_proj_bwd(dz, p["w_in"], sv["x"], p["g_mix"], dx1, _swap_rider([g[n] for n in LATE]))
    dx, g["g_mix"] = outs[:2]
    late = {n: _add_half(g[n], s, c_idx) for n, s in zip(LATE, outs[2:])}
    return dx, g, late, owned


def _place():
    x, y, c = lax.axis_index("x"), lax.axis_index("y"), lax.axis_index("c")
    others = [(1 - x, y), (x, 1 - y), (1 - x, 1 - y)]
    return x, y, c, others


def _half(ref, hc):
    rows = ref.shape[-2] // 2
    idx = (slice(None),) * (len(ref.shape) - 2) + (pl.ds(hc * rows, rows), slice(None))
    return ref.at[idx]


def _comm_call(body, name, ins, out_shapes, n_remote, aliases=None):
    scratch = [pltpu.SemaphoreType.DMA((n_remote,)), pltpu.SemaphoreType.DMA((n_remote,))]
    return pl.pallas_call(
        body, name=name, in_specs=[ANY] * len(ins), out_specs=[ANY] * len(out_shapes), out_shape=out_shapes,
        scratch_shapes=scratch, input_output_aliases=aliases or {})(*ins)


def _cast_place(w, l, chip_idx):
    _, R, C = w.shape
    tr = R // 2

    def body(me_ref, w_ref, o_ref):
        o_ref[...] = w_ref[...].astype(BF16)

    return _call(body, name=f"cast_place_l{l}", grid=(R // tr,), prefetch=1,
                 in_specs=[pl.BlockSpec((None, tr, C), lambda i, me_ref: (l, i, 0))],
                 out_specs=pl.BlockSpec((None, tr, C), lambda i, me_ref: (me_ref[0], i, 0)),
                 out_shape=_sds((N_CHIPS, R, C), BF16), sem=("arbitrary",))(chip_idx, w)


def _in_place_rider(bufs, pairs, per_buf=3):
    n = len(bufs)

    def copies(outs, send, recv, side):
        return [pltpu.make_async_remote_copy(src_ref=pair[side][0], dst_ref=pair[side][0], send_sem=send.at[k],
                                             recv_sem=recv.at[k], device_id=pair[side][1], device_id_type=MESH)
                for k, pair in enumerate(pairs(outs))]

    def start(ins, outs, send, recv):
        for cp in copies(outs, send, recv, 0):
            cp.start()

    def wait(ins, outs, send, recv):
        for cp in copies(outs, send, recv, 1):
            cp.wait_recv()
        for cp in copies(outs, send, recv, 0):
            cp.wait_send()

    return Rider(list(bufs), [_sds(b.shape, b.dtype) for b in bufs], per_buf * n, start, wait,
                 {a: a for a in range(n)})


def _gather_ici_rider(bufs):
    def pairs(outs):
        x, y, c, others = _place()
        return [((_half(o.at[2 * x + y], c), (cx, cy, c)), (_half(o.at[2 * cx + cy], c), (cx, cy, c)))
                for o in outs for cx, cy in others]
    return _in_place_rider(bufs, pairs)


def _gather_d2d_rider(bufs):
    def pairs(outs):
        x, y, c, others = _place()
        sib = (x, y, 1 - c)
        return [((_half(o.at[2 * cx + cy], c), sib), (_half(o.at[2 * cx + cy], 1 - c), sib))
                for o in outs for cx, cy in others]
    return _in_place_rider(bufs, pairs)


def _swap_rider(gs):
    n = len(gs)

    def copies(ins, outs, send, recv):
        x, y, c, _ = _place()
        return [pltpu.make_async_remote_copy(src_ref=_half(ins[a], 1 - c), dst_ref=outs[a], send_sem=send.at[a],
                                             recv_sem=recv.at[a], device_id=(x, y, 1 - c), device_id_type=MESH)
                for a in range(n)]

    def start(ins, outs, send, recv):
        for cp in copies(ins, outs, send, recv):
            cp.start()

    def wait(ins, outs, send, recv):
        for cp in copies(ins, outs, send, recv):
            cp.wait()

    outs = [_sds((g.shape[0], g.shape[1] // 2, g.shape[2]), g.dtype) for g in gs]
    return Rider(list(gs), outs, n, start, wait, {})


def _scatter_rider(ss):
    n = len(ss)

    def copies(ins, outs, send, recv):
        x, y, c, others = _place()
        return [pltpu.make_async_remote_copy(
            src_ref=ins[a].at[2 * cx + cy], dst_ref=outs[a].at[j], send_sem=send.at[3 * a + j],
            recv_sem=recv.at[3 * a + j], device_id=(cx, cy, c), device_id_type=MESH)
            for a in range(n) for j, (cx, cy) in enumerate(others)]

    def start(ins, outs, send, recv):
        for cp in copies(ins, outs, send, recv):
            cp.start()

    def wait(ins, outs, send, recv):
        for cp in copies(ins, outs, send, recv):
            cp.wait()

    outs = [_sds((N_CHIPS - 1,) + s.shape[1:], s.dtype) for s in ss]
    return Rider(list(ss), outs, 3 * n, start, wait, {})


def _run_rider(rider, name):
    n_in = len(rider.ins)

    def body(*refs):
        ins, outs = refs[:n_in], refs[n_in:n_in + len(rider.out_shapes)]
        send, recv = refs[n_in + len(rider.out_shapes):]
        rider.start(ins, outs, send, recv)
        rider.wait(ins, outs, send, recv)

    return _comm_call(body, name, rider.ins, rider.out_shapes, rider.n_sem, aliases=rider.aliases)


def _join_rider(bufs):
    def pairs(outs):
        x, y, c, _ = _place()
        sib = (x, y, 1 - c)
        return [((_half(o, c), sib), (_half(o, 1 - c), sib)) for o in outs]
    return _in_place_rider(bufs, pairs, per_buf=1)


def _place_small(v, dev_idx):
    rows, n = v.shape

    def body(idx_ref, v_ref, o_ref):
        o_ref[...] = v_ref[...]

    return _call(body, name="place_small", grid=(1,), prefetch=1,
                 in_specs=[pl.BlockSpec((rows, n), lambda i, idx_ref: (0, 0))],
                 out_specs=pl.BlockSpec((None, rows, n), lambda i, idx_ref: (idx_ref[0], 0, 0)),
                 out_shape=_sds((8, rows, n), v.dtype), sem=("arbitrary",))(dev_idx, v)


def _small_ici_rider(buf):
    def pairs(outs):
        x, y, c, others = _place()
        peers = [(x, y, 1 - c)] + [(cx, cy, c) for cx, cy in others]
        return [((outs[0].at[4 * x + 2 * y + c], peer), (outs[0].at[4 * peer[0] + 2 * peer[1] + peer[2]], peer))
                for peer in peers]
    return _in_place_rider([buf], pairs, per_buf=4)


def _small_d2d_rider(buf):
    def pairs(outs):
        x, y, c, others = _place()
        sib = (x, y, 1 - c)
        return [((outs[0].at[4 * cx + 2 * cy + c], sib), (outs[0].at[4 * cx + 2 * cy + 1 - c], sib))
                for cx, cy in others]
    return _in_place_rider([buf], pairs)


def _add_half(g, p, c):
    _, R, C = g.shape
    half = R // 2

    def body(c_ref, g_ref, p_ref, o_ref):
        o_ref[...] = g_ref[...] + p_ref[...]

    blk = (None, half, C)
    return _call(body, name="add_half", grid=(N_CHIPS,), prefetch=1,
                 in_specs=[pl.BlockSpec(blk, lambda s, c_ref: (s, c_ref[0], 0)),
                           pl.BlockSpec(blk, lambda s, c_ref: (s, 0, 0))],
                 out_specs=pl.BlockSpec(blk, lambda s, c_ref: (s, 0, 0)),
                 out_shape=_sds((N_CHIPS, half, C), F32), sem=("arbitrary",))(c, g, p)


def _sum_owner(s, q, buf, l, me, c):
    _, half, C = s.shape
    tr = half // 2

    def body(me_ref, c_ref, s_ref, q0, q1, q2, buf_ref, o_ref):
        o_ref[...] = ((s_ref[...] + q0[...]) + q1[...]) + q2[...]

    blk = (None, tr, C)
    qspec = lambda j: pl.BlockSpec(blk, lambda i, me_ref, c_ref: (j, i, 0))
    return _call(body, name=f"sum_owner_l{l}", grid=(half // tr,), prefetch=2,
                 in_specs=[pl.BlockSpec(blk, lambda i, me_ref, c_ref: (me_ref[0], i, 0)),
                           qspec(0), qspec(1), qspec(2), ANY],
                 out_specs=pl.BlockSpec(blk, lambda i, me_ref, c_ref: (l, 2 * c_ref[0] + i, 0)),
                 out_shape=_sds(buf.shape, F32), sem=("arbitrary",), aliases={6: 0})(me, c, s, q, q, q, buf)


def _adamw_math(w, g, m, v):
    m = ADAM_B1 * m + (1.0 - ADAM_B1) * g
    v = ADAM_B2 * v + (1.0 - ADAM_B2) * (g * g)
    m_hat = m / (1.0 - ADAM_B1 ** ADAM_STEP)
    v_hat = v / (1.0 - ADAM_B2 ** ADAM_STEP)
    delta = -ADAM_LR * (m_hat / (jnp.sqrt(v_hat) + ADAM_EPS) + ADAM_WD * w)
    return delta, m, v


def _adamw(w, g, m, v, rider=None):
    depth, R, C = w.shape
    tr = next(t for t in (256, 128, 64, 32) if R % t == 0)

    def body(w_ref, g_ref, m_ref, v_ref, d_ref, nm_ref, nv_ref):
        d, nm, nv = _adamw_math(w_ref[...], g_ref[...], m_ref[...], v_ref[...])
        d_ref[...] = d
        nm_ref[...] = nm
        nv_ref[...] = nv

    spec = pl.BlockSpec((None, tr, C), lambda l, i: (l, i, 0))
    return _call(body, name="adamw", grid=(depth, R // tr), in_specs=[spec] * 4, out_specs=[spec] * 3,
                 out_shape=[_sds(w.shape, F32)] * 3, sem=("parallel", "parallel"), rider=rider)(w, g, m, v)


def _small_update(gathered, w, m, v):
    _, rows, n = gathered.shape
    tr = rows // 7

    def body(ga_ref, w_ref, m_ref, v_ref, g_ref, d_ref, nm_ref, nv_ref):
        g = ga_ref[0]
        for k in range(1, 8):
            g = g + ga_ref[k]
        d, nm, nv = _adamw_math(w_ref[...], g, m_ref[...], v_ref[...])
        g_ref[...] = g
        d_ref[...] = d
        nm_ref[...] = nm
        nv_ref[...] = nv

    spec = pl.BlockSpec((tr, n), lambda i: (i, 0))
    return _call(body, name="small_update", grid=(rows // tr,),
                 in_specs=[pl.BlockSpec((8, tr, n), lambda i: (0, i, 0)), spec, spec, spec], out_specs=[spec] * 4,
                 out_shape=[_sds((rows, n), F32)] * 4, sem=("parallel",))(gathered, w, m, v)


WEIGHTS = ("g_mix", "w_in", "g_q", "g_k", "w_attn_proj", "lambda_re", "lambda_im", "log_dt", "b_re", "b_im",
           "c_re", "c_im", "d_skip", "w_glu_a", "w_glu_b", "w_out", "g_ffn", "w_ffn_gate", "w_ffn_up", "w_ffn_down")
BIG = ("w_in", "w_attn_proj", "w_glu_a", "w_glu_b", "w_out", "w_ffn_gate", "w_ffn_up", "w_ffn_down")
FLIPPED = ("w_ffn_gate", "w_ffn_up")
SMALL = tuple(n for n in WEIGHTS if n not in BIG)
ROW_VECTORS = ("g_mix", "g_q", "g_k", "d_skip", "g_ffn")
PACK_QUANTUM = LANES * SUBLANES * 7


def _pack_small(parts, extra):
    flat = jnp.concatenate([parts[n].reshape(-1).astype(F32) for n in SMALL] + [extra.reshape(-1)])
    pad = -flat.shape[0] % PACK_QUANTUM
    return jnp.pad(flat, (0, pad)).reshape(-1, LANES)


def _unpack_small(packed, like):
    flat = packed.reshape(-1)
    out, at = {}, 0
    for n in SMALL:
        size = math.prod(like[n].shape)
        out[n] = flat[at:at + size].reshape(like[n].shape)
        at += size
    return out, flat[at]


def kernel(x, g_mix, w_in, g_q, g_k, w_attn_proj, lambda_re, lambda_im, log_dt, b_re, b_im, c_re, c_im, d_skip, w_glu_a, w_glu_b, w_out, g_ffn, w_ffn_gate, w_ffn_up, w_ffn_down, loss_target, m_g_mix, m_w_in, m_g_q, m_g_k, m_w_attn_proj, m_lambda_re, m_lambda_im, m_log_dt, m_b_re, m_b_im, m_c_re, m_c_im, m_d_skip, m_w_glu_a, m_w_glu_b, m_w_out, m_g_ffn, m_w_ffn_gate, m_w_ffn_up, m_w_ffn_down, v_g_mix, v_w_in, v_g_q, v_g_k, v_w_attn_proj, v_lambda_re, v_lambda_im, v_log_dt, v_b_re, v_b_im, v_c_re, v_c_im, v_d_skip, v_w_glu_a, v_w_glu_b, v_w_out, v_g_ffn, v_w_ffn_gate, v_w_ffn_up, v_w_ffn_down):
    given = dict(locals())
    flip = lambda n, a: jnp.swapaxes(a, 1, 2) if n in FLIPPED else a
    W = {n: flip(n, given[n]) for n in WEIGHTS}
    M = {n: flip(n, given["m_" + n]) for n in WEIGHTS}
    V = {n: flip(n, given["v_" + n]) for n in WEIGHTS}
    depth = g_mix.shape[0]
    xl = x.reshape(x.shape[-2:])
    target = loss_target.reshape(loss_target.shape[-2:])
    c_idx = lax.axis_index("c").astype(jnp.int32).reshape(1)
    chip_idx = (2 * lax.axis_index("x") + lax.axis_index("y")).astype(jnp.int32).reshape(1)

    place = lambda l: [_cast_place(W[n], l, chip_idx) for n in BIG]
    bufs = place(0)
    w_in = _run_rider(_gather_d2d_rider(_run_rider(_gather_ici_rider(bufs[:1]), "gather_ici")), "gather_d2d")[0]
    rest, stage = bufs[1:], "ici"
    params, saved, h = [], [], xl
    for l in range(depth):
        p = {"w_in": w_in}
        for n in SMALL:
            p[n] = W[n][l][None] if n in ROW_VECTORS else W[n][l]
        h, sv, p, nxt = _layer_fwd(h, p, rest, stage, place(l + 1) if l + 1 < depth else None)
        params.append(p)
        saved.append(sv)
        if nxt:
            (w_in, rest), stage = nxt, "d2d"
    dx, loss_part = _loss_head(h, target)

    owned = {n: lax.empty(W[n].shape, F32) for n in BIG}
    small_grads = [None] * depth
    pending = None
    for l in reversed(range(depth)):
        dx, small_grads[l], pending, owned = _layer_bwd(dx, saved[l], params[l], pending, owned, l,
                                                        (chip_idx, c_idx))

    ct = {k: jnp.stack([small_grads[l]["ssm_pack_ct"][k] for l in range(depth)])
          for k in small_grads[0]["ssm_pack_ct"]}
    ct["a_re"], ct["a_im"] = jnp.sum(ct["a_re"], axis=2), jnp.sum(ct["a_im"], axis=2)
    ct["a64_re"] = ct["a64_im"] = jnp.zeros_like(ct["a_re"])
    _, pull = jax.vjp(jax.vmap(_ssm_pack), *[W[n] for n in SSM_PARAMS])
    stacked = dict(zip(SSM_PARAMS, pull(ct)))
    for n in SMALL:
        if n not in stacked:
            stacked[n] = jnp.stack([small_grads[l][n] for l in range(depth)])
    zero = jnp.zeros((1,), F32)
    dev_idx = (4 * lax.axis_index("x") + 2 * lax.axis_index("y") + lax.axis_index("c")).astype(jnp.int32).reshape(1)
    gathered = _place_small(_pack_small(stacked, loss_part), dev_idx)
    outs = _run_rider(_join_riders(_scatter_rider([pending[n] for n in LATE]), _small_ici_rider(gathered)),
                      "scatter_to_owners")
    for n, q in zip(LATE, outs[:len(LATE)]):
        owned[n] = _sum_owner(pending[n], q, owned[n], 0, chip_idx, c_idx)
    outs = _run_rider(_join_riders(_join_rider([owned[n] for n in BIG]), _small_d2d_rider(outs[len(LATE)])),
                      "join_halves")
    reduced, gathered = dict(zip(BIG, outs[:len(BIG)])), outs[len(BIG)]
    grads, delta, new_m, new_v = {}, {}, {}, {}
    for n in BIG:
        outs = (reduced[n], *_adamw(W[n], reduced[n], M[n], V[n]))
        grads[n], delta[n], new_m[n], new_v[n] = [flip(n, t) for t in outs]
    gs, ds, nms, nvs = _small_update(gathered, _pack_small(W, zero), _pack_small(M, zero), _pack_small(V, zero))
    sg, loss = _unpack_small(gs, W)
    sd, _ = _unpack_small(ds, W)
    sm, _ = _unpack_small(nms, W)
    sv_, _ = _unpack_small(nvs, W)
    for n in SMALL:
        grads[n], delta[n], new_m[n], new_v[n] = sg[n], sd[n], sm[n], sv_[n]

    return (loss, dx.reshape(x.shape), *[grads[n] for n in WEIGHTS], *[delta[n] for n in WEIGHTS],
            *[new_m[n] for n in WEIGHTS], *[new_v[n] for n in WEIGHTS])
```

```python
import collections
import functools
import math

import jax
import jax.numpy as jnp
from jax import lax
from jax.experimental import pallas as pl
from jax.experimental.pallas import tpu as pltpu

F32 = jnp.float32
BF16 = jnp.bfloat16

D_MODEL = 1024
DEPTH = 4
HEAD_DIM = 64
N_HEADS = 8
ATTN_WIDTH = N_HEADS * HEAD_DIM
ATTN_PATTERN = ((128, 1), (512, 4), (2048, 16))
N_GROUPS = len(ATTN_PATTERN)
BLK = 128
SSM_WIDTH = 512
SSM_GROUP = 16
SSM_GROUPS = 32
SSM_STATE = 64
D_FF = 2816
IN_COLS = 7168
EPS = 1e-6
ADAM_LR, ADAM_B1, ADAM_B2, ADAM_EPS, ADAM_WD, ADAM_STEP = 0.001, 0.9, 0.999, 1e-08, 0.01, 10

N_CHIPS = 4
MESH = pl.DeviceIdType.MESH

LANES = 128
SUBLANES = 8
VMEM_LIMIT = 56 * 1024 * 1024

TM = 512
TM_PROJ = 1024
TL_WGRAD = 2048
TM_MIX = 512

SSM_TB = 512
SSM_TC = 64
SSM_SUB = SUBLANES
SSM_PITCH = 68
N_SLAB = SSM_GROUPS * SSM_STATE // LANES
SSM_WIN = 256
N_PAIR = N_SLAB // 2
PAIRS_PER_WIN = 4
SCAN_GROUP = 4


def _params(sem=None, collective=False):
    return pltpu.CompilerParams(dimension_semantics=sem, vmem_limit_bytes=VMEM_LIMIT)


ANY = pl.BlockSpec(memory_space=pl.ANY)

Rider = collections.namedtuple("Rider", "ins out_shapes n_sem start wait aliases")


class _SemWindow:
    def __init__(self, ref, offset):
        self.ref, self.offset = ref, offset

    @property
    def at(self):
        return self

    def __getitem__(self, k):
        return self.ref.at[self.offset + k]


def _join_riders(*riders):
    riders = [r for r in riders if r is not None]
    if len(riders) <= 1:
        return riders[0] if riders else None

    def each(fn_name):
        def run(ins, outs, send, recv):
            i = o = s = 0
            for r in riders:
                getattr(r, fn_name)(ins[i:i + len(r.ins)], outs[o:o + len(r.out_shapes)],
                                    _SemWindow(send, s), _SemWindow(recv, s))
                i, o, s = i + len(r.ins), o + len(r.out_shapes), s + r.n_sem
        return run

    aliases, i, o = {}, 0, 0
    for r in riders:
        aliases.update({i + a: o + b for a, b in r.aliases.items()})
        i, o = i + len(r.ins), o + len(r.out_shapes)
    return Rider([t for r in riders for t in r.ins], [t for r in riders for t in r.out_shapes],
                 sum(r.n_sem for r in riders), each("start"), each("wait"), aliases)


def _with_rider(body, rider, grid, prefetch, n_in, n_out, n_scratch):
    n_rin, n_rout = len(rider.ins), len(rider.out_shapes)

    def hosted(*refs):
        pre, rest = refs[:prefetch], refs[prefetch:]
        ins, rin = rest[:n_in], rest[n_in:n_in + n_rin]
        o0 = n_in + n_rin
        outs, rout = rest[o0:o0 + n_out], rest[o0 + n_out:o0 + n_out + n_rout]
        s0 = o0 + n_out + n_rout
        scr, (send, recv) = rest[s0:s0 + n_scratch], rest[s0 + n_scratch:]
        first = functools.reduce(jnp.logical_and, [pl.program_id(k) == 0 for k in range(len(grid))])
        last = functools.reduce(jnp.logical_and, [pl.program_id(k) == grid[k] - 1 for k in range(len(grid))])

        @pl.when(first)
        def _():
            rider.start(rin, rout, send, recv)

        body(*pre, *ins, *outs, *scr)

        @pl.when(last)
        def _():
            rider.wait(rin, rout, send, recv)

    return hosted


def _call(body, *, name, grid, in_specs, out_specs, out_shape, scratch=(), sem=None, aliases=None,
          prefetch=0, rider=None):
    if rider is not None:
        single = not isinstance(out_specs, (list, tuple))
        out_specs = [out_specs] if single else list(out_specs)
        out_shape = [out_shape] if single else list(out_shape)
        body = _with_rider(body, rider, grid, prefetch, len(in_specs), len(out_specs), len(scratch))
        aliases = dict(aliases or {})
        aliases.update({prefetch + len(in_specs) + k: len(out_specs) + v for k, v in rider.aliases.items()})
        in_specs = list(in_specs) + [ANY] * len(rider.ins)
        out_specs = out_specs + [ANY] * len(rider.out_shapes)
        out_shape = out_shape + list(rider.out_shapes)
        scratch = list(scratch) + [pltpu.SemaphoreType.DMA((rider.n_sem,)), pltpu.SemaphoreType.DMA((rider.n_sem,))]
        sem = ("arbitrary",) * len(grid)
        fn = _call(body, name=name + "_host", grid=grid, in_specs=in_specs, out_specs=out_specs, out_shape=out_shape,
                   scratch=scratch, sem=sem, aliases=aliases, prefetch=prefetch)
        return lambda *args: fn(*args, *rider.ins)
    kw = {}
    if aliases:
        kw["input_output_aliases"] = aliases
    if prefetch:
        gs = pltpu.PrefetchScalarGridSpec(num_scalar_prefetch=prefetch, grid=grid, in_specs=in_specs,
                                          out_specs=out_specs, scratch_shapes=list(scratch))
        return pl.pallas_call(body, name=name, grid_spec=gs, out_shape=out_shape,
                              compiler_params=_params(sem), **kw)
    return pl.pallas_call(body, name=name, grid=grid, in_specs=in_specs, out_specs=out_specs,
                          out_shape=out_shape, scratch_shapes=list(scratch),
                          compiler_params=_params(sem), **kw)


def _sds(shape, dtype):
    return jax.ShapeDtypeStruct(shape, dtype)


def _sigmoid(v):
    return 1.0 / (1.0 + jnp.exp(-v))


def _dot(a, b):
    return jnp.dot(a, b, preferred_element_type=F32)


def _dot_nt(a, b):
    return lax.dot_general(a, b, (((1,), (1,)), ((), ())), preferred_element_type=F32)


def _dot_tn(a, b):
    return lax.dot_general(a, b, (((0,), (0,)), ((), ())), preferred_element_type=F32)


def _in_proj_fwd(x, g, w, rider=None):
    L = x.shape[0]
    ns = w.shape[2]
    tn = ns
    nj = ns // tn
    TM = TM_PROJ

    def body(x_ref, g_ref, w_ref, z_ref, h_ref):
        @pl.when(pl.program_id(1) == 0)
        def _():
            xv = x_ref[...]
            r = lax.rsqrt(jnp.mean(xv * xv, axis=-1, keepdims=True) + EPS)
            h_ref[...] = (xv * r * g_ref[...]).astype(BF16)
        z_ref[...] = _dot(h_ref[...], w_ref[...]).astype(BF16)

    return _call(
        body, name="in_proj_fwd", grid=(L // TM, N_CHIPS * nj),
        in_specs=[pl.BlockSpec((TM, D_MODEL), lambda i, j: (i, 0)),
                  pl.BlockSpec((1, D_MODEL), lambda i, j: (0, 0)),
                  pl.BlockSpec((None, D_MODEL, tn), lambda i, j: (j // nj, 0, j % nj))],
        out_specs=[pl.BlockSpec((TM, tn), lambda i, j: (i, j)),
                   pl.BlockSpec((TM, D_MODEL), lambda i, j: (i, 0))],
        out_shape=[_sds((L, N_CHIPS * ns), BF16), _sds((L, D_MODEL), BF16)],
        sem=("parallel", "arbitrary"), rider=rider)(x, g, w)


DL_TILE = 512
SCALE = HEAD_DIM ** -0.5


def _perm_matrix(d):
    rho = jnp.arange(DL_TILE)
    src = rho // (DL_TILE // d) + d * (rho % (DL_TILE // d))
    return (src[:, None] == jnp.arange(DL_TILE)[None, :]).astype(BF16)


def _head_sum_matrix():
    h = jnp.arange(ATTN_WIDTH) // HEAD_DIM
    return (h[:, None] == h[None, :]).astype(BF16)


def _split(v):
    hi = v.astype(BF16)
    return hi, (v - hi.astype(F32)).astype(BF16)


def _head_sum(v, hs):
    vb = v.astype(BF16)
    half = ATTN_WIDTH // 2
    blk = hs[:half, :half]
    return jnp.concatenate([_dot(vb[:, :half], blk), _dot(vb[:, half:], blk)], axis=1)


def _permute(pm, v):
    hi, lo = _split(v)
    return _dot(pm, hi) + _dot(pm, lo)


def _dl_view(t, d):
    if d * BLK <= DL_TILE:
        return t
    return t.reshape(t.shape[0] // DL_TILE, d, DL_TILE // d, t.shape[1])


def _dl_spec(d, width, which):
    if d * BLK <= DL_TILE:
        per_tile = DL_TILE // (d * BLK)
        return pl.BlockSpec((BLK, width), lambda r, n: ((which(n) // per_tile) * (DL_TILE // BLK)
                                                       + r * per_tile + which(n) % per_tile, 0))
    tiles = d * BLK // DL_TILE
    return pl.BlockSpec((tiles, None, DL_TILE // d, width), lambda r, n: (which(n), r, 0, 0))


def _dl_read(ref):
    v = ref[...]
    return v if v.ndim == 2 else v.reshape(BLK, v.shape[-1])


def _dl_write(ref, v):
    ref[...] = v if len(ref.shape) == 2 else v.reshape(ref.shape)


def _qkv_prep(z, gq_t, gk_t, rider=None):
    L = z.shape[0]
    qkv_w = N_GROUPS * ATTN_WIDTH

    def body(zq_ref, zk_ref, zv_ref, gq_ref, gk_ref, hs_ref, p1_ref, p2_ref, *outs):
        hs = hs_ref[...]
        perms = (None, p1_ref[...], p2_ref[...])
        for g in range(N_GROUPS):
            cols = slice(g * ATTN_WIDTH, (g + 1) * ATTN_WIDTH)
            xq = zq_ref[:, cols].astype(F32)
            xk = zk_ref[:, cols].astype(F32)
            rq = lax.rsqrt(_head_sum(xq * xq, hs) * (1.0 / HEAD_DIM) + EPS)
            rk = lax.rsqrt(_head_sum(xk * xk, hs) * (1.0 / HEAD_DIM) + EPS)
            vals = [(xq * rq * (gq_ref[...] * SCALE)).astype(BF16), (xk * rk * gk_ref[...]).astype(BF16),
                    zv_ref[:, cols]]
            for j, t in enumerate(vals):
                if perms[g] is not None:
                    t = _dot(perms[g], t).astype(BF16)
                outs[3 * g + j][...] = t

    tile = pl.BlockSpec((DL_TILE, ATTN_WIDTH), lambda i: (i, 0))
    mat = pl.BlockSpec((DL_TILE, DL_TILE), lambda i: (0, 0))
    vec = pl.BlockSpec((1, ATTN_WIDTH), lambda i: (0, 0))
    outs = _call(
        body, name="qkv_prep", grid=(L // DL_TILE,),
        in_specs=[pl.BlockSpec((DL_TILE, qkv_w), lambda i: (i, 0)), pl.BlockSpec((DL_TILE, qkv_w), lambda i: (i, 1)),
                  pl.BlockSpec((DL_TILE, qkv_w), lambda i: (i, 2)), vec, vec, mat, mat, mat],
        out_specs=[tile] * 9, out_shape=[_sds((L, ATTN_WIDTH), BF16)] * 9,
        sem=("parallel",), rider=rider)(z, z, z, gq_t, gk_t, _head_sum_matrix(), _perm_matrix(ATTN_PATTERN[1][1]),
                                        _perm_matrix(ATTN_PATTERN[2][1]))
    return [tuple(outs[3 * g:3 * g + 3]) for g in range(N_GROUPS)], list(outs[3 * N_GROUPS:])


def _pair_masks():
    lane = lax.broadcasted_iota(jnp.int32, (1, LANES), 1)
    return lane < HEAD_DIM, lane >= HEAD_DIM


def _attn_fwd(qs, ks, v, gi):
    L = qs.shape[0]
    _, d = ATTN_PATTERN[gi]
    nb = L // (d * BLK)

    def body(q_ref, kc_ref, kp_ref, vc_ref, vp_ref, o_ref, l_ref):
        n = pl.program_id(1)
        qi = lax.broadcasted_iota(jnp.int32, (BLK, 2 * BLK), 0)
        kj = lax.broadcasted_iota(jnp.int32, (BLK, 2 * BLK), 1)
        prev = kj < BLK
        mask = jnp.logical_and(jnp.where(prev, kj, qi) >= jnp.where(prev, qi, kj - BLK),
                               kj >= jnp.where(n > 0, 0, BLK))
        q = _dl_read(q_ref)
        kw = jnp.concatenate([_dl_read(kp_ref), _dl_read(kc_ref)], axis=0)
        vw = jnp.concatenate([_dl_read(vp_ref), _dl_read(vc_ref)], axis=0)
        one = jnp.ones((2 * BLK, LANES), BF16)
        o_parts, l_parts = [], []
        for hp in range(N_HEADS // 2):
            ls = slice(hp * LANES, (hp + 1) * LANES)
            qp, kp_, vp_ = q[:, ls], kw[:, ls], vw[:, ls]
            num = jnp.zeros((BLK, LANES), F32)
            den = jnp.zeros((BLK, LANES), F32)
            mb = jnp.zeros((BLK, LANES), F32)
            for he in _pair_masks():
                s = jnp.where(mask, _dot_nt(jnp.where(he, qp, 0), kp_), -jnp.inf)
                m = jnp.max(s, axis=-1, keepdims=True)
                p = jnp.exp(s - m).astype(BF16)
                acc = _dot(p, jnp.concatenate([jnp.where(he, vp_, 0), jnp.where(he, one, 0)], axis=1))
                num += acc[:, :LANES]
                den += acc[:, LANES:]
                mb = jnp.where(he, m, mb)
            o_parts.append((num / den).astype(BF16))
            l_parts.append(mb + jnp.log(den))
        _dl_write(o_ref, jnp.concatenate(o_parts, axis=1))
        _dl_write(l_ref, jnp.concatenate(l_parts, axis=1))

    cur = _dl_spec(d, ATTN_WIDTH, lambda n: n)
    prev = _dl_spec(d, ATTN_WIDTH, lambda n: jnp.maximum(n - 1, 0))
    view = lambda t: _dl_view(t, d)
    o, l = _call(
        body, name=f"attn_fwd_g{gi}", grid=(d, nb), in_specs=[cur, cur, prev, cur, prev], out_specs=[cur, cur],
        out_shape=[_sds(view(qs).shape, BF16), _sds(view(qs).shape, F32)],
        sem=("parallel", "parallel"))(view(qs), view(ks), view(ks), view(v), view(v))
    return o.reshape(L, ATTN_WIDTH), l.reshape(L, ATTN_WIDTH)


def _to_token_order(os_, ls_, pts):
    o_tok, l_tok = [], []
    for o, l, pt in zip(os_, ls_, pts):
        if pt is None:
            o_tok.append(o.astype(F32))
            l_tok.append(l)
        else:
            o_tok.append(_dot(pt, o))
            l_tok.append(_permute(pt, l))
    return o_tok, l_tok


def _combine_fwd(os_, ls_):
    L = os_[0].shape[0]

    def body(o0, o1, o2, l0, l1, l2, pt1_ref, pt2_ref, a_ref):
        o_tok, l_tok = _to_token_order((o0[...], o1[...], o2[...]), (l0[...], l1[...], l2[...]),
                                       (None, pt1_ref[...], pt2_ref[...]))
        w = _combine_weights(*l_tok)
        a_ref[...] = (w[0] * o_tok[0] + w[1] * o_tok[1] + w[2] * o_tok[2]).astype(BF16)

    tile = pl.BlockSpec((DL_TILE, ATTN_WIDTH), lambda i: (i, 0))
    mat = pl.BlockSpec((DL_TILE, DL_TILE), lambda i: (0, 0))
    return _call(body, name="combine_fwd", grid=(L // DL_TILE,), in_specs=[tile] * 6 + [mat, mat], out_specs=tile,
                 out_shape=_sds((L, ATTN_WIDTH), BF16), sem=("parallel",))(
                     *os_, *ls_, _perm_matrix(ATTN_PATTERN[1][1]).T, _perm_matrix(ATTN_PATTERN[2][1]).T)


def _gelu(v):
    c = math.sqrt(2.0 / math.pi)
    return 0.5 * v * (1.0 + jnp.tanh(c * (v + 0.044715 * v * v * v)))


def _gelu_grad(v):
    c = math.sqrt(2.0 / math.pi)
    t = jnp.tanh(c * (v + 0.044715 * v * v * v))
    return 0.5 * (1.0 + t) + 0.5 * v * (1.0 - t * t) * c * (1.0 + 3.0 * 0.044715 * v * v)


def _ssm_fill(u, bwre_ref, bwim_ref, sre, sim):
    for k2 in range(N_PAIR):
        uw = u[:, _win_cols(k2)]
        _to_slabs(sre, k2, _dot(uw, bwre_ref[k2]))
        _to_slabs(sim, k2, _dot(uw, bwim_ref[k2]))


def _win_cols(k2):
    w = k2 // PAIRS_PER_WIN
    return slice(w * SSM_WIN, (w + 1) * SSM_WIN)


def _to_slabs(ref, k2, v):
    for half in range(2):
        for j in range(SSM_SUB):
            ref[2 * k2 + half, j * SSM_PITCH:j * SSM_PITCH + SSM_TC, :] = (
                v[j * SSM_TC:(j + 1) * SSM_TC, half * LANES:(half + 1) * LANES])


def _rows(i):
    return pl.ds(i, SSM_SUB, stride=SSM_PITCH)


def _slab_rows(ref, k):
    return jnp.concatenate([ref[k, j * SSM_PITCH:j * SSM_PITCH + SSM_TC, :] for j in range(SSM_SUB)], axis=0)


def _pair_rows(ref, k2):
    return jnp.concatenate([_slab_rows(ref, 2 * k2), _slab_rows(ref, 2 * k2 + 1)], axis=1).astype(BF16)


def _bcast(ref, k):
    return jnp.broadcast_to(ref[pl.ds(k, 1), :], (SSM_SUB, LANES))


def _scan(sre, sim, are_ref, aim_ref, k0, init, *, reverse, store, sign=1.0):
    ar = [_bcast(are_ref, k0 + kk) for kk in range(SCAN_GROUP)]
    ai = [sign * _bcast(aim_ref, k0 + kk) for kk in range(SCAN_GROUP)]

    def step(t, carry):
        i = SSM_TC - 1 - t if reverse else t
        out = []
        for kk in range(SCAN_GROUP):
            k = k0 + kk
            xr, xi = carry[2 * kk], carry[2 * kk + 1]
            nr = ar[kk] * xr - ai[kk] * xi + sre[k, _rows(i), :]
            ni = ar[kk] * xi + ai[kk] * xr + sim[k, _rows(i), :]
            if store:
                sre[k, _rows(i), :] = nr
                sim[k, _rows(i), :] = ni
            out += [nr, ni]
        return tuple(out)

    flat = []
    for re, im in init:
        flat += [re, im]
    res = lax.fori_loop(0, SSM_TC, step, tuple(flat), unroll=2)
    return [(res[2 * kk], res[2 * kk + 1]) for kk in range(SCAN_GROUP)]


def _ssm_seeds(ends_re, ends_im, a64re_ref, a64im_ref, carry_re, carry_im, seed_re, seed_im, k,
               *, reverse, sign=1.0):
    ar = a64re_ref[pl.ds(k, 1), :]
    ai = sign * a64im_ref[pl.ds(k, 1), :]
    cr = carry_re[pl.ds(k, 1), :]
    ci = carry_im[pl.ds(k, 1), :]
    order = range(SSM_SUB - 1, -1, -1) if reverse else range(SSM_SUB)
    for j in order:
        seed_re[k, pl.ds(j, 1), :] = cr
        seed_im[k, pl.ds(j, 1), :] = ci
        er = ends_re[k, pl.ds(j, 1), :]
        ei = ends_im[k, pl.ds(j, 1), :]
        cr, ci = ar * cr - ai * ci + er, ar * ci + ai * cr + ei
    carry_re[pl.ds(k, 1), :] = cr
    carry_im[pl.ds(k, 1), :] = ci


def _ssm_specs_consts():
    c2 = pl.BlockSpec((N_SLAB, LANES), lambda b: (0, 0))
    c3 = pl.BlockSpec((N_PAIR, SSM_WIN, SSM_WIN), lambda b: (0, 0, 0))
    return c2, c3


def _ssm_scratch():
    rows = SSM_SUB * SSM_PITCH
    return [pltpu.VMEM((N_SLAB, rows, LANES), F32), pltpu.VMEM((N_SLAB, rows, LANES), F32)]


def _ssm_fwd(z, pk, dskip, rider=None):
    L = z.shape[0]
    nb = L // SSM_TB
    ucol = (3 * N_GROUPS * ATTN_WIDTH) // SSM_WIDTH

    def body(u_ref, are_ref, aim_ref, a64re_ref, a64im_ref, bwre_ref, bwim_ref, cwre_ref, cwim_ref, d_ref,
             ypre_ref, yact_ref, sdre_ref, sdim_ref, sre, sim, carry_re, carry_im, ends_re, ends_im,
             seed_re, seed_im):
        @pl.when(pl.program_id(0) == 0)
        def _():
            carry_re[...] = jnp.zeros_like(carry_re)
            carry_im[...] = jnp.zeros_like(carry_im)

        u = u_ref[...]
        _ssm_fill(u, bwre_ref, bwim_ref, sre, sim)
        zero = jnp.zeros((SSM_SUB, LANES), F32)
        for k0 in range(0, N_SLAB, SCAN_GROUP):
            ends = _scan(sre, sim, are_ref, aim_ref, k0, [(zero, zero)] * SCAN_GROUP, reverse=False, store=False)
            for kk in range(SCAN_GROUP):
                ends_re[k0 + kk] = ends[kk][0]
                ends_im[k0 + kk] = ends[kk][1]
            for kk in range(SCAN_GROUP):
                _ssm_seeds(ends_re, ends_im, a64re_ref, a64im_ref, carry_re, carry_im, seed_re, seed_im,
                           k0 + kk, reverse=False)
            init = [(seed_re[k0 + kk], seed_im[k0 + kk]) for kk in range(SCAN_GROUP)]
            _scan(sre, sim, are_ref, aim_ref, k0, init, reverse=False, store=True)
        sdre_ref[...] = seed_re[...]
        sdim_ref[...] = seed_im[...]
        for w in range(N_PAIR // PAIRS_PER_WIN):
            acc = jnp.zeros((SSM_TB, SSM_WIN), F32)
            for kk in range(PAIRS_PER_WIN):
                k2 = w * PAIRS_PER_WIN + kk
                acc += _dot(_pair_rows(sre, k2), cwre_ref[k2])
                acc -= _dot(_pair_rows(sim, k2), cwim_ref[k2])
            cols = _win_cols(w * PAIRS_PER_WIN)
            ypre = acc + d_ref[:, cols] * u[:, cols].astype(F32)
            ypre_ref[:, cols] = ypre
            yact_ref[:, cols] = _gelu(ypre).astype(BF16)

    c2, c3 = _ssm_specs_consts()
    seed_spec = pl.BlockSpec((None, N_SLAB, SSM_SUB, LANES), lambda b: (b, 0, 0, 0))
    small = pltpu.VMEM((N_SLAB, LANES), F32)
    tile = pltpu.VMEM((N_SLAB, SSM_SUB, LANES), F32)
    return _call(
        body, name="ssm_fwd", grid=(nb,),
        in_specs=[pl.BlockSpec((SSM_TB, SSM_WIDTH), lambda b: (b, ucol)), c2, c2, c2, c2, c3, c3, c3, c3,
                  pl.BlockSpec((1, SSM_WIDTH), lambda b: (0, 0))],
        out_specs=[pl.BlockSpec((SSM_TB, SSM_WIDTH), lambda b: (b, 0)),
                   pl.BlockSpec((SSM_TB, SSM_WIDTH), lambda b: (b, 0)), seed_spec, seed_spec],
        out_shape=[_sds((L, SSM_WIDTH), F32), _sds((L, SSM_WIDTH), BF16),
                   _sds((nb, N_SLAB, SSM_SUB, LANES), F32), _sds((nb, N_SLAB, SSM_SUB, LANES), F32)],
        scratch=_ssm_scratch() + [small, small, tile, tile, tile, tile],
        sem=("arbitrary",), rider=rider)(z, pk["a_re"], pk["a_im"], pk["a64_re"], pk["a64_im"],
                                         pk["bw_re"].astype(BF16), pk["bw_im"].astype(BF16),
                                         pk["cw_re"].astype(BF16), pk["cw_im"].astype(BF16), dskip)


def _combine_weights(l0, l1, l2):
    m = jnp.maximum(jnp.maximum(l0, l1), l2)
    e0, e1, e2 = jnp.exp(l0 - m), jnp.exp(l1 - m), jnp.exp(l2 - m)
    inv = 1.0 / (e0 + e1 + e2)
    return e0 * inv, e1 * inv, e2 * inv


def _mix_fwd(x, z, a, yact, w_ap, w_ga, w_gb, w_out):
    L = x.shape[0]
    cs = D_MODEL // N_CHIPS
    ga_col = (3 * N_GROUPS * ATTN_WIDTH + SSM_WIDTH) // D_MODEL

    def body(x_ref, ga_ref, gs_ref, a_ref, y_ref, wap_ref, wga_ref, wgb_ref, wout_ref,
             x1_ref, aout_ref, sa_ref, sb_ref, mix_ref):
        a = a_ref[...]
        y = y_ref[...]
        for s in range(N_CHIPS):
            cols = slice(s * cs, (s + 1) * cs)
            aout_ref[:, cols] = _dot(a, wap_ref[s]).astype(BF16)
            sa_ref[:, cols] = _dot(y, wga_ref[s]).astype(BF16)
            sb_ref[:, cols] = _dot(y, wgb_ref[s]).astype(BF16)
        s_out = sa_ref[...].astype(F32) * _sigmoid(sb_ref[...].astype(F32))
        mix = (_sigmoid(ga_ref[...].astype(F32)) * aout_ref[...].astype(F32)
               + _sigmoid(gs_ref[...].astype(F32)) * s_out).astype(BF16)
        mix_ref[...] = mix
        x1_ref[...] = x_ref[...] + _dot(mix, wout_ref[...])

    tok = lambda w: pl.BlockSpec((TM_MIX, w), lambda i: (i, 0))
    wsm = pl.BlockSpec((N_CHIPS, ATTN_WIDTH, cs), lambda i: (0, 0, 0))
    return _call(
        body, name="mix_fwd", grid=(L // TM_MIX,),
        in_specs=[tok(D_MODEL), pl.BlockSpec((TM_MIX, D_MODEL), lambda i: (i, ga_col)),
                  pl.BlockSpec((TM_MIX, D_MODEL), lambda i: (i, ga_col + 1))]
                 + [tok(ATTN_WIDTH)] * 2 + [wsm, wsm, wsm, pl.BlockSpec((D_MODEL, D_MODEL), lambda i: (0, 0))],
        out_specs=[tok(D_MODEL), tok(D_MODEL), tok(D_MODEL), tok(D_MODEL), tok(D_MODEL)],
        out_shape=[_sds((L, D_MODEL), F32)] + [_sds((L, D_MODEL), BF16)] * 4,
        sem=("parallel",))(x, z, z, a, yact, w_ap, w_ga, w_gb, w_out.reshape(D_MODEL, D_MODEL))


def _ffn_fwd(x1, g, w_g, w_u, w_d, rider=None):
    L = x1.shape[0]
    fs = D_FF // N_CHIPS
    TM = TM_PROJ

    def body(x_ref, g_ref, wg_ref, wu_ref, wd_ref, x2_ref, h_ref, gate_ref, up_ref, act_ref, acc):
        s = pl.program_id(1)

        @pl.when(s == 0)
        def _():
            xv = x_ref[...]
            r = lax.rsqrt(jnp.mean(xv * xv, axis=-1, keepdims=True) + EPS)
            h_ref[...] = (xv * r * g_ref[...]).astype(BF16)
            acc[...] = jnp.zeros_like(acc)

        h = h_ref[...]
        gate = _dot_nt(h, wg_ref[...])
        up = _dot_nt(h, wu_ref[...])
        act = (gate * _sigmoid(gate) * up).astype(BF16)
        gate_ref[...] = gate.astype(BF16)
        up_ref[...] = up.astype(BF16)
        act_ref[...] = act
        acc[...] += _dot(act, wd_ref[...])

        @pl.when(s == N_CHIPS - 1)
        def _():
            x2_ref[...] = x_ref[...] + acc[...]

    tok = pl.BlockSpec((TM, D_MODEL), lambda i, s: (i, 0))
    ffs = pl.BlockSpec((None, TM, fs), lambda i, s: (s, i, 0))
    return _call(
        body, name="ffn_fwd", grid=(L // TM, N_CHIPS),
        in_specs=[tok, pl.BlockSpec((1, D_MODEL), lambda i, s: (0, 0))]
                 + [pl.BlockSpec((None, fs, D_MODEL), lambda i, s: (s, 0, 0))] * 3,
        out_specs=[tok, tok, ffs, ffs, ffs],
        out_shape=[_sds((L, D_MODEL), F32), _sds((L, D_MODEL), BF16)] + [_sds((N_CHIPS, L, fs), BF16)] * 3,
        scratch=[pltpu.VMEM((TM, D_MODEL), F32)],
        sem=("parallel", "arbitrary"), rider=rider)(x1, g, w_g, w_u, w_d)


def _loss_head(xl, target):
    L = xl.shape[0]

    def body(x_ref, t_ref, dx_ref, loss_ref, acc):
        i = pl.program_id(0)

        @pl.when(i == 0)
        def _():
            acc[...] = jnp.zeros_like(acc)

        e = x_ref[...] - t_ref[...]
        dx_ref[...] = e * (1.0 / D_MODEL)
        acc[...] += jnp.sum((e * e).reshape(TM // SUBLANES, SUBLANES, D_MODEL), axis=0)

        @pl.when(i == pl.num_programs(0) - 1)
        def _():
            loss_ref[...] = (0.5 / D_MODEL) * jnp.sum(acc[...]).reshape(1, 1)

    tok = pl.BlockSpec((TM, D_MODEL), lambda i: (i, 0))
    return _call(
        body, name="loss_head", grid=(L // TM,), in_specs=[tok, tok],
        out_specs=[tok, pl.BlockSpec((1, 1), lambda i: (0, 0))],
        out_shape=[_sds((L, D_MODEL), F32), _sds((1, 1), F32)],
        scratch=[pltpu.VMEM((SUBLANES, D_MODEL), F32)], sem=("arbitrary",))(xl, target)


def _ssm_pack(lam_re, lam_im, log_dt, b_re, b_im, c_re, c_im):
    dt = jnp.exp(log_dt)[:, None]
    mag = jnp.exp(lam_re * dt)
    ang = lam_im * dt
    ar = mag * jnp.cos(ang)
    ai = mag * jnp.sin(ang)
    nr = ar - 1.0
    ni = ai
    den = lam_re * lam_re + lam_im * lam_im
    cr = ((nr * lam_re + ni * lam_im) / den)[..., None]
    ci = ((ni * lam_re - nr * lam_im) / den)[..., None]
    bbr = cr * b_re - ci * b_im
    bbi = cr * b_im + ci * b_re
    gpp = SSM_WIN // SSM_STATE
    gpw = SSM_WIN // SSM_GROUP
    k2 = jnp.arange(N_PAIR)[:, None, None]
    gs = jnp.arange(gpp)[None, :, None]
    gl = jnp.arange(gpw)[None, None, :]
    same = (gl == gpp * (k2 % PAIRS_PER_WIN) + gs).astype(F32)

    def b_windows(bb):
        return jnp.einsum('kgl,kgpc->klcgp', same, bb.reshape(N_PAIR, gpp, SSM_STATE, SSM_GROUP)).reshape(
            N_PAIR, SSM_WIN, SSM_WIN)

    def c_windows(cc):
        return jnp.einsum('kgl,kgcp->kgplc', same, cc.reshape(N_PAIR, gpp, SSM_GROUP, SSM_STATE)).reshape(
            N_PAIR, SSM_WIN, SSM_WIN)

    pr, pi = ar, ai
    for _ in range(int(math.log2(SSM_TC))):
        pr, pi = pr * pr - pi * pi, 2.0 * pr * pi
    return dict(a_re=ar.reshape(N_SLAB, LANES), a_im=ai.reshape(N_SLAB, LANES),
                a64_re=pr.reshape(N_SLAB, LANES), a64_im=pi.reshape(N_SLAB, LANES),
                bw_re=b_windows(bbr), bw_im=b_windows(bbi), cw_re=c_windows(c_re), cw_im=c_windows(c_im))


def _layer_fwd(x, p, rest, rest_stage, next_bufs=None):
    first = {"ici": _gather_ici_rider, "d2d": _gather_d2d_rider}[rest_stage]
    outs = _in_proj_fwd(x, p["g_mix"], p["w_in"], first(rest))
    (z, h), rest = outs[:2], list(outs[2:])
    qkv, got = _qkv_prep(z, jnp.tile(p["g_q"], (1, N_HEADS)), jnp.tile(p["g_k"], (1, N_HEADS)),
                         _gather_d2d_rider(rest) if rest_stage == "ici" else None)
    p = {**p, **dict(zip(BIG[1:], got if rest_stage == "ici" else rest))}
    os_, ls_ = [], []
    for gi in range(N_GROUPS):
        o, l = _attn_fwd(*qkv[gi], gi)
        os_.append(o)
        ls_.append(l)
    a = _combine_fwd(os_, ls_)
    pk = _ssm_pack(p["lambda_re"], p["lambda_im"], p["log_dt"], p["b_re"], p["b_im"], p["c_re"], p["c_im"])
    outs = _ssm_fwd(z, pk, p["d_skip"], _gather_ici_rider(next_bufs[:1]) if next_bufs else None)
    (ypre, yact, sd_re, sd_im), next_in = outs[:4], list(outs[4:])
    x1, aout, sa, sb, mix = _mix_fwd(x, z, a, yact, p["w_attn_proj"], p["w_glu_a"], p["w_glu_b"], p["w_out"])
    outs = _ffn_fwd(x1, p["g_ffn"], p["w_ffn_gate"], p["w_ffn_up"], p["w_ffn_down"],
                    _join_riders(_gather_ici_rider(next_bufs[1:]), _gather_d2d_rider(next_in)) if next_bufs else None)
    x2, h2, gate, up, act = outs[:5]
    nxt = (outs[-1], list(outs[5:-1])) if next_bufs else None
    saved = dict(x=x, z=z, h=h, qkv=qkv, os=os_, ls=ls_, pk=pk, ypre=ypre, yact=yact, sd_re=sd_re, sd_im=sd_im,
                 x1=x1, a=a, aout=aout, sa=sa, sb=sb, mix=mix, h2=h2, gate=gate, up=up, act=act)
    return x2, saved, p, nxt


def _rms_bwd(xv, g, dh):
    r = lax.rsqrt(jnp.mean(xv * xv, axis=-1, keepdims=True) + EPS)
    xn = xv * r
    dxn = dh * g
    dx = r * (dxn - xn * jnp.mean(dxn * xn, axis=-1, keepdims=True))
    dg = jnp.sum((dh * xn).reshape(xv.shape[0] // SUBLANES, SUBLANES, xv.shape[1]), axis=0)
    return dx, dg


def _ffn_bwd_act(dx2, gate, up, w_d):
    L = dx2.shape[0]
    fs = D_FF // N_CHIPS
    TM = TM_PROJ

    def body(dx_ref, gate_ref, up_ref, wd_ref, dgate_ref, dup_ref):
        dact = _dot_nt(dx_ref[...].astype(BF16), wd_ref[...])
        gt = gate_ref[...].astype(F32)
        sg = _sigmoid(gt)
        dgate_ref[...] = (dact * up_ref[...].astype(F32) * (sg * (1.0 + gt * (1.0 - sg)))).astype(BF16)
        dup_ref[...] = (dact * gt * sg).astype(BF16)

    ffs = pl.BlockSpec((None, TM, fs), lambda i, s: (s, i, 0))
    return _call(
        body, name="ffn_bwd_act", grid=(L // TM, N_CHIPS),
        in_specs=[pl.BlockSpec((TM, D_MODEL), lambda i, s: (i, 0)), ffs, ffs,
                  pl.BlockSpec((None, fs, D_MODEL), lambda i, s: (s, 0, 0))],
        out_specs=[ffs, ffs], out_shape=[_sds((N_CHIPS, L, fs), BF16)] * 2,
        sem=("parallel", "parallel"))(dx2, gate, up, w_d)


def _ffn_bwd_in(dx2, x1, g, dgate, dup, w_g, w_u, rider=None):
    L = x1.shape[0]
    fs = D_FF // N_CHIPS
    TM = TM_PROJ

    def body(dx_ref, x_ref, g_ref, dgate_ref, dup_ref, wg_ref, wu_ref, dx1_ref, dg_ref, acc, dgacc):
        i, s = pl.program_id(0), pl.program_id(1)

        @pl.when(s == 0)
        def _():
            acc[...] = jnp.zeros_like(acc)

        @pl.when(jnp.logical_and(i == 0, s == 0))
        def _():
            dgacc[...] = jnp.zeros_like(dgacc)

        acc[...] += _dot(dgate_ref[...], wg_ref[...]) + _dot(dup_ref[...], wu_ref[...])

        @pl.when(s == N_CHIPS - 1)
        def _():
            dx, dg = _rms_bwd(x_ref[...], g_ref[...], acc[...])
            dx1_ref[...] = dx_ref[...] + dx
            dgacc[...] += dg

        @pl.when(jnp.logical_and(i == pl.num_programs(0) - 1, s == N_CHIPS - 1))
        def _():
            dg_ref[...] = jnp.sum(dgacc[...], axis=0, keepdims=True)

    tok = pl.BlockSpec((TM, D_MODEL), lambda i, s: (i, 0))
    ffs = pl.BlockSpec((None, TM, fs), lambda i, s: (s, i, 0))
    vec = pl.BlockSpec((1, D_MODEL), lambda i, s: (0, 0))
    return _call(
        body, name="ffn_bwd_in", grid=(L // TM, N_CHIPS),
        in_specs=[tok, tok, vec, ffs, ffs,
                  pl.BlockSpec((None, fs, D_MODEL), lambda i, s: (s, 0, 0)),
                  pl.BlockSpec((None, fs, D_MODEL), lambda i, s: (s, 0, 0))],
        out_specs=[tok, vec],
        out_shape=[_sds((L, D_MODEL), F32), _sds((1, D_MODEL), F32)],
        scratch=[pltpu.VMEM((TM, D_MODEL), F32), pltpu.VMEM((SUBLANES, D_MODEL), F32)],
        sem=("arbitrary", "arbitrary"), rider=rider)(dx2, x1, g, dgate, dup, w_g, w_u)


def _wgrad(a, b, *, name, grid_kn, a_spec, b_spec, out_shape, out_spec):
    L = a.shape[-2]
    nl = L // TL_WGRAD

    def body(a_ref, b_ref, o_ref):
        @pl.when(pl.program_id(2) == 0)
        def _():
            o_ref[...] = jnp.zeros_like(o_ref)
        o_ref[...] += _dot_tn(a_ref[...].astype(BF16), b_ref[...].astype(BF16))

    return _call(body, name=name, grid=(*grid_kn, nl), in_specs=[a_spec, b_spec], out_specs=out_spec,
                 out_shape=out_shape, sem=("parallel", "parallel", "arbitrary"))(a, b)


def _wgrad_cols(a, b, name):
    K, N = a.shape[1], b.shape[1]
    ns = N // N_CHIPS
    if N * K * 4 <= 4 * 1024 * 1024:
        L = a.shape[0]

        def body(a_ref, b_ref, o_ref):
            @pl.when(pl.program_id(0) == 0)
            def _():
                o_ref[...] = jnp.zeros_like(o_ref)
            av = a_ref[...].astype(BF16)
            for s in range(N_CHIPS):
                o_ref[s] += _dot_tn(av, b_ref[:, s * ns:(s + 1) * ns].astype(BF16))

        return _call(body, name=name, grid=(L // TL_WGRAD,),
                     in_specs=[pl.BlockSpec((TL_WGRAD, K), lambda t: (t, 0)),
                               pl.BlockSpec((TL_WGRAD, N), lambda t: (t, 0))],
                     out_specs=pl.BlockSpec((N_CHIPS, K, ns), lambda t: (0, 0, 0)),
                     out_shape=_sds((N_CHIPS, K, ns), F32), sem=("arbitrary",))(a, b)
    tn = ns // 2 if ns % (2 * LANES) == 0 else ns
    nj = ns // tn
    return _wgrad(a, b, name=name, grid_kn=(1, N_CHIPS * nj),
                  a_spec=pl.BlockSpec((TL_WGRAD, K), lambda i, j, t: (t, 0)),
                  b_spec=pl.BlockSpec((TL_WGRAD, tn), lambda i, j, t: (t, j)),
                  out_shape=_sds((N_CHIPS, K, ns), F32),
                  out_spec=pl.BlockSpec((None, K, tn), lambda i, j, t: (j // nj, 0, j % nj)))


def _wgrad_full(a, b, name):
    K, N = a.shape[1], b.shape[1]
    return _wgrad(a, b, name=name, grid_kn=(1, 1),
                  a_spec=pl.BlockSpec((TL_WGRAD, K), lambda i, j, t: (t, 0)),
                  b_spec=pl.BlockSpec((TL_WGRAD, N), lambda i, j, t: (t, 0)),
                  out_shape=_sds((K, N), F32), out_spec=pl.BlockSpec((K, N), lambda i, j, t: (0, 0)))


def _wgrad_ff_cols(a, b, name):
    K, fs = a.shape[1], b.shape[2]
    return _wgrad(a, b, name=name, grid_kn=(1, N_CHIPS),
                  a_spec=pl.BlockSpec((TL_WGRAD, K), lambda i, j, t: (t, 0)),
                  b_spec=pl.BlockSpec((None, TL_WGRAD, fs), lambda i, j, t: (j, t, 0)),
                  out_shape=_sds((N_CHIPS, K, fs), F32),
                  out_spec=pl.BlockSpec((None, K, fs), lambda i, j, t: (j, 0, 0)))


def _wgrad_ff_rows(a, b, name):
    fs, N = a.shape[2], b.shape[1]
    return _wgrad(a, b, name=name, grid_kn=(N_CHIPS, 1),
                  a_spec=pl.BlockSpec((None, TL_WGRAD, fs), lambda i, j, t: (i, t, 0)),
                  b_spec=pl.BlockSpec((TL_WGRAD, N), lambda i, j, t: (t, 0)),
                  out_shape=_sds((N_CHIPS, fs, N), F32),
                  out_spec=pl.BlockSpec((None, fs, N), lambda i, j, t: (i, 0, 0)))


def _mix_bwd(dx, z, aout, sa, sb, ypre, w_ap, w_ga, w_gb, w_out, rider=None):
    L = dx.shape[0]
    cs = D_MODEL // N_CHIPS
    ga_col = (3 * N_GROUPS * ATTN_WIDTH + SSM_WIDTH) // D_MODEL

    def body(dx_ref, ga_ref, gs_ref, aout_ref, sa_ref, sb_ref, ypre_ref, wap_ref, wga_ref, wgb_ref, wout_ref,
             dgates_ref, da_ref, gy_ref, daout_ref, dsa_ref, dsb_ref):
        dmix = _dot_nt(dx_ref[...].astype(BF16), wout_ref[...])
        sig_a = _sigmoid(ga_ref[...].astype(F32))
        sig_s = _sigmoid(gs_ref[...].astype(F32))
        a_out = aout_ref[...].astype(F32)
        s_a = sa_ref[...].astype(F32)
        sig_b = _sigmoid(sb_ref[...].astype(F32))
        s_out = s_a * sig_b
        daout = (dmix * sig_a).astype(BF16)
        daout_ref[...] = daout
        dgates_ref[:, :D_MODEL] = (dmix * a_out * sig_a * (1.0 - sig_a)).astype(BF16)
        dgates_ref[:, D_MODEL:] = (dmix * s_out * sig_s * (1.0 - sig_s)).astype(BF16)
        ds_out = dmix * sig_s
        dsa = (ds_out * sig_b).astype(BF16)
        dsb = (ds_out * s_a * sig_b * (1.0 - sig_b)).astype(BF16)
        dsa_ref[...] = dsa
        dsb_ref[...] = dsb
        da = jnp.zeros((TM_MIX, ATTN_WIDTH), F32)
        dy = jnp.zeros((TM_MIX, SSM_WIDTH), F32)
        for s in range(N_CHIPS):
            cols = slice(s * cs, (s + 1) * cs)
            da += _dot_nt(daout[:, cols], wap_ref[s])
            dy += _dot_nt(dsa[:, cols], wga_ref[s]) + _dot_nt(dsb[:, cols], wgb_ref[s])
        gy_ref[...] = dy * _gelu_grad(ypre_ref[...])
        da_ref[...] = da

    tok = lambda w: pl.BlockSpec((TM_MIX, w), lambda i: (i, 0))
    wsm = pl.BlockSpec((N_CHIPS, ATTN_WIDTH, cs), lambda i: (0, 0, 0))
    return _call(
        body, name="mix_bwd", grid=(L // TM_MIX,),
        in_specs=[tok(D_MODEL), pl.BlockSpec((TM_MIX, D_MODEL), lambda i: (i, ga_col)),
                  pl.BlockSpec((TM_MIX, D_MODEL), lambda i: (i, ga_col + 1)),
                  tok(D_MODEL), tok(D_MODEL), tok(D_MODEL), tok(SSM_WIDTH),
                  wsm, wsm, wsm, pl.BlockSpec((D_MODEL, D_MODEL), lambda i: (0, 0))],
        out_specs=[tok(2 * D_MODEL), tok(ATTN_WIDTH), tok(SSM_WIDTH)] + [tok(D_MODEL)] * 3,
        out_shape=[_sds((L, 2 * D_MODEL), BF16), _sds((L, ATTN_WIDTH), F32), _sds((L, SSM_WIDTH), F32)]
                  + [_sds((L, D_MODEL), BF16)] * 3,
        sem=("parallel",), rider=rider)(dx, z, z, aout, sa, sb, ypre, w_ap, w_ga, w_gb,
                                        w_out.reshape(D_MODEL, D_MODEL))


def _combine_bwd(da, os_, ls_):
    L = da.shape[0]

    def body(da_ref, o0, o1, o2, l0, l1, l2, hs_ref, p1_ref, p2_ref, pt1_ref, pt2_ref,
             do0, do1, do2, c0, c1, c2):
        o_tok, l_tok = _to_token_order((o0[...], o1[...], o2[...]), (l0[...], l1[...], l2[...]),
                                       (None, pt1_ref[...], pt2_ref[...]))
        w = _combine_weights(*l_tok)
        dav = da_ref[...]
        hs = hs_ref[...]
        tbar = sum(wg * _head_sum(dav * og, hs) for wg, og in zip(w, o_tok))
        for wg, pm, do_ref, c_ref in zip(w, (None, p1_ref[...], p2_ref[...]), (do0, do1, do2), (c0, c1, c2)):
            dog = (wg * dav).astype(BF16)
            cg = -wg * tbar
            do_ref[...] = dog if pm is None else _dot(pm, dog).astype(BF16)
            c_ref[...] = cg if pm is None else _dot(pm, cg.astype(BF16))

    tile = pl.BlockSpec((DL_TILE, ATTN_WIDTH), lambda i: (i, 0))
    mat = pl.BlockSpec((DL_TILE, DL_TILE), lambda i: (0, 0))
    p1, p2 = _perm_matrix(ATTN_PATTERN[1][1]), _perm_matrix(ATTN_PATTERN[2][1])
    outs = _call(body, name="combine_bwd", grid=(L // DL_TILE,), in_specs=[tile] * 7 + [mat] * 5,
                 out_specs=[tile] * 6,
                 out_shape=[_sds((L, ATTN_WIDTH), BF16)] * 3 + [_sds((L, ATTN_WIDTH), F32)] * 3,
                 sem=("parallel",))(da, *os_, *ls_, _head_sum_matrix(), p1, p2, p1.T, p2.T)
    return outs[:3], outs[3:]


def _attn_bwd(qs, ks, v, do, l, c, gi, rider=None):
    L = qs.shape[0]
    _, d = ATTN_PATTERN[gi]
    nb = L // (d * BLK)

    def body(q0_ref, q1_ref, k_ref, v_ref, do0_ref, do1_ref, l0_ref, l1_ref, c0_ref, c1_ref,
             dq_ref, dk_ref, dv_ref, carry):
        n = pl.program_id(1)

        @pl.when(n == 0)
        def _():
            carry[...] = jnp.zeros_like(carry)

        qi = lax.broadcasted_iota(jnp.int32, (2 * BLK, BLK), 0)
        kj = lax.broadcasted_iota(jnp.int32, (2 * BLK, BLK), 1)
        first = qi < BLK
        mask = jnp.logical_and(jnp.where(first, qi, kj) >= jnp.where(first, kj, qi - BLK),
                               qi < jnp.where(n < nb - 1, 2 * BLK, BLK))
        q2 = jnp.concatenate([_dl_read(q0_ref), _dl_read(q1_ref)], axis=0)
        do2 = jnp.concatenate([_dl_read(do0_ref), _dl_read(do1_ref)], axis=0)
        l2 = jnp.concatenate([_dl_read(l0_ref), _dl_read(l1_ref)], axis=0)
        c2 = jnp.concatenate([_dl_read(c0_ref), _dl_read(c1_ref)], axis=0)
        k = _dl_read(k_ref)
        v_ = _dl_read(v_ref)
        h0, h1 = _pair_masks()
        mask2 = jnp.concatenate([mask, mask], axis=1)
        dq_parts, dk_parts, dv_parts = [], [], []
        for hp in range(N_HEADS // 2):
            ls = slice(hp * LANES, (hp + 1) * LANES)
            qp, dop, kp_, vp_ = q2[:, ls], do2[:, ls], k[:, ls], v_[:, ls]
            kk = jnp.concatenate([jnp.where(h0, kp_, 0), jnp.where(h1, kp_, 0)], axis=0)
            vv = jnp.concatenate([jnp.where(h0, vp_, 0), jnp.where(h1, vp_, 0)], axis=0)

            def per_head(t):
                a = jnp.broadcast_to(t[:, hp * LANES:hp * LANES + 1], (2 * BLK, BLK))
                b = jnp.broadcast_to(t[:, hp * LANES + HEAD_DIM:hp * LANES + HEAD_DIM + 1], (2 * BLK, BLK))
                return jnp.concatenate([a, b], axis=1)

            p = jnp.where(mask2, jnp.exp(_dot_nt(qp, kk) - per_head(l2)), 0.0)
            ds = (p * (_dot_nt(dop, vv) + per_head(c2))).astype(BF16)
            dv2 = _dot_tn(p.astype(BF16), dop)
            dk2 = _dot_tn(ds, qp)
            dq2 = _dot(ds, kk)
            dq_parts.append((dq2[:BLK] + carry[:, ls]).astype(BF16))
            carry[:, ls] = dq2[BLK:]
            dk_parts.append(jnp.where(h0, dk2[:BLK], dk2[BLK:]).astype(BF16))
            dv_parts.append(jnp.where(h0, dv2[:BLK], dv2[BLK:]).astype(BF16))
        _dl_write(dq_ref, jnp.concatenate(dq_parts, axis=1))
        _dl_write(dk_ref, jnp.concatenate(dk_parts, axis=1))
        _dl_write(dv_ref, jnp.concatenate(dv_parts, axis=1))

    cur = _dl_spec(d, ATTN_WIDTH, lambda n: n)
    nxt = _dl_spec(d, ATTN_WIDTH, lambda n: jnp.minimum(n + 1, nb - 1))
    view = lambda t: _dl_view(t, d)
    outs = _call(
        body, name=f"attn_bwd_g{gi}", grid=(d, nb),
        in_specs=[cur, nxt, cur, cur, cur, nxt, cur, nxt, cur, nxt], out_specs=[cur, cur, cur],
        out_shape=[_sds(view(qs).shape, BF16)] * 3, scratch=[pltpu.VMEM((BLK, ATTN_WIDTH), F32)],
        sem=("parallel", "arbitrary"), rider=rider)(view(qs), view(qs), view(ks), view(v), view(do), view(do),
                                                    view(l), view(l), view(c), view(c))
    return [t.reshape(L, ATTN_WIDTH) for t in outs[:3]], list(outs[3:])


def _qkv_post(z, dqkv, du, dgates, gq_t, gk_t):
    L = z.shape[0]
    qkv_w = N_GROUPS * ATTN_WIDTH

    def body(zq_ref, zk_ref, gq_ref, gk_ref, hs_ref, pt1_ref, pt2_ref, du_ref, dgates_ref, *rest):
        dl_refs, (dz_ref, dgq_ref, dgk_ref) = rest[:9], rest[9:]

        @pl.when(pl.program_id(0) == 0)
        def _():
            dgq_ref[...] = jnp.zeros_like(dgq_ref)
            dgk_ref[...] = jnp.zeros_like(dgk_ref)

        hs = hs_ref[...]
        pts = (None, pt1_ref[...], pt2_ref[...])

        def rows8(t):
            return jnp.sum(t.reshape(DL_TILE // SUBLANES, SUBLANES, ATTN_WIDTH), axis=0)

        def norm_bwd(x, gain, dn):
            r = lax.rsqrt(_head_sum(x * x, hs) * (1.0 / HEAD_DIM) + EPS)
            xh = x * r
            dh = dn * gain
            return r * (dh - xh * (_head_sum(dh * xh, hs) * (1.0 / HEAD_DIM))), rows8(dn * xh)

        for g in range(N_GROUPS):
            tok = [t[...].astype(F32) if pts[g] is None else _dot(pts[g], t[...]) for t in dl_refs[3 * g:3 * g + 3]]
            cols = slice(g * ATTN_WIDTH, (g + 1) * ATTN_WIDTH)
            dq, pq = norm_bwd(zq_ref[:, cols].astype(F32), gq_ref[...] * SCALE, tok[0])
            dk, pk_ = norm_bwd(zk_ref[:, cols].astype(F32), gk_ref[...], tok[1])
            dgq_ref[...] += pq * SCALE
            dgk_ref[...] += pk_
            dz_ref[:, cols] = dq.astype(BF16)
            dz_ref[:, qkv_w + g * ATTN_WIDTH:qkv_w + (g + 1) * ATTN_WIDTH] = dk.astype(BF16)
            dz_ref[:, 2 * qkv_w + g * ATTN_WIDTH:2 * qkv_w + (g + 1) * ATTN_WIDTH] = tok[2].astype(BF16)
        dz_ref[:, 3 * qkv_w:3 * qkv_w + SSM_WIDTH] = du_ref[...]
        dz_ref[:, 3 * qkv_w + SSM_WIDTH:] = dgates_ref[...]

    tile = lambda w: pl.BlockSpec((DL_TILE, w), lambda i: (i, 0))
    mat = pl.BlockSpec((DL_TILE, DL_TILE), lambda i: (0, 0))
    vec = pl.BlockSpec((1, ATTN_WIDTH), lambda i: (0, 0))
    acc = pl.BlockSpec((SUBLANES, ATTN_WIDTH), lambda i: (0, 0))
    flat = [t for grp in dqkv for t in grp]
    return _call(
        body, name="qkv_post", grid=(L // DL_TILE,),
        in_specs=[tile(qkv_w), pl.BlockSpec((DL_TILE, qkv_w), lambda i: (i, 1)), vec, vec, mat, mat, mat,
                  tile(SSM_WIDTH), tile(2 * D_MODEL)] + [tile(ATTN_WIDTH)] * 9,
        out_specs=[tile(IN_COLS), acc, acc],
        out_shape=[_sds((L, IN_COLS), BF16), _sds((SUBLANES, ATTN_WIDTH), F32), _sds((SUBLANES, ATTN_WIDTH), F32)],
        sem=("arbitrary",))(z, z, gq_t, gk_t, _head_sum_matrix(), _perm_matrix(ATTN_PATTERN[1][1]).T,
                            _perm_matrix(ATTN_PATTERN[2][1]).T, du, dgates, *flat)


def _scan_rev_grad(sre, sim, rre, rim, are_ref, aim_ref, k0, init, seed_re, seed_im):
    ar = [_bcast(are_ref, k0 + kk) for kk in range(SCAN_GROUP)]
    ai = [-_bcast(aim_ref, k0 + kk) for kk in range(SCAN_GROUP)]

    def update(i, xprev, carry):
        out = []
        for kk in range(SCAN_GROUP):
            k = k0 + kk
            lr, li, dr, di = carry[4 * kk:4 * kk + 4]
            nr = ar[kk] * lr - ai[kk] * li + rre[k, _rows(i), :]
            ni = ar[kk] * li + ai[kk] * lr + rim[k, _rows(i), :]
            rre[k, _rows(i), :] = nr
            rim[k, _rows(i), :] = ni
            xr, xi = xprev(k)
            out += [nr, ni, dr + xr * nr + xi * ni, di + xr * ni - xi * nr]
        return tuple(out)

    def step(t, carry):
        i = SSM_TC - 1 - t
        return update(i, lambda k: (sre[k, _rows(i - 1), :], sim[k, _rows(i - 1), :]), carry)

    zero = jnp.zeros((SSM_SUB, LANES), F32)
    flat = []
    for re, im in init:
        flat += [re, im, zero, zero]
    res = lax.fori_loop(0, SSM_TC - 1, step, tuple(flat), unroll=3)
    res = update(0, lambda k: (seed_re[k], seed_im[k]), res)
    return [(res[4 * kk + 2], res[4 * kk + 3]) for kk in range(SCAN_GROUP)]


def _ssm_bwd(z, gy, pk, dskip, sd_re, sd_im, rider=None):
    L = z.shape[0]
    nb = L // SSM_TB
    ucol = (3 * N_GROUPS * ATTN_WIDTH) // SSM_WIDTH
    nwin = N_PAIR // PAIRS_PER_WIN

    def body(u_ref, gy_ref, are_ref, aim_ref, a64re_ref, a64im_ref, bwre_ref, bwim_ref, cwre_ref, cwim_ref, d_ref,
             sdre_ref, sdim_ref,
             du_ref, dare_ref, daim_ref, dbre_ref, dbim_ref, dcre_ref, dcim_ref, dd_ref,
             sre, sim, rre, rim, carry_re, carry_im, ends_re, ends_im, seed_re, seed_im):
        @pl.when(pl.program_id(0) == 0)
        def _():
            carry_re[...] = jnp.zeros_like(carry_re)
            carry_im[...] = jnp.zeros_like(carry_im)
            for ref in (dare_ref, daim_ref, dbre_ref, dbim_ref, dcre_ref, dcim_ref, dd_ref):
                ref[...] = jnp.zeros_like(ref)

        u = u_ref[...]
        gyv = gy_ref[...]
        gyb = gyv.astype(BF16)
        _ssm_fill(u, bwre_ref, bwim_ref, sre, sim)
        for k2 in range(N_PAIR):
            gw = gyb[:, _win_cols(k2)]
            _to_slabs(rre, k2, _dot_nt(gw, cwre_ref[k2]))
            _to_slabs(rim, k2, -_dot_nt(gw, cwim_ref[k2]))
        zero = jnp.zeros((SSM_SUB, LANES), F32)
        for k0 in range(0, N_SLAB, SCAN_GROUP):
            grp = range(k0, k0 + SCAN_GROUP)
            _scan(sre, sim, are_ref, aim_ref, k0, [(sdre_ref[k], sdim_ref[k]) for k in grp],
                  reverse=False, store=True)
            ends = _scan(rre, rim, are_ref, aim_ref, k0, [(zero, zero)] * SCAN_GROUP, reverse=True, store=False,
                         sign=-1.0)
            for kk, k in enumerate(grp):
                ends_re[k] = ends[kk][0]
                ends_im[k] = ends[kk][1]
            for k in grp:
                _ssm_seeds(ends_re, ends_im, a64re_ref, a64im_ref, carry_re, carry_im, seed_re, seed_im, k,
                           reverse=True, sign=-1.0)
            das = _scan_rev_grad(sre, sim, rre, rim, are_ref, aim_ref, k0,
                                 [(seed_re[k], seed_im[k]) for k in grp], sdre_ref, sdim_ref)
            for kk, k in enumerate(grp):
                dare_ref[k] += das[kk][0]
                daim_ref[k] += das[kk][1]
        for w in range(nwin):
            cols = _win_cols(w * PAIRS_PER_WIN)
            uw = u[:, cols]
            gw = gyb[:, cols]
            acc = gyv[:, cols] * d_ref[:, cols]
            for kk in range(PAIRS_PER_WIN):
                k2 = w * PAIRS_PER_WIN + kk
                lr = _pair_rows(rre, k2)
                li = _pair_rows(rim, k2)
                acc += _dot_nt(lr, bwre_ref[k2]) + _dot_nt(li, bwim_ref[k2])
                dbre_ref[k2] += _dot_tn(uw, lr)
                dbim_ref[k2] += _dot_tn(uw, li)
                dcre_ref[k2] += _dot_tn(_pair_rows(sre, k2), gw)
                dcim_ref[k2] -= _dot_tn(_pair_rows(sim, k2), gw)
            du_ref[:, cols] = acc.astype(BF16)
        dd_ref[...] += jnp.sum((gyv * u.astype(F32)).reshape(SSM_TB // SUBLANES, SUBLANES, SSM_WIDTH), axis=0)

    c2, c3 = _ssm_specs_consts()
    rev = lambda b: nb - 1 - b
    seed_spec = pl.BlockSpec((None, N_SLAB, SSM_SUB, LANES), lambda b: (rev(b), 0, 0, 0))
    tile_out = pl.BlockSpec((N_SLAB, SSM_SUB, LANES), lambda b: (0, 0, 0))
    small = pltpu.VMEM((N_SLAB, LANES), F32)
    tile = pltpu.VMEM((N_SLAB, SSM_SUB, LANES), F32)
    return _call(
        body, name="ssm_bwd", grid=(nb,),
        in_specs=[pl.BlockSpec((SSM_TB, SSM_WIDTH), lambda b: (rev(b), ucol)),
                  pl.BlockSpec((SSM_TB, SSM_WIDTH), lambda b: (rev(b), 0)),
                  c2, c2, c2, c2, c3, c3, c3, c3, pl.BlockSpec((1, SSM_WIDTH), lambda b: (0, 0)),
                  seed_spec, seed_spec],
        out_specs=[pl.BlockSpec((SSM_TB, SSM_WIDTH), lambda b: (rev(b), 0)), tile_out, tile_out, c3, c3, c3, c3,
                   pl.BlockSpec((SUBLANES, SSM_WIDTH), lambda b: (0, 0))],
        out_shape=[_sds((L, SSM_WIDTH), BF16), _sds((N_SLAB, SSM_SUB, LANES), F32),
                   _sds((N_SLAB, SSM_SUB, LANES), F32)] + [_sds((N_PAIR, SSM_WIN, SSM_WIN), F32)] * 4
                  + [_sds((SUBLANES, SSM_WIDTH), F32)],
        scratch=_ssm_scratch() + _ssm_scratch() + [small, small, tile, tile, tile, tile],
        sem=("arbitrary",), rider=rider)(z, gy, pk["a_re"], pk["a_im"], pk["a64_re"], pk["a64_im"],
                            pk["bw_re"].astype(BF16), pk["bw_im"].astype(BF16),
                            pk["cw_re"].astype(BF16), pk["cw_im"].astype(BF16), dskip, sd_re, sd_im)


def _in_proj_bwd(dz, w, x, g, dres, rider=None):
    L = x.shape[0]
    ns = w.shape[2]
    tn = ns
    nj = ns // tn
    nt = N_CHIPS * nj
    TM = TM_PROJ

    def body(dz_ref, w_ref, x_ref, g_ref, dres_ref, dx_ref, dg_ref, acc, dgacc):
        i, j = pl.program_id(0), pl.program_id(1)

        @pl.when(j == 0)
        def _():
            acc[...] = jnp.zeros_like(acc)

        @pl.when(jnp.logical_and(i == 0, j == 0))
        def _():
            dgacc[...] = jnp.zeros_like(dgacc)

        acc[...] += _dot_nt(dz_ref[...], w_ref[...])

        @pl.when(j == nt - 1)
        def _():
            dx, dg = _rms_bwd(x_ref[...], g_ref[...], acc[...])
            dx_ref[...] = dres_ref[...] + dx
            dgacc[...] += dg

        @pl.when(jnp.logical_and(i == pl.num_programs(0) - 1, j == nt - 1))
        def _():
            dg_ref[...] = jnp.sum(dgacc[...], axis=0, keepdims=True)

    tok = pl.BlockSpec((TM, D_MODEL), lambda i, j: (i, 0))
    vec = pl.BlockSpec((1, D_MODEL), lambda i, j: (0, 0))
    return _call(
        body, name="in_proj_bwd", grid=(L // TM, nt),
        in_specs=[pl.BlockSpec((TM, tn), lambda i, j: (i, j)),
                  pl.BlockSpec((None, D_MODEL, tn), lambda i, j: (j // nj, 0, j % nj)), tok, vec, tok],
        out_specs=[tok, vec],
        out_shape=[_sds((L, D_MODEL), F32), _sds((1, D_MODEL), F32)],
        scratch=[pltpu.VMEM((TM, D_MODEL), F32), pltpu.VMEM((SUBLANES, D_MODEL), F32)],
        sem=("arbitrary", "arbitrary"), rider=rider)(dz, w, x, g, dres)


SSM_PARAMS = ("lambda_re", "lambda_im", "log_dt", "b_re", "b_im", "c_re", "c_im")
EARLY = ("w_ffn_gate", "w_ffn_up", "w_ffn_down")
LATE = ("w_in", "w_attn_proj", "w_glu_a", "w_glu_b", "w_out")


def _layer_bwd(dx2, sv, p, pending, owned, l, idx):
    chip_idx, c_idx = idx
    g = {}
    owned = dict(owned)

    def settle(name, partial, arrived, layer):
        owned[name] = _sum_owner(partial, arrived, owned[name], layer, chip_idx, c_idx)

    dgate, dup = _ffn_bwd_act(dx2, sv["gate"], sv["up"], p["w_ffn_down"])
    outs = _ffn_bwd_in(dx2, sv["x1"], p["g_ffn"], dgate, dup, p["w_ffn_gate"], p["w_ffn_up"],
                       _scatter_rider([pending[n] for n in LATE[1:]]) if pending else None)
    dx1, g["g_ffn"] = outs[:2]
    for n, t in zip(LATE[1:], outs[2:]):
        settle(n, pending[n], t, l + 1)
    g["w_ffn_gate"] = _wgrad_ff_rows(dgate, sv["h2"], "wgrad_ffn_gate")
    g["w_ffn_up"] = _wgrad_ff_rows(dup, sv["h2"], "wgrad_ffn_up")
    g["w_ffn_down"] = _wgrad_ff_rows(sv["act"], dx2, "wgrad_ffn_down")

    outs = _mix_bwd(dx1, sv["z"], sv["aout"], sv["sa"], sv["sb"], sv["ypre"], p["w_attn_proj"], p["w_glu_a"],
                    p["w_glu_b"], p["w_out"], _swap_rider([g[n] for n in EARLY]))
    dgates, da, gy, daout, dsa, dsb = outs[:6]
    early = [_add_half(g[n], s, c_idx) for n, s in zip(EARLY, outs[6:])]
    g["w_out"] = _wgrad_full(sv["mix"], dx1, "wgrad_out").reshape(N_CHIPS, D_MODEL // N_CHIPS, D_MODEL)
    g["w_attn_proj"] = _wgrad_cols(sv["a"], daout, "wgrad_attn_proj")
    g["w_glu_a"] = _wgrad_cols(sv["yact"], dsa, "wgrad_glu_a")
    g["w_glu_b"] = _wgrad_cols(sv["yact"], dsb, "wgrad_glu_b")

    outs = _ssm_bwd(sv["z"], gy, sv["pk"], p["d_skip"], sv["sd_re"], sv["sd_im"],
                    _scatter_rider([pending[LATE[0]]]) if pending else None)
    du, da_re, da_im, dbw_re, dbw_im, dcw_re, dcw_im, dd = outs[:8]
    if pending:
        settle(LATE[0], pending[LATE[0]], outs[8], l + 1)
    g["d_skip"] = jnp.sum(dd, axis=0, keepdims=True)
    g["ssm_pack_ct"] = dict(a_re=da_re, a_im=da_im, bw_re=dbw_re, bw_im=dbw_im, cw_re=dcw_re, cw_im=dcw_im)

    dos, cs = _combine_bwd(da, sv["os"], sv["ls"])
    dqkv = []
    for gi in range(N_GROUPS):
        grads, arrived = _attn_bwd(*sv["qkv"][gi], dos[gi], sv["ls"][gi], cs[gi], gi, _scatter_rider([early[gi]]))
        settle(EARLY[gi], early[gi], arrived[0], l)
        dqkv.append(grads)
    dz, gq8, gk8 = _qkv_post(sv["z"], dqkv, du, dgates, jnp.tile(p["g_q"], (1, N_HEADS)),
                             jnp.tile(p["g_k"], (1, N_HEADS)))
    g["g_q"] = jnp.sum(gq8.reshape(SUBLANES * N_HEADS, HEAD_DIM), axis=0, keepdims=True)
    g["g_k"] = jnp.sum(gk8.reshape(SUBLANES * N_HEADS, HEAD_DIM), axis=0, keepdims=True)
    g["w_in"] = _wgrad_cols(sv["h"], dz, "wgrad_in")
    outs = _in_proj_bwd(dz, p["w_in"], sv["x"], p["g_mix"], dx1, _swap_rider([g[n] for n in LATE]))
    dx, g["g_mix"] = outs[:2]
    late = {n: _add_half(g[n], s, c_idx) for n, s in zip(LATE, outs[2:])}
    return dx, g, late, owned


def _place():
    x, y, c = lax.axis_index("x"), lax.axis_index("y"), lax.axis_index("c")
    others = [(1 - x, y), (x, 1 - y), (1 - x, 1 - y)]
    return x, y, c, others


def _half(ref, hc):
    rows = ref.shape[-2] // 2
    idx = (slice(None),) * (len(ref.shape) - 2) + (pl.ds(hc * rows, rows), slice(None))
    return ref.at[idx]


def _comm_call(body, name, ins, out_shapes, n_remote, aliases=None):
    scratch = [pltpu.SemaphoreType.DMA((n_remote,)), pltpu.SemaphoreType.DMA((n_remote,))]
    return pl.pallas_call(
        body, name=name, in_specs=[ANY] * len(ins), out_specs=[ANY] * len(out_shapes), out_shape=out_shapes,
        scratch_shapes=scratch, input_output_aliases=aliases or {})(*ins)


def _cast_place(w, l, chip_idx):
    _, R, C = w.shape
    tr = R // 2

    def body(me_ref, w_ref, o_ref):
        o_ref[...] = w_ref[...].astype(BF16)

    return _call(body, name=f"cast_place_l{l}", grid=(R // tr,), prefetch=1,
                 in_specs=[pl.BlockSpec((None, tr, C), lambda i, me_ref: (l, i, 0))],
                 out_specs=pl.BlockSpec((None, tr, C), lambda i, me_ref: (me_ref[0], i, 0)),
                 out_shape=_sds((N_CHIPS, R, C), BF16), sem=("arbitrary",))(chip_idx, w)


def _in_place_rider(bufs, pairs, per_buf=3):
    n = len(bufs)

    def copies(outs, send, recv, side):
        return [pltpu.make_async_remote_copy(src_ref=pair[side][0], dst_ref=pair[side][0], send_sem=send.at[k],
                                             recv_sem=recv.at[k], device_id=pair[side][1], device_id_type=MESH)
                for k, pair in enumerate(pairs(outs))]

    def start(ins, outs, send, recv):
        for cp in copies(outs, send, recv, 0):
            cp.start()

    def wait(ins, outs, send, recv):
        for cp in copies(outs, send, recv, 1):
            cp.wait_recv()
        for cp in copies(outs, send, recv, 0):
            cp.wait_send()

    return Rider(list(bufs), [_sds(b.shape, b.dtype) for b in bufs], per_buf * n, start, wait,
                 {a: a for a in range(n)})


def _gather_ici_rider(bufs):
    def pairs(outs):
        x, y, c, others = _place()
        return [((_half(o.at[2 * x + y], c), (cx, cy, c)), (_half(o.at[2 * cx + cy], c), (cx, cy, c)))
                for o in outs for cx, cy in others]
    return _in_place_rider(bufs, pairs)


def _gather_d2d_rider(bufs):
    def pairs(outs):
        x, y, c, others = _place()
        sib = (x, y, 1 - c)
        return [((_half(o.at[2 * cx + cy], c), sib), (_half(o.at[2 * cx + cy], 1 - c), sib))
                for o in outs for cx, cy in others]
    return _in_place_rider(bufs, pairs)


def _swap_rider(gs):
    n = len(gs)

    def copies(ins, outs, send, recv):
        x, y, c, _ = _place()
        return [pltpu.make_async_remote_copy(src_ref=_half(ins[a], 1 - c), dst_ref=outs[a], send_sem=send.at[a],
                                             recv_sem=recv.at[a], device_id=(x, y, 1 - c), device_id_type=MESH)
                for a in range(n)]

    def start(ins, outs, send, recv):
        for cp in copies(ins, outs, send, recv):
            cp.start()

    def wait(ins, outs, send, recv):
        for cp in copies(ins, outs, send, recv):
            cp.wait()

    outs = [_sds((g.shape[0], g.shape[1] // 2, g.shape[2]), g.dtype) for g in gs]
    return Rider(list(gs), outs, n, start, wait, {})


def _scatter_rider(ss):
    n = len(ss)

    def copies(ins, outs, send, recv):
        x, y, c, others = _place()
        return [pltpu.make_async_remote_copy(
            src_ref=ins[a].at[2 * cx + cy], dst_ref=outs[a].at[j], send_sem=send.at[3 * a + j],
            recv_sem=recv.at[3 * a + j], device_id=(cx, cy, c), device_id_type=MESH)
            for a in range(n) for j, (cx, cy) in enumerate(others)]

    def start(ins, outs, send, recv):
        for cp in copies(ins, outs, send, recv):
            cp.start()

    def wait(ins, outs, send, recv):
        for cp in copies(ins, outs, send, recv):
            cp.wait()

    outs = [_sds((N_CHIPS - 1,) + s.shape[1:], s.dtype) for s in ss]
    return Rider(list(ss), outs, 3 * n, start, wait, {})


def _run_rider(rider, name):
    n_in = len(rider.ins)

    def body(*refs):
        ins, outs = refs[:n_in], refs[n_in:n_in + len(rider.out_shapes)]
        send, recv = refs[n_in + len(rider.out_shapes):]
        rider.start(ins, outs, send, recv)
        rider.wait(ins, outs, send, recv)

    return _comm_call(body, name, rider.ins, rider.out_shapes, rider.n_sem, aliases=rider.aliases)


def _join_rider(bufs):
    def pairs(outs):
        x, y, c, _ = _place()
        sib = (x, y, 1 - c)
        return [((_half(o, c), sib), (_half(o, 1 - c), sib)) for o in outs]
    return _in_place_rider(bufs, pairs, per_buf=1)


def _place_small(v, dev_idx):
    rows, n = v.shape

    def body(idx_ref, v_ref, o_ref):
        o_ref[...] = v_ref[...]

    return _call(body, name="place_small", grid=(1,), prefetch=1,
                 in_specs=[pl.BlockSpec((rows, n), lambda i, idx_ref: (0, 0))],
                 out_specs=pl.BlockSpec((None, rows, n), lambda i, idx_ref: (idx_ref[0], 0, 0)),
                 out_shape=_sds((8, rows, n), v.dtype), sem=("arbitrary",))(dev_idx, v)


def _small_ici_rider(buf):
    def pairs(outs):
        x, y, c, others = _place()
        peers = [(x, y, 1 - c)] + [(cx, cy, c) for cx, cy in others]
        return [((outs[0].at[4 * x + 2 * y + c], peer), (outs[0].at[4 * peer[0] + 2 * peer[1] + peer[2]], peer))
                for peer in peers]
    return _in_place_rider([buf], pairs, per_buf=4)


def _small_d2d_rider(buf):
    def pairs(outs):
        x, y, c, others = _place()
        sib = (x, y, 1 - c)
        return [((outs[0].at[4 * cx + 2 * cy + c], sib), (outs[0].at[4 * cx + 2 * cy + 1 - c], sib))
                for cx, cy in others]
    return _in_place_rider([buf], pairs)


def _add_half(g, p, c):
    _, R, C = g.shape
    half = R // 2

    def body(c_ref, g_ref, p_ref, o_ref):
        o_ref[...] = g_ref[...] + p_ref[...]

    blk = (None, half, C)
    return _call(body, name="add_half", grid=(N_CHIPS,), prefetch=1,
                 in_specs=[pl.BlockSpec(blk, lambda s, c_ref: (s, c_ref[0], 0)),
                           pl.BlockSpec(blk, lambda s, c_ref: (s, 0, 0))],
                 out_specs=pl.BlockSpec(blk, lambda s, c_ref: (s, 0, 0)),
                 out_shape=_sds((N_CHIPS, half, C), F32), sem=("arbitrary",))(c, g, p)


def _sum_owner(s, q, buf, l, me, c):
    _, half, C = s.shape
    tr = half // 2

    def body(me_ref, c_ref, s_ref, q0, q1, q2, buf_ref, o_ref):
        o_ref[...] = ((s_ref[...] + q0[...]) + q1[...]) + q2[...]

    blk = (None, tr, C)
    qspec = lambda j: pl.BlockSpec(blk, lambda i, me_ref, c_ref: (j, i, 0))
    return _call(body, name=f"sum_owner_l{l}", grid=(half // tr,), prefetch=2,
                 in_specs=[pl.BlockSpec(blk, lambda i, me_ref, c_ref: (me_ref[0], i, 0)),
                           qspec(0), qspec(1), qspec(2), ANY],
                 out_specs=pl.BlockSpec(blk, lambda i, me_ref, c_ref: (l, 2 * c_ref[0] + i, 0)),
                 out_shape=_sds(buf.shape, F32), sem=("arbitrary",), aliases={6: 0})(me, c, s, q, q, q, buf)


def _adamw_math(w, g, m, v):
    m = ADAM_B1 * m + (1.0 - ADAM_B1) * g
    v = ADAM_B2 * v + (1.0 - ADAM_B2) * (g * g)
    m_hat = m / (1.0 - ADAM_B1 ** ADAM_STEP)
    v_hat = v / (1.0 - ADAM_B2 ** ADAM_STEP)
    delta = -ADAM_LR * (m_hat / (jnp.sqrt(v_hat) + ADAM_EPS) + ADAM_WD * w)
    return delta, m, v


def _adamw(w, g, m, v, rider=None):
    depth, R, C = w.shape
    tr = max(t for t in range(SUBLANES, R + 1, SUBLANES) if R % t == 0 and t * C * 4 <= 2 * 1024 * 1024)

    def body(w_ref, g_ref, m_ref, v_ref, d_ref, nm_ref, nv_ref):
        d, nm, nv = _adamw_math(w_ref[...], g_ref[...], m_ref[...], v_ref[...])
        d_ref[...] = d
        nm_ref[...] = nm
        nv_ref[...] = nv

    spec = pl.BlockSpec((None, tr, C), lambda l, i: (l, i, 0))
    return _call(body, name="adamw", grid=(depth, R // tr), in_specs=[spec] * 4, out_specs=[spec] * 3,
                 out_shape=[_sds(w.shape, F32)] * 3, sem=("parallel", "parallel"), rider=rider)(w, g, m, v)


def _small_update(gathered, w, m, v):
    _, rows, n = gathered.shape
    tr = rows // 7

    def body(ga_ref, w_ref, m_ref, v_ref, g_ref, d_ref, nm_ref, nv_ref):
        g = ga_ref[0]
        for k in range(1, 8):
            g = g + ga_ref[k]
        d, nm, nv = _adamw_math(w_ref[...], g, m_ref[...], v_ref[...])
        g_ref[...] = g
        d_ref[...] = d
        nm_ref[...] = nm
        nv_ref[...] = nv

    spec = pl.BlockSpec((tr, n), lambda i: (i, 0))
    return _call(body, name="small_update", grid=(rows // tr,),
                 in_specs=[pl.BlockSpec((8, tr, n), lambda i: (0, i, 0)), spec, spec, spec], out_specs=[spec] * 4,
                 out_shape=[_sds((rows, n), F32)] * 4, sem=("parallel",))(gathered, w, m, v)


WEIGHTS = ("g_mix", "w_in", "g_q", "g_k", "w_attn_proj", "lambda_re", "lambda_im", "log_dt", "b_re", "b_im",
           "c_re", "c_im", "d_skip", "w_glu_a", "w_glu_b", "w_out", "g_ffn", "w_ffn_gate", "w_ffn_up", "w_ffn_down")
BIG = ("w_in", "w_attn_proj", "w_glu_a", "w_glu_b", "w_out", "w_ffn_gate", "w_ffn_up", "w_ffn_down")
FLIPPED = ("w_ffn_gate", "w_ffn_up")
SMALL = tuple(n for n in WEIGHTS if n not in BIG)
ROW_VECTORS = ("g_mix", "g_q", "g_k", "d_skip", "g_ffn")
PACK_QUANTUM = LANES * SUBLANES * 7


def _pack_small(parts, extra):
    flat = jnp.concatenate([parts[n].reshape(-1).astype(F32) for n in SMALL] + [extra.reshape(-1)])
    pad = -flat.shape[0] % PACK_QUANTUM
    return jnp.pad(flat, (0, pad)).reshape(-1, LANES)


def _unpack_small(packed, like):
    flat = packed.reshape(-1)
    out, at = {}, 0
    for n in SMALL:
        size = math.prod(like[n].shape)
        out[n] = flat[at:at + size].reshape(like[n].shape)
        at += size
    return out, flat[at]


def kernel(x, g_mix, w_in, g_q, g_k, w_attn_proj, lambda_re, lambda_im, log_dt, b_re, b_im, c_re, c_im, d_skip, w_glu_a, w_glu_b, w_out, g_ffn, w_ffn_gate, w_ffn_up, w_ffn_down, loss_target, m_g_mix, m_w_in, m_g_q, m_g_k, m_w_attn_proj, m_lambda_re, m_lambda_im, m_log_dt, m_b_re, m_b_im, m_c_re, m_c_im, m_d_skip, m_w_glu_a, m_w_glu_b, m_w_out, m_g_ffn, m_w_ffn_gate, m_w_ffn_up, m_w_ffn_down, v_g_mix, v_w_in, v_g_q, v_g_k, v_w_attn_proj, v_lambda_re, v_lambda_im, v_log_dt, v_b_re, v_b_im, v_c_re, v_c_im, v_d_skip, v_w_glu_a, v_w_glu_b, v_w_out, v_g_ffn, v_w_ffn_gate, v_w_ffn_up, v_w_ffn_down):
    given = dict(locals())
    flip = lambda n, a: jnp.swapaxes(a, 1, 2) if n in FLIPPED else a
    W = {n: flip(n, given[n]) for n in WEIGHTS}
    M = {n: flip(n, given["m_" + n]) for n in WEIGHTS}
    V = {n: flip(n, given["v_" + n]) for n in WEIGHTS}
    depth = g_mix.shape[0]
    xl = x.reshape(x.shape[-2:])
    target = loss_target.reshape(loss_target.shape[-2:])
    c_idx = lax.axis_index("c").astype(jnp.int32).reshape(1)
    chip_idx = (2 * lax.axis_index("x") + lax.axis_index("y")).astype(jnp.int32).reshape(1)

    place = lambda l: [_cast_place(W[n], l, chip_idx) for n in BIG]
    bufs = place(0)
    w_in = _run_rider(_gather_d2d_rider(_run_rider(_gather_ici_rider(bufs[:1]), "gather_ici")), "gather_d2d")[0]
    rest, stage = bufs[1:], "ici"
    params, saved, h = [], [], xl
    for l in range(depth):
        p = {"w_in": w_in}
        for n in SMALL:
            p[n] = W[n][l][None] if n in ROW_VECTORS else W[n][l]
        h, sv, p, nxt = _layer_fwd(h, p, rest, stage, place(l + 1) if l + 1 < depth else None)
        params.append(p)
        saved.append(sv)
        if nxt:
            (w_in, rest), stage = nxt, "d2d"
    dx, loss_part = _loss_head(h, target)

    owned = {n: lax.empty(W[n].shape, F32) for n in BIG}
    small_grads = [None] * depth
    pending = None
    for l in reversed(range(depth)):
        dx, small_grads[l], pending, owned = _layer_bwd(dx, saved[l], params[l], pending, owned, l,
                                                        (chip_idx, c_idx))

    ct = {k: jnp.stack([small_grads[l]["ssm_pack_ct"][k] for l in range(depth)])
          for k in small_grads[0]["ssm_pack_ct"]}
    ct["a_re"], ct["a_im"] = jnp.sum(ct["a_re"], axis=2), jnp.sum(ct["a_im"], axis=2)
    ct["a64_re"] = ct["a64_im"] = jnp.zeros_like(ct["a_re"])
    _, pull = jax.vjp(jax.vmap(_ssm_pack), *[W[n] for n in SSM_PARAMS])
    stacked = dict(zip(SSM_PARAMS, pull(ct)))
    for n in SMALL:
        if n not in stacked:
            stacked[n] = jnp.stack([small_grads[l][n] for l in range(depth)])
    zero = jnp.zeros((1,), F32)
    dev_idx = (4 * lax.axis_index("x") + 2 * lax.axis_index("y") + lax.axis_index("c")).astype(jnp.int32).reshape(1)
    gathered = _place_small(_pack_small(stacked, loss_part), dev_idx)
    outs = _run_rider(_join_riders(_scatter_rider([pending[n] for n in LATE]), _small_ici_rider(gathered)),
                      "scatter_to_owners")
    for n, q in zip(LATE, outs[:len(LATE)]):
        owned[n] = _sum_owner(pending[n], q, owned[n], 0, chip_idx, c_idx)
    outs = _run_rider(_join_riders(_join_rider([owned[n] for n in BIG]), _small_d2d_rider(outs[len(LATE)])),
                      "join_halves")
    reduced, gathered = dict(zip(BIG, outs[:len(BIG)])), outs[len(BIG)]
    grads, delta, new_m, new_v = {}, {}, {}, {}
    for n in BIG:
        outs = (reduced[n], *_adamw(W[n], reduced[n], M[n], V[n]))
        grads[n], delta[n], new_m[n], new_v[n] = [flip(n, t) for t in outs]
    gs, ds, nms, nvs = _small_update(gathered, _pack_small(W, zero), _pack_small(M, zero), _pack_small(V, zero))
    sg, loss = _unpack_small(gs, W)
    sd, _ = _unpack_small(ds, W)
    sm, _ = _unpack_small(nms, W)
    sv_, _ = _unpack_small(nvs, W)
    for n in SMALL:
        grads[n], delta[n], new_m[n], new_v[n] = sg[n], sd[n], sm[n], sv_[n]

    return (loss, dx.reshape(x.shape), *[grads[n] for n in WEIGHTS], *[delta[n] for n in WEIGHTS],
            *[new_m[n] for n in WEIGHTS], *[new_v[n] for n in WEIGHTS])
```

```python
import collections
import functools
import math

import jax
import jax.numpy as jnp
from jax import lax
from jax.experimental import pallas as pl
from jax.experimental.pallas import tpu as pltpu

F32 = jnp.float32
BF16 = jnp.bfloat16

D_MODEL = 1024
DEPTH = 4
HEAD_DIM = 64
N_HEADS = 8
ATTN_WIDTH = N_HEADS * HEAD_DIM
ATTN_PATTERN = ((128, 1), (512, 4), (2048, 16))
N_GROUPS = len(ATTN_PATTERN)
BLK = 128
SSM_WIDTH = 512
SSM_GROUP = 16
SSM_GROUPS = 32
SSM_STATE = 64
D_FF = 2816
IN_COLS = 7168
EPS = 1e-6
ADAM_LR, ADAM_B1, ADAM_B2, ADAM_EPS, ADAM_WD, ADAM_STEP = 0.001, 0.9, 0.999, 1e-08, 0.01, 10

N_CHIPS = 4
MESH = pl.DeviceIdType.MESH

LANES = 128
SUBLANES = 8
VMEM_LIMIT = 56 * 1024 * 1024

TM = 512
TM_PROJ = 1024
TL_WGRAD = 2048
TM_MIX = 512

SSM_TB = 512
SSM_TC = 64
SSM_SUB = SUBLANES
SSM_PITCH = 68
N_SLAB = SSM_GROUPS * SSM_STATE // LANES
SSM_WIN = 256
N_PAIR = N_SLAB // 2
PAIRS_PER_WIN = 4
SCAN_GROUP = 4


def _params(sem=None, collective=False):
    return pltpu.CompilerParams(dimension_semantics=sem, vmem_limit_bytes=VMEM_LIMIT)


ANY = pl.BlockSpec(memory_space=pl.ANY)

Rider = collections.namedtuple("Rider", "ins out_shapes n_sem start wait aliases")


class _SemWindow:
    def __init__(self, ref, offset):
        self.ref, self.offset = ref, offset

    @property
    def at(self):
        return self

    def __getitem__(self, k):
        return self.ref.at[self.offset + k]


def _join_riders(*riders):
    riders = [r for r in riders if r is not None]
    if len(riders) <= 1:
        return riders[0] if riders else None

    def each(fn_name):
        def run(ins, outs, send, recv):
            i = o = s = 0
            for r in riders:
                getattr(r, fn_name)(ins[i:i + len(r.ins)], outs[o:o + len(r.out_shapes)],
                                    _SemWindow(send, s), _SemWindow(recv, s))
                i, o, s = i + len(r.ins), o + len(r.out_shapes), s + r.n_sem
        return run

    aliases, i, o = {}, 0, 0
    for r in riders:
        aliases.update({i + a: o + b for a, b in r.aliases.items()})
        i, o = i + len(r.ins), o + len(r.out_shapes)
    return Rider([t for r in riders for t in r.ins], [t for r in riders for t in r.out_shapes],
                 sum(r.n_sem for r in riders), each("start"), each("wait"), aliases)


def _with_rider(body, rider, grid, prefetch, n_in, n_out, n_scratch):
    n_rin, n_rout = len(rider.ins), len(rider.out_shapes)

    def hosted(*refs):
        pre, rest = refs[:prefetch], refs[prefetch:]
        ins, rin = rest[:n_in], rest[n_in:n_in + n_rin]
        o0 = n_in + n_rin
        outs, rout = rest[o0:o0 + n_out], rest[o0 + n_out:o0 + n_out + n_rout]
        s0 = o0 + n_out + n_rout
        scr, (send, recv) = rest[s0:s0 + n_scratch], rest[s0 + n_scratch:]
        first = functools.reduce(jnp.logical_and, [pl.program_id(k) == 0 for k in range(len(grid))])
        last = functools.reduce(jnp.logical_and, [pl.program_id(k) == grid[k] - 1 for k in range(len(grid))])

        @pl.when(first)
        def _():
            rider.start(rin, rout, send, recv)

        body(*pre, *ins, *outs, *scr)

        @pl.when(last)
        def _():
            rider.wait(rin, rout, send, recv)

    return hosted


def _call(body, *, name, grid, in_specs, out_specs, out_shape, scratch=(), sem=None, aliases=None,
          prefetch=0, rider=None):
    if rider is not None:
        single = not isinstance(out_specs, (list, tuple))
        out_specs = [out_specs] if single else list(out_specs)
        out_shape = [out_shape] if single else list(out_shape)
        body = _with_rider(body, rider, grid, prefetch, len(in_specs), len(out_specs), len(scratch))
        aliases = dict(aliases or {})
        aliases.update({prefetch + len(in_specs) + k: len(out_specs) + v for k, v in rider.aliases.items()})
        in_specs = list(in_specs) + [ANY] * len(rider.ins)
        out_specs = out_specs + [ANY] * len(rider.out_shapes)
        out_shape = out_shape + list(rider.out_shapes)
        scratch = list(scratch) + [pltpu.SemaphoreType.DMA((rider.n_sem,)), pltpu.SemaphoreType.DMA((rider.n_sem,))]
        sem = ("arbitrary",) * len(grid)
        fn = _call(body, name=name + "_host", grid=grid, in_specs=in_specs, out_specs=out_specs, out_shape=out_shape,
                   scratch=scratch, sem=sem, aliases=aliases, prefetch=prefetch)
        return lambda *args: fn(*args, *rider.ins)
    kw = {}
    if aliases:
        kw["input_output_aliases"] = aliases
    if prefetch:
        gs = pltpu.PrefetchScalarGridSpec(num_scalar_prefetch=prefetch, grid=grid, in_specs=in_specs,
                                          out_specs=out_specs, scratch_shapes=list(scratch))
        return pl.pallas_call(body, name=name, grid_spec=gs, out_shape=out_shape,
                              compiler_params=_params(sem), **kw)
    return pl.pallas_call(body, name=name, grid=grid, in_specs=in_specs, out_specs=out_specs,
                          out_shape=out_shape, scratch_shapes=list(scratch),
                          compiler_params=_params(sem), **kw)


def _sds(shape, dtype):
    return jax.ShapeDtypeStruct(shape, dtype)


def _sigmoid(v):
    return 1.0 / (1.0 + jnp.exp(-v))


def _dot(a, b):
    return jnp.dot(a, b, preferred_element_type=F32)


def _dot_nt(a, b):
    return lax.dot_general(a, b, (((1,), (1,)), ((), ())), preferred_element_type=F32)


def _dot_tn(a, b):
    return lax.dot_general(a, b, (((0,), (0,)), ((), ())), preferred_element_type=F32)


def _in_proj_fwd(x, g, w, rider=None):
    L = x.shape[0]
    ns = w.shape[2]
    tn = ns
    nj = ns // tn
    TM = TM_PROJ

    def body(x_ref, g_ref, w_ref, z_ref, h_ref):
        @pl.when(pl.program_id(1) == 0)
        def _():
            xv = x_ref[...]
            r = lax.rsqrt(jnp.mean(xv * xv, axis=-1, keepdims=True) + EPS)
            h_ref[...] = (xv * r * g_ref[...]).astype(BF16)
        z_ref[...] = _dot(h_ref[...], w_ref[...]).astype(BF16)

    return _call(
        body, name="in_proj_fwd", grid=(L // TM, N_CHIPS * nj),
        in_specs=[pl.BlockSpec((TM, D_MODEL), lambda i, j: (i, 0)),
                  pl.BlockSpec((1, D_MODEL), lambda i, j: (0, 0)),
                  pl.BlockSpec((None, D_MODEL, tn), lambda i, j: (j // nj, 0, j % nj))],
        out_specs=[pl.BlockSpec((TM, tn), lambda i, j: (i, j)),
                   pl.BlockSpec((TM, D_MODEL), lambda i, j: (i, 0))],
        out_shape=[_sds((L, N_CHIPS * ns), BF16), _sds((L, D_MODEL), BF16)],
        sem=("parallel", "arbitrary"), rider=rider)(x, g, w)


DL_TILE = 512
SCALE = HEAD_DIM ** -0.5


def _perm_matrix(d):
    rho = jnp.arange(DL_TILE)
    src = rho // (DL_TILE // d) + d * (rho % (DL_TILE // d))
    return (src[:, None] == jnp.arange(DL_TILE)[None, :]).astype(BF16)


def _head_sum_matrix():
    h = jnp.arange(ATTN_WIDTH) // HEAD_DIM
    return (h[:, None] == h[None, :]).astype(BF16)


def _split(v):
    hi = v.astype(BF16)
    return hi, (v - hi.astype(F32)).astype(BF16)


def _head_sum(v, hs):
    vb = v.astype(BF16)
    half = ATTN_WIDTH // 2
    blk = hs[:half, :half]
    return jnp.concatenate([_dot(vb[:, :half], blk), _dot(vb[:, half:], blk)], axis=1)


def _permute(pm, v):
    hi, lo = _split(v)
    return _dot(pm, hi) + _dot(pm, lo)


def _dl_view(t, d):
    if d * BLK <= DL_TILE:
        return t
    return t.reshape(t.shape[0] // DL_TILE, d, DL_TILE // d, t.shape[1])


def _dl_spec(d, width, which):
    if d * BLK <= DL_TILE:
        per_tile = DL_TILE // (d * BLK)
        return pl.BlockSpec((BLK, width), lambda r, n: ((which(n) // per_tile) * (DL_TILE // BLK)
                                                       + r * per_tile + which(n) % per_tile, 0))
    tiles = d * BLK // DL_TILE
    return pl.BlockSpec((tiles, None, DL_TILE // d, width), lambda r, n: (which(n), r, 0, 0))


def _dl_read(ref):
    v = ref[...]
    return v if v.ndim == 2 else v.reshape(BLK, v.shape[-1])


def _dl_write(ref, v):
    ref[...] = v if len(ref.shape) == 2 else v.reshape(ref.shape)


def _qkv_prep(z, gq_t, gk_t, rider=None):
    L = z.shape[0]
    qkv_w = N_GROUPS * ATTN_WIDTH

    def body(zq_ref, zk_ref, zv_ref, gq_ref, gk_ref, hs_ref, p1_ref, p2_ref, *outs):
        hs = hs_ref[...]
        perms = (None, p1_ref[...], p2_ref[...])
        for g in range(N_GROUPS):
            cols = slice(g * ATTN_WIDTH, (g + 1) * ATTN_WIDTH)
            xq = zq_ref[:, cols].astype(F32)
            xk = zk_ref[:, cols].astype(F32)
            rq = lax.rsqrt(_head_sum(xq * xq, hs) * (1.0 / HEAD_DIM) + EPS)
            rk = lax.rsqrt(_head_sum(xk * xk, hs) * (1.0 / HEAD_DIM) + EPS)
            vals = [(xq * rq * (gq_ref[...] * SCALE)).astype(BF16), (xk * rk * gk_ref[...]).astype(BF16),
                    zv_ref[:, cols]]
            for j, t in enumerate(vals):
                if perms[g] is not None:
                    t = _dot(perms[g], t).astype(BF16)
                outs[3 * g + j][...] = t

    tile = pl.BlockSpec((DL_TILE, ATTN_WIDTH), lambda i: (i, 0))
    mat = pl.BlockSpec((DL_TILE, DL_TILE), lambda i: (0, 0))
    vec = pl.BlockSpec((1, ATTN_WIDTH), lambda i: (0, 0))
    outs = _call(
        body, name="qkv_prep", grid=(L // DL_TILE,),
        in_specs=[pl.BlockSpec((DL_TILE, qkv_w), lambda i: (i, 0)), pl.BlockSpec((DL_TILE, qkv_w), lambda i: (i, 1)),
                  pl.BlockSpec((DL_TILE, qkv_w), lambda i: (i, 2)), vec, vec, mat, mat, mat],
        out_specs=[tile] * 9, out_shape=[_sds((L, ATTN_WIDTH), BF16)] * 9,
        sem=("parallel",), rider=rider)(z, z, z, gq_t, gk_t, _head_sum_matrix(), _perm_matrix(ATTN_PATTERN[1][1]),
                                        _perm_matrix(ATTN_PATTERN[2][1]))
    return [tuple(outs[3 * g:3 * g + 3]) for g in range(N_GROUPS)], list(outs[3 * N_GROUPS:])


def _pair_masks():
    lane = lax.broadcasted_iota(jnp.int32, (1, LANES), 1)
    return lane < HEAD_DIM, lane >= HEAD_DIM


def _attn_fwd(qs, ks, v, gi):
    L = qs.shape[0]
    _, d = ATTN_PATTERN[gi]
    nb = L // (d * BLK)

    def body(q_ref, kc_ref, kp_ref, vc_ref, vp_ref, o_ref, l_ref):
        n = pl.program_id(1)
        qi = lax.broadcasted_iota(jnp.int32, (BLK, 2 * BLK), 0)
        kj = lax.broadcasted_iota(jnp.int32, (BLK, 2 * BLK), 1)
        prev = kj < BLK
        mask = jnp.logical_and(jnp.where(prev, kj, qi) >= jnp.where(prev, qi, kj - BLK),
                               kj >= jnp.where(n > 0, 0, BLK))
        q = _dl_read(q_ref)
        kw = jnp.concatenate([_dl_read(kp_ref), _dl_read(kc_ref)], axis=0)
        vw = jnp.concatenate([_dl_read(vp_ref), _dl_read(vc_ref)], axis=0)
        one = jnp.ones((2 * BLK, LANES), BF16)
        o_parts, l_parts = [], []
        for hp in range(N_HEADS // 2):
            ls = slice(hp * LANES, (hp + 1) * LANES)
            qp, kp_, vp_ = q[:, ls], kw[:, ls], vw[:, ls]
            num = jnp.zeros((BLK, LANES), F32)
            den = jnp.zeros((BLK, LANES), F32)
            mb = jnp.zeros((BLK, LANES), F32)
            for he in _pair_masks():
                s = jnp.where(mask, _dot_nt(jnp.where(he, qp, 0), kp_), -jnp.inf)
                m = jnp.max(s, axis=-1, keepdims=True)
                p = jnp.exp(s - m).astype(BF16)
                acc = _dot(p, jnp.concatenate([jnp.where(he, vp_, 0), jnp.where(he, one, 0)], axis=1))
                num += acc[:, :LANES]
                den += acc[:, LANES:]
                mb = jnp.where(he, m, mb)
            o_parts.append((num / den).astype(BF16))
            l_parts.append(mb + jnp.log(den))
        _dl_write(o_ref, jnp.concatenate(o_parts, axis=1))
        _dl_write(l_ref, jnp.concatenate(l_parts, axis=1))

    cur = _dl_spec(d, ATTN_WIDTH, lambda n: n)
    prev = _dl_spec(d, ATTN_WIDTH, lambda n: jnp.maximum(n - 1, 0))
    view = lambda t: _dl_view(t, d)
    o, l = _call(
        body, name=f"attn_fwd_g{gi}", grid=(d, nb), in_specs=[cur, cur, prev, cur, prev], out_specs=[cur, cur],
        out_shape=[_sds(view(qs).shape, BF16), _sds(view(qs).shape, F32)],
        sem=("parallel", "parallel"))(view(qs), view(ks), view(ks), view(v), view(v))
    return o.reshape(L, ATTN_WIDTH), l.reshape(L, ATTN_WIDTH)


def _to_token_order(os_, ls_, pts):
    o_tok, l_tok = [], []
    for o, l, pt in zip(os_, ls_, pts):
        if pt is None:
            o_tok.append(o.astype(F32))
            l_tok.append(l)
        else:
            o_tok.append(_dot(pt, o))
            l_tok.append(_permute(pt, l))
    return o_tok, l_tok


def _combine_fwd(os_, ls_):
    L = os_[0].shape[0]

    def body(o0, o1, o2, l0, l1, l2, pt1_ref, pt2_ref, a_ref):
        o_tok, l_tok = _to_token_order((o0[...], o1[...], o2[...]), (l0[...], l1[...], l2[...]),
                                       (None, pt1_ref[...], pt2_ref[...]))
        w = _combine_weights(*l_tok)
        a_ref[...] = (w[0] * o_tok[0] + w[1] * o_tok[1] + w[2] * o_tok[2]).astype(BF16)

    tile = pl.BlockSpec((DL_TILE, ATTN_WIDTH), lambda i: (i, 0))
    mat = pl.BlockSpec((DL_TILE, DL_TILE), lambda i: (0, 0))
    return _call(body, name="combine_fwd", grid=(L // DL_TILE,), in_specs=[tile] * 6 + [mat, mat], out_specs=tile,
                 out_shape=_sds((L, ATTN_WIDTH), BF16), sem=("parallel",))(
                     *os_, *ls_, _perm_matrix(ATTN_PATTERN[1][1]).T, _perm_matrix(ATTN_PATTERN[2][1]).T)


def _gelu(v):
    c = math.sqrt(2.0 / math.pi)
    return 0.5 * v * (1.0 + jnp.tanh(c * (v + 0.044715 * v * v * v)))


def _gelu_grad(v):
    c = math.sqrt(2.0 / math.pi)
    t = jnp.tanh(c * (v + 0.044715 * v * v * v))
    return 0.5 * (1.0 + t) + 0.5 * v * (1.0 - t * t) * c * (1.0 + 3.0 * 0.044715 * v * v)


def _ssm_fill(u, bwre_ref, bwim_ref, sre, sim):
    for k2 in range(N_PAIR):
        uw = u[:, _win_cols(k2)]
        _to_slabs(sre, k2, _dot(uw, bwre_ref[k2]))
        _to_slabs(sim, k2, _dot(uw, bwim_ref[k2]))


def _win_cols(k2):
    w = k2 // PAIRS_PER_WIN
    return slice(w * SSM_WIN, (w + 1) * SSM_WIN)


def _to_slabs(ref, k2, v):
    for half in range(2):
        for j in range(SSM_SUB):
            ref[2 * k2 + half, j * SSM_PITCH:j * SSM_PITCH + SSM_TC, :] = (
                v[j * SSM_TC:(j + 1) * SSM_TC, half * LANES:(half + 1) * LANES])


def _rows(i):
    return pl.ds(i, SSM_SUB, stride=SSM_PITCH)


def _slab_rows(ref, k):
    return jnp.concatenate([ref[k, j * SSM_PITCH:j * SSM_PITCH + SSM_TC, :] for j in range(SSM_SUB)], axis=0)


def _pair_rows(ref, k2):
    return jnp.concatenate([_slab_rows(ref, 2 * k2), _slab_rows(ref, 2 * k2 + 1)], axis=1).astype(BF16)


def _bcast(ref, k):
    return jnp.broadcast_to(ref[pl.ds(k, 1), :], (SSM_SUB, LANES))


def _scan(sre, sim, are_ref, aim_ref, k0, init, *, reverse, store, sign=1.0):
    ar = [_bcast(are_ref, k0 + kk) for kk in range(SCAN_GROUP)]
    ai = [sign * _bcast(aim_ref, k0 + kk) for kk in range(SCAN_GROUP)]

    def step(t, carry):
        i = SSM_TC - 1 - t if reverse else t
        out = []
        for kk in range(SCAN_GROUP):
            k = k0 + kk
            xr, xi = carry[2 * kk], carry[2 * kk + 1]
            nr = ar[kk] * xr - ai[kk] * xi + sre[k, _rows(i), :]
            ni = ar[kk] * xi + ai[kk] * xr + sim[k, _rows(i), :]
            if store:
                sre[k, _rows(i), :] = nr
                sim[k, _rows(i), :] = ni
            out += [nr, ni]
        return tuple(out)

    flat = []
    for re, im in init:
        flat += [re, im]
    res = lax.fori_loop(0, SSM_TC // 2, lambda t, c: step(2 * t + 1, step(2 * t, c)), tuple(flat))
    return [(res[2 * kk], res[2 * kk + 1]) for kk in range(SCAN_GROUP)]


def _ssm_seeds(ends_re, ends_im, a64re_ref, a64im_ref, carry_re, carry_im, seed_re, seed_im, k,
               *, reverse, sign=1.0):
    ar = a64re_ref[pl.ds(k, 1), :]
    ai = sign * a64im_ref[pl.ds(k, 1), :]
    cr = carry_re[pl.ds(k, 1), :]
    ci = carry_im[pl.ds(k, 1), :]
    order = range(SSM_SUB - 1, -1, -1) if reverse else range(SSM_SUB)
    for j in order:
        seed_re[k, pl.ds(j, 1), :] = cr
        seed_im[k, pl.ds(j, 1), :] = ci
        er = ends_re[k, pl.ds(j, 1), :]
        ei = ends_im[k, pl.ds(j, 1), :]
        cr, ci = ar * cr - ai * ci + er, ar * ci + ai * cr + ei
    carry_re[pl.ds(k, 1), :] = cr
    carry_im[pl.ds(k, 1), :] = ci


def _ssm_specs_consts():
    c2 = pl.BlockSpec((N_SLAB, LANES), lambda b: (0, 0))
    c3 = pl.BlockSpec((N_PAIR, SSM_WIN, SSM_WIN), lambda b: (0, 0, 0))
    return c2, c3


def _ssm_scratch():
    rows = SSM_SUB * SSM_PITCH
    return [pltpu.VMEM((N_SLAB, rows, LANES), F32), pltpu.VMEM((N_SLAB, rows, LANES), F32)]


def _ssm_fwd(z, pk, dskip, rider=None):
    L = z.shape[0]
    nb = L // SSM_TB
    ucol = (3 * N_GROUPS * ATTN_WIDTH) // SSM_WIDTH

    def body(u_ref, are_ref, aim_ref, a64re_ref, a64im_ref, bwre_ref, bwim_ref, cwre_ref, cwim_ref, d_ref,
             ypre_ref, yact_ref, sdre_ref, sdim_ref, sre, sim, carry_re, carry_im, ends_re, ends_im,
             seed_re, seed_im):
        @pl.when(pl.program_id(0) == 0)
        def _():
            carry_re[...] = jnp.zeros_like(carry_re)
            carry_im[...] = jnp.zeros_like(carry_im)

        u = u_ref[...]
        _ssm_fill(u, bwre_ref, bwim_ref, sre, sim)
        zero = jnp.zeros((SSM_SUB, LANES), F32)
        for k0 in range(0, N_SLAB, SCAN_GROUP):
            ends = _scan(sre, sim, are_ref, aim_ref, k0, [(zero, zero)] * SCAN_GROUP, reverse=False, store=False)
            for kk in range(SCAN_GROUP):
                ends_re[k0 + kk] = ends[kk][0]
                ends_im[k0 + kk] = ends[kk][1]
            for kk in range(SCAN_GROUP):
                _ssm_seeds(ends_re, ends_im, a64re_ref, a64im_ref, carry_re, carry_im, seed_re, seed_im,
                           k0 + kk, reverse=False)
            init = [(seed_re[k0 + kk], seed_im[k0 + kk]) for kk in range(SCAN_GROUP)]
            _scan(sre, sim, are_ref, aim_ref, k0, init, reverse=False, store=True)
        sdre_ref[...] = seed_re[...]
        sdim_ref[...] = seed_im[...]
        for w in range(N_PAIR // PAIRS_PER_WIN):
            acc = jnp.zeros((SSM_TB, SSM_WIN), F32)
            for kk in range(PAIRS_PER_WIN):
                k2 = w * PAIRS_PER_WIN + kk
                acc += _dot(_pair_rows(sre, k2), cwre_ref[k2])
                acc -= _dot(_pair_rows(sim, k2), cwim_ref[k2])
            cols = _win_cols(w * PAIRS_PER_WIN)
            ypre = acc + d_ref[:, cols] * u[:, cols].astype(F32)
            ypre_ref[:, cols] = ypre
            yact_ref[:, cols] = _gelu(ypre).astype(BF16)

    c2, c3 = _ssm_specs_consts()
    seed_spec = pl.BlockSpec((None, N_SLAB, SSM_SUB, LANES), lambda b: (b, 0, 0, 0))
    small = pltpu.VMEM((N_SLAB, LANES), F32)
    tile = pltpu.VMEM((N_SLAB, SSM_SUB, LANES), F32)
    return _call(
        body, name="ssm_fwd", grid=(nb,),
        in_specs=[pl.BlockSpec((SSM_TB, SSM_WIDTH), lambda b: (b, ucol)), c2, c2, c2, c2, c3, c3, c3, c3,
                  pl.BlockSpec((1, SSM_WIDTH), lambda b: (0, 0))],
        out_specs=[pl.BlockSpec((SSM_TB, SSM_WIDTH), lambda b: (b, 0)),
                   pl.BlockSpec((SSM_TB, SSM_WIDTH), lambda b: (b, 0)), seed_spec, seed_spec],
        out_shape=[_sds((L, SSM_WIDTH), F32), _sds((L, SSM_WIDTH), BF16),
                   _sds((nb, N_SLAB, SSM_SUB, LANES), F32), _sds((nb, N_SLAB, SSM_SUB, LANES), F32)],
        scratch=_ssm_scratch() + [small, small, tile, tile, tile, tile],
        sem=("arbitrary",), rider=rider)(z, pk["a_re"], pk["a_im"], pk["a64_re"], pk["a64_im"],
                                         pk["bw_re"].astype(BF16), pk["bw_im"].astype(BF16),
                                         pk["cw_re"].astype(BF16), pk["cw_im"].astype(BF16), dskip)


def _combine_weights(l0, l1, l2):
    m = jnp.maximum(jnp.maximum(l0, l1), l2)
    e0, e1, e2 = jnp.exp(l0 - m), jnp.exp(l1 - m), jnp.exp(l2 - m)
    inv = 1.0 / (e0 + e1 + e2)
    return e0 * inv, e1 * inv, e2 * inv


def _mix_fwd(x, z, a, yact, w_ap, w_ga, w_gb, w_out):
    L = x.shape[0]
    cs = D_MODEL // N_CHIPS
    ga_col = (3 * N_GROUPS * ATTN_WIDTH + SSM_WIDTH) // D_MODEL

    def body(x_ref, ga_ref, gs_ref, a_ref, y_ref, wap_ref, wga_ref, wgb_ref, wout_ref,
             x1_ref, aout_ref, sa_ref, sb_ref, mix_ref):
        a = a_ref[...]
        y = y_ref[...]
        for s in range(N_CHIPS):
            cols = slice(s * cs, (s + 1) * cs)
            aout_ref[:, cols] = _dot(a, wap_ref[s]).astype(BF16)
            sa_ref[:, cols] = _dot(y, wga_ref[s]).astype(BF16)
            sb_ref[:, cols] = _dot(y, wgb_ref[s]).astype(BF16)
        s_out = sa_ref[...].astype(F32) * _sigmoid(sb_ref[...].astype(F32))
        mix = (_sigmoid(ga_ref[...].astype(F32)) * aout_ref[...].astype(F32)
               + _sigmoid(gs_ref[...].astype(F32)) * s_out).astype(BF16)
        mix_ref[...] = mix
        x1_ref[...] = x_ref[...] + _dot(mix, wout_ref[...])

    tok = lambda w: pl.BlockSpec((TM_MIX, w), lambda i: (i, 0))
    wsm = pl.BlockSpec((N_CHIPS, ATTN_WIDTH, cs), lambda i: (0, 0, 0))
    return _call(
        body, name="mix_fwd", grid=(L // TM_MIX,),
        in_specs=[tok(D_MODEL), pl.BlockSpec((TM_MIX, D_MODEL), lambda i: (i, ga_col)),
                  pl.BlockSpec((TM_MIX, D_MODEL), lambda i: (i, ga_col + 1))]
                 + [tok(ATTN_WIDTH)] * 2 + [wsm, wsm, wsm, pl.BlockSpec((D_MODEL, D_MODEL), lambda i: (0, 0))],
        out_specs=[tok(D_MODEL), tok(D_MODEL), tok(D_MODEL), tok(D_MODEL), tok(D_MODEL)],
        out_shape=[_sds((L, D_MODEL), F32)] + [_sds((L, D_MODEL), BF16)] * 4,
        sem=("parallel",))(x, z, z, a, yact, w_ap, w_ga, w_gb, w_out.reshape(D_MODEL, D_MODEL))


def _ffn_fwd(x1, g, w_g, w_u, w_d, rider=None):
    L = x1.shape[0]
    fs = D_FF // N_CHIPS
    TM = TM_PROJ

    def body(x_ref, g_ref, wg_ref, wu_ref, wd_ref, x2_ref, h_ref, gate_ref, up_ref, act_ref, acc):
        s = pl.program_id(1)

        @pl.when(s == 0)
        def _():
            xv = x_ref[...]
            r = lax.rsqrt(jnp.mean(xv * xv, axis=-1, keepdims=True) + EPS)
            h_ref[...] = (xv * r * g_ref[...]).astype(BF16)
            acc[...] = jnp.zeros_like(acc)

        h = h_ref[...]
        gate = _dot_nt(h, wg_ref[...])
        up = _dot_nt(h, wu_ref[...])
        act = (gate * _sigmoid(gate) * up).astype(BF16)
        gate_ref[...] = gate.astype(BF16)
        up_ref[...] = up.astype(BF16)
        act_ref[...] = act
        acc[...] += _dot(act, wd_ref[...])

        @pl.when(s == N_CHIPS - 1)
        def _():
            x2_ref[...] = x_ref[...] + acc[...]

    tok = pl.BlockSpec((TM, D_MODEL), lambda i, s: (i, 0))
    ffs = pl.BlockSpec((None, TM, fs), lambda i, s: (s, i, 0))
    return _call(
        body, name="ffn_fwd", grid=(L // TM, N_CHIPS),
        in_specs=[tok, pl.BlockSpec((1, D_MODEL), lambda i, s: (0, 0))]
                 + [pl.BlockSpec((None, fs, D_MODEL), lambda i, s: (s, 0, 0))] * 3,
        out_specs=[tok, tok, ffs, ffs, ffs],
        out_shape=[_sds((L, D_MODEL), F32), _sds((L, D_MODEL), BF16)] + [_sds((N_CHIPS, L, fs), BF16)] * 3,
        scratch=[pltpu.VMEM((TM, D_MODEL), F32)],
        sem=("parallel", "arbitrary"), rider=rider)(x1, g, w_g, w_u, w_d)


def _loss_head(xl, target):
    L = xl.shape[0]

    def body(x_ref, t_ref, dx_ref, loss_ref, acc):
        i = pl.program_id(0)

        @pl.when(i == 0)
        def _():
            acc[...] = jnp.zeros_like(acc)

        e = x_ref[...] - t_ref[...]
        dx_ref[...] = e * (1.0 / D_MODEL)
        acc[...] += jnp.sum((e * e).reshape(TM // SUBLANES, SUBLANES, D_MODEL), axis=0)

        @pl.when(i == pl.num_programs(0) - 1)
        def _():
            loss_ref[...] = (0.5 / D_MODEL) * jnp.sum(acc[...]).reshape(1, 1)

    tok = pl.BlockSpec((TM, D_MODEL), lambda i: (i, 0))
    return _call(
        body, name="loss_head", grid=(L // TM,), in_specs=[tok, tok],
        out_specs=[tok, pl.BlockSpec((1, 1), lambda i: (0, 0))],
        out_shape=[_sds((L, D_MODEL), F32), _sds((1, 1), F32)],
        scratch=[pltpu.VMEM((SUBLANES, D_MODEL), F32)], sem=("arbitrary",))(xl, target)


def _ssm_pack(lam_re, lam_im, log_dt, b_re, b_im, c_re, c_im):
    dt = jnp.exp(log_dt)[:, None]
    mag = jnp.exp(lam_re * dt)
    ang = lam_im * dt
    ar = mag * jnp.cos(ang)
    ai = mag * jnp.sin(ang)
    nr = ar - 1.0
    ni = ai
    den = lam_re * lam_re + lam_im * lam_im
    cr = ((nr * lam_re + ni * lam_im) / den)[..., None]
    ci = ((ni * lam_re - nr * lam_im) / den)[..., None]
    bbr = cr * b_re - ci * b_im
    bbi = cr * b_im + ci * b_re
    gpp = SSM_WIN // SSM_STATE
    gpw = SSM_WIN // SSM_GROUP
    k2 = jnp.arange(N_PAIR)[:, None, None]
    gs = jnp.arange(gpp)[None, :, None]
    gl = jnp.arange(gpw)[None, None, :]
    same = (gl == gpp * (k2 % PAIRS_PER_WIN) + gs).astype(F32)

    def b_windows(bb):
        return jnp.einsum('kgl,kgpc->klcgp', same, bb.reshape(N_PAIR, gpp, SSM_STATE, SSM_GROUP)).reshape(
            N_PAIR, SSM_WIN, SSM_WIN)

    def c_windows(cc):
        return jnp.einsum('kgl,kgcp->kgplc', same, cc.reshape(N_PAIR, gpp, SSM_GROUP, SSM_STATE)).reshape(
            N_PAIR, SSM_WIN, SSM_WIN)

    pr, pi = ar, ai
    for _ in range(int(math.log2(SSM_TC))):
        pr, pi = pr * pr - pi * pi, 2.0 * pr * pi
    return dict(a_re=ar.reshape(N_SLAB, LANES), a_im=ai.reshape(N_SLAB, LANES),
                a64_re=pr.reshape(N_SLAB, LANES), a64_im=pi.reshape(N_SLAB, LANES),
                bw_re=b_windows(bbr), bw_im=b_windows(bbi), cw_re=c_windows(c_re), cw_im=c_windows(c_im))


def _layer_fwd(x, p, rest, rest_stage, next_bufs=None):
    first = {"ici": _gather_ici_rider, "d2d": _gather_d2d_rider}[rest_stage]
    outs = _in_proj_fwd(x, p["g_mix"], p["w_in"], first(rest))
    (z, h), rest = outs[:2], list(outs[2:])
    qkv, got = _qkv_prep(z, jnp.tile(p["g_q"], (1, N_HEADS)), jnp.tile(p["g_k"], (1, N_HEADS)),
                         _gather_d2d_rider(rest) if rest_stage == "ici" else None)
    p = {**p, **dict(zip(BIG[1:], got if rest_stage == "ici" else rest))}
    os_, ls_ = [], []
    for gi in range(N_GROUPS):
        o, l = _attn_fwd(*qkv[gi], gi)
        os_.append(o)
        ls_.append(l)
    a = _combine_fwd(os_, ls_)
    pk = _ssm_pack(p["lambda_re"], p["lambda_im"], p["log_dt"], p["b_re"], p["b_im"], p["c_re"], p["c_im"])
    outs = _ssm_fwd(z, pk, p["d_skip"], _gather_ici_rider(next_bufs[:1]) if next_bufs else None)
    (ypre, yact, sd_re, sd_im), next_in = outs[:4], list(outs[4:])
    x1, aout, sa, sb, mix = _mix_fwd(x, z, a, yact, p["w_attn_proj"], p["w_glu_a"], p["w_glu_b"], p["w_out"])
    outs = _ffn_fwd(x1, p["g_ffn"], p["w_ffn_gate"], p["w_ffn_up"], p["w_ffn_down"],
                    _join_riders(_gather_ici_rider(next_bufs[1:]), _gather_d2d_rider(next_in)) if next_bufs else None)
    x2, h2, gate, up, act = outs[:5]
    nxt = (outs[-1], list(outs[5:-1])) if next_bufs else None
    saved = dict(x=x, z=z, h=h, qkv=qkv, os=os_, ls=ls_, pk=pk, ypre=ypre, yact=yact, sd_re=sd_re, sd_im=sd_im,
                 x1=x1, a=a, aout=aout, sa=sa, sb=sb, mix=mix, h2=h2, gate=gate, up=up, act=act)
    return x2, saved, p, nxt


def _rms_bwd(xv, g, dh):
    r = lax.rsqrt(jnp.mean(xv * xv, axis=-1, keepdims=True) + EPS)
    xn = xv * r
    dxn = dh * g
    dx = r * (dxn - xn * jnp.mean(dxn * xn, axis=-1, keepdims=True))
    dg = jnp.sum((dh * xn).reshape(xv.shape[0] // SUBLANES, SUBLANES, xv.shape[1]), axis=0)
    return dx, dg


def _ffn_bwd_act(dx2, gate, up, w_d):
    L = dx2.shape[0]
    fs = D_FF // N_CHIPS
    TM = TM_PROJ

    def body(dx_ref, gate_ref, up_ref, wd_ref, dgate_ref, dup_ref):
        dact = _dot_nt(dx_ref[...].astype(BF16), wd_ref[...])
        gt = gate_ref[...].astype(F32)
        sg = _sigmoid(gt)
        dgate_ref[...] = (dact * up_ref[...].astype(F32) * (sg * (1.0 + gt * (1.0 - sg)))).astype(BF16)
        dup_ref[...] = (dact * gt * sg).astype(BF16)

    ffs = pl.BlockSpec((None, TM, fs), lambda i, s: (s, i, 0))
    return _call(
        body, name="ffn_bwd_act", grid=(L // TM, N_CHIPS),
        in_specs=[pl.BlockSpec((TM, D_MODEL), lambda i, s: (i, 0)), ffs, ffs,
                  pl.BlockSpec((None, fs, D_MODEL), lambda i, s: (s, 0, 0))],
        out_specs=[ffs, ffs], out_shape=[_sds((N_CHIPS, L, fs), BF16)] * 2,
        sem=("parallel", "parallel"))(dx2, gate, up, w_d)


def _ffn_bwd_in(dx2, x1, g, dgate, dup, w_g, w_u, rider=None):
    L = x1.shape[0]
    fs = D_FF // N_CHIPS
    TM = TM_PROJ

    def body(dx_ref, x_ref, g_ref, dgate_ref, dup_ref, wg_ref, wu_ref, dx1_ref, dg_ref, acc, dgacc):
        i, s = pl.program_id(0), pl.program_id(1)

        @pl.when(s == 0)
        def _():
            acc[...] = jnp.zeros_like(acc)

        @pl.when(jnp.logical_and(i == 0, s == 0))
        def _():
            dgacc[...] = jnp.zeros_like(dgacc)

        acc[...] += _dot(dgate_ref[...], wg_ref[...]) + _dot(dup_ref[...], wu_ref[...])

        @pl.when(s == N_CHIPS - 1)
        def _():
            dx, dg = _rms_bwd(x_ref[...], g_ref[...], acc[...])
            dx1_ref[...] = dx_ref[...] + dx
            dgacc[...] += dg

        @pl.when(jnp.logical_and(i == pl.num_programs(0) - 1, s == N_CHIPS - 1))
        def _():
            dg_ref[...] = jnp.sum(dgacc[...], axis=0, keepdims=True)

    tok = pl.BlockSpec((TM, D_MODEL), lambda i, s: (i, 0))
    ffs = pl.BlockSpec((None, TM, fs), lambda i, s: (s, i, 0))
    vec = pl.BlockSpec((1, D_MODEL), lambda i, s: (0, 0))
    return _call(
        body, name="ffn_bwd_in", grid=(L // TM, N_CHIPS),
        in_specs=[tok, tok, vec, ffs, ffs,
                  pl.BlockSpec((None, fs, D_MODEL), lambda i, s: (s, 0, 0)),
                  pl.BlockSpec((None, fs, D_MODEL), lambda i, s: (s, 0, 0))],
        out_specs=[tok, vec],
        out_shape=[_sds((L, D_MODEL), F32), _sds((1, D_MODEL), F32)],
        scratch=[pltpu.VMEM((TM, D_MODEL), F32), pltpu.VMEM((SUBLANES, D_MODEL), F32)],
        sem=("arbitrary", "arbitrary"), rider=rider)(dx2, x1, g, dgate, dup, w_g, w_u)


def _wgrad(a, b, *, name, grid_kn, a_spec, b_spec, out_shape, out_spec):
    L = a.shape[-2]
    nl = L // TL_WGRAD

    def body(a_ref, b_ref, o_ref):
        @pl.when(pl.program_id(2) == 0)
        def _():
            o_ref[...] = jnp.zeros_like(o_ref)
        o_ref[...] += _dot_tn(a_ref[...].astype(BF16), b_ref[...].astype(BF16))

    return _call(body, name=name, grid=(*grid_kn, nl), in_specs=[a_spec, b_spec], out_specs=out_spec,
                 out_shape=out_shape, sem=("parallel", "parallel", "arbitrary"))(a, b)


def _wgrad_cols(a, b, name):
    K, N = a.shape[1], b.shape[1]
    ns = N // N_CHIPS
    if N * K * 4 <= 4 * 1024 * 1024:
        L = a.shape[0]

        def body(a_ref, b_ref, o_ref):
            @pl.when(pl.program_id(0) == 0)
            def _():
                o_ref[...] = jnp.zeros_like(o_ref)
            av = a_ref[...].astype(BF16)
            for s in range(N_CHIPS):
                o_ref[s] += _dot_tn(av, b_ref[:, s * ns:(s + 1) * ns].astype(BF16))

        return _call(body, name=name, grid=(L // TL_WGRAD,),
                     in_specs=[pl.BlockSpec((TL_WGRAD, K), lambda t: (t, 0)),
                               pl.BlockSpec((TL_WGRAD, N), lambda t: (t, 0))],
                     out_specs=pl.BlockSpec((N_CHIPS, K, ns), lambda t: (0, 0, 0)),
                     out_shape=_sds((N_CHIPS, K, ns), F32), sem=("arbitrary",))(a, b)
    tn = ns // 2 if ns % (2 * LANES) == 0 else ns
    nj = ns // tn
    return _wgrad(a, b, name=name, grid_kn=(1, N_CHIPS * nj),
                  a_spec=pl.BlockSpec((TL_WGRAD, K), lambda i, j, t: (t, 0)),
                  b_spec=pl.BlockSpec((TL_WGRAD, tn), lambda i, j, t: (t, j)),
                  out_shape=_sds((N_CHIPS, K, ns), F32),
                  out_spec=pl.BlockSpec((None, K, tn), lambda i, j, t: (j // nj, 0, j % nj)))


def _wgrad_full(a, b, name):
    K, N = a.shape[1], b.shape[1]
    return _wgrad(a, b, name=name, grid_kn=(1, 1),
                  a_spec=pl.BlockSpec((TL_WGRAD, K), lambda i, j, t: (t, 0)),
                  b_spec=pl.BlockSpec((TL_WGRAD, N), lambda i, j, t: (t, 0)),
                  out_shape=_sds((K, N), F32), out_spec=pl.BlockSpec((K, N), lambda i, j, t: (0, 0)))


def _wgrad_ff_cols(a, b, name):
    K, fs = a.shape[1], b.shape[2]
    return _wgrad(a, b, name=name, grid_kn=(1, N_CHIPS),
                  a_spec=pl.BlockSpec((TL_WGRAD, K), lambda i, j, t: (t, 0)),
                  b_spec=pl.BlockSpec((None, TL_WGRAD, fs), lambda i, j, t: (j, t, 0)),
                  out_shape=_sds((N_CHIPS, K, fs), F32),
                  out_spec=pl.BlockSpec((None, K, fs), lambda i, j, t: (j, 0, 0)))


def _wgrad_ff_rows(a, b, name):
    fs, N = a.shape[2], b.shape[1]
    return _wgrad(a, b, name=name, grid_kn=(N_CHIPS, 1),
                  a_spec=pl.BlockSpec((None, TL_WGRAD, fs), lambda i, j, t: (i, t, 0)),
                  b_spec=pl.BlockSpec((TL_WGRAD, N), lambda i, j, t: (t, 0)),
                  out_shape=_sds((N_CHIPS, fs, N), F32),
                  out_spec=pl.BlockSpec((None, fs, N), lambda i, j, t: (i, 0, 0)))


def _mix_bwd(dx, z, aout, sa, sb, ypre, w_ap, w_ga, w_gb, w_out, rider=None):
    L = dx.shape[0]
    cs = D_MODEL // N_CHIPS
    ga_col = (3 * N_GROUPS * ATTN_WIDTH + SSM_WIDTH) // D_MODEL

    def body(dx_ref, ga_ref, gs_ref, aout_ref, sa_ref, sb_ref, ypre_ref, wap_ref, wga_ref, wgb_ref, wout_ref,
             dgates_ref, da_ref, gy_ref, daout_ref, dsa_ref, dsb_ref):
        dmix = _dot_nt(dx_ref[...].astype(BF16), wout_ref[...])
        sig_a = _sigmoid(ga_ref[...].astype(F32))
        sig_s = _sigmoid(gs_ref[...].astype(F32))
        a_out = aout_ref[...].astype(F32)
        s_a = sa_ref[...].astype(F32)
        sig_b = _sigmoid(sb_ref[...].astype(F32))
        s_out = s_a * sig_b
        daout = (dmix * sig_a).astype(BF16)
        daout_ref[...] = daout
        dgates_ref[:, :D_MODEL] = (dmix * a_out * sig_a * (1.0 - sig_a)).astype(BF16)
        dgates_ref[:, D_MODEL:] = (dmix * s_out * sig_s * (1.0 - sig_s)).astype(BF16)
        ds_out = dmix * sig_s
        dsa = (ds_out * sig_b).astype(BF16)
        dsb = (ds_out * s_a * sig_b * (1.0 - sig_b)).astype(BF16)
        dsa_ref[...] = dsa
        dsb_ref[...] = dsb
        da = jnp.zeros((TM_MIX, ATTN_WIDTH), F32)
        dy = jnp.zeros((TM_MIX, SSM_WIDTH), F32)
        for s in range(N_CHIPS):
            cols = slice(s * cs, (s + 1) * cs)
            da += _dot_nt(daout[:, cols], wap_ref[s])
            dy += _dot_nt(dsa[:, cols], wga_ref[s]) + _dot_nt(dsb[:, cols], wgb_ref[s])
        gy_ref[...] = dy * _gelu_grad(ypre_ref[...])
        da_ref[...] = da

    tok = lambda w: pl.BlockSpec((TM_MIX, w), lambda i: (i, 0))
    wsm = pl.BlockSpec((N_CHIPS, ATTN_WIDTH, cs), lambda i: (0, 0, 0))
    return _call(
        body, name="mix_bwd", grid=(L // TM_MIX,),
        in_specs=[tok(D_MODEL), pl.BlockSpec((TM_MIX, D_MODEL), lambda i: (i, ga_col)),
                  pl.BlockSpec((TM_MIX, D_MODEL), lambda i: (i, ga_col + 1)),
                  tok(D_MODEL), tok(D_MODEL), tok(D_MODEL), tok(SSM_WIDTH),
                  wsm, wsm, wsm, pl.BlockSpec((D_MODEL, D_MODEL), lambda i: (0, 0))],
        out_specs=[tok(2 * D_MODEL), tok(ATTN_WIDTH), tok(SSM_WIDTH)] + [tok(D_MODEL)] * 3,
        out_shape=[_sds((L, 2 * D_MODEL), BF16), _sds((L, ATTN_WIDTH), F32), _sds((L, SSM_WIDTH), F32)]
                  + [_sds((L, D_MODEL), BF16)] * 3,
        sem=("parallel",), rider=rider)(dx, z, z, aout, sa, sb, ypre, w_ap, w_ga, w_gb,
                                        w_out.reshape(D_MODEL, D_MODEL))


def _combine_bwd(da, os_, ls_):
    L = da.shape[0]

    def body(da_ref, o0, o1, o2, l0, l1, l2, hs_ref, p1_ref, p2_ref, pt1_ref, pt2_ref,
             do0, do1, do2, c0, c1, c2):
        o_tok, l_tok = _to_token_order((o0[...], o1[...], o2[...]), (l0[...], l1[...], l2[...]),
                                       (None, pt1_ref[...], pt2_ref[...]))
        w = _combine_weights(*l_tok)
        dav = da_ref[...]
        hs = hs_ref[...]
        tbar = sum(wg * _head_sum(dav * og, hs) for wg, og in zip(w, o_tok))
        for wg, pm, do_ref, c_ref in zip(w, (None, p1_ref[...], p2_ref[...]), (do0, do1, do2), (c0, c1, c2)):
            dog = (wg * dav).astype(BF16)
            cg = -wg * tbar
            do_ref[...] = dog if pm is None else _dot(pm, dog).astype(BF16)
            c_ref[...] = cg if pm is None else _dot(pm, cg.astype(BF16))

    tile = pl.BlockSpec((DL_TILE, ATTN_WIDTH), lambda i: (i, 0))
    mat = pl.BlockSpec((DL_TILE, DL_TILE), lambda i: (0, 0))
    p1, p2 = _perm_matrix(ATTN_PATTERN[1][1]), _perm_matrix(ATTN_PATTERN[2][1])
    outs = _call(body, name="combine_bwd", grid=(L // DL_TILE,), in_specs=[tile] * 7 + [mat] * 5,
                 out_specs=[tile] * 6,
                 out_shape=[_sds((L, ATTN_WIDTH), BF16)] * 3 + [_sds((L, ATTN_WIDTH), F32)] * 3,
                 sem=("parallel",))(da, *os_, *ls_, _head_sum_matrix(), p1, p2, p1.T, p2.T)
    return outs[:3], outs[3:]


def _attn_bwd(qs, ks, v, do, l, c, gi, rider=None):
    L = qs.shape[0]
    _, d = ATTN_PATTERN[gi]
    nb = L // (d * BLK)

    def body(q0_ref, q1_ref, k_ref, v_ref, do0_ref, do1_ref, l0_ref, l1_ref, c0_ref, c1_ref,
             dq_ref, dk_ref, dv_ref, carry):
        n = pl.program_id(1)

        @pl.when(n == 0)
        def _():
            carry[...] = jnp.zeros_like(carry)

        qi = lax.broadcasted_iota(jnp.int32, (2 * BLK, BLK), 0)
        kj = lax.broadcasted_iota(jnp.int32, (2 * BLK, BLK), 1)
        first = qi < BLK
        mask = jnp.logical_and(jnp.where(first, qi, kj) >= jnp.where(first, kj, qi - BLK),
                               qi < jnp.where(n < nb - 1, 2 * BLK, BLK))
        q2 = jnp.concatenate([_dl_read(q0_ref), _dl_read(q1_ref)], axis=0)
        do2 = jnp.concatenate([_dl_read(do0_ref), _dl_read(do1_ref)], axis=0)
        l2 = jnp.concatenate([_dl_read(l0_ref), _dl_read(l1_ref)], axis=0)
        c2 = jnp.concatenate([_dl_read(c0_ref), _dl_read(c1_ref)], axis=0)
        k = _dl_read(k_ref)
        v_ = _dl_read(v_ref)
        h0, h1 = _pair_masks()
        mask2 = jnp.concatenate([mask, mask], axis=1)
        dq_parts, dk_parts, dv_parts = [], [], []
        for hp in range(N_HEADS // 2):
            ls = slice(hp * LANES, (hp + 1) * LANES)
            qp, dop, kp_, vp_ = q2[:, ls], do2[:, ls], k[:, ls], v_[:, ls]
            kk = jnp.concatenate([jnp.where(h0, kp_, 0), jnp.where(h1, kp_, 0)], axis=0)
            vv = jnp.concatenate([jnp.where(h0, vp_, 0), jnp.where(h1, vp_, 0)], axis=0)

            def per_head(t):
                a = jnp.broadcast_to(t[:, hp * LANES:hp * LANES + 1], (2 * BLK, BLK))
                b = jnp.broadcast_to(t[:, hp * LANES + HEAD_DIM:hp * LANES + HEAD_DIM + 1], (2 * BLK, BLK))
                return jnp.concatenate([a, b], axis=1)

            p = jnp.where(mask2, jnp.exp(_dot_nt(qp, kk) - per_head(l2)), 0.0)
            ds = (p * (_dot_nt(dop, vv) + per_head(c2))).astype(BF16)
            dv2 = _dot_tn(p.astype(BF16), dop)
            dk2 = _dot_tn(ds, qp)
            dq2 = _dot(ds, kk)
            dq_parts.append((dq2[:BLK] + carry[:, ls]).astype(BF16))
            carry[:, ls] = dq2[BLK:]
            dk_parts.append(jnp.where(h0, dk2[:BLK], dk2[BLK:]).astype(BF16))
            dv_parts.append(jnp.where(h0, dv2[:BLK], dv2[BLK:]).astype(BF16))
        _dl_write(dq_ref, jnp.concatenate(dq_parts, axis=1))
        _dl_write(dk_ref, jnp.concatenate(dk_parts, axis=1))
        _dl_write(dv_ref, jnp.concatenate(dv_parts, axis=1))

    cur = _dl_spec(d, ATTN_WIDTH, lambda n: n)
    nxt = _dl_spec(d, ATTN_WIDTH, lambda n: jnp.minimum(n + 1, nb - 1))
    view = lambda t: _dl_view(t, d)
    outs = _call(
        body, name=f"attn_bwd_g{gi}", grid=(d, nb),
        in_specs=[cur, nxt, cur, cur, cur, nxt, cur, nxt, cur, nxt], out_specs=[cur, cur, cur],
        out_shape=[_sds(view(qs).shape, BF16)] * 3, scratch=[pltpu.VMEM((BLK, ATTN_WIDTH), F32)],
        sem=("parallel", "arbitrary"), rider=rider)(view(qs), view(qs), view(ks), view(v), view(do), view(do),
                                                    view(l), view(l), view(c), view(c))
    return [t.reshape(L, ATTN_WIDTH) for t in outs[:3]], list(outs[3:])


def _qkv_post(z, dqkv, du, dgates, gq_t, gk_t):
    L = z.shape[0]
    qkv_w = N_GROUPS * ATTN_WIDTH

    def body(zq_ref, zk_ref, gq_ref, gk_ref, hs_ref, pt1_ref, pt2_ref, du_ref, dgates_ref, *rest):
        dl_refs, (dz_ref, dgq_ref, dgk_ref) = rest[:9], rest[9:]

        @pl.when(pl.program_id(0) == 0)
        def _():
            dgq_ref[...] = jnp.zeros_like(dgq_ref)
            dgk_ref[...] = jnp.zeros_like(dgk_ref)

        hs = hs_ref[...]
        pts = (None, pt1_ref[...], pt2_ref[...])

        def rows8(t):
            return jnp.sum(t.reshape(DL_TILE // SUBLANES, SUBLANES, ATTN_WIDTH), axis=0)

        def norm_bwd(x, gain, dn):
            r = lax.rsqrt(_head_sum(x * x, hs) * (1.0 / HEAD_DIM) + EPS)
            xh = x * r
            dh = dn * gain
            return r * (dh - xh * (_head_sum(dh * xh, hs) * (1.0 / HEAD_DIM))), rows8(dn * xh)

        for g in range(N_GROUPS):
            tok = [t[...].astype(F32) if pts[g] is None else _dot(pts[g], t[...]) for t in dl_refs[3 * g:3 * g + 3]]
            cols = slice(g * ATTN_WIDTH, (g + 1) * ATTN_WIDTH)
            dq, pq = norm_bwd(zq_ref[:, cols].astype(F32), gq_ref[...] * SCALE, tok[0])
            dk, pk_ = norm_bwd(zk_ref[:, cols].astype(F32), gk_ref[...], tok[1])
            dgq_ref[...] += pq * SCALE
            dgk_ref[...] += pk_
            dz_ref[:, cols] = dq.astype(BF16)
            dz_ref[:, qkv_w + g * ATTN_WIDTH:qkv_w + (g + 1) * ATTN_WIDTH] = dk.astype(BF16)
            dz_ref[:, 2 * qkv_w + g * ATTN_WIDTH:2 * qkv_w + (g + 1) * ATTN_WIDTH] = tok[2].astype(BF16)
        dz_ref[:, 3 * qkv_w:3 * qkv_w + SSM_WIDTH] = du_ref[...]
        dz_ref[:, 3 * qkv_w + SSM_WIDTH:] = dgates_ref[...]

    tile = lambda w: pl.BlockSpec((DL_TILE, w), lambda i: (i, 0))
    mat = pl.BlockSpec((DL_TILE, DL_TILE), lambda i: (0, 0))
    vec = pl.BlockSpec((1, ATTN_WIDTH), lambda i: (0, 0))
    acc = pl.BlockSpec((SUBLANES, ATTN_WIDTH), lambda i: (0, 0))
    flat = [t for grp in dqkv for t in grp]
    return _call(
        body, name="qkv_post", grid=(L // DL_TILE,),
        in_specs=[tile(qkv_w), pl.BlockSpec((DL_TILE, qkv_w), lambda i: (i, 1)), vec, vec, mat, mat, mat,
                  tile(SSM_WIDTH), tile(2 * D_MODEL)] + [tile(ATTN_WIDTH)] * 9,
        out_specs=[tile(IN_COLS), acc, acc],
        out_shape=[_sds((L, IN_COLS), BF16), _sds((SUBLANES, ATTN_WIDTH), F32), _sds((SUBLANES, ATTN_WIDTH), F32)],
        sem=("arbitrary",))(z, z, gq_t, gk_t, _head_sum_matrix(), _perm_matrix(ATTN_PATTERN[1][1]).T,
                            _perm_matrix(ATTN_PATTERN[2][1]).T, du, dgates, *flat)


def _scan_rev_grad(sre, sim, rre, rim, are_ref, aim_ref, k0, init, seed_re, seed_im):
    ar = [_bcast(are_ref, k0 + kk) for kk in range(SCAN_GROUP)]
    ai = [-_bcast(aim_ref, k0 + kk) for kk in range(SCAN_GROUP)]

    def update(i, xprev, carry):
        out = []
        for kk in range(SCAN_GROUP):
            k = k0 + kk
            lr, li, dr, di = carry[4 * kk:4 * kk + 4]
            nr = ar[kk] * lr - ai[kk] * li + rre[k, _rows(i), :]
            ni = ar[kk] * li + ai[kk] * lr + rim[k, _rows(i), :]
            rre[k, _rows(i), :] = nr
            rim[k, _rows(i), :] = ni
            xr, xi = xprev(k)
            out += [nr, ni, dr + xr * nr + xi * ni, di + xr * ni - xi * nr]
        return tuple(out)

    def step(t, carry):
        i = SSM_TC - 1 - t
        return update(i, lambda k: (sre[k, _rows(i - 1), :], sim[k, _rows(i - 1), :]), carry)

    zero = jnp.zeros((SSM_SUB, LANES), F32)
    flat = []
    for re, im in init:
        flat += [re, im, zero, zero]
    res = lax.fori_loop(0, (SSM_TC - 1) // 2, lambda t, c: step(2 * t + 1, step(2 * t, c)), tuple(flat))
    res = step(SSM_TC - 2, res)
    res = update(0, lambda k: (seed_re[k], seed_im[k]), res)
    return [(res[4 * kk + 2], res[4 * kk + 3]) for kk in range(SCAN_GROUP)]


def _ssm_bwd(z, gy, pk, dskip, sd_re, sd_im, rider=None):
    L = z.shape[0]
    nb = L // SSM_TB
    ucol = (3 * N_GROUPS * ATTN_WIDTH) // SSM_WIDTH
    nwin = N_PAIR // PAIRS_PER_WIN

    def body(u_ref, gy_ref, are_ref, aim_ref, a64re_ref, a64im_ref, bwre_ref, bwim_ref, cwre_ref, cwim_ref, d_ref,
             sdre_ref, sdim_ref,
             du_ref, dare_ref, daim_ref, dbre_ref, dbim_ref, dcre_ref, dcim_ref, dd_ref,
             sre, sim, rre, rim, carry_re, carry_im, ends_re, ends_im, seed_re, seed_im):
        @pl.when(pl.program_id(0) == 0)
        def _():
            carry_re[...] = jnp.zeros_like(carry_re)
            carry_im[...] = jnp.zeros_like(carry_im)
            for ref in (dare_ref, daim_ref, dbre_ref, dbim_ref, dcre_ref, dcim_ref, dd_ref):
                ref[...] = jnp.zeros_like(ref)

        u = u_ref[...]
        gyv = gy_ref[...]
        gyb = gyv.astype(BF16)
        _ssm_fill(u, bwre_ref, bwim_ref, sre, sim)
        for k2 in range(N_PAIR):
            gw = gyb[:, _win_cols(k2)]
            _to_slabs(rre, k2, _dot_nt(gw, cwre_ref[k2]))
            _to_slabs(rim, k2, -_dot_nt(gw, cwim_ref[k2]))
        zero = jnp.zeros((SSM_SUB, LANES), F32)
        for k0 in range(0, N_SLAB, SCAN_GROUP):
            grp = range(k0, k0 + SCAN_GROUP)
            _scan(sre, sim, are_ref, aim_ref, k0, [(sdre_ref[k], sdim_ref[k]) for k in grp],
                  reverse=False, store=True)
            ends = _scan(rre, rim, are_ref, aim_ref, k0, [(zero, zero)] * SCAN_GROUP, reverse=True, store=False,
                         sign=-1.0)
            for kk, k in enumerate(grp):
                ends_re[k] = ends[kk][0]
                ends_im[k] = ends[kk][1]
            for k in grp:
                _ssm_seeds(ends_re, ends_im, a64re_ref, a64im_ref, carry_re, carry_im, seed_re, seed_im, k,
                           reverse=True, sign=-1.0)
            das = _scan_rev_grad(sre, sim, rre, rim, are_ref, aim_ref, k0,
                                 [(seed_re[k], seed_im[k]) for k in grp], sdre_ref, sdim_ref)
            for kk, k in enumerate(grp):
                dare_ref[k] += das[kk][0]
                daim_ref[k] += das[kk][1]
        for w in range(nwin):
            cols = _win_cols(w * PAIRS_PER_WIN)
            uw = u[:, cols]
            gw = gyb[:, cols]
            acc = gyv[:, cols] * d_ref[:, cols]
            for kk in range(PAIRS_PER_WIN):
                k2 = w * PAIRS_PER_WIN + kk
                lr = _pair_rows(rre, k2)
                li = _pair_rows(rim, k2)
                acc += _dot_nt(lr, bwre_ref[k2]) + _dot_nt(li, bwim_ref[k2])
                dbre_ref[k2] += _dot_tn(uw, lr)
                dbim_ref[k2] += _dot_tn(uw, li)
                dcre_ref[k2] += _dot_tn(_pair_rows(sre, k2), gw)
                dcim_ref[k2] -= _dot_tn(_pair_rows(sim, k2), gw)
            du_ref[:, cols] = acc.astype(BF16)
        dd_ref[...] += jnp.sum((gyv * u.astype(F32)).reshape(SSM_TB // SUBLANES, SUBLANES, SSM_WIDTH), axis=0)

    c2, c3 = _ssm_specs_consts()
    rev = lambda b: nb - 1 - b
    seed_spec = pl.BlockSpec((None, N_SLAB, SSM_SUB, LANES), lambda b: (rev(b), 0, 0, 0))
    tile_out = pl.BlockSpec((N_SLAB, SSM_SUB, LANES), lambda b: (0, 0, 0))
    small = pltpu.VMEM((N_SLAB, LANES), F32)
    tile = pltpu.VMEM((N_SLAB, SSM_SUB, LANES), F32)
    return _call(
        body, name="ssm_bwd", grid=(nb,),
        in_specs=[pl.BlockSpec((SSM_TB, SSM_WIDTH), lambda b: (rev(b), ucol)),
                  pl.BlockSpec((SSM_TB, SSM_WIDTH), lambda b: (rev(b), 0)),
                  c2, c2, c2, c2, c3, c3, c3, c3, pl.BlockSpec((1, SSM_WIDTH), lambda b: (0, 0)),
                  seed_spec, seed_spec],
        out_specs=[pl.BlockSpec((SSM_TB, SSM_WIDTH), lambda b: (rev(b), 0)), tile_out, tile_out, c3, c3, c3, c3,
                   pl.BlockSpec((SUBLANES, SSM_WIDTH), lambda b: (0, 0))],
        out_shape=[_sds((L, SSM_WIDTH), BF16), _sds((N_SLAB, SSM_SUB, LANES), F32),
                   _sds((N_SLAB, SSM_SUB, LANES), F32)] + [_sds((N_PAIR, SSM_WIN, SSM_WIN), F32)] * 4
                  + [_sds((SUBLANES, SSM_WIDTH), F32)],
        scratch=_ssm_scratch() + _ssm_scratch() + [small, small, tile, tile, tile, tile],
        sem=("arbitrary",), rider=rider)(z, gy, pk["a_re"], pk["a_im"], pk["a64_re"], pk["a64_im"],
                            pk["bw_re"].astype(BF16), pk["bw_im"].astype(BF16),
                            pk["cw_re"].astype(BF16), pk["cw_im"].astype(BF16), dskip, sd_re, sd_im)


def _in_proj_bwd(dz, w, x, g, dres, rider=None):
    L = x.shape[0]
    ns = w.shape[2]
    tn = ns
    nj = ns // tn
    nt = N_CHIPS * nj
    TM = TM_PROJ

    def body(dz_ref, w_ref, x_ref, g_ref, dres_ref, dx_ref, dg_ref, acc, dgacc):
        i, j = pl.program_id(0), pl.program_id(1)

        @pl.when(j == 0)
        def _():
            acc[...] = jnp.zeros_like(acc)

        @pl.when(jnp.logical_and(i == 0, j == 0))
        def _():
            dgacc[...] = jnp.zeros_like(dgacc)

        acc[...] += _dot_nt(dz_ref[...], w_ref[...])

        @pl.when(j == nt - 1)
        def _():
            dx, dg = _rms_bwd(x_ref[...], g_ref[...], acc[...])
            dx_ref[...] = dres_ref[...] + dx
            dgacc[...] += dg

        @pl.when(jnp.logical_and(i == pl.num_programs(0) - 1, j == nt - 1))
        def _():
            dg_ref[...] = jnp.sum(dgacc[...], axis=0, keepdims=True)

    tok = pl.BlockSpec((TM, D_MODEL), lambda i, j: (i, 0))
    vec = pl.BlockSpec((1, D_MODEL), lambda i, j: (0, 0))
    return _call(
        body, name="in_proj_bwd", grid=(L // TM, nt),
        in_specs=[pl.BlockSpec((TM, tn), lambda i, j: (i, j)),
                  pl.BlockSpec((None, D_MODEL, tn), lambda i, j: (j // nj, 0, j % nj)), tok, vec, tok],
        out_specs=[tok, vec],
        out_shape=[_sds((L, D_MODEL), F32), _sds((1, D_MODEL), F32)],
        scratch=[pltpu.VMEM((TM, D_MODEL), F32), pltpu.VMEM((SUBLANES, D_MODEL), F32)],
        sem=("arbitrary", "arbitrary"), rider=rider)(dz, w, x, g, dres)


SSM_PARAMS = ("lambda_re", "lambda_im", "log_dt", "b_re", "b_im", "c_re", "c_im")
EARLY = ("w_ffn_gate", "w_ffn_up", "w_ffn_down")
LATE = ("w_in", "w_attn_proj", "w_glu_a", "w_glu_b", "w_out")


def _layer_bwd(dx2, sv, p, pending, owned, l, idx):
    chip_idx, c_idx = idx
    g = {}
    owned = dict(owned)

    def settle(name, partial, arrived, layer):
        owned[name] = _sum_owner(partial, arrived, owned[name], layer, chip_idx, c_idx)

    dgate, dup = _ffn_bwd_act(dx2, sv["gate"], sv["up"], p["w_ffn_down"])
    outs = _ffn_bwd_in(dx2, sv["x1"], p["g_ffn"], dgate, dup, p["w_ffn_gate"], p["w_ffn_up"],
                       _scatter_rider([pending[n] for n in LATE[1:]]) if pending else None)
    dx1, g["g_ffn"] = outs[:2]
    for n, t in zip(LATE[1:], outs[2:]):
        settle(n, pending[n], t, l + 1)
    g["w_ffn_gate"] = _wgrad_ff_rows(dgate, sv["h2"], "wgrad_ffn_gate")
    g["w_ffn_up"] = _wgrad_ff_rows(dup, sv["h2"], "wgrad_ffn_up")
    g["w_ffn_down"] = _wgrad_ff_rows(sv["act"], dx2, "wgrad_ffn_down")

    outs = _mix_bwd(dx1, sv["z"], sv["aout"], sv["sa"], sv["sb"], sv["ypre"], p["w_attn_proj"], p["w_glu_a"],
                    p["w_glu_b"], p["w_out"], _swap_rider([g[n] for n in EARLY]))
    dgates, da, gy, daout, dsa, dsb = outs[:6]
    early = [_add_half(g[n], s, c_idx) for n, s in zip(EARLY, outs[6:])]
    g["w_out"] = _wgrad_full(sv["mix"], dx1, "wgrad_out").reshape(N_CHIPS, D_MODEL // N_CHIPS, D_MODEL)
    g["w_attn_proj"] = _wgrad_cols(sv["a"], daout, "wgrad_attn_proj")
    g["w_glu_a"] = _wgrad_cols(sv["yact"], dsa, "wgrad_glu_a")
    g["w_glu_b"] = _wgrad_cols(sv["yact"], dsb, "wgrad_glu_b")

    outs = _ssm_bwd(sv["z"], gy, sv["pk"], p["d_skip"], sv["sd_re"], sv["sd_im"],
                    _scatter_rider([pending[LATE[0]]]) if pending else None)
    du, da_re, da_im, dbw_re, dbw_im, dcw_re, dcw_im, dd = outs[:8]
    if pending:
        settle(LATE[0], pending[LATE[0]], outs[8], l + 1)
    g["d_skip"] = jnp.sum(dd, axis=0, keepdims=True)
    g["ssm_pack_ct"] = dict(a_re=da_re, a_im=da_im, bw_re=dbw_re, bw_im=dbw_im, cw_re=dcw_re, cw_im=dcw_im)

    dos, cs = _combine_bwd(da, sv["os"], sv["ls"])
    dqkv = []
    for gi in range(N_GROUPS):
        grads, arrived = _attn_bwd(*sv["qkv"][gi], dos[gi], sv["ls"][gi], cs[gi], gi, _scatter_rider([early[gi]]))
        settle(EARLY[gi], early[gi], arrived[0], l)
        dqkv.append(grads)
    dz, gq8, gk8 = _qkv_post(sv["z"], dqkv, du, dgates, jnp.tile(p["g_q"], (1, N_HEADS)),
                             jnp.tile(p["g_k"], (1, N_HEADS)))
    g["g_q"] = jnp.sum(gq8.reshape(SUBLANES * N_HEADS, HEAD_DIM), axis=0, keepdims=True)
    g["g_k"] = jnp.sum(gk8.reshape(SUBLANES * N_HEADS, HEAD_DIM), axis=0, keepdims=True)
    g["w_in"] = _wgrad_cols(sv["h"], dz, "wgrad_in")
    outs = _in_proj_bwd(dz, p["w_in"], sv["x"], p["g_mix"], dx1, _swap_rider([g[n] for n in LATE]))
    dx, g["g_mix"] = outs[:2]
    late = {n: _add_half(g[n], s, c_idx) for n, s in zip(LATE, outs[2:])}
    return dx, g, late, owned


def _place():
    x, y, c = lax.axis_index("x"), lax.axis_index("y"), lax.axis_index("c")
    others = [(1 - x, y), (x, 1 - y), (1 - x, 1 - y)]
    return x, y, c, others


def _half(ref, hc):
    rows = ref.shape[-2] // 2
    idx = (slice(None),) * (len(ref.shape) - 2) + (pl.ds(hc * rows, rows), slice(None))
    return ref.at[idx]


def _comm_call(body, name, ins, out_shapes, n_remote, aliases=None):
    scratch = [pltpu.SemaphoreType.DMA((n_remote,)), pltpu.SemaphoreType.DMA((n_remote,))]
    return pl.pallas_call(
        body, name=name, in_specs=[ANY] * len(ins), out_specs=[ANY] * len(out_shapes), out_shape=out_shapes,
        scratch_shapes=scratch, input_output_aliases=aliases or {})(*ins)


def _cast_place(w, l, chip_idx):
    _, R, C = w.shape
    tr = R // 2

    def body(me_ref, w_ref, o_ref):
        o_ref[...] = w_ref[...].astype(BF16)

    return _call(body, name=f"cast_place_l{l}", grid=(R // tr,), prefetch=1,
                 in_specs=[pl.BlockSpec((None, tr, C), lambda i, me_ref: (l, i, 0))],
                 out_specs=pl.BlockSpec((None, tr, C), lambda i, me_ref: (me_ref[0], i, 0)),
                 out_shape=_sds((N_CHIPS, R, C), BF16), sem=("arbitrary",))(chip_idx, w)


def _in_place_rider(bufs, pairs, per_buf=3):
    n = len(bufs)

    def copies(outs, send, recv, side):
        return [pltpu.make_async_remote_copy(src_ref=pair[side][0], dst_ref=pair[side][0], send_sem=send.at[k],
                                             recv_sem=recv.at[k], device_id=pair[side][1], device_id_type=MESH)
                for k, pair in enumerate(pairs(outs))]

    def start(ins, outs, send, recv):
        for cp in copies(outs, send, recv, 0):
            cp.start()

    def wait(ins, outs, send, recv):
        for cp in copies(outs, send, recv, 1):
            cp.wait_recv()
        for cp in copies(outs, send, recv, 0):
            cp.wait_send()

    return Rider(list(bufs), [_sds(b.shape, b.dtype) for b in bufs], per_buf * n, start, wait,
                 {a: a for a in range(n)})


def _gather_ici_rider(bufs):
    def pairs(outs):
        x, y, c, others = _place()
        return [((_half(o.at[2 * x + y], c), (cx, cy, c)), (_half(o.at[2 * cx + cy], c), (cx, cy, c)))
                for o in outs for cx, cy in others]
    return _in_place_rider(bufs, pairs)


def _gather_d2d_rider(bufs):
    def pairs(outs):
        x, y, c, others = _place()
        sib = (x, y, 1 - c)
        return [((_half(o.at[2 * cx + cy], c), sib), (_half(o.at[2 * cx + cy], 1 - c), sib))
                for o in outs for cx, cy in others]
    return _in_place_rider(bufs, pairs)


def _swap_rider(gs):
    n = len(gs)

    def copies(ins, outs, send, recv):
        x, y, c, _ = _place()
        return [pltpu.make_async_remote_copy(src_ref=_half(ins[a], 1 - c), dst_ref=outs[a], send_sem=send.at[a],
                                             recv_sem=recv.at[a], device_id=(x, y, 1 - c), device_id_type=MESH)
                for a in range(n)]

    def start(ins, outs, send, recv):
        for cp in copies(ins, outs, send, recv):
            cp.start()

    def wait(ins, outs, send, recv):
        for cp in copies(ins, outs, send, recv):
            cp.wait()

    outs = [_sds((g.shape[0], g.shape[1] // 2, g.shape[2]), g.dtype) for g in gs]
    return Rider(list(gs), outs, n, start, wait, {})


def _scatter_rider(ss):
    n = len(ss)

    def copies(ins, outs, send, recv):
        x, y, c, others = _place()
        return [pltpu.make_async_remote_copy(
            src_ref=ins[a].at[2 * cx + cy], dst_ref=outs[a].at[j], send_sem=send.at[3 * a + j],
            recv_sem=recv.at[3 * a + j], device_id=(cx, cy, c), device_id_type=MESH)
            for a in range(n) for j, (cx, cy) in enumerate(others)]

    def start(ins, outs, send, recv):
        for cp in copies(ins, outs, send, recv):
            cp.start()

    def wait(ins, outs, send, recv):
        for cp in copies(ins, outs, send, recv):
            cp.wait()

    outs = [_sds((N_CHIPS - 1,) + s.shape[1:], s.dtype) for s in ss]
    return Rider(list(ss), outs, 3 * n, start, wait, {})


def _run_rider(rider, name):
    n_in = len(rider.ins)

    def body(*refs):
        ins, outs = refs[:n_in], refs[n_in:n_in + len(rider.out_shapes)]
        send, recv = refs[n_in + len(rider.out_shapes):]
        rider.start(ins, outs, send, recv)
        rider.wait(ins, outs, send, recv)

    return _comm_call(body, name, rider.ins, rider.out_shapes, rider.n_sem, aliases=rider.aliases)


def _join_rider(bufs):
    def pairs(outs):
        x, y, c, _ = _place()
        sib = (x, y, 1 - c)
        return [((_half(o, c), sib), (_half(o, 1 - c), sib)) for o in outs]
    return _in_place_rider(bufs, pairs, per_buf=1)


def _place_small(v, dev_idx):
    rows, n = v.shape

    def body(idx_ref, v_ref, o_ref):
        o_ref[...] = v_ref[...]

    return _call(body, name="place_small", grid=(1,), prefetch=1,
                 in_specs=[pl.BlockSpec((rows, n), lambda i, idx_ref: (0, 0))],
                 out_specs=pl.BlockSpec((None, rows, n), lambda i, idx_ref: (idx_ref[0], 0, 0)),
                 out_shape=_sds((8, rows, n), v.dtype), sem=("arbitrary",))(dev_idx, v)


def _small_ici_rider(buf):
    def pairs(outs):
        x, y, c, others = _place()
        peers = [(x, y, 1 - c)] + [(cx, cy, c) for cx, cy in others]
        return [((outs[0].at[4 * x + 2 * y + c], peer), (outs[0].at[4 * peer[0] + 2 * peer[1] + peer[2]], peer))
                for peer in peers]
    return _in_place_rider([buf], pairs, per_buf=4)


def _small_d2d_rider(buf):
    def pairs(outs):
        x, y, c, others = _place()
        sib = (x, y, 1 - c)
        return [((outs[0].at[4 * cx + 2 * cy + c], sib), (outs[0].at[4 * cx + 2 * cy + 1 - c], sib))
                for cx, cy in others]
    return _in_place_rider([buf], pairs)


def _add_half(g, p, c):
    _, R, C = g.shape
    half = R // 2

    def body(c_ref, g_ref, p_ref, o_ref):
        o_ref[...] = g_ref[...] + p_ref[...]

    blk = (None, half, C)
    return _call(body, name="add_half", grid=(N_CHIPS,), prefetch=1,
                 in_specs=[pl.BlockSpec(blk, lambda s, c_ref: (s, c_ref[0], 0)),
                           pl.BlockSpec(blk, lambda s, c_ref: (s, 0, 0))],
                 out_specs=pl.BlockSpec(blk, lambda s, c_ref: (s, 0, 0)),
                 out_shape=_sds((N_CHIPS, half, C), F32), sem=("arbitrary",))(c, g, p)


def _sum_owner(s, q, buf, l, me, c):
    _, half, C = s.shape
    tr = half // 2

    def body(me_ref, c_ref, s_ref, q0, q1, q2, buf_ref, o_ref):
        o_ref[...] = ((s_ref[...] + q0[...]) + q1[...]) + q2[...]

    blk = (None, tr, C)
    qspec = lambda j: pl.BlockSpec(blk, lambda i, me_ref, c_ref: (j, i, 0))
    return _call(body, name=f"sum_owner_l{l}", grid=(half // tr,), prefetch=2,
                 in_specs=[pl.BlockSpec(blk, lambda i, me_ref, c_ref: (me_ref[0], i, 0)),
                           qspec(0), qspec(1), qspec(2), ANY],
                 out_specs=pl.BlockSpec(blk, lambda i, me_ref, c_ref: (l, 2 * c_ref[0] + i, 0)),
                 out_shape=_sds(buf.shape, F32), sem=("arbitrary",), aliases={6: 0})(me, c, s, q, q, q, buf)


def _adamw_math(w, g, m, v):
    m = ADAM_B1 * m + (1.0 - ADAM_B1) * g
    v = ADAM_B2 * v + (1.0 - ADAM_B2) * (g * g)
    m_hat = m / (1.0 - ADAM_B1 ** ADAM_STEP)
    v_hat = v / (1.0 - ADAM_B2 ** ADAM_STEP)
    delta = -ADAM_LR * (m_hat / (jnp.sqrt(v_hat) + ADAM_EPS) + ADAM_WD * w)
    return delta, m, v


def _adamw(w, g, m, v, rider=None):
    depth, R, C = w.shape
    tr = max(t for t in range(SUBLANES, R + 1, SUBLANES) if R % t == 0 and t * C * 4 <= 2 * 1024 * 1024)

    def body(w_ref, g_ref, m_ref, v_ref, d_ref, nm_ref, nv_ref):
        d, nm, nv = _adamw_math(w_ref[...], g_ref[...], m_ref[...], v_ref[...])
        d_ref[...] = d
        nm_ref[...] = nm
        nv_ref[...] = nv

    spec = pl.BlockSpec((None, tr, C), lambda l, i: (l, i, 0))
    return _call(body, name="adamw", grid=(depth, R // tr), in_specs=[spec] * 4, out_specs=[spec] * 3,
                 out_shape=[_sds(w.shape, F32)] * 3, sem=("parallel", "parallel"), rider=rider)(w, g, m, v)


def _small_update(gathered, w, m, v):
    _, rows, n = gathered.shape
    tr = rows // 7

    def body(ga_ref, w_ref, m_ref, v_ref, g_ref, d_ref, nm_ref, nv_ref):
        g = ga_ref[0]
        for k in range(1, 8):
            g = g + ga_ref[k]
        d, nm, nv = _adamw_math(w_ref[...], g, m_ref[...], v_ref[...])
        g_ref[...] = g
        d_ref[...] = d
        nm_ref[...] = nm
        nv_ref[...] = nv

    spec = pl.BlockSpec((tr, n), lambda i: (i, 0))
    return _call(body, name="small_update", grid=(rows // tr,),
                 in_specs=[pl.BlockSpec((8, tr, n), lambda i: (0, i, 0)), spec, spec, spec], out_specs=[spec] * 4,
                 out_shape=[_sds((rows, n), F32)] * 4, sem=("parallel",))(gathered, w, m, v)


WEIGHTS = ("g_mix", "w_in", "g_q", "g_k", "w_attn_proj", "lambda_re", "lambda_im", "log_dt", "b_re", "b_im",
           "c_re", "c_im", "d_skip", "w_glu_a", "w_glu_b", "w_out", "g_ffn", "w_ffn_gate", "w_ffn_up", "w_ffn_down")
BIG = ("w_in", "w_attn_proj", "w_glu_a", "w_glu_b", "w_out", "w_ffn_gate", "w_ffn_up", "w_ffn_down")
FLIPPED = ("w_ffn_gate", "w_ffn_up")
SMALL = tuple(n for n in WEIGHTS if n not in BIG)
ROW_VECTORS = ("g_mix", "g_q", "g_k", "d_skip", "g_ffn")
PACK_QUANTUM = LANES * SUBLANES * 7


def _pack_small(parts, extra):
    flat = jnp.concatenate([parts[n].reshape(-1).astype(F32) for n in SMALL] + [extra.reshape(-1)])
    pad = -flat.shape[0] % PACK_QUANTUM
    return jnp.pad(flat, (0, pad)).reshape(-1, LANES)


def _unpack_small(packed, like):
    flat = packed.reshape(-1)
    out, at = {}, 0
    for n in SMALL:
        size = math.prod(like[n].shape)
        out[n] = flat[at:at + size].reshape(like[n].shape)
        at += size
    return out, flat[at]


def kernel(x, g_mix, w_in, g_q, g_k, w_attn_proj, lambda_re, lambda_im, log_dt, b_re, b_im, c_re, c_im, d_skip, w_glu_a, w_glu_b, w_out, g_ffn, w_ffn_gate, w_ffn_up, w_ffn_down, loss_target, m_g_mix, m_w_in, m_g_q, m_g_k, m_w_attn_proj, m_lambda_re, m_lambda_im, m_log_dt, m_b_re, m_b_im, m_c_re, m_c_im, m_d_skip, m_w_glu_a, m_w_glu_b, m_w_out, m_g_ffn, m_w_ffn_gate, m_w_ffn_up, m_w_ffn_down, v_g_mix, v_w_in, v_g_q, v_g_k, v_w_attn_proj, v_lambda_re, v_lambda_im, v_log_dt, v_b_re, v_b_im, v_c_re, v_c_im, v_d_skip, v_w_glu_a, v_w_glu_b, v_w_out, v_g_ffn, v_w_ffn_gate, v_w_ffn_up, v_w_ffn_down):
    given = dict(locals())
    flip = lambda n, a: jnp.swapaxes(a, 1, 2) if n in FLIPPED else a
    W = {n: flip(n, given[n]) for n in WEIGHTS}
    M = {n: flip(n, given["m_" + n]) for n in WEIGHTS}
    V = {n: flip(n, given["v_" + n]) for n in WEIGHTS}
    depth = g_mix.shape[0]
    xl = x.reshape(x.shape[-2:])
    target = loss_target.reshape(loss_target.shape[-2:])
    c_idx = lax.axis_index("c").astype(jnp.int32).reshape(1)
    chip_idx = (2 * lax.axis_index("x") + lax.axis_index("y")).astype(jnp.int32).reshape(1)

    place = lambda l: [_cast_place(W[n], l, chip_idx) for n in BIG]
    bufs = place(0)
    w_in = _run_rider(_gather_d2d_rider(_run_rider(_gather_ici_rider(bufs[:1]), "gather_ici")), "gather_d2d")[0]
    rest, stage = bufs[1:], "ici"
    params, saved, h = [], [], xl
    for l in range(depth):
        p = {"w_in": w_in}
        for n in SMALL:
            p[n] = W[n][l][None] if n in ROW_VECTORS else W[n][l]
        h, sv, p, nxt = _layer_fwd(h, p, rest, stage, place(l + 1) if l + 1 < depth else None)
        params.append(p)
        saved.append(sv)
        if nxt:
            (w_in, rest), stage = nxt, "d2d"
    dx, loss_part = _loss_head(h, target)

    owned = {n: lax.empty(W[n].shape, F32) for n in BIG}
    small_grads = [None] * depth
    pending = None
    for l in reversed(range(depth)):
        dx, small_grads[l], pending, owned = _layer_bwd(dx, saved[l], params[l], pending, owned, l,
                                                        (chip_idx, c_idx))

    ct = {k: jnp.stack([small_grads[l]["ssm_pack_ct"][k] for l in range(depth)])
          for k in small_grads[0]["ssm_pack_ct"]}
    ct["a_re"], ct["a_im"] = jnp.sum(ct["a_re"], axis=2), jnp.sum(ct["a_im"], axis=2)
    ct["a64_re"] = ct["a64_im"] = jnp.zeros_like(ct["a_re"])
    _, pull = jax.vjp(jax.vmap(_ssm_pack), *[W[n] for n in SSM_PARAMS])
    stacked = dict(zip(SSM_PARAMS, pull(ct)))
    for n in SMALL:
        if n not in stacked:
            stacked[n] = jnp.stack([small_grads[l][n] for l in range(depth)])
    zero = jnp.zeros((1,), F32)
    dev_idx = (4 * lax.axis_index("x") + 2 * lax.axis_index("y") + lax.axis_index("c")).astype(jnp.int32).reshape(1)
    gathered = _place_small(_pack_small(stacked, loss_part), dev_idx)
    outs = _run_rider(_join_riders(_scatter_rider([pending[n] for n in LATE]), _small_ici_rider(gathered)),
                      "scatter_to_owners")
    for n, q in zip(LATE, outs[:len(LATE)]):
        owned[n] = _sum_owner(pending[n], q, owned[n], 0, chip_idx, c_idx)
    outs = _run_rider(_join_riders(_join_rider([owned[n] for n in BIG]), _small_d2d_rider(outs[len(LATE)])),
                      "join_halves")
    reduced, gathered = dict(zip(BIG, outs[:len(BIG)])), outs[len(BIG)]
    grads, delta, new_m, new_v = {}, {}, {}, {}
    for n in BIG:
        outs = (reduced[n], *_adamw(W[n], reduced[n], M[n], V[n]))
        grads[n], delta[n], new_m[n], new_v[n] = [flip(n, t) for t in outs]
    gs, ds, nms, nvs = _small_update(gathered, _pack_small(W, zero), _pack_small(M, zero), _pack_small(V, zero))
    sg, loss = _unpack_small(gs, W)
    sd, _ = _unpack_small(ds, W)
    sm, _ = _unpack_small(nms, W)
    sv_, _ = _unpack_small(nvs, W)
    for n in SMALL:
        grads[n], delta[n], new_m[n], new_v[n] = sg[n], sd[n], sm[n], sv_[n]

    return (loss, dx.reshape(x.shape), *[grads[n] for n in WEIGHTS], *[delta[n] for n in WEIGHTS],
            *[new_m[n] for n in WEIGHTS], *[new_v[n] for n in WEIGHTS])
```

```python
import collections
import functools
import math

import jax
import jax.numpy as jnp
from jax import lax
from jax.experimental import pallas as pl
from jax.experimental.pallas import tpu as pltpu

F32 = jnp.float32
BF16 = jnp.bfloat16

D_MODEL = 1024
DEPTH = 4
HEAD_DIM = 64
N_HEADS = 8
ATTN_WIDTH = N_HEADS * HEAD_DIM
ATTN_PATTERN = ((128, 1), (512, 4), (2048, 16))
N_GROUPS = len(ATTN_PATTERN)
BLK = 128
SSM_WIDTH = 512
SSM_GROUP = 16
SSM_GROUPS = 32
SSM_STATE = 64
D_FF = 2816
IN_COLS = 7168
EPS = 1e-6
ADAM_LR, ADAM_B1, ADAM_B2, ADAM_EPS, ADAM_WD, ADAM_STEP = 0.001, 0.9, 0.999, 1e-08, 0.01, 10

N_CHIPS = 4
MESH = pl.DeviceIdType.MESH

LANES = 128
SUBLANES = 8
VMEM_LIMIT = 56 * 1024 * 1024

TM = 512
TM_PROJ = 1024
TL_WGRAD = 2048
TM_MIX = 512

SSM_TB = 512
SSM_TC = 64
SSM_SUB = SUBLANES
SSM_PITCH = 68
N_SLAB = SSM_GROUPS * SSM_STATE // LANES
SSM_WIN = 256
N_PAIR = N_SLAB // 2
PAIRS_PER_WIN = 4
SCAN_GROUP = 4


def _params(sem=None, collective=False):
    return pltpu.CompilerParams(dimension_semantics=sem, vmem_limit_bytes=VMEM_LIMIT)


ANY = pl.BlockSpec(memory_space=pl.ANY)

Rider = collections.namedtuple("Rider", "ins out_shapes n_sem start wait aliases")


class _SemWindow:
    def __init__(self, ref, offset):
        self.ref, self.offset = ref, offset

    @property
    def at(self):
        return self

    def __getitem__(self, k):
        return self.ref.at[self.offset + k]


def _join_riders(*riders):
    riders = [r for r in riders if r is not None]
    if len(riders) <= 1:
        return riders[0] if riders else None

    def each(fn_name):
        def run(ins, outs, send, recv):
            i = o = s = 0
            for r in riders:
                getattr(r, fn_name)(ins[i:i + len(r.ins)], outs[o:o + len(r.out_shapes)],
                                    _SemWindow(send, s), _SemWindow(recv, s))
                i, o, s = i + len(r.ins), o + len(r.out_shapes), s + r.n_sem
        return run

    aliases, i, o = {}, 0, 0
    for r in riders:
        aliases.update({i + a: o + b for a, b in r.aliases.items()})
        i, o = i + len(r.ins), o + len(r.out_shapes)
    return Rider([t for r in riders for t in r.ins], [t for r in riders for t in r.out_shapes],
                 sum(r.n_sem for r in riders), each("start"), each("wait"), aliases)


def _with_rider(body, rider, grid, prefetch, n_in, n_out, n_scratch):
    n_rin, n_rout = len(rider.ins), len(rider.out_shapes)

    def hosted(*refs):
        pre, rest = refs[:prefetch], refs[prefetch:]
        ins, rin = rest[:n_in], rest[n_in:n_in + n_rin]
        o0 = n_in + n_rin
        outs, rout = rest[o0:o0 + n_out], rest[o0 + n_out:o0 + n_out + n_rout]
        s0 = o0 + n_out + n_rout
        scr, (send, recv) = rest[s0:s0 + n_scratch], rest[s0 + n_scratch:]
        first = functools.reduce(jnp.logical_and, [pl.program_id(k) == 0 for k in range(len(grid))])
        last = functools.reduce(jnp.logical_and, [pl.program_id(k) == grid[k] - 1 for k in range(len(grid))])

        @pl.when(first)
        def _():
            rider.start(rin, rout, send, recv)

        body(*pre, *ins, *outs, *scr)

        @pl.when(last)
        def _():
            rider.wait(rin, rout, send, recv)

    return hosted


def _call(body, *, name, grid, in_specs, out_specs, out_shape, scratch=(), sem=None, aliases=None,
          prefetch=0, rider=None):
    if rider is not None:
        single = not isinstance(out_specs, (list, tuple))
        out_specs = [out_specs] if single else list(out_specs)
        out_shape = [out_shape] if single else list(out_shape)
        body = _with_rider(body, rider, grid, prefetch, len(in_specs), len(out_specs), len(scratch))
        aliases = dict(aliases or {})
        aliases.update({prefetch + len(in_specs) + k: len(out_specs) + v for k, v in rider.aliases.items()})
        in_specs = list(in_specs) + [ANY] * len(rider.ins)
        out_specs = out_specs + [ANY] * len(rider.out_shapes)
        out_shape = out_shape + list(rider.out_shapes)
        scratch = list(scratch) + [pltpu.SemaphoreType.DMA((rider.n_sem,)), pltpu.SemaphoreType.DMA((rider.n_sem,))]
        sem = ("arbitrary",) * len(grid)
        fn = _call(body, name=name + "_host", grid=grid, in_specs=in_specs, out_specs=out_specs, out_shape=out_shape,
                   scratch=scratch, sem=sem, aliases=aliases, prefetch=prefetch)
        return lambda *args: fn(*args, *rider.ins)
    kw = {}
    if aliases:
        kw["input_output_aliases"] = aliases
    if prefetch:
        gs = pltpu.PrefetchScalarGridSpec(num_scalar_prefetch=prefetch, grid=grid, in_specs=in_specs,
                                          out_specs=out_specs, scratch_shapes=list(scratch))
        return pl.pallas_call(body, name=name, grid_spec=gs, out_shape=out_shape,
                              compiler_params=_params(sem), **kw)
    return pl.pallas_call(body, name=name, grid=grid, in_specs=in_specs, out_specs=out_specs,
                          out_shape=out_shape, scratch_shapes=list(scratch),
                          compiler_params=_params(sem), **kw)


def _sds(shape, dtype):
    return jax.ShapeDtypeStruct(shape, dtype)


def _sigmoid(v):
    return 0.5 * jnp.tanh(0.5 * v) + 0.5


def _dot(a, b):
    return jnp.dot(a, b, preferred_element_type=F32)


def _dot_nt(a, b):
    return lax.dot_general(a, b, (((1,), (1,)), ((), ())), preferred_element_type=F32)


def _dot_tn(a, b):
    return lax.dot_general(a, b, (((0,), (0,)), ((), ())), preferred_element_type=F32)


def _in_proj_fwd(x, g, w, rider=None):
    L = x.shape[0]
    ns = w.shape[2]
    tn = ns
    nj = ns // tn
    TM = TM_PROJ

    def body(x_ref, g_ref, w_ref, z_ref, h_ref):
        @pl.when(pl.program_id(1) == 0)
        def _():
            xv = x_ref[...]
            r = lax.rsqrt(jnp.mean(xv * xv, axis=-1, keepdims=True) + EPS)
            h_ref[...] = (xv * r * g_ref[...]).astype(BF16)
        z_ref[...] = _dot(h_ref[...], w_ref[...]).astype(BF16)

    return _call(
        body, name="in_proj_fwd", grid=(L // TM, N_CHIPS * nj),
        in_specs=[pl.BlockSpec((TM, D_MODEL), lambda i, j: (i, 0)),
                  pl.BlockSpec((1, D_MODEL), lambda i, j: (0, 0)),
                  pl.BlockSpec((None, D_MODEL, tn), lambda i, j: (j // nj, 0, j % nj))],
        out_specs=[pl.BlockSpec((TM, tn), lambda i, j: (i, j)),
                   pl.BlockSpec((TM, D_MODEL), lambda i, j: (i, 0))],
        out_shape=[_sds((L, N_CHIPS * ns), BF16), _sds((L, D_MODEL), BF16)],
        sem=("parallel", "arbitrary"), rider=rider)(x, g, w)


DL_TILE = 512
SCALE = HEAD_DIM ** -0.5


def _perm_matrix(d):
    rho = jnp.arange(DL_TILE)
    src = rho // (DL_TILE // d) + d * (rho % (DL_TILE // d))
    return (src[:, None] == jnp.arange(DL_TILE)[None, :]).astype(BF16)


def _head_sum_matrix():
    h = jnp.arange(ATTN_WIDTH) // HEAD_DIM
    return (h[:, None] == h[None, :]).astype(BF16)


def _split(v):
    hi = v.astype(BF16)
    return hi, (v - hi.astype(F32)).astype(BF16)


def _head_sum(v, hs):
    vb = v.astype(BF16)
    half = ATTN_WIDTH // 2
    blk = hs[:half, :half]
    return jnp.concatenate([_dot(vb[:, :half], blk), _dot(vb[:, half:], blk)], axis=1)


def _permute(pm, v):
    hi, lo = _split(v)
    return _dot(pm, hi) + _dot(pm, lo)


def _dl_view(t, d):
    if d * BLK <= DL_TILE:
        return t
    return t.reshape(t.shape[0] // DL_TILE, d, DL_TILE // d, t.shape[1])


def _dl_spec(d, width, which):
    if d * BLK <= DL_TILE:
        per_tile = DL_TILE // (d * BLK)
        return pl.BlockSpec((BLK, width), lambda r, n: ((which(n) // per_tile) * (DL_TILE // BLK)
                                                       + r * per_tile + which(n) % per_tile, 0))
    tiles = d * BLK // DL_TILE
    return pl.BlockSpec((tiles, None, DL_TILE // d, width), lambda r, n: (which(n), r, 0, 0))


def _dl_read(ref):
    v = ref[...]
    return v if v.ndim == 2 else v.reshape(BLK, v.shape[-1])


def _dl_write(ref, v):
    ref[...] = v if len(ref.shape) == 2 else v.reshape(ref.shape)


def _qkv_prep(z, gq_t, gk_t, rider=None):
    L = z.shape[0]
    qkv_w = N_GROUPS * ATTN_WIDTH

    def body(zq_ref, zk_ref, zv_ref, gq_ref, gk_ref, hs_ref, p1_ref, p2_ref, *outs):
        hs = hs_ref[...]
        perms = (None, p1_ref[...], p2_ref[...])
        for g in range(N_GROUPS):
            cols = slice(g * ATTN_WIDTH, (g + 1) * ATTN_WIDTH)
            xq = zq_ref[:, cols].astype(F32)
            xk = zk_ref[:, cols].astype(F32)
            rq = lax.rsqrt(_head_sum(xq * xq, hs) * (1.0 / HEAD_DIM) + EPS)
            rk = lax.rsqrt(_head_sum(xk * xk, hs) * (1.0 / HEAD_DIM) + EPS)
            vals = [(xq * rq * (gq_ref[...] * SCALE)).astype(BF16), (xk * rk * gk_ref[...]).astype(BF16),
                    zv_ref[:, cols]]
            for j, t in enumerate(vals):
                if perms[g] is not None:
                    t = _dot(perms[g], t).astype(BF16)
                outs[3 * g + j][...] = t

    tile = pl.BlockSpec((DL_TILE, ATTN_WIDTH), lambda i: (i, 0))
    mat = pl.BlockSpec((DL_TILE, DL_TILE), lambda i: (0, 0))
    vec = pl.BlockSpec((1, ATTN_WIDTH), lambda i: (0, 0))
    outs = _call(
        body, name="qkv_prep", grid=(L // DL_TILE,),
        in_specs=[pl.BlockSpec((DL_TILE, qkv_w), lambda i: (i, 0)), pl.BlockSpec((DL_TILE, qkv_w), lambda i: (i, 1)),
                  pl.BlockSpec((DL_TILE, qkv_w), lambda i: (i, 2)), vec, vec, mat, mat, mat],
        out_specs=[tile] * 9, out_shape=[_sds((L, ATTN_WIDTH), BF16)] * 9,
        sem=("parallel",), rider=rider)(z, z, z, gq_t, gk_t, _head_sum_matrix(), _perm_matrix(ATTN_PATTERN[1][1]),
                                        _perm_matrix(ATTN_PATTERN[2][1]))
    return [tuple(outs[3 * g:3 * g + 3]) for g in range(N_GROUPS)], list(outs[3 * N_GROUPS:])


def _pair_masks():
    lane = lax.broadcasted_iota(jnp.int32, (1, LANES), 1)
    return lane < HEAD_DIM, lane >= HEAD_DIM


def _attn_fwd(qs, ks, v, gi):
    L = qs.shape[0]
    _, d = ATTN_PATTERN[gi]
    nb = L // (d * BLK)

    def body(q_ref, kc_ref, kp_ref, vc_ref, vp_ref, o_ref, l_ref):
        n = pl.program_id(1)
        qi = lax.broadcasted_iota(jnp.int32, (BLK, 2 * BLK), 0)
        kj = lax.broadcasted_iota(jnp.int32, (BLK, 2 * BLK), 1)
        prev = kj < BLK
        mask = jnp.logical_and(jnp.where(prev, kj, qi) >= jnp.where(prev, qi, kj - BLK),
                               kj >= jnp.where(n > 0, 0, BLK))
        q = _dl_read(q_ref)
        kw = jnp.concatenate([_dl_read(kp_ref), _dl_read(kc_ref)], axis=0)
        vw = jnp.concatenate([_dl_read(vp_ref), _dl_read(vc_ref)], axis=0)
        one = jnp.ones((2 * BLK, LANES), BF16)
        o_parts, l_parts = [], []
        for hp in range(N_HEADS // 2):
            ls = slice(hp * LANES, (hp + 1) * LANES)
            qp, kp_, vp_ = q[:, ls], kw[:, ls], vw[:, ls]
            num = jnp.zeros((BLK, LANES), F32)
            den = jnp.zeros((BLK, LANES), F32)
            mb = jnp.zeros((BLK, LANES), F32)
            for he in _pair_masks():
                s = jnp.where(mask, _dot_nt(jnp.where(he, qp, 0), kp_), -jnp.inf)
                m = jnp.max(s, axis=-1, keepdims=True)
                p = jnp.exp(s - m).astype(BF16)
                acc = _dot(p, jnp.concatenate([jnp.where(he, vp_, 0), jnp.where(he, one, 0)], axis=1))
                num += acc[:, :LANES]
                den += acc[:, LANES:]
                mb = jnp.where(he, m, mb)
            o_parts.append((num / den).astype(BF16))
            l_parts.append(mb + jnp.log(den))
        _dl_write(o_ref, jnp.concatenate(o_parts, axis=1))
        _dl_write(l_ref, jnp.concatenate(l_parts, axis=1))

    cur = _dl_spec(d, ATTN_WIDTH, lambda n: n)
    prev = _dl_spec(d, ATTN_WIDTH, lambda n: jnp.maximum(n - 1, 0))
    view = lambda t: _dl_view(t, d)
    o, l = _call(
        body, name=f"attn_fwd_g{gi}", grid=(d, nb), in_specs=[cur, cur, prev, cur, prev], out_specs=[cur, cur],
        out_shape=[_sds(view(qs).shape, BF16), _sds(view(qs).shape, F32)],
        sem=("parallel", "parallel"))(view(qs), view(ks), view(ks), view(v), view(v))
    return o.reshape(L, ATTN_WIDTH), l.reshape(L, ATTN_WIDTH)


def _to_token_order(os_, ls_, pts):
    o_tok, l_tok = [], []
    for o, l, pt in zip(os_, ls_, pts):
        if pt is None:
            o_tok.append(o.astype(F32))
            l_tok.append(l)
        else:
            o_tok.append(_dot(pt, o))
            l_tok.append(_permute(pt, l))
    return o_tok, l_tok


def _combine_fwd(os_, ls_):
    L = os_[0].shape[0]

    def body(o0, o1, o2, l0, l1, l2, pt1_ref, pt2_ref, a_ref):
        o_tok, l_tok = _to_token_order((o0[...], o1[...], o2[...]), (l0[...], l1[...], l2[...]),
                                       (None, pt1_ref[...], pt2_ref[...]))
        w = _combine_weights(*l_tok)
        a_ref[...] = (w[0] * o_tok[0] + w[1] * o_tok[1] + w[2] * o_tok[2]).astype(BF16)

    tile = pl.BlockSpec((DL_TILE, ATTN_WIDTH), lambda i: (i, 0))
    mat = pl.BlockSpec((DL_TILE, DL_TILE), lambda i: (0, 0))
    return _call(body, name="combine_fwd", grid=(L // DL_TILE,), in_specs=[tile] * 6 + [mat, mat], out_specs=tile,
                 out_shape=_sds((L, ATTN_WIDTH), BF16), sem=("parallel",))(
                     *os_, *ls_, _perm_matrix(ATTN_PATTERN[1][1]).T, _perm_matrix(ATTN_PATTERN[2][1]).T)


def _gelu(v):
    c = math.sqrt(2.0 / math.pi)
    return 0.5 * v * (1.0 + jnp.tanh(c * (v + 0.044715 * v * v * v)))


def _gelu_grad(v):
    c = math.sqrt(2.0 / math.pi)
    t = jnp.tanh(c * (v + 0.044715 * v * v * v))
    return 0.5 * (1.0 + t) + 0.5 * v * (1.0 - t * t) * c * (1.0 + 3.0 * 0.044715 * v * v)


def _ssm_fill(u, bwre_ref, bwim_ref, sre, sim):
    for k2 in range(N_PAIR):
        uw = u[:, _win_cols(k2)]
        _to_slabs(sre, k2, _dot(uw, bwre_ref[k2]))
        _to_slabs(sim, k2, _dot(uw, bwim_ref[k2]))


def _win_cols(k2):
    w = k2 // PAIRS_PER_WIN
    return slice(w * SSM_WIN, (w + 1) * SSM_WIN)


def _to_slabs(ref, k2, v):
    for half in range(2):
        for j in range(SSM_SUB):
            ref[2 * k2 + half, j * SSM_PITCH:j * SSM_PITCH + SSM_TC, :] = (
                v[j * SSM_TC:(j + 1) * SSM_TC, half * LANES:(half + 1) * LANES])


def _rows(i):
    return pl.ds(i, SSM_SUB, stride=SSM_PITCH)


def _slab_rows(ref, k):
    return jnp.concatenate([ref[k, j * SSM_PITCH:j * SSM_PITCH + SSM_TC, :] for j in range(SSM_SUB)], axis=0)


def _pair_rows(ref, k2):
    return jnp.concatenate([_slab_rows(ref, 2 * k2), _slab_rows(ref, 2 * k2 + 1)], axis=1).astype(BF16)


def _bcast(ref, k):
    return jnp.broadcast_to(ref[pl.ds(k, 1), :], (SSM_SUB, LANES))


def _scan(sre, sim, are_ref, aim_ref, k0, init, *, reverse, store, sign=1.0):
    ar = [_bcast(are_ref, k0 + kk) for kk in range(SCAN_GROUP)]
    ai = [sign * _bcast(aim_ref, k0 + kk) for kk in range(SCAN_GROUP)]

    def step(t, carry):
        i = SSM_TC - 1 - t if reverse else t
        out = []
        for kk in range(SCAN_GROUP):
            k = k0 + kk
            xr, xi = carry[2 * kk], carry[2 * kk + 1]
            nr = ar[kk] * xr - ai[kk] * xi + sre[k, _rows(i), :]
            ni = ar[kk] * xi + ai[kk] * xr + sim[k, _rows(i), :]
            if store:
                sre[k, _rows(i), :] = nr
                sim[k, _rows(i), :] = ni
            out += [nr, ni]
        return tuple(out)

    flat = []
    for re, im in init:
        flat += [re, im]
    res = lax.fori_loop(0, SSM_TC // 2, lambda t, c: step(2 * t + 1, step(2 * t, c)), tuple(flat))
    return [(res[2 * kk], res[2 * kk + 1]) for kk in range(SCAN_GROUP)]


def _ssm_seeds(ends_re, ends_im, a64re_ref, a64im_ref, carry_re, carry_im, seed_re, seed_im, k,
               *, reverse, sign=1.0):
    ar = a64re_ref[pl.ds(k, 1), :]
    ai = sign * a64im_ref[pl.ds(k, 1), :]
    cr = carry_re[pl.ds(k, 1), :]
    ci = carry_im[pl.ds(k, 1), :]
    order = range(SSM_SUB - 1, -1, -1) if reverse else range(SSM_SUB)
    for j in order:
        seed_re[k, pl.ds(j, 1), :] = cr
        seed_im[k, pl.ds(j, 1), :] = ci
        er = ends_re[k, pl.ds(j, 1), :]
        ei = ends_im[k, pl.ds(j, 1), :]
        cr, ci = ar * cr - ai * ci + er, ar * ci + ai * cr + ei
    carry_re[pl.ds(k, 1), :] = cr
    carry_im[pl.ds(k, 1), :] = ci


def _ssm_specs_consts():
    c2 = pl.BlockSpec((N_SLAB, LANES), lambda b: (0, 0))
    c3 = pl.BlockSpec((N_PAIR, SSM_WIN, SSM_WIN), lambda b: (0, 0, 0))
    return c2, c3


def _ssm_scratch():
    rows = SSM_SUB * SSM_PITCH
    return [pltpu.VMEM((N_SLAB, rows, LANES), F32), pltpu.VMEM((N_SLAB, rows, LANES), F32)]


def _ssm_fwd(z, pk, dskip, rider=None):
    L = z.shape[0]
    nb = L // SSM_TB
    ucol = (3 * N_GROUPS * ATTN_WIDTH) // SSM_WIDTH

    def body(u_ref, are_ref, aim_ref, a64re_ref, a64im_ref, bwre_ref, bwim_ref, cwre_ref, cwim_ref, d_ref,
             ypre_ref, yact_ref, sdre_ref, sdim_ref, sre, sim, carry_re, carry_im, ends_re, ends_im,
             seed_re, seed_im):
        @pl.when(pl.program_id(0) == 0)
        def _():
            carry_re[...] = jnp.zeros_like(carry_re)
            carry_im[...] = jnp.zeros_like(carry_im)

        u = u_ref[...]
        _ssm_fill(u, bwre_ref, bwim_ref, sre, sim)
        zero = jnp.zeros((SSM_SUB, LANES), F32)
        for k0 in range(0, N_SLAB, SCAN_GROUP):
            ends = _scan(sre, sim, are_ref, aim_ref, k0, [(zero, zero)] * SCAN_GROUP, reverse=False, store=False)
            for kk in range(SCAN_GROUP):
                ends_re[k0 + kk] = ends[kk][0]
                ends_im[k0 + kk] = ends[kk][1]
            for kk in range(SCAN_GROUP):
                _ssm_seeds(ends_re, ends_im, a64re_ref, a64im_ref, carry_re, carry_im, seed_re, seed_im,
                           k0 + kk, reverse=False)
            init = [(seed_re[k0 + kk], seed_im[k0 + kk]) for kk in range(SCAN_GROUP)]
            _scan(sre, sim, are_ref, aim_ref, k0, init, reverse=False, store=True)
        sdre_ref[...] = seed_re[...]
        sdim_ref[...] = seed_im[...]
        for w in range(N_PAIR // PAIRS_PER_WIN):
            acc = jnp.zeros((SSM_TB, SSM_WIN), F32)
            for kk in range(PAIRS_PER_WIN):
                k2 = w * PAIRS_PER_WIN + kk
                acc += _dot(_pair_rows(sre, k2), cwre_ref[k2])
                acc -= _dot(_pair_rows(sim, k2), cwim_ref[k2])
            cols = _win_cols(w * PAIRS_PER_WIN)
            ypre = acc + d_ref[:, cols] * u[:, cols].astype(F32)
            ypre_ref[:, cols] = ypre
            yact_ref[:, cols] = _gelu(ypre).astype(BF16)

    c2, c3 = _ssm_specs_consts()
    seed_spec = pl.BlockSpec((None, N_SLAB, SSM_SUB, LANES), lambda b: (b, 0, 0, 0))
    small = pltpu.VMEM((N_SLAB, LANES), F32)
    tile = pltpu.VMEM((N_SLAB, SSM_SUB, LANES), F32)
    return _call(
        body, name="ssm_fwd", grid=(nb,),
        in_specs=[pl.BlockSpec((SSM_TB, SSM_WIDTH), lambda b: (b, ucol)), c2, c2, c2, c2, c3, c3, c3, c3,
                  pl.BlockSpec((1, SSM_WIDTH), lambda b: (0, 0))],
        out_specs=[pl.BlockSpec((SSM_TB, SSM_WIDTH), lambda b: (b, 0)),
                   pl.BlockSpec((SSM_TB, SSM_WIDTH), lambda b: (b, 0)), seed_spec, seed_spec],
        out_shape=[_sds((L, SSM_WIDTH), F32), _sds((L, SSM_WIDTH), BF16),
                   _sds((nb, N_SLAB, SSM_SUB, LANES), F32), _sds((nb, N_SLAB, SSM_SUB, LANES), F32)],
        scratch=_ssm_scratch() + [small, small, tile, tile, tile, tile],
        sem=("arbitrary",), rider=rider)(z, pk["a_re"], pk["a_im"], pk["a64_re"], pk["a64_im"],
                                         pk["bw_re"].astype(BF16), pk["bw_im"].astype(BF16),
                                         pk["cw_re"].astype(BF16), pk["cw_im"].astype(BF16), dskip)


def _combine_weights(l0, l1, l2):
    m = jnp.maximum(jnp.maximum(l0, l1), l2)
    e0, e1, e2 = jnp.exp(l0 - m), jnp.exp(l1 - m), jnp.exp(l2 - m)
    inv = 1.0 / (e0 + e1 + e2)
    return e0 * inv, e1 * inv, e2 * inv


def _mix_fwd(x, z, a, yact, w_ap, w_ga, w_gb, w_out):
    L = x.shape[0]
    cs = D_MODEL // N_CHIPS
    ga_col = (3 * N_GROUPS * ATTN_WIDTH + SSM_WIDTH) // D_MODEL

    def body(x_ref, ga_ref, gs_ref, a_ref, y_ref, wap_ref, wga_ref, wgb_ref, wout_ref,
             x1_ref, aout_ref, sa_ref, sb_ref, mix_ref):
        a = a_ref[...]
        y = y_ref[...]
        for s in range(N_CHIPS):
            cols = slice(s * cs, (s + 1) * cs)
            aout_ref[:, cols] = _dot(a, wap_ref[s]).astype(BF16)
            sa_ref[:, cols] = _dot(y, wga_ref[s]).astype(BF16)
            sb_ref[:, cols] = _dot(y, wgb_ref[s]).astype(BF16)
        s_out = sa_ref[...].astype(F32) * _sigmoid(sb_ref[...].astype(F32))
        mix = (_sigmoid(ga_ref[...].astype(F32)) * aout_ref[...].astype(F32)
               + _sigmoid(gs_ref[...].astype(F32)) * s_out).astype(BF16)
        mix_ref[...] = mix
        x1_ref[...] = x_ref[...] + _dot(mix, wout_ref[...])

    tok = lambda w: pl.BlockSpec((TM_MIX, w), lambda i: (i, 0))
    wsm = pl.BlockSpec((N_CHIPS, ATTN_WIDTH, cs), lambda i: (0, 0, 0))
    return _call(
        body, name="mix_fwd", grid=(L // TM_MIX,),
        in_specs=[tok(D_MODEL), pl.BlockSpec((TM_MIX, D_MODEL), lambda i: (i, ga_col)),
                  pl.BlockSpec((TM_MIX, D_MODEL), lambda i: (i, ga_col + 1))]
                 + [tok(ATTN_WIDTH)] * 2 + [wsm, wsm, wsm, pl.BlockSpec((D_MODEL, D_MODEL), lambda i: (0, 0))],
        out_specs=[tok(D_MODEL), tok(D_MODEL), tok(D_MODEL), tok(D_MODEL), tok(D_MODEL)],
        out_shape=[_sds((L, D_MODEL), F32)] + [_sds((L, D_MODEL), BF16)] * 4,
        sem=("parallel",))(x, z, z, a, yact, w_ap, w_ga, w_gb, w_out.reshape(D_MODEL, D_MODEL))


def _ffn_fwd(x1, g, w_g, w_u, w_d, rider=None):
    L = x1.shape[0]
    fs = D_FF // N_CHIPS
    TM = TM_PROJ

    def body(x_ref, g_ref, wg_ref, wu_ref, wd_ref, x2_ref, h_ref, gate_ref, up_ref, act_ref, acc):
        s = pl.program_id(1)

        @pl.when(s == 0)
        def _():
            xv = x_ref[...]
            r = lax.rsqrt(jnp.mean(xv * xv, axis=-1, keepdims=True) + EPS)
            h_ref[...] = (xv * r * g_ref[...]).astype(BF16)
            acc[...] = jnp.zeros_like(acc)

        h = h_ref[...]
        gate = _dot_nt(h, wg_ref[...])
        up = _dot_nt(h, wu_ref[...])
        act = (gate * _sigmoid(gate) * up).astype(BF16)
        gate_ref[...] = gate.astype(BF16)
        up_ref[...] = up.astype(BF16)
        act_ref[...] = act
        acc[...] += _dot(act, wd_ref[...])

        @pl.when(s == N_CHIPS - 1)
        def _():
            x2_ref[...] = x_ref[...] + acc[...]

    tok = pl.BlockSpec((TM, D_MODEL), lambda i, s: (i, 0))
    ffs = pl.BlockSpec((None, TM, fs), lambda i, s: (s, i, 0))
    return _call(
        body, name="ffn_fwd", grid=(L // TM, N_CHIPS),
        in_specs=[tok, pl.BlockSpec((1, D_MODEL), lambda i, s: (0, 0))]
                 + [pl.BlockSpec((None, fs, D_MODEL), lambda i, s: (s, 0, 0))] * 3,
        out_specs=[tok, tok, ffs, ffs, ffs],
        out_shape=[_sds((L, D_MODEL), F32), _sds((L, D_MODEL), BF16)] + [_sds((N_CHIPS, L, fs), BF16)] * 3,
        scratch=[pltpu.VMEM((TM, D_MODEL), F32)],
        sem=("parallel", "arbitrary"), rider=rider)(x1, g, w_g, w_u, w_d)


def _loss_head(xl, target):
    L = xl.shape[0]

    def body(x_ref, t_ref, dx_ref, loss_ref, acc):
        i = pl.program_id(0)

        @pl.when(i == 0)
        def _():
            acc[...] = jnp.zeros_like(acc)

        e = x_ref[...] - t_ref[...]
        dx_ref[...] = e * (1.0 / D_MODEL)
        acc[...] += jnp.sum((e * e).reshape(TM // SUBLANES, SUBLANES, D_MODEL), axis=0)

        @pl.when(i == pl.num_programs(0) - 1)
        def _():
            loss_ref[...] = (0.5 / D_MODEL) * jnp.sum(acc[...]).reshape(1, 1)

    tok = pl.BlockSpec((TM, D_MODEL), lambda i: (i, 0))
    return _call(
        body, name="loss_head", grid=(L // TM,), in_specs=[tok, tok],
        out_specs=[tok, pl.BlockSpec((1, 1), lambda i: (0, 0))],
        out_shape=[_sds((L, D_MODEL), F32), _sds((1, 1), F32)],
        scratch=[pltpu.VMEM((SUBLANES, D_MODEL), F32)], sem=("arbitrary",))(xl, target)


def _ssm_pack(lam_re, lam_im, log_dt, b_re, b_im, c_re, c_im):
    dt = jnp.exp(log_dt)[:, None]
    mag = jnp.exp(lam_re * dt)
    ang = lam_im * dt
    ar = mag * jnp.cos(ang)
    ai = mag * jnp.sin(ang)
    nr = ar - 1.0
    ni = ai
    den = lam_re * lam_re + lam_im * lam_im
    cr = ((nr * lam_re + ni * lam_im) / den)[..., None]
    ci = ((ni * lam_re - nr * lam_im) / den)[..., None]
    bbr = cr * b_re - ci * b_im
    bbi = cr * b_im + ci * b_re
    gpp = SSM_WIN // SSM_STATE
    gpw = SSM_WIN // SSM_GROUP
    k2 = jnp.arange(N_PAIR)[:, None, None]
    gs = jnp.arange(gpp)[None, :, None]
    gl = jnp.arange(gpw)[None, None, :]
    same = (gl == gpp * (k2 % PAIRS_PER_WIN) + gs).astype(F32)

    def b_windows(bb):
        return jnp.einsum('kgl,kgpc->klcgp', same, bb.reshape(N_PAIR, gpp, SSM_STATE, SSM_GROUP)).reshape(
            N_PAIR, SSM_WIN, SSM_WIN)

    def c_windows(cc):
        return jnp.einsum('kgl,kgcp->kgplc', same, cc.reshape(N_PAIR, gpp, SSM_GROUP, SSM_STATE)).reshape(
            N_PAIR, SSM_WIN, SSM_WIN)

    pr, pi = ar, ai
    for _ in range(int(math.log2(SSM_TC))):
        pr, pi = pr * pr - pi * pi, 2.0 * pr * pi
    return dict(a_re=ar.reshape(N_SLAB, LANES), a_im=ai.reshape(N_SLAB, LANES),
                a64_re=pr.reshape(N_SLAB, LANES), a64_im=pi.reshape(N_SLAB, LANES),
                bw_re=b_windows(bbr), bw_im=b_windows(bbi), cw_re=c_windows(c_re), cw_im=c_windows(c_im))


def _layer_fwd(x, p, rest, rest_stage, next_bufs=None):
    first = {"ici": _gather_ici_rider, "d2d": _gather_d2d_rider}[rest_stage]
    outs = _in_proj_fwd(x, p["g_mix"], p["w_in"], first(rest))
    (z, h), rest = outs[:2], list(outs[2:])
    qkv, got = _qkv_prep(z, jnp.tile(p["g_q"], (1, N_HEADS)), jnp.tile(p["g_k"], (1, N_HEADS)),
                         _gather_d2d_rider(rest) if rest_stage == "ici" else None)
    p = {**p, **dict(zip(BIG[1:], got if rest_stage == "ici" else rest))}
    os_, ls_ = [], []
    for gi in range(N_GROUPS):
        o, l = _attn_fwd(*qkv[gi], gi)
        os_.append(o)
        ls_.append(l)
    a = _combine_fwd(os_, ls_)
    pk = _ssm_pack(p["lambda_re"], p["lambda_im"], p["log_dt"], p["b_re"], p["b_im"], p["c_re"], p["c_im"])
    outs = _ssm_fwd(z, pk, p["d_skip"], _gather_ici_rider(next_bufs[:1]) if next_bufs else None)
    (ypre, yact, sd_re, sd_im), next_in = outs[:4], list(outs[4:])
    x1, aout, sa, sb, mix = _mix_fwd(x, z, a, yact, p["w_attn_proj"], p["w_glu_a"], p["w_glu_b"], p["w_out"])
    outs = _ffn_fwd(x1, p["g_ffn"], p["w_ffn_gate"], p["w_ffn_up"], p["w_ffn_down"],
                    _join_riders(_gather_ici_rider(next_bufs[1:]), _gather_d2d_rider(next_in)) if next_bufs else None)
    x2, h2, gate, up, act = outs[:5]
    nxt = (outs[-1], list(outs[5:-1])) if next_bufs else None
    saved = dict(x=x, z=z, h=h, qkv=qkv, os=os_, ls=ls_, pk=pk, ypre=ypre, yact=yact, sd_re=sd_re, sd_im=sd_im,
                 x1=x1, a=a, aout=aout, sa=sa, sb=sb, mix=mix, h2=h2, gate=gate, up=up, act=act)
    return x2, saved, p, nxt


def _rms_bwd(xv, g, dh):
    r = lax.rsqrt(jnp.mean(xv * xv, axis=-1, keepdims=True) + EPS)
    xn = xv * r
    dxn = dh * g
    dx = r * (dxn - xn * jnp.mean(dxn * xn, axis=-1, keepdims=True))
    dg = jnp.sum((dh * xn).reshape(xv.shape[0] // SUBLANES, SUBLANES, xv.shape[1]), axis=0)
    return dx, dg


def _ffn_bwd_act(dx2, gate, up, w_d):
    L = dx2.shape[0]
    fs = D_FF // N_CHIPS
    TM = TM_PROJ

    def body(dx_ref, gate_ref, up_ref, wd_ref, dgate_ref, dup_ref):
        dact = _dot_nt(dx_ref[...].astype(BF16), wd_ref[...])
        gt = gate_ref[...].astype(F32)
        sg = _sigmoid(gt)
        dgate_ref[...] = (dact * up_ref[...].astype(F32) * (sg * (1.0 + gt * (1.0 - sg)))).astype(BF16)
        dup_ref[...] = (dact * gt * sg).astype(BF16)

    ffs = pl.BlockSpec((None, TM, fs), lambda i, s: (s, i, 0))
    return _call(
        body, name="ffn_bwd_act", grid=(L // TM, N_CHIPS),
        in_specs=[pl.BlockSpec((TM, D_MODEL), lambda i, s: (i, 0)), ffs, ffs,
                  pl.BlockSpec((None, fs, D_MODEL), lambda i, s: (s, 0, 0))],
        out_specs=[ffs, ffs], out_shape=[_sds((N_CHIPS, L, fs), BF16)] * 2,
        sem=("parallel", "parallel"))(dx2, gate, up, w_d)


def _ffn_bwd_in(dx2, x1, g, dgate, dup, w_g, w_u, rider=None):
    L = x1.shape[0]
    fs = D_FF // N_CHIPS
    TM = TM_PROJ

    def body(dx_ref, x_ref, g_ref, dgate_ref, dup_ref, wg_ref, wu_ref, dx1_ref, dg_ref, acc, dgacc):
        i, s = pl.program_id(0), pl.program_id(1)

        @pl.when(s == 0)
        def _():
            acc[...] = jnp.zeros_like(acc)

        @pl.when(jnp.logical_and(i == 0, s == 0))
        def _():
            dgacc[...] = jnp.zeros_like(dgacc)

        acc[...] += _dot(dgate_ref[...], wg_ref[...]) + _dot(dup_ref[...], wu_ref[...])

        @pl.when(s == N_CHIPS - 1)
        def _():
            dx, dg = _rms_bwd(x_ref[...], g_ref[...], acc[...])
            dx1_ref[...] = dx_ref[...] + dx
            dgacc[...] += dg

        @pl.when(jnp.logical_and(i == pl.num_programs(0) - 1, s == N_CHIPS - 1))
        def _():
            dg_ref[...] = jnp.sum(dgacc[...], axis=0, keepdims=True)

    tok = pl.BlockSpec((TM, D_MODEL), lambda i, s: (i, 0))
    ffs = pl.BlockSpec((None, TM, fs), lambda i, s: (s, i, 0))
    vec = pl.BlockSpec((1, D_MODEL), lambda i, s: (0, 0))
    return _call(
        body, name="ffn_bwd_in", grid=(L // TM, N_CHIPS),
        in_specs=[tok, tok, vec, ffs, ffs,
                  pl.BlockSpec((None, fs, D_MODEL), lambda i, s: (s, 0, 0)),
                  pl.BlockSpec((None, fs, D_MODEL), lambda i, s: (s, 0, 0))],
        out_specs=[tok, vec],
        out_shape=[_sds((L, D_MODEL), F32), _sds((1, D_MODEL), F32)],
        scratch=[pltpu.VMEM((TM, D_MODEL), F32), pltpu.VMEM((SUBLANES, D_MODEL), F32)],
        sem=("arbitrary", "arbitrary"), rider=rider)(dx2, x1, g, dgate, dup, w_g, w_u)


def _wgrad(a, b, *, name, grid_kn, a_spec, b_spec, out_shape, out_spec):
    L = a.shape[-2]
    nl = L // TL_WGRAD

    def body(a_ref, b_ref, o_ref):
        @pl.when(pl.program_id(2) == 0)
        def _():
            o_ref[...] = jnp.zeros_like(o_ref)
        o_ref[...] += _dot_tn(a_ref[...].astype(BF16), b_ref[...].astype(BF16))

    return _call(body, name=name, grid=(*grid_kn, nl), in_specs=[a_spec, b_spec], out_specs=out_spec,
                 out_shape=out_shape, sem=("parallel", "parallel", "arbitrary"))(a, b)


def _wgrad_cols(a, b, name):
    K, N = a.shape[1], b.shape[1]
    ns = N // N_CHIPS
    if N * K * 4 <= 4 * 1024 * 1024:
        L = a.shape[0]

        def body(a_ref, b_ref, o_ref):
            @pl.when(pl.program_id(0) == 0)
            def _():
                o_ref[...] = jnp.zeros_like(o_ref)
            av = a_ref[...].astype(BF16)
            for s in range(N_CHIPS):
                o_ref[s] += _dot_tn(av, b_ref[:, s * ns:(s + 1) * ns].astype(BF16))

        return _call(body, name=name, grid=(L // TL_WGRAD,),
                     in_specs=[pl.BlockSpec((TL_WGRAD, K), lambda t: (t, 0)),
                               pl.BlockSpec((TL_WGRAD, N), lambda t: (t, 0))],
                     out_specs=pl.BlockSpec((N_CHIPS, K, ns), lambda t: (0, 0, 0)),
                     out_shape=_sds((N_CHIPS, K, ns), F32), sem=("arbitrary",))(a, b)
    tn = ns // 2 if ns % (2 * LANES) == 0 else ns
    nj = ns // tn
    return _wgrad(a, b, name=name, grid_kn=(1, N_CHIPS * nj),
                  a_spec=pl.BlockSpec((TL_WGRAD, K), lambda i, j, t: (t, 0)),
                  b_spec=pl.BlockSpec((TL_WGRAD, tn), lambda i, j, t: (t, j)),
                  out_shape=_sds((N_CHIPS, K, ns), F32),
                  out_spec=pl.BlockSpec((None, K, tn), lambda i, j, t: (j // nj, 0, j % nj)))


def _wgrad_full(a, b, name):
    K, N = a.shape[1], b.shape[1]
    return _wgrad(a, b, name=name, grid_kn=(1, 1),
                  a_spec=pl.BlockSpec((TL_WGRAD, K), lambda i, j, t: (t, 0)),
                  b_spec=pl.BlockSpec((TL_WGRAD, N), lambda i, j, t: (t, 0)),
                  out_shape=_sds((K, N), F32), out_spec=pl.BlockSpec((K, N), lambda i, j, t: (0, 0)))


def _wgrad_ff_cols(a, b, name):
    K, fs = a.shape[1], b.shape[2]
    return _wgrad(a, b, name=name, grid_kn=(1, N_CHIPS),
                  a_spec=pl.BlockSpec((TL_WGRAD, K), lambda i, j, t: (t, 0)),
                  b_spec=pl.BlockSpec((None, TL_WGRAD, fs), lambda i, j, t: (j, t, 0)),
                  out_shape=_sds((N_CHIPS, K, fs), F32),
                  out_spec=pl.BlockSpec((None, K, fs), lambda i, j, t: (j, 0, 0)))


def _wgrad_ff_rows(a, b, name):
    fs, N = a.shape[2], b.shape[1]
    return _wgrad(a, b, name=name, grid_kn=(N_CHIPS, 1),
                  a_spec=pl.BlockSpec((None, TL_WGRAD, fs), lambda i, j, t: (i, t, 0)),
                  b_spec=pl.BlockSpec((TL_WGRAD, N), lambda i, j, t: (t, 0)),
                  out_shape=_sds((N_CHIPS, fs, N), F32),
                  out_spec=pl.BlockSpec((None, fs, N), lambda i, j, t: (i, 0, 0)))


def _mix_bwd(dx, z, aout, sa, sb, ypre, w_ap, w_ga, w_gb, w_out, rider=None):
    L = dx.shape[0]
    cs = D_MODEL // N_CHIPS
    ga_col = (3 * N_GROUPS * ATTN_WIDTH + SSM_WIDTH) // D_MODEL

    def body(dx_ref, ga_ref, gs_ref, aout_ref, sa_ref, sb_ref, ypre_ref, wap_ref, wga_ref, wgb_ref, wout_ref,
             dgates_ref, da_ref, gy_ref, daout_ref, dsa_ref, dsb_ref):
        dmix = _dot_nt(dx_ref[...].astype(BF16), wout_ref[...])
        sig_a = _sigmoid(ga_ref[...].astype(F32))
        sig_s = _sigmoid(gs_ref[...].astype(F32))
        a_out = aout_ref[...].astype(F32)
        s_a = sa_ref[...].astype(F32)
        sig_b = _sigmoid(sb_ref[...].astype(F32))
        s_out = s_a * sig_b
        daout = (dmix * sig_a).astype(BF16)
        daout_ref[...] = daout
        dgates_ref[:, :D_MODEL] = (dmix * a_out * sig_a * (1.0 - sig_a)).astype(BF16)
        dgates_ref[:, D_MODEL:] = (dmix * s_out * sig_s * (1.0 - sig_s)).astype(BF16)
        ds_out = dmix * sig_s
        dsa = (ds_out * sig_b).astype(BF16)
        dsb = (ds_out * s_a * sig_b * (1.0 - sig_b)).astype(BF16)
        dsa_ref[...] = dsa
        dsb_ref[...] = dsb
        da = jnp.zeros((TM_MIX, ATTN_WIDTH), F32)
        dy = jnp.zeros((TM_MIX, SSM_WIDTH), F32)
        for s in range(N_CHIPS):
            cols = slice(s * cs, (s + 1) * cs)
            da += _dot_nt(daout[:, cols], wap_ref[s])
            dy += _dot_nt(dsa[:, cols], wga_ref[s]) + _dot_nt(dsb[:, cols], wgb_ref[s])
        gy_ref[...] = dy * _gelu_grad(ypre_ref[...])
        da_ref[...] = da

    tok = lambda w: pl.BlockSpec((TM_MIX, w), lambda i: (i, 0))
    wsm = pl.BlockSpec((N_CHIPS, ATTN_WIDTH, cs), lambda i: (0, 0, 0))
    return _call(
        body, name="mix_bwd", grid=(L // TM_MIX,),
        in_specs=[tok(D_MODEL), pl.BlockSpec((TM_MIX, D_MODEL), lambda i: (i, ga_col)),
                  pl.BlockSpec((TM_MIX, D_MODEL), lambda i: (i, ga_col + 1)),
                  tok(D_MODEL), tok(D_MODEL), tok(D_MODEL), tok(SSM_WIDTH),
                  wsm, wsm, wsm, pl.BlockSpec((D_MODEL, D_MODEL), lambda i: (0, 0))],
        out_specs=[tok(2 * D_MODEL), tok(ATTN_WIDTH), tok(SSM_WIDTH)] + [tok(D_MODEL)] * 3,
        out_shape=[_sds((L, 2 * D_MODEL), BF16), _sds((L, ATTN_WIDTH), F32), _sds((L, SSM_WIDTH), F32)]
                  + [_sds((L, D_MODEL), BF16)] * 3,
        sem=("parallel",), rider=rider)(dx, z, z, aout, sa, sb, ypre, w_ap, w_ga, w_gb,
                                        w_out.reshape(D_MODEL, D_MODEL))


def _combine_bwd(da, os_, ls_):
    L = da.shape[0]

    def body(da_ref, o0, o1, o2, l0, l1, l2, hs_ref, p1_ref, p2_ref, pt1_ref, pt2_ref,
             do0, do1, do2, c0, c1, c2):
        o_tok, l_tok = _to_token_order((o0[...], o1[...], o2[...]), (l0[...], l1[...], l2[...]),
                                       (None, pt1_ref[...], pt2_ref[...]))
        w = _combine_weights(*l_tok)
        dav = da_ref[...]
        hs = hs_ref[...]
        tbar = sum(wg * _head_sum(dav * og, hs) for wg, og in zip(w, o_tok))
        for wg, pm, do_ref, c_ref in zip(w, (None, p1_ref[...], p2_ref[...]), (do0, do1, do2), (c0, c1, c2)):
            dog = (wg * dav).astype(BF16)
            cg = -wg * tbar
            do_ref[...] = dog if pm is None else _dot(pm, dog).astype(BF16)
            c_ref[...] = cg if pm is None else _dot(pm, cg.astype(BF16))

    tile = pl.BlockSpec((DL_TILE, ATTN_WIDTH), lambda i: (i, 0))
    mat = pl.BlockSpec((DL_TILE, DL_TILE), lambda i: (0, 0))
    p1, p2 = _perm_matrix(ATTN_PATTERN[1][1]), _perm_matrix(ATTN_PATTERN[2][1])
    outs = _call(body, name="combine_bwd", grid=(L // DL_TILE,), in_specs=[tile] * 7 + [mat] * 5,
                 out_specs=[tile] * 6,
                 out_shape=[_sds((L, ATTN_WIDTH), BF16)] * 3 + [_sds((L, ATTN_WIDTH), F32)] * 3,
                 sem=("parallel",))(da, *os_, *ls_, _head_sum_matrix(), p1, p2, p1.T, p2.T)
    return outs[:3], outs[3:]


def _attn_bwd(qs, ks, v, do, l, c, gi, rider=None):
    L = qs.shape[0]
    _, d = ATTN_PATTERN[gi]
    nb = L // (d * BLK)

    def body(q0_ref, q1_ref, k_ref, v_ref, do0_ref, do1_ref, l0_ref, l1_ref, c0_ref, c1_ref,
             dq_ref, dk_ref, dv_ref, carry):
        n = pl.program_id(1)

        @pl.when(n == 0)
        def _():
            carry[...] = jnp.zeros_like(carry)

        qi = lax.broadcasted_iota(jnp.int32, (2 * BLK, BLK), 0)
        kj = lax.broadcasted_iota(jnp.int32, (2 * BLK, BLK), 1)
        first = qi < BLK
        mask = jnp.logical_and(jnp.where(first, qi, kj) >= jnp.where(first, kj, qi - BLK),
                               qi < jnp.where(n < nb - 1, 2 * BLK, BLK))
        q2 = jnp.concatenate([_dl_read(q0_ref), _dl_read(q1_ref)], axis=0)
        do2 = jnp.concatenate([_dl_read(do0_ref), _dl_read(do1_ref)], axis=0)
        l2 = jnp.concatenate([_dl_read(l0_ref), _dl_read(l1_ref)], axis=0)
        c2 = jnp.concatenate([_dl_read(c0_ref), _dl_read(c1_ref)], axis=0)
        k = _dl_read(k_ref)
        v_ = _dl_read(v_ref)
        h0, h1 = _pair_masks()
        mask2 = jnp.concatenate([mask, mask], axis=1)
        dq_parts, dk_parts, dv_parts = [], [], []
        for hp in range(N_HEADS // 2):
            ls = slice(hp * LANES, (hp + 1) * LANES)
            qp, dop, kp_, vp_ = q2[:, ls], do2[:, ls], k[:, ls], v_[:, ls]
            kk = jnp.concatenate([jnp.where(h0, kp_, 0), jnp.where(h1, kp_, 0)], axis=0)
            vv = jnp.concatenate([jnp.where(h0, vp_, 0), jnp.where(h1, vp_, 0)], axis=0)

            def per_head(t):
                a = jnp.broadcast_to(t[:, hp * LANES:hp * LANES + 1], (2 * BLK, BLK))
                b = jnp.broadcast_to(t[:, hp * LANES + HEAD_DIM:hp * LANES + HEAD_DIM + 1], (2 * BLK, BLK))
                return jnp.concatenate([a, b], axis=1)

            p = jnp.where(mask2, jnp.exp(_dot_nt(qp, kk) - per_head(l2)), 0.0)
            ds = (p * (_dot_nt(dop, vv) + per_head(c2))).astype(BF16)
            dv2 = _dot_tn(p.astype(BF16), dop)
            dk2 = _dot_tn(ds, qp)
            dq2 = _dot(ds, kk)
            dq_parts.append((dq2[:BLK] + carry[:, ls]).astype(BF16))
            carry[:, ls] = dq2[BLK:]
            dk_parts.append(jnp.where(h0, dk2[:BLK], dk2[BLK:]).astype(BF16))
            dv_parts.append(jnp.where(h0, dv2[:BLK], dv2[BLK:]).astype(BF16))
        _dl_write(dq_ref, jnp.concatenate(dq_parts, axis=1))
        _dl_write(dk_ref, jnp.concatenate(dk_parts, axis=1))
        _dl_write(dv_ref, jnp.concatenate(dv_parts, axis=1))

    cur = _dl_spec(d, ATTN_WIDTH, lambda n: n)
    nxt = _dl_spec(d, ATTN_WIDTH, lambda n: jnp.minimum(n + 1, nb - 1))
    view = lambda t: _dl_view(t, d)
    outs = _call(
        body, name=f"attn_bwd_g{gi}", grid=(d, nb),
        in_specs=[cur, nxt, cur, cur, cur, nxt, cur, nxt, cur, nxt], out_specs=[cur, cur, cur],
        out_shape=[_sds(view(qs).shape, BF16)] * 3, scratch=[pltpu.VMEM((BLK, ATTN_WIDTH), F32)],
        sem=("parallel", "arbitrary"), rider=rider)(view(qs), view(qs), view(ks), view(v), view(do), view(do),
                                                    view(l), view(l), view(c), view(c))
    return [t.reshape(L, ATTN_WIDTH) for t in outs[:3]], list(outs[3:])


def _qkv_post(z, dqkv, du, dgates, gq_t, gk_t):
    L = z.shape[0]
    qkv_w = N_GROUPS * ATTN_WIDTH

    def body(zq_ref, zk_ref, gq_ref, gk_ref, hs_ref, pt1_ref, pt2_ref, du_ref, dgates_ref, *rest):
        dl_refs, (dz_ref, dgq_ref, dgk_ref) = rest[:9], rest[9:]

        @pl.when(pl.program_id(0) == 0)
        def _():
            dgq_ref[...] = jnp.zeros_like(dgq_ref)
            dgk_ref[...] = jnp.zeros_like(dgk_ref)

        hs = hs_ref[...]
        pts = (None, pt1_ref[...], pt2_ref[...])

        def rows8(t):
            return jnp.sum(t.reshape(DL_TILE // SUBLANES, SUBLANES, ATTN_WIDTH), axis=0)

        def norm_bwd(x, gain, dn):
            r = lax.rsqrt(_head_sum(x * x, hs) * (1.0 / HEAD_DIM) + EPS)
            xh = x * r
            dh = dn * gain
            return r * (dh - xh * (_head_sum(dh * xh, hs) * (1.0 / HEAD_DIM))), rows8(dn * xh)

        for g in range(N_GROUPS):
            tok = [t[...].astype(F32) if pts[g] is None else _dot(pts[g], t[...]) for t in dl_refs[3 * g:3 * g + 3]]
            cols = slice(g * ATTN_WIDTH, (g + 1) * ATTN_WIDTH)
            dq, pq = norm_bwd(zq_ref[:, cols].astype(F32), gq_ref[...] * SCALE, tok[0])
            dk, pk_ = norm_bwd(zk_ref[:, cols].astype(F32), gk_ref[...], tok[1])
            dgq_ref[...] += pq * SCALE
            dgk_ref[...] += pk_
            dz_ref[:, cols] = dq.astype(BF16)
            dz_ref[:, qkv_w + g * ATTN_WIDTH:qkv_w + (g + 1) * ATTN_WIDTH] = dk.astype(BF16)
            dz_ref[:, 2 * qkv_w + g * ATTN_WIDTH:2 * qkv_w + (g + 1) * ATTN_WIDTH] = tok[2].astype(BF16)
        dz_ref[:, 3 * qkv_w:3 * qkv_w + SSM_WIDTH] = du_ref[...]
        dz_ref[:, 3 * qkv_w + SSM_WIDTH:] = dgates_ref[...]

    tile = lambda w: pl.BlockSpec((DL_TILE, w), lambda i: (i, 0))
    mat = pl.BlockSpec((DL_TILE, DL_TILE), lambda i: (0, 0))
    vec = pl.BlockSpec((1, ATTN_WIDTH), lambda i: (0, 0))
    acc = pl.BlockSpec((SUBLANES, ATTN_WIDTH), lambda i: (0, 0))
    flat = [t for grp in dqkv for t in grp]
    return _call(
        body, name="qkv_post", grid=(L // DL_TILE,),
        in_specs=[tile(qkv_w), pl.BlockSpec((DL_TILE, qkv_w), lambda i: (i, 1)), vec, vec, mat, mat, mat,
                  tile(SSM_WIDTH), tile(2 * D_MODEL)] + [tile(ATTN_WIDTH)] * 9,
        out_specs=[tile(IN_COLS), acc, acc],
        out_shape=[_sds((L, IN_COLS), BF16), _sds((SUBLANES, ATTN_WIDTH), F32), _sds((SUBLANES, ATTN_WIDTH), F32)],
        sem=("arbitrary",))(z, z, gq_t, gk_t, _head_sum_matrix(), _perm_matrix(ATTN_PATTERN[1][1]).T,
                            _perm_matrix(ATTN_PATTERN[2][1]).T, du, dgates, *flat)


def _scan_rev_grad(sre, sim, rre, rim, are_ref, aim_ref, k0, init, seed_re, seed_im):
    ar = [_bcast(are_ref, k0 + kk) for kk in range(SCAN_GROUP)]
    ai = [-_bcast(aim_ref, k0 + kk) for kk in range(SCAN_GROUP)]

    def update(i, xprev, carry):
        out = []
        for kk in range(SCAN_GROUP):
            k = k0 + kk
            lr, li, dr, di = carry[4 * kk:4 * kk + 4]
            nr = ar[kk] * lr - ai[kk] * li + rre[k, _rows(i), :]
            ni = ar[kk] * li + ai[kk] * lr + rim[k, _rows(i), :]
            rre[k, _rows(i), :] = nr
            rim[k, _rows(i), :] = ni
            xr, xi = xprev(k)
            out += [nr, ni, dr + xr * nr + xi * ni, di + xr * ni - xi * nr]
        return tuple(out)

    def step(t, carry):
        i = SSM_TC - 1 - t
        return update(i, lambda k: (sre[k, _rows(i - 1), :], sim[k, _rows(i - 1), :]), carry)

    zero = jnp.zeros((SSM_SUB, LANES), F32)
    flat = []
    for re, im in init:
        flat += [re, im, zero, zero]
    res = lax.fori_loop(0, (SSM_TC - 1) // 2, lambda t, c: step(2 * t + 1, step(2 * t, c)), tuple(flat))
    res = step(SSM_TC - 2, res)
    res = update(0, lambda k: (seed_re[k], seed_im[k]), res)
    return [(res[4 * kk + 2], res[4 * kk + 3]) for kk in range(SCAN_GROUP)]


def _ssm_bwd(z, gy, pk, dskip, sd_re, sd_im, rider=None):
    L = z.shape[0]
    nb = L // SSM_TB
    ucol = (3 * N_GROUPS * ATTN_WIDTH) // SSM_WIDTH
    nwin = N_PAIR // PAIRS_PER_WIN

    def body(u_ref, gy_ref, are_ref, aim_ref, a64re_ref, a64im_ref, bwre_ref, bwim_ref, cwre_ref, cwim_ref, d_ref,
             sdre_ref, sdim_ref,
             du_ref, dare_ref, daim_ref, dbre_ref, dbim_ref, dcre_ref, dcim_ref, dd_ref,
             sre, sim, rre, rim, carry_re, carry_im, ends_re, ends_im, seed_re, seed_im):
        @pl.when(pl.program_id(0) == 0)
        def _():
            carry_re[...] = jnp.zeros_like(carry_re)
            carry_im[...] = jnp.zeros_like(carry_im)
            for ref in (dare_ref, daim_ref, dbre_ref, dbim_ref, dcre_ref, dcim_ref, dd_ref):
                ref[...] = jnp.zeros_like(ref)

        u = u_ref[...]
        gyv = gy_ref[...]
        gyb = gyv.astype(BF16)
        _ssm_fill(u, bwre_ref, bwim_ref, sre, sim)
        for k2 in range(N_PAIR):
            gw = gyb[:, _win_cols(k2)]
            _to_slabs(rre, k2, _dot_nt(gw, cwre_ref[k2]))
            _to_slabs(rim, k2, -_dot_nt(gw, cwim_ref[k2]))
        zero = jnp.zeros((SSM_SUB, LANES), F32)
        for k0 in range(0, N_SLAB, SCAN_GROUP):
            grp = range(k0, k0 + SCAN_GROUP)
            _scan(sre, sim, are_ref, aim_ref, k0, [(sdre_ref[k], sdim_ref[k]) for k in grp],
                  reverse=False, store=True)
            ends = _scan(rre, rim, are_ref, aim_ref, k0, [(zero, zero)] * SCAN_GROUP, reverse=True, store=False,
                         sign=-1.0)
            for kk, k in enumerate(grp):
                ends_re[k] = ends[kk][0]
                ends_im[k] = ends[kk][1]
            for k in grp:
                _ssm_seeds(ends_re, ends_im, a64re_ref, a64im_ref, carry_re, carry_im, seed_re, seed_im, k,
                           reverse=True, sign=-1.0)
            das = _scan_rev_grad(sre, sim, rre, rim, are_ref, aim_ref, k0,
                                 [(seed_re[k], seed_im[k]) for k in grp], sdre_ref, sdim_ref)
            for kk, k in enumerate(grp):
                dare_ref[k] += das[kk][0]
                daim_ref[k] += das[kk][1]
        for w in range(nwin):
            cols = _win_cols(w * PAIRS_PER_WIN)
            uw = u[:, cols]
            gw = gyb[:, cols]
            acc = gyv[:, cols] * d_ref[:, cols]
            for kk in range(PAIRS_PER_WIN):
                k2 = w * PAIRS_PER_WIN + kk
                lr = _pair_rows(rre, k2)
                li = _pair_rows(rim, k2)
                acc += _dot_nt(lr, bwre_ref[k2]) + _dot_nt(li, bwim_ref[k2])
                dbre_ref[k2] += _dot_tn(uw, lr)
                dbim_ref[k2] += _dot_tn(uw, li)
                dcre_ref[k2] += _dot_tn(_pair_rows(sre, k2), gw)
                dcim_ref[k2] -= _dot_tn(_pair_rows(sim, k2), gw)
            du_ref[:, cols] = acc.astype(BF16)
        dd_ref[...] += jnp.sum((gyv * u.astype(F32)).reshape(SSM_TB // SUBLANES, SUBLANES, SSM_WIDTH), axis=0)

    c2, c3 = _ssm_specs_consts()
    rev = lambda b: nb - 1 - b
    seed_spec = pl.BlockSpec((None, N_SLAB, SSM_SUB, LANES), lambda b: (rev(b), 0, 0, 0))
    tile_out = pl.BlockSpec((N_SLAB, SSM_SUB, LANES), lambda b: (0, 0, 0))
    small = pltpu.VMEM((N_SLAB, LANES), F32)
    tile = pltpu.VMEM((N_SLAB, SSM_SUB, LANES), F32)
    return _call(
        body, name="ssm_bwd", grid=(nb,),
        in_specs=[pl.BlockSpec((SSM_TB, SSM_WIDTH), lambda b: (rev(b), ucol)),
                  pl.BlockSpec((SSM_TB, SSM_WIDTH), lambda b: (rev(b), 0)),
                  c2, c2, c2, c2, c3, c3, c3, c3, pl.BlockSpec((1, SSM_WIDTH), lambda b: (0, 0)),
                  seed_spec, seed_spec],
        out_specs=[pl.BlockSpec((SSM_TB, SSM_WIDTH), lambda b: (rev(b), 0)), tile_out, tile_out, c3, c3, c3, c3,
                   pl.BlockSpec((SUBLANES, SSM_WIDTH), lambda b: (0, 0))],
        out_shape=[_sds((L, SSM_WIDTH), BF16), _sds((N_SLAB, SSM_SUB, LANES), F32),
                   _sds((N_SLAB, SSM_SUB, LANES), F32)] + [_sds((N_PAIR, SSM_WIN, SSM_WIN), F32)] * 4
                  + [_sds((SUBLANES, SSM_WIDTH), F32)],
        scratch=_ssm_scratch() + _ssm_scratch() + [small, small, tile, tile, tile, tile],
        sem=("arbitrary",), rider=rider)(z, gy, pk["a_re"], pk["a_im"], pk["a64_re"], pk["a64_im"],
                            pk["bw_re"].astype(BF16), pk["bw_im"].astype(BF16),
                            pk["cw_re"].astype(BF16), pk["cw_im"].astype(BF16), dskip, sd_re, sd_im)


def _in_proj_bwd(dz, w, x, g, dres, rider=None):
    L = x.shape[0]
    ns = w.shape[2]
    tn = ns
    nj = ns // tn
    nt = N_CHIPS * nj
    TM = TM_PROJ

    def body(dz_ref, w_ref, x_ref, g_ref, dres_ref, dx_ref, dg_ref, acc, dgacc):
        i, j = pl.program_id(0), pl.program_id(1)

        @pl.when(j == 0)
        def _():
            acc[...] = jnp.zeros_like(acc)

        @pl.when(jnp.logical_and(i == 0, j == 0))
        def _():
            dgacc[...] = jnp.zeros_like(dgacc)

        acc[...] += _dot_nt(dz_ref[...], w_ref[...])

        @pl.when(j == nt - 1)
        def _():
            dx, dg = _rms_bwd(x_ref[...], g_ref[...], acc[...])
            dx_ref[...] = dres_ref[...] + dx
            dgacc[...] += dg

        @pl.when(jnp.logical_and(i == pl.num_programs(0) - 1, j == nt - 1))
        def _():
            dg_ref[...] = jnp.sum(dgacc[...], axis=0, keepdims=True)

    tok = pl.BlockSpec((TM, D_MODEL), lambda i, j: (i, 0))
    vec = pl.BlockSpec((1, D_MODEL), lambda i, j: (0, 0))
    return _call(
        body, name="in_proj_bwd", grid=(L // TM, nt),
        in_specs=[pl.BlockSpec((TM, tn), lambda i, j: (i, j)),
                  pl.BlockSpec((None, D_MODEL, tn), lambda i, j: (j // nj, 0, j % nj)), tok, vec, tok],
        out_specs=[tok, vec],
        out_shape=[_sds((L, D_MODEL), F32), _sds((1, D_MODEL), F32)],
        scratch=[pltpu.VMEM((TM, D_MODEL), F32), pltpu.VMEM((SUBLANES, D_MODEL), F32)],
        sem=("arbitrary", "arbitrary"), rider=rider)(dz, w, x, g, dres)


SSM_PARAMS = ("lambda_re", "lambda_im", "log_dt", "b_re", "b_im", "c_re", "c_im")
EARLY = ("w_ffn_gate", "w_ffn_up", "w_ffn_down")
LATE = ("w_in", "w_attn_proj", "w_glu_a", "w_glu_b", "w_out")


def _layer_bwd(dx2, sv, p, pending, owned, l, idx):
    chip_idx, c_idx = idx
    g = {}
    owned = dict(owned)

    def settle(name, partial, arrived, layer):
        owned[name] = _sum_owner(partial, arrived, owned[name], layer, chip_idx, c_idx)

    dgate, dup = _ffn_bwd_act(dx2, sv["gate"], sv["up"], p["w_ffn_down"])
    outs = _ffn_bwd_in(dx2, sv["x1"], p["g_ffn"], dgate, dup, p["w_ffn_gate"], p["w_ffn_up"],
                       _scatter_rider([pending[n] for n in LATE[1:]]) if pending else None)
    dx1, g["g_ffn"] = outs[:2]
    for n, t in zip(LATE[1:], outs[2:]):
        settle(n, pending[n], t, l + 1)
    g["w_ffn_gate"] = _wgrad_ff_rows(dgate, sv["h2"], "wgrad_ffn_gate")
    g["w_ffn_up"] = _wgrad_ff_rows(dup, sv["h2"], "wgrad_ffn_up")
    g["w_ffn_down"] = _wgrad_ff_rows(sv["act"], dx2, "wgrad_ffn_down")

    outs = _mix_bwd(dx1, sv["z"], sv["aout"], sv["sa"], sv["sb"], sv["ypre"], p["w_attn_proj"], p["w_glu_a"],
                    p["w_glu_b"], p["w_out"], _swap_rider([g[n] for n in EARLY]))
    dgates, da, gy, daout, dsa, dsb = outs[:6]
    early = [_add_half(g[n], s, c_idx) for n, s in zip(EARLY, outs[6:])]
    g["w_out"] = _wgrad_full(sv["mix"], dx1, "wgrad_out").reshape(N_CHIPS, D_MODEL // N_CHIPS, D_MODEL)
    g["w_attn_proj"] = _wgrad_cols(sv["a"], daout, "wgrad_attn_proj")
    g["w_glu_a"] = _wgrad_cols(sv["yact"], dsa, "wgrad_glu_a")
    g["w_glu_b"] = _wgrad_cols(sv["yact"], dsb, "wgrad_glu_b")

    outs = _ssm_bwd(sv["z"], gy, sv["pk"], p["d_skip"], sv["sd_re"], sv["sd_im"],
                    _scatter_rider([pending[LATE[0]]]) if pending else None)
    du, da_re, da_im, dbw_re, dbw_im, dcw_re, dcw_im, dd = outs[:8]
    if pending:
        settle(LATE[0], pending[LATE[0]], outs[8], l + 1)
    g["d_skip"] = jnp.sum(dd, axis=0, keepdims=True)
    g["ssm_pack_ct"] = dict(a_re=da_re, a_im=da_im, bw_re=dbw_re, bw_im=dbw_im, cw_re=dcw_re, cw_im=dcw_im)

    dos, cs = _combine_bwd(da, sv["os"], sv["ls"])
    dqkv = []
    for gi in range(N_GROUPS):
        grads, arrived = _attn_bwd(*sv["qkv"][gi], dos[gi], sv["ls"][gi], cs[gi], gi, _scatter_rider([early[gi]]))
        settle(EARLY[gi], early[gi], arrived[0], l)
        dqkv.append(grads)
    dz, gq8, gk8 = _qkv_post(sv["z"], dqkv, du, dgates, jnp.tile(p["g_q"], (1, N_HEADS)),
                             jnp.tile(p["g_k"], (1, N_HEADS)))
    g["g_q"] = jnp.sum(gq8.reshape(SUBLANES * N_HEADS, HEAD_DIM), axis=0, keepdims=True)
    g["g_k"] = jnp.sum(gk8.reshape(SUBLANES * N_HEADS, HEAD_DIM), axis=0, keepdims=True)
    g["w_in"] = _wgrad_cols(sv["h"], dz, "wgrad_in")
    outs = _in_proj_bwd(dz, p["w_in"], sv["x"], p["g_mix"], dx1, _swap_rider([g[n] for n in LATE]))
    dx, g["g_mix"] = outs[:2]
    late = {n: _add_half(g[n], s, c_idx) for n, s in zip(LATE, outs[2:])}
    return dx, g, late, owned


def _place():
    x, y, c = lax.axis_index("x"), lax.axis_index("y"), lax.axis_index("c")
    others = [(1 - x, y), (x, 1 - y), (1 - x, 1 - y)]
    return x, y, c, others


def _half(ref, hc):
    rows = ref.shape[-2] // 2
    idx = (slice(None),) * (len(ref.shape) - 2) + (pl.ds(hc * rows, rows), slice(None))
    return ref.at[idx]


def _comm_call(body, name, ins, out_shapes, n_remote, aliases=None):
    scratch = [pltpu.SemaphoreType.DMA((n_remote,)), pltpu.SemaphoreType.DMA((n_remote,))]
    return pl.pallas_call(
        body, name=name, in_specs=[ANY] * len(ins), out_specs=[ANY] * len(out_shapes), out_shape=out_shapes,
        scratch_shapes=scratch, input_output_aliases=aliases or {})(*ins)


def _cast_place(w, l, chip_idx):
    _, R, C = w.shape
    tr = R // 2

    def body(me_ref, w_ref, o_ref):
        o_ref[...] = w_ref[...].astype(BF16)

    return _call(body, name=f"cast_place_l{l}", grid=(R // tr,), prefetch=1,
                 in_specs=[pl.BlockSpec((None, tr, C), lambda i, me_ref: (l, i, 0))],
                 out_specs=pl.BlockSpec((None, tr, C), lambda i, me_ref: (me_ref[0], i, 0)),
                 out_shape=_sds((N_CHIPS, R, C), BF16), sem=("arbitrary",))(chip_idx, w)


def _in_place_rider(bufs, pairs, per_buf=3):
    n = len(bufs)

    def copies(outs, send, recv, side):
        return [pltpu.make_async_remote_copy(src_ref=pair[side][0], dst_ref=pair[side][0], send_sem=send.at[k],
                                             recv_sem=recv.at[k], device_id=pair[side][1], device_id_type=MESH)
                for k, pair in enumerate(pairs(outs))]

    def start(ins, outs, send, recv):
        for cp in copies(outs, send, recv, 0):
            cp.start()

    def wait(ins, outs, send, recv):
        for cp in copies(outs, send, recv, 1):
            cp.wait_recv()
        for cp in copies(outs, send, recv, 0):
            cp.wait_send()

    return Rider(list(bufs), [_sds(b.shape, b.dtype) for b in bufs], per_buf * n, start, wait,
                 {a: a for a in range(n)})


def _gather_ici_rider(bufs):
    def pairs(outs):
        x, y, c, others = _place()
        return [((_half(o.at[2 * x + y], c), (cx, cy, c)), (_half(o.at[2 * cx + cy], c), (cx, cy, c)))
                for o in outs for cx, cy in others]
    return _in_place_rider(bufs, pairs)


def _gather_d2d_rider(bufs):
    def pairs(outs):
        x, y, c, others = _place()
        sib = (x, y, 1 - c)
        return [((_half(o.at[2 * cx + cy], c), sib), (_half(o.at[2 * cx + cy], 1 - c), sib))
                for o in outs for cx, cy in others]
    return _in_place_rider(bufs, pairs)


def _swap_rider(gs):
    n = len(gs)

    def copies(ins, outs, send, recv):
        x, y, c, _ = _place()
        return [pltpu.make_async_remote_copy(src_ref=_half(ins[a], 1 - c), dst_ref=outs[a], send_sem=send.at[a],
                                             recv_sem=recv.at[a], device_id=(x, y, 1 - c), device_id_type=MESH)
                for a in range(n)]

    def start(ins, outs, send, recv):
        for cp in copies(ins, outs, send, recv):
            cp.start()

    def wait(ins, outs, send, recv):
        for cp in copies(ins, outs, send, recv):
            cp.wait()

    outs = [_sds((g.shape[0], g.shape[1] // 2, g.shape[2]), g.dtype) for g in gs]
    return Rider(list(gs), outs, n, start, wait, {})


def _scatter_rider(ss):
    n = len(ss)

    def copies(ins, outs, send, recv):
        x, y, c, others = _place()
        return [pltpu.make_async_remote_copy(
            src_ref=ins[a].at[2 * cx + cy], dst_ref=outs[a].at[j], send_sem=send.at[3 * a + j],
            recv_sem=recv.at[3 * a + j], device_id=(cx, cy, c), device_id_type=MESH)
            for a in range(n) for j, (cx, cy) in enumerate(others)]

    def start(ins, outs, send, recv):
        for cp in copies(ins, outs, send, recv):
            cp.start()

    def wait(ins, outs, send, recv):
        for cp in copies(ins, outs, send, recv):
            cp.wait()

    outs = [_sds((N_CHIPS - 1,) + s.shape[1:], s.dtype) for s in ss]
    return Rider(list(ss), outs, 3 * n, start, wait, {})


def _run_rider(rider, name):
    n_in = len(rider.ins)

    def body(*refs):
        ins, outs = refs[:n_in], refs[n_in:n_in + len(rider.out_shapes)]
        send, recv = refs[n_in + len(rider.out_shapes):]
        rider.start(ins, outs, send, recv)
        rider.wait(ins, outs, send, recv)

    return _comm_call(body, name, rider.ins, rider.out_shapes, rider.n_sem, aliases=rider.aliases)


def _join_rider(bufs):
    def pairs(outs):
        x, y, c, _ = _place()
        sib = (x, y, 1 - c)
        return [((_half(o, c), sib), (_half(o, 1 - c), sib)) for o in outs]
    return _in_place_rider(bufs, pairs, per_buf=1)


def _place_small(v, dev_idx):
    rows, n = v.shape

    def body(idx_ref, v_ref, o_ref):
        o_ref[...] = v_ref[...]

    return _call(body, name="place_small", grid=(1,), prefetch=1,
                 in_specs=[pl.BlockSpec((rows, n), lambda i, idx_ref: (0, 0))],
                 out_specs=pl.BlockSpec((None, rows, n), lambda i, idx_ref: (idx_ref[0], 0, 0)),
                 out_shape=_sds((8, rows, n), v.dtype), sem=("arbitrary",))(dev_idx, v)


def _small_ici_rider(buf):
    def pairs(outs):
        x, y, c, others = _place()
        peers = [(x, y, 1 - c)] + [(cx, cy, c) for cx, cy in others]
        return [((outs[0].at[4 * x + 2 * y + c], peer), (outs[0].at[4 * peer[0] + 2 * peer[1] + peer[2]], peer))
                for peer in peers]
    return _in_place_rider([buf], pairs, per_buf=4)


def _small_d2d_rider(buf):
    def pairs(outs):
        x, y, c, others = _place()
        sib = (x, y, 1 - c)
        return [((outs[0].at[4 * cx + 2 * cy + c], sib), (outs[0].at[4 * cx + 2 * cy + 1 - c], sib))
                for cx, cy in others]
    return _in_place_rider([buf], pairs)


def _add_half(g, p, c):
    _, R, C = g.shape
    half = R // 2

    def body(c_ref, g_ref, p_ref, o_ref):
        o_ref[...] = g_ref[...] + p_ref[...]

    blk = (None, half, C)
    return _call(body, name="add_half", grid=(N_CHIPS,), prefetch=1,
                 in_specs=[pl.BlockSpec(blk, lambda s, c_ref: (s, c_ref[0], 0)),
                           pl.BlockSpec(blk, lambda s, c_ref: (s, 0, 0))],
                 out_specs=pl.BlockSpec(blk, lambda s, c_ref: (s, 0, 0)),
                 out_shape=_sds((N_CHIPS, half, C), F32), sem=("arbitrary",))(c, g, p)


def _sum_owner(s, q, buf, l, me, c):
    _, half, C = s.shape
    tr = half // 2

    def body(me_ref, c_ref, s_ref, q0, q1, q2, buf_ref, o_ref):
        o_ref[...] = ((s_ref[...] + q0[...]) + q1[...]) + q2[...]

    blk = (None, tr, C)
    qspec = lambda j: pl.BlockSpec(blk, lambda i, me_ref, c_ref: (j, i, 0))
    return _call(body, name=f"sum_owner_l{l}", grid=(half // tr,), prefetch=2,
                 in_specs=[pl.BlockSpec(blk, lambda i, me_ref, c_ref: (me_ref[0], i, 0)),
                           qspec(0), qspec(1), qspec(2), ANY],
                 out_specs=pl.BlockSpec(blk, lambda i, me_ref, c_ref: (l, 2 * c_ref[0] + i, 0)),
                 out_shape=_sds(buf.shape, F32), sem=("arbitrary",), aliases={6: 0})(me, c, s, q, q, q, buf)


def _adamw_math(w, g, m, v):
    m = ADAM_B1 * m + (1.0 - ADAM_B1) * g
    v = ADAM_B2 * v + (1.0 - ADAM_B2) * (g * g)
    m_hat = m / (1.0 - ADAM_B1 ** ADAM_STEP)
    v_hat = v / (1.0 - ADAM_B2 ** ADAM_STEP)
    delta = -ADAM_LR * (m_hat / (jnp.sqrt(v_hat) + ADAM_EPS) + ADAM_WD * w)
    return delta, m, v


def _adamw(w, g, m, v, rider=None):
    depth, R, C = w.shape
    tr = max(t for t in range(SUBLANES, R + 1, SUBLANES) if R % t == 0 and t * C * 4 <= 2 * 1024 * 1024)

    def body(w_ref, g_ref, m_ref, v_ref, d_ref, nm_ref, nv_ref):
        d, nm, nv = _adamw_math(w_ref[...], g_ref[...], m_ref[...], v_ref[...])
        d_ref[...] = d
        nm_ref[...] = nm
        nv_ref[...] = nv

    spec = pl.BlockSpec((None, tr, C), lambda l, i: (l, i, 0))
    return _call(body, name="adamw", grid=(depth, R // tr), in_specs=[spec] * 4, out_specs=[spec] * 3,
                 out_shape=[_sds(w.shape, F32)] * 3, sem=("parallel", "parallel"), rider=rider)(w, g, m, v)


def _small_update(gathered, w, m, v):
    _, rows, n = gathered.shape
    tr = rows // 7

    def body(ga_ref, w_ref, m_ref, v_ref, g_ref, d_ref, nm_ref, nv_ref):
        g = ga_ref[0]
        for k in range(1, 8):
            g = g + ga_ref[k]
        d, nm, nv = _adamw_math(w_ref[...], g, m_ref[...], v_ref[...])
        g_ref[...] = g
        d_ref[...] = d
        nm_ref[...] = nm
        nv_ref[...] = nv

    spec = pl.BlockSpec((tr, n), lambda i: (i, 0))
    return _call(body, name="small_update", grid=(rows // tr,),
                 in_specs=[pl.BlockSpec((8, tr, n), lambda i: (0, i, 0)), spec, spec, spec], out_specs=[spec] * 4,
                 out_shape=[_sds((rows, n), F32)] * 4, sem=("parallel",))(gathered, w, m, v)


WEIGHTS = ("g_mix", "w_in", "g_q", "g_k", "w_attn_proj", "lambda_re", "lambda_im", "log_dt", "b_re", "b_im",
           "c_re", "c_im", "d_skip", "w_glu_a", "w_glu_b", "w_out", "g_ffn", "w_ffn_gate", "w_ffn_up", "w_ffn_down")
BIG = ("w_in", "w_attn_proj", "w_glu_a", "w_glu_b", "w_out", "w_ffn_gate", "w_ffn_up", "w_ffn_down")
FLIPPED = ("w_ffn_gate", "w_ffn_up")
SMALL = tuple(n for n in WEIGHTS if n not in BIG)
ROW_VECTORS = ("g_mix", "g_q", "g_k", "d_skip", "g_ffn")
PACK_QUANTUM = LANES * SUBLANES * 7


def _pack_small(parts, extra):
    flat = jnp.concatenate([parts[n].reshape(-1).astype(F32) for n in SMALL] + [extra.reshape(-1)])
    pad = -flat.shape[0] % PACK_QUANTUM
    return jnp.pad(flat, (0, pad)).reshape(-1, LANES)


def _unpack_small(packed, like):
    flat = packed.reshape(-1)
    out, at = {}, 0
    for n in SMALL:
        size = math.prod(like[n].shape)
        out[n] = flat[at:at + size].reshape(like[n].shape)
        at += size
    return out, flat[at]


def kernel(x, g_mix, w_in, g_q, g_k, w_attn_proj, lambda_re, lambda_im, log_dt, b_re, b_im, c_re, c_im, d_skip, w_glu_a, w_glu_b, w_out, g_ffn, w_ffn_gate, w_ffn_up, w_ffn_down, loss_target, m_g_mix, m_w_in, m_g_q, m_g_k, m_w_attn_proj, m_lambda_re, m_lambda_im, m_log_dt, m_b_re, m_b_im, m_c_re, m_c_im, m_d_skip, m_w_glu_a, m_w_glu_b, m_w_out, m_g_ffn, m_w_ffn_gate, m_w_ffn_up, m_w_ffn_down, v_g_mix, v_w_in, v_g_q, v_g_k, v_w_attn_proj, v_lambda_re, v_lambda_im, v_log_dt, v_b_re, v_b_im, v_c_re, v_c_im, v_d_skip, v_w_glu_a, v_w_glu_b, v_w_out, v_g_ffn, v_w_ffn_gate, v_w_ffn_up, v_w_ffn_down):
    given = dict(locals())
    flip = lambda n, a: jnp.swapaxes(a, 1, 2) if n in FLIPPED else a
    W = {n: flip(n, given[n]) for n in WEIGHTS}
    M = {n: flip(n, given["m_" + n]) for n in WEIGHTS}
    V = {n: flip(n, given["v_" + n]) for n in WEIGHTS}
    depth = g_mix.shape[0]
    xl = x.reshape(x.shape[-2:])
    target = loss_target.reshape(loss_target.shape[-2:])
    c_idx = lax.axis_index("c").astype(jnp.int32).reshape(1)
    chip_idx = (2 * lax.axis_index("x") + lax.axis_index("y")).astype(jnp.int32).reshape(1)

    place = lambda l: [_cast_place(W[n], l, chip_idx) for n in BIG]
    bufs = place(0)
    w_in = _run_rider(_gather_d2d_rider(_run_rider(_gather_ici_rider(bufs[:1]), "gather_ici")), "gather_d2d")[0]
    rest, stage = bufs[1:], "ici"
    params, saved, h = [], [], xl
    for l in range(depth):
        p = {"w_in": w_in}
        for n in SMALL:
            p[n] = W[n][l][None] if n in ROW_VECTORS else W[n][l]
        h, sv, p, nxt = _layer_fwd(h, p, rest, stage, place(l + 1) if l + 1 < depth else None)
        params.append(p)
        saved.append(sv)
        if nxt:
            (w_in, rest), stage = nxt, "d2d"
    dx, loss_part = _loss_head(h, target)

    owned = {n: lax.empty(W[n].shape, F32) for n in BIG}
    small_grads = [None] * depth
    pending = None
    for l in reversed(range(depth)):
        dx, small_grads[l], pending, owned = _layer_bwd(dx, saved[l], params[l], pending, owned, l,
                                                        (chip_idx, c_idx))

    ct = {k: jnp.stack([small_grads[l]["ssm_pack_ct"][k] for l in range(depth)])
          for k in small_grads[0]["ssm_pack_ct"]}
    ct["a_re"], ct["a_im"] = jnp.sum(ct["a_re"], axis=2), jnp.sum(ct["a_im"], axis=2)
    ct["a64_re"] = ct["a64_im"] = jnp.zeros_like(ct["a_re"])
    _, pull = jax.vjp(jax.vmap(_ssm_pack), *[W[n] for n in SSM_PARAMS])
    stacked = dict(zip(SSM_PARAMS, pull(ct)))
    for n in SMALL:
        if n not in stacked:
            stacked[n] = jnp.stack([small_grads[l][n] for l in range(depth)])
    zero = jnp.zeros((1,), F32)
    dev_idx = (4 * lax.axis_index("x") + 2 * lax.axis_index("y") + lax.axis_index("c")).astype(jnp.int32).reshape(1)
    gathered = _place_small(_pack_small(stacked, loss_part), dev_idx)
    outs = _run_rider(_join_riders(_scatter_rider([pending[n] for n in LATE]), _small_ici_rider(gathered)),
                      "scatter_to_owners")
    for n, q in zip(LATE, outs[:len(LATE)]):
        owned[n] = _sum_owner(pending[n], q, owned[n], 0, chip_idx, c_idx)
    outs = _run_rider(_join_riders(_join_rider([owned[n] for n in BIG]), _small_d2d_rider(outs[len(LATE)])),
                      "join_halves")
    reduced, gathered = dict(zip(BIG, outs[:len(BIG)])), outs[len(BIG)]
    grads, delta, new_m, new_v = {}, {}, {}, {}
    for n in BIG:
        outs = (reduced[n], *_adamw(W[n], reduced[n], M[n], V[n]))
        grads[n], delta[n], new_m[n], new_v[n] = [flip(n, t) for t in outs]
    gs, ds, nms, nvs = _small_update(gathered, _pack_small(W, zero), _pack_small(M, zero), _pack_small(V, zero))
    sg, loss = _unpack_small(gs, W)
    sd, _ = _unpack_small(ds, W)
    sm, _ = _unpack_small(nms, W)
    sv_, _ = _unpack_small(nvs, W)
    for n in SMALL:
        grads[n], delta[n], new_m[n], new_v[n] = sg[n], sd[n], sm[n], sv_[n]

    return (loss, dx.reshape(x.shape), *[grads[n] for n in WEIGHTS], *[delta[n] for n in WEIGHTS],
            *[new_m[n] for n in WEIGHTS], *[new_v[n] for n in WEIGHTS])
```

```python
import collections
import functools
import math

import jax
import jax.numpy as jnp
from jax import lax
from jax.experimental import pallas as pl
from jax.experimental.pallas import tpu as pltpu

F32 = jnp.float32
BF16 = jnp.bfloat16

D_MODEL = 1024
DEPTH = 4
HEAD_DIM = 64
N_HEADS = 8
ATTN_WIDTH = N_HEADS * HEAD_DIM
ATTN_PATTERN = ((128, 1), (512, 4), (2048, 16))
N_GROUPS = len(ATTN_PATTERN)
BLK = 128
SSM_WIDTH = 512
SSM_GROUP = 16
SSM_GROUPS = 32
SSM_STATE = 64
D_FF = 2816
IN_COLS = 7168
EPS = 1e-6
ADAM_LR, ADAM_B1, ADAM_B2, ADAM_EPS, ADAM_WD, ADAM_STEP = 0.001, 0.9, 0.999, 1e-08, 0.01, 10

N_CHIPS = 4
MESH = pl.DeviceIdType.MESH

LANES = 128
SUBLANES = 8
VMEM_LIMIT = 56 * 1024 * 1024

TM = 512
TM_PROJ = 1024
TL_WGRAD = 2048
TM_MIX = 512

SSM_TB = 512
SSM_TC = 64
SSM_SUB = SUBLANES
SSM_PITCH = 68
N_SLAB = SSM_GROUPS * SSM_STATE // LANES
SSM_WIN = 256
N_PAIR = N_SLAB // 2
PAIRS_PER_WIN = 4
SCAN_GROUP = 4


def _params(sem=None, collective=False):
    return pltpu.CompilerParams(dimension_semantics=sem, vmem_limit_bytes=VMEM_LIMIT)


ANY = pl.BlockSpec(memory_space=pl.ANY)

Rider = collections.namedtuple("Rider", "ins out_shapes n_sem start wait aliases")


class _SemWindow:
    def __init__(self, ref, offset):
        self.ref, self.offset = ref, offset

    @property
    def at(self):
        return self

    def __getitem__(self, k):
        return self.ref.at[self.offset + k]


def _join_riders(*riders):
    riders = [r for r in riders if r is not None]
    if len(riders) <= 1:
        return riders[0] if riders else None

    def each(fn_name):
        def run(ins, outs, send, recv):
            i = o = s = 0
            for r in riders:
                getattr(r, fn_name)(ins[i:i + len(r.ins)], outs[o:o + len(r.out_shapes)],
                                    _SemWindow(send, s), _SemWindow(recv, s))
                i, o, s = i + len(r.ins), o + len(r.out_shapes), s + r.n_sem
        return run

    aliases, i, o = {}, 0, 0
    for r in riders:
        aliases.update({i + a: o + b for a, b in r.aliases.items()})
        i, o = i + len(r.ins), o + len(r.out_shapes)
    return Rider([t for r in riders for t in r.ins], [t for r in riders for t in r.out_shapes],
                 sum(r.n_sem for r in riders), each("start"), each("wait"), aliases)


def _with_rider(body, rider, grid, prefetch, n_in, n_out, n_scratch):
    n_rin, n_rout = len(rider.ins), len(rider.out_shapes)

    def hosted(*refs):
        pre, rest = refs[:prefetch], refs[prefetch:]
        ins, rin = rest[:n_in], rest[n_in:n_in + n_rin]
        o0 = n_in + n_rin
        outs, rout = rest[o0:o0 + n_out], rest[o0 + n_out:o0 + n_out + n_rout]
        s0 = o0 + n_out + n_rout
        scr, (send, recv) = rest[s0:s0 + n_scratch], rest[s0 + n_scratch:]
        first = functools.reduce(jnp.logical_and, [pl.program_id(k) == 0 for k in range(len(grid))])
        last = functools.reduce(jnp.logical_and, [pl.program_id(k) == grid[k] - 1 for k in range(len(grid))])

        @pl.when(first)
        def _():
            rider.start(rin, rout, send, recv)

        body(*pre, *ins, *outs, *scr)

        @pl.when(last)
        def _():
            rider.wait(rin, rout, send, recv)

    return hosted


def _call(body, *, name, grid, in_specs, out_specs, out_shape, scratch=(), sem=None, aliases=None,
          prefetch=0, rider=None):
    if rider is not None:
        single = not isinstance(out_specs, (list, tuple))
        out_specs = [out_specs] if single else list(out_specs)
        out_shape = [out_shape] if single else list(out_shape)
        body = _with_rider(body, rider, grid, prefetch, len(in_specs), len(out_specs), len(scratch))
        aliases = dict(aliases or {})
        aliases.update({prefetch + len(in_specs) + k: len(out_specs) + v for k, v in rider.aliases.items()})
        in_specs = list(in_specs) + [ANY] * len(rider.ins)
        out_specs = out_specs + [ANY] * len(rider.out_shapes)
        out_shape = out_shape + list(rider.out_shapes)
        scratch = list(scratch) + [pltpu.SemaphoreType.DMA((rider.n_sem,)), pltpu.SemaphoreType.DMA((rider.n_sem,))]
        sem = ("arbitrary",) * len(grid)
        fn = _call(body, name=name + "_host", grid=grid, in_specs=in_specs, out_specs=out_specs, out_shape=out_shape,
                   scratch=scratch, sem=sem, aliases=aliases, prefetch=prefetch)
        return lambda *args: fn(*args, *rider.ins)
    kw = {}
    if aliases:
        kw["input_output_aliases"] = aliases
    if prefetch:
        gs = pltpu.PrefetchScalarGridSpec(num_scalar_prefetch=prefetch, grid=grid, in_specs=in_specs,
                                          out_specs=out_specs, scratch_shapes=list(scratch))
        return pl.pallas_call(body, name=name, grid_spec=gs, out_shape=out_shape,
                              compiler_params=_params(sem), **kw)
    return pl.pallas_call(body, name=name, grid=grid, in_specs=in_specs, out_specs=out_specs,
                          out_shape=out_shape, scratch_shapes=list(scratch),
                          compiler_params=_params(sem), **kw)


def _sds(shape, dtype):
    return jax.ShapeDtypeStruct(shape, dtype)


def _sigmoid(v):
    return 0.5 * jnp.tanh(0.5 * v) + 0.5


def _dot(a, b):
    return jnp.dot(a, b, preferred_element_type=F32)


def _dot_nt(a, b):
    return lax.dot_general(a, b, (((1,), (1,)), ((), ())), preferred_element_type=F32)


def _dot_tn(a, b):
    return lax.dot_general(a, b, (((0,), (0,)), ((), ())), preferred_element_type=F32)


def _in_proj_fwd(x, g, w, rider=None):
    L = x.shape[0]
    ns = w.shape[2]
    tn = ns
    nj = ns // tn
    TM = TM_PROJ

    def body(x_ref, g_ref, w_ref, z_ref, h_ref):
        @pl.when(pl.program_id(1) == 0)
        def _():
            xv = x_ref[...]
            r = lax.rsqrt(jnp.mean(xv * xv, axis=-1, keepdims=True) + EPS)
            h_ref[...] = (xv * r * g_ref[...]).astype(BF16)
        z_ref[...] = _dot(h_ref[...], w_ref[...]).astype(BF16)

    return _call(
        body, name="in_proj_fwd", grid=(L // TM, N_CHIPS * nj),
        in_specs=[pl.BlockSpec((TM, D_MODEL), lambda i, j: (i, 0)),
                  pl.BlockSpec((1, D_MODEL), lambda i, j: (0, 0)),
                  pl.BlockSpec((None, D_MODEL, tn), lambda i, j: (j // nj, 0, j % nj))],
        out_specs=[pl.BlockSpec((TM, tn), lambda i, j: (i, j)),
                   pl.BlockSpec((TM, D_MODEL), lambda i, j: (i, 0))],
        out_shape=[_sds((L, N_CHIPS * ns), BF16), _sds((L, D_MODEL), BF16)],
        sem=("parallel", "arbitrary"), rider=rider)(x, g, w)


DL_TILE = 512
SCALE = HEAD_DIM ** -0.5


def _perm_matrix(d):
    rho = jnp.arange(DL_TILE)
    src = rho // (DL_TILE // d) + d * (rho % (DL_TILE // d))
    return (src[:, None] == jnp.arange(DL_TILE)[None, :]).astype(BF16)


def _head_sum_matrix():
    h = jnp.arange(ATTN_WIDTH) // HEAD_DIM
    return (h[:, None] == h[None, :]).astype(BF16)


def _split(v):
    hi = v.astype(BF16)
    return hi, (v - hi.astype(F32)).astype(BF16)


def _head_sum(v, hs):
    vb = v.astype(BF16)
    half = ATTN_WIDTH // 2
    blk = hs[:half, :half]
    return jnp.concatenate([_dot(vb[:, :half], blk), _dot(vb[:, half:], blk)], axis=1)


def _permute(pm, v):
    hi, lo = _split(v)
    return _dot(pm, hi) + _dot(pm, lo)


def _dl_view(t, d):
    if d * BLK <= DL_TILE:
        return t
    return t.reshape(t.shape[0] // DL_TILE, d, DL_TILE // d, t.shape[1])


def _dl_spec(d, width, which):
    if d * BLK <= DL_TILE:
        per_tile = DL_TILE // (d * BLK)
        return pl.BlockSpec((BLK, width), lambda r, n: ((which(n) // per_tile) * (DL_TILE // BLK)
                                                       + r * per_tile + which(n) % per_tile, 0))
    tiles = d * BLK // DL_TILE
    return pl.BlockSpec((tiles, None, DL_TILE // d, width), lambda r, n: (which(n), r, 0, 0))


def _dl_read(ref):
    v = ref[...]
    return v if v.ndim == 2 else v.reshape(BLK, v.shape[-1])


def _dl_write(ref, v):
    ref[...] = v if len(ref.shape) == 2 else v.reshape(ref.shape)


def _qkv_prep(z, gq_t, gk_t, rider=None):
    L = z.shape[0]
    qkv_w = N_GROUPS * ATTN_WIDTH

    def body(zq_ref, zk_ref, zv_ref, gq_ref, gk_ref, hs_ref, p1_ref, p2_ref, *outs):
        hs = hs_ref[...]
        perms = (None, p1_ref[...], p2_ref[...])
        for g in range(N_GROUPS):
            cols = slice(g * ATTN_WIDTH, (g + 1) * ATTN_WIDTH)
            xq = zq_ref[:, cols].astype(F32)
            xk = zk_ref[:, cols].astype(F32)
            rq = lax.rsqrt(_head_sum(xq * xq, hs) * (1.0 / HEAD_DIM) + EPS)
            rk = lax.rsqrt(_head_sum(xk * xk, hs) * (1.0 / HEAD_DIM) + EPS)
            vals = [(xq * rq * (gq_ref[...] * SCALE)).astype(BF16), (xk * rk * gk_ref[...]).astype(BF16),
                    zv_ref[:, cols]]
            for j, t in enumerate(vals):
                if perms[g] is not None:
                    t = _dot(perms[g], t).astype(BF16)
                outs[3 * g + j][...] = t

    tile = pl.BlockSpec((DL_TILE, ATTN_WIDTH), lambda i: (i, 0))
    mat = pl.BlockSpec((DL_TILE, DL_TILE), lambda i: (0, 0))
    vec = pl.BlockSpec((1, ATTN_WIDTH), lambda i: (0, 0))
    outs = _call(
        body, name="qkv_prep", grid=(L // DL_TILE,),
        in_specs=[pl.BlockSpec((DL_TILE, qkv_w), lambda i: (i, 0)), pl.BlockSpec((DL_TILE, qkv_w), lambda i: (i, 1)),
                  pl.BlockSpec((DL_TILE, qkv_w), lambda i: (i, 2)), vec, vec, mat, mat, mat],
        out_specs=[tile] * 9, out_shape=[_sds((L, ATTN_WIDTH), BF16)] * 9,
        sem=("parallel",), rider=rider)(z, z, z, gq_t, gk_t, _head_sum_matrix(), _perm_matrix(ATTN_PATTERN[1][1]),
                                        _perm_matrix(ATTN_PATTERN[2][1]))
    return [tuple(outs[3 * g:3 * g + 3]) for g in range(N_GROUPS)], list(outs[3 * N_GROUPS:])


def _pair_masks():
    lane = lax.broadcasted_iota(jnp.int32, (1, LANES), 1)
    return lane < HEAD_DIM, lane >= HEAD_DIM


def _attn_fwd(qs, ks, v, gi):
    L = qs.shape[0]
    _, d = ATTN_PATTERN[gi]
    nb = L // (d * BLK)

    def body(q_ref, kc_ref, kp_ref, vc_ref, vp_ref, o_ref, l_ref):
        n = pl.program_id(1)
        qi = lax.broadcasted_iota(jnp.int32, (BLK, 2 * BLK), 0)
        kj = lax.broadcasted_iota(jnp.int32, (BLK, 2 * BLK), 1)
        prev = kj < BLK
        mask = jnp.logical_and(jnp.where(prev, kj, qi) >= jnp.where(prev, qi, kj - BLK),
                               kj >= jnp.where(n > 0, 0, BLK))
        q = _dl_read(q_ref)
        kw = jnp.concatenate([_dl_read(kp_ref), _dl_read(kc_ref)], axis=0)
        vw = jnp.concatenate([_dl_read(vp_ref), _dl_read(vc_ref)], axis=0)
        one = jnp.ones((2 * BLK, LANES), BF16)
        o_parts, l_parts = [], []
        for hp in range(N_HEADS // 2):
            ls = slice(hp * LANES, (hp + 1) * LANES)
            qp, kp_, vp_ = q[:, ls], kw[:, ls], vw[:, ls]
            num = jnp.zeros((BLK, LANES), F32)
            den = jnp.zeros((BLK, LANES), F32)
            mb = jnp.zeros((BLK, LANES), F32)
            for he in _pair_masks():
                s = jnp.where(mask, _dot_nt(jnp.where(he, qp, 0), kp_), -jnp.inf)
                m = jnp.max(s, axis=-1, keepdims=True)
                p = jnp.exp(s - m).astype(BF16)
                acc = _dot(p, jnp.concatenate([jnp.where(he, vp_, 0), jnp.where(he, one, 0)], axis=1))
                num += acc[:, :LANES]
                den += acc[:, LANES:]
                mb = jnp.where(he, m, mb)
            o_parts.append((num / den).astype(BF16))
            l_parts.append(mb + jnp.log(den))
        _dl_write(o_ref, jnp.concatenate(o_parts, axis=1))
        _dl_write(l_ref, jnp.concatenate(l_parts, axis=1))

    cur = _dl_spec(d, ATTN_WIDTH, lambda n: n)
    prev = _dl_spec(d, ATTN_WIDTH, lambda n: jnp.maximum(n - 1, 0))
    view = lambda t: _dl_view(t, d)
    o, l = _call(
        body, name=f"attn_fwd_g{gi}", grid=(d, nb), in_specs=[cur, cur, prev, cur, prev], out_specs=[cur, cur],
        out_shape=[_sds(view(qs).shape, BF16), _sds(view(qs).shape, F32)],
        sem=("parallel", "parallel"))(view(qs), view(ks), view(ks), view(v), view(v))
    return o.reshape(L, ATTN_WIDTH), l.reshape(L, ATTN_WIDTH)


def _to_token_order(os_, ls_, pts):
    o_tok, l_tok = [], []
    for o, l, pt in zip(os_, ls_, pts):
        if pt is None:
            o_tok.append(o.astype(F32))
            l_tok.append(l)
        else:
            o_tok.append(_dot(pt, o))
            l_tok.append(_permute(pt, l))
    return o_tok, l_tok


def _combine_fwd(os_, ls_):
    L = os_[0].shape[0]

    def body(o0, o1, o2, l0, l1, l2, pt1_ref, pt2_ref, a_ref):
        o_tok, l_tok = _to_token_order((o0[...], o1[...], o2[...]), (l0[...], l1[...], l2[...]),
                                       (None, pt1_ref[...], pt2_ref[...]))
        w = _combine_weights(*l_tok)
        a_ref[...] = (w[0] * o_tok[0] + w[1] * o_tok[1] + w[2] * o_tok[2]).astype(BF16)

    tile = pl.BlockSpec((DL_TILE, ATTN_WIDTH), lambda i: (i, 0))
    mat = pl.BlockSpec((DL_TILE, DL_TILE), lambda i: (0, 0))
    return _call(body, name="combine_fwd", grid=(L // DL_TILE,), in_specs=[tile] * 6 + [mat, mat], out_specs=tile,
                 out_shape=_sds((L, ATTN_WIDTH), BF16), sem=("parallel",))(
                     *os_, *ls_, _perm_matrix(ATTN_PATTERN[1][1]).T, _perm_matrix(ATTN_PATTERN[2][1]).T)


def _gelu(v):
    c = math.sqrt(2.0 / math.pi)
    return 0.5 * v * (1.0 + jnp.tanh(c * (v + 0.044715 * v * v * v)))


def _gelu_grad(v):
    c = math.sqrt(2.0 / math.pi)
    t = jnp.tanh(c * (v + 0.044715 * v * v * v))
    return 0.5 * (1.0 + t) + 0.5 * v * (1.0 - t * t) * c * (1.0 + 3.0 * 0.044715 * v * v)


def _ssm_fill(u, bwre_ref, bwim_ref, sre, sim):
    for k2 in range(N_PAIR):
        uw = u[:, _win_cols(k2)]
        _to_slabs(sre, k2, _dot(uw, bwre_ref[k2]))
        _to_slabs(sim, k2, _dot(uw, bwim_ref[k2]))


def _win_cols(k2):
    w = k2 // PAIRS_PER_WIN
    return slice(w * SSM_WIN, (w + 1) * SSM_WIN)


def _to_slabs(ref, k2, v):
    for half in range(2):
        for j in range(SSM_SUB):
            ref[2 * k2 + half, j * SSM_PITCH:j * SSM_PITCH + SSM_TC, :] = (
                v[j * SSM_TC:(j + 1) * SSM_TC, half * LANES:(half + 1) * LANES])


def _rows(i):
    return pl.ds(i, SSM_SUB, stride=SSM_PITCH)


def _slab_rows(ref, k):
    return jnp.concatenate([ref[k, j * SSM_PITCH:j * SSM_PITCH + SSM_TC, :] for j in range(SSM_SUB)], axis=0)


def _pair_rows(ref, k2):
    return jnp.concatenate([_slab_rows(ref, 2 * k2), _slab_rows(ref, 2 * k2 + 1)], axis=1).astype(BF16)


def _bcast(ref, k):
    return jnp.broadcast_to(ref[pl.ds(k, 1), :], (SSM_SUB, LANES))


def _scan(sre, sim, are_ref, aim_ref, k0, init, *, reverse, store, sign=1.0):
    ar = [_bcast(are_ref, k0 + kk) for kk in range(SCAN_GROUP)]
    ai = [sign * _bcast(aim_ref, k0 + kk) for kk in range(SCAN_GROUP)]

    def step(t, carry):
        i = SSM_TC - 1 - t if reverse else t
        out = []
        for kk in range(SCAN_GROUP):
            k = k0 + kk
            xr, xi = carry[2 * kk], carry[2 * kk + 1]
            nr = ar[kk] * xr - ai[kk] * xi + sre[k, _rows(i), :]
            ni = ar[kk] * xi + ai[kk] * xr + sim[k, _rows(i), :]
            if store:
                sre[k, _rows(i), :] = nr
                sim[k, _rows(i), :] = ni
            out += [nr, ni]
        return tuple(out)

    flat = []
    for re, im in init:
        flat += [re, im]
    res = lax.fori_loop(0, SSM_TC // 2, lambda t, c: step(2 * t + 1, step(2 * t, c)), tuple(flat))
    return [(res[2 * kk], res[2 * kk + 1]) for kk in range(SCAN_GROUP)]


def _ssm_seeds(ends_re, ends_im, a64re_ref, a64im_ref, carry_re, carry_im, seed_re, seed_im, k,
               *, reverse, sign=1.0):
    ar = a64re_ref[pl.ds(k, 1), :]
    ai = sign * a64im_ref[pl.ds(k, 1), :]
    cr = carry_re[pl.ds(k, 1), :]
    ci = carry_im[pl.ds(k, 1), :]
    order = range(SSM_SUB - 1, -1, -1) if reverse else range(SSM_SUB)
    for j in order:
        seed_re[k, pl.ds(j, 1), :] = cr
        seed_im[k, pl.ds(j, 1), :] = ci
        er = ends_re[k, pl.ds(j, 1), :]
        ei = ends_im[k, pl.ds(j, 1), :]
        cr, ci = ar * cr - ai * ci + er, ar * ci + ai * cr + ei
    carry_re[pl.ds(k, 1), :] = cr
    carry_im[pl.ds(k, 1), :] = ci


def _ssm_specs_consts():
    c2 = pl.BlockSpec((N_SLAB, LANES), lambda b: (0, 0))
    c3 = pl.BlockSpec((N_PAIR, SSM_WIN, SSM_WIN), lambda b: (0, 0, 0))
    return c2, c3


def _ssm_scratch():
    rows = SSM_SUB * SSM_PITCH
    return [pltpu.VMEM((N_SLAB, rows, LANES), F32), pltpu.VMEM((N_SLAB, rows, LANES), F32)]


def _ssm_fwd(z, pk, dskip, rider=None):
    L = z.shape[0]
    nb = L // SSM_TB
    ucol = (3 * N_GROUPS * ATTN_WIDTH) // SSM_WIDTH

    def body(u_ref, are_ref, aim_ref, a64re_ref, a64im_ref, bwre_ref, bwim_ref, cwre_ref, cwim_ref, d_ref,
             ypre_ref, yact_ref, sdre_ref, sdim_ref, sre, sim, carry_re, carry_im, ends_re, ends_im,
             seed_re, seed_im):
        @pl.when(pl.program_id(0) == 0)
        def _():
            carry_re[...] = jnp.zeros_like(carry_re)
            carry_im[...] = jnp.zeros_like(carry_im)

        u = u_ref[...]
        _ssm_fill(u, bwre_ref, bwim_ref, sre, sim)
        zero = jnp.zeros((SSM_SUB, LANES), F32)
        for k0 in range(0, N_SLAB, SCAN_GROUP):
            ends = _scan(sre, sim, are_ref, aim_ref, k0, [(zero, zero)] * SCAN_GROUP, reverse=False, store=False)
            for kk in range(SCAN_GROUP):
                ends_re[k0 + kk] = ends[kk][0]
                ends_im[k0 + kk] = ends[kk][1]
            for kk in range(SCAN_GROUP):
                _ssm_seeds(ends_re, ends_im, a64re_ref, a64im_ref, carry_re, carry_im, seed_re, seed_im,
                           k0 + kk, reverse=False)
            init = [(seed_re[k0 + kk], seed_im[k0 + kk]) for kk in range(SCAN_GROUP)]
            _scan(sre, sim, are_ref, aim_ref, k0, init, reverse=False, store=True)
        sdre_ref[...] = seed_re[...]
        sdim_ref[...] = seed_im[...]
        for w in range(N_PAIR // PAIRS_PER_WIN):
            acc = jnp.zeros((SSM_TB, SSM_WIN), F32)
            for kk in range(PAIRS_PER_WIN):
                k2 = w * PAIRS_PER_WIN + kk
                acc += _dot(_pair_rows(sre, k2), cwre_ref[k2])
                acc -= _dot(_pair_rows(sim, k2), cwim_ref[k2])
            cols = _win_cols(w * PAIRS_PER_WIN)
            ypre = acc + d_ref[:, cols] * u[:, cols].astype(F32)
            ypre_ref[:, cols] = ypre
            yact_ref[:, cols] = _gelu(ypre).astype(BF16)

    c2, c3 = _ssm_specs_consts()
    seed_spec = pl.BlockSpec((None, N_SLAB, SSM_SUB, LANES), lambda b: (b, 0, 0, 0))
    small = pltpu.VMEM((N_SLAB, LANES), F32)
    tile = pltpu.VMEM((N_SLAB, SSM_SUB, LANES), F32)
    return _call(
        body, name="ssm_fwd", grid=(nb,),
        in_specs=[pl.BlockSpec((SSM_TB, SSM_WIDTH), lambda b: (b, ucol)), c2, c2, c2, c2, c3, c3, c3, c3,
                  pl.BlockSpec((1, SSM_WIDTH), lambda b: (0, 0))],
        out_specs=[pl.BlockSpec((SSM_TB, SSM_WIDTH), lambda b: (b, 0)),
                   pl.BlockSpec((SSM_TB, SSM_WIDTH), lambda b: (b, 0)), seed_spec, seed_spec],
        out_shape=[_sds((L, SSM_WIDTH), F32), _sds((L, SSM_WIDTH), BF16),
                   _sds((nb, N_SLAB, SSM_SUB, LANES), F32), _sds((nb, N_SLAB, SSM_SUB, LANES), F32)],
        scratch=_ssm_scratch() + [small, small, tile, tile, tile, tile],
        sem=("arbitrary",), rider=rider)(z, pk["a_re"], pk["a_im"], pk["a64_re"], pk["a64_im"],
                                         pk["bw_re"].astype(BF16), pk["bw_im"].astype(BF16),
                                         pk["cw_re"].astype(BF16), pk["cw_im"].astype(BF16), dskip)


def _combine_weights(l0, l1, l2):
    m = jnp.maximum(jnp.maximum(l0, l1), l2)
    e0, e1, e2 = jnp.exp(l0 - m), jnp.exp(l1 - m), jnp.exp(l2 - m)
    inv = 1.0 / (e0 + e1 + e2)
    return e0 * inv, e1 * inv, e2 * inv


def _mix_fwd(x, z, a, yact, w_ap, w_ga, w_gb, w_out):
    L = x.shape[0]
    cs = D_MODEL // N_CHIPS
    ga_col = (3 * N_GROUPS * ATTN_WIDTH + SSM_WIDTH) // D_MODEL

    def body(x_ref, ga_ref, gs_ref, a_ref, y_ref, wap_ref, wga_ref, wgb_ref, wout_ref,
             x1_ref, aout_ref, sa_ref, sb_ref, mix_ref):
        a = a_ref[...]
        y = y_ref[...]
        for s in range(N_CHIPS):
            cols = slice(s * cs, (s + 1) * cs)
            aout_ref[:, cols] = _dot(a, wap_ref[s]).astype(BF16)
            sa_ref[:, cols] = _dot(y, wga_ref[s]).astype(BF16)
            sb_ref[:, cols] = _dot(y, wgb_ref[s]).astype(BF16)
        s_out = sa_ref[...].astype(F32) * _sigmoid(sb_ref[...].astype(F32))
        mix = (_sigmoid(ga_ref[...].astype(F32)) * aout_ref[...].astype(F32)
               + _sigmoid(gs_ref[...].astype(F32)) * s_out).astype(BF16)
        mix_ref[...] = mix
        x1_ref[...] = x_ref[...] + _dot(mix, wout_ref[...])

    tok = lambda w: pl.BlockSpec((TM_MIX, w), lambda i: (i, 0))
    wsm = pl.BlockSpec((N_CHIPS, ATTN_WIDTH, cs), lambda i: (0, 0, 0))
    return _call(
        body, name="mix_fwd", grid=(L // TM_MIX,),
        in_specs=[tok(D_MODEL), pl.BlockSpec((TM_MIX, D_MODEL), lambda i: (i, ga_col)),
                  pl.BlockSpec((TM_MIX, D_MODEL), lambda i: (i, ga_col + 1))]
                 + [tok(ATTN_WIDTH)] * 2 + [wsm, wsm, wsm, pl.BlockSpec((D_MODEL, D_MODEL), lambda i: (0, 0))],
        out_specs=[tok(D_MODEL), tok(D_MODEL), tok(D_MODEL), tok(D_MODEL), tok(D_MODEL)],
        out_shape=[_sds((L, D_MODEL), F32)] + [_sds((L, D_MODEL), BF16)] * 4,
        sem=("parallel",))(x, z, z, a, yact, w_ap, w_ga, w_gb, w_out.reshape(D_MODEL, D_MODEL))


def _ffn_fwd(x1, g, w_g, w_u, w_d, rider=None):
    L = x1.shape[0]
    fs = D_FF // N_CHIPS
    TM = TM_PROJ

    def body(x_ref, g_ref, wg_ref, wu_ref, wd_ref, x2_ref, h_ref, gate_ref, up_ref, act_ref, acc):
        s = pl.program_id(1)

        @pl.when(s == 0)
        def _():
            xv = x_ref[...]
            r = lax.rsqrt(jnp.mean(xv * xv, axis=-1, keepdims=True) + EPS)
            h_ref[...] = (xv * r * g_ref[...]).astype(BF16)
            acc[...] = jnp.zeros_like(acc)

        h = h_ref[...]
        gate = _dot_nt(h, wg_ref[...])
        up = _dot_nt(h, wu_ref[...])
        sg = _sigmoid(gate)
        silu = gate * sg
        act = (silu * up).astype(BF16)
        gate_ref[...] = (up * (sg * (1.0 + gate * (1.0 - sg)))).astype(BF16)
        up_ref[...] = silu.astype(BF16)
        act_ref[...] = act
        acc[...] += _dot(act, wd_ref[...])

        @pl.when(s == N_CHIPS - 1)
        def _():
            x2_ref[...] = x_ref[...] + acc[...]

    tok = pl.BlockSpec((TM, D_MODEL), lambda i, s: (i, 0))
    ffs = pl.BlockSpec((None, TM, fs), lambda i, s: (s, i, 0))
    return _call(
        body, name="ffn_fwd", grid=(L // TM, N_CHIPS),
        in_specs=[tok, pl.BlockSpec((1, D_MODEL), lambda i, s: (0, 0))]
                 + [pl.BlockSpec((None, fs, D_MODEL), lambda i, s: (s, 0, 0))] * 3,
        out_specs=[tok, tok, ffs, ffs, ffs],
        out_shape=[_sds((L, D_MODEL), F32), _sds((L, D_MODEL), BF16)] + [_sds((N_CHIPS, L, fs), BF16)] * 3,
        scratch=[pltpu.VMEM((TM, D_MODEL), F32)],
        sem=("parallel", "arbitrary"), rider=rider)(x1, g, w_g, w_u, w_d)


def _loss_head(xl, target):
    L = xl.shape[0]

    def body(x_ref, t_ref, dx_ref, loss_ref, acc):
        i = pl.program_id(0)

        @pl.when(i == 0)
        def _():
            acc[...] = jnp.zeros_like(acc)

        e = x_ref[...] - t_ref[...]
        dx_ref[...] = e * (1.0 / D_MODEL)
        acc[...] += jnp.sum((e * e).reshape(TM // SUBLANES, SUBLANES, D_MODEL), axis=0)

        @pl.when(i == pl.num_programs(0) - 1)
        def _():
            loss_ref[...] = (0.5 / D_MODEL) * jnp.sum(acc[...]).reshape(1, 1)

    tok = pl.BlockSpec((TM, D_MODEL), lambda i: (i, 0))
    return _call(
        body, name="loss_head", grid=(L // TM,), in_specs=[tok, tok],
        out_specs=[tok, pl.BlockSpec((1, 1), lambda i: (0, 0))],
        out_shape=[_sds((L, D_MODEL), F32), _sds((1, 1), F32)],
        scratch=[pltpu.VMEM((SUBLANES, D_MODEL), F32)], sem=("arbitrary",))(xl, target)


def _ssm_pack(lam_re, lam_im, log_dt, b_re, b_im, c_re, c_im):
    dt = jnp.exp(log_dt)[:, None]
    mag = jnp.exp(lam_re * dt)
    ang = lam_im * dt
    ar = mag * jnp.cos(ang)
    ai = mag * jnp.sin(ang)
    nr = ar - 1.0
    ni = ai
    den = lam_re * lam_re + lam_im * lam_im
    cr = ((nr * lam_re + ni * lam_im) / den)[..., None]
    ci = ((ni * lam_re - nr * lam_im) / den)[..., None]
    bbr = cr * b_re - ci * b_im
    bbi = cr * b_im + ci * b_re
    gpp = SSM_WIN // SSM_STATE
    gpw = SSM_WIN // SSM_GROUP
    k2 = jnp.arange(N_PAIR)[:, None, None]
    gs = jnp.arange(gpp)[None, :, None]
    gl = jnp.arange(gpw)[None, None, :]
    same = (gl == gpp * (k2 % PAIRS_PER_WIN) + gs).astype(F32)

    def b_windows(bb):
        return jnp.einsum('kgl,kgpc->klcgp', same, bb.reshape(N_PAIR, gpp, SSM_STATE, SSM_GROUP)).reshape(
            N_PAIR, SSM_WIN, SSM_WIN)

    def c_windows(cc):
        return jnp.einsum('kgl,kgcp->kgplc', same, cc.reshape(N_PAIR, gpp, SSM_GROUP, SSM_STATE)).reshape(
            N_PAIR, SSM_WIN, SSM_WIN)

    pr, pi = ar, ai
    for _ in range(int(math.log2(SSM_TC))):
        pr, pi = pr * pr - pi * pi, 2.0 * pr * pi
    return dict(a_re=ar.reshape(N_SLAB, LANES), a_im=ai.reshape(N_SLAB, LANES),
                a64_re=pr.reshape(N_SLAB, LANES), a64_im=pi.reshape(N_SLAB, LANES),
                bw_re=b_windows(bbr), bw_im=b_windows(bbi), cw_re=c_windows(c_re), cw_im=c_windows(c_im))


def _layer_fwd(x, p, rest, rest_stage, next_bufs=None):
    first = {"ici": _gather_ici_rider, "d2d": _gather_d2d_rider}[rest_stage]
    outs = _in_proj_fwd(x, p["g_mix"], p["w_in"], first(rest))
    (z, h), rest = outs[:2], list(outs[2:])
    qkv, got = _qkv_prep(z, jnp.tile(p["g_q"], (1, N_HEADS)), jnp.tile(p["g_k"], (1, N_HEADS)),
                         _gather_d2d_rider(rest) if rest_stage == "ici" else None)
    p = {**p, **dict(zip(BIG[1:], got if rest_stage == "ici" else rest))}
    os_, ls_ = [], []
    for gi in range(N_GROUPS):
        o, l = _attn_fwd(*qkv[gi], gi)
        os_.append(o)
        ls_.append(l)
    a = _combine_fwd(os_, ls_)
    pk = _ssm_pack(p["lambda_re"], p["lambda_im"], p["log_dt"], p["b_re"], p["b_im"], p["c_re"], p["c_im"])
    outs = _ssm_fwd(z, pk, p["d_skip"], _gather_ici_rider(next_bufs[:1]) if next_bufs else None)
    (ypre, yact, sd_re, sd_im), next_in = outs[:4], list(outs[4:])
    x1, aout, sa, sb, mix = _mix_fwd(x, z, a, yact, p["w_attn_proj"], p["w_glu_a"], p["w_glu_b"], p["w_out"])
    outs = _ffn_fwd(x1, p["g_ffn"], p["w_ffn_gate"], p["w_ffn_up"], p["w_ffn_down"],
                    _join_riders(_gather_ici_rider(next_bufs[1:]), _gather_d2d_rider(next_in)) if next_bufs else None)
    x2, h2, gate, up, act = outs[:5]
    nxt = (outs[-1], list(outs[5:-1])) if next_bufs else None
    saved = dict(x=x, z=z, h=h, qkv=qkv, os=os_, ls=ls_, pk=pk, ypre=ypre, yact=yact, sd_re=sd_re, sd_im=sd_im,
                 x1=x1, a=a, aout=aout, sa=sa, sb=sb, mix=mix, h2=h2, gate=gate, up=up, act=act)
    return x2, saved, p, nxt


def _rms_bwd(xv, g, dh):
    r = lax.rsqrt(jnp.mean(xv * xv, axis=-1, keepdims=True) + EPS)
    xn = xv * r
    dxn = dh * g
    dx = r * (dxn - xn * jnp.mean(dxn * xn, axis=-1, keepdims=True))
    dg = jnp.sum((dh * xn).reshape(xv.shape[0] // SUBLANES, SUBLANES, xv.shape[1]), axis=0)
    return dx, dg


def _ffn_bwd_act(dx2, gate, up, w_d):
    L = dx2.shape[0]
    fs = D_FF // N_CHIPS
    TM = TM_PROJ

    def body(dx_ref, dact_dgate_ref, dact_dup_ref, wd_ref, dgate_ref, dup_ref):
        dact = _dot_nt(dx_ref[...].astype(BF16), wd_ref[...])
        dgate_ref[...] = (dact * dact_dgate_ref[...].astype(F32)).astype(BF16)
        dup_ref[...] = (dact * dact_dup_ref[...].astype(F32)).astype(BF16)

    ffs = pl.BlockSpec((None, TM, fs), lambda i, s: (s, i, 0))
    return _call(
        body, name="ffn_bwd_act", grid=(L // TM, N_CHIPS),
        in_specs=[pl.BlockSpec((TM, D_MODEL), lambda i, s: (i, 0)), ffs, ffs,
                  pl.BlockSpec((None, fs, D_MODEL), lambda i, s: (s, 0, 0))],
        out_specs=[ffs, ffs], out_shape=[_sds((N_CHIPS, L, fs), BF16)] * 2,
        sem=("parallel", "parallel"))(dx2, gate, up, w_d)


def _ffn_bwd_in(dx2, x1, g, dgate, dup, w_g, w_u, rider=None):
    L = x1.shape[0]
    fs = D_FF // N_CHIPS
    TM = TM_PROJ

    def body(dx_ref, x_ref, g_ref, dgate_ref, dup_ref, wg_ref, wu_ref, dx1_ref, dg_ref, acc, dgacc):
        i, s = pl.program_id(0), pl.program_id(1)

        @pl.when(s == 0)
        def _():
            acc[...] = jnp.zeros_like(acc)

        @pl.when(jnp.logical_and(i == 0, s == 0))
        def _():
            dgacc[...] = jnp.zeros_like(dgacc)

        acc[...] += _dot(dgate_ref[...], wg_ref[...]) + _dot(dup_ref[...], wu_ref[...])

        @pl.when(s == N_CHIPS - 1)
        def _():
            dx, dg = _rms_bwd(x_ref[...], g_ref[...], acc[...])
            dx1_ref[...] = dx_ref[...] + dx
            dgacc[...] += dg

        @pl.when(jnp.logical_and(i == pl.num_programs(0) - 1, s == N_CHIPS - 1))
        def _():
            dg_ref[...] = jnp.sum(dgacc[...], axis=0, keepdims=True)

    tok = pl.BlockSpec((TM, D_MODEL), lambda i, s: (i, 0))
    ffs = pl.BlockSpec((None, TM, fs), lambda i, s: (s, i, 0))
    vec = pl.BlockSpec((1, D_MODEL), lambda i, s: (0, 0))
    return _call(
        body, name="ffn_bwd_in", grid=(L // TM, N_CHIPS),
        in_specs=[tok, tok, vec, ffs, ffs,
                  pl.BlockSpec((None, fs, D_MODEL), lambda i, s: (s, 0, 0)),
                  pl.BlockSpec((None, fs, D_MODEL), lambda i, s: (s, 0, 0))],
        out_specs=[tok, vec],
        out_shape=[_sds((L, D_MODEL), F32), _sds((1, D_MODEL), F32)],
        scratch=[pltpu.VMEM((TM, D_MODEL), F32), pltpu.VMEM((SUBLANES, D_MODEL), F32)],
        sem=("arbitrary", "arbitrary"), rider=rider)(dx2, x1, g, dgate, dup, w_g, w_u)


def _wgrad(a, b, *, name, grid_kn, a_spec, b_spec, out_shape, out_spec):
    L = a.shape[-2]
    nl = L // TL_WGRAD

    def body(a_ref, b_ref, o_ref):
        @pl.when(pl.program_id(2) == 0)
        def _():
            o_ref[...] = jnp.zeros_like(o_ref)
        o_ref[...] += _dot_tn(a_ref[...].astype(BF16), b_ref[...].astype(BF16))

    return _call(body, name=name, grid=(*grid_kn, nl), in_specs=[a_spec, b_spec], out_specs=out_spec,
                 out_shape=out_shape, sem=("parallel", "parallel", "arbitrary"))(a, b)


def _wgrad_cols(a, b, name):
    K, N = a.shape[1], b.shape[1]
    ns = N // N_CHIPS
    if N * K * 4 <= 4 * 1024 * 1024:
        L = a.shape[0]

        def body(a_ref, b_ref, o_ref):
            @pl.when(pl.program_id(0) == 0)
            def _():
                o_ref[...] = jnp.zeros_like(o_ref)
            av = a_ref[...].astype(BF16)
            for s in range(N_CHIPS):
                o_ref[s] += _dot_tn(av, b_ref[:, s * ns:(s + 1) * ns].astype(BF16))

        return _call(body, name=name, grid=(L // TL_WGRAD,),
                     in_specs=[pl.BlockSpec((TL_WGRAD, K), lambda t: (t, 0)),
                               pl.BlockSpec((TL_WGRAD, N), lambda t: (t, 0))],
                     out_specs=pl.BlockSpec((N_CHIPS, K, ns), lambda t: (0, 0, 0)),
                     out_shape=_sds((N_CHIPS, K, ns), F32), sem=("arbitrary",))(a, b)
    tn = ns // 2 if ns % (2 * LANES) == 0 else ns
    nj = ns // tn
    return _wgrad(a, b, name=name, grid_kn=(1, N_CHIPS * nj),
                  a_spec=pl.BlockSpec((TL_WGRAD, K), lambda i, j, t: (t, 0)),
                  b_spec=pl.BlockSpec((TL_WGRAD, tn), lambda i, j, t: (t, j)),
                  out_shape=_sds((N_CHIPS, K, ns), F32),
                  out_spec=pl.BlockSpec((None, K, tn), lambda i, j, t: (j // nj, 0, j % nj)))


def _wgrad_full(a, b, name):
    K, N = a.shape[1], b.shape[1]
    return _wgrad(a, b, name=name, grid_kn=(1, 1),
                  a_spec=pl.BlockSpec((TL_WGRAD, K), lambda i, j, t: (t, 0)),
                  b_spec=pl.BlockSpec((TL_WGRAD, N), lambda i, j, t: (t, 0)),
                  out_shape=_sds((K, N), F32), out_spec=pl.BlockSpec((K, N), lambda i, j, t: (0, 0)))


def _wgrad_ff_cols(a, b, name):
    K, fs = a.shape[1], b.shape[2]
    return _wgrad(a, b, name=name, grid_kn=(1, N_CHIPS),
                  a_spec=pl.BlockSpec((TL_WGRAD, K), lambda i, j, t: (t, 0)),
                  b_spec=pl.BlockSpec((None, TL_WGRAD, fs), lambda i, j, t: (j, t, 0)),
                  out_shape=_sds((N_CHIPS, K, fs), F32),
                  out_spec=pl.BlockSpec((None, K, fs), lambda i, j, t: (j, 0, 0)))


def _wgrad_ff_rows(a, b, name):
    fs, N = a.shape[2], b.shape[1]
    return _wgrad(a, b, name=name, grid_kn=(N_CHIPS, 1),
                  a_spec=pl.BlockSpec((None, TL_WGRAD, fs), lambda i, j, t: (i, t, 0)),
                  b_spec=pl.BlockSpec((TL_WGRAD, N), lambda i, j, t: (t, 0)),
                  out_shape=_sds((N_CHIPS, fs, N), F32),
                  out_spec=pl.BlockSpec((None, fs, N), lambda i, j, t: (i, 0, 0)))


def _mix_bwd(dx, z, aout, sa, sb, ypre, w_ap, w_ga, w_gb, w_out, rider=None):
    L = dx.shape[0]
    cs = D_MODEL // N_CHIPS
    ga_col = (3 * N_GROUPS * ATTN_WIDTH + SSM_WIDTH) // D_MODEL

    def body(dx_ref, ga_ref, gs_ref, aout_ref, sa_ref, sb_ref, ypre_ref, wap_ref, wga_ref, wgb_ref, wout_ref,
             dgates_ref, da_ref, gy_ref, daout_ref, dsa_ref, dsb_ref):
        dmix = _dot_nt(dx_ref[...].astype(BF16), wout_ref[...])
        sig_a = _sigmoid(ga_ref[...].astype(F32))
        sig_s = _sigmoid(gs_ref[...].astype(F32))
        a_out = aout_ref[...].astype(F32)
        s_a = sa_ref[...].astype(F32)
        sig_b = _sigmoid(sb_ref[...].astype(F32))
        s_out = s_a * sig_b
        daout = (dmix * sig_a).astype(BF16)
        daout_ref[...] = daout
        dgates_ref[:, :D_MODEL] = (dmix * a_out * sig_a * (1.0 - sig_a)).astype(BF16)
        dgates_ref[:, D_MODEL:] = (dmix * s_out * sig_s * (1.0 - sig_s)).astype(BF16)
        ds_out = dmix * sig_s
        dsa = (ds_out * sig_b).astype(BF16)
        dsb = (ds_out * s_a * sig_b * (1.0 - sig_b)).astype(BF16)
        dsa_ref[...] = dsa
        dsb_ref[...] = dsb
        da = jnp.zeros((TM_MIX, ATTN_WIDTH), F32)
        dy = jnp.zeros((TM_MIX, SSM_WIDTH), F32)
        for s in range(N_CHIPS):
            cols = slice(s * cs, (s + 1) * cs)
            da += _dot_nt(daout[:, cols], wap_ref[s])
            dy += _dot_nt(dsa[:, cols], wga_ref[s]) + _dot_nt(dsb[:, cols], wgb_ref[s])
        gy_ref[...] = dy * _gelu_grad(ypre_ref[...])
        da_ref[...] = da

    tok = lambda w: pl.BlockSpec((TM_MIX, w), lambda i: (i, 0))
    wsm = pl.BlockSpec((N_CHIPS, ATTN_WIDTH, cs), lambda i: (0, 0, 0))
    return _call(
        body, name="mix_bwd", grid=(L // TM_MIX,),
        in_specs=[tok(D_MODEL), pl.BlockSpec((TM_MIX, D_MODEL), lambda i: (i, ga_col)),
                  pl.BlockSpec((TM_MIX, D_MODEL), lambda i: (i, ga_col + 1)),
                  tok(D_MODEL), tok(D_MODEL), tok(D_MODEL), tok(SSM_WIDTH),
                  wsm, wsm, wsm, pl.BlockSpec((D_MODEL, D_MODEL), lambda i: (0, 0))],
        out_specs=[tok(2 * D_MODEL), tok(ATTN_WIDTH), tok(SSM_WIDTH)] + [tok(D_MODEL)] * 3,
        out_shape=[_sds((L, 2 * D_MODEL), BF16), _sds((L, ATTN_WIDTH), F32), _sds((L, SSM_WIDTH), F32)]
                  + [_sds((L, D_MODEL), BF16)] * 3,
        sem=("parallel",), rider=rider)(dx, z, z, aout, sa, sb, ypre, w_ap, w_ga, w_gb,
                                        w_out.reshape(D_MODEL, D_MODEL))


def _combine_bwd(da, os_, ls_):
    L = da.shape[0]

    def body(da_ref, o0, o1, o2, l0, l1, l2, hs_ref, p1_ref, p2_ref, pt1_ref, pt2_ref,
             do0, do1, do2, c0, c1, c2):
        o_tok, l_tok = _to_token_order((o0[...], o1[...], o2[...]), (l0[...], l1[...], l2[...]),
                                       (None, pt1_ref[...], pt2_ref[...]))
        w = _combine_weights(*l_tok)
        dav = da_ref[...]
        hs = hs_ref[...]
        tbar = sum(wg * _head_sum(dav * og, hs) for wg, og in zip(w, o_tok))
        for wg, pm, do_ref, c_ref in zip(w, (None, p1_ref[...], p2_ref[...]), (do0, do1, do2), (c0, c1, c2)):
            dog = (wg * dav).astype(BF16)
            cg = -wg * tbar
            do_ref[...] = dog if pm is None else _dot(pm, dog).astype(BF16)
            c_ref[...] = cg if pm is None else _dot(pm, cg.astype(BF16))

    tile = pl.BlockSpec((DL_TILE, ATTN_WIDTH), lambda i: (i, 0))
    mat = pl.BlockSpec((DL_TILE, DL_TILE), lambda i: (0, 0))
    p1, p2 = _perm_matrix(ATTN_PATTERN[1][1]), _perm_matrix(ATTN_PATTERN[2][1])
    outs = _call(body, name="combine_bwd", grid=(L // DL_TILE,), in_specs=[tile] * 7 + [mat] * 5,
                 out_specs=[tile] * 6,
                 out_shape=[_sds((L, ATTN_WIDTH), BF16)] * 3 + [_sds((L, ATTN_WIDTH), F32)] * 3,
                 sem=("parallel",))(da, *os_, *ls_, _head_sum_matrix(), p1, p2, p1.T, p2.T)
    return outs[:3], outs[3:]


def _attn_bwd(qs, ks, v, do, l, c, gi, rider=None):
    L = qs.shape[0]
    _, d = ATTN_PATTERN[gi]
    nb = L // (d * BLK)

    def body(q0_ref, q1_ref, k_ref, v_ref, do0_ref, do1_ref, l0_ref, l1_ref, c0_ref, c1_ref,
             dq_ref, dk_ref, dv_ref, carry):
        n = pl.program_id(1)

        @pl.when(n == 0)
        def _():
            carry[...] = jnp.zeros_like(carry)

        qi = lax.broadcasted_iota(jnp.int32, (2 * BLK, BLK), 0)
        kj = lax.broadcasted_iota(jnp.int32, (2 * BLK, BLK), 1)
        first = qi < BLK
        mask = jnp.logical_and(jnp.where(first, qi, kj) >= jnp.where(first, kj, qi - BLK),
                               qi < jnp.where(n < nb - 1, 2 * BLK, BLK))
        q2 = jnp.concatenate([_dl_read(q0_ref), _dl_read(q1_ref)], axis=0)
        do2 = jnp.concatenate([_dl_read(do0_ref), _dl_read(do1_ref)], axis=0)
        l2 = jnp.concatenate([_dl_read(l0_ref), _dl_read(l1_ref)], axis=0)
        c2 = jnp.concatenate([_dl_read(c0_ref), _dl_read(c1_ref)], axis=0)
        k = _dl_read(k_ref)
        v_ = _dl_read(v_ref)
        h0, h1 = _pair_masks()
        mask2 = jnp.concatenate([mask, mask], axis=1)
        dq_parts, dk_parts, dv_parts = [], [], []
        for hp in range(N_HEADS // 2):
            ls = slice(hp * LANES, (hp + 1) * LANES)
            qp, dop, kp_, vp_ = q2[:, ls], do2[:, ls], k[:, ls], v_[:, ls]
            kk = jnp.concatenate([jnp.where(h0, kp_, 0), jnp.where(h1, kp_, 0)], axis=0)
            vv = jnp.concatenate([jnp.where(h0, vp_, 0), jnp.where(h1, vp_, 0)], axis=0)

            def per_head(t):
                a = jnp.broadcast_to(t[:, hp * LANES:hp * LANES + 1], (2 * BLK, BLK))
                b = jnp.broadcast_to(t[:, hp * LANES + HEAD_DIM:hp * LANES + HEAD_DIM + 1], (2 * BLK, BLK))
                return jnp.concatenate([a, b], axis=1)

            p = jnp.where(mask2, jnp.exp(_dot_nt(qp, kk) - per_head(l2)), 0.0)
            ds = (p * (_dot_nt(dop, vv) + per_head(c2))).astype(BF16)
            dv2 = _dot_tn(p.astype(BF16), dop)
            dk2 = _dot_tn(ds, qp)
            dq2 = _dot(ds, kk)
            dq_parts.append((dq2[:BLK] + carry[:, ls]).astype(BF16))
            carry[:, ls] = dq2[BLK:]
            dk_parts.append(jnp.where(h0, dk2[:BLK], dk2[BLK:]).astype(BF16))
            dv_parts.append(jnp.where(h0, dv2[:BLK], dv2[BLK:]).astype(BF16))
        _dl_write(dq_ref, jnp.concatenate(dq_parts, axis=1))
        _dl_write(dk_ref, jnp.concatenate(dk_parts, axis=1))
        _dl_write(dv_ref, jnp.concatenate(dv_parts, axis=1))

    cur = _dl_spec(d, ATTN_WIDTH, lambda n: n)
    nxt = _dl_spec(d, ATTN_WIDTH, lambda n: jnp.minimum(n + 1, nb - 1))
    view = lambda t: _dl_view(t, d)
    outs = _call(
        body, name=f"attn_bwd_g{gi}", grid=(d, nb),
        in_specs=[cur, nxt, cur, cur, cur, nxt, cur, nxt, cur, nxt], out_specs=[cur, cur, cur],
        out_shape=[_sds(view(qs).shape, BF16)] * 3, scratch=[pltpu.VMEM((BLK, ATTN_WIDTH), F32)],
        sem=("parallel", "arbitrary"), rider=rider)(view(qs), view(qs), view(ks), view(v), view(do), view(do),
                                                    view(l), view(l), view(c), view(c))
    return [t.reshape(L, ATTN_WIDTH) for t in outs[:3]], list(outs[3:])


def _qkv_post(z, dqkv, du, dgates, gq_t, gk_t):
    L = z.shape[0]
    qkv_w = N_GROUPS * ATTN_WIDTH

    def body(zq_ref, zk_ref, gq_ref, gk_ref, hs_ref, pt1_ref, pt2_ref, du_ref, dgates_ref, *rest):
        dl_refs, (dz_ref, dgq_ref, dgk_ref) = rest[:9], rest[9:]

        @pl.when(pl.program_id(0) == 0)
        def _():
            dgq_ref[...] = jnp.zeros_like(dgq_ref)
            dgk_ref[...] = jnp.zeros_like(dgk_ref)

        hs = hs_ref[...]
        pts = (None, pt1_ref[...], pt2_ref[...])

        def rows8(t):
            return jnp.sum(t.reshape(DL_TILE // SUBLANES, SUBLANES, ATTN_WIDTH), axis=0)

        def norm_bwd(x, gain, dn):
            r = lax.rsqrt(_head_sum(x * x, hs) * (1.0 / HEAD_DIM) + EPS)
            xh = x * r
            dh = dn * gain
            return r * (dh - xh * (_head_sum(dh * xh, hs) * (1.0 / HEAD_DIM))), rows8(dn * xh)

        for g in range(N_GROUPS):
            tok = [t[...].astype(F32) if pts[g] is None else _dot(pts[g], t[...]) for t in dl_refs[3 * g:3 * g + 3]]
            cols = slice(g * ATTN_WIDTH, (g + 1) * ATTN_WIDTH)
            dq, pq = norm_bwd(zq_ref[:, cols].astype(F32), gq_ref[...] * SCALE, tok[0])
            dk, pk_ = norm_bwd(zk_ref[:, cols].astype(F32), gk_ref[...], tok[1])
            dgq_ref[...] += pq * SCALE
            dgk_ref[...] += pk_
            dz_ref[:, cols] = dq.astype(BF16)
            dz_ref[:, qkv_w + g * ATTN_WIDTH:qkv_w + (g + 1) * ATTN_WIDTH] = dk.astype(BF16)
            dz_ref[:, 2 * qkv_w + g * ATTN_WIDTH:2 * qkv_w + (g + 1) * ATTN_WIDTH] = tok[2].astype(BF16)
        dz_ref[:, 3 * qkv_w:3 * qkv_w + SSM_WIDTH] = du_ref[...]
        dz_ref[:, 3 * qkv_w + SSM_WIDTH:] = dgates_ref[...]

    tile = lambda w: pl.BlockSpec((DL_TILE, w), lambda i: (i, 0))
    mat = pl.BlockSpec((DL_TILE, DL_TILE), lambda i: (0, 0))
    vec = pl.BlockSpec((1, ATTN_WIDTH), lambda i: (0, 0))
    acc = pl.BlockSpec((SUBLANES, ATTN_WIDTH), lambda i: (0, 0))
    flat = [t for grp in dqkv for t in grp]
    return _call(
        body, name="qkv_post", grid=(L // DL_TILE,),
        in_specs=[tile(qkv_w), pl.BlockSpec((DL_TILE, qkv_w), lambda i: (i, 1)), vec, vec, mat, mat, mat,
                  tile(SSM_WIDTH), tile(2 * D_MODEL)] + [tile(ATTN_WIDTH)] * 9,
        out_specs=[tile(IN_COLS), acc, acc],
        out_shape=[_sds((L, IN_COLS), BF16), _sds((SUBLANES, ATTN_WIDTH), F32), _sds((SUBLANES, ATTN_WIDTH), F32)],
        sem=("arbitrary",))(z, z, gq_t, gk_t, _head_sum_matrix(), _perm_matrix(ATTN_PATTERN[1][1]).T,
                            _perm_matrix(ATTN_PATTERN[2][1]).T, du, dgates, *flat)


def _scan_rev_grad(sre, sim, rre, rim, are_ref, aim_ref, k0, init, seed_re, seed_im):
    ar = [_bcast(are_ref, k0 + kk) for kk in range(SCAN_GROUP)]
    ai = [-_bcast(aim_ref, k0 + kk) for kk in range(SCAN_GROUP)]

    def update(i, xprev, carry):
        out = []
        for kk in range(SCAN_GROUP):
            k = k0 + kk
            lr, li, dr, di = carry[4 * kk:4 * kk + 4]
            nr = ar[kk] * lr - ai[kk] * li + rre[k, _rows(i), :]
            ni = ar[kk] * li + ai[kk] * lr + rim[k, _rows(i), :]
            rre[k, _rows(i), :] = nr
            rim[k, _rows(i), :] = ni
            xr, xi = xprev(k)
            out += [nr, ni, dr + xr * nr + xi * ni, di + xr * ni - xi * nr]
        return tuple(out)

    def step(t, carry):
        i = SSM_TC - 1 - t
        return update(i, lambda k: (sre[k, _rows(i - 1), :], sim[k, _rows(i - 1), :]), carry)

    zero = jnp.zeros((SSM_SUB, LANES), F32)
    flat = []
    for re, im in init:
        flat += [re, im, zero, zero]
    res = lax.fori_loop(0, (SSM_TC - 1) // 2, lambda t, c: step(2 * t + 1, step(2 * t, c)), tuple(flat))
    res = step(SSM_TC - 2, res)
    res = update(0, lambda k: (seed_re[k], seed_im[k]), res)
    return [(res[4 * kk + 2], res[4 * kk + 3]) for kk in range(SCAN_GROUP)]


def _ssm_bwd(z, gy, pk, dskip, sd_re, sd_im, rider=None):
    L = z.shape[0]
    nb = L // SSM_TB
    ucol = (3 * N_GROUPS * ATTN_WIDTH) // SSM_WIDTH
    nwin = N_PAIR // PAIRS_PER_WIN

    def body(u_ref, gy_ref, are_ref, aim_ref, a64re_ref, a64im_ref, bwre_ref, bwim_ref, cwre_ref, cwim_ref, d_ref,
             sdre_ref, sdim_ref,
             du_ref, dare_ref, daim_ref, dbre_ref, dbim_ref, dcre_ref, dcim_ref, dd_ref,
             sre, sim, rre, rim, carry_re, carry_im, ends_re, ends_im, seed_re, seed_im):
        @pl.when(pl.program_id(0) == 0)
        def _():
            carry_re[...] = jnp.zeros_like(carry_re)
            carry_im[...] = jnp.zeros_like(carry_im)
            for ref in (dare_ref, daim_ref, dbre_ref, dbim_ref, dcre_ref, dcim_ref, dd_ref):
                ref[...] = jnp.zeros_like(ref)

        u = u_ref[...]
        gyv = gy_ref[...]
        gyb = gyv.astype(BF16)
        _ssm_fill(u, bwre_ref, bwim_ref, sre, sim)
        for k2 in range(N_PAIR):
            gw = gyb[:, _win_cols(k2)]
            _to_slabs(rre, k2, _dot_nt(gw, cwre_ref[k2]))
            _to_slabs(rim, k2, -_dot_nt(gw, cwim_ref[k2]))
        zero = jnp.zeros((SSM_SUB, LANES), F32)
        for k0 in range(0, N_SLAB, SCAN_GROUP):
            grp = range(k0, k0 + SCAN_GROUP)
            _scan(sre, sim, are_ref, aim_ref, k0, [(sdre_ref[k], sdim_ref[k]) for k in grp],
                  reverse=False, store=True)
            ends = _scan(rre, rim, are_ref, aim_ref, k0, [(zero, zero)] * SCAN_GROUP, reverse=True, store=False,
                         sign=-1.0)
            for kk, k in enumerate(grp):
                ends_re[k] = ends[kk][0]
                ends_im[k] = ends[kk][1]
            for k in grp:
                _ssm_seeds(ends_re, ends_im, a64re_ref, a64im_ref, carry_re, carry_im, seed_re, seed_im, k,
                           reverse=True, sign=-1.0)
            das = _scan_rev_grad(sre, sim, rre, rim, are_ref, aim_ref, k0,
                                 [(seed_re[k], seed_im[k]) for k in grp], sdre_ref, sdim_ref)
            for kk, k in enumerate(grp):
                dare_ref[k] += das[kk][0]
                daim_ref[k] += das[kk][1]
        for w in range(nwin):
            cols = _win_cols(w * PAIRS_PER_WIN)
            uw = u[:, cols]
            gw = gyb[:, cols]
            acc = gyv[:, cols] * d_ref[:, cols]
            for kk in range(PAIRS_PER_WIN):
                k2 = w * PAIRS_PER_WIN + kk
                lr = _pair_rows(rre, k2)
                li = _pair_rows(rim, k2)
                acc += _dot_nt(lr, bwre_ref[k2]) + _dot_nt(li, bwim_ref[k2])
                dbre_ref[k2] += _dot_tn(uw, lr)
                dbim_ref[k2] += _dot_tn(uw, li)
                dcre_ref[k2] += _dot_tn(_pair_rows(sre, k2), gw)
                dcim_ref[k2] -= _dot_tn(_pair_rows(sim, k2), gw)
            du_ref[:, cols] = acc.astype(BF16)
        dd_ref[...] += jnp.sum((gyv * u.astype(F32)).reshape(SSM_TB // SUBLANES, SUBLANES, SSM_WIDTH), axis=0)

    c2, c3 = _ssm_specs_consts()
    rev = lambda b: nb - 1 - b
    seed_spec = pl.BlockSpec((None, N_SLAB, SSM_SUB, LANES), lambda b: (rev(b), 0, 0, 0))
    tile_out = pl.BlockSpec((N_SLAB, SSM_SUB, LANES), lambda b: (0, 0, 0))
    small = pltpu.VMEM((N_SLAB, LANES), F32)
    tile = pltpu.VMEM((N_SLAB, SSM_SUB, LANES), F32)
    return _call(
        body, name="ssm_bwd", grid=(nb,),
        in_specs=[pl.BlockSpec((SSM_TB, SSM_WIDTH), lambda b: (rev(b), ucol)),
                  pl.BlockSpec((SSM_TB, SSM_WIDTH), lambda b: (rev(b), 0)),
                  c2, c2, c2, c2, c3, c3, c3, c3, pl.BlockSpec((1, SSM_WIDTH), lambda b: (0, 0)),
                  seed_spec, seed_spec],
        out_specs=[pl.BlockSpec((SSM_TB, SSM_WIDTH), lambda b: (rev(b), 0)), tile_out, tile_out, c3, c3, c3, c3,
                   pl.BlockSpec((SUBLANES, SSM_WIDTH), lambda b: (0, 0))],
        out_shape=[_sds((L, SSM_WIDTH), BF16), _sds((N_SLAB, SSM_SUB, LANES), F32),
                   _sds((N_SLAB, SSM_SUB, LANES), F32)] + [_sds((N_PAIR, SSM_WIN, SSM_WIN), F32)] * 4
                  + [_sds((SUBLANES, SSM_WIDTH), F32)],
        scratch=_ssm_scratch() + _ssm_scratch() + [small, small, tile, tile, tile, tile],
        sem=("arbitrary",), rider=rider)(z, gy, pk["a_re"], pk["a_im"], pk["a64_re"], pk["a64_im"],
                            pk["bw_re"].astype(BF16), pk["bw_im"].astype(BF16),
                            pk["cw_re"].astype(BF16), pk["cw_im"].astype(BF16), dskip, sd_re, sd_im)


def _in_proj_bwd(dz, w, x, g, dres, rider=None):
    L = x.shape[0]
    ns = w.shape[2]
    tn = ns
    nj = ns // tn
    nt = N_CHIPS * nj
    TM = TM_PROJ

    def body(dz_ref, w_ref, x_ref, g_ref, dres_ref, dx_ref, dg_ref, acc, dgacc):
        i, j = pl.program_id(0), pl.program_id(1)

        @pl.when(j == 0)
        def _():
            acc[...] = jnp.zeros_like(acc)

        @pl.when(jnp.logical_and(i == 0, j == 0))
        def _():
            dgacc[...] = jnp.zeros_like(dgacc)

        acc[...] += _dot_nt(dz_ref[...], w_ref[...])

        @pl.when(j == nt - 1)
        def _():
            dx, dg = _rms_bwd(x_ref[...], g_ref[...], acc[...])
            dx_ref[...] = dres_ref[...] + dx
            dgacc[...] += dg

        @pl.when(jnp.logical_and(i == pl.num_programs(0) - 1, j == nt - 1))
        def _():
            dg_ref[...] = jnp.sum(dgacc[...], axis=0, keepdims=True)

    tok = pl.BlockSpec((TM, D_MODEL), lambda i, j: (i, 0))
    vec = pl.BlockSpec((1, D_MODEL), lambda i, j: (0, 0))
    return _call(
        body, name="in_proj_bwd", grid=(L // TM, nt),
        in_specs=[pl.BlockSpec((TM, tn), lambda i, j: (i, j)),
                  pl.BlockSpec((None, D_MODEL, tn), lambda i, j: (j // nj, 0, j % nj)), tok, vec, tok],
        out_specs=[tok, vec],
        out_shape=[_sds((L, D_MODEL), F32), _sds((1, D_MODEL), F32)],
        scratch=[pltpu.VMEM((TM, D_MODEL), F32), pltpu.VMEM((SUBLANES, D_MODEL), F32)],
        sem=("arbitrary", "arbitrary"), rider=rider)(dz, w, x, g, dres)


SSM_PARAMS = ("lambda_re", "lambda_im", "log_dt", "b_re", "b_im", "c_re", "c_im")
EARLY = ("w_ffn_gate", "w_ffn_up", "w_ffn_down")
LATE = ("w_in", "w_attn_proj", "w_glu_a", "w_glu_b", "w_out")


def _layer_bwd(dx2, sv, p, pending, owned, l, idx):
    chip_idx, c_idx = idx
    g = {}
    owned = dict(owned)

    def settle(name, partial, arrived, layer):
        owned[name] = _sum_owner(partial, arrived, owned[name], layer, chip_idx, c_idx)

    dgate, dup = _ffn_bwd_act(dx2, sv["gate"], sv["up"], p["w_ffn_down"])
    outs = _ffn_bwd_in(dx2, sv["x1"], p["g_ffn"], dgate, dup, p["w_ffn_gate"], p["w_ffn_up"],
                       _scatter_rider([pending[n] for n in LATE[1:]]) if pending else None)
    dx1, g["g_ffn"] = outs[:2]
    for n, t in zip(LATE[1:], outs[2:]):
        settle(n, pending[n], t, l + 1)
    g["w_ffn_gate"] = _wgrad_ff_rows(dgate, sv["h2"], "wgrad_ffn_gate")
    g["w_ffn_up"] = _wgrad_ff_rows(dup, sv["h2"], "wgrad_ffn_up")
    g["w_ffn_down"] = _wgrad_ff_rows(sv["act"], dx2, "wgrad_ffn_down")

    outs = _mix_bwd(dx1, sv["z"], sv["aout"], sv["sa"], sv["sb"], sv["ypre"], p["w_attn_proj"], p["w_glu_a"],
                    p["w_glu_b"], p["w_out"], _swap_rider([g[n] for n in EARLY]))
    dgates, da, gy, daout, dsa, dsb = outs[:6]
    early = [_add_half(g[n], s, c_idx) for n, s in zip(EARLY, outs[6:])]
    g["w_out"] = _wgrad_full(sv["mix"], dx1, "wgrad_out").reshape(N_CHIPS, D_MODEL // N_CHIPS, D_MODEL)
    g["w_attn_proj"] = _wgrad_cols(sv["a"], daout, "wgrad_attn_proj")
    g["w_glu_a"] = _wgrad_cols(sv["yact"], dsa, "wgrad_glu_a")
    g["w_glu_b"] = _wgrad_cols(sv["yact"], dsb, "wgrad_glu_b")

    outs = _ssm_bwd(sv["z"], gy, sv["pk"], p["d_skip"], sv["sd_re"], sv["sd_im"],
                    _scatter_rider([pending[LATE[0]]]) if pending else None)
    du, da_re, da_im, dbw_re, dbw_im, dcw_re, dcw_im, dd = outs[:8]
    if pending:
        settle(LATE[0], pending[LATE[0]], outs[8], l + 1)
    g["d_skip"] = jnp.sum(dd, axis=0, keepdims=True)
    g["ssm_pack_ct"] = dict(a_re=da_re, a_im=da_im, bw_re=dbw_re, bw_im=dbw_im, cw_re=dcw_re, cw_im=dcw_im)

    dos, cs = _combine_bwd(da, sv["os"], sv["ls"])
    dqkv = []
    for gi in range(N_GROUPS):
        grads, arrived = _attn_bwd(*sv["qkv"][gi], dos[gi], sv["ls"][gi], cs[gi], gi, _scatter_rider([early[gi]]))
        settle(EARLY[gi], early[gi], arrived[0], l)
        dqkv.append(grads)
    dz, gq8, gk8 = _qkv_post(sv["z"], dqkv, du, dgates, jnp.tile(p["g_q"], (1, N_HEADS)),
                             jnp.tile(p["g_k"], (1, N_HEADS)))
    g["g_q"] = jnp.sum(gq8.reshape(SUBLANES * N_HEADS, HEAD_DIM), axis=0, keepdims=True)
    g["g_k"] = jnp.sum(gk8.reshape(SUBLANES * N_HEADS, HEAD_DIM), axis=0, keepdims=True)
    g["w_in"] = _wgrad_cols(sv["h"], dz, "wgrad_in")
    outs = _in_proj_bwd(dz, p["w_in"], sv["x"], p["g_mix"], dx1, _swap_rider([g[n] for n in LATE]))
    dx, g["g_mix"] = outs[:2]
    late = {n: _add_half(g[n], s, c_idx) for n, s in zip(LATE, outs[2:])}
    return dx, g, late, owned


def _place():
    x, y, c = lax.axis_index("x"), lax.axis_index("y"), lax.axis_index("c")
    others = [(1 - x, y), (x, 1 - y), (1 - x, 1 - y)]
    return x, y, c, others


def _half(ref, hc):
    rows = ref.shape[-2] // 2
    idx = (slice(None),) * (len(ref.shape) - 2) + (pl.ds(hc * rows, rows), slice(None))
    return ref.at[idx]


def _comm_call(body, name, ins, out_shapes, n_remote, aliases=None):
    scratch = [pltpu.SemaphoreType.DMA((n_remote,)), pltpu.SemaphoreType.DMA((n_remote,))]
    return pl.pallas_call(
        body, name=name, in_specs=[ANY] * len(ins), out_specs=[ANY] * len(out_shapes), out_shape=out_shapes,
        scratch_shapes=scratch, input_output_aliases=aliases or {})(*ins)


def _cast_place(w, l, chip_idx):
    _, R, C = w.shape
    tr = R // 2

    def body(me_ref, w_ref, o_ref):
        o_ref[...] = w_ref[...].astype(BF16)

    return _call(body, name=f"cast_place_l{l}", grid=(R // tr,), prefetch=1,
                 in_specs=[pl.BlockSpec((None, tr, C), lambda i, me_ref: (l, i, 0))],
                 out_specs=pl.BlockSpec((None, tr, C), lambda i, me_ref: (me_ref[0], i, 0)),
                 out_shape=_sds((N_CHIPS, R, C), BF16), sem=("arbitrary",))(chip_idx, w)


def _in_place_rider(bufs, pairs, per_buf=3):
    n = len(bufs)

    def copies(outs, send, recv, side):
        return [pltpu.make_async_remote_copy(src_ref=pair[side][0], dst_ref=pair[side][0], send_sem=send.at[k],
                                             recv_sem=recv.at[k], device_id=pair[side][1], device_id_type=MESH)
                for k, pair in enumerate(pairs(outs))]

    def start(ins, outs, send, recv):
        for cp in copies(outs, send, recv, 0):
            cp.start()

    def wait(ins, outs, send, recv):
        for cp in copies(outs, send, recv, 1):
            cp.wait_recv()
        for cp in copies(outs, send, recv, 0):
            cp.wait_send()

    return Rider(list(bufs), [_sds(b.shape, b.dtype) for b in bufs], per_buf * n, start, wait,
                 {a: a for a in range(n)})


def _gather_ici_rider(bufs):
    def pairs(outs):
        x, y, c, others = _place()
        return [((_half(o.at[2 * x + y], c), (cx, cy, c)), (_half(o.at[2 * cx + cy], c), (cx, cy, c)))
                for o in outs for cx, cy in others]
    return _in_place_rider(bufs, pairs)


def _gather_d2d_rider(bufs):
    def pairs(outs):
        x, y, c, others = _place()
        sib = (x, y, 1 - c)
        return [((_half(o.at[2 * cx + cy], c), sib), (_half(o.at[2 * cx + cy], 1 - c), sib))
                for o in outs for cx, cy in others]
    return _in_place_rider(bufs, pairs)


def _swap_rider(gs):
    n = len(gs)

    def copies(ins, outs, send, recv):
        x, y, c, _ = _place()
        return [pltpu.make_async_remote_copy(src_ref=_half(ins[a], 1 - c), dst_ref=outs[a], send_sem=send.at[a],
                                             recv_sem=recv.at[a], device_id=(x, y, 1 - c), device_id_type=MESH)
                for a in range(n)]

    def start(ins, outs, send, recv):
        for cp in copies(ins, outs, send, recv):
            cp.start()

    def wait(ins, outs, send, recv):
        for cp in copies(ins, outs, send, recv):
            cp.wait()

    outs = [_sds((g.shape[0], g.shape[1] // 2, g.shape[2]), g.dtype) for g in gs]
    return Rider(list(gs), outs, n, start, wait, {})


def _scatter_rider(ss):
    n = len(ss)

    def copies(ins, outs, send, recv):
        x, y, c, others = _place()
        return [pltpu.make_async_remote_copy(
            src_ref=ins[a].at[2 * cx + cy], dst_ref=outs[a].at[j], send_sem=send.at[3 * a + j],
            recv_sem=recv.at[3 * a + j], device_id=(cx, cy, c), device_id_type=MESH)
            for a in range(n) for j, (cx, cy) in enumerate(others)]

    def start(ins, outs, send, recv):
        for cp in copies(ins, outs, send, recv):
            cp.start()

    def wait(ins, outs, send, recv):
        for cp in copies(ins, outs, send, recv):
            cp.wait()

    outs = [_sds((N_CHIPS - 1,) + s.shape[1:], s.dtype) for s in ss]
    return Rider(list(ss), outs, 3 * n, start, wait, {})


def _run_rider(rider, name):
    n_in = len(rider.ins)

    def body(*refs):
        ins, outs = refs[:n_in], refs[n_in:n_in + len(rider.out_shapes)]
        send, recv = refs[n_in + len(rider.out_shapes):]
        rider.start(ins, outs, send, recv)
        rider.wait(ins, outs, send, recv)

    return _comm_call(body, name, rider.ins, rider.out_shapes, rider.n_sem, aliases=rider.aliases)


def _join_rider(bufs):
    def pairs(outs):
        x, y, c, _ = _place()
        sib = (x, y, 1 - c)
        return [((_half(o, c), sib), (_half(o, 1 - c), sib)) for o in outs]
    return _in_place_rider(bufs, pairs, per_buf=1)


def _place_small(v, dev_idx):
    rows, n = v.shape

    def body(idx_ref, v_ref, o_ref):
        o_ref[...] = v_ref[...]

    return _call(body, name="place_small", grid=(1,), prefetch=1,
                 in_specs=[pl.BlockSpec((rows, n), lambda i, idx_ref: (0, 0))],
                 out_specs=pl.BlockSpec((None, rows, n), lambda i, idx_ref: (idx_ref[0], 0, 0)),
                 out_shape=_sds((8, rows, n), v.dtype), sem=("arbitrary",))(dev_idx, v)


def _small_ici_rider(buf):
    def pairs(outs):
        x, y, c, others = _place()
        peers = [(x, y, 1 - c)] + [(cx, cy, c) for cx, cy in others]
        return [((outs[0].at[4 * x + 2 * y + c], peer), (outs[0].at[4 * peer[0] + 2 * peer[1] + peer[2]], peer))
                for peer in peers]
    return _in_place_rider([buf], pairs, per_buf=4)


def _small_d2d_rider(buf):
    def pairs(outs):
        x, y, c, others = _place()
        sib = (x, y, 1 - c)
        return [((outs[0].at[4 * cx + 2 * cy + c], sib), (outs[0].at[4 * cx + 2 * cy + 1 - c], sib))
                for cx, cy in others]
    return _in_place_rider([buf], pairs)


def _add_half(g, p, c):
    _, R, C = g.shape
    half = R // 2

    def body(c_ref, g_ref, p_ref, o_ref):
        o_ref[...] = g_ref[...] + p_ref[...]

    blk = (None, half, C)
    return _call(body, name="add_half", grid=(N_CHIPS,), prefetch=1,
                 in_specs=[pl.BlockSpec(blk, lambda s, c_ref: (s, c_ref[0], 0)),
                           pl.BlockSpec(blk, lambda s, c_ref: (s, 0, 0))],
                 out_specs=pl.BlockSpec(blk, lambda s, c_ref: (s, 0, 0)),
                 out_shape=_sds((N_CHIPS, half, C), F32), sem=("arbitrary",))(c, g, p)


def _sum_owner(s, q, buf, l, me, c):
    _, half, C = s.shape
    tr = half // 2

    def body(me_ref, c_ref, s_ref, q0, q1, q2, buf_ref, o_ref):
        o_ref[...] = ((s_ref[...] + q0[...]) + q1[...]) + q2[...]

    blk = (None, tr, C)
    qspec = lambda j: pl.BlockSpec(blk, lambda i, me_ref, c_ref: (j, i, 0))
    return _call(body, name=f"sum_owner_l{l}", grid=(half // tr,), prefetch=2,
                 in_specs=[pl.BlockSpec(blk, lambda i, me_ref, c_ref: (me_ref[0], i, 0)),
                           qspec(0), qspec(1), qspec(2), ANY],
                 out_specs=pl.BlockSpec(blk, lambda i, me_ref, c_ref: (l, 2 * c_ref[0] + i, 0)),
                 out_shape=_sds(buf.shape, F32), sem=("arbitrary",), aliases={6: 0})(me, c, s, q, q, q, buf)


def _adamw_math(w, g, m, v):
    m = ADAM_B1 * m + (1.0 - ADAM_B1) * g
    v = ADAM_B2 * v + (1.0 - ADAM_B2) * (g * g)
    m_hat = m / (1.0 - ADAM_B1 ** ADAM_STEP)
    v_hat = v / (1.0 - ADAM_B2 ** ADAM_STEP)
    delta = -ADAM_LR * (m_hat / (jnp.sqrt(v_hat) + ADAM_EPS) + ADAM_WD * w)
    return delta, m, v


def _adamw(w, g, m, v, rider=None):
    depth, R, C = w.shape
    tr = max(t for t in range(SUBLANES, R + 1, SUBLANES) if R % t == 0 and t * C * 4 <= 2 * 1024 * 1024)

    def body(w_ref, g_ref, m_ref, v_ref, d_ref, nm_ref, nv_ref):
        d, nm, nv = _adamw_math(w_ref[...], g_ref[...], m_ref[...], v_ref[...])
        d_ref[...] = d
        nm_ref[...] = nm
        nv_ref[...] = nv

    spec = pl.BlockSpec((None, tr, C), lambda l, i: (l, i, 0))
    return _call(body, name="adamw", grid=(depth, R // tr), in_specs=[spec] * 4, out_specs=[spec] * 3,
                 out_shape=[_sds(w.shape, F32)] * 3, sem=("parallel", "parallel"), rider=rider)(w, g, m, v)


def _small_update(gathered, w, m, v):
    _, rows, n = gathered.shape
    tr = rows // 7

    def body(ga_ref, w_ref, m_ref, v_ref, g_ref, d_ref, nm_ref, nv_ref):
        g = ga_ref[0]
        for k in range(1, 8):
            g = g + ga_ref[k]
        d, nm, nv = _adamw_math(w_ref[...], g, m_ref[...], v_ref[...])
        g_ref[...] = g
        d_ref[...] = d
        nm_ref[...] = nm
        nv_ref[...] = nv

    spec = pl.BlockSpec((tr, n), lambda i: (i, 0))
    return _call(body, name="small_update", grid=(rows // tr,),
                 in_specs=[pl.BlockSpec((8, tr, n), lambda i: (0, i, 0)), spec, spec, spec], out_specs=[spec] * 4,
                 out_shape=[_sds((rows, n), F32)] * 4, sem=("parallel",))(gathered, w, m, v)


WEIGHTS = ("g_mix", "w_in", "g_q", "g_k", "w_attn_proj", "lambda_re", "lambda_im", "log_dt", "b_re", "b_im",
           "c_re", "c_im", "d_skip", "w_glu_a", "w_glu_b", "w_out", "g_ffn", "w_ffn_gate", "w_ffn_up", "w_ffn_down")
BIG = ("w_in", "w_attn_proj", "w_glu_a", "w_glu_b", "w_out", "w_ffn_gate", "w_ffn_up", "w_ffn_down")
FLIPPED = ("w_ffn_gate", "w_ffn_up")
SMALL = tuple(n for n in WEIGHTS if n not in BIG)
ROW_VECTORS = ("g_mix", "g_q", "g_k", "d_skip", "g_ffn")
PACK_QUANTUM = LANES * SUBLANES * 7


def _pack_small(parts, extra):
    flat = jnp.concatenate([parts[n].reshape(-1).astype(F32) for n in SMALL] + [extra.reshape(-1)])
    pad = -flat.shape[0] % PACK_QUANTUM
    return jnp.pad(flat, (0, pad)).reshape(-1, LANES)


def _unpack_small(packed, like):
    flat = packed.reshape(-1)
    out, at = {}, 0
    for n in SMALL:
        size = math.prod(like[n].shape)
        out[n] = flat[at:at + size].reshape(like[n].shape)
        at += size
    return out, flat[at]


def kernel(x, g_mix, w_in, g_q, g_k, w_attn_proj, lambda_re, lambda_im, log_dt, b_re, b_im, c_re, c_im, d_skip, w_glu_a, w_glu_b, w_out, g_ffn, w_ffn_gate, w_ffn_up, w_ffn_down, loss_target, m_g_mix, m_w_in, m_g_q, m_g_k, m_w_attn_proj, m_lambda_re, m_lambda_im, m_log_dt, m_b_re, m_b_im, m_c_re, m_c_im, m_d_skip, m_w_glu_a, m_w_glu_b, m_w_out, m_g_ffn, m_w_ffn_gate, m_w_ffn_up, m_w_ffn_down, v_g_mix, v_w_in, v_g_q, v_g_k, v_w_attn_proj, v_lambda_re, v_lambda_im, v_log_dt, v_b_re, v_b_im, v_c_re, v_c_im, v_d_skip, v_w_glu_a, v_w_glu_b, v_w_out, v_g_ffn, v_w_ffn_gate, v_w_ffn_up, v_w_ffn_down):
    given = dict(locals())
    flip = lambda n, a: jnp.swapaxes(a, 1, 2) if n in FLIPPED else a
    W = {n: flip(n, given[n]) for n in WEIGHTS}
    M = {n: flip(n, given["m_" + n]) for n in WEIGHTS}
    V = {n: flip(n, given["v_" + n]) for n in WEIGHTS}
    depth = g_mix.shape[0]
    xl = x.reshape(x.shape[-2:])
    target = loss_target.reshape(loss_target.shape[-2:])
    c_idx = lax.axis_index("c").astype(jnp.int32).reshape(1)
    chip_idx = (2 * lax.axis_index("x") + lax.axis_index("y")).astype(jnp.int32).reshape(1)

    place = lambda l: [_cast_place(W[n], l, chip_idx) for n in BIG]
    bufs = place(0)
    w_in = _run_rider(_gather_d2d_rider(_run_rider(_gather_ici_rider(bufs[:1]), "gather_ici")), "gather_d2d")[0]
    rest, stage = bufs[1:], "ici"
    params, saved, h = [], [], xl
    for l in range(depth):
        p = {"w_in": w_in}
        for n in SMALL:
            p[n] = W[n][l][None] if n in ROW_VECTORS else W[n][l]
        h, sv, p, nxt = _layer_fwd(h, p, rest, stage, place(l + 1) if l + 1 < depth else None)
        params.append(p)
        saved.append(sv)
        if nxt:
            (w_in, rest), stage = nxt, "d2d"
    dx, loss_part = _loss_head(h, target)

    owned = {n: lax.empty(W[n].shape, F32) for n in BIG}
    small_grads = [None] * depth
    pending = None
    for l in reversed(range(depth)):
        dx, small_grads[l], pending, owned = _layer_bwd(dx, saved[l], params[l], pending, owned, l,
                                                        (chip_idx, c_idx))

    ct = {k: jnp.stack([small_grads[l]["ssm_pack_ct"][k] for l in range(depth)])
          for k in small_grads[0]["ssm_pack_ct"]}
    ct["a_re"], ct["a_im"] = jnp.sum(ct["a_re"], axis=2), jnp.sum(ct["a_im"], axis=2)
    ct["a64_re"] = ct["a64_im"] = jnp.zeros_like(ct["a_re"])
    _, pull = jax.vjp(jax.vmap(_ssm_pack), *[W[n] for n in SSM_PARAMS])
    stacked = dict(zip(SSM_PARAMS, pull(ct)))
    for n in SMALL:
        if n not in stacked:
            stacked[n] = jnp.stack([small_grads[l][n] for l in range(depth)])
    zero = jnp.zeros((1,), F32)
    dev_idx = (4 * lax.axis_index("x") + 2 * lax.axis_index("y") + lax.axis_index("c")).astype(jnp.int32).reshape(1)
    gathered = _place_small(_pack_small(stacked, loss_part), dev_idx)
    outs = _run_rider(_join_riders(_scatter_rider([pending[n] for n in LATE]), _small_ici_rider(gathered)),
                      "scatter_to_owners")
    for n, q in zip(LATE, outs[:len(LATE)]):
        owned[n] = _sum_owner(pending[n], q, owned[n], 0, chip_idx, c_idx)
    outs = _run_rider(_join_riders(_join_rider([owned[n] for n in BIG]), _small_d2d_rider(outs[len(LATE)])),
                      "join_halves")
    reduced, gathered = dict(zip(BIG, outs[:len(BIG)])), outs[len(BIG)]
    grads, delta, new_m, new_v = {}, {}, {}, {}
    for n in BIG:
        outs = (reduced[n], *_adamw(W[n], reduced[n], M[n], V[n]))
        grads[n], delta[n], new_m[n], new_v[n] = [flip(n, t) for t in outs]
    gs, ds, nms, nvs = _small_update(gathered, _pack_small(W, zero), _pack_small(M, zero), _pack_small(V, zero))
    sg, loss = _unpack_small(gs, W)
    sd, _ = _unpack_small(ds, W)
    sm, _ = _unpack_small(nms, W)
    sv_, _ = _unpack_small(nvs, W)
    for n in SMALL:
        grads[n], delta[n], new_m[n], new_v[n] = sg[n], sd[n], sm[n], sv_[n]

    return (loss, dx.reshape(x.shape), *[grads[n] for n in WEIGHTS], *[delta[n] for n in WEIGHTS],
            *[new_m[n] for n in WEIGHTS], *[new_v[n] for n in WEIGHTS])
```

```python
import collections
import functools
import math

import jax
import jax.numpy as jnp
from jax import lax
from jax.experimental import pallas as pl
from jax.experimental.pallas import tpu as pltpu

F32 = jnp.float32
BF16 = jnp.bfloat16

D_MODEL = 1024
DEPTH = 4
HEAD_DIM = 64
N_HEADS = 8
ATTN_WIDTH = N_HEADS * HEAD_DIM
ATTN_PATTERN = ((128, 1), (512, 4), (2048, 16))
N_GROUPS = len(ATTN_PATTERN)
BLK = 128
SSM_WIDTH = 512
SSM_GROUP = 16
SSM_GROUPS = 32
SSM_STATE = 64
D_FF = 2816
IN_COLS = 7168
EPS = 1e-6
ADAM_LR, ADAM_B1, ADAM_B2, ADAM_EPS, ADAM_WD, ADAM_STEP = 0.001, 0.9, 0.999, 1e-08, 0.01, 10

N_CHIPS = 4
MESH = pl.DeviceIdType.MESH

LANES = 128
SUBLANES = 8
VMEM_LIMIT = 56 * 1024 * 1024

TM = 512
TM_PROJ = 1024
TL_WGRAD = 2048
TM_MIX = 512

SSM_TB = 512
SSM_TC = 64
SSM_SUB = SUBLANES
SSM_PITCH = 68
N_SLAB = SSM_GROUPS * SSM_STATE // LANES
SSM_WIN = 256
N_PAIR = N_SLAB // 2
PAIRS_PER_WIN = 4
SCAN_GROUP = 4


def _params(sem=None, collective=False):
    return pltpu.CompilerParams(dimension_semantics=sem, vmem_limit_bytes=VMEM_LIMIT)


ANY = pl.BlockSpec(memory_space=pl.ANY)

Rider = collections.namedtuple("Rider", "ins out_shapes n_sem start wait aliases")


class _SemWindow:
    def __init__(self, ref, offset):
        self.ref, self.offset = ref, offset

    @property
    def at(self):
        return self

    def __getitem__(self, k):
        return self.ref.at[self.offset + k]


def _join_riders(*riders):
    riders = [r for r in riders if r is not None]
    if len(riders) <= 1:
        return riders[0] if riders else None

    def each(fn_name):
        def run(ins, outs, send, recv):
            i = o = s = 0
            for r in riders:
                getattr(r, fn_name)(ins[i:i + len(r.ins)], outs[o:o + len(r.out_shapes)],
                                    _SemWindow(send, s), _SemWindow(recv, s))
                i, o, s = i + len(r.ins), o + len(r.out_shapes), s + r.n_sem
        return run

    aliases, i, o = {}, 0, 0
    for r in riders:
        aliases.update({i + a: o + b for a, b in r.aliases.items()})
        i, o = i + len(r.ins), o + len(r.out_shapes)
    return Rider([t for r in riders for t in r.ins], [t for r in riders for t in r.out_shapes],
                 sum(r.n_sem for r in riders), each("start"), each("wait"), aliases)


def _with_rider(body, rider, grid, prefetch, n_in, n_out, n_scratch):
    n_rin, n_rout = len(rider.ins), len(rider.out_shapes)

    def hosted(*refs):
        pre, rest = refs[:prefetch], refs[prefetch:]
        ins, rin = rest[:n_in], rest[n_in:n_in + n_rin]
        o0 = n_in + n_rin
        outs, rout = rest[o0:o0 + n_out], rest[o0 + n_out:o0 + n_out + n_rout]
        s0 = o0 + n_out + n_rout
        scr, (send, recv) = rest[s0:s0 + n_scratch], rest[s0 + n_scratch:]
        first = functools.reduce(jnp.logical_and, [pl.program_id(k) == 0 for k in range(len(grid))])
        last = functools.reduce(jnp.logical_and, [pl.program_id(k) == grid[k] - 1 for k in range(len(grid))])

        @pl.when(first)
        def _():
            rider.start(rin, rout, send, recv)

        body(*pre, *ins, *outs, *scr)

        @pl.when(last)
        def _():
            rider.wait(rin, rout, send, recv)

    return hosted


def _call(body, *, name, grid, in_specs, out_specs, out_shape, scratch=(), sem=None, aliases=None,
          prefetch=0, rider=None):
    if rider is not None:
        single = not isinstance(out_specs, (list, tuple))
        out_specs = [out_specs] if single else list(out_specs)
        out_shape = [out_shape] if single else list(out_shape)
        body = _with_rider(body, rider, grid, prefetch, len(in_specs), len(out_specs), len(scratch))
        aliases = dict(aliases or {})
        aliases.update({prefetch + len(in_specs) + k: len(out_specs) + v for k, v in rider.aliases.items()})
        in_specs = list(in_specs) + [ANY] * len(rider.ins)
        out_specs = out_specs + [ANY] * len(rider.out_shapes)
        out_shape = out_shape + list(rider.out_shapes)
        scratch = list(scratch) + [pltpu.SemaphoreType.DMA((rider.n_sem,)), pltpu.SemaphoreType.DMA((rider.n_sem,))]
        sem = ("arbitrary",) * len(grid)
        fn = _call(body, name=name + "_host", grid=grid, in_specs=in_specs, out_specs=out_specs, out_shape=out_shape,
                   scratch=scratch, sem=sem, aliases=aliases, prefetch=prefetch)
        return lambda *args: fn(*args, *rider.ins)
    kw = {}
    if aliases:
        kw["input_output_aliases"] = aliases
    if prefetch:
        gs = pltpu.PrefetchScalarGridSpec(num_scalar_prefetch=prefetch, grid=grid, in_specs=in_specs,
                                          out_specs=out_specs, scratch_shapes=list(scratch))
        return pl.pallas_call(body, name=name, grid_spec=gs, out_shape=out_shape,
                              compiler_params=_params(sem), **kw)
    return pl.pallas_call(body, name=name, grid=grid, in_specs=in_specs, out_specs=out_specs,
                          out_shape=out_shape, scratch_shapes=list(scratch),
                          compiler_params=_params(sem), **kw)


def _sds(shape, dtype):
    return jax.ShapeDtypeStruct(shape, dtype)


def _sigmoid(v):
    return 0.5 * jnp.tanh(0.5 * v) + 0.5


def _dot(a, b):
    return jnp.dot(a, b, preferred_element_type=F32)


def _dot_nt(a, b):
    return lax.dot_general(a, b, (((1,), (1,)), ((), ())), preferred_element_type=F32)


def _dot_tn(a, b):
    return lax.dot_general(a, b, (((0,), (0,)), ((), ())), preferred_element_type=F32)


def _in_proj_fwd(x, g, w, rider=None):
    L = x.shape[0]
    ns = w.shape[2]
    tn = ns
    nj = ns // tn
    TM = TM_PROJ

    def body(x_ref, g_ref, w_ref, z_ref, h_ref):
        @pl.when(pl.program_id(1) == 0)
        def _():
            xv = x_ref[...]
            r = lax.rsqrt(jnp.mean(xv * xv, axis=-1, keepdims=True) + EPS)
            h_ref[...] = (xv * r * g_ref[...]).astype(BF16)
        z_ref[...] = _dot(h_ref[...], w_ref[...]).astype(BF16)

    return _call(
        body, name="in_proj_fwd", grid=(L // TM, N_CHIPS * nj),
        in_specs=[pl.BlockSpec((TM, D_MODEL), lambda i, j: (i, 0)),
                  pl.BlockSpec((1, D_MODEL), lambda i, j: (0, 0)),
                  pl.BlockSpec((None, D_MODEL, tn), lambda i, j: (j // nj, 0, j % nj))],
        out_specs=[pl.BlockSpec((TM, tn), lambda i, j: (i, j)),
                   pl.BlockSpec((TM, D_MODEL), lambda i, j: (i, 0))],
        out_shape=[_sds((L, N_CHIPS * ns), BF16), _sds((L, D_MODEL), BF16)],
        sem=("parallel", "arbitrary"), rider=rider)(x, g, w)


DL_TILE = 512
SCALE = HEAD_DIM ** -0.5


def _perm_matrix(d):
    rho = jnp.arange(DL_TILE)
    src = rho // (DL_TILE // d) + d * (rho % (DL_TILE // d))
    return (src[:, None] == jnp.arange(DL_TILE)[None, :]).astype(BF16)


def _head_sum_matrix():
    h = jnp.arange(ATTN_WIDTH) // HEAD_DIM
    return (h[:, None] == h[None, :]).astype(BF16)


def _split(v):
    hi = v.astype(BF16)
    return hi, (v - hi.astype(F32)).astype(BF16)


def _head_sum(v, hs):
    vb = v.astype(BF16)
    half = ATTN_WIDTH // 2
    blk = hs[:half, :half]
    return jnp.concatenate([_dot(vb[:, :half], blk), _dot(vb[:, half:], blk)], axis=1)


def _permute(pm, v):
    hi, lo = _split(v)
    return _dot(pm, hi) + _dot(pm, lo)


def _dl_view(t, d):
    if d * BLK <= DL_TILE:
        return t
    return t.reshape(t.shape[0] // DL_TILE, d, DL_TILE // d, t.shape[1])


def _dl_spec(d, width, which):
    if d * BLK <= DL_TILE:
        per_tile = DL_TILE // (d * BLK)
        return pl.BlockSpec((BLK, width), lambda r, n: ((which(n) // per_tile) * (DL_TILE // BLK)
                                                       + r * per_tile + which(n) % per_tile, 0))
    tiles = d * BLK // DL_TILE
    return pl.BlockSpec((tiles, None, DL_TILE // d, width), lambda r, n: (which(n), r, 0, 0))


def _dl_read(ref):
    v = ref[...]
    return v if v.ndim == 2 else v.reshape(BLK, v.shape[-1])


def _dl_write(ref, v):
    ref[...] = v if len(ref.shape) == 2 else v.reshape(ref.shape)


def _qkv_prep(z, gq_t, gk_t, rider=None):
    L = z.shape[0]
    qkv_w = N_GROUPS * ATTN_WIDTH

    def body(zq_ref, zk_ref, zv_ref, gq_ref, gk_ref, hs_ref, p1_ref, p2_ref, *outs):
        hs = hs_ref[...]
        perms = (None, p1_ref[...], p2_ref[...])
        for g in range(N_GROUPS):
            cols = slice(g * ATTN_WIDTH, (g + 1) * ATTN_WIDTH)
            xq = zq_ref[:, cols].astype(F32)
            xk = zk_ref[:, cols].astype(F32)
            rq = lax.rsqrt(_head_sum(xq * xq, hs) * (1.0 / HEAD_DIM) + EPS)
            rk = lax.rsqrt(_head_sum(xk * xk, hs) * (1.0 / HEAD_DIM) + EPS)
            vals = [(xq * rq * (gq_ref[...] * SCALE)).astype(BF16), (xk * rk * gk_ref[...]).astype(BF16),
                    zv_ref[:, cols]]
            for j, t in enumerate(vals):
                if perms[g] is not None:
                    t = _dot(perms[g], t).astype(BF16)
                outs[3 * g + j][...] = t

    tile = pl.BlockSpec((DL_TILE, ATTN_WIDTH), lambda i: (i, 0))
    mat = pl.BlockSpec((DL_TILE, DL_TILE), lambda i: (0, 0))
    vec = pl.BlockSpec((1, ATTN_WIDTH), lambda i: (0, 0))
    outs = _call(
        body, name="qkv_prep", grid=(L // DL_TILE,),
        in_specs=[pl.BlockSpec((DL_TILE, qkv_w), lambda i: (i, 0)), pl.BlockSpec((DL_TILE, qkv_w), lambda i: (i, 1)),
                  pl.BlockSpec((DL_TILE, qkv_w), lambda i: (i, 2)), vec, vec, mat, mat, mat],
        out_specs=[tile] * 9, out_shape=[_sds((L, ATTN_WIDTH), BF16)] * 9,
        sem=("parallel",), rider=rider)(z, z, z, gq_t, gk_t, _head_sum_matrix(), _perm_matrix(ATTN_PATTERN[1][1]),
                                        _perm_matrix(ATTN_PATTERN[2][1]))
    return [tuple(outs[3 * g:3 * g + 3]) for g in range(N_GROUPS)], list(outs[3 * N_GROUPS:])


def _pair_masks():
    lane = lax.broadcasted_iota(jnp.int32, (1, LANES), 1)
    return lane < HEAD_DIM, lane >= HEAD_DIM


def _attn_fwd(qs, ks, v, gi):
    L = qs.shape[0]
    _, d = ATTN_PATTERN[gi]
    nb = L // (d * BLK)

    def body(q_ref, kc_ref, kp_ref, vc_ref, vp_ref, o_ref, l_ref):
        n = pl.program_id(1)
        qi = lax.broadcasted_iota(jnp.int32, (BLK, 2 * BLK), 0)
        kj = lax.broadcasted_iota(jnp.int32, (BLK, 2 * BLK), 1)
        prev = kj < BLK
        mask = jnp.logical_and(jnp.where(prev, kj, qi) >= jnp.where(prev, qi, kj - BLK),
                               kj >= jnp.where(n > 0, 0, BLK))
        q = _dl_read(q_ref)
        kw = jnp.concatenate([_dl_read(kp_ref), _dl_read(kc_ref)], axis=0)
        vw = jnp.concatenate([_dl_read(vp_ref), _dl_read(vc_ref)], axis=0)
        one = jnp.ones((2 * BLK, LANES), BF16)
        o_parts, l_parts = [], []
        for hp in range(N_HEADS // 2):
            ls = slice(hp * LANES, (hp + 1) * LANES)
            qp, kp_, vp_ = q[:, ls], kw[:, ls], vw[:, ls]
            num = jnp.zeros((BLK, LANES), F32)
            den = jnp.zeros((BLK, LANES), F32)
            mb = jnp.zeros((BLK, LANES), F32)
            for he in _pair_masks():
                s = jnp.where(mask, _dot_nt(jnp.where(he, qp, 0), kp_), -jnp.inf)
                m = jnp.max(s, axis=-1, keepdims=True)
                p = jnp.exp(s - m).astype(BF16)
                acc = _dot(p, jnp.concatenate([jnp.where(he, vp_, 0), jnp.where(he, one, 0)], axis=1))
                num += acc[:, :LANES]
                den += acc[:, LANES:]
                mb = jnp.where(he, m, mb)
            o_parts.append((num / den).astype(BF16))
            l_parts.append(mb + jnp.log(den))
        _dl_write(o_ref, jnp.concatenate(o_parts, axis=1))
        _dl_write(l_ref, jnp.concatenate(l_parts, axis=1))

    cur = _dl_spec(d, ATTN_WIDTH, lambda n: n)
    prev = _dl_spec(d, ATTN_WIDTH, lambda n: jnp.maximum(n - 1, 0))
    view = lambda t: _dl_view(t, d)
    o, l = _call(
        body, name=f"attn_fwd_g{gi}", grid=(d, nb), in_specs=[cur, cur, prev, cur, prev], out_specs=[cur, cur],
        out_shape=[_sds(view(qs).shape, BF16), _sds(view(qs).shape, F32)],
        sem=("parallel", "parallel"))(view(qs), view(ks), view(ks), view(v), view(v))
    return o.reshape(L, ATTN_WIDTH), l.reshape(L, ATTN_WIDTH)


def _to_token_order(os_, ls_, pts):
    o_tok, l_tok = [], []
    for o, l, pt in zip(os_, ls_, pts):
        if pt is None:
            o_tok.append(o.astype(F32))
            l_tok.append(l)
        else:
            o_tok.append(_dot(pt, o))
            l_tok.append(_permute(pt, l))
    return o_tok, l_tok


def _combine_fwd(os_, ls_):
    L = os_[0].shape[0]

    def body(o0, o1, o2, l0, l1, l2, pt1_ref, pt2_ref, a_ref):
        o_tok, l_tok = _to_token_order((o0[...], o1[...], o2[...]), (l0[...], l1[...], l2[...]),
                                       (None, pt1_ref[...], pt2_ref[...]))
        w = _combine_weights(*l_tok)
        a_ref[...] = (w[0] * o_tok[0] + w[1] * o_tok[1] + w[2] * o_tok[2]).astype(BF16)

    tile = pl.BlockSpec((DL_TILE, ATTN_WIDTH), lambda i: (i, 0))
    mat = pl.BlockSpec((DL_TILE, DL_TILE), lambda i: (0, 0))
    return _call(body, name="combine_fwd", grid=(L // DL_TILE,), in_specs=[tile] * 6 + [mat, mat], out_specs=tile,
                 out_shape=_sds((L, ATTN_WIDTH), BF16), sem=("parallel",))(
                     *os_, *ls_, _perm_matrix(ATTN_PATTERN[1][1]).T, _perm_matrix(ATTN_PATTERN[2][1]).T)


def _gelu(v):
    c = math.sqrt(2.0 / math.pi)
    return 0.5 * v * (1.0 + jnp.tanh(c * (v + 0.044715 * v * v * v)))


def _gelu_grad(v):
    c = math.sqrt(2.0 / math.pi)
    t = jnp.tanh(c * (v + 0.044715 * v * v * v))
    return 0.5 * (1.0 + t) + 0.5 * v * (1.0 - t * t) * c * (1.0 + 3.0 * 0.044715 * v * v)


def _ssm_fill(u, bwre_ref, bwim_ref, sre, sim):
    for k2 in range(N_PAIR):
        uw = u[:, _win_cols(k2)]
        _to_slabs(sre, k2, _dot(uw, bwre_ref[k2]))
        _to_slabs(sim, k2, _dot(uw, bwim_ref[k2]))


def _win_cols(k2):
    w = k2 // PAIRS_PER_WIN
    return slice(w * SSM_WIN, (w + 1) * SSM_WIN)


def _to_slabs(ref, k2, v):
    for half in range(2):
        for j in range(SSM_SUB):
            ref[2 * k2 + half, j * SSM_PITCH:j * SSM_PITCH + SSM_TC, :] = (
                v[j * SSM_TC:(j + 1) * SSM_TC, half * LANES:(half + 1) * LANES])


def _rows(i):
    return pl.ds(i, SSM_SUB, stride=SSM_PITCH)


def _slab_rows(ref, k):
    return jnp.concatenate([ref[k, j * SSM_PITCH:j * SSM_PITCH + SSM_TC, :] for j in range(SSM_SUB)], axis=0)


def _pair_rows(ref, k2):
    return jnp.concatenate([_slab_rows(ref, 2 * k2), _slab_rows(ref, 2 * k2 + 1)], axis=1).astype(BF16)


def _bcast(ref, k):
    return jnp.broadcast_to(ref[pl.ds(k, 1), :], (SSM_SUB, LANES))


def _scan(sre, sim, are_ref, aim_ref, k0, init, *, reverse, store, sign=1.0):
    ar = [_bcast(are_ref, k0 + kk) for kk in range(SCAN_GROUP)]
    ai = [sign * _bcast(aim_ref, k0 + kk) for kk in range(SCAN_GROUP)]

    def step(t, carry):
        i = SSM_TC - 1 - t if reverse else t
        out = []
        for kk in range(SCAN_GROUP):
            k = k0 + kk
            xr, xi = carry[2 * kk], carry[2 * kk + 1]
            nr = ar[kk] * xr - ai[kk] * xi + sre[k, _rows(i), :]
            ni = ar[kk] * xi + ai[kk] * xr + sim[k, _rows(i), :]
            if store:
                sre[k, _rows(i), :] = nr
                sim[k, _rows(i), :] = ni
            out += [nr, ni]
        return tuple(out)

    flat = []
    for re, im in init:
        flat += [re, im]
    res = lax.fori_loop(0, SSM_TC // 2, lambda t, c: step(2 * t + 1, step(2 * t, c)), tuple(flat))
    return [(res[2 * kk], res[2 * kk + 1]) for kk in range(SCAN_GROUP)]


def _ssm_seeds(ends_re, ends_im, a64re_ref, a64im_ref, carry_re, carry_im, seed_re, seed_im, k,
               *, reverse, sign=1.0):
    ar = a64re_ref[pl.ds(k, 1), :]
    ai = sign * a64im_ref[pl.ds(k, 1), :]
    cr = carry_re[pl.ds(k, 1), :]
    ci = carry_im[pl.ds(k, 1), :]
    order = range(SSM_SUB - 1, -1, -1) if reverse else range(SSM_SUB)
    for j in order:
        seed_re[k, pl.ds(j, 1), :] = cr
        seed_im[k, pl.ds(j, 1), :] = ci
        er = ends_re[k, pl.ds(j, 1), :]
        ei = ends_im[k, pl.ds(j, 1), :]
        cr, ci = ar * cr - ai * ci + er, ar * ci + ai * cr + ei
    carry_re[pl.ds(k, 1), :] = cr
    carry_im[pl.ds(k, 1), :] = ci


def _ssm_specs_consts():
    c2 = pl.BlockSpec((N_SLAB, LANES), lambda b: (0, 0))
    c3 = pl.BlockSpec((N_PAIR, SSM_WIN, SSM_WIN), lambda b: (0, 0, 0))
    return c2, c3


def _ssm_scratch():
    rows = SSM_SUB * SSM_PITCH
    return [pltpu.VMEM((N_SLAB, rows, LANES), F32), pltpu.VMEM((N_SLAB, rows, LANES), F32)]


def _ssm_fwd(z, pk, dskip, rider=None):
    L = z.shape[0]
    nb = L // SSM_TB
    ucol = (3 * N_GROUPS * ATTN_WIDTH) // SSM_WIDTH

    def body(u_ref, are_ref, aim_ref, a64re_ref, a64im_ref, bwre_ref, bwim_ref, cwre_ref, cwim_ref, d_ref,
             ypre_ref, yact_ref, sdre_ref, sdim_ref, sre, sim, carry_re, carry_im, ends_re, ends_im,
             seed_re, seed_im):
        @pl.when(pl.program_id(0) == 0)
        def _():
            carry_re[...] = jnp.zeros_like(carry_re)
            carry_im[...] = jnp.zeros_like(carry_im)

        u = u_ref[...]
        _ssm_fill(u, bwre_ref, bwim_ref, sre, sim)
        zero = jnp.zeros((SSM_SUB, LANES), F32)
        for k0 in range(0, N_SLAB, SCAN_GROUP):
            ends = _scan(sre, sim, are_ref, aim_ref, k0, [(zero, zero)] * SCAN_GROUP, reverse=False, store=False)
            for kk in range(SCAN_GROUP):
                ends_re[k0 + kk] = ends[kk][0]
                ends_im[k0 + kk] = ends[kk][1]
            for kk in range(SCAN_GROUP):
                _ssm_seeds(ends_re, ends_im, a64re_ref, a64im_ref, carry_re, carry_im, seed_re, seed_im,
                           k0 + kk, reverse=False)
            init = [(seed_re[k0 + kk], seed_im[k0 + kk]) for kk in range(SCAN_GROUP)]
            _scan(sre, sim, are_ref, aim_ref, k0, init, reverse=False, store=True)
        sdre_ref[...] = seed_re[...]
        sdim_ref[...] = seed_im[...]
        for w in range(N_PAIR // PAIRS_PER_WIN):
            acc = jnp.zeros((SSM_TB, SSM_WIN), F32)
            for kk in range(PAIRS_PER_WIN):
                k2 = w * PAIRS_PER_WIN + kk
                acc += _dot(_pair_rows(sre, k2), cwre_ref[k2])
                acc -= _dot(_pair_rows(sim, k2), cwim_ref[k2])
            cols = _win_cols(w * PAIRS_PER_WIN)
            ypre = acc + d_ref[:, cols] * u[:, cols].astype(F32)
            ypre_ref[:, cols] = ypre
            yact_ref[:, cols] = _gelu(ypre).astype(BF16)

    c2, c3 = _ssm_specs_consts()
    seed_spec = pl.BlockSpec((None, N_SLAB, SSM_SUB, LANES), lambda b: (b, 0, 0, 0))
    small = pltpu.VMEM((N_SLAB, LANES), F32)
    tile = pltpu.VMEM((N_SLAB, SSM_SUB, LANES), F32)
    return _call(
        body, name="ssm_fwd", grid=(nb,),
        in_specs=[pl.BlockSpec((SSM_TB, SSM_WIDTH), lambda b: (b, ucol)), c2, c2, c2, c2, c3, c3, c3, c3,
                  pl.BlockSpec((1, SSM_WIDTH), lambda b: (0, 0))],
        out_specs=[pl.BlockSpec((SSM_TB, SSM_WIDTH), lambda b: (b, 0)),
                   pl.BlockSpec((SSM_TB, SSM_WIDTH), lambda b: (b, 0)), seed_spec, seed_spec],
        out_shape=[_sds((L, SSM_WIDTH), F32), _sds((L, SSM_WIDTH), BF16),
                   _sds((nb, N_SLAB, SSM_SUB, LANES), F32), _sds((nb, N_SLAB, SSM_SUB, LANES), F32)],
        scratch=_ssm_scratch() + [small, small, tile, tile, tile, tile],
        sem=("arbitrary",), rider=rider)(z, pk["a_re"], pk["a_im"], pk["a64_re"], pk["a64_im"],
                                         pk["bw_re"].astype(BF16), pk["bw_im"].astype(BF16),
                                         pk["cw_re"].astype(BF16), pk["cw_im"].astype(BF16), dskip)


def _combine_weights(l0, l1, l2):
    m = jnp.maximum(jnp.maximum(l0, l1), l2)
    e0, e1, e2 = jnp.exp(l0 - m), jnp.exp(l1 - m), jnp.exp(l2 - m)
    inv = 1.0 / (e0 + e1 + e2)
    return e0 * inv, e1 * inv, e2 * inv


def _mix_fwd(x, z, a, yact, w_ap, w_ga, w_gb, w_out):
    L = x.shape[0]
    cs = D_MODEL // N_CHIPS
    ga_col = (3 * N_GROUPS * ATTN_WIDTH + SSM_WIDTH) // D_MODEL

    def body(x_ref, ga_ref, gs_ref, a_ref, y_ref, wap_ref, wga_ref, wgb_ref, wout_ref,
             x1_ref, aout_ref, sa_ref, sb_ref, mix_ref):
        a = a_ref[...]
        y = y_ref[...]
        for s in range(N_CHIPS):
            cols = slice(s * cs, (s + 1) * cs)
            aout_ref[:, cols] = _dot(a, wap_ref[s]).astype(BF16)
            sa_ref[:, cols] = _dot(y, wga_ref[s]).astype(BF16)
            sb_ref[:, cols] = _dot(y, wgb_ref[s]).astype(BF16)
        s_out = sa_ref[...].astype(F32) * _sigmoid(sb_ref[...].astype(F32))
        mix = (_sigmoid(ga_ref[...].astype(F32)) * aout_ref[...].astype(F32)
               + _sigmoid(gs_ref[...].astype(F32)) * s_out).astype(BF16)
        mix_ref[...] = mix
        x1_ref[...] = x_ref[...] + _dot(mix, wout_ref[...])

    tok = lambda w: pl.BlockSpec((TM_MIX, w), lambda i: (i, 0))
    wsm = pl.BlockSpec((N_CHIPS, ATTN_WIDTH, cs), lambda i: (0, 0, 0))
    return _call(
        body, name="mix_fwd", grid=(L // TM_MIX,),
        in_specs=[tok(D_MODEL), pl.BlockSpec((TM_MIX, D_MODEL), lambda i: (i, ga_col)),
                  pl.BlockSpec((TM_MIX, D_MODEL), lambda i: (i, ga_col + 1))]
                 + [tok(ATTN_WIDTH)] * 2 + [wsm, wsm, wsm, pl.BlockSpec((D_MODEL, D_MODEL), lambda i: (0, 0))],
        out_specs=[tok(D_MODEL), tok(D_MODEL), tok(D_MODEL), tok(D_MODEL), tok(D_MODEL)],
        out_shape=[_sds((L, D_MODEL), F32)] + [_sds((L, D_MODEL), BF16)] * 4,
        sem=("parallel",))(x, z, z, a, yact, w_ap, w_ga, w_gb, w_out.reshape(D_MODEL, D_MODEL))


def _ffn_fwd(x1, g, w_g, w_u, w_d, rider=None):
    L = x1.shape[0]
    fs = D_FF // N_CHIPS
    TM = TM_PROJ

    def body(x_ref, g_ref, wg_ref, wu_ref, wd_ref, x2_ref, h_ref, gate_ref, up_ref, act_ref, acc):
        s = pl.program_id(1)

        @pl.when(s == 0)
        def _():
            xv = x_ref[...]
            r = lax.rsqrt(jnp.mean(xv * xv, axis=-1, keepdims=True) + EPS)
            h_ref[...] = (xv * r * g_ref[...]).astype(BF16)
            acc[...] = jnp.zeros_like(acc)

        h = h_ref[...]
        gate = _dot_nt(h, wg_ref[...])
        up = _dot_nt(h, wu_ref[...])
        sg = _sigmoid(gate)
        silu = gate * sg
        act = (silu * up).astype(BF16)
        gate_ref[...] = (up * (sg * (1.0 + gate * (1.0 - sg)))).astype(BF16)
        up_ref[...] = silu.astype(BF16)
        act_ref[...] = act
        acc[...] += _dot(act, wd_ref[...])

        @pl.when(s == N_CHIPS - 1)
        def _():
            x2_ref[...] = x_ref[...] + acc[...]

    tok = pl.BlockSpec((TM, D_MODEL), lambda i, s: (i, 0))
    ffs = pl.BlockSpec((None, TM, fs), lambda i, s: (s, i, 0))
    return _call(
        body, name="ffn_fwd", grid=(L // TM, N_CHIPS),
        in_specs=[tok, pl.BlockSpec((1, D_MODEL), lambda i, s: (0, 0))]
                 + [pl.BlockSpec((None, fs, D_MODEL), lambda i, s: (s, 0, 0))] * 3,
        out_specs=[tok, tok, ffs, ffs, ffs],
        out_shape=[_sds((L, D_MODEL), F32), _sds((L, D_MODEL), BF16)] + [_sds((N_CHIPS, L, fs), BF16)] * 3,
        scratch=[pltpu.VMEM((TM, D_MODEL), F32)],
        sem=("parallel", "arbitrary"), rider=rider)(x1, g, w_g, w_u, w_d)


def _loss_head(xl, target):
    L = xl.shape[0]

    def body(x_ref, t_ref, dx_ref, loss_ref, acc):
        i = pl.program_id(0)

        @pl.when(i == 0)
        def _():
            acc[...] = jnp.zeros_like(acc)

        e = x_ref[...] - t_ref[...]
        dx_ref[...] = e * (1.0 / D_MODEL)
        acc[...] += jnp.sum((e * e).reshape(TM // SUBLANES, SUBLANES, D_MODEL), axis=0)

        @pl.when(i == pl.num_programs(0) - 1)
        def _():
            loss_ref[...] = (0.5 / D_MODEL) * jnp.sum(acc[...]).reshape(1, 1)

    tok = pl.BlockSpec((TM, D_MODEL), lambda i: (i, 0))
    return _call(
        body, name="loss_head", grid=(L // TM,), in_specs=[tok, tok],
        out_specs=[tok, pl.BlockSpec((1, 1), lambda i: (0, 0))],
        out_shape=[_sds((L, D_MODEL), F32), _sds((1, 1), F32)],
        scratch=[pltpu.VMEM((SUBLANES, D_MODEL), F32)], sem=("arbitrary",))(xl, target)


def _ssm_pack(lam_re, lam_im, log_dt, b_re, b_im, c_re, c_im):
    dt = jnp.exp(log_dt)[:, None]
    mag = jnp.exp(lam_re * dt)
    ang = lam_im * dt
    ar = mag * jnp.cos(ang)
    ai = mag * jnp.sin(ang)
    nr = ar - 1.0
    ni = ai
    den = lam_re * lam_re + lam_im * lam_im
    cr = ((nr * lam_re + ni * lam_im) / den)[..., None]
    ci = ((ni * lam_re - nr * lam_im) / den)[..., None]
    bbr = cr * b_re - ci * b_im
    bbi = cr * b_im + ci * b_re
    gpp = SSM_WIN // SSM_STATE
    gpw = SSM_WIN // SSM_GROUP
    k2 = jnp.arange(N_PAIR)[:, None, None]
    gs = jnp.arange(gpp)[None, :, None]
    gl = jnp.arange(gpw)[None, None, :]
    same = (gl == gpp * (k2 % PAIRS_PER_WIN) + gs).astype(F32)

    def b_windows(bb):
        return jnp.einsum('kgl,kgpc->klcgp', same, bb.reshape(N_PAIR, gpp, SSM_STATE, SSM_GROUP)).reshape(
            N_PAIR, SSM_WIN, SSM_WIN)

    def c_windows(cc):
        return jnp.einsum('kgl,kgcp->kgplc', same, cc.reshape(N_PAIR, gpp, SSM_GROUP, SSM_STATE)).reshape(
            N_PAIR, SSM_WIN, SSM_WIN)

    pr, pi = ar, ai
    for _ in range(int(math.log2(SSM_TC))):
        pr, pi = pr * pr - pi * pi, 2.0 * pr * pi
    return dict(a_re=ar.reshape(N_SLAB, LANES), a_im=ai.reshape(N_SLAB, LANES),
                a64_re=pr.reshape(N_SLAB, LANES), a64_im=pi.reshape(N_SLAB, LANES),
                bw_re=b_windows(bbr), bw_im=b_windows(bbi), cw_re=c_windows(c_re), cw_im=c_windows(c_im))


def _layer_fwd(x, p, rest, rest_stage, next_bufs=None):
    first = {"ici": _gather_ici_rider, "d2d": _gather_d2d_rider}[rest_stage]
    outs = _in_proj_fwd(x, p["g_mix"], p["w_in"], first(rest))
    (z, h), rest = outs[:2], list(outs[2:])
    qkv, got = _qkv_prep(z, jnp.tile(p["g_q"], (1, N_HEADS)), jnp.tile(p["g_k"], (1, N_HEADS)),
                         _gather_d2d_rider(rest) if rest_stage == "ici" else None)
    p = {**p, **dict(zip(BIG[1:], got if rest_stage == "ici" else rest))}
    os_, ls_ = [], []
    for gi in range(N_GROUPS):
        o, l = _attn_fwd(*qkv[gi], gi)
        os_.append(o)
        ls_.append(l)
    a = _combine_fwd(os_, ls_)
    pk = _ssm_pack(p["lambda_re"], p["lambda_im"], p["log_dt"], p["b_re"], p["b_im"], p["c_re"], p["c_im"])
    outs = _ssm_fwd(z, pk, p["d_skip"], _gather_ici_rider(next_bufs[:1]) if next_bufs else None)
    (ypre, yact, sd_re, sd_im), next_in = outs[:4], list(outs[4:])
    x1, aout, sa, sb, mix = _mix_fwd(x, z, a, yact, p["w_attn_proj"], p["w_glu_a"], p["w_glu_b"], p["w_out"])
    outs = _ffn_fwd(x1, p["g_ffn"], p["w_ffn_gate"], p["w_ffn_up"], p["w_ffn_down"],
                    _join_riders(_gather_ici_rider(next_bufs[1:]), _gather_d2d_rider(next_in)) if next_bufs else None)
    x2, h2, gate, up, act = outs[:5]
    nxt = (outs[-1], list(outs[5:-1])) if next_bufs else None
    saved = dict(x=x, z=z, h=h, qkv=qkv, os=os_, ls=ls_, pk=pk, ypre=ypre, yact=yact, sd_re=sd_re, sd_im=sd_im,
                 x1=x1, a=a, aout=aout, sa=sa, sb=sb, mix=mix, h2=h2, gate=gate, up=up, act=act)
    return x2, saved, p, nxt


def _rms_bwd(xv, g, dh):
    r = lax.rsqrt(jnp.mean(xv * xv, axis=-1, keepdims=True) + EPS)
    xn = xv * r
    dxn = dh * g
    dx = r * (dxn - xn * jnp.mean(dxn * xn, axis=-1, keepdims=True))
    dg = jnp.sum((dh * xn).reshape(xv.shape[0] // SUBLANES, SUBLANES, xv.shape[1]), axis=0)
    return dx, dg


def _ffn_bwd_act(dx2, gate, up, w_d):
    L = dx2.shape[0]
    fs = D_FF // N_CHIPS
    TM = TM_PROJ

    def body(dx_ref, dact_dgate_ref, dact_dup_ref, wd_ref, dgate_ref, dup_ref, dxb_ref):
        @pl.when(pl.program_id(1) == 0)
        def _():
            dxb_ref[...] = dx_ref[...].astype(BF16)

        dact = _dot_nt(dxb_ref[...], wd_ref[...])
        dgate_ref[...] = (dact * dact_dgate_ref[...].astype(F32)).astype(BF16)
        dup_ref[...] = (dact * dact_dup_ref[...].astype(F32)).astype(BF16)

    ffs = pl.BlockSpec((None, TM, fs), lambda i, s: (s, i, 0))
    tok = pl.BlockSpec((TM, D_MODEL), lambda i, s: (i, 0))
    return _call(
        body, name="ffn_bwd_act", grid=(L // TM, N_CHIPS),
        in_specs=[tok, ffs, ffs, pl.BlockSpec((None, fs, D_MODEL), lambda i, s: (s, 0, 0))],
        out_specs=[ffs, ffs, tok], out_shape=[_sds((N_CHIPS, L, fs), BF16)] * 2 + [_sds((L, D_MODEL), BF16)],
        sem=("parallel", "arbitrary"))(dx2, gate, up, w_d)


def _ffn_bwd_in(dx2, x1, g, dgate, dup, w_g, w_u, rider=None):
    L = x1.shape[0]
    fs = D_FF // N_CHIPS
    TM = TM_PROJ

    def body(dx_ref, x_ref, g_ref, dgate_ref, dup_ref, wg_ref, wu_ref, dx1_ref, dg_ref, acc, dgacc):
        i, s = pl.program_id(0), pl.program_id(1)

        @pl.when(s == 0)
        def _():
            acc[...] = jnp.zeros_like(acc)

        @pl.when(jnp.logical_and(i == 0, s == 0))
        def _():
            dgacc[...] = jnp.zeros_like(dgacc)

        acc[...] += _dot(dgate_ref[...], wg_ref[...]) + _dot(dup_ref[...], wu_ref[...])

        @pl.when(s == N_CHIPS - 1)
        def _():
            dx, dg = _rms_bwd(x_ref[...], g_ref[...], acc[...])
            dx1_ref[...] = dx_ref[...] + dx
            dgacc[...] += dg

        @pl.when(jnp.logical_and(i == pl.num_programs(0) - 1, s == N_CHIPS - 1))
        def _():
            dg_ref[...] = jnp.sum(dgacc[...], axis=0, keepdims=True)

    tok = pl.BlockSpec((TM, D_MODEL), lambda i, s: (i, 0))
    ffs = pl.BlockSpec((None, TM, fs), lambda i, s: (s, i, 0))
    vec = pl.BlockSpec((1, D_MODEL), lambda i, s: (0, 0))
    return _call(
        body, name="ffn_bwd_in", grid=(L // TM, N_CHIPS),
        in_specs=[tok, tok, vec, ffs, ffs,
                  pl.BlockSpec((None, fs, D_MODEL), lambda i, s: (s, 0, 0)),
                  pl.BlockSpec((None, fs, D_MODEL), lambda i, s: (s, 0, 0))],
        out_specs=[tok, vec],
        out_shape=[_sds((L, D_MODEL), F32), _sds((1, D_MODEL), F32)],
        scratch=[pltpu.VMEM((TM, D_MODEL), F32), pltpu.VMEM((SUBLANES, D_MODEL), F32)],
        sem=("arbitrary", "arbitrary"), rider=rider)(dx2, x1, g, dgate, dup, w_g, w_u)


def _wgrad(a, b, *, name, grid_kn, a_spec, b_spec, out_shape, out_spec):
    L = a.shape[-2]
    nl = L // TL_WGRAD

    def body(a_ref, b_ref, o_ref):
        @pl.when(pl.program_id(2) == 0)
        def _():
            o_ref[...] = jnp.zeros_like(o_ref)
        o_ref[...] += _dot_tn(a_ref[...].astype(BF16), b_ref[...].astype(BF16))

    return _call(body, name=name, grid=(*grid_kn, nl), in_specs=[a_spec, b_spec], out_specs=out_spec,
                 out_shape=out_shape, sem=("parallel", "parallel", "arbitrary"))(a, b)


def _wgrad_cols(a, b, name):
    K, N = a.shape[1], b.shape[1]
    ns = N // N_CHIPS
    if N * K * 4 <= 4 * 1024 * 1024:
        L = a.shape[0]

        def body(a_ref, b_ref, o_ref):
            @pl.when(pl.program_id(0) == 0)
            def _():
                o_ref[...] = jnp.zeros_like(o_ref)
            av = a_ref[...].astype(BF16)
            for s in range(N_CHIPS):
                o_ref[s] += _dot_tn(av, b_ref[:, s * ns:(s + 1) * ns].astype(BF16))

        return _call(body, name=name, grid=(L // TL_WGRAD,),
                     in_specs=[pl.BlockSpec((TL_WGRAD, K), lambda t: (t, 0)),
                               pl.BlockSpec((TL_WGRAD, N), lambda t: (t, 0))],
                     out_specs=pl.BlockSpec((N_CHIPS, K, ns), lambda t: (0, 0, 0)),
                     out_shape=_sds((N_CHIPS, K, ns), F32), sem=("arbitrary",))(a, b)
    tn = ns // 2 if ns % (2 * LANES) == 0 else ns
    nj = ns // tn
    return _wgrad(a, b, name=name, grid_kn=(1, N_CHIPS * nj),
                  a_spec=pl.BlockSpec((TL_WGRAD, K), lambda i, j, t: (t, 0)),
                  b_spec=pl.BlockSpec((TL_WGRAD, tn), lambda i, j, t: (t, j)),
                  out_shape=_sds((N_CHIPS, K, ns), F32),
                  out_spec=pl.BlockSpec((None, K, tn), lambda i, j, t: (j // nj, 0, j % nj)))


def _wgrad_full(a, b, name):
    K, N = a.shape[1], b.shape[1]
    return _wgrad(a, b, name=name, grid_kn=(1, 1),
                  a_spec=pl.BlockSpec((TL_WGRAD, K), lambda i, j, t: (t, 0)),
                  b_spec=pl.BlockSpec((TL_WGRAD, N), lambda i, j, t: (t, 0)),
                  out_shape=_sds((K, N), F32), out_spec=pl.BlockSpec((K, N), lambda i, j, t: (0, 0)))


def _wgrad_ff_cols(a, b, name):
    K, fs = a.shape[1], b.shape[2]
    return _wgrad(a, b, name=name, grid_kn=(1, N_CHIPS),
                  a_spec=pl.BlockSpec((TL_WGRAD, K), lambda i, j, t: (t, 0)),
                  b_spec=pl.BlockSpec((None, TL_WGRAD, fs), lambda i, j, t: (j, t, 0)),
                  out_shape=_sds((N_CHIPS, K, fs), F32),
                  out_spec=pl.BlockSpec((None, K, fs), lambda i, j, t: (j, 0, 0)))


def _wgrad_ff_rows(a, b, name):
    fs, N = a.shape[2], b.shape[1]
    return _wgrad(a, b, name=name, grid_kn=(N_CHIPS, 1),
                  a_spec=pl.BlockSpec((None, TL_WGRAD, fs), lambda i, j, t: (i, t, 0)),
                  b_spec=pl.BlockSpec((TL_WGRAD, N), lambda i, j, t: (t, 0)),
                  out_shape=_sds((N_CHIPS, fs, N), F32),
                  out_spec=pl.BlockSpec((None, fs, N), lambda i, j, t: (i, 0, 0)))


def _mix_bwd(dx, z, aout, sa, sb, ypre, w_ap, w_ga, w_gb, w_out, rider=None):
    L = dx.shape[0]
    cs = D_MODEL // N_CHIPS
    ga_col = (3 * N_GROUPS * ATTN_WIDTH + SSM_WIDTH) // D_MODEL

    def body(dx_ref, ga_ref, gs_ref, aout_ref, sa_ref, sb_ref, ypre_ref, wap_ref, wga_ref, wgb_ref, wout_ref,
             dgates_ref, da_ref, gy_ref, daout_ref, dsa_ref, dsb_ref):
        dmix = _dot_nt(dx_ref[...].astype(BF16), wout_ref[...])
        sig_a = _sigmoid(ga_ref[...].astype(F32))
        sig_s = _sigmoid(gs_ref[...].astype(F32))
        a_out = aout_ref[...].astype(F32)
        s_a = sa_ref[...].astype(F32)
        sig_b = _sigmoid(sb_ref[...].astype(F32))
        s_out = s_a * sig_b
        daout = (dmix * sig_a).astype(BF16)
        daout_ref[...] = daout
        dgates_ref[:, :D_MODEL] = (dmix * a_out * sig_a * (1.0 - sig_a)).astype(BF16)
        dgates_ref[:, D_MODEL:] = (dmix * s_out * sig_s * (1.0 - sig_s)).astype(BF16)
        ds_out = dmix * sig_s
        dsa = (ds_out * sig_b).astype(BF16)
        dsb = (ds_out * s_a * sig_b * (1.0 - sig_b)).astype(BF16)
        dsa_ref[...] = dsa
        dsb_ref[...] = dsb
        da = jnp.zeros((TM_MIX, ATTN_WIDTH), F32)
        dy = jnp.zeros((TM_MIX, SSM_WIDTH), F32)
        for s in range(N_CHIPS):
            cols = slice(s * cs, (s + 1) * cs)
            da += _dot_nt(daout[:, cols], wap_ref[s])
            dy += _dot_nt(dsa[:, cols], wga_ref[s]) + _dot_nt(dsb[:, cols], wgb_ref[s])
        gy_ref[...] = dy * _gelu_grad(ypre_ref[...])
        da_ref[...] = da

    tok = lambda w: pl.BlockSpec((TM_MIX, w), lambda i: (i, 0))
    wsm = pl.BlockSpec((N_CHIPS, ATTN_WIDTH, cs), lambda i: (0, 0, 0))
    return _call(
        body, name="mix_bwd", grid=(L // TM_MIX,),
        in_specs=[tok(D_MODEL), pl.BlockSpec((TM_MIX, D_MODEL), lambda i: (i, ga_col)),
                  pl.BlockSpec((TM_MIX, D_MODEL), lambda i: (i, ga_col + 1)),
                  tok(D_MODEL), tok(D_MODEL), tok(D_MODEL), tok(SSM_WIDTH),
                  wsm, wsm, wsm, pl.BlockSpec((D_MODEL, D_MODEL), lambda i: (0, 0))],
        out_specs=[tok(2 * D_MODEL), tok(ATTN_WIDTH), tok(SSM_WIDTH)] + [tok(D_MODEL)] * 3,
        out_shape=[_sds((L, 2 * D_MODEL), BF16), _sds((L, ATTN_WIDTH), F32), _sds((L, SSM_WIDTH), F32)]
                  + [_sds((L, D_MODEL), BF16)] * 3,
        sem=("parallel",), rider=rider)(dx, z, z, aout, sa, sb, ypre, w_ap, w_ga, w_gb,
                                        w_out.reshape(D_MODEL, D_MODEL))


def _combine_bwd(da, os_, ls_):
    L = da.shape[0]

    def body(da_ref, o0, o1, o2, l0, l1, l2, hs_ref, p1_ref, p2_ref, pt1_ref, pt2_ref,
             do0, do1, do2, c0, c1, c2):
        o_tok, l_tok = _to_token_order((o0[...], o1[...], o2[...]), (l0[...], l1[...], l2[...]),
                                       (None, pt1_ref[...], pt2_ref[...]))
        w = _combine_weights(*l_tok)
        dav = da_ref[...]
        hs = hs_ref[...]
        tbar = sum(wg * _head_sum(dav * og, hs) for wg, og in zip(w, o_tok))
        for wg, pm, do_ref, c_ref in zip(w, (None, p1_ref[...], p2_ref[...]), (do0, do1, do2), (c0, c1, c2)):
            dog = (wg * dav).astype(BF16)
            cg = -wg * tbar
            do_ref[...] = dog if pm is None else _dot(pm, dog).astype(BF16)
            c_ref[...] = cg if pm is None else _dot(pm, cg.astype(BF16))

    tile = pl.BlockSpec((DL_TILE, ATTN_WIDTH), lambda i: (i, 0))
    mat = pl.BlockSpec((DL_TILE, DL_TILE), lambda i: (0, 0))
    p1, p2 = _perm_matrix(ATTN_PATTERN[1][1]), _perm_matrix(ATTN_PATTERN[2][1])
    outs = _call(body, name="combine_bwd", grid=(L // DL_TILE,), in_specs=[tile] * 7 + [mat] * 5,
                 out_specs=[tile] * 6,
                 out_shape=[_sds((L, ATTN_WIDTH), BF16)] * 3 + [_sds((L, ATTN_WIDTH), F32)] * 3,
                 sem=("parallel",))(da, *os_, *ls_, _head_sum_matrix(), p1, p2, p1.T, p2.T)
    return outs[:3], outs[3:]


def _attn_bwd(qs, ks, v, do, l, c, gi, rider=None):
    L = qs.shape[0]
    _, d = ATTN_PATTERN[gi]
    nb = L // (d * BLK)

    def body(q0_ref, q1_ref, k_ref, v_ref, do0_ref, do1_ref, l0_ref, l1_ref, c0_ref, c1_ref,
             dq_ref, dk_ref, dv_ref, carry):
        n = pl.program_id(1)

        @pl.when(n == 0)
        def _():
            carry[...] = jnp.zeros_like(carry)

        qi = lax.broadcasted_iota(jnp.int32, (2 * BLK, BLK), 0)
        kj = lax.broadcasted_iota(jnp.int32, (2 * BLK, BLK), 1)
        first = qi < BLK
        mask = jnp.logical_and(jnp.where(first, qi, kj) >= jnp.where(first, kj, qi - BLK),
                               qi < jnp.where(n < nb - 1, 2 * BLK, BLK))
        q2 = jnp.concatenate([_dl_read(q0_ref), _dl_read(q1_ref)], axis=0)
        do2 = jnp.concatenate([_dl_read(do0_ref), _dl_read(do1_ref)], axis=0)
        l2 = jnp.concatenate([_dl_read(l0_ref), _dl_read(l1_ref)], axis=0)
        c2 = jnp.concatenate([_dl_read(c0_ref), _dl_read(c1_ref)], axis=0)
        k = _dl_read(k_ref)
        v_ = _dl_read(v_ref)
        h0, h1 = _pair_masks()
        mask2 = jnp.concatenate([mask, mask], axis=1)
        dq_parts, dk_parts, dv_parts = [], [], []
        for hp in range(N_HEADS // 2):
            ls = slice(hp * LANES, (hp + 1) * LANES)
            qp, dop, kp_, vp_ = q2[:, ls], do2[:, ls], k[:, ls], v_[:, ls]
            kk = jnp.concatenate([jnp.where(h0, kp_, 0), jnp.where(h1, kp_, 0)], axis=0)
            vv = jnp.concatenate([jnp.where(h0, vp_, 0), jnp.where(h1, vp_, 0)], axis=0)

            def per_head(t):
                a = jnp.broadcast_to(t[:, hp * LANES:hp * LANES + 1], (2 * BLK, BLK))
                b = jnp.broadcast_to(t[:, hp * LANES + HEAD_DIM:hp * LANES + HEAD_DIM + 1], (2 * BLK, BLK))
                return jnp.concatenate([a, b], axis=1)

            p = jnp.where(mask2, jnp.exp(_dot_nt(qp, kk) - per_head(l2)), 0.0)
            ds = (p * (_dot_nt(dop, vv) + per_head(c2))).astype(BF16)
            dv2 = _dot_tn(p.astype(BF16), dop)
            dk2 = _dot_tn(ds, qp)
            dq2 = _dot(ds, kk)
            dq_parts.append((dq2[:BLK] + carry[:, ls]).astype(BF16))
            carry[:, ls] = dq2[BLK:]
            dk_parts.append(jnp.where(h0, dk2[:BLK], dk2[BLK:]).astype(BF16))
            dv_parts.append(jnp.where(h0, dv2[:BLK], dv2[BLK:]).astype(BF16))
        _dl_write(dq_ref, jnp.concatenate(dq_parts, axis=1))
        _dl_write(dk_ref, jnp.concatenate(dk_parts, axis=1))
        _dl_write(dv_ref, jnp.concatenate(dv_parts, axis=1))

    cur = _dl_spec(d, ATTN_WIDTH, lambda n: n)
    nxt = _dl_spec(d, ATTN_WIDTH, lambda n: jnp.minimum(n + 1, nb - 1))
    view = lambda t: _dl_view(t, d)
    outs = _call(
        body, name=f"attn_bwd_g{gi}", grid=(d, nb),
        in_specs=[cur, nxt, cur, cur, cur, nxt, cur, nxt, cur, nxt], out_specs=[cur, cur, cur],
        out_shape=[_sds(view(qs).shape, BF16)] * 3, scratch=[pltpu.VMEM((BLK, ATTN_WIDTH), F32)],
        sem=("parallel", "arbitrary"), rider=rider)(view(qs), view(qs), view(ks), view(v), view(do), view(do),
                                                    view(l), view(l), view(c), view(c))
    return [t.reshape(L, ATTN_WIDTH) for t in outs[:3]], list(outs[3:])


def _qkv_post(z, dqkv, du, dgates, gq_t, gk_t):
    L = z.shape[0]
    qkv_w = N_GROUPS * ATTN_WIDTH

    def body(zq_ref, zk_ref, gq_ref, gk_ref, hs_ref, pt1_ref, pt2_ref, du_ref, dgates_ref, *rest):
        dl_refs, (dz_ref, dgq_ref, dgk_ref) = rest[:9], rest[9:]

        @pl.when(pl.program_id(0) == 0)
        def _():
            dgq_ref[...] = jnp.zeros_like(dgq_ref)
            dgk_ref[...] = jnp.zeros_like(dgk_ref)

        hs = hs_ref[...]
        pts = (None, pt1_ref[...], pt2_ref[...])

        def rows8(t):
            return jnp.sum(t.reshape(DL_TILE // SUBLANES, SUBLANES, ATTN_WIDTH), axis=0)

        def norm_bwd(x, gain, dn):
            r = lax.rsqrt(_head_sum(x * x, hs) * (1.0 / HEAD_DIM) + EPS)
            xh = x * r
            dh = dn * gain
            return r * (dh - xh * (_head_sum(dh * xh, hs) * (1.0 / HEAD_DIM))), rows8(dn * xh)

        for g in range(N_GROUPS):
            tok = [t[...].astype(F32) if pts[g] is None else _dot(pts[g], t[...]) for t in dl_refs[3 * g:3 * g + 3]]
            cols = slice(g * ATTN_WIDTH, (g + 1) * ATTN_WIDTH)
            dq, pq = norm_bwd(zq_ref[:, cols].astype(F32), gq_ref[...] * SCALE, tok[0])
            dk, pk_ = norm_bwd(zk_ref[:, cols].astype(F32), gk_ref[...], tok[1])
            dgq_ref[...] += pq * SCALE
            dgk_ref[...] += pk_
            dz_ref[:, cols] = dq.astype(BF16)
            dz_ref[:, qkv_w + g * ATTN_WIDTH:qkv_w + (g + 1) * ATTN_WIDTH] = dk.astype(BF16)
            dz_ref[:, 2 * qkv_w + g * ATTN_WIDTH:2 * qkv_w + (g + 1) * ATTN_WIDTH] = tok[2].astype(BF16)
        dz_ref[:, 3 * qkv_w:3 * qkv_w + SSM_WIDTH] = du_ref[...]
        dz_ref[:, 3 * qkv_w + SSM_WIDTH:] = dgates_ref[...]

    tile = lambda w: pl.BlockSpec((DL_TILE, w), lambda i: (i, 0))
    mat = pl.BlockSpec((DL_TILE, DL_TILE), lambda i: (0, 0))
    vec = pl.BlockSpec((1, ATTN_WIDTH), lambda i: (0, 0))
    acc = pl.BlockSpec((SUBLANES, ATTN_WIDTH), lambda i: (0, 0))
    flat = [t for grp in dqkv for t in grp]
    return _call(
        body, name="qkv_post", grid=(L // DL_TILE,),
        in_specs=[tile(qkv_w), pl.BlockSpec((DL_TILE, qkv_w), lambda i: (i, 1)), vec, vec, mat, mat, mat,
                  tile(SSM_WIDTH), tile(2 * D_MODEL)] + [tile(ATTN_WIDTH)] * 9,
        out_specs=[tile(IN_COLS), acc, acc],
        out_shape=[_sds((L, IN_COLS), BF16), _sds((SUBLANES, ATTN_WIDTH), F32), _sds((SUBLANES, ATTN_WIDTH), F32)],
        sem=("arbitrary",))(z, z, gq_t, gk_t, _head_sum_matrix(), _perm_matrix(ATTN_PATTERN[1][1]).T,
                            _perm_matrix(ATTN_PATTERN[2][1]).T, du, dgates, *flat)


def _scan_rev_grad(sre, sim, rre, rim, are_ref, aim_ref, k0, init, seed_re, seed_im):
    ar = [_bcast(are_ref, k0 + kk) for kk in range(SCAN_GROUP)]
    ai = [-_bcast(aim_ref, k0 + kk) for kk in range(SCAN_GROUP)]

    def update(i, xprev, carry):
        out = []
        for kk in range(SCAN_GROUP):
            k = k0 + kk
            lr, li, dr, di = carry[4 * kk:4 * kk + 4]
            nr = ar[kk] * lr - ai[kk] * li + rre[k, _rows(i), :]
            ni = ar[kk] * li + ai[kk] * lr + rim[k, _rows(i), :]
            rre[k, _rows(i), :] = nr
            rim[k, _rows(i), :] = ni
            xr, xi = xprev(k)
            out += [nr, ni, dr + xr * nr + xi * ni, di + xr * ni - xi * nr]
        return tuple(out)

    def step(t, carry):
        i = SSM_TC - 1 - t
        return update(i, lambda k: (sre[k, _rows(i - 1), :], sim[k, _rows(i - 1), :]), carry)

    zero = jnp.zeros((SSM_SUB, LANES), F32)
    flat = []
    for re, im in init:
        flat += [re, im, zero, zero]
    res = lax.fori_loop(0, (SSM_TC - 1) // 2, lambda t, c: step(2 * t + 1, step(2 * t, c)), tuple(flat))
    res = step(SSM_TC - 2, res)
    res = update(0, lambda k: (seed_re[k], seed_im[k]), res)
    return [(res[4 * kk + 2], res[4 * kk + 3]) for kk in range(SCAN_GROUP)]


def _ssm_bwd(z, gy, pk, dskip, sd_re, sd_im, rider=None):
    L = z.shape[0]
    nb = L // SSM_TB
    ucol = (3 * N_GROUPS * ATTN_WIDTH) // SSM_WIDTH
    nwin = N_PAIR // PAIRS_PER_WIN

    def body(u_ref, gy_ref, are_ref, aim_ref, a64re_ref, a64im_ref, bwre_ref, bwim_ref, cwre_ref, cwim_ref, d_ref,
             sdre_ref, sdim_ref,
             du_ref, dare_ref, daim_ref, dbre_ref, dbim_ref, dcre_ref, dcim_ref, dd_ref,
             sre, sim, rre, rim, carry_re, carry_im, ends_re, ends_im, seed_re, seed_im):
        @pl.when(pl.program_id(0) == 0)
        def _():
            carry_re[...] = jnp.zeros_like(carry_re)
            carry_im[...] = jnp.zeros_like(carry_im)
            for ref in (dare_ref, daim_ref, dbre_ref, dbim_ref, dcre_ref, dcim_ref, dd_ref):
                ref[...] = jnp.zeros_like(ref)

        u = u_ref[...]
        gyv = gy_ref[...]
        gyb = gyv.astype(BF16)
        _ssm_fill(u, bwre_ref, bwim_ref, sre, sim)
        for k2 in range(N_PAIR):
            gw = gyb[:, _win_cols(k2)]
            _to_slabs(rre, k2, _dot_nt(gw, cwre_ref[k2]))
            _to_slabs(rim, k2, -_dot_nt(gw, cwim_ref[k2]))
        zero = jnp.zeros((SSM_SUB, LANES), F32)
        for k0 in range(0, N_SLAB, SCAN_GROUP):
            grp = range(k0, k0 + SCAN_GROUP)
            _scan(sre, sim, are_ref, aim_ref, k0, [(sdre_ref[k], sdim_ref[k]) for k in grp],
                  reverse=False, store=True)
            ends = _scan(rre, rim, are_ref, aim_ref, k0, [(zero, zero)] * SCAN_GROUP, reverse=True, store=False,
                         sign=-1.0)
            for kk, k in enumerate(grp):
                ends_re[k] = ends[kk][0]
                ends_im[k] = ends[kk][1]
            for k in grp:
                _ssm_seeds(ends_re, ends_im, a64re_ref, a64im_ref, carry_re, carry_im, seed_re, seed_im, k,
                           reverse=True, sign=-1.0)
            das = _scan_rev_grad(sre, sim, rre, rim, are_ref, aim_ref, k0,
                                 [(seed_re[k], seed_im[k]) for k in grp], sdre_ref, sdim_ref)
            for kk, k in enumerate(grp):
                dare_ref[k] += das[kk][0]
                daim_ref[k] += das[kk][1]
        for w in range(nwin):
            cols = _win_cols(w * PAIRS_PER_WIN)
            uw = u[:, cols]
            gw = gyb[:, cols]
            acc = gyv[:, cols] * d_ref[:, cols]
            for kk in range(PAIRS_PER_WIN):
                k2 = w * PAIRS_PER_WIN + kk
                lr = _pair_rows(rre, k2)
                li = _pair_rows(rim, k2)
                acc += _dot_nt(lr, bwre_ref[k2]) + _dot_nt(li, bwim_ref[k2])
                dbre_ref[k2] += _dot_tn(uw, lr)
                dbim_ref[k2] += _dot_tn(uw, li)
                dcre_ref[k2] += _dot_tn(_pair_rows(sre, k2), gw)
                dcim_ref[k2] -= _dot_tn(_pair_rows(sim, k2), gw)
            du_ref[:, cols] = acc.astype(BF16)
        dd_ref[...] += jnp.sum((gyv * u.astype(F32)).reshape(SSM_TB // SUBLANES, SUBLANES, SSM_WIDTH), axis=0)

    c2, c3 = _ssm_specs_consts()
    rev = lambda b: nb - 1 - b
    seed_spec = pl.BlockSpec((None, N_SLAB, SSM_SUB, LANES), lambda b: (rev(b), 0, 0, 0))
    tile_out = pl.BlockSpec((N_SLAB, SSM_SUB, LANES), lambda b: (0, 0, 0))
    small = pltpu.VMEM((N_SLAB, LANES), F32)
    tile = pltpu.VMEM((N_SLAB, SSM_SUB, LANES), F32)
    return _call(
        body, name="ssm_bwd", grid=(nb,),
        in_specs=[pl.BlockSpec((SSM_TB, SSM_WIDTH), lambda b: (rev(b), ucol)),
                  pl.BlockSpec((SSM_TB, SSM_WIDTH), lambda b: (rev(b), 0)),
                  c2, c2, c2, c2, c3, c3, c3, c3, pl.BlockSpec((1, SSM_WIDTH), lambda b: (0, 0)),
                  seed_spec, seed_spec],
        out_specs=[pl.BlockSpec((SSM_TB, SSM_WIDTH), lambda b: (rev(b), 0)), tile_out, tile_out, c3, c3, c3, c3,
                   pl.BlockSpec((SUBLANES, SSM_WIDTH), lambda b: (0, 0))],
        out_shape=[_sds((L, SSM_WIDTH), BF16), _sds((N_SLAB, SSM_SUB, LANES), F32),
                   _sds((N_SLAB, SSM_SUB, LANES), F32)] + [_sds((N_PAIR, SSM_WIN, SSM_WIN), F32)] * 4
                  + [_sds((SUBLANES, SSM_WIDTH), F32)],
        scratch=_ssm_scratch() + _ssm_scratch() + [small, small, tile, tile, tile, tile],
        sem=("arbitrary",), rider=rider)(z, gy, pk["a_re"], pk["a_im"], pk["a64_re"], pk["a64_im"],
                            pk["bw_re"].astype(BF16), pk["bw_im"].astype(BF16),
                            pk["cw_re"].astype(BF16), pk["cw_im"].astype(BF16), dskip, sd_re, sd_im)


def _in_proj_bwd(dz, w, x, g, dres, rider=None):
    L = x.shape[0]
    ns = w.shape[2]
    tn = ns
    nj = ns // tn
    nt = N_CHIPS * nj
    TM = TM_PROJ

    def body(dz_ref, w_ref, x_ref, g_ref, dres_ref, dx_ref, dg_ref, acc, dgacc):
        i, j = pl.program_id(0), pl.program_id(1)

        @pl.when(j == 0)
        def _():
            acc[...] = jnp.zeros_like(acc)

        @pl.when(jnp.logical_and(i == 0, j == 0))
        def _():
            dgacc[...] = jnp.zeros_like(dgacc)

        acc[...] += _dot_nt(dz_ref[...], w_ref[...])

        @pl.when(j == nt - 1)
        def _():
            dx, dg = _rms_bwd(x_ref[...], g_ref[...], acc[...])
            dx_ref[...] = dres_ref[...] + dx
            dgacc[...] += dg

        @pl.when(jnp.logical_and(i == pl.num_programs(0) - 1, j == nt - 1))
        def _():
            dg_ref[...] = jnp.sum(dgacc[...], axis=0, keepdims=True)

    tok = pl.BlockSpec((TM, D_MODEL), lambda i, j: (i, 0))
    vec = pl.BlockSpec((1, D_MODEL), lambda i, j: (0, 0))
    return _call(
        body, name="in_proj_bwd", grid=(L // TM, nt),
        in_specs=[pl.BlockSpec((TM, tn), lambda i, j: (i, j)),
                  pl.BlockSpec((None, D_MODEL, tn), lambda i, j: (j // nj, 0, j % nj)), tok, vec, tok],
        out_specs=[tok, vec],
        out_shape=[_sds((L, D_MODEL), F32), _sds((1, D_MODEL), F32)],
        scratch=[pltpu.VMEM((TM, D_MODEL), F32), pltpu.VMEM((SUBLANES, D_MODEL), F32)],
        sem=("arbitrary", "arbitrary"), rider=rider)(dz, w, x, g, dres)


SSM_PARAMS = ("lambda_re", "lambda_im", "log_dt", "b_re", "b_im", "c_re", "c_im")
EARLY = ("w_ffn_gate", "w_ffn_up", "w_ffn_down")
LATE = ("w_in", "w_attn_proj", "w_glu_a", "w_glu_b", "w_out")


def _layer_bwd(dx2, sv, p, pending, owned, l, idx):
    chip_idx, c_idx = idx
    g = {}
    owned = dict(owned)

    def settle(name, partial, arrived, layer):
        owned[name] = _sum_owner(partial, arrived, owned[name], layer, chip_idx, c_idx)

    dgate, dup, dx2b = _ffn_bwd_act(dx2, sv["gate"], sv["up"], p["w_ffn_down"])
    outs = _ffn_bwd_in(dx2, sv["x1"], p["g_ffn"], dgate, dup, p["w_ffn_gate"], p["w_ffn_up"],
                       _scatter_rider([pending[n] for n in LATE[1:]]) if pending else None)
    dx1, g["g_ffn"] = outs[:2]
    for n, t in zip(LATE[1:], outs[2:]):
        settle(n, pending[n], t, l + 1)
    g["w_ffn_gate"] = _wgrad_ff_rows(dgate, sv["h2"], "wgrad_ffn_gate")
    g["w_ffn_up"] = _wgrad_ff_rows(dup, sv["h2"], "wgrad_ffn_up")
    g["w_ffn_down"] = _wgrad_ff_rows(sv["act"], dx2b, "wgrad_ffn_down")

    outs = _mix_bwd(dx1, sv["z"], sv["aout"], sv["sa"], sv["sb"], sv["ypre"], p["w_attn_proj"], p["w_glu_a"],
                    p["w_glu_b"], p["w_out"], _swap_rider([g[n] for n in EARLY]))
    dgates, da, gy, daout, dsa, dsb = outs[:6]
    early = [_add_half(g[n], s, c_idx) for n, s in zip(EARLY, outs[6:])]
    g["w_out"] = _wgrad_full(sv["mix"], dx1, "wgrad_out").reshape(N_CHIPS, D_MODEL // N_CHIPS, D_MODEL)
    g["w_attn_proj"] = _wgrad_cols(sv["a"], daout, "wgrad_attn_proj")
    g["w_glu_a"] = _wgrad_cols(sv["yact"], dsa, "wgrad_glu_a")
    g["w_glu_b"] = _wgrad_cols(sv["yact"], dsb, "wgrad_glu_b")

    outs = _ssm_bwd(sv["z"], gy, sv["pk"], p["d_skip"], sv["sd_re"], sv["sd_im"],
                    _scatter_rider([pending[LATE[0]]]) if pending else None)
    du, da_re, da_im, dbw_re, dbw_im, dcw_re, dcw_im, dd = outs[:8]
    if pending:
        settle(LATE[0], pending[LATE[0]], outs[8], l + 1)
    g["d_skip"] = jnp.sum(dd, axis=0, keepdims=True)
    g["ssm_pack_ct"] = dict(a_re=da_re, a_im=da_im, bw_re=dbw_re, bw_im=dbw_im, cw_re=dcw_re, cw_im=dcw_im)

    dos, cs = _combine_bwd(da, sv["os"], sv["ls"])
    dqkv = []
    for gi in range(N_GROUPS):
        grads, arrived = _attn_bwd(*sv["qkv"][gi], dos[gi], sv["ls"][gi], cs[gi], gi, _scatter_rider([early[gi]]))
        settle(EARLY[gi], early[gi], arrived[0], l)
        dqkv.append(grads)
    dz, gq8, gk8 = _qkv_post(sv["z"], dqkv, du, dgates, jnp.tile(p["g_q"], (1, N_HEADS)),
                             jnp.tile(p["g_k"], (1, N_HEADS)))
    g["g_q"] = jnp.sum(gq8.reshape(SUBLANES * N_HEADS, HEAD_DIM), axis=0, keepdims=True)
    g["g_k"] = jnp.sum(gk8.reshape(SUBLANES * N_HEADS, HEAD_DIM), axis=0, keepdims=True)
    g["w_in"] = _wgrad_cols(sv["h"], dz, "wgrad_in")
    outs = _in_proj_bwd(dz, p["w_in"], sv["x"], p["g_mix"], dx1, _swap_rider([g[n] for n in LATE]))
    dx, g["g_mix"] = outs[:2]
    late = {n: _add_half(g[n], s, c_idx) for n, s in zip(LATE, outs[2:])}
    return dx, g, late, owned


def _place():
    x, y, c = lax.axis_index("x"), lax.axis_index("y"), lax.axis_index("c")
    others = [(1 - x, y), (x, 1 - y), (1 - x, 1 - y)]
    return x, y, c, others


def _half(ref, hc):
    rows = ref.shape[-2] // 2
    idx = (slice(None),) * (len(ref.shape) - 2) + (pl.ds(hc * rows, rows), slice(None))
    return ref.at[idx]


def _comm_call(body, name, ins, out_shapes, n_remote, aliases=None):
    scratch = [pltpu.SemaphoreType.DMA((n_remote,)), pltpu.SemaphoreType.DMA((n_remote,))]
    return pl.pallas_call(
        body, name=name, in_specs=[ANY] * len(ins), out_specs=[ANY] * len(out_shapes), out_shape=out_shapes,
        scratch_shapes=scratch, input_output_aliases=aliases or {})(*ins)


def _cast_place(w, l, chip_idx):
    _, R, C = w.shape
    tr = R // 2

    def body(me_ref, w_ref, o_ref):
        o_ref[...] = w_ref[...].astype(BF16)

    return _call(body, name=f"cast_place_l{l}", grid=(R // tr,), prefetch=1,
                 in_specs=[pl.BlockSpec((None, tr, C), lambda i, me_ref: (l, i, 0))],
                 out_specs=pl.BlockSpec((None, tr, C), lambda i, me_ref: (me_ref[0], i, 0)),
                 out_shape=_sds((N_CHIPS, R, C), BF16), sem=("arbitrary",))(chip_idx, w)


def _in_place_rider(bufs, pairs, per_buf=3):
    n = len(bufs)

    def copies(outs, send, recv, side):
        return [pltpu.make_async_remote_copy(src_ref=pair[side][0], dst_ref=pair[side][0], send_sem=send.at[k],
                                             recv_sem=recv.at[k], device_id=pair[side][1], device_id_type=MESH)
                for k, pair in enumerate(pairs(outs))]

    def start(ins, outs, send, recv):
        for cp in copies(outs, send, recv, 0):
            cp.start()

    def wait(ins, outs, send, recv):
        for cp in copies(outs, send, recv, 1):
            cp.wait_recv()
        for cp in copies(outs, send, recv, 0):
            cp.wait_send()

    return Rider(list(bufs), [_sds(b.shape, b.dtype) for b in bufs], per_buf * n, start, wait,
                 {a: a for a in range(n)})


def _gather_ici_rider(bufs):
    def pairs(outs):
        x, y, c, others = _place()
        return [((_half(o.at[2 * x + y], c), (cx, cy, c)), (_half(o.at[2 * cx + cy], c), (cx, cy, c)))
                for o in outs for cx, cy in others]
    return _in_place_rider(bufs, pairs)


def _gather_d2d_rider(bufs):
    def pairs(outs):
        x, y, c, others = _place()
        sib = (x, y, 1 - c)
        return [((_half(o.at[2 * cx + cy], c), sib), (_half(o.at[2 * cx + cy], 1 - c), sib))
                for o in outs for cx, cy in others]
    return _in_place_rider(bufs, pairs)


def _swap_rider(gs):
    n = len(gs)

    def copies(ins, outs, send, recv):
        x, y, c, _ = _place()
        return [pltpu.make_async_remote_copy(src_ref=_half(ins[a], 1 - c), dst_ref=outs[a], send_sem=send.at[a],
                                             recv_sem=recv.at[a], device_id=(x, y, 1 - c), device_id_type=MESH)
                for a in range(n)]

    def start(ins, outs, send, recv):
        for cp in copies(ins, outs, send, recv):
            cp.start()

    def wait(ins, outs, send, recv):
        for cp in copies(ins, outs, send, recv):
            cp.wait()

    outs = [_sds((g.shape[0], g.shape[1] // 2, g.shape[2]), g.dtype) for g in gs]
    return Rider(list(gs), outs, n, start, wait, {})


def _scatter_rider(ss):
    n = len(ss)

    def copies(ins, outs, send, recv):
        x, y, c, others = _place()
        return [pltpu.make_async_remote_copy(
            src_ref=ins[a].at[2 * cx + cy], dst_ref=outs[a].at[j], send_sem=send.at[3 * a + j],
            recv_sem=recv.at[3 * a + j], device_id=(cx, cy, c), device_id_type=MESH)
            for a in range(n) for j, (cx, cy) in enumerate(others)]

    def start(ins, outs, send, recv):
        for cp in copies(ins, outs, send, recv):
            cp.start()

    def wait(ins, outs, send, recv):
        for cp in copies(ins, outs, send, recv):
            cp.wait()

    outs = [_sds((N_CHIPS - 1,) + s.shape[1:], s.dtype) for s in ss]
    return Rider(list(ss), outs, 3 * n, start, wait, {})


def _run_rider(rider, name):
    n_in = len(rider.ins)

    def body(*refs):
        ins, outs = refs[:n_in], refs[n_in:n_in + len(rider.out_shapes)]
        send, recv = refs[n_in + len(rider.out_shapes):]
        rider.start(ins, outs, send, recv)
        rider.wait(ins, outs, send, recv)

    return _comm_call(body, name, rider.ins, rider.out_shapes, rider.n_sem, aliases=rider.aliases)


def _join_rider(bufs):
    def pairs(outs):
        x, y, c, _ = _place()
        sib = (x, y, 1 - c)
        return [((_half(o, c), sib), (_half(o, 1 - c), sib)) for o in outs]
    return _in_place_rider(bufs, pairs, per_buf=1)


def _place_small(v, dev_idx):
    rows, n = v.shape

    def body(idx_ref, v_ref, o_ref):
        o_ref[...] = v_ref[...]

    return _call(body, name="place_small", grid=(1,), prefetch=1,
                 in_specs=[pl.BlockSpec((rows, n), lambda i, idx_ref: (0, 0))],
                 out_specs=pl.BlockSpec((None, rows, n), lambda i, idx_ref: (idx_ref[0], 0, 0)),
                 out_shape=_sds((8, rows, n), v.dtype), sem=("arbitrary",))(dev_idx, v)


def _small_ici_rider(buf):
    def pairs(outs):
        x, y, c, others = _place()
        peers = [(x, y, 1 - c)] + [(cx, cy, c) for cx, cy in others]
        return [((outs[0].at[4 * x + 2 * y + c], peer), (outs[0].at[4 * peer[0] + 2 * peer[1] + peer[2]], peer))
                for peer in peers]
    return _in_place_rider([buf], pairs, per_buf=4)


def _small_d2d_rider(buf):
    def pairs(outs):
        x, y, c, others = _place()
        sib = (x, y, 1 - c)
        return [((outs[0].at[4 * cx + 2 * cy + c], sib), (outs[0].at[4 * cx + 2 * cy + 1 - c], sib))
                for cx, cy in others]
    return _in_place_rider([buf], pairs)


def _add_half(g, p, c):
    _, R, C = g.shape
    half = R // 2

    def body(c_ref, g_ref, p_ref, o_ref):
        o_ref[...] = g_ref[...] + p_ref[...]

    blk = (None, half, C)
    return _call(body, name="add_half", grid=(N_CHIPS,), prefetch=1,
                 in_specs=[pl.BlockSpec(blk, lambda s, c_ref: (s, c_ref[0], 0)),
                           pl.BlockSpec(blk, lambda s, c_ref: (s, 0, 0))],
                 out_specs=pl.BlockSpec(blk, lambda s, c_ref: (s, 0, 0)),
                 out_shape=_sds((N_CHIPS, half, C), F32), sem=("arbitrary",))(c, g, p)


def _sum_owner(s, q, buf, l, me, c):
    _, half, C = s.shape
    tr = half // 2

    def body(me_ref, c_ref, s_ref, q0, q1, q2, buf_ref, o_ref):
        o_ref[...] = ((s_ref[...] + q0[...]) + q1[...]) + q2[...]

    blk = (None, tr, C)
    qspec = lambda j: pl.BlockSpec(blk, lambda i, me_ref, c_ref: (j, i, 0))
    return _call(body, name=f"sum_owner_l{l}", grid=(half // tr,), prefetch=2,
                 in_specs=[pl.BlockSpec(blk, lambda i, me_ref, c_ref: (me_ref[0], i, 0)),
                           qspec(0), qspec(1), qspec(2), ANY],
                 out_specs=pl.BlockSpec(blk, lambda i, me_ref, c_ref: (l, 2 * c_ref[0] + i, 0)),
                 out_shape=_sds(buf.shape, F32), sem=("arbitrary",), aliases={6: 0})(me, c, s, q, q, q, buf)


def _adamw_math(w, g, m, v):
    m = ADAM_B1 * m + (1.0 - ADAM_B1) * g
    v = ADAM_B2 * v + (1.0 - ADAM_B2) * (g * g)
    m_hat = m / (1.0 - ADAM_B1 ** ADAM_STEP)
    v_hat = v / (1.0 - ADAM_B2 ** ADAM_STEP)
    delta = -ADAM_LR * (m_hat / (jnp.sqrt(v_hat) + ADAM_EPS) + ADAM_WD * w)
    return delta, m, v


def _adamw(w, g, m, v, rider=None):
    depth, R, C = w.shape
    tr = max(t for t in range(SUBLANES, R + 1, SUBLANES) if R % t == 0 and t * C * 4 <= 2 * 1024 * 1024)

    def body(w_ref, g_ref, m_ref, v_ref, d_ref, nm_ref, nv_ref):
        d, nm, nv = _adamw_math(w_ref[...], g_ref[...], m_ref[...], v_ref[...])
        d_ref[...] = d
        nm_ref[...] = nm
        nv_ref[...] = nv

    spec = pl.BlockSpec((None, tr, C), lambda l, i: (l, i, 0))
    return _call(body, name="adamw", grid=(depth, R // tr), in_specs=[spec] * 4, out_specs=[spec] * 3,
                 out_shape=[_sds(w.shape, F32)] * 3, sem=("parallel", "parallel"), rider=rider)(w, g, m, v)


def _small_update(gathered, w, m, v):
    _, rows, n = gathered.shape
    tr = rows // 7

    def body(ga_ref, w_ref, m_ref, v_ref, g_ref, d_ref, nm_ref, nv_ref):
        g = ga_ref[0]
        for k in range(1, 8):
            g = g + ga_ref[k]
        d, nm, nv = _adamw_math(w_ref[...], g, m_ref[...], v_ref[...])
        g_ref[...] = g
        d_ref[...] = d
        nm_ref[...] = nm
        nv_ref[...] = nv

    spec = pl.BlockSpec((tr, n), lambda i: (i, 0))
    return _call(body, name="small_update", grid=(rows // tr,),
                 in_specs=[pl.BlockSpec((8, tr, n), lambda i: (0, i, 0)), spec, spec, spec], out_specs=[spec] * 4,
                 out_shape=[_sds((rows, n), F32)] * 4, sem=("parallel",))(gathered, w, m, v)


WEIGHTS = ("g_mix", "w_in", "g_q", "g_k", "w_attn_proj", "lambda_re", "lambda_im", "log_dt", "b_re", "b_im",
           "c_re", "c_im", "d_skip", "w_glu_a", "w_glu_b", "w_out", "g_ffn", "w_ffn_gate", "w_ffn_up", "w_ffn_down")
BIG = ("w_in", "w_attn_proj", "w_glu_a", "w_glu_b", "w_out", "w_ffn_gate", "w_ffn_up", "w_ffn_down")
FLIPPED = ("w_ffn_gate", "w_ffn_up")
SMALL = tuple(n for n in WEIGHTS if n not in BIG)
ROW_VECTORS = ("g_mix", "g_q", "g_k", "d_skip", "g_ffn")
PACK_QUANTUM = LANES * SUBLANES * 7


def _pack_small(parts, extra):
    flat = jnp.concatenate([parts[n].reshape(-1).astype(F32) for n in SMALL] + [extra.reshape(-1)])
    pad = -flat.shape[0] % PACK_QUANTUM
    return jnp.pad(flat, (0, pad)).reshape(-1, LANES)


def _unpack_small(packed, like):
    flat = packed.reshape(-1)
    out, at = {}, 0
    for n in SMALL:
        size = math.prod(like[n].shape)
        out[n] = flat[at:at + size].reshape(like[n].shape)
        at += size
    return out, flat[at]


def kernel(x, g_mix, w_in, g_q, g_k, w_attn_proj, lambda_re, lambda_im, log_dt, b_re, b_im, c_re, c_im, d_skip, w_glu_a, w_glu_b, w_out, g_ffn, w_ffn_gate, w_ffn_up, w_ffn_down, loss_target, m_g_mix, m_w_in, m_g_q, m_g_k, m_w_attn_proj, m_lambda_re, m_lambda_im, m_log_dt, m_b_re, m_b_im, m_c_re, m_c_im, m_d_skip, m_w_glu_a, m_w_glu_b, m_w_out, m_g_ffn, m_w_ffn_gate, m_w_ffn_up, m_w_ffn_down, v_g_mix, v_w_in, v_g_q, v_g_k, v_w_attn_proj, v_lambda_re, v_lambda_im, v_log_dt, v_b_re, v_b_im, v_c_re, v_c_im, v_d_skip, v_w_glu_a, v_w_glu_b, v_w_out, v_g_ffn, v_w_ffn_gate, v_w_ffn_up, v_w_ffn_down):
    given = dict(locals())
    flip = lambda n, a: jnp.swapaxes(a, 1, 2) if n in FLIPPED else a
    W = {n: flip(n, given[n]) for n in WEIGHTS}
    M = {n: flip(n, given["m_" + n]) for n in WEIGHTS}
    V = {n: flip(n, given["v_" + n]) for n in WEIGHTS}
    depth = g_mix.shape[0]
    xl = x.reshape(x.shape[-2:])
    target = loss_target.reshape(loss_target.shape[-2:])
    c_idx = lax.axis_index("c").astype(jnp.int32).reshape(1)
    chip_idx = (2 * lax.axis_index("x") + lax.axis_index("y")).astype(jnp.int32).reshape(1)

    place = lambda l: [_cast_place(W[n], l, chip_idx) for n in BIG]
    bufs = place(0)
    w_in = _run_rider(_gather_d2d_rider(_run_rider(_gather_ici_rider(bufs[:1]), "gather_ici")), "gather_d2d")[0]
    rest, stage = bufs[1:], "ici"
    params, saved, h = [], [], xl
    for l in range(depth):
        p = {"w_in": w_in}
        for n in SMALL:
            p[n] = W[n][l][None] if n in ROW_VECTORS else W[n][l]
        h, sv, p, nxt = _layer_fwd(h, p, rest, stage, place(l + 1) if l + 1 < depth else None)
        params.append(p)
        saved.append(sv)
        if nxt:
            (w_in, rest), stage = nxt, "d2d"
    dx, loss_part = _loss_head(h, target)

    owned = {n: lax.empty(W[n].shape, F32) for n in BIG}
    small_grads = [None] * depth
    pending = None
    for l in reversed(range(depth)):
        dx, small_grads[l], pending, owned = _layer_bwd(dx, saved[l], params[l], pending, owned, l,
                                                        (chip_idx, c_idx))

    ct = {k: jnp.stack([small_grads[l]["ssm_pack_ct"][k] for l in range(depth)])
          for k in small_grads[0]["ssm_pack_ct"]}
    ct["a_re"], ct["a_im"] = jnp.sum(ct["a_re"], axis=2), jnp.sum(ct["a_im"], axis=2)
    ct["a64_re"] = ct["a64_im"] = jnp.zeros_like(ct["a_re"])
    _, pull = jax.vjp(jax.vmap(_ssm_pack), *[W[n] for n in SSM_PARAMS])
    stacked = dict(zip(SSM_PARAMS, pull(ct)))
    for n in SMALL:
        if n not in stacked:
            stacked[n] = jnp.stack([small_grads[l][n] for l in range(depth)])
    zero = jnp.zeros((1,), F32)
    dev_idx = (4 * lax.axis_index("x") + 2 * lax.axis_index("y") + lax.axis_index("c")).astype(jnp.int32).reshape(1)
    gathered = _place_small(_pack_small(stacked, loss_part), dev_idx)
    outs = _run_rider(_join_riders(_scatter_rider([pending[n] for n in LATE]), _small_ici_rider(gathered)),
                      "scatter_to_owners")
    for n, q in zip(LATE, outs[:len(LATE)]):
        owned[n] = _sum_owner(pending[n], q, owned[n], 0, chip_idx, c_idx)
    outs = _run_rider(_join_riders(_join_rider([owned[n] for n in BIG]), _small_d2d_rider(outs[len(LATE)])),
                      "join_halves")
    reduced, gathered = dict(zip(BIG, outs[:len(BIG)])), outs[len(BIG)]
    grads, delta, new_m, new_v = {}, {}, {}, {}
    for n in BIG:
        outs = (reduced[n], *_adamw(W[n], reduced[n], M[n], V[n]))
        grads[n], delta[n], new_m[n], new_v[n] = [flip(n, t) for t in outs]
    gs, ds, nms, nvs = _small_update(gathered, _pack_small(W, zero), _pack_small(M, zero), _pack_small(V, zero))
    sg, loss = _unpack_small(gs, W)
    sd, _ = _unpack_small(ds, W)
    sm, _ = _unpack_small(nms, W)
    sv_, _ = _unpack_small(nvs, W)
    for n in SMALL:
        grads[n], delta[n], new_m[n], new_v[n] = sg[n], sd[n], sm[n], sv_[n]

    return (loss, dx.reshape(x.shape), *[grads[n] for n in WEIGHTS], *[delta[n] for n in WEIGHTS],
            *[new_m[n] for n in WEIGHTS], *[new_v[n] for n in WEIGHTS])
```

```python
import collections
import functools
import math

import jax
import jax.numpy as jnp
from jax import lax
from jax.experimental import pallas as pl
from jax.experimental.pallas import tpu as pltpu

F32 = jnp.float32
BF16 = jnp.bfloat16

D_MODEL = 1024
DEPTH = 4
HEAD_DIM = 64
N_HEADS = 8
ATTN_WIDTH = N_HEADS * HEAD_DIM
ATTN_PATTERN = ((128, 1), (512, 4), (2048, 16))
N_GROUPS = len(ATTN_PATTERN)
BLK = 128
SSM_WIDTH = 512
SSM_GROUP = 16
SSM_GROUPS = 32
SSM_STATE = 64
D_FF = 2816
IN_COLS = 7168
EPS = 1e-6
ADAM_LR, ADAM_B1, ADAM_B2, ADAM_EPS, ADAM_WD, ADAM_STEP = 0.001, 0.9, 0.999, 1e-08, 0.01, 10

N_CHIPS = 4
MESH = pl.DeviceIdType.MESH

LANES = 128
SUBLANES = 8
VMEM_LIMIT = 56 * 1024 * 1024

TM = 512
TM_PROJ = 1024
TL_WGRAD = 2048
TM_MIX = 512

SSM_TB = 512
SSM_TC = 64
SSM_SUB = SUBLANES
SSM_PITCH = 68
N_SLAB = SSM_GROUPS * SSM_STATE // LANES
SSM_WIN = 256
N_PAIR = N_SLAB // 2
PAIRS_PER_WIN = 4
SCAN_GROUP = 4


def _params(sem=None):
    return pltpu.CompilerParams(dimension_semantics=sem, vmem_limit_bytes=VMEM_LIMIT)


ANY = pl.BlockSpec(memory_space=pl.ANY)

Rider = collections.namedtuple("Rider", "ins out_shapes n_sem start wait aliases")


class _SemWindow:
    def __init__(self, ref, offset):
        self.ref, self.offset = ref, offset

    @property
    def at(self):
        return self

    def __getitem__(self, k):
        return self.ref.at[self.offset + k]


def _join_riders(*riders):
    riders = [r for r in riders if r is not None]
    if len(riders) <= 1:
        return riders[0] if riders else None

    def each(fn_name):
        def run(ins, outs, send, recv):
            i = o = s = 0
            for r in riders:
                getattr(r, fn_name)(ins[i:i + len(r.ins)], outs[o:o + len(r.out_shapes)],
                                    _SemWindow(send, s), _SemWindow(recv, s))
                i, o, s = i + len(r.ins), o + len(r.out_shapes), s + r.n_sem
        return run

    aliases, i, o = {}, 0, 0
    for r in riders:
        aliases.update({i + a: o + b for a, b in r.aliases.items()})
        i, o = i + len(r.ins), o + len(r.out_shapes)
    return Rider([t for r in riders for t in r.ins], [t for r in riders for t in r.out_shapes],
                 sum(r.n_sem for r in riders), each("start"), each("wait"), aliases)


def _with_rider(body, rider, grid, prefetch, n_in, n_out, n_scratch):
    n_rin, n_rout = len(rider.ins), len(rider.out_shapes)

    def hosted(*refs):
        pre, rest = refs[:prefetch], refs[prefetch:]
        ins, rin = rest[:n_in], rest[n_in:n_in + n_rin]
        o0 = n_in + n_rin
        outs, rout = rest[o0:o0 + n_out], rest[o0 + n_out:o0 + n_out + n_rout]
        s0 = o0 + n_out + n_rout
        scr, (send, recv) = rest[s0:s0 + n_scratch], rest[s0 + n_scratch:]
        first = functools.reduce(jnp.logical_and, [pl.program_id(k) == 0 for k in range(len(grid))])
        last = functools.reduce(jnp.logical_and, [pl.program_id(k) == grid[k] - 1 for k in range(len(grid))])

        @pl.when(first)
        def _():
            rider.start(rin, rout, send, recv)

        body(*pre, *ins, *outs, *scr)

        @pl.when(last)
        def _():
            rider.wait(rin, rout, send, recv)

    return hosted


def _call(body, *, name, grid, in_specs, out_specs, out_shape, scratch=(), sem=None, aliases=None,
          prefetch=0, rider=None):
    if rider is not None:
        single = not isinstance(out_specs, (list, tuple))
        out_specs = [out_specs] if single else list(out_specs)
        out_shape = [out_shape] if single else list(out_shape)
        body = _with_rider(body, rider, grid, prefetch, len(in_specs), len(out_specs), len(scratch))
        aliases = dict(aliases or {})
        aliases.update({prefetch + len(in_specs) + k: len(out_specs) + v for k, v in rider.aliases.items()})
        in_specs = list(in_specs) + [ANY] * len(rider.ins)
        out_specs = out_specs + [ANY] * len(rider.out_shapes)
        out_shape = out_shape + list(rider.out_shapes)
        scratch = list(scratch) + [pltpu.SemaphoreType.DMA((rider.n_sem,)), pltpu.SemaphoreType.DMA((rider.n_sem,))]
        sem = ("arbitrary",) * len(grid)
        fn = _call(body, name=name + "_host", grid=grid, in_specs=in_specs, out_specs=out_specs, out_shape=out_shape,
                   scratch=scratch, sem=sem, aliases=aliases, prefetch=prefetch)
        return lambda *args: fn(*args, *rider.ins)
    kw = {}
    if aliases:
        kw["input_output_aliases"] = aliases
    if prefetch:
        gs = pltpu.PrefetchScalarGridSpec(num_scalar_prefetch=prefetch, grid=grid, in_specs=in_specs,
                                          out_specs=out_specs, scratch_shapes=list(scratch))
        return pl.pallas_call(body, name=name, grid_spec=gs, out_shape=out_shape,
                              compiler_params=_params(sem), **kw)
    return pl.pallas_call(body, name=name, grid=grid, in_specs=in_specs, out_specs=out_specs,
                          out_shape=out_shape, scratch_shapes=list(scratch),
                          compiler_params=_params(sem), **kw)


def _sds(shape, dtype):
    return jax.ShapeDtypeStruct(shape, dtype)


def _sigmoid(v):
    return 0.5 * jnp.tanh(0.5 * v) + 0.5


def _dot(a, b):
    return jnp.dot(a, b, preferred_element_type=F32)


def _dot_nt(a, b):
    return lax.dot_general(a, b, (((1,), (1,)), ((), ())), preferred_element_type=F32)


def _dot_tn(a, b):
    return lax.dot_general(a, b, (((0,), (0,)), ((), ())), preferred_element_type=F32)


def _in_proj_fwd(x, g, w, rider=None):
    L = x.shape[0]
    ns = w.shape[2]
    tn = ns
    nj = ns // tn
    TM = TM_PROJ

    def body(x_ref, g_ref, w_ref, z_ref, h_ref):
        @pl.when(pl.program_id(1) == 0)
        def _():
            xv = x_ref[...]
            r = lax.rsqrt(jnp.mean(xv * xv, axis=-1, keepdims=True) + EPS)
            h_ref[...] = (xv * r * g_ref[...]).astype(BF16)
        z_ref[...] = _dot(h_ref[...], w_ref[...]).astype(BF16)

    return _call(
        body, name="in_proj_fwd", grid=(L // TM, N_CHIPS * nj),
        in_specs=[pl.BlockSpec((TM, D_MODEL), lambda i, j: (i, 0)),
                  pl.BlockSpec((1, D_MODEL), lambda i, j: (0, 0)),
                  pl.BlockSpec((None, D_MODEL, tn), lambda i, j: (j // nj, 0, j % nj))],
        out_specs=[pl.BlockSpec((TM, tn), lambda i, j: (i, j)),
                   pl.BlockSpec((TM, D_MODEL), lambda i, j: (i, 0))],
        out_shape=[_sds((L, N_CHIPS * ns), BF16), _sds((L, D_MODEL), BF16)],
        sem=("parallel", "arbitrary"), rider=rider)(x, g, w)


DL_TILE = 512
SCALE = HEAD_DIM ** -0.5


def _perm_matrix(d):
    rho = jnp.arange(DL_TILE)
    src = rho // (DL_TILE // d) + d * (rho % (DL_TILE // d))
    return (src[:, None] == jnp.arange(DL_TILE)[None, :]).astype(BF16)


def _head_sum_matrix():
    h = jnp.arange(ATTN_WIDTH) // HEAD_DIM
    return (h[:, None] == h[None, :]).astype(BF16)


def _split(v):
    hi = v.astype(BF16)
    return hi, (v - hi.astype(F32)).astype(BF16)


def _head_sum(v, hs):
    vb = v.astype(BF16)
    half = ATTN_WIDTH // 2
    blk = hs[:half, :half]
    return jnp.concatenate([_dot(vb[:, :half], blk), _dot(vb[:, half:], blk)], axis=1)


def _permute(pm, v):
    hi, lo = _split(v)
    return _dot(pm, hi) + _dot(pm, lo)


def _dl_view(t, d):
    if d * BLK <= DL_TILE:
        return t
    return t.reshape(t.shape[0] // DL_TILE, d, DL_TILE // d, t.shape[1])


def _dl_spec(d, width, which):
    if d * BLK <= DL_TILE:
        per_tile = DL_TILE // (d * BLK)
        return pl.BlockSpec((BLK, width), lambda r, n: ((which(n) // per_tile) * (DL_TILE // BLK)
                                                       + r * per_tile + which(n) % per_tile, 0))
    tiles = d * BLK // DL_TILE
    return pl.BlockSpec((tiles, None, DL_TILE // d, width), lambda r, n: (which(n), r, 0, 0))


def _dl_read(ref):
    v = ref[...]
    return v if v.ndim == 2 else v.reshape(BLK, v.shape[-1])


def _dl_write(ref, v):
    ref[...] = v if len(ref.shape) == 2 else v.reshape(ref.shape)


def _qkv_prep(z, gq_t, gk_t, rider=None):
    L = z.shape[0]
    qkv_w = N_GROUPS * ATTN_WIDTH

    def body(zq_ref, zk_ref, zv_ref, gq_ref, gk_ref, hs_ref, p1_ref, p2_ref, *outs):
        hs = hs_ref[...]
        perms = (None, p1_ref[...], p2_ref[...])
        for g in range(N_GROUPS):
            cols = slice(g * ATTN_WIDTH, (g + 1) * ATTN_WIDTH)
            xq = zq_ref[:, cols].astype(F32)
            xk = zk_ref[:, cols].astype(F32)
            rq = lax.rsqrt(_head_sum(xq * xq, hs) * (1.0 / HEAD_DIM) + EPS)
            rk = lax.rsqrt(_head_sum(xk * xk, hs) * (1.0 / HEAD_DIM) + EPS)
            vals = [(xq * rq * (gq_ref[...] * SCALE)).astype(BF16), (xk * rk * gk_ref[...]).astype(BF16),
                    zv_ref[:, cols]]
            for j, t in enumerate(vals):
                if perms[g] is not None:
                    t = _dot(perms[g], t).astype(BF16)
                outs[3 * g + j][...] = t

    tile = pl.BlockSpec((DL_TILE, ATTN_WIDTH), lambda i: (i, 0))
    mat = pl.BlockSpec((DL_TILE, DL_TILE), lambda i: (0, 0))
    vec = pl.BlockSpec((1, ATTN_WIDTH), lambda i: (0, 0))
    outs = _call(
        body, name="qkv_prep", grid=(L // DL_TILE,),
        in_specs=[pl.BlockSpec((DL_TILE, qkv_w), lambda i: (i, 0)), pl.BlockSpec((DL_TILE, qkv_w), lambda i: (i, 1)),
                  pl.BlockSpec((DL_TILE, qkv_w), lambda i: (i, 2)), vec, vec, mat, mat, mat],
        out_specs=[tile] * 9, out_shape=[_sds((L, ATTN_WIDTH), BF16)] * 9,
        sem=("parallel",), rider=rider)(z, z, z, gq_t, gk_t, _head_sum_matrix(), _perm_matrix(ATTN_PATTERN[1][1]),
                                        _perm_matrix(ATTN_PATTERN[2][1]))
    return [tuple(outs[3 * g:3 * g + 3]) for g in range(N_GROUPS)], list(outs[3 * N_GROUPS:])


def _pair_masks():
    lane = lax.broadcasted_iota(jnp.int32, (1, LANES), 1)
    return lane < HEAD_DIM, lane >= HEAD_DIM


def _attn_fwd(qs, ks, v, gi):
    L = qs.shape[0]
    _, d = ATTN_PATTERN[gi]
    nb = L // (d * BLK)

    def body(q_ref, kc_ref, kp_ref, vc_ref, vp_ref, o_ref, l_ref):
        n = pl.program_id(1)
        qi = lax.broadcasted_iota(jnp.int32, (BLK, 2 * BLK), 0)
        kj = lax.broadcasted_iota(jnp.int32, (BLK, 2 * BLK), 1)
        prev = kj < BLK
        mask = jnp.logical_and(jnp.where(prev, kj, qi) >= jnp.where(prev, qi, kj - BLK),
                               kj >= jnp.where(n > 0, 0, BLK))
        q = _dl_read(q_ref)
        kw = jnp.concatenate([_dl_read(kp_ref), _dl_read(kc_ref)], axis=0)
        vw = jnp.concatenate([_dl_read(vp_ref), _dl_read(vc_ref)], axis=0)
        one = jnp.ones((2 * BLK, LANES), BF16)
        o_parts, l_parts = [], []
        for hp in range(N_HEADS // 2):
            ls = slice(hp * LANES, (hp + 1) * LANES)
            qp, kp_, vp_ = q[:, ls], kw[:, ls], vw[:, ls]
            num = jnp.zeros((BLK, LANES), F32)
            den = jnp.zeros((BLK, LANES), F32)
            mb = jnp.zeros((BLK, LANES), F32)
            for he in _pair_masks():
                s = jnp.where(mask, _dot_nt(jnp.where(he, qp, 0), kp_), -jnp.inf)
                m = jnp.max(s, axis=-1, keepdims=True)
                p = jnp.exp(s - m).astype(BF16)
                acc = _dot(p, jnp.concatenate([jnp.where(he, vp_, 0), jnp.where(he, one, 0)], axis=1))
                num += acc[:, :LANES]
                den += acc[:, LANES:]
                mb = jnp.where(he, m, mb)
            o_parts.append((num / den).astype(BF16))
            l_parts.append(mb + jnp.log(den))
        _dl_write(o_ref, jnp.concatenate(o_parts, axis=1))
        _dl_write(l_ref, jnp.concatenate(l_parts, axis=1))

    cur = _dl_spec(d, ATTN_WIDTH, lambda n: n)
    prev = _dl_spec(d, ATTN_WIDTH, lambda n: jnp.maximum(n - 1, 0))
    view = lambda t: _dl_view(t, d)
    o, l = _call(
        body, name=f"attn_fwd_g{gi}", grid=(d, nb), in_specs=[cur, cur, prev, cur, prev], out_specs=[cur, cur],
        out_shape=[_sds(view(qs).shape, BF16), _sds(view(qs).shape, F32)],
        sem=("parallel", "parallel"))(view(qs), view(ks), view(ks), view(v), view(v))
    return o.reshape(L, ATTN_WIDTH), l.reshape(L, ATTN_WIDTH)


def _to_token_order(os_, ls_, pts):
    o_tok, l_tok = [], []
    for o, l, pt in zip(os_, ls_, pts):
        if pt is None:
            o_tok.append(o.astype(F32))
            l_tok.append(l)
        else:
            o_tok.append(_dot(pt, o))
            l_tok.append(_permute(pt, l))
    return o_tok, l_tok


def _combine_fwd(os_, ls_):
    L = os_[0].shape[0]

    def body(o0, o1, o2, l0, l1, l2, pt1_ref, pt2_ref, a_ref):
        o_tok, l_tok = _to_token_order((o0[...], o1[...], o2[...]), (l0[...], l1[...], l2[...]),
                                       (None, pt1_ref[...], pt2_ref[...]))
        w = _combine_weights(*l_tok)
        a_ref[...] = (w[0] * o_tok[0] + w[1] * o_tok[1] + w[2] * o_tok[2]).astype(BF16)

    tile = pl.BlockSpec((DL_TILE, ATTN_WIDTH), lambda i: (i, 0))
    mat = pl.BlockSpec((DL_TILE, DL_TILE), lambda i: (0, 0))
    return _call(body, name="combine_fwd", grid=(L // DL_TILE,), in_specs=[tile] * 6 + [mat, mat], out_specs=tile,
                 out_shape=_sds((L, ATTN_WIDTH), BF16), sem=("parallel",))(
                     *os_, *ls_, _perm_matrix(ATTN_PATTERN[1][1]).T, _perm_matrix(ATTN_PATTERN[2][1]).T)


def _gelu(v):
    c = math.sqrt(2.0 / math.pi)
    return 0.5 * v * (1.0 + jnp.tanh(c * (v + 0.044715 * v * v * v)))


def _gelu_grad(v):
    c = math.sqrt(2.0 / math.pi)
    t = jnp.tanh(c * (v + 0.044715 * v * v * v))
    return 0.5 * (1.0 + t) + 0.5 * v * (1.0 - t * t) * c * (1.0 + 3.0 * 0.044715 * v * v)


def _ssm_fill(u, bwre_ref, bwim_ref, sre, sim):
    for k2 in range(N_PAIR):
        uw = u[:, _win_cols(k2)]
        _to_slabs(sre, k2, _dot(uw, bwre_ref[k2]))
        _to_slabs(sim, k2, _dot(uw, bwim_ref[k2]))


def _win_cols(k2):
    w = k2 // PAIRS_PER_WIN
    return slice(w * SSM_WIN, (w + 1) * SSM_WIN)


def _to_slabs(ref, k2, v):
    for half in range(2):
        for j in range(SSM_SUB):
            ref[2 * k2 + half, j * SSM_PITCH:j * SSM_PITCH + SSM_TC, :] = (
                v[j * SSM_TC:(j + 1) * SSM_TC, half * LANES:(half + 1) * LANES])


def _rows(i):
    return pl.ds(i, SSM_SUB, stride=SSM_PITCH)


def _slab_rows(ref, k):
    return jnp.concatenate([ref[k, j * SSM_PITCH:j * SSM_PITCH + SSM_TC, :] for j in range(SSM_SUB)], axis=0)


def _pair_rows(ref, k2):
    return jnp.concatenate([_slab_rows(ref, 2 * k2), _slab_rows(ref, 2 * k2 + 1)], axis=1).astype(BF16)


def _bcast(ref, k):
    return jnp.broadcast_to(ref[pl.ds(k, 1), :], (SSM_SUB, LANES))


def _scan(sre, sim, are_ref, aim_ref, k0, init, *, reverse, store, sign=1.0):
    ar = [_bcast(are_ref, k0 + kk) for kk in range(SCAN_GROUP)]
    ai = [sign * _bcast(aim_ref, k0 + kk) for kk in range(SCAN_GROUP)]

    def step(t, carry):
        i = SSM_TC - 1 - t if reverse else t
        out = []
        for kk in range(SCAN_GROUP):
            k = k0 + kk
            xr, xi = carry[2 * kk], carry[2 * kk + 1]
            nr = ar[kk] * xr - ai[kk] * xi + sre[k, _rows(i), :]
            ni = ar[kk] * xi + ai[kk] * xr + sim[k, _rows(i), :]
            if store:
                sre[k, _rows(i), :] = nr
                sim[k, _rows(i), :] = ni
            out += [nr, ni]
        return tuple(out)

    flat = []
    for re, im in init:
        flat += [re, im]
    res = lax.fori_loop(0, SSM_TC // 2, lambda t, c: step(2 * t + 1, step(2 * t, c)), tuple(flat))
    return [(res[2 * kk], res[2 * kk + 1]) for kk in range(SCAN_GROUP)]


def _ssm_seeds(ends_re, ends_im, a64re_ref, a64im_ref, carry_re, carry_im, seed_re, seed_im, k,
               *, reverse, sign=1.0):
    ar = a64re_ref[pl.ds(k, 1), :]
    ai = sign * a64im_ref[pl.ds(k, 1), :]
    cr = carry_re[pl.ds(k, 1), :]
    ci = carry_im[pl.ds(k, 1), :]
    order = range(SSM_SUB - 1, -1, -1) if reverse else range(SSM_SUB)
    for j in order:
        seed_re[k, pl.ds(j, 1), :] = cr
        seed_im[k, pl.ds(j, 1), :] = ci
        er = ends_re[k, pl.ds(j, 1), :]
        ei = ends_im[k, pl.ds(j, 1), :]
        cr, ci = ar * cr - ai * ci + er, ar * ci + ai * cr + ei
    carry_re[pl.ds(k, 1), :] = cr
    carry_im[pl.ds(k, 1), :] = ci


def _ssm_specs_consts():
    c2 = pl.BlockSpec((N_SLAB, LANES), lambda b: (0, 0))
    c3 = pl.BlockSpec((N_PAIR, SSM_WIN, SSM_WIN), lambda b: (0, 0, 0))
    return c2, c3


def _ssm_scratch():
    rows = SSM_SUB * SSM_PITCH
    return [pltpu.VMEM((N_SLAB, rows, LANES), F32), pltpu.VMEM((N_SLAB, rows, LANES), F32)]


def _ssm_fwd(z, pk, dskip, rider=None):
    L = z.shape[0]
    nb = L // SSM_TB
    ucol = (3 * N_GROUPS * ATTN_WIDTH) // SSM_WIDTH

    def body(u_ref, are_ref, aim_ref, a64re_ref, a64im_ref, bwre_ref, bwim_ref, cwre_ref, cwim_ref, d_ref,
             ypre_ref, yact_ref, sdre_ref, sdim_ref, sre, sim, carry_re, carry_im, ends_re, ends_im,
             seed_re, seed_im):
        @pl.when(pl.program_id(0) == 0)
        def _():
            carry_re[...] = jnp.zeros_like(carry_re)
            carry_im[...] = jnp.zeros_like(carry_im)

        u = u_ref[...]
        _ssm_fill(u, bwre_ref, bwim_ref, sre, sim)
        zero = jnp.zeros((SSM_SUB, LANES), F32)
        for k0 in range(0, N_SLAB, SCAN_GROUP):
            ends = _scan(sre, sim, are_ref, aim_ref, k0, [(zero, zero)] * SCAN_GROUP, reverse=False, store=False)
            for kk in range(SCAN_GROUP):
                ends_re[k0 + kk] = ends[kk][0]
                ends_im[k0 + kk] = ends[kk][1]
            for kk in range(SCAN_GROUP):
                _ssm_seeds(ends_re, ends_im, a64re_ref, a64im_ref, carry_re, carry_im, seed_re, seed_im,
                           k0 + kk, reverse=False)
            init = [(seed_re[k0 + kk], seed_im[k0 + kk]) for kk in range(SCAN_GROUP)]
            _scan(sre, sim, are_ref, aim_ref, k0, init, reverse=False, store=True)
        sdre_ref[...] = seed_re[...]
        sdim_ref[...] = seed_im[...]
        for w in range(N_PAIR // PAIRS_PER_WIN):
            acc = jnp.zeros((SSM_TB, SSM_WIN), F32)
            for kk in range(PAIRS_PER_WIN):
                k2 = w * PAIRS_PER_WIN + kk
                acc += _dot(_pair_rows(sre, k2), cwre_ref[k2])
                acc -= _dot(_pair_rows(sim, k2), cwim_ref[k2])
            cols = _win_cols(w * PAIRS_PER_WIN)
            ypre = acc + d_ref[:, cols] * u[:, cols].astype(F32)
            ypre_ref[:, cols] = ypre
            yact_ref[:, cols] = _gelu(ypre).astype(BF16)

    c2, c3 = _ssm_specs_consts()
    seed_spec = pl.BlockSpec((None, N_SLAB, SSM_SUB, LANES), lambda b: (b, 0, 0, 0))
    small = pltpu.VMEM((N_SLAB, LANES), F32)
    tile = pltpu.VMEM((N_SLAB, SSM_SUB, LANES), F32)
    return _call(
        body, name="ssm_fwd", grid=(nb,),
        in_specs=[pl.BlockSpec((SSM_TB, SSM_WIDTH), lambda b: (b, ucol)), c2, c2, c2, c2, c3, c3, c3, c3,
                  pl.BlockSpec((1, SSM_WIDTH), lambda b: (0, 0))],
        out_specs=[pl.BlockSpec((SSM_TB, SSM_WIDTH), lambda b: (b, 0)),
                   pl.BlockSpec((SSM_TB, SSM_WIDTH), lambda b: (b, 0)), seed_spec, seed_spec],
        out_shape=[_sds((L, SSM_WIDTH), F32), _sds((L, SSM_WIDTH), BF16),
                   _sds((nb, N_SLAB, SSM_SUB, LANES), F32), _sds((nb, N_SLAB, SSM_SUB, LANES), F32)],
        scratch=_ssm_scratch() + [small, small, tile, tile, tile, tile],
        sem=("arbitrary",), rider=rider)(z, pk["a_re"], pk["a_im"], pk["a64_re"], pk["a64_im"],
                                         pk["bw_re"].astype(BF16), pk["bw_im"].astype(BF16),
                                         pk["cw_re"].astype(BF16), pk["cw_im"].astype(BF16), dskip)


def _combine_weights(l0, l1, l2):
    m = jnp.maximum(jnp.maximum(l0, l1), l2)
    e0, e1, e2 = jnp.exp(l0 - m), jnp.exp(l1 - m), jnp.exp(l2 - m)
    inv = 1.0 / (e0 + e1 + e2)
    return e0 * inv, e1 * inv, e2 * inv


def _mix_fwd(x, z, a, yact, w_ap, w_ga, w_gb, w_out):
    L = x.shape[0]
    cs = D_MODEL // N_CHIPS
    ga_col = (3 * N_GROUPS * ATTN_WIDTH + SSM_WIDTH) // D_MODEL

    def body(x_ref, ga_ref, gs_ref, a_ref, y_ref, wap_ref, wga_ref, wgb_ref, wout_ref,
             x1_ref, aout_ref, sa_ref, sb_ref, mix_ref):
        a = a_ref[...]
        y = y_ref[...]
        for s in range(N_CHIPS):
            cols = slice(s * cs, (s + 1) * cs)
            aout_ref[:, cols] = _dot(a, wap_ref[s]).astype(BF16)
            sa_ref[:, cols] = _dot(y, wga_ref[s]).astype(BF16)
            sb_ref[:, cols] = _dot(y, wgb_ref[s]).astype(BF16)
        s_out = sa_ref[...].astype(F32) * _sigmoid(sb_ref[...].astype(F32))
        mix = (_sigmoid(ga_ref[...].astype(F32)) * aout_ref[...].astype(F32)
               + _sigmoid(gs_ref[...].astype(F32)) * s_out).astype(BF16)
        mix_ref[...] = mix
        x1_ref[...] = x_ref[...] + _dot(mix, wout_ref[...])

    tok = lambda w: pl.BlockSpec((TM_MIX, w), lambda i: (i, 0))
    wsm = pl.BlockSpec((N_CHIPS, ATTN_WIDTH, cs), lambda i: (0, 0, 0))
    return _call(
        body, name="mix_fwd", grid=(L // TM_MIX,),
        in_specs=[tok(D_MODEL), pl.BlockSpec((TM_MIX, D_MODEL), lambda i: (i, ga_col)),
                  pl.BlockSpec((TM_MIX, D_MODEL), lambda i: (i, ga_col + 1))]
                 + [tok(ATTN_WIDTH)] * 2 + [wsm, wsm, wsm, pl.BlockSpec((D_MODEL, D_MODEL), lambda i: (0, 0))],
        out_specs=[tok(D_MODEL), tok(D_MODEL), tok(D_MODEL), tok(D_MODEL), tok(D_MODEL)],
        out_shape=[_sds((L, D_MODEL), F32)] + [_sds((L, D_MODEL), BF16)] * 4,
        sem=("parallel",))(x, z, z, a, yact, w_ap, w_ga, w_gb, w_out.reshape(D_MODEL, D_MODEL))


def _ffn_fwd(x1, g, w_g, w_u, w_d, rider=None):
    L = x1.shape[0]
    fs = D_FF // N_CHIPS
    TM = TM_PROJ

    def body(x_ref, g_ref, wg_ref, wu_ref, wd_ref, x2_ref, h_ref, gate_ref, up_ref, act_ref, acc):
        s = pl.program_id(1)

        @pl.when(s == 0)
        def _():
            xv = x_ref[...]
            r = lax.rsqrt(jnp.mean(xv * xv, axis=-1, keepdims=True) + EPS)
            h_ref[...] = (xv * r * g_ref[...]).astype(BF16)
            acc[...] = jnp.zeros_like(acc)

        h = h_ref[...]
        gate = _dot_nt(h, wg_ref[...])
        up = _dot_nt(h, wu_ref[...])
        sg = _sigmoid(gate)
        silu = gate * sg
        act = (silu * up).astype(BF16)
        gate_ref[...] = (up * (sg * (1.0 + gate * (1.0 - sg)))).astype(BF16)
        up_ref[...] = silu.astype(BF16)
        act_ref[...] = act
        acc[...] += _dot(act, wd_ref[...])

        @pl.when(s == N_CHIPS - 1)
        def _():
            x2_ref[...] = x_ref[...] + acc[...]

    tok = pl.BlockSpec((TM, D_MODEL), lambda i, s: (i, 0))
    ffs = pl.BlockSpec((None, TM, fs), lambda i, s: (s, i, 0))
    return _call(
        body, name="ffn_fwd", grid=(L // TM, N_CHIPS),
        in_specs=[tok, pl.BlockSpec((1, D_MODEL), lambda i, s: (0, 0))]
                 + [pl.BlockSpec((None, fs, D_MODEL), lambda i, s: (s, 0, 0))] * 3,
        out_specs=[tok, tok, ffs, ffs, ffs],
        out_shape=[_sds((L, D_MODEL), F32), _sds((L, D_MODEL), BF16)] + [_sds((N_CHIPS, L, fs), BF16)] * 3,
        scratch=[pltpu.VMEM((TM, D_MODEL), F32)],
        sem=("parallel", "arbitrary"), rider=rider)(x1, g, w_g, w_u, w_d)


def _loss_head(xl, target):
    L = xl.shape[0]

    def body(x_ref, t_ref, dx_ref, loss_ref, acc):
        i = pl.program_id(0)

        @pl.when(i == 0)
        def _():
            acc[...] = jnp.zeros_like(acc)

        e = x_ref[...] - t_ref[...]
        dx_ref[...] = e * (1.0 / D_MODEL)
        acc[...] += jnp.sum((e * e).reshape(TM // SUBLANES, SUBLANES, D_MODEL), axis=0)

        @pl.when(i == pl.num_programs(0) - 1)
        def _():
            loss_ref[...] = (0.5 / D_MODEL) * jnp.sum(acc[...]).reshape(1, 1)

    tok = pl.BlockSpec((TM, D_MODEL), lambda i: (i, 0))
    return _call(
        body, name="loss_head", grid=(L // TM,), in_specs=[tok, tok],
        out_specs=[tok, pl.BlockSpec((1, 1), lambda i: (0, 0))],
        out_shape=[_sds((L, D_MODEL), F32), _sds((1, 1), F32)],
        scratch=[pltpu.VMEM((SUBLANES, D_MODEL), F32)], sem=("arbitrary",))(xl, target)


def _ssm_pack(lam_re, lam_im, log_dt, b_re, b_im, c_re, c_im):
    dt = jnp.exp(log_dt)[:, None]
    mag = jnp.exp(lam_re * dt)
    ang = lam_im * dt
    ar = mag * jnp.cos(ang)
    ai = mag * jnp.sin(ang)
    nr = ar - 1.0
    ni = ai
    den = lam_re * lam_re + lam_im * lam_im
    cr = ((nr * lam_re + ni * lam_im) / den)[..., None]
    ci = ((ni * lam_re - nr * lam_im) / den)[..., None]
    bbr = cr * b_re - ci * b_im
    bbi = cr * b_im + ci * b_re
    gpp = SSM_WIN // SSM_STATE
    gpw = SSM_WIN // SSM_GROUP
    k2 = jnp.arange(N_PAIR)[:, None, None]
    gs = jnp.arange(gpp)[None, :, None]
    gl = jnp.arange(gpw)[None, None, :]
    same = (gl == gpp * (k2 % PAIRS_PER_WIN) + gs).astype(F32)

    def b_windows(bb):
        return jnp.einsum('kgl,kgpc->klcgp', same, bb.reshape(N_PAIR, gpp, SSM_STATE, SSM_GROUP)).reshape(
            N_PAIR, SSM_WIN, SSM_WIN)

    def c_windows(cc):
        return jnp.einsum('kgl,kgcp->kgplc', same, cc.reshape(N_PAIR, gpp, SSM_GROUP, SSM_STATE)).reshape(
            N_PAIR, SSM_WIN, SSM_WIN)

    pr, pi = ar, ai
    for _ in range(int(math.log2(SSM_TC))):
        pr, pi = pr * pr - pi * pi, 2.0 * pr * pi
    return dict(a_re=ar.reshape(N_SLAB, LANES), a_im=ai.reshape(N_SLAB, LANES),
                a64_re=pr.reshape(N_SLAB, LANES), a64_im=pi.reshape(N_SLAB, LANES),
                bw_re=b_windows(bbr), bw_im=b_windows(bbi), cw_re=c_windows(c_re), cw_im=c_windows(c_im))


def _layer_fwd(x, p, rest, rest_stage, next_bufs=None):
    first = {"ici": _gather_ici_rider, "d2d": _gather_d2d_rider}[rest_stage]
    outs = _in_proj_fwd(x, p["g_mix"], p["w_in"], first(rest))
    (z, h), rest = outs[:2], list(outs[2:])
    qkv, got = _qkv_prep(z, jnp.tile(p["g_q"], (1, N_HEADS)), jnp.tile(p["g_k"], (1, N_HEADS)),
                         _gather_d2d_rider(rest) if rest_stage == "ici" else None)
    p = {**p, **dict(zip(BIG[1:], got if rest_stage == "ici" else rest))}
    os_, ls_ = [], []
    for gi in range(N_GROUPS):
        o, l = _attn_fwd(*qkv[gi], gi)
        os_.append(o)
        ls_.append(l)
    a = _combine_fwd(os_, ls_)
    pk = _ssm_pack(p["lambda_re"], p["lambda_im"], p["log_dt"], p["b_re"], p["b_im"], p["c_re"], p["c_im"])
    outs = _ssm_fwd(z, pk, p["d_skip"], _gather_ici_rider(next_bufs[:1]) if next_bufs else None)
    (ypre, yact, sd_re, sd_im), next_in = outs[:4], list(outs[4:])
    x1, aout, sa, sb, mix = _mix_fwd(x, z, a, yact, p["w_attn_proj"], p["w_glu_a"], p["w_glu_b"], p["w_out"])
    outs = _ffn_fwd(x1, p["g_ffn"], p["w_ffn_gate"], p["w_ffn_up"], p["w_ffn_down"],
                    _join_riders(_gather_ici_rider(next_bufs[1:]), _gather_d2d_rider(next_in)) if next_bufs else None)
    x2, h2, gate, up, act = outs[:5]
    nxt = (outs[-1], list(outs[5:-1])) if next_bufs else None
    saved = dict(x=x, z=z, h=h, qkv=qkv, os=os_, ls=ls_, pk=pk, ypre=ypre, yact=yact, sd_re=sd_re, sd_im=sd_im,
                 x1=x1, a=a, aout=aout, sa=sa, sb=sb, mix=mix, h2=h2, gate=gate, up=up, act=act)
    return x2, saved, p, nxt


def _rms_bwd(xv, g, dh):
    r = lax.rsqrt(jnp.mean(xv * xv, axis=-1, keepdims=True) + EPS)
    xn = xv * r
    dxn = dh * g
    dx = r * (dxn - xn * jnp.mean(dxn * xn, axis=-1, keepdims=True))
    dg = jnp.sum((dh * xn).reshape(xv.shape[0] // SUBLANES, SUBLANES, xv.shape[1]), axis=0)
    return dx, dg


def _ffn_bwd_act(dx2, gate, up, w_d):
    L = dx2.shape[0]
    fs = D_FF // N_CHIPS
    TM = TM_PROJ

    def body(dx_ref, dact_dgate_ref, dact_dup_ref, wd_ref, dgate_ref, dup_ref, dxb_ref):
        @pl.when(pl.program_id(1) == 0)
        def _():
            dxb_ref[...] = dx_ref[...].astype(BF16)

        dact = _dot_nt(dxb_ref[...], wd_ref[...])
        dgate_ref[...] = (dact * dact_dgate_ref[...].astype(F32)).astype(BF16)
        dup_ref[...] = (dact * dact_dup_ref[...].astype(F32)).astype(BF16)

    ffs = pl.BlockSpec((None, TM, fs), lambda i, s: (s, i, 0))
    tok = pl.BlockSpec((TM, D_MODEL), lambda i, s: (i, 0))
    return _call(
        body, name="ffn_bwd_act", grid=(L // TM, N_CHIPS),
        in_specs=[tok, ffs, ffs, pl.BlockSpec((None, fs, D_MODEL), lambda i, s: (s, 0, 0))],
        out_specs=[ffs, ffs, tok], out_shape=[_sds((N_CHIPS, L, fs), BF16)] * 2 + [_sds((L, D_MODEL), BF16)],
        sem=("parallel", "arbitrary"))(dx2, gate, up, w_d)


def _ffn_bwd_in(dx2, x1, g, dgate, dup, w_g, w_u, rider=None):
    L = x1.shape[0]
    fs = D_FF // N_CHIPS
    TM = TM_PROJ

    def body(dx_ref, x_ref, g_ref, dgate_ref, dup_ref, wg_ref, wu_ref, dx1_ref, dg_ref, acc, dgacc):
        i, s = pl.program_id(0), pl.program_id(1)

        @pl.when(s == 0)
        def _():
            acc[...] = jnp.zeros_like(acc)

        @pl.when(jnp.logical_and(i == 0, s == 0))
        def _():
            dgacc[...] = jnp.zeros_like(dgacc)

        acc[...] += _dot(dgate_ref[...], wg_ref[...]) + _dot(dup_ref[...], wu_ref[...])

        @pl.when(s == N_CHIPS - 1)
        def _():
            dx, dg = _rms_bwd(x_ref[...], g_ref[...], acc[...])
            dx1_ref[...] = dx_ref[...] + dx
            dgacc[...] += dg

        @pl.when(jnp.logical_and(i == pl.num_programs(0) - 1, s == N_CHIPS - 1))
        def _():
            dg_ref[...] = jnp.sum(dgacc[...], axis=0, keepdims=True)

    tok = pl.BlockSpec((TM, D_MODEL), lambda i, s: (i, 0))
    ffs = pl.BlockSpec((None, TM, fs), lambda i, s: (s, i, 0))
    vec = pl.BlockSpec((1, D_MODEL), lambda i, s: (0, 0))
    return _call(
        body, name="ffn_bwd_in", grid=(L // TM, N_CHIPS),
        in_specs=[tok, tok, vec, ffs, ffs,
                  pl.BlockSpec((None, fs, D_MODEL), lambda i, s: (s, 0, 0)),
                  pl.BlockSpec((None, fs, D_MODEL), lambda i, s: (s, 0, 0))],
        out_specs=[tok, vec],
        out_shape=[_sds((L, D_MODEL), F32), _sds((1, D_MODEL), F32)],
        scratch=[pltpu.VMEM((TM, D_MODEL), F32), pltpu.VMEM((SUBLANES, D_MODEL), F32)],
        sem=("arbitrary", "arbitrary"), rider=rider)(dx2, x1, g, dgate, dup, w_g, w_u)


def _wgrad(a, b, *, name, grid_kn, a_spec, b_spec, out_shape, out_spec):
    L = a.shape[-2]
    nl = L // TL_WGRAD

    def body(a_ref, b_ref, o_ref):
        @pl.when(pl.program_id(2) == 0)
        def _():
            o_ref[...] = jnp.zeros_like(o_ref)
        o_ref[...] += _dot_tn(a_ref[...].astype(BF16), b_ref[...].astype(BF16))

    return _call(body, name=name, grid=(*grid_kn, nl), in_specs=[a_spec, b_spec], out_specs=out_spec,
                 out_shape=out_shape, sem=("parallel", "parallel", "arbitrary"))(a, b)


def _wgrad_cols(a, b, name):
    K, N = a.shape[1], b.shape[1]
    ns = N // N_CHIPS
    if N * K * 4 <= 4 * 1024 * 1024:
        L = a.shape[0]

        def body(a_ref, b_ref, o_ref):
            @pl.when(pl.program_id(0) == 0)
            def _():
                o_ref[...] = jnp.zeros_like(o_ref)
            av = a_ref[...].astype(BF16)
            for s in range(N_CHIPS):
                o_ref[s] += _dot_tn(av, b_ref[:, s * ns:(s + 1) * ns].astype(BF16))

        return _call(body, name=name, grid=(L // TL_WGRAD,),
                     in_specs=[pl.BlockSpec((TL_WGRAD, K), lambda t: (t, 0)),
                               pl.BlockSpec((TL_WGRAD, N), lambda t: (t, 0))],
                     out_specs=pl.BlockSpec((N_CHIPS, K, ns), lambda t: (0, 0, 0)),
                     out_shape=_sds((N_CHIPS, K, ns), F32), sem=("arbitrary",))(a, b)
    tn = ns // 2 if ns % (2 * LANES) == 0 else ns
    nj = ns // tn
    return _wgrad(a, b, name=name, grid_kn=(1, N_CHIPS * nj),
                  a_spec=pl.BlockSpec((TL_WGRAD, K), lambda i, j, t: (t, 0)),
                  b_spec=pl.BlockSpec((TL_WGRAD, tn), lambda i, j, t: (t, j)),
                  out_shape=_sds((N_CHIPS, K, ns), F32),
                  out_spec=pl.BlockSpec((None, K, tn), lambda i, j, t: (j // nj, 0, j % nj)))


def _wgrad_full(a, b, name):
    K, N = a.shape[1], b.shape[1]
    return _wgrad(a, b, name=name, grid_kn=(1, 1),
                  a_spec=pl.BlockSpec((TL_WGRAD, K), lambda i, j, t: (t, 0)),
                  b_spec=pl.BlockSpec((TL_WGRAD, N), lambda i, j, t: (t, 0)),
                  out_shape=_sds((K, N), F32), out_spec=pl.BlockSpec((K, N), lambda i, j, t: (0, 0)))


def _wgrad_ff_rows(a, b, name):
    fs, N = a.shape[2], b.shape[1]
    return _wgrad(a, b, name=name, grid_kn=(N_CHIPS, 1),
                  a_spec=pl.BlockSpec((None, TL_WGRAD, fs), lambda i, j, t: (i, t, 0)),
                  b_spec=pl.BlockSpec((TL_WGRAD, N), lambda i, j, t: (t, 0)),
                  out_shape=_sds((N_CHIPS, fs, N), F32),
                  out_spec=pl.BlockSpec((None, fs, N), lambda i, j, t: (i, 0, 0)))


def _mix_bwd(dx, z, aout, sa, sb, ypre, w_ap, w_ga, w_gb, w_out, rider=None):
    L = dx.shape[0]
    cs = D_MODEL // N_CHIPS
    ga_col = (3 * N_GROUPS * ATTN_WIDTH + SSM_WIDTH) // D_MODEL

    def body(dx_ref, ga_ref, gs_ref, aout_ref, sa_ref, sb_ref, ypre_ref, wap_ref, wga_ref, wgb_ref, wout_ref,
             dgates_ref, da_ref, gy_ref, daout_ref, dsa_ref, dsb_ref):
        dmix = _dot_nt(dx_ref[...].astype(BF16), wout_ref[...])
        sig_a = _sigmoid(ga_ref[...].astype(F32))
        sig_s = _sigmoid(gs_ref[...].astype(F32))
        a_out = aout_ref[...].astype(F32)
        s_a = sa_ref[...].astype(F32)
        sig_b = _sigmoid(sb_ref[...].astype(F32))
        s_out = s_a * sig_b
        daout = (dmix * sig_a).astype(BF16)
        daout_ref[...] = daout
        dgates_ref[:, :D_MODEL] = (dmix * a_out * sig_a * (1.0 - sig_a)).astype(BF16)
        dgates_ref[:, D_MODEL:] = (dmix * s_out * sig_s * (1.0 - sig_s)).astype(BF16)
        ds_out = dmix * sig_s
        dsa = (ds_out * sig_b).astype(BF16)
        dsb = (ds_out * s_a * sig_b * (1.0 - sig_b)).astype(BF16)
        dsa_ref[...] = dsa
        dsb_ref[...] = dsb
        da = jnp.zeros((TM_MIX, ATTN_WIDTH), F32)
        dy = jnp.zeros((TM_MIX, SSM_WIDTH), F32)
        for s in range(N_CHIPS):
            cols = slice(s * cs, (s + 1) * cs)
            da += _dot_nt(daout[:, cols], wap_ref[s])
            dy += _dot_nt(dsa[:, cols], wga_ref[s]) + _dot_nt(dsb[:, cols], wgb_ref[s])
        gy_ref[...] = dy * _gelu_grad(ypre_ref[...])
        da_ref[...] = da

    tok = lambda w: pl.BlockSpec((TM_MIX, w), lambda i: (i, 0))
    wsm = pl.BlockSpec((N_CHIPS, ATTN_WIDTH, cs), lambda i: (0, 0, 0))
    return _call(
        body, name="mix_bwd", grid=(L // TM_MIX,),
        in_specs=[tok(D_MODEL), pl.BlockSpec((TM_MIX, D_MODEL), lambda i: (i, ga_col)),
                  pl.BlockSpec((TM_MIX, D_MODEL), lambda i: (i, ga_col + 1)),
                  tok(D_MODEL), tok(D_MODEL), tok(D_MODEL), tok(SSM_WIDTH),
                  wsm, wsm, wsm, pl.BlockSpec((D_MODEL, D_MODEL), lambda i: (0, 0))],
        out_specs=[tok(2 * D_MODEL), tok(ATTN_WIDTH), tok(SSM_WIDTH)] + [tok(D_MODEL)] * 3,
        out_shape=[_sds((L, 2 * D_MODEL), BF16), _sds((L, ATTN_WIDTH), F32), _sds((L, SSM_WIDTH), F32)]
                  + [_sds((L, D_MODEL), BF16)] * 3,
        sem=("parallel",), rider=rider)(dx, z, z, aout, sa, sb, ypre, w_ap, w_ga, w_gb,
                                        w_out.reshape(D_MODEL, D_MODEL))


def _combine_bwd(da, os_, ls_):
    L = da.shape[0]

    def body(da_ref, o0, o1, o2, l0, l1, l2, hs_ref, p1_ref, p2_ref, pt1_ref, pt2_ref,
             do0, do1, do2, c0, c1, c2):
        o_tok, l_tok = _to_token_order((o0[...], o1[...], o2[...]), (l0[...], l1[...], l2[...]),
                                       (None, pt1_ref[...], pt2_ref[...]))
        w = _combine_weights(*l_tok)
        dav = da_ref[...]
        hs = hs_ref[...]
        tbar = sum(wg * _head_sum(dav * og, hs) for wg, og in zip(w, o_tok))
        for wg, pm, do_ref, c_ref in zip(w, (None, p1_ref[...], p2_ref[...]), (do0, do1, do2), (c0, c1, c2)):
            dog = (wg * dav).astype(BF16)
            cg = -wg * tbar
            do_ref[...] = dog if pm is None else _dot(pm, dog).astype(BF16)
            c_ref[...] = cg if pm is None else _dot(pm, cg.astype(BF16))

    tile = pl.BlockSpec((DL_TILE, ATTN_WIDTH), lambda i: (i, 0))
    mat = pl.BlockSpec((DL_TILE, DL_TILE), lambda i: (0, 0))
    p1, p2 = _perm_matrix(ATTN_PATTERN[1][1]), _perm_matrix(ATTN_PATTERN[2][1])
    outs = _call(body, name="combine_bwd", grid=(L // DL_TILE,), in_specs=[tile] * 7 + [mat] * 5,
                 out_specs=[tile] * 6,
                 out_shape=[_sds((L, ATTN_WIDTH), BF16)] * 3 + [_sds((L, ATTN_WIDTH), F32)] * 3,
                 sem=("parallel",))(da, *os_, *ls_, _head_sum_matrix(), p1, p2, p1.T, p2.T)
    return outs[:3], outs[3:]


def _attn_bwd(qs, ks, v, do, l, c, gi, rider=None):
    L = qs.shape[0]
    _, d = ATTN_PATTERN[gi]
    nb = L // (d * BLK)

    def body(q0_ref, q1_ref, k_ref, v_ref, do0_ref, do1_ref, l0_ref, l1_ref, c0_ref, c1_ref,
             dq_ref, dk_ref, dv_ref, carry):
        n = pl.program_id(1)

        @pl.when(n == 0)
        def _():
            carry[...] = jnp.zeros_like(carry)

        qi = lax.broadcasted_iota(jnp.int32, (2 * BLK, BLK), 0)
        kj = lax.broadcasted_iota(jnp.int32, (2 * BLK, BLK), 1)
        first = qi < BLK
        mask = jnp.logical_and(jnp.where(first, qi, kj) >= jnp.where(first, kj, qi - BLK),
                               qi < jnp.where(n < nb - 1, 2 * BLK, BLK))
        q2 = jnp.concatenate([_dl_read(q0_ref), _dl_read(q1_ref)], axis=0)
        do2 = jnp.concatenate([_dl_read(do0_ref), _dl_read(do1_ref)], axis=0)
        l2 = jnp.concatenate([_dl_read(l0_ref), _dl_read(l1_ref)], axis=0)
        c2 = jnp.concatenate([_dl_read(c0_ref), _dl_read(c1_ref)], axis=0)
        k = _dl_read(k_ref)
        v_ = _dl_read(v_ref)
        h0, h1 = _pair_masks()
        mask2 = jnp.concatenate([mask, mask], axis=1)
        dq_parts, dk_parts, dv_parts = [], [], []
        for hp in range(N_HEADS // 2):
            ls = slice(hp * LANES, (hp + 1) * LANES)
            qp, dop, kp_, vp_ = q2[:, ls], do2[:, ls], k[:, ls], v_[:, ls]
            kk = jnp.concatenate([jnp.where(h0, kp_, 0), jnp.where(h1, kp_, 0)], axis=0)
            vv = jnp.concatenate([jnp.where(h0, vp_, 0), jnp.where(h1, vp_, 0)], axis=0)

            def per_head(t):
                a = jnp.broadcast_to(t[:, hp * LANES:hp * LANES + 1], (2 * BLK, BLK))
                b = jnp.broadcast_to(t[:, hp * LANES + HEAD_DIM:hp * LANES + HEAD_DIM + 1], (2 * BLK, BLK))
                return jnp.concatenate([a, b], axis=1)

            p = jnp.where(mask2, jnp.exp(_dot_nt(qp, kk) - per_head(l2)), 0.0)
            ds = (p * (_dot_nt(dop, vv) + per_head(c2))).astype(BF16)
            dv2 = _dot_tn(p.astype(BF16), dop)
            dk2 = _dot_tn(ds, qp)
            dq2 = _dot(ds, kk)
            dq_parts.append((dq2[:BLK] + carry[:, ls]).astype(BF16))
            carry[:, ls] = dq2[BLK:]
            dk_parts.append(jnp.where(h0, dk2[:BLK], dk2[BLK:]).astype(BF16))
            dv_parts.append(jnp.where(h0, dv2[:BLK], dv2[BLK:]).astype(BF16))
        _dl_write(dq_ref, jnp.concatenate(dq_parts, axis=1))
        _dl_write(dk_ref, jnp.concatenate(dk_parts, axis=1))
        _dl_write(dv_ref, jnp.concatenate(dv_parts, axis=1))

    cur = _dl_spec(d, ATTN_WIDTH, lambda n: n)
    nxt = _dl_spec(d, ATTN_WIDTH, lambda n: jnp.minimum(n + 1, nb - 1))
    view = lambda t: _dl_view(t, d)
    outs = _call(
        body, name=f"attn_bwd_g{gi}", grid=(d, nb),
        in_specs=[cur, nxt, cur, cur, cur, nxt, cur, nxt, cur, nxt], out_specs=[cur, cur, cur],
        out_shape=[_sds(view(qs).shape, BF16)] * 3, scratch=[pltpu.VMEM((BLK, ATTN_WIDTH), F32)],
        sem=("parallel", "arbitrary"), rider=rider)(view(qs), view(qs), view(ks), view(v), view(do), view(do),
                                                    view(l), view(l), view(c), view(c))
    return [t.reshape(L, ATTN_WIDTH) for t in outs[:3]], list(outs[3:])


def _qkv_post(z, dqkv, du, dgates, gq_t, gk_t):
    L = z.shape[0]
    qkv_w = N_GROUPS * ATTN_WIDTH

    def body(zq_ref, zk_ref, gq_ref, gk_ref, hs_ref, pt1_ref, pt2_ref, du_ref, dgates_ref, *rest):
        dl_refs, (dz_ref, dgq_ref, dgk_ref) = rest[:9], rest[9:]

        @pl.when(pl.program_id(0) == 0)
        def _():
            dgq_ref[...] = jnp.zeros_like(dgq_ref)
            dgk_ref[...] = jnp.zeros_like(dgk_ref)

        hs = hs_ref[...]
        pts = (None, pt1_ref[...], pt2_ref[...])

        def rows8(t):
            return jnp.sum(t.reshape(DL_TILE // SUBLANES, SUBLANES, ATTN_WIDTH), axis=0)

        def norm_bwd(x, gain, dn):
            r = lax.rsqrt(_head_sum(x * x, hs) * (1.0 / HEAD_DIM) + EPS)
            xh = x * r
            dh = dn * gain
            return r * (dh - xh * (_head_sum(dh * xh, hs) * (1.0 / HEAD_DIM))), rows8(dn * xh)

        for g in range(N_GROUPS):
            tok = [t[...].astype(F32) if pts[g] is None else _dot(pts[g], t[...]) for t in dl_refs[3 * g:3 * g + 3]]
            cols = slice(g * ATTN_WIDTH, (g + 1) * ATTN_WIDTH)
            dq, pq = norm_bwd(zq_ref[:, cols].astype(F32), gq_ref[...] * SCALE, tok[0])
            dk, pk_ = norm_bwd(zk_ref[:, cols].astype(F32), gk_ref[...], tok[1])
            dgq_ref[...] += pq * SCALE
            dgk_ref[...] += pk_
            dz_ref[:, cols] = dq.astype(BF16)
            dz_ref[:, qkv_w + g * ATTN_WIDTH:qkv_w + (g + 1) * ATTN_WIDTH] = dk.astype(BF16)
            dz_ref[:, 2 * qkv_w + g * ATTN_WIDTH:2 * qkv_w + (g + 1) * ATTN_WIDTH] = tok[2].astype(BF16)
        dz_ref[:, 3 * qkv_w:3 * qkv_w + SSM_WIDTH] = du_ref[...]
        dz_ref[:, 3 * qkv_w + SSM_WIDTH:] = dgates_ref[...]

    tile = lambda w: pl.BlockSpec((DL_TILE, w), lambda i: (i, 0))
    mat = pl.BlockSpec((DL_TILE, DL_TILE), lambda i: (0, 0))
    vec = pl.BlockSpec((1, ATTN_WIDTH), lambda i: (0, 0))
    acc = pl.BlockSpec((SUBLANES, ATTN_WIDTH), lambda i: (0, 0))
    flat = [t for grp in dqkv for t in grp]
    return _call(
        body, name="qkv_post", grid=(L // DL_TILE,),
        in_specs=[tile(qkv_w), pl.BlockSpec((DL_TILE, qkv_w), lambda i: (i, 1)), vec, vec, mat, mat, mat,
                  tile(SSM_WIDTH), tile(2 * D_MODEL)] + [tile(ATTN_WIDTH)] * 9,
        out_specs=[tile(IN_COLS), acc, acc],
        out_shape=[_sds((L, IN_COLS), BF16), _sds((SUBLANES, ATTN_WIDTH), F32), _sds((SUBLANES, ATTN_WIDTH), F32)],
        sem=("arbitrary",))(z, z, gq_t, gk_t, _head_sum_matrix(), _perm_matrix(ATTN_PATTERN[1][1]).T,
                            _perm_matrix(ATTN_PATTERN[2][1]).T, du, dgates, *flat)


def _scan_rev_grad(sre, sim, rre, rim, are_ref, aim_ref, k0, init, seed_re, seed_im):
    ar = [_bcast(are_ref, k0 + kk) for kk in range(SCAN_GROUP)]
    ai = [-_bcast(aim_ref, k0 + kk) for kk in range(SCAN_GROUP)]

    def update(i, xprev, carry):
        out = []
        for kk in range(SCAN_GROUP):
            k = k0 + kk
            lr, li, dr, di = carry[4 * kk:4 * kk + 4]
            nr = ar[kk] * lr - ai[kk] * li + rre[k, _rows(i), :]
            ni = ar[kk] * li + ai[kk] * lr + rim[k, _rows(i), :]
            rre[k, _rows(i), :] = nr
            rim[k, _rows(i), :] = ni
            xr, xi = xprev(k)
            out += [nr, ni, dr + xr * nr + xi * ni, di + xr * ni - xi * nr]
        return tuple(out)

    def step(t, carry):
        i = SSM_TC - 1 - t
        return update(i, lambda k: (sre[k, _rows(i - 1), :], sim[k, _rows(i - 1), :]), carry)

    zero = jnp.zeros((SSM_SUB, LANES), F32)
    flat = []
    for re, im in init:
        flat += [re, im, zero, zero]
    res = lax.fori_loop(0, (SSM_TC - 1) // 2, lambda t, c: step(2 * t + 1, step(2 * t, c)), tuple(flat))
    res = step(SSM_TC - 2, res)
    res = update(0, lambda k: (seed_re[k], seed_im[k]), res)
    return [(res[4 * kk + 2], res[4 * kk + 3]) for kk in range(SCAN_GROUP)]


def _ssm_bwd(z, gy, pk, dskip, sd_re, sd_im, rider=None):
    L = z.shape[0]
    nb = L // SSM_TB
    ucol = (3 * N_GROUPS * ATTN_WIDTH) // SSM_WIDTH
    nwin = N_PAIR // PAIRS_PER_WIN

    def body(u_ref, gy_ref, are_ref, aim_ref, a64re_ref, a64im_ref, bwre_ref, bwim_ref, cwre_ref, cwim_ref, d_ref,
             sdre_ref, sdim_ref,
             du_ref, dare_ref, daim_ref, dbre_ref, dbim_ref, dcre_ref, dcim_ref, dd_ref,
             sre, sim, rre, rim, carry_re, carry_im, ends_re, ends_im, seed_re, seed_im):
        @pl.when(pl.program_id(0) == 0)
        def _():
            carry_re[...] = jnp.zeros_like(carry_re)
            carry_im[...] = jnp.zeros_like(carry_im)
            for ref in (dare_ref, daim_ref, dbre_ref, dbim_ref, dcre_ref, dcim_ref, dd_ref):
                ref[...] = jnp.zeros_like(ref)

        u = u_ref[...]
        gyv = gy_ref[...]
        gyb = gyv.astype(BF16)
        _ssm_fill(u, bwre_ref, bwim_ref, sre, sim)
        for k2 in range(N_PAIR):
            gw = gyb[:, _win_cols(k2)]
            _to_slabs(rre, k2, _dot_nt(gw, cwre_ref[k2]))
            _to_slabs(rim, k2, -_dot_nt(gw, cwim_ref[k2]))
        zero = jnp.zeros((SSM_SUB, LANES), F32)
        for k0 in range(0, N_SLAB, SCAN_GROUP):
            grp = range(k0, k0 + SCAN_GROUP)
            _scan(sre, sim, are_ref, aim_ref, k0, [(sdre_ref[k], sdim_ref[k]) for k in grp],
                  reverse=False, store=True)
            ends = _scan(rre, rim, are_ref, aim_ref, k0, [(zero, zero)] * SCAN_GROUP, reverse=True, store=False,
                         sign=-1.0)
            for kk, k in enumerate(grp):
                ends_re[k] = ends[kk][0]
                ends_im[k] = ends[kk][1]
            for k in grp:
                _ssm_seeds(ends_re, ends_im, a64re_ref, a64im_ref, carry_re, carry_im, seed_re, seed_im, k,
                           reverse=True, sign=-1.0)
            das = _scan_rev_grad(sre, sim, rre, rim, are_ref, aim_ref, k0,
                                 [(seed_re[k], seed_im[k]) for k in grp], sdre_ref, sdim_ref)
            for kk, k in enumerate(grp):
                dare_ref[k] += das[kk][0]
                daim_ref[k] += das[kk][1]
        for w in range(nwin):
            cols = _win_cols(w * PAIRS_PER_WIN)
            uw = u[:, cols]
            gw = gyb[:, cols]
            acc = gyv[:, cols] * d_ref[:, cols]
            for kk in range(PAIRS_PER_WIN):
                k2 = w * PAIRS_PER_WIN + kk
                lr = _pair_rows(rre, k2)
                li = _pair_rows(rim, k2)
                acc += _dot_nt(lr, bwre_ref[k2]) + _dot_nt(li, bwim_ref[k2])
                dbre_ref[k2] += _dot_tn(uw, lr)
                dbim_ref[k2] += _dot_tn(uw, li)
                dcre_ref[k2] += _dot_tn(_pair_rows(sre, k2), gw)
                dcim_ref[k2] -= _dot_tn(_pair_rows(sim, k2), gw)
            du_ref[:, cols] = acc.astype(BF16)
        dd_ref[...] += jnp.sum((gyv * u.astype(F32)).reshape(SSM_TB // SUBLANES, SUBLANES, SSM_WIDTH), axis=0)

    c2, c3 = _ssm_specs_consts()
    rev = lambda b: nb - 1 - b
    seed_spec = pl.BlockSpec((None, N_SLAB, SSM_SUB, LANES), lambda b: (rev(b), 0, 0, 0))
    tile_out = pl.BlockSpec((N_SLAB, SSM_SUB, LANES), lambda b: (0, 0, 0))
    small = pltpu.VMEM((N_SLAB, LANES), F32)
    tile = pltpu.VMEM((N_SLAB, SSM_SUB, LANES), F32)
    return _call(
        body, name="ssm_bwd", grid=(nb,),
        in_specs=[pl.BlockSpec((SSM_TB, SSM_WIDTH), lambda b: (rev(b), ucol)),
                  pl.BlockSpec((SSM_TB, SSM_WIDTH), lambda b: (rev(b), 0)),
                  c2, c2, c2, c2, c3, c3, c3, c3, pl.BlockSpec((1, SSM_WIDTH), lambda b: (0, 0)),
                  seed_spec, seed_spec],
        out_specs=[pl.BlockSpec((SSM_TB, SSM_WIDTH), lambda b: (rev(b), 0)), tile_out, tile_out, c3, c3, c3, c3,
                   pl.BlockSpec((SUBLANES, SSM_WIDTH), lambda b: (0, 0))],
        out_shape=[_sds((L, SSM_WIDTH), BF16), _sds((N_SLAB, SSM_SUB, LANES), F32),
                   _sds((N_SLAB, SSM_SUB, LANES), F32)] + [_sds((N_PAIR, SSM_WIN, SSM_WIN), F32)] * 4
                  + [_sds((SUBLANES, SSM_WIDTH), F32)],
        scratch=_ssm_scratch() + _ssm_scratch() + [small, small, tile, tile, tile, tile],
        sem=("arbitrary",), rider=rider)(z, gy, pk["a_re"], pk["a_im"], pk["a64_re"], pk["a64_im"],
                            pk["bw_re"].astype(BF16), pk["bw_im"].astype(BF16),
                            pk["cw_re"].astype(BF16), pk["cw_im"].astype(BF16), dskip, sd_re, sd_im)


def _in_proj_bwd(dz, w, x, g, dres, rider=None):
    L = x.shape[0]
    ns = w.shape[2]
    tn = ns
    nj = ns // tn
    nt = N_CHIPS * nj
    TM = TM_PROJ

    def body(dz_ref, w_ref, x_ref, g_ref, dres_ref, dx_ref, dg_ref, acc, dgacc):
        i, j = pl.program_id(0), pl.program_id(1)

        @pl.when(j == 0)
        def _():
            acc[...] = jnp.zeros_like(acc)

        @pl.when(jnp.logical_and(i == 0, j == 0))
        def _():
            dgacc[...] = jnp.zeros_like(dgacc)

        acc[...] += _dot_nt(dz_ref[...], w_ref[...])

        @pl.when(j == nt - 1)
        def _():
            dx, dg = _rms_bwd(x_ref[...], g_ref[...], acc[...])
            dx_ref[...] = dres_ref[...] + dx
            dgacc[...] += dg

        @pl.when(jnp.logical_and(i == pl.num_programs(0) - 1, j == nt - 1))
        def _():
            dg_ref[...] = jnp.sum(dgacc[...], axis=0, keepdims=True)

    tok = pl.BlockSpec((TM, D_MODEL), lambda i, j: (i, 0))
    vec = pl.BlockSpec((1, D_MODEL), lambda i, j: (0, 0))
    return _call(
        body, name="in_proj_bwd", grid=(L // TM, nt),
        in_specs=[pl.BlockSpec((TM, tn), lambda i, j: (i, j)),
                  pl.BlockSpec((None, D_MODEL, tn), lambda i, j: (j // nj, 0, j % nj)), tok, vec, tok],
        out_specs=[tok, vec],
        out_shape=[_sds((L, D_MODEL), F32), _sds((1, D_MODEL), F32)],
        scratch=[pltpu.VMEM((TM, D_MODEL), F32), pltpu.VMEM((SUBLANES, D_MODEL), F32)],
        sem=("arbitrary", "arbitrary"), rider=rider)(dz, w, x, g, dres)


SSM_PARAMS = ("lambda_re", "lambda_im", "log_dt", "b_re", "b_im", "c_re", "c_im")
EARLY = ("w_ffn_gate", "w_ffn_up", "w_ffn_down")
MID = ("w_attn_proj", "w_glu_a", "w_glu_b", "w_out")
MID_RIDE = ((0,), (1, 2), (3,))
LATE = ("w_in",)


def _layer_bwd(dx2, sv, p, pending, owned, l, idx):
    chip_idx, c_idx = idx
    g = {}
    owned = dict(owned)

    def settle(name, partial, arrived, layer):
        owned[name] = _sum_owner(partial, arrived, owned[name], layer, chip_idx, c_idx)

    dgate, dup, dx2b = _ffn_bwd_act(dx2, sv["gate"], sv["up"], p["w_ffn_down"])
    dx1, g["g_ffn"] = _ffn_bwd_in(dx2, sv["x1"], p["g_ffn"], dgate, dup, p["w_ffn_gate"], p["w_ffn_up"])
    g["w_ffn_gate"] = _wgrad_ff_rows(dgate, sv["h2"], "wgrad_ffn_gate")
    g["w_ffn_up"] = _wgrad_ff_rows(dup, sv["h2"], "wgrad_ffn_up")
    g["w_ffn_down"] = _wgrad_ff_rows(sv["act"], dx2b, "wgrad_ffn_down")

    outs = _mix_bwd(dx1, sv["z"], sv["aout"], sv["sa"], sv["sb"], sv["ypre"], p["w_attn_proj"], p["w_glu_a"],
                    p["w_glu_b"], p["w_out"], _swap_rider([g[n] for n in EARLY]))
    dgates, da, gy, daout, dsa, dsb = outs[:6]
    early = [_add_half(g[n], s, c_idx) for n, s in zip(EARLY, outs[6:])]
    g["w_out"] = _wgrad_full(sv["mix"], dx1, "wgrad_out").reshape(N_CHIPS, D_MODEL // N_CHIPS, D_MODEL)
    g["w_attn_proj"] = _wgrad_cols(sv["a"], daout, "wgrad_attn_proj")
    g["w_glu_a"] = _wgrad_cols(sv["yact"], dsa, "wgrad_glu_a")
    g["w_glu_b"] = _wgrad_cols(sv["yact"], dsb, "wgrad_glu_b")

    outs = _ssm_bwd(sv["z"], gy, sv["pk"], p["d_skip"], sv["sd_re"], sv["sd_im"],
                    _join_riders(_scatter_rider([pending[LATE[0]]]) if pending else None,
                                 _swap_rider([g[n] for n in MID])))
    du, da_re, da_im, dbw_re, dbw_im, dcw_re, dcw_im, dd = outs[:8]
    if pending:
        settle(LATE[0], pending[LATE[0]], outs[8], l + 1)
    mid = [_add_half(g[n], s, c_idx) for n, s in zip(MID, outs[-len(MID):])]
    g["d_skip"] = jnp.sum(dd, axis=0, keepdims=True)
    g["ssm_pack_ct"] = dict(a_re=da_re, a_im=da_im, bw_re=dbw_re, bw_im=dbw_im, cw_re=dcw_re, cw_im=dcw_im)

    dos, cs = _combine_bwd(da, sv["os"], sv["ls"])
    dqkv = []
    for gi in range(N_GROUPS):
        grads, arrived = _attn_bwd(*sv["qkv"][gi], dos[gi], sv["ls"][gi], cs[gi], gi,
                                   _scatter_rider([early[gi]] + [mid[k] for k in MID_RIDE[gi]]))
        settle(EARLY[gi], early[gi], arrived[0], l)
        for k, t in zip(MID_RIDE[gi], arrived[1:]):
            settle(MID[k], mid[k], t, l)
        dqkv.append(grads)
    dz, gq8, gk8 = _qkv_post(sv["z"], dqkv, du, dgates, jnp.tile(p["g_q"], (1, N_HEADS)),
                             jnp.tile(p["g_k"], (1, N_HEADS)))
    g["g_q"] = jnp.sum(gq8.reshape(SUBLANES * N_HEADS, HEAD_DIM), axis=0, keepdims=True)
    g["g_k"] = jnp.sum(gk8.reshape(SUBLANES * N_HEADS, HEAD_DIM), axis=0, keepdims=True)
    g["w_in"] = _wgrad_cols(sv["h"], dz, "wgrad_in")
    outs = _in_proj_bwd(dz, p["w_in"], sv["x"], p["g_mix"], dx1, _swap_rider([g[n] for n in LATE]))
    dx, g["g_mix"] = outs[:2]
    late = {n: _add_half(g[n], s, c_idx) for n, s in zip(LATE, outs[2:])}
    return dx, g, late, owned


def _place():
    x, y, c = lax.axis_index("x"), lax.axis_index("y"), lax.axis_index("c")
    others = [(1 - x, y), (x, 1 - y), (1 - x, 1 - y)]
    return x, y, c, others


def _half(ref, hc):
    rows = ref.shape[-2] // 2
    idx = (slice(None),) * (len(ref.shape) - 2) + (pl.ds(hc * rows, rows), slice(None))
    return ref.at[idx]


def _comm_call(body, name, ins, out_shapes, n_remote, aliases=None):
    scratch = [pltpu.SemaphoreType.DMA((n_remote,)), pltpu.SemaphoreType.DMA((n_remote,))]
    return pl.pallas_call(
        body, name=name, in_specs=[ANY] * len(ins), out_specs=[ANY] * len(out_shapes), out_shape=out_shapes,
        scratch_shapes=scratch, input_output_aliases=aliases or {})(*ins)


def _cast_place(w, l, chip_idx):
    _, R, C = w.shape
    tr = R // 2

    def body(me_ref, w_ref, o_ref):
        o_ref[...] = w_ref[...].astype(BF16)

    return _call(body, name=f"cast_place_l{l}", grid=(R // tr,), prefetch=1,
                 in_specs=[pl.BlockSpec((None, tr, C), lambda i, me_ref: (l, i, 0))],
                 out_specs=pl.BlockSpec((None, tr, C), lambda i, me_ref: (me_ref[0], i, 0)),
                 out_shape=_sds((N_CHIPS, R, C), BF16), sem=("arbitrary",))(chip_idx, w)


def _in_place_rider(bufs, pairs, per_buf=3):
    n = len(bufs)

    def copies(outs, send, recv, side):
        return [pltpu.make_async_remote_copy(src_ref=pair[side][0], dst_ref=pair[side][0], send_sem=send.at[k],
                                             recv_sem=recv.at[k], device_id=pair[side][1], device_id_type=MESH)
                for k, pair in enumerate(pairs(outs))]

    def start(ins, outs, send, recv):
        for cp in copies(outs, send, recv, 0):
            cp.start()

    def wait(ins, outs, send, recv):
        for cp in copies(outs, send, recv, 1):
            cp.wait_recv()
        for cp in copies(outs, send, recv, 0):
            cp.wait_send()

    return Rider(list(bufs), [_sds(b.shape, b.dtype) for b in bufs], per_buf * n, start, wait,
                 {a: a for a in range(n)})


def _gather_ici_rider(bufs):
    def pairs(outs):
        x, y, c, others = _place()
        return [((_half(o.at[2 * x + y], c), (cx, cy, c)), (_half(o.at[2 * cx + cy], c), (cx, cy, c)))
                for o in outs for cx, cy in others]
    return _in_place_rider(bufs, pairs)


def _gather_d2d_rider(bufs):
    def pairs(outs):
        x, y, c, others = _place()
        sib = (x, y, 1 - c)
        return [((_half(o.at[2 * cx + cy], c), sib), (_half(o.at[2 * cx + cy], 1 - c), sib))
                for o in outs for cx, cy in others]
    return _in_place_rider(bufs, pairs)


def _swap_rider(gs):
    n = len(gs)

    def copies(ins, outs, send, recv):
        x, y, c, _ = _place()
        return [pltpu.make_async_remote_copy(src_ref=_half(ins[a], 1 - c), dst_ref=outs[a], send_sem=send.at[a],
                                             recv_sem=recv.at[a], device_id=(x, y, 1 - c), device_id_type=MESH)
                for a in range(n)]

    def start(ins, outs, send, recv):
        for cp in copies(ins, outs, send, recv):
            cp.start()

    def wait(ins, outs, send, recv):
        for cp in copies(ins, outs, send, recv):
            cp.wait()

    outs = [_sds((g.shape[0], g.shape[1] // 2, g.shape[2]), g.dtype) for g in gs]
    return Rider(list(gs), outs, n, start, wait, {})


def _scatter_rider(ss):
    n = len(ss)

    def copies(ins, outs, send, recv):
        x, y, c, others = _place()
        return [pltpu.make_async_remote_copy(
            src_ref=ins[a].at[2 * cx + cy], dst_ref=outs[a].at[j], send_sem=send.at[3 * a + j],
            recv_sem=recv.at[3 * a + j], device_id=(cx, cy, c), device_id_type=MESH)
            for a in range(n) for j, (cx, cy) in enumerate(others)]

    def start(ins, outs, send, recv):
        for cp in copies(ins, outs, send, recv):
            cp.start()

    def wait(ins, outs, send, recv):
        for cp in copies(ins, outs, send, recv):
            cp.wait()

    outs = [_sds((N_CHIPS - 1,) + s.shape[1:], s.dtype) for s in ss]
    return Rider(list(ss), outs, 3 * n, start, wait, {})


def _run_rider(rider, name):
    n_in = len(rider.ins)

    def body(*refs):
        ins, outs = refs[:n_in], refs[n_in:n_in + len(rider.out_shapes)]
        send, recv = refs[n_in + len(rider.out_shapes):]
        rider.start(ins, outs, send, recv)
        rider.wait(ins, outs, send, recv)

    return _comm_call(body, name, rider.ins, rider.out_shapes, rider.n_sem, aliases=rider.aliases)


def _join_rider(bufs):
    def pairs(outs):
        x, y, c, _ = _place()
        sib = (x, y, 1 - c)
        return [((_half(o, c), sib), (_half(o, 1 - c), sib)) for o in outs]
    return _in_place_rider(bufs, pairs, per_buf=1)


def _place_small(v, dev_idx):
    rows, n = v.shape

    def body(idx_ref, v_ref, o_ref):
        o_ref[...] = v_ref[...]

    return _call(body, name="place_small", grid=(1,), prefetch=1,
                 in_specs=[pl.BlockSpec((rows, n), lambda i, idx_ref: (0, 0))],
                 out_specs=pl.BlockSpec((None, rows, n), lambda i, idx_ref: (idx_ref[0], 0, 0)),
                 out_shape=_sds((8, rows, n), v.dtype), sem=("arbitrary",))(dev_idx, v)


def _small_ici_rider(buf):
    def pairs(outs):
        x, y, c, others = _place()
        peers = [(x, y, 1 - c)] + [(cx, cy, c) for cx, cy in others]
        return [((outs[0].at[4 * x + 2 * y + c], peer), (outs[0].at[4 * peer[0] + 2 * peer[1] + peer[2]], peer))
                for peer in peers]
    return _in_place_rider([buf], pairs, per_buf=4)


def _small_d2d_rider(buf):
    def pairs(outs):
        x, y, c, others = _place()
        sib = (x, y, 1 - c)
        return [((outs[0].at[4 * cx + 2 * cy + c], sib), (outs[0].at[4 * cx + 2 * cy + 1 - c], sib))
                for cx, cy in others]
    return _in_place_rider([buf], pairs)


def _add_half(g, p, c):
    _, R, C = g.shape
    half = R // 2

    def body(c_ref, g_ref, p_ref, o_ref):
        o_ref[...] = g_ref[...] + p_ref[...]

    blk = (None, half, C)
    return _call(body, name="add_half", grid=(N_CHIPS,), prefetch=1,
                 in_specs=[pl.BlockSpec(blk, lambda s, c_ref: (s, c_ref[0], 0)),
                           pl.BlockSpec(blk, lambda s, c_ref: (s, 0, 0))],
                 out_specs=pl.BlockSpec(blk, lambda s, c_ref: (s, 0, 0)),
                 out_shape=_sds((N_CHIPS, half, C), F32), sem=("arbitrary",))(c, g, p)


def _sum_owner(s, q, buf, l, me, c):
    _, half, C = s.shape
    tr = half // 2

    def body(me_ref, c_ref, s_ref, q0, q1, q2, buf_ref, o_ref):
        o_ref[...] = ((s_ref[...] + q0[...]) + q1[...]) + q2[...]

    blk = (None, tr, C)
    qspec = lambda j: pl.BlockSpec(blk, lambda i, me_ref, c_ref: (j, i, 0))
    return _call(body, name=f"sum_owner_l{l}", grid=(half // tr,), prefetch=2,
                 in_specs=[pl.BlockSpec(blk, lambda i, me_ref, c_ref: (me_ref[0], i, 0)),
                           qspec(0), qspec(1), qspec(2), ANY],
                 out_specs=pl.BlockSpec(blk, lambda i, me_ref, c_ref: (l, 2 * c_ref[0] + i, 0)),
                 out_shape=_sds(buf.shape, F32), sem=("arbitrary",), aliases={6: 0})(me, c, s, q, q, q, buf)


def _adamw_math(w, g, m, v):
    m = ADAM_B1 * m + (1.0 - ADAM_B1) * g
    v = ADAM_B2 * v + (1.0 - ADAM_B2) * (g * g)
    m_hat = m / (1.0 - ADAM_B1 ** ADAM_STEP)
    v_hat = v / (1.0 - ADAM_B2 ** ADAM_STEP)
    delta = -ADAM_LR * (m_hat / (jnp.sqrt(v_hat) + ADAM_EPS) + ADAM_WD * w)
    return delta, m, v


def _adamw(w, g, m, v, rider=None):
    depth, R, C = w.shape
    tr = max(t for t in range(SUBLANES, R + 1, SUBLANES) if R % t == 0 and t * C * 4 <= 2 * 1024 * 1024)

    def body(w_ref, g_ref, m_ref, v_ref, d_ref, nm_ref, nv_ref):
        d, nm, nv = _adamw_math(w_ref[...], g_ref[...], m_ref[...], v_ref[...])
        d_ref[...] = d
        nm_ref[...] = nm
        nv_ref[...] = nv

    spec = pl.BlockSpec((None, tr, C), lambda l, i: (l, i, 0))
    return _call(body, name="adamw", grid=(depth, R // tr), in_specs=[spec] * 4, out_specs=[spec] * 3,
                 out_shape=[_sds(w.shape, F32)] * 3, sem=("parallel", "parallel"), rider=rider)(w, g, m, v)


def _small_update(gathered, w, m, v):
    _, rows, n = gathered.shape
    tr = rows // 7

    def body(ga_ref, w_ref, m_ref, v_ref, g_ref, d_ref, nm_ref, nv_ref):
        g = ga_ref[0]
        for k in range(1, 8):
            g = g + ga_ref[k]
        d, nm, nv = _adamw_math(w_ref[...], g, m_ref[...], v_ref[...])
        g_ref[...] = g
        d_ref[...] = d
        nm_ref[...] = nm
        nv_ref[...] = nv

    spec = pl.BlockSpec((tr, n), lambda i: (i, 0))
    return _call(body, name="small_update", grid=(rows // tr,),
                 in_specs=[pl.BlockSpec((8, tr, n), lambda i: (0, i, 0)), spec, spec, spec], out_specs=[spec] * 4,
                 out_shape=[_sds((rows, n), F32)] * 4, sem=("parallel",))(gathered, w, m, v)


WEIGHTS = ("g_mix", "w_in", "g_q", "g_k", "w_attn_proj", "lambda_re", "lambda_im", "log_dt", "b_re", "b_im",
           "c_re", "c_im", "d_skip", "w_glu_a", "w_glu_b", "w_out", "g_ffn", "w_ffn_gate", "w_ffn_up", "w_ffn_down")
BIG = ("w_in", "w_attn_proj", "w_glu_a", "w_glu_b", "w_out", "w_ffn_gate", "w_ffn_up", "w_ffn_down")
FLIPPED = ("w_ffn_gate", "w_ffn_up")
SMALL = tuple(n for n in WEIGHTS if n not in BIG)
ROW_VECTORS = ("g_mix", "g_q", "g_k", "d_skip", "g_ffn")
PACK_QUANTUM = LANES * SUBLANES * 7


def _pack_small(parts, extra):
    flat = jnp.concatenate([parts[n].reshape(-1).astype(F32) for n in SMALL] + [extra.reshape(-1)])
    pad = -flat.shape[0] % PACK_QUANTUM
    return jnp.pad(flat, (0, pad)).reshape(-1, LANES)


def _unpack_small(packed, like):
    flat = packed.reshape(-1)
    out, at = {}, 0
    for n in SMALL:
        size = math.prod(like[n].shape)
        out[n] = flat[at:at + size].reshape(like[n].shape)
        at += size
    return out, flat[at]


def kernel(x, g_mix, w_in, g_q, g_k, w_attn_proj, lambda_re, lambda_im, log_dt, b_re, b_im, c_re, c_im, d_skip, w_glu_a, w_glu_b, w_out, g_ffn, w_ffn_gate, w_ffn_up, w_ffn_down, loss_target, m_g_mix, m_w_in, m_g_q, m_g_k, m_w_attn_proj, m_lambda_re, m_lambda_im, m_log_dt, m_b_re, m_b_im, m_c_re, m_c_im, m_d_skip, m_w_glu_a, m_w_glu_b, m_w_out, m_g_ffn, m_w_ffn_gate, m_w_ffn_up, m_w_ffn_down, v_g_mix, v_w_in, v_g_q, v_g_k, v_w_attn_proj, v_lambda_re, v_lambda_im, v_log_dt, v_b_re, v_b_im, v_c_re, v_c_im, v_d_skip, v_w_glu_a, v_w_glu_b, v_w_out, v_g_ffn, v_w_ffn_gate, v_w_ffn_up, v_w_ffn_down):
    given = dict(locals())
    flip = lambda n, a: jnp.swapaxes(a, 1, 2) if n in FLIPPED else a
    W = {n: flip(n, given[n]) for n in WEIGHTS}
    M = {n: flip(n, given["m_" + n]) for n in WEIGHTS}
    V = {n: flip(n, given["v_" + n]) for n in WEIGHTS}
    depth = g_mix.shape[0]
    xl = x.reshape(x.shape[-2:])
    target = loss_target.reshape(loss_target.shape[-2:])
    c_idx = lax.axis_index("c").astype(jnp.int32).reshape(1)
    chip_idx = (2 * lax.axis_index("x") + lax.axis_index("y")).astype(jnp.int32).reshape(1)

    place = lambda l: [_cast_place(W[n], l, chip_idx) for n in BIG]
    bufs = place(0)
    w_in = _run_rider(_gather_d2d_rider(_run_rider(_gather_ici_rider(bufs[:1]), "gather_ici")), "gather_d2d")[0]
    rest, stage = bufs[1:], "ici"
    params, saved, h = [], [], xl
    for l in range(depth):
        p = {"w_in": w_in}
        for n in SMALL:
            p[n] = W[n][l][None] if n in ROW_VECTORS else W[n][l]
        h, sv, p, nxt = _layer_fwd(h, p, rest, stage, place(l + 1) if l + 1 < depth else None)
        params.append(p)
        saved.append(sv)
        if nxt:
            (w_in, rest), stage = nxt, "d2d"
    dx, loss_part = _loss_head(h, target)

    owned = {n: lax.empty(W[n].shape, F32) for n in BIG}
    small_grads = [None] * depth
    pending = None
    for l in reversed(range(depth)):
        dx, small_grads[l], pending, owned = _layer_bwd(dx, saved[l], params[l], pending, owned, l,
                                                        (chip_idx, c_idx))

    ct = {k: jnp.stack([small_grads[l]["ssm_pack_ct"][k] for l in range(depth)])
          for k in small_grads[0]["ssm_pack_ct"]}
    ct["a_re"], ct["a_im"] = jnp.sum(ct["a_re"], axis=2), jnp.sum(ct["a_im"], axis=2)
    ct["a64_re"] = ct["a64_im"] = jnp.zeros_like(ct["a_re"])
    _, pull = jax.vjp(jax.vmap(_ssm_pack), *[W[n] for n in SSM_PARAMS])
    stacked = dict(zip(SSM_PARAMS, pull(ct)))
    for n in SMALL:
        if n not in stacked:
            stacked[n] = jnp.stack([small_grads[l][n] for l in range(depth)])
    zero = jnp.zeros((1,), F32)
    dev_idx = (4 * lax.axis_index("x") + 2 * lax.axis_index("y") + lax.axis_index("c")).astype(jnp.int32).reshape(1)
    gathered = _place_small(_pack_small(stacked, loss_part), dev_idx)
    outs = _run_rider(_join_riders(_scatter_rider([pending[n] for n in LATE]), _small_ici_rider(gathered)),
                      "scatter_to_owners")
    for n, q in zip(LATE, outs[:len(LATE)]):
        owned[n] = _sum_owner(pending[n], q, owned[n], 0, chip_idx, c_idx)
    outs = _run_rider(_join_riders(_join_rider([owned[n] for n in BIG]), _small_d2d_rider(outs[len(LATE)])),
                      "join_halves")
    reduced, gathered = dict(zip(BIG, outs[:len(BIG)])), outs[len(BIG)]
    grads, delta, new_m, new_v = {}, {}, {}, {}
    for n in BIG:
        outs = (reduced[n], *_adamw(W[n], reduced[n], M[n], V[n]))
        grads[n], delta[n], new_m[n], new_v[n] = [flip(n, t) for t in outs]
    gs, ds, nms, nvs = _small_update(gathered, _pack_small(W, zero), _pack_small(M, zero), _pack_small(V, zero))
    sg, loss = _unpack_small(gs, W)
    sd, _ = _unpack_small(ds, W)
    sm, _ = _unpack_small(nms, W)
    sv_, _ = _unpack_small(nvs, W)
    for n in SMALL:
        grads[n], delta[n], new_m[n], new_v[n] = sg[n], sd[n], sm[n], sv_[n]

    return (loss, dx.reshape(x.shape), *[grads[n] for n in WEIGHTS], *[delta[n] for n in WEIGHTS],
            *[new_m[n] for n in WEIGHTS], *[new_v[n] for n in WEIGHTS])
```

```python
import collections
import functools
import math

import jax
import jax.numpy as jnp
from jax import lax
from jax.experimental import pallas as pl
from jax.experimental.pallas import tpu as pltpu

F32 = jnp.float32
BF16 = jnp.bfloat16

D_MODEL = 1024
DEPTH = 4
HEAD_DIM = 64
N_HEADS = 8
ATTN_WIDTH = N_HEADS * HEAD_DIM
ATTN_PATTERN = ((128, 1), (512, 4), (2048, 16))
N_GROUPS = len(ATTN_PATTERN)
BLK = 128
SSM_WIDTH = 512
SSM_GROUP = 16
SSM_GROUPS = 32
SSM_STATE = 64
D_FF = 2816
IN_COLS = 7168
EPS = 1e-6
ADAM_LR, ADAM_B1, ADAM_B2, ADAM_EPS, ADAM_WD, ADAM_STEP = 0.001, 0.9, 0.999, 1e-08, 0.01, 10

N_CHIPS = 4
MESH = pl.DeviceIdType.MESH

LANES = 128
SUBLANES = 8
VMEM_LIMIT = 56 * 1024 * 1024

TM = 512
TM_PROJ = 1024
TL_WGRAD = 2048
TM_MIX = 512

SSM_TB = 512
SSM_TC = 64
SSM_SUB = SUBLANES
SSM_PITCH = 68
N_SLAB = SSM_GROUPS * SSM_STATE // LANES
SSM_WIN = 256
N_PAIR = N_SLAB // 2
PAIRS_PER_WIN = 4
SCAN_GROUP = 4


def _params(sem=None):
    return pltpu.CompilerParams(dimension_semantics=sem, vmem_limit_bytes=VMEM_LIMIT)


ANY = pl.BlockSpec(memory_space=pl.ANY)

Rider = collections.namedtuple("Rider", "ins out_shapes n_sem start wait aliases")


class _SemWindow:
    def __init__(self, ref, offset):
        self.ref, self.offset = ref, offset

    @property
    def at(self):
        return self

    def __getitem__(self, k):
        return self.ref.at[self.offset + k]


def _join_riders(*riders):
    riders = [r for r in riders if r is not None]
    if len(riders) <= 1:
        return riders[0] if riders else None

    def each(fn_name):
        def run(ins, outs, send, recv):
            i = o = s = 0
            for r in riders:
                getattr(r, fn_name)(ins[i:i + len(r.ins)], outs[o:o + len(r.out_shapes)],
                                    _SemWindow(send, s), _SemWindow(recv, s))
                i, o, s = i + len(r.ins), o + len(r.out_shapes), s + r.n_sem
        return run

    aliases, i, o = {}, 0, 0
    for r in riders:
        aliases.update({i + a: o + b for a, b in r.aliases.items()})
        i, o = i + len(r.ins), o + len(r.out_shapes)
    return Rider([t for r in riders for t in r.ins], [t for r in riders for t in r.out_shapes],
                 sum(r.n_sem for r in riders), each("start"), each("wait"), aliases)


def _with_rider(body, rider, grid, prefetch, n_in, n_out, n_scratch):
    n_rin, n_rout = len(rider.ins), len(rider.out_shapes)

    def hosted(*refs):
        pre, rest = refs[:prefetch], refs[prefetch:]
        ins, rin = rest[:n_in], rest[n_in:n_in + n_rin]
        o0 = n_in + n_rin
        outs, rout = rest[o0:o0 + n_out], rest[o0 + n_out:o0 + n_out + n_rout]
        s0 = o0 + n_out + n_rout
        scr, (send, recv) = rest[s0:s0 + n_scratch], rest[s0 + n_scratch:]
        first = functools.reduce(jnp.logical_and, [pl.program_id(k) == 0 for k in range(len(grid))])
        last = functools.reduce(jnp.logical_and, [pl.program_id(k) == grid[k] - 1 for k in range(len(grid))])

        @pl.when(first)
        def _():
            rider.start(rin, rout, send, recv)

        body(*pre, *ins, *outs, *scr)

        @pl.when(last)
        def _():
            rider.wait(rin, rout, send, recv)

    return hosted


def _call(body, *, name, grid, in_specs, out_specs, out_shape, scratch=(), sem=None, aliases=None,
          prefetch=0, rider=None):
    if rider is not None:
        single = not isinstance(out_specs, (list, tuple))
        out_specs = [out_specs] if single else list(out_specs)
        out_shape = [out_shape] if single else list(out_shape)
        body = _with_rider(body, rider, grid, prefetch, len(in_specs), len(out_specs), len(scratch))
        aliases = dict(aliases or {})
        aliases.update({prefetch + len(in_specs) + k: len(out_specs) + v for k, v in rider.aliases.items()})
        in_specs = list(in_specs) + [ANY] * len(rider.ins)
        out_specs = out_specs + [ANY] * len(rider.out_shapes)
        out_shape = out_shape + list(rider.out_shapes)
        scratch = list(scratch) + [pltpu.SemaphoreType.DMA((rider.n_sem,)), pltpu.SemaphoreType.DMA((rider.n_sem,))]
        sem = ("arbitrary",) * len(grid)
        fn = _call(body, name=name + "_host", grid=grid, in_specs=in_specs, out_specs=out_specs, out_shape=out_shape,
                   scratch=scratch, sem=sem, aliases=aliases, prefetch=prefetch)
        return lambda *args: fn(*args, *rider.ins)
    kw = {}
    if aliases:
        kw["input_output_aliases"] = aliases
    if prefetch:
        gs = pltpu.PrefetchScalarGridSpec(num_scalar_prefetch=prefetch, grid=grid, in_specs=in_specs,
                                          out_specs=out_specs, scratch_shapes=list(scratch))
        return pl.pallas_call(body, name=name, grid_spec=gs, out_shape=out_shape,
                              compiler_params=_params(sem), **kw)
    return pl.pallas_call(body, name=name, grid=grid, in_specs=in_specs, out_specs=out_specs,
                          out_shape=out_shape, scratch_shapes=list(scratch),
                          compiler_params=_params(sem), **kw)


def _sds(shape, dtype):
    return jax.ShapeDtypeStruct(shape, dtype)


def _sigmoid(v):
    return 0.5 * jnp.tanh(0.5 * v) + 0.5


def _dot(a, b):
    return jnp.dot(a, b, preferred_element_type=F32)


def _dot_nt(a, b):
    return lax.dot_general(a, b, (((1,), (1,)), ((), ())), preferred_element_type=F32)


def _dot_tn(a, b):
    return lax.dot_general(a, b, (((0,), (0,)), ((), ())), preferred_element_type=F32)


def _in_proj_fwd(x, g, w, rider=None):
    L = x.shape[0]
    ns = w.shape[2]
    tn = ns
    nj = ns // tn
    TM = TM_PROJ

    def body(x_ref, g_ref, w_ref, z_ref, h_ref):
        @pl.when(pl.program_id(1) == 0)
        def _():
            xv = x_ref[...]
            r = lax.rsqrt(jnp.mean(xv * xv, axis=-1, keepdims=True) + EPS)
            h_ref[...] = (xv * r * g_ref[...]).astype(BF16)
        z_ref[...] = _dot(h_ref[...], w_ref[...]).astype(BF16)

    return _call(
        body, name="in_proj_fwd", grid=(L // TM, N_CHIPS * nj),
        in_specs=[pl.BlockSpec((TM, D_MODEL), lambda i, j: (i, 0)),
                  pl.BlockSpec((1, D_MODEL), lambda i, j: (0, 0)),
                  pl.BlockSpec((None, D_MODEL, tn), lambda i, j: (j // nj, 0, j % nj))],
        out_specs=[pl.BlockSpec((TM, tn), lambda i, j: (i, j)),
                   pl.BlockSpec((TM, D_MODEL), lambda i, j: (i, 0))],
        out_shape=[_sds((L, N_CHIPS * ns), BF16), _sds((L, D_MODEL), BF16)],
        sem=("parallel", "arbitrary"), rider=rider)(x, g, w)


DL_TILE = 512
SCALE = HEAD_DIM ** -0.5


def _perm_matrix(d):
    rho = jnp.arange(DL_TILE)
    src = rho // (DL_TILE // d) + d * (rho % (DL_TILE // d))
    return (src[:, None] == jnp.arange(DL_TILE)[None, :]).astype(BF16)


def _head_sum_matrix():
    h = jnp.arange(ATTN_WIDTH) // HEAD_DIM
    return (h[:, None] == h[None, :]).astype(BF16)


def _split(v):
    hi = v.astype(BF16)
    return hi, (v - hi.astype(F32)).astype(BF16)


def _head_sum(v, hs):
    vb = v.astype(BF16)
    half = ATTN_WIDTH // 2
    blk = hs[:half, :half]
    return jnp.concatenate([_dot(vb[:, :half], blk), _dot(vb[:, half:], blk)], axis=1)


def _permute(pm, v):
    hi, lo = _split(v)
    return _dot(pm, hi) + _dot(pm, lo)


def _dl_view(t, d):
    if d * BLK <= DL_TILE:
        return t
    return t.reshape(t.shape[0] // DL_TILE, d, DL_TILE // d, t.shape[1])


def _dl_spec(d, width, which):
    if d * BLK <= DL_TILE:
        per_tile = DL_TILE // (d * BLK)
        return pl.BlockSpec((BLK, width), lambda r, n: ((which(n) // per_tile) * (DL_TILE // BLK)
                                                       + r * per_tile + which(n) % per_tile, 0))
    tiles = d * BLK // DL_TILE
    return pl.BlockSpec((tiles, None, DL_TILE // d, width), lambda r, n: (which(n), r, 0, 0))


def _dl_read(ref):
    v = ref[...]
    return v if v.ndim == 2 else v.reshape(BLK, v.shape[-1])


def _dl_write(ref, v):
    ref[...] = v if len(ref.shape) == 2 else v.reshape(ref.shape)


def _qkv_prep(z, gq_t, gk_t, rider=None):
    L = z.shape[0]
    qkv_w = N_GROUPS * ATTN_WIDTH

    def body(zq_ref, zk_ref, zv_ref, gq_ref, gk_ref, hs_ref, p1_ref, p2_ref, *outs):
        hs = hs_ref[...]
        perms = (None, p1_ref[...], p2_ref[...])
        for g in range(N_GROUPS):
            cols = slice(g * ATTN_WIDTH, (g + 1) * ATTN_WIDTH)
            xq = zq_ref[:, cols].astype(F32)
            xk = zk_ref[:, cols].astype(F32)
            rq = lax.rsqrt(_head_sum(xq * xq, hs) * (1.0 / HEAD_DIM) + EPS)
            rk = lax.rsqrt(_head_sum(xk * xk, hs) * (1.0 / HEAD_DIM) + EPS)
            vals = [(xq * rq * (gq_ref[...] * SCALE)).astype(BF16), (xk * rk * gk_ref[...]).astype(BF16),
                    zv_ref[:, cols]]
            for j, t in enumerate(vals):
                if perms[g] is not None:
                    t = _dot(perms[g], t).astype(BF16)
                outs[3 * g + j][...] = t

    tile = pl.BlockSpec((DL_TILE, ATTN_WIDTH), lambda i: (i, 0))
    mat = pl.BlockSpec((DL_TILE, DL_TILE), lambda i: (0, 0))
    vec = pl.BlockSpec((1, ATTN_WIDTH), lambda i: (0, 0))
    outs = _call(
        body, name="qkv_prep", grid=(L // DL_TILE,),
        in_specs=[pl.BlockSpec((DL_TILE, qkv_w), lambda i: (i, 0)), pl.BlockSpec((DL_TILE, qkv_w), lambda i: (i, 1)),
                  pl.BlockSpec((DL_TILE, qkv_w), lambda i: (i, 2)), vec, vec, mat, mat, mat],
        out_specs=[tile] * 9, out_shape=[_sds((L, ATTN_WIDTH), BF16)] * 9,
        sem=("parallel",), rider=rider)(z, z, z, gq_t, gk_t, _head_sum_matrix(), _perm_matrix(ATTN_PATTERN[1][1]),
                                        _perm_matrix(ATTN_PATTERN[2][1]))
    return [tuple(outs[3 * g:3 * g + 3]) for g in range(N_GROUPS)], list(outs[3 * N_GROUPS:])


def _pair_masks():
    lane = lax.broadcasted_iota(jnp.int32, (1, LANES), 1)
    return lane < HEAD_DIM, lane >= HEAD_DIM


def _attn_fwd(qs, ks, v, gi):
    L = qs.shape[0]
    _, d = ATTN_PATTERN[gi]
    nb = L // (d * BLK)

    def body(q_ref, kc_ref, kp_ref, vc_ref, vp_ref, o_ref, l_ref):
        n = pl.program_id(1)
        qi = lax.broadcasted_iota(jnp.int32, (BLK, 2 * BLK), 0)
        kj = lax.broadcasted_iota(jnp.int32, (BLK, 2 * BLK), 1)
        prev = kj < BLK
        mask = jnp.logical_and(jnp.where(prev, kj, qi) >= jnp.where(prev, qi, kj - BLK),
                               kj >= jnp.where(n > 0, 0, BLK))
        q = _dl_read(q_ref)
        kw = jnp.concatenate([_dl_read(kp_ref), _dl_read(kc_ref)], axis=0)
        vw = jnp.concatenate([_dl_read(vp_ref), _dl_read(vc_ref)], axis=0)
        one = jnp.ones((2 * BLK, LANES), BF16)
        o_parts, l_parts = [], []
        for hp in range(N_HEADS // 2):
            ls = slice(hp * LANES, (hp + 1) * LANES)
            qp, kp_, vp_ = q[:, ls], kw[:, ls], vw[:, ls]
            num = jnp.zeros((BLK, LANES), F32)
            den = jnp.zeros((BLK, LANES), F32)
            mb = jnp.zeros((BLK, LANES), F32)
            for he in _pair_masks():
                s = jnp.where(mask, _dot_nt(jnp.where(he, qp, 0), kp_), -jnp.inf)
                m = jnp.max(s, axis=-1, keepdims=True)
                p = jnp.exp(s - m).astype(BF16)
                acc = _dot(p, jnp.concatenate([jnp.where(he, vp_, 0), jnp.where(he, one, 0)], axis=1))
                num += acc[:, :LANES]
                den += acc[:, LANES:]
                mb = jnp.where(he, m, mb)
            o_parts.append((num / den).astype(BF16))
            l_parts.append(mb + jnp.log(den))
        _dl_write(o_ref, jnp.concatenate(o_parts, axis=1))
        _dl_write(l_ref, jnp.concatenate(l_parts, axis=1))

    cur = _dl_spec(d, ATTN_WIDTH, lambda n: n)
    prev = _dl_spec(d, ATTN_WIDTH, lambda n: jnp.maximum(n - 1, 0))
    view = lambda t: _dl_view(t, d)
    o, l = _call(
        body, name=f"attn_fwd_g{gi}", grid=(d, nb), in_specs=[cur, cur, prev, cur, prev], out_specs=[cur, cur],
        out_shape=[_sds(view(qs).shape, BF16), _sds(view(qs).shape, F32)],
        sem=("parallel", "parallel"))(view(qs), view(ks), view(ks), view(v), view(v))
    return o.reshape(L, ATTN_WIDTH), l.reshape(L, ATTN_WIDTH)


def _to_token_order(os_, ls_, pts):
    o_tok, l_tok = [], []
    for o, l, pt in zip(os_, ls_, pts):
        if pt is None:
            o_tok.append(o.astype(F32))
            l_tok.append(l)
        else:
            o_tok.append(_dot(pt, o))
            l_tok.append(_permute(pt, l))
    return o_tok, l_tok


def _combine_fwd(os_, ls_):
    L = os_[0].shape[0]

    def body(o0, o1, o2, l0, l1, l2, pt1_ref, pt2_ref, a_ref):
        o_tok, l_tok = _to_token_order((o0[...], o1[...], o2[...]), (l0[...], l1[...], l2[...]),
                                       (None, pt1_ref[...], pt2_ref[...]))
        w = _combine_weights(*l_tok)
        a_ref[...] = (w[0] * o_tok[0] + w[1] * o_tok[1] + w[2] * o_tok[2]).astype(BF16)

    tile = pl.BlockSpec((DL_TILE, ATTN_WIDTH), lambda i: (i, 0))
    mat = pl.BlockSpec((DL_TILE, DL_TILE), lambda i: (0, 0))
    return _call(body, name="combine_fwd", grid=(L // DL_TILE,), in_specs=[tile] * 6 + [mat, mat], out_specs=tile,
                 out_shape=_sds((L, ATTN_WIDTH), BF16), sem=("parallel",))(
                     *os_, *ls_, _perm_matrix(ATTN_PATTERN[1][1]).T, _perm_matrix(ATTN_PATTERN[2][1]).T)


def _gelu(v):
    c = math.sqrt(2.0 / math.pi)
    return 0.5 * v * (1.0 + jnp.tanh(c * (v + 0.044715 * v * v * v)))


def _gelu_grad(v):
    c = math.sqrt(2.0 / math.pi)
    t = jnp.tanh(c * (v + 0.044715 * v * v * v))
    return 0.5 * (1.0 + t) + 0.5 * v * (1.0 - t * t) * c * (1.0 + 3.0 * 0.044715 * v * v)


def _ssm_fill(u, bwre_ref, bwim_ref, sre, sim):
    for k2 in range(N_PAIR):
        uw = u[:, _win_cols(k2)]
        _to_slabs(sre, k2, _dot(uw, bwre_ref[k2]))
        _to_slabs(sim, k2, _dot(uw, bwim_ref[k2]))


def _win_cols(k2):
    w = k2 // PAIRS_PER_WIN
    return slice(w * SSM_WIN, (w + 1) * SSM_WIN)


def _to_slabs(ref, k2, v):
    for half in range(2):
        for j in range(SSM_SUB):
            ref[2 * k2 + half, j * SSM_PITCH:j * SSM_PITCH + SSM_TC, :] = (
                v[j * SSM_TC:(j + 1) * SSM_TC, half * LANES:(half + 1) * LANES])


def _rows(i):
    return pl.ds(i, SSM_SUB, stride=SSM_PITCH)


def _slab_rows(ref, k):
    return jnp.concatenate([ref[k, j * SSM_PITCH:j * SSM_PITCH + SSM_TC, :] for j in range(SSM_SUB)], axis=0)


def _pair_rows(ref, k2):
    return jnp.concatenate([_slab_rows(ref, 2 * k2), _slab_rows(ref, 2 * k2 + 1)], axis=1).astype(BF16)


def _bcast(ref, k):
    return jnp.broadcast_to(ref[pl.ds(k, 1), :], (SSM_SUB, LANES))


def _scan(sre, sim, are_ref, aim_ref, k0, init, *, reverse, store, sign=1.0):
    ar = [_bcast(are_ref, k0 + kk) for kk in range(SCAN_GROUP)]
    ai = [sign * _bcast(aim_ref, k0 + kk) for kk in range(SCAN_GROUP)]

    def step(t, carry):
        i = SSM_TC - 1 - t if reverse else t
        out = []
        for kk in range(SCAN_GROUP):
            k = k0 + kk
            xr, xi = carry[2 * kk], carry[2 * kk + 1]
            nr = ar[kk] * xr - ai[kk] * xi + sre[k, _rows(i), :]
            ni = ar[kk] * xi + ai[kk] * xr + sim[k, _rows(i), :]
            if store:
                sre[k, _rows(i), :] = nr
                sim[k, _rows(i), :] = ni
            out += [nr, ni]
        return tuple(out)

    flat = []
    for re, im in init:
        flat += [re, im]
    res = lax.fori_loop(0, SSM_TC // 2, lambda t, c: step(2 * t + 1, step(2 * t, c)), tuple(flat))
    return [(res[2 * kk], res[2 * kk + 1]) for kk in range(SCAN_GROUP)]


def _ssm_seeds(ends_re, ends_im, a64re_ref, a64im_ref, carry_re, carry_im, seed_re, seed_im, k,
               *, reverse, sign=1.0):
    ar = a64re_ref[pl.ds(k, 1), :]
    ai = sign * a64im_ref[pl.ds(k, 1), :]
    cr = carry_re[pl.ds(k, 1), :]
    ci = carry_im[pl.ds(k, 1), :]
    order = range(SSM_SUB - 1, -1, -1) if reverse else range(SSM_SUB)
    for j in order:
        seed_re[k, pl.ds(j, 1), :] = cr
        seed_im[k, pl.ds(j, 1), :] = ci
        er = ends_re[k, pl.ds(j, 1), :]
        ei = ends_im[k, pl.ds(j, 1), :]
        cr, ci = ar * cr - ai * ci + er, ar * ci + ai * cr + ei
    carry_re[pl.ds(k, 1), :] = cr
    carry_im[pl.ds(k, 1), :] = ci


def _ssm_specs_consts():
    c2 = pl.BlockSpec((N_SLAB, LANES), lambda b: (0, 0))
    c3 = pl.BlockSpec((N_PAIR, SSM_WIN, SSM_WIN), lambda b: (0, 0, 0))
    return c2, c3


def _ssm_scratch():
    rows = SSM_SUB * SSM_PITCH
    return [pltpu.VMEM((N_SLAB, rows, LANES), F32), pltpu.VMEM((N_SLAB, rows, LANES), F32)]


def _ssm_fwd(z, pk, dskip, rider=None):
    L = z.shape[0]
    nb = L // SSM_TB
    ucol = (3 * N_GROUPS * ATTN_WIDTH) // SSM_WIDTH

    def body(u_ref, are_ref, aim_ref, a64re_ref, a64im_ref, bwre_ref, bwim_ref, cwre_ref, cwim_ref, d_ref,
             ypre_ref, yact_ref, sdre_ref, sdim_ref, sre, sim, carry_re, carry_im, ends_re, ends_im,
             seed_re, seed_im):
        @pl.when(pl.program_id(0) == 0)
        def _():
            carry_re[...] = jnp.zeros_like(carry_re)
            carry_im[...] = jnp.zeros_like(carry_im)

        u = u_ref[...]
        _ssm_fill(u, bwre_ref, bwim_ref, sre, sim)
        zero = jnp.zeros((SSM_SUB, LANES), F32)
        for k0 in range(0, N_SLAB, SCAN_GROUP):
            ends = _scan(sre, sim, are_ref, aim_ref, k0, [(zero, zero)] * SCAN_GROUP, reverse=False, store=False)
            for kk in range(SCAN_GROUP):
                ends_re[k0 + kk] = ends[kk][0]
                ends_im[k0 + kk] = ends[kk][1]
            for kk in range(SCAN_GROUP):
                _ssm_seeds(ends_re, ends_im, a64re_ref, a64im_ref, carry_re, carry_im, seed_re, seed_im,
                           k0 + kk, reverse=False)
            init = [(seed_re[k0 + kk], seed_im[k0 + kk]) for kk in range(SCAN_GROUP)]
            _scan(sre, sim, are_ref, aim_ref, k0, init, reverse=False, store=True)
        sdre_ref[...] = seed_re[...]
        sdim_ref[...] = seed_im[...]
        for w in range(N_PAIR // PAIRS_PER_WIN):
            acc = jnp.zeros((SSM_TB, SSM_WIN), F32)
            for kk in range(PAIRS_PER_WIN):
                k2 = w * PAIRS_PER_WIN + kk
                acc += _dot(_pair_rows(sre, k2), cwre_ref[k2])
                acc -= _dot(_pair_rows(sim, k2), cwim_ref[k2])
            cols = _win_cols(w * PAIRS_PER_WIN)
            ypre = acc + d_ref[:, cols] * u[:, cols].astype(F32)
            ypre_ref[:, cols] = ypre
            yact_ref[:, cols] = _gelu(ypre).astype(BF16)

    c2, c3 = _ssm_specs_consts()
    seed_spec = pl.BlockSpec((None, N_SLAB, SSM_SUB, LANES), lambda b: (b, 0, 0, 0))
    small = pltpu.VMEM((N_SLAB, LANES), F32)
    tile = pltpu.VMEM((N_SLAB, SSM_SUB, LANES), F32)
    return _call(
        body, name="ssm_fwd", grid=(nb,),
        in_specs=[pl.BlockSpec((SSM_TB, SSM_WIDTH), lambda b: (b, ucol)), c2, c2, c2, c2, c3, c3, c3, c3,
                  pl.BlockSpec((1, SSM_WIDTH), lambda b: (0, 0))],
        out_specs=[pl.BlockSpec((SSM_TB, SSM_WIDTH), lambda b: (b, 0)),
                   pl.BlockSpec((SSM_TB, SSM_WIDTH), lambda b: (b, 0)), seed_spec, seed_spec],
        out_shape=[_sds((L, SSM_WIDTH), F32), _sds((L, SSM_WIDTH), BF16),
                   _sds((nb, N_SLAB, SSM_SUB, LANES), F32), _sds((nb, N_SLAB, SSM_SUB, LANES), F32)],
        scratch=_ssm_scratch() + [small, small, tile, tile, tile, tile],
        sem=("arbitrary",), rider=rider)(z, pk["a_re"], pk["a_im"], pk["a64_re"], pk["a64_im"],
                                         pk["bw_re"].astype(BF16), pk["bw_im"].astype(BF16),
                                         pk["cw_re"].astype(BF16), pk["cw_im"].astype(BF16), dskip)


def _combine_weights(l0, l1, l2):
    m = jnp.maximum(jnp.maximum(l0, l1), l2)
    e0, e1, e2 = jnp.exp(l0 - m), jnp.exp(l1 - m), jnp.exp(l2 - m)
    inv = 1.0 / (e0 + e1 + e2)
    return e0 * inv, e1 * inv, e2 * inv


def _mix_fwd(x, z, a, yact, w_ap, w_ga, w_gb, w_out):
    L = x.shape[0]
    cs = D_MODEL // N_CHIPS
    ga_col = (3 * N_GROUPS * ATTN_WIDTH + SSM_WIDTH) // D_MODEL

    def body(x_ref, ga_ref, gs_ref, a_ref, y_ref, wap_ref, wga_ref, wgb_ref, wout_ref,
             x1_ref, aout_ref, sa_ref, sb_ref, mix_ref):
        a = a_ref[...]
        y = y_ref[...]
        for s in range(N_CHIPS):
            cols = slice(s * cs, (s + 1) * cs)
            aout_ref[:, cols] = _dot(a, wap_ref[s]).astype(BF16)
            sa_ref[:, cols] = _dot(y, wga_ref[s]).astype(BF16)
            sb_ref[:, cols] = _dot(y, wgb_ref[s]).astype(BF16)
        s_out = sa_ref[...].astype(F32) * _sigmoid(sb_ref[...].astype(F32))
        mix = (_sigmoid(ga_ref[...].astype(F32)) * aout_ref[...].astype(F32)
               + _sigmoid(gs_ref[...].astype(F32)) * s_out).astype(BF16)
        mix_ref[...] = mix
        x1_ref[...] = x_ref[...] + _dot(mix, wout_ref[...])

    tok = lambda w: pl.BlockSpec((TM_MIX, w), lambda i: (i, 0))
    wsm = pl.BlockSpec((N_CHIPS, ATTN_WIDTH, cs), lambda i: (0, 0, 0))
    return _call(
        body, name="mix_fwd", grid=(L // TM_MIX,),
        in_specs=[tok(D_MODEL), pl.BlockSpec((TM_MIX, D_MODEL), lambda i: (i, ga_col)),
                  pl.BlockSpec((TM_MIX, D_MODEL), lambda i: (i, ga_col + 1))]
                 + [tok(ATTN_WIDTH)] * 2 + [wsm, wsm, wsm, pl.BlockSpec((D_MODEL, D_MODEL), lambda i: (0, 0))],
        out_specs=[tok(D_MODEL), tok(D_MODEL), tok(D_MODEL), tok(D_MODEL), tok(D_MODEL)],
        out_shape=[_sds((L, D_MODEL), F32)] + [_sds((L, D_MODEL), BF16)] * 4,
        sem=("parallel",))(x, z, z, a, yact, w_ap, w_ga, w_gb, w_out.reshape(D_MODEL, D_MODEL))


def _ffn_fwd(x1, g, w_g, w_u, w_d, rider=None):
    L = x1.shape[0]
    fs = D_FF // N_CHIPS
    TM = TM_PROJ

    def body(x_ref, g_ref, wg_ref, wu_ref, wd_ref, x2_ref, h_ref, gate_ref, up_ref, act_ref, acc):
        s = pl.program_id(1)

        @pl.when(s == 0)
        def _():
            xv = x_ref[...]
            r = lax.rsqrt(jnp.mean(xv * xv, axis=-1, keepdims=True) + EPS)
            h_ref[...] = (xv * r * g_ref[...]).astype(BF16)
            acc[...] = jnp.zeros_like(acc)

        h = h_ref[...]
        gate = _dot_nt(h, wg_ref[...])
        up = _dot_nt(h, wu_ref[...])
        sg = _sigmoid(gate)
        silu = gate * sg
        act = (silu * up).astype(BF16)
        gate_ref[...] = (up * (sg * (1.0 + gate * (1.0 - sg)))).astype(BF16)
        up_ref[...] = silu.astype(BF16)
        act_ref[...] = act
        acc[...] += _dot(act, wd_ref[...])

        @pl.when(s == N_CHIPS - 1)
        def _():
            x2_ref[...] = x_ref[...] + acc[...]

    tok = pl.BlockSpec((TM, D_MODEL), lambda i, s: (i, 0))
    ffs = pl.BlockSpec((None, TM, fs), lambda i, s: (s, i, 0))
    return _call(
        body, name="ffn_fwd", grid=(L // TM, N_CHIPS),
        in_specs=[tok, pl.BlockSpec((1, D_MODEL), lambda i, s: (0, 0))]
                 + [pl.BlockSpec((None, fs, D_MODEL), lambda i, s: (s, 0, 0))] * 3,
        out_specs=[tok, tok, ffs, ffs, ffs],
        out_shape=[_sds((L, D_MODEL), F32), _sds((L, D_MODEL), BF16)] + [_sds((N_CHIPS, L, fs), BF16)] * 3,
        scratch=[pltpu.VMEM((TM, D_MODEL), F32)],
        sem=("parallel", "arbitrary"), rider=rider)(x1, g, w_g, w_u, w_d)


def _loss_head(xl, target):
    L = xl.shape[0]

    def body(x_ref, t_ref, dx_ref, loss_ref, acc):
        i = pl.program_id(0)

        @pl.when(i == 0)
        def _():
            acc[...] = jnp.zeros_like(acc)

        e = x_ref[...] - t_ref[...]
        dx_ref[...] = e * (1.0 / D_MODEL)
        acc[...] += jnp.sum((e * e).reshape(TM // SUBLANES, SUBLANES, D_MODEL), axis=0)

        @pl.when(i == pl.num_programs(0) - 1)
        def _():
            loss_ref[...] = (0.5 / D_MODEL) * jnp.sum(acc[...]).reshape(1, 1)

    tok = pl.BlockSpec((TM, D_MODEL), lambda i: (i, 0))
    return _call(
        body, name="loss_head", grid=(L // TM,), in_specs=[tok, tok],
        out_specs=[tok, pl.BlockSpec((1, 1), lambda i: (0, 0))],
        out_shape=[_sds((L, D_MODEL), F32), _sds((1, 1), F32)],
        scratch=[pltpu.VMEM((SUBLANES, D_MODEL), F32)], sem=("arbitrary",))(xl, target)


def _ssm_pack(lam_re, lam_im, log_dt, b_re, b_im, c_re, c_im):
    dt = jnp.exp(log_dt)[:, None]
    mag = jnp.exp(lam_re * dt)
    ang = lam_im * dt
    ar = mag * jnp.cos(ang)
    ai = mag * jnp.sin(ang)
    nr = ar - 1.0
    ni = ai
    den = lam_re * lam_re + lam_im * lam_im
    cr = ((nr * lam_re + ni * lam_im) / den)[..., None]
    ci = ((ni * lam_re - nr * lam_im) / den)[..., None]
    bbr = cr * b_re - ci * b_im
    bbi = cr * b_im + ci * b_re
    gpp = SSM_WIN // SSM_STATE
    gpw = SSM_WIN // SSM_GROUP
    k2 = jnp.arange(N_PAIR)[:, None, None]
    gs = jnp.arange(gpp)[None, :, None]
    gl = jnp.arange(gpw)[None, None, :]
    same = (gl == gpp * (k2 % PAIRS_PER_WIN) + gs).astype(F32)

    def b_windows(bb):
        return jnp.einsum('kgl,kgpc->klcgp', same, bb.reshape(N_PAIR, gpp, SSM_STATE, SSM_GROUP)).reshape(
            N_PAIR, SSM_WIN, SSM_WIN)

    def c_windows(cc):
        return jnp.einsum('kgl,kgcp->kgplc', same, cc.reshape(N_PAIR, gpp, SSM_GROUP, SSM_STATE)).reshape(
            N_PAIR, SSM_WIN, SSM_WIN)

    pr, pi = ar, ai
    for _ in range(int(math.log2(SSM_TC))):
        pr, pi = pr * pr - pi * pi, 2.0 * pr * pi
    return dict(a_re=ar.reshape(N_SLAB, LANES), a_im=ai.reshape(N_SLAB, LANES),
                a64_re=pr.reshape(N_SLAB, LANES), a64_im=pi.reshape(N_SLAB, LANES),
                bw_re=b_windows(bbr), bw_im=b_windows(bbi), cw_re=c_windows(c_re), cw_im=c_windows(c_im))


def _layer_fwd(x, p, rest, rest_stage, next_bufs=None):
    first = {"ici": _gather_ici_rider, "d2d": _gather_d2d_rider}[rest_stage]
    outs = _in_proj_fwd(x, p["g_mix"], p["w_in"], first(rest))
    (z, h), rest = outs[:2], list(outs[2:])
    qkv, got = _qkv_prep(z, jnp.tile(p["g_q"], (1, N_HEADS)), jnp.tile(p["g_k"], (1, N_HEADS)),
                         _gather_d2d_rider(rest) if rest_stage == "ici" else None)
    p = {**p, **dict(zip(BIG[1:], got if rest_stage == "ici" else rest))}
    os_, ls_ = [], []
    for gi in range(N_GROUPS):
        o, l = _attn_fwd(*qkv[gi], gi)
        os_.append(o)
        ls_.append(l)
    a = _combine_fwd(os_, ls_)
    pk = _ssm_pack(p["lambda_re"], p["lambda_im"], p["log_dt"], p["b_re"], p["b_im"], p["c_re"], p["c_im"])
    outs = _ssm_fwd(z, pk, p["d_skip"], _gather_ici_rider(next_bufs[:1]) if next_bufs else None)
    (ypre, yact, sd_re, sd_im), next_in = outs[:4], list(outs[4:])
    x1, aout, sa, sb, mix = _mix_fwd(x, z, a, yact, p["w_attn_proj"], p["w_glu_a"], p["w_glu_b"], p["w_out"])
    outs = _ffn_fwd(x1, p["g_ffn"], p["w_ffn_gate"], p["w_ffn_up"], p["w_ffn_down"],
                    _join_riders(_gather_ici_rider(next_bufs[1:]), _gather_d2d_rider(next_in)) if next_bufs else None)
    x2, h2, gate, up, act = outs[:5]
    nxt = (outs[-1], list(outs[5:-1])) if next_bufs else None
    saved = dict(x=x, z=z, h=h, qkv=qkv, os=os_, ls=ls_, pk=pk, ypre=ypre, yact=yact, sd_re=sd_re, sd_im=sd_im,
                 x1=x1, a=a, aout=aout, sa=sa, sb=sb, mix=mix, h2=h2, gate=gate, up=up, act=act)
    return x2, saved, p, nxt


def _rms_bwd(xv, g, dh):
    r = lax.rsqrt(jnp.mean(xv * xv, axis=-1, keepdims=True) + EPS)
    xn = xv * r
    dxn = dh * g
    dx = r * (dxn - xn * jnp.mean(dxn * xn, axis=-1, keepdims=True))
    dg = jnp.sum((dh * xn).reshape(xv.shape[0] // SUBLANES, SUBLANES, xv.shape[1]), axis=0)
    return dx, dg


def _ffn_bwd_act(dx2, gate, up, w_d):
    L = dx2.shape[0]
    fs = D_FF // N_CHIPS
    TM = TM_PROJ

    def body(dx_ref, dact_dgate_ref, dact_dup_ref, wd_ref, dgate_ref, dup_ref, dxb_ref):
        @pl.when(pl.program_id(1) == 0)
        def _():
            dxb_ref[...] = dx_ref[...].astype(BF16)

        dact = _dot_nt(dxb_ref[...], wd_ref[...])
        dgate_ref[...] = (dact * dact_dgate_ref[...].astype(F32)).astype(BF16)
        dup_ref[...] = (dact * dact_dup_ref[...].astype(F32)).astype(BF16)

    ffs = pl.BlockSpec((None, TM, fs), lambda i, s: (s, i, 0))
    tok = pl.BlockSpec((TM, D_MODEL), lambda i, s: (i, 0))
    return _call(
        body, name="ffn_bwd_act", grid=(L // TM, N_CHIPS),
        in_specs=[tok, ffs, ffs, pl.BlockSpec((None, fs, D_MODEL), lambda i, s: (s, 0, 0))],
        out_specs=[ffs, ffs, tok], out_shape=[_sds((N_CHIPS, L, fs), BF16)] * 2 + [_sds((L, D_MODEL), BF16)],
        sem=("parallel", "arbitrary"))(dx2, gate, up, w_d)


def _ffn_bwd_in(dx2, x1, g, dgate, dup, w_g, w_u, rider=None):
    L = x1.shape[0]
    fs = D_FF // N_CHIPS
    TM = TM_PROJ

    def body(dx_ref, x_ref, g_ref, dgate_ref, dup_ref, wg_ref, wu_ref, dx1_ref, dg_ref, acc, dgacc):
        i, s = pl.program_id(0), pl.program_id(1)

        @pl.when(s == 0)
        def _():
            acc[...] = jnp.zeros_like(acc)

        @pl.when(jnp.logical_and(i == 0, s == 0))
        def _():
            dgacc[...] = jnp.zeros_like(dgacc)

        acc[...] += _dot(dgate_ref[...], wg_ref[...]) + _dot(dup_ref[...], wu_ref[...])

        @pl.when(s == N_CHIPS - 1)
        def _():
            dx, dg = _rms_bwd(x_ref[...], g_ref[...], acc[...])
            dx1_ref[...] = dx_ref[...] + dx
            dgacc[...] += dg

        @pl.when(jnp.logical_and(i == pl.num_programs(0) - 1, s == N_CHIPS - 1))
        def _():
            dg_ref[...] = jnp.sum(dgacc[...], axis=0, keepdims=True)

    tok = pl.BlockSpec((TM, D_MODEL), lambda i, s: (i, 0))
    ffs = pl.BlockSpec((None, TM, fs), lambda i, s: (s, i, 0))
    vec = pl.BlockSpec((1, D_MODEL), lambda i, s: (0, 0))
    return _call(
        body, name="ffn_bwd_in", grid=(L // TM, N_CHIPS),
        in_specs=[tok, tok, vec, ffs, ffs,
                  pl.BlockSpec((None, fs, D_MODEL), lambda i, s: (s, 0, 0)),
                  pl.BlockSpec((None, fs, D_MODEL), lambda i, s: (s, 0, 0))],
        out_specs=[tok, vec],
        out_shape=[_sds((L, D_MODEL), F32), _sds((1, D_MODEL), F32)],
        scratch=[pltpu.VMEM((TM, D_MODEL), F32), pltpu.VMEM((SUBLANES, D_MODEL), F32)],
        sem=("arbitrary", "arbitrary"), rider=rider)(dx2, x1, g, dgate, dup, w_g, w_u)


def _wgrad(a, b, *, name, grid_kn, a_spec, b_spec, out_shape, out_spec):
    L = a.shape[-2]
    nl = L // TL_WGRAD

    def body(a_ref, b_ref, o_ref):
        @pl.when(pl.program_id(2) == 0)
        def _():
            o_ref[...] = jnp.zeros_like(o_ref)
        o_ref[...] += _dot_tn(a_ref[...].astype(BF16), b_ref[...].astype(BF16))

    return _call(body, name=name, grid=(*grid_kn, nl), in_specs=[a_spec, b_spec], out_specs=out_spec,
                 out_shape=out_shape, sem=("parallel", "parallel", "arbitrary"))(a, b)


def _wgrad_cols(a, b, name):
    K, N = a.shape[1], b.shape[1]
    ns = N // N_CHIPS
    if N * K * 4 <= 4 * 1024 * 1024:
        L = a.shape[0]

        def body(a_ref, b_ref, o_ref):
            @pl.when(pl.program_id(0) == 0)
            def _():
                o_ref[...] = jnp.zeros_like(o_ref)
            av = a_ref[...].astype(BF16)
            for s in range(N_CHIPS):
                o_ref[s] += _dot_tn(av, b_ref[:, s * ns:(s + 1) * ns].astype(BF16))

        return _call(body, name=name, grid=(L // TL_WGRAD,),
                     in_specs=[pl.BlockSpec((TL_WGRAD, K), lambda t: (t, 0)),
                               pl.BlockSpec((TL_WGRAD, N), lambda t: (t, 0))],
                     out_specs=pl.BlockSpec((N_CHIPS, K, ns), lambda t: (0, 0, 0)),
                     out_shape=_sds((N_CHIPS, K, ns), F32), sem=("arbitrary",))(a, b)
    tn = ns // 2 if ns % (2 * LANES) == 0 else ns
    nj = ns // tn
    return _wgrad(a, b, name=name, grid_kn=(1, N_CHIPS * nj),
                  a_spec=pl.BlockSpec((TL_WGRAD, K), lambda i, j, t: (t, 0)),
                  b_spec=pl.BlockSpec((TL_WGRAD, tn), lambda i, j, t: (t, j)),
                  out_shape=_sds((N_CHIPS, K, ns), F32),
                  out_spec=pl.BlockSpec((None, K, tn), lambda i, j, t: (j // nj, 0, j % nj)))


def _wgrad_full(a, b, name):
    K, N = a.shape[1], b.shape[1]
    return _wgrad(a, b, name=name, grid_kn=(1, 1),
                  a_spec=pl.BlockSpec((TL_WGRAD, K), lambda i, j, t: (t, 0)),
                  b_spec=pl.BlockSpec((TL_WGRAD, N), lambda i, j, t: (t, 0)),
                  out_shape=_sds((K, N), F32), out_spec=pl.BlockSpec((K, N), lambda i, j, t: (0, 0)))


def _wgrad_ff_rows(a, b, name):
    fs, N = a.shape[2], b.shape[1]
    return _wgrad(a, b, name=name, grid_kn=(N_CHIPS, 1),
                  a_spec=pl.BlockSpec((None, TL_WGRAD, fs), lambda i, j, t: (i, t, 0)),
                  b_spec=pl.BlockSpec((TL_WGRAD, N), lambda i, j, t: (t, 0)),
                  out_shape=_sds((N_CHIPS, fs, N), F32),
                  out_spec=pl.BlockSpec((None, fs, N), lambda i, j, t: (i, 0, 0)))


def _mix_bwd(dx, z, aout, sa, sb, ypre, w_ap, w_ga, w_gb, w_out, rider=None):
    L = dx.shape[0]
    cs = D_MODEL // N_CHIPS
    ga_col = (3 * N_GROUPS * ATTN_WIDTH + SSM_WIDTH) // D_MODEL

    def body(dx_ref, ga_ref, gs_ref, aout_ref, sa_ref, sb_ref, ypre_ref, wap_ref, wga_ref, wgb_ref, wout_ref,
             dgates_ref, da_ref, gy_ref, daout_ref, dsa_ref, dsb_ref):
        dmix = _dot_nt(dx_ref[...].astype(BF16), wout_ref[...])
        sig_a = _sigmoid(ga_ref[...].astype(F32))
        sig_s = _sigmoid(gs_ref[...].astype(F32))
        a_out = aout_ref[...].astype(F32)
        s_a = sa_ref[...].astype(F32)
        sig_b = _sigmoid(sb_ref[...].astype(F32))
        s_out = s_a * sig_b
        daout = (dmix * sig_a).astype(BF16)
        daout_ref[...] = daout
        dgates_ref[:, :D_MODEL] = (dmix * a_out * sig_a * (1.0 - sig_a)).astype(BF16)
        dgates_ref[:, D_MODEL:] = (dmix * s_out * sig_s * (1.0 - sig_s)).astype(BF16)
        ds_out = dmix * sig_s
        dsa = (ds_out * sig_b).astype(BF16)
        dsb = (ds_out * s_a * sig_b * (1.0 - sig_b)).astype(BF16)
        dsa_ref[...] = dsa
        dsb_ref[...] = dsb
        da = jnp.zeros((TM_MIX, ATTN_WIDTH), F32)
        dy = jnp.zeros((TM_MIX, SSM_WIDTH), F32)
        for s in range(N_CHIPS):
            cols = slice(s * cs, (s + 1) * cs)
            da += _dot_nt(daout[:, cols], wap_ref[s])
            dy += _dot_nt(dsa[:, cols], wga_ref[s]) + _dot_nt(dsb[:, cols], wgb_ref[s])
        gy_ref[...] = dy * _gelu_grad(ypre_ref[...])
        da_ref[...] = da

    tok = lambda w: pl.BlockSpec((TM_MIX, w), lambda i: (i, 0))
    wsm = pl.BlockSpec((N_CHIPS, ATTN_WIDTH, cs), lambda i: (0, 0, 0))
    return _call(
        body, name="mix_bwd", grid=(L // TM_MIX,),
        in_specs=[tok(D_MODEL), pl.BlockSpec((TM_MIX, D_MODEL), lambda i: (i, ga_col)),
                  pl.BlockSpec((TM_MIX, D_MODEL), lambda i: (i, ga_col + 1)),
                  tok(D_MODEL), tok(D_MODEL), tok(D_MODEL), tok(SSM_WIDTH),
                  wsm, wsm, wsm, pl.BlockSpec((D_MODEL, D_MODEL), lambda i: (0, 0))],
        out_specs=[tok(2 * D_MODEL), tok(ATTN_WIDTH), tok(SSM_WIDTH)] + [tok(D_MODEL)] * 3,
        out_shape=[_sds((L, 2 * D_MODEL), BF16), _sds((L, ATTN_WIDTH), F32), _sds((L, SSM_WIDTH), F32)]
                  + [_sds((L, D_MODEL), BF16)] * 3,
        sem=("parallel",), rider=rider)(dx, z, z, aout, sa, sb, ypre, w_ap, w_ga, w_gb,
                                        w_out.reshape(D_MODEL, D_MODEL))


def _combine_bwd(da, os_, ls_):
    L = da.shape[0]

    def body(da_ref, o0, o1, o2, l0, l1, l2, hs_ref, p1_ref, p2_ref, pt1_ref, pt2_ref,
             do0, do1, do2, c0, c1, c2):
        o_tok, l_tok = _to_token_order((o0[...], o1[...], o2[...]), (l0[...], l1[...], l2[...]),
                                       (None, pt1_ref[...], pt2_ref[...]))
        w = _combine_weights(*l_tok)
        dav = da_ref[...]
        hs = hs_ref[...]
        tbar = sum(wg * _head_sum(dav * og, hs) for wg, og in zip(w, o_tok))
        for wg, pm, do_ref, c_ref in zip(w, (None, p1_ref[...], p2_ref[...]), (do0, do1, do2), (c0, c1, c2)):
            dog = (wg * dav).astype(BF16)
            cg = -wg * tbar
            do_ref[...] = dog if pm is None else _dot(pm, dog).astype(BF16)
            c_ref[...] = cg if pm is None else _dot(pm, cg.astype(BF16))

    tile = pl.BlockSpec((DL_TILE, ATTN_WIDTH), lambda i: (i, 0))
    mat = pl.BlockSpec((DL_TILE, DL_TILE), lambda i: (0, 0))
    p1, p2 = _perm_matrix(ATTN_PATTERN[1][1]), _perm_matrix(ATTN_PATTERN[2][1])
    outs = _call(body, name="combine_bwd", grid=(L // DL_TILE,), in_specs=[tile] * 7 + [mat] * 5,
                 out_specs=[tile] * 6,
                 out_shape=[_sds((L, ATTN_WIDTH), BF16)] * 3 + [_sds((L, ATTN_WIDTH), F32)] * 3,
                 sem=("parallel",))(da, *os_, *ls_, _head_sum_matrix(), p1, p2, p1.T, p2.T)
    return outs[:3], outs[3:]


def _attn_bwd(qs, ks, v, do, l, c, gi, rider=None):
    L = qs.shape[0]
    _, d = ATTN_PATTERN[gi]
    nb = L // (d * BLK)

    def body(q0_ref, q1_ref, k_ref, v_ref, do0_ref, do1_ref, l0_ref, l1_ref, c0_ref, c1_ref,
             dq_ref, dk_ref, dv_ref, carry):
        n = pl.program_id(1)

        @pl.when(n == 0)
        def _():
            carry[...] = jnp.zeros_like(carry)

        qi = lax.broadcasted_iota(jnp.int32, (2 * BLK, BLK), 0)
        kj = lax.broadcasted_iota(jnp.int32, (2 * BLK, BLK), 1)
        first = qi < BLK
        mask = jnp.logical_and(jnp.where(first, qi, kj) >= jnp.where(first, kj, qi - BLK),
                               qi < jnp.where(n < nb - 1, 2 * BLK, BLK))
        q2 = jnp.concatenate([_dl_read(q0_ref), _dl_read(q1_ref)], axis=0)
        do2 = jnp.concatenate([_dl_read(do0_ref), _dl_read(do1_ref)], axis=0)
        l2 = jnp.concatenate([_dl_read(l0_ref), _dl_read(l1_ref)], axis=0)
        c2 = jnp.concatenate([_dl_read(c0_ref), _dl_read(c1_ref)], axis=0)
        k = _dl_read(k_ref)
        v_ = _dl_read(v_ref)
        h0, h1 = _pair_masks()
        mask2 = jnp.concatenate([mask, mask], axis=1)
        dq_parts, dk_parts, dv_parts = [], [], []
        for hp in range(N_HEADS // 2):
            ls = slice(hp * LANES, (hp + 1) * LANES)
            qp, dop, kp_, vp_ = q2[:, ls], do2[:, ls], k[:, ls], v_[:, ls]
            kk = jnp.concatenate([jnp.where(h0, kp_, 0), jnp.where(h1, kp_, 0)], axis=0)
            vv = jnp.concatenate([jnp.where(h0, vp_, 0), jnp.where(h1, vp_, 0)], axis=0)

            def per_head(t):
                a = jnp.broadcast_to(t[:, hp * LANES:hp * LANES + 1], (2 * BLK, BLK))
                b = jnp.broadcast_to(t[:, hp * LANES + HEAD_DIM:hp * LANES + HEAD_DIM + 1], (2 * BLK, BLK))
                return jnp.concatenate([a, b], axis=1)

            p = jnp.where(mask2, jnp.exp(_dot_nt(qp, kk) - per_head(l2)), 0.0)
            ds = (p * (_dot_nt(dop, vv) + per_head(c2))).astype(BF16)
            dv2 = _dot_tn(p.astype(BF16), dop)
            dk2 = _dot_tn(ds, qp)
            dq2 = _dot(ds, kk)
            dq_parts.append((dq2[:BLK] + carry[:, ls]).astype(BF16))
            carry[:, ls] = dq2[BLK:]
            dk_parts.append(jnp.where(h0, dk2[:BLK], dk2[BLK:]).astype(BF16))
            dv_parts.append(jnp.where(h0, dv2[:BLK], dv2[BLK:]).astype(BF16))
        _dl_write(dq_ref, jnp.concatenate(dq_parts, axis=1))
        _dl_write(dk_ref, jnp.concatenate(dk_parts, axis=1))
        _dl_write(dv_ref, jnp.concatenate(dv_parts, axis=1))

    cur = _dl_spec(d, ATTN_WIDTH, lambda n: n)
    nxt = _dl_spec(d, ATTN_WIDTH, lambda n: jnp.minimum(n + 1, nb - 1))
    view = lambda t: _dl_view(t, d)
    outs = _call(
        body, name=f"attn_bwd_g{gi}", grid=(d, nb),
        in_specs=[cur, nxt, cur, cur, cur, nxt, cur, nxt, cur, nxt], out_specs=[cur, cur, cur],
        out_shape=[_sds(view(qs).shape, BF16)] * 3, scratch=[pltpu.VMEM((BLK, ATTN_WIDTH), F32)],
        sem=("parallel", "arbitrary"), rider=rider)(view(qs), view(qs), view(ks), view(v), view(do), view(do),
                                                    view(l), view(l), view(c), view(c))
    return [t.reshape(L, ATTN_WIDTH) for t in outs[:3]], list(outs[3:])


def _qkv_post(z, dqkv, du, dgates, gq_t, gk_t):
    L = z.shape[0]
    qkv_w = N_GROUPS * ATTN_WIDTH

    def body(zq_ref, zk_ref, gq_ref, gk_ref, hs_ref, pt1_ref, pt2_ref, du_ref, dgates_ref, *rest):
        dl_refs, (dz_ref, dgq_ref, dgk_ref) = rest[:9], rest[9:]

        @pl.when(pl.program_id(0) == 0)
        def _():
            dgq_ref[...] = jnp.zeros_like(dgq_ref)
            dgk_ref[...] = jnp.zeros_like(dgk_ref)

        hs = hs_ref[...]
        pts = (None, pt1_ref[...], pt2_ref[...])

        def rows8(t):
            return jnp.sum(t.reshape(DL_TILE // SUBLANES, SUBLANES, ATTN_WIDTH), axis=0)

        def norm_bwd(x, gain, dn):
            r = lax.rsqrt(_head_sum(x * x, hs) * (1.0 / HEAD_DIM) + EPS)
            xh = x * r
            dh = dn * gain
            return r * (dh - xh * (_head_sum(dh * xh, hs) * (1.0 / HEAD_DIM))), rows8(dn * xh)

        for g in range(N_GROUPS):
            tok = [t[...].astype(F32) if pts[g] is None else _dot(pts[g], t[...]) for t in dl_refs[3 * g:3 * g + 3]]
            cols = slice(g * ATTN_WIDTH, (g + 1) * ATTN_WIDTH)
            dq, pq = norm_bwd(zq_ref[:, cols].astype(F32), gq_ref[...] * SCALE, tok[0])
            dk, pk_ = norm_bwd(zk_ref[:, cols].astype(F32), gk_ref[...], tok[1])
            dgq_ref[...] += pq * SCALE
            dgk_ref[...] += pk_
            dz_ref[:, cols] = dq.astype(BF16)
            dz_ref[:, qkv_w + g * ATTN_WIDTH:qkv_w + (g + 1) * ATTN_WIDTH] = dk.astype(BF16)
            dz_ref[:, 2 * qkv_w + g * ATTN_WIDTH:2 * qkv_w + (g + 1) * ATTN_WIDTH] = tok[2].astype(BF16)
        dz_ref[:, 3 * qkv_w:3 * qkv_w + SSM_WIDTH] = du_ref[...]
        dz_ref[:, 3 * qkv_w + SSM_WIDTH:] = dgates_ref[...]

    tile = lambda w: pl.BlockSpec((DL_TILE, w), lambda i: (i, 0))
    mat = pl.BlockSpec((DL_TILE, DL_TILE), lambda i: (0, 0))
    vec = pl.BlockSpec((1, ATTN_WIDTH), lambda i: (0, 0))
    acc = pl.BlockSpec((SUBLANES, ATTN_WIDTH), lambda i: (0, 0))
    flat = [t for grp in dqkv for t in grp]
    return _call(
        body, name="qkv_post", grid=(L // DL_TILE,),
        in_specs=[tile(qkv_w), pl.BlockSpec((DL_TILE, qkv_w), lambda i: (i, 1)), vec, vec, mat, mat, mat,
                  tile(SSM_WIDTH), tile(2 * D_MODEL)] + [tile(ATTN_WIDTH)] * 9,
        out_specs=[tile(IN_COLS), acc, acc],
        out_shape=[_sds((L, IN_COLS), BF16), _sds((SUBLANES, ATTN_WIDTH), F32), _sds((SUBLANES, ATTN_WIDTH), F32)],
        sem=("arbitrary",))(z, z, gq_t, gk_t, _head_sum_matrix(), _perm_matrix(ATTN_PATTERN[1][1]).T,
                            _perm_matrix(ATTN_PATTERN[2][1]).T, du, dgates, *flat)


def _scan_rev_grad(sre, sim, rre, rim, are_ref, aim_ref, k0, init, seed_re, seed_im):
    ar = [_bcast(are_ref, k0 + kk) for kk in range(SCAN_GROUP)]
    ai = [-_bcast(aim_ref, k0 + kk) for kk in range(SCAN_GROUP)]

    def update(i, xprev, carry):
        out = []
        for kk in range(SCAN_GROUP):
            k = k0 + kk
            lr, li, dr, di = carry[4 * kk:4 * kk + 4]
            nr = ar[kk] * lr - ai[kk] * li + rre[k, _rows(i), :]
            ni = ar[kk] * li + ai[kk] * lr + rim[k, _rows(i), :]
            rre[k, _rows(i), :] = nr
            rim[k, _rows(i), :] = ni
            xr, xi = xprev(k)
            out += [nr, ni, dr + xr * nr + xi * ni, di + xr * ni - xi * nr]
        return tuple(out)

    def step(t, carry):
        i = SSM_TC - 1 - t
        return update(i, lambda k: (sre[k, _rows(i - 1), :], sim[k, _rows(i - 1), :]), carry)

    zero = jnp.zeros((SSM_SUB, LANES), F32)
    flat = []
    for re, im in init:
        flat += [re, im, zero, zero]
    res = lax.fori_loop(0, (SSM_TC - 1) // 2, lambda t, c: step(2 * t + 1, step(2 * t, c)), tuple(flat))
    res = step(SSM_TC - 2, res)
    res = update(0, lambda k: (seed_re[k], seed_im[k]), res)
    return [(res[4 * kk + 2], res[4 * kk + 3]) for kk in range(SCAN_GROUP)]


def _ssm_bwd(z, gy, pk, dskip, sd_re, sd_im, rider=None):
    L = z.shape[0]
    nb = L // SSM_TB
    ucol = (3 * N_GROUPS * ATTN_WIDTH) // SSM_WIDTH
    nwin = N_PAIR // PAIRS_PER_WIN

    def body(u_ref, gy_ref, are_ref, aim_ref, a64re_ref, a64im_ref, bwre_ref, bwim_ref, cwre_ref, cwim_ref, d_ref,
             sdre_ref, sdim_ref,
             du_ref, dare_ref, daim_ref, dbre_ref, dbim_ref, dcre_ref, dcim_ref, dd_ref,
             sre, sim, rre, rim, carry_re, carry_im, ends_re, ends_im, seed_re, seed_im):
        @pl.when(pl.program_id(0) == 0)
        def _():
            carry_re[...] = jnp.zeros_like(carry_re)
            carry_im[...] = jnp.zeros_like(carry_im)
            for ref in (dare_ref, daim_ref, dbre_ref, dbim_ref, dcre_ref, dcim_ref, dd_ref):
                ref[...] = jnp.zeros_like(ref)

        u = u_ref[...]
        gyv = gy_ref[...]
        gyb = gyv.astype(BF16)
        _ssm_fill(u, bwre_ref, bwim_ref, sre, sim)
        for k2 in range(N_PAIR):
            gw = gyb[:, _win_cols(k2)]
            _to_slabs(rre, k2, _dot_nt(gw, cwre_ref[k2]))
            _to_slabs(rim, k2, -_dot_nt(gw, cwim_ref[k2]))
        zero = jnp.zeros((SSM_SUB, LANES), F32)
        for k0 in range(0, N_SLAB, SCAN_GROUP):
            grp = range(k0, k0 + SCAN_GROUP)
            _scan(sre, sim, are_ref, aim_ref, k0, [(sdre_ref[k], sdim_ref[k]) for k in grp],
                  reverse=False, store=True)
            ends = _scan(rre, rim, are_ref, aim_ref, k0, [(zero, zero)] * SCAN_GROUP, reverse=True, store=False,
                         sign=-1.0)
            for kk, k in enumerate(grp):
                ends_re[k] = ends[kk][0]
                ends_im[k] = ends[kk][1]
            for k in grp:
                _ssm_seeds(ends_re, ends_im, a64re_ref, a64im_ref, carry_re, carry_im, seed_re, seed_im, k,
                           reverse=True, sign=-1.0)
            das = _scan_rev_grad(sre, sim, rre, rim, are_ref, aim_ref, k0,
                                 [(seed_re[k], seed_im[k]) for k in grp], sdre_ref, sdim_ref)
            for kk, k in enumerate(grp):
                dare_ref[k] += das[kk][0]
                daim_ref[k] += das[kk][1]
        for w in range(nwin):
            cols = _win_cols(w * PAIRS_PER_WIN)
            uw = u[:, cols]
            gw = gyb[:, cols]
            acc = gyv[:, cols] * d_ref[:, cols]
            for kk in range(PAIRS_PER_WIN):
                k2 = w * PAIRS_PER_WIN + kk
                lr = _pair_rows(rre, k2)
                li = _pair_rows(rim, k2)
                acc += _dot_nt(lr, bwre_ref[k2]) + _dot_nt(li, bwim_ref[k2])
                dbre_ref[k2] += _dot_tn(uw, lr)
                dbim_ref[k2] += _dot_tn(uw, li)
                dcre_ref[k2] += _dot_tn(_pair_rows(sre, k2), gw)
                dcim_ref[k2] -= _dot_tn(_pair_rows(sim, k2), gw)
            du_ref[:, cols] = acc.astype(BF16)
        dd_ref[...] += jnp.sum((gyv * u.astype(F32)).reshape(SSM_TB // SUBLANES, SUBLANES, SSM_WIDTH), axis=0)

    c2, c3 = _ssm_specs_consts()
    rev = lambda b: nb - 1 - b
    seed_spec = pl.BlockSpec((None, N_SLAB, SSM_SUB, LANES), lambda b: (rev(b), 0, 0, 0))
    tile_out = pl.BlockSpec((N_SLAB, SSM_SUB, LANES), lambda b: (0, 0, 0))
    small = pltpu.VMEM((N_SLAB, LANES), F32)
    tile = pltpu.VMEM((N_SLAB, SSM_SUB, LANES), F32)
    return _call(
        body, name="ssm_bwd", grid=(nb,),
        in_specs=[pl.BlockSpec((SSM_TB, SSM_WIDTH), lambda b: (rev(b), ucol)),
                  pl.BlockSpec((SSM_TB, SSM_WIDTH), lambda b: (rev(b), 0)),
                  c2, c2, c2, c2, c3, c3, c3, c3, pl.BlockSpec((1, SSM_WIDTH), lambda b: (0, 0)),
                  seed_spec, seed_spec],
        out_specs=[pl.BlockSpec((SSM_TB, SSM_WIDTH), lambda b: (rev(b), 0)), tile_out, tile_out, c3, c3, c3, c3,
                   pl.BlockSpec((SUBLANES, SSM_WIDTH), lambda b: (0, 0))],
        out_shape=[_sds((L, SSM_WIDTH), BF16), _sds((N_SLAB, SSM_SUB, LANES), F32),
                   _sds((N_SLAB, SSM_SUB, LANES), F32)] + [_sds((N_PAIR, SSM_WIN, SSM_WIN), F32)] * 4
                  + [_sds((SUBLANES, SSM_WIDTH), F32)],
        scratch=_ssm_scratch() + _ssm_scratch() + [small, small, tile, tile, tile, tile],
        sem=("arbitrary",), rider=rider)(z, gy, pk["a_re"], pk["a_im"], pk["a64_re"], pk["a64_im"],
                            pk["bw_re"].astype(BF16), pk["bw_im"].astype(BF16),
                            pk["cw_re"].astype(BF16), pk["cw_im"].astype(BF16), dskip, sd_re, sd_im)


def _in_proj_bwd(dz, w, x, g, dres, rider=None):
    L = x.shape[0]
    ns = w.shape[2]
    tn = ns
    nj = ns // tn
    nt = N_CHIPS * nj
    TM = TM_PROJ

    def body(dz_ref, w_ref, x_ref, g_ref, dres_ref, dx_ref, dg_ref, acc, dgacc):
        i, j = pl.program_id(0), pl.program_id(1)

        @pl.when(j == 0)
        def _():
            acc[...] = jnp.zeros_like(acc)

        @pl.when(jnp.logical_and(i == 0, j == 0))
        def _():
            dgacc[...] = jnp.zeros_like(dgacc)

        acc[...] += _dot_nt(dz_ref[...], w_ref[...])

        @pl.when(j == nt - 1)
        def _():
            dx, dg = _rms_bwd(x_ref[...], g_ref[...], acc[...])
            dx_ref[...] = dres_ref[...] + dx
            dgacc[...] += dg

        @pl.when(jnp.logical_and(i == pl.num_programs(0) - 1, j == nt - 1))
        def _():
            dg_ref[...] = jnp.sum(dgacc[...], axis=0, keepdims=True)

    tok = pl.BlockSpec((TM, D_MODEL), lambda i, j: (i, 0))
    vec = pl.BlockSpec((1, D_MODEL), lambda i, j: (0, 0))
    return _call(
        body, name="in_proj_bwd", grid=(L // TM, nt),
        in_specs=[pl.BlockSpec((TM, tn), lambda i, j: (i, j)),
                  pl.BlockSpec((None, D_MODEL, tn), lambda i, j: (j // nj, 0, j % nj)), tok, vec, tok],
        out_specs=[tok, vec],
        out_shape=[_sds((L, D_MODEL), F32), _sds((1, D_MODEL), F32)],
        scratch=[pltpu.VMEM((TM, D_MODEL), F32), pltpu.VMEM((SUBLANES, D_MODEL), F32)],
        sem=("arbitrary", "arbitrary"), rider=rider)(dz, w, x, g, dres)


SSM_PARAMS = ("lambda_re", "lambda_im", "log_dt", "b_re", "b_im", "c_re", "c_im")
EARLY = ("w_ffn_gate", "w_ffn_up", "w_ffn_down")
MID = ("w_attn_proj", "w_glu_a", "w_glu_b", "w_out")
MID_RIDE = ((0,), (1, 2), (3,))
LATE = ("w_in",)


def _layer_bwd(dx2, sv, p, pending, owned, l, idx):
    chip_idx, c_idx = idx
    g = {}
    owned = dict(owned)

    def settle(name, partial, arrived, layer):
        owned[name] = _sum_owner(partial, arrived, owned[name], layer, chip_idx, c_idx)

    dgate, dup, dx2b = _ffn_bwd_act(dx2, sv["gate"], sv["up"], p["w_ffn_down"])
    dx1, g["g_ffn"] = _ffn_bwd_in(dx2, sv["x1"], p["g_ffn"], dgate, dup, p["w_ffn_gate"], p["w_ffn_up"])
    g["w_ffn_gate"] = _wgrad_ff_rows(dgate, sv["h2"], "wgrad_ffn_gate")
    g["w_ffn_up"] = _wgrad_ff_rows(dup, sv["h2"], "wgrad_ffn_up")
    g["w_ffn_down"] = _wgrad_ff_rows(sv["act"], dx2b, "wgrad_ffn_down")

    outs = _mix_bwd(dx1, sv["z"], sv["aout"], sv["sa"], sv["sb"], sv["ypre"], p["w_attn_proj"], p["w_glu_a"],
                    p["w_glu_b"], p["w_out"], _swap_rider([g[n] for n in EARLY]))
    dgates, da, gy, daout, dsa, dsb = outs[:6]
    early = [_add_half(g[n], s, c_idx) for n, s in zip(EARLY, outs[6:])]
    g["w_out"] = _wgrad_full(sv["mix"], dx1, "wgrad_out").reshape(N_CHIPS, D_MODEL // N_CHIPS, D_MODEL)
    g["w_attn_proj"] = _wgrad_cols(sv["a"], daout, "wgrad_attn_proj")
    g["w_glu_a"] = _wgrad_cols(sv["yact"], dsa, "wgrad_glu_a")
    g["w_glu_b"] = _wgrad_cols(sv["yact"], dsb, "wgrad_glu_b")

    outs = _ssm_bwd(sv["z"], gy, sv["pk"], p["d_skip"], sv["sd_re"], sv["sd_im"],
                    _join_riders(_scatter_rider([pending[LATE[0]]]) if pending else None,
                                 _swap_rider([g[n] for n in MID])))
    du, da_re, da_im, dbw_re, dbw_im, dcw_re, dcw_im, dd = outs[:8]
    if pending:
        settle(LATE[0], pending[LATE[0]], outs[8], l + 1)
    mid = [_add_half(g[n], s, c_idx) for n, s in zip(MID, outs[-len(MID):])]
    g["d_skip"] = jnp.sum(dd, axis=0, keepdims=True)
    g["ssm_pack_ct"] = dict(a_re=da_re, a_im=da_im, bw_re=dbw_re, bw_im=dbw_im, cw_re=dcw_re, cw_im=dcw_im)

    dos, cs = _combine_bwd(da, sv["os"], sv["ls"])
    dqkv = []
    for gi in range(N_GROUPS):
        grads, arrived = _attn_bwd(*sv["qkv"][gi], dos[gi], sv["ls"][gi], cs[gi], gi,
                                   _scatter_rider([early[gi]] + [mid[k] for k in MID_RIDE[gi]]))
        settle(EARLY[gi], early[gi], arrived[0], l)
        for k, t in zip(MID_RIDE[gi], arrived[1:]):
            settle(MID[k], mid[k], t, l)
        dqkv.append(grads)
    dz, gq8, gk8 = _qkv_post(sv["z"], dqkv, du, dgates, jnp.tile(p["g_q"], (1, N_HEADS)),
                             jnp.tile(p["g_k"], (1, N_HEADS)))
    g["g_q"] = jnp.sum(gq8.reshape(SUBLANES * N_HEADS, HEAD_DIM), axis=0, keepdims=True)
    g["g_k"] = jnp.sum(gk8.reshape(SUBLANES * N_HEADS, HEAD_DIM), axis=0, keepdims=True)
    g["w_in"] = _wgrad_cols(sv["h"], dz, "wgrad_in")
    outs = _in_proj_bwd(dz, p["w_in"], sv["x"], p["g_mix"], dx1, _swap_rider([g[n] for n in LATE]))
    dx, g["g_mix"] = outs[:2]
    late = {n: _add_half(g[n], s, c_idx) for n, s in zip(LATE, outs[2:])}
    return dx, g, late, owned


def _place():
    x, y, c = lax.axis_index("x"), lax.axis_index("y"), lax.axis_index("c")
    others = [(1 - x, y), (x, 1 - y), (1 - x, 1 - y)]
    return x, y, c, others


def _half(ref, hc):
    rows = ref.shape[-2] // 2
    idx = (slice(None),) * (len(ref.shape) - 2) + (pl.ds(hc * rows, rows), slice(None))
    return ref.at[idx]


def _comm_call(body, name, ins, out_shapes, n_remote, aliases=None):
    scratch = [pltpu.SemaphoreType.DMA((n_remote,)), pltpu.SemaphoreType.DMA((n_remote,))]
    return pl.pallas_call(
        body, name=name, in_specs=[ANY] * len(ins), out_specs=[ANY] * len(out_shapes), out_shape=out_shapes,
        scratch_shapes=scratch, input_output_aliases=aliases or {})(*ins)


def _cast_place(w, l, chip_idx):
    _, R, C = w.shape
    tr = R // 2

    def body(me_ref, w_ref, o_ref):
        o_ref[...] = w_ref[...].astype(BF16)

    return _call(body, name=f"cast_place_l{l}", grid=(R // tr,), prefetch=1,
                 in_specs=[pl.BlockSpec((None, tr, C), lambda i, me_ref: (l, i, 0))],
                 out_specs=pl.BlockSpec((None, tr, C), lambda i, me_ref: (me_ref[0], i, 0)),
                 out_shape=_sds((N_CHIPS, R, C), BF16), sem=("arbitrary",))(chip_idx, w)


def _in_place_rider(bufs, pairs, per_buf=3):
    n = len(bufs)

    def copies(outs, send, recv, side):
        return [pltpu.make_async_remote_copy(src_ref=pair[side][0], dst_ref=pair[side][0], send_sem=send.at[k],
                                             recv_sem=recv.at[k], device_id=pair[side][1], device_id_type=MESH)
                for k, pair in enumerate(pairs(outs))]

    def start(ins, outs, send, recv):
        for cp in copies(outs, send, recv, 0):
            cp.start()

    def wait(ins, outs, send, recv):
        for cp in copies(outs, send, recv, 1):
            cp.wait_recv()
        for cp in copies(outs, send, recv, 0):
            cp.wait_send()

    return Rider(list(bufs), [_sds(b.shape, b.dtype) for b in bufs], per_buf * n, start, wait,
                 {a: a for a in range(n)})


def _gather_ici_rider(bufs):
    def pairs(outs):
        x, y, c, others = _place()
        return [((_half(o.at[2 * x + y], c), (cx, cy, c)), (_half(o.at[2 * cx + cy], c), (cx, cy, c)))
                for o in outs for cx, cy in others]
    return _in_place_rider(bufs, pairs)


def _gather_d2d_rider(bufs):
    def pairs(outs):
        x, y, c, others = _place()
        sib = (x, y, 1 - c)
        return [((_half(o.at[2 * cx + cy], c), sib), (_half(o.at[2 * cx + cy], 1 - c), sib))
                for o in outs for cx, cy in others]
    return _in_place_rider(bufs, pairs)


def _swap_rider(gs):
    n = len(gs)

    def copies(ins, outs, send, recv):
        x, y, c, _ = _place()
        return [pltpu.make_async_remote_copy(src_ref=_half(ins[a], 1 - c), dst_ref=outs[a], send_sem=send.at[a],
                                             recv_sem=recv.at[a], device_id=(x, y, 1 - c), device_id_type=MESH)
                for a in range(n)]

    def start(ins, outs, send, recv):
        for cp in copies(ins, outs, send, recv):
            cp.start()

    def wait(ins, outs, send, recv):
        for cp in copies(ins, outs, send, recv):
            cp.wait()

    outs = [_sds((g.shape[0], g.shape[1] // 2, g.shape[2]), g.dtype) for g in gs]
    return Rider(list(gs), outs, n, start, wait, {})


def _scatter_rider(ss):
    n = len(ss)

    def copies(ins, outs, send, recv):
        x, y, c, others = _place()
        return [pltpu.make_async_remote_copy(
            src_ref=ins[a].at[2 * cx + cy], dst_ref=outs[a].at[j], send_sem=send.at[3 * a + j],
            recv_sem=recv.at[3 * a + j], device_id=(cx, cy, c), device_id_type=MESH)
            for a in range(n) for j, (cx, cy) in enumerate(others)]

    def start(ins, outs, send, recv):
        for cp in copies(ins, outs, send, recv):
            cp.start()

    def wait(ins, outs, send, recv):
        for cp in copies(ins, outs, send, recv):
            cp.wait()

    outs = [_sds((N_CHIPS - 1,) + s.shape[1:], s.dtype) for s in ss]
    return Rider(list(ss), outs, 3 * n, start, wait, {})


def _run_rider(rider, name):
    n_in = len(rider.ins)

    def body(*refs):
        ins, outs = refs[:n_in], refs[n_in:n_in + len(rider.out_shapes)]
        send, recv = refs[n_in + len(rider.out_shapes):]
        rider.start(ins, outs, send, recv)
        rider.wait(ins, outs, send, recv)

    return _comm_call(body, name, rider.ins, rider.out_shapes, rider.n_sem, aliases=rider.aliases)


def _join_rider(bufs):
    def pairs(outs):
        x, y, c, _ = _place()
        sib = (x, y, 1 - c)
        return [((_half(o, c), sib), (_half(o, 1 - c), sib)) for o in outs]
    return _in_place_rider(bufs, pairs, per_buf=1)


def _place_small(v, dev_idx):
    rows, n = v.shape

    def body(idx_ref, v_ref, o_ref):
        o_ref[...] = v_ref[...]

    return _call(body, name="place_small", grid=(1,), prefetch=1,
                 in_specs=[pl.BlockSpec((rows, n), lambda i, idx_ref: (0, 0))],
                 out_specs=pl.BlockSpec((None, rows, n), lambda i, idx_ref: (idx_ref[0], 0, 0)),
                 out_shape=_sds((8, rows, n), v.dtype), sem=("arbitrary",))(dev_idx, v)


def _small_ici_rider(buf):
    def pairs(outs):
        x, y, c, others = _place()
        peers = [(x, y, 1 - c)] + [(cx, cy, c) for cx, cy in others]
        return [((outs[0].at[4 * x + 2 * y + c], peer), (outs[0].at[4 * peer[0] + 2 * peer[1] + peer[2]], peer))
                for peer in peers]
    return _in_place_rider([buf], pairs, per_buf=4)


def _small_d2d_rider(buf):
    def pairs(outs):
        x, y, c, others = _place()
        sib = (x, y, 1 - c)
        return [((outs[0].at[4 * cx + 2 * cy + c], sib), (outs[0].at[4 * cx + 2 * cy + 1 - c], sib))
                for cx, cy in others]
    return _in_place_rider([buf], pairs)


def _add_half(g, p, c):
    _, R, C = g.shape
    half = R // 2

    def body(c_ref, g_ref, p_ref, o_ref):
        o_ref[...] = g_ref[...] + p_ref[...]

    blk = (None, half, C)
    return _call(body, name="add_half", grid=(N_CHIPS,), prefetch=1,
                 in_specs=[pl.BlockSpec(blk, lambda s, c_ref: (s, c_ref[0], 0)),
                           pl.BlockSpec(blk, lambda s, c_ref: (s, 0, 0))],
                 out_specs=pl.BlockSpec(blk, lambda s, c_ref: (s, 0, 0)),
                 out_shape=_sds((N_CHIPS, half, C), F32), sem=("arbitrary",))(c, g, p)


def _sum_owner(s, q, buf, l, me, c):
    _, half, C = s.shape
    tr = half // 2

    def body(me_ref, c_ref, s_ref, q0, q1, q2, buf_ref, o_ref):
        o_ref[...] = ((s_ref[...] + q0[...].astype(F32)) + q1[...].astype(F32)) + q2[...].astype(F32)

    blk = (None, tr, C)
    qspec = lambda j: pl.BlockSpec(blk, lambda i, me_ref, c_ref: (j, i, 0))
    return _call(body, name=f"sum_owner_l{l}", grid=(half // tr,), prefetch=2,
                 in_specs=[pl.BlockSpec(blk, lambda i, me_ref, c_ref: (me_ref[0], i, 0)),
                           qspec(0), qspec(1), qspec(2), ANY],
                 out_specs=pl.BlockSpec(blk, lambda i, me_ref, c_ref: (l, 2 * c_ref[0] + i, 0)),
                 out_shape=_sds(buf.shape, F32), sem=("arbitrary",), aliases={6: 0})(me, c, s, q, q, q, buf)


def _adamw_math(w, g, m, v):
    m = ADAM_B1 * m + (1.0 - ADAM_B1) * g
    v = ADAM_B2 * v + (1.0 - ADAM_B2) * (g * g)
    m_hat = m / (1.0 - ADAM_B1 ** ADAM_STEP)
    v_hat = v / (1.0 - ADAM_B2 ** ADAM_STEP)
    delta = -ADAM_LR * (m_hat / (jnp.sqrt(v_hat) + ADAM_EPS) + ADAM_WD * w)
    return delta, m, v


def _adamw(w, g, m, v, rider=None):
    depth, R, C = w.shape
    tr = max(t for t in range(SUBLANES, R + 1, SUBLANES) if R % t == 0 and t * C * 4 <= 2 * 1024 * 1024)

    def body(w_ref, g_ref, m_ref, v_ref, d_ref, nm_ref, nv_ref):
        d, nm, nv = _adamw_math(w_ref[...], g_ref[...], m_ref[...], v_ref[...])
        d_ref[...] = d
        nm_ref[...] = nm
        nv_ref[...] = nv

    spec = pl.BlockSpec((None, tr, C), lambda l, i: (l, i, 0))
    return _call(body, name="adamw", grid=(depth, R // tr), in_specs=[spec] * 4, out_specs=[spec] * 3,
                 out_shape=[_sds(w.shape, F32)] * 3, sem=("parallel", "parallel"), rider=rider)(w, g, m, v)


def _small_update(gathered, w, m, v):
    _, rows, n = gathered.shape
    tr = rows // 7

    def body(ga_ref, w_ref, m_ref, v_ref, g_ref, d_ref, nm_ref, nv_ref):
        g = ga_ref[0]
        for k in range(1, 8):
            g = g + ga_ref[k]
        d, nm, nv = _adamw_math(w_ref[...], g, m_ref[...], v_ref[...])
        g_ref[...] = g
        d_ref[...] = d
        nm_ref[...] = nm
        nv_ref[...] = nv

    spec = pl.BlockSpec((tr, n), lambda i: (i, 0))
    return _call(body, name="small_update", grid=(rows // tr,),
                 in_specs=[pl.BlockSpec((8, tr, n), lambda i: (0, i, 0)), spec, spec, spec], out_specs=[spec] * 4,
                 out_shape=[_sds((rows, n), F32)] * 4, sem=("parallel",))(gathered, w, m, v)


WEIGHTS = ("g_mix", "w_in", "g_q", "g_k", "w_attn_proj", "lambda_re", "lambda_im", "log_dt", "b_re", "b_im",
           "c_re", "c_im", "d_skip", "w_glu_a", "w_glu_b", "w_out", "g_ffn", "w_ffn_gate", "w_ffn_up", "w_ffn_down")
BIG = ("w_in", "w_attn_proj", "w_glu_a", "w_glu_b", "w_out", "w_ffn_gate", "w_ffn_up", "w_ffn_down")
FLIPPED = ("w_ffn_gate", "w_ffn_up")
SMALL = tuple(n for n in WEIGHTS if n not in BIG)
ROW_VECTORS = ("g_mix", "g_q", "g_k", "d_skip", "g_ffn")
PACK_QUANTUM = LANES * SUBLANES * 7


def _pack_small(parts, extra):
    flat = jnp.concatenate([parts[n].reshape(-1).astype(F32) for n in SMALL] + [extra.reshape(-1)])
    pad = -flat.shape[0] % PACK_QUANTUM
    return jnp.pad(flat, (0, pad)).reshape(-1, LANES)


def _unpack_small(packed, like):
    flat = packed.reshape(-1)
    out, at = {}, 0
    for n in SMALL:
        size = math.prod(like[n].shape)
        out[n] = flat[at:at + size].reshape(like[n].shape)
        at += size
    return out, flat[at]


def kernel(x, g_mix, w_in, g_q, g_k, w_attn_proj, lambda_re, lambda_im, log_dt, b_re, b_im, c_re, c_im, d_skip, w_glu_a, w_glu_b, w_out, g_ffn, w_ffn_gate, w_ffn_up, w_ffn_down, loss_target, m_g_mix, m_w_in, m_g_q, m_g_k, m_w_attn_proj, m_lambda_re, m_lambda_im, m_log_dt, m_b_re, m_b_im, m_c_re, m_c_im, m_d_skip, m_w_glu_a, m_w_glu_b, m_w_out, m_g_ffn, m_w_ffn_gate, m_w_ffn_up, m_w_ffn_down, v_g_mix, v_w_in, v_g_q, v_g_k, v_w_attn_proj, v_lambda_re, v_lambda_im, v_log_dt, v_b_re, v_b_im, v_c_re, v_c_im, v_d_skip, v_w_glu_a, v_w_glu_b, v_w_out, v_g_ffn, v_w_ffn_gate, v_w_ffn_up, v_w_ffn_down):
    given = dict(locals())
    flip = lambda n, a: jnp.swapaxes(a, 1, 2) if n in FLIPPED else a
    W = {n: flip(n, given[n]) for n in WEIGHTS}
    M = {n: flip(n, given["m_" + n]) for n in WEIGHTS}
    V = {n: flip(n, given["v_" + n]) for n in WEIGHTS}
    depth = g_mix.shape[0]
    xl = x.reshape(x.shape[-2:])
    target = loss_target.reshape(loss_target.shape[-2:])
    c_idx = lax.axis_index("c").astype(jnp.int32).reshape(1)
    chip_idx = (2 * lax.axis_index("x") + lax.axis_index("y")).astype(jnp.int32).reshape(1)

    place = lambda l: [_cast_place(W[n], l, chip_idx) for n in BIG]
    bufs = place(0)
    w_in = _run_rider(_gather_d2d_rider(_run_rider(_gather_ici_rider(bufs[:1]), "gather_ici")), "gather_d2d")[0]
    rest, stage = bufs[1:], "ici"
    params, saved, h = [], [], xl
    for l in range(depth):
        p = {"w_in": w_in}
        for n in SMALL:
            p[n] = W[n][l][None] if n in ROW_VECTORS else W[n][l]
        h, sv, p, nxt = _layer_fwd(h, p, rest, stage, place(l + 1) if l + 1 < depth else None)
        params.append(p)
        saved.append(sv)
        if nxt:
            (w_in, rest), stage = nxt, "d2d"
    dx, loss_part = _loss_head(h, target)

    owned = {n: lax.empty(W[n].shape, F32) for n in BIG}
    small_grads = [None] * depth
    pending = None
    for l in reversed(range(depth)):
        dx, small_grads[l], pending, owned = _layer_bwd(dx, saved[l], params[l], pending, owned, l,
                                                        (chip_idx, c_idx))

    ct = {k: jnp.stack([small_grads[l]["ssm_pack_ct"][k] for l in range(depth)])
          for k in small_grads[0]["ssm_pack_ct"]}
    ct["a_re"], ct["a_im"] = jnp.sum(ct["a_re"], axis=2), jnp.sum(ct["a_im"], axis=2)
    ct["a64_re"] = ct["a64_im"] = jnp.zeros_like(ct["a_re"])
    _, pull = jax.vjp(jax.vmap(_ssm_pack), *[W[n] for n in SSM_PARAMS])
    stacked = dict(zip(SSM_PARAMS, pull(ct)))
    for n in SMALL:
        if n not in stacked:
            stacked[n] = jnp.stack([small_grads[l][n] for l in range(depth)])
    zero = jnp.zeros((1,), F32)
    dev_idx = (4 * lax.axis_index("x") + 2 * lax.axis_index("y") + lax.axis_index("c")).astype(jnp.int32).reshape(1)
    gathered = _place_small(_pack_small(stacked, loss_part), dev_idx)
    outs = _run_rider(_join_riders(_scatter_rider([pending[n].astype(BF16) for n in LATE]),
                                   _small_ici_rider(gathered)), "scatter_to_owners")
    for n, q in zip(LATE, outs[:len(LATE)]):
        owned[n] = _sum_owner(pending[n], q, owned[n], 0, chip_idx, c_idx)
    outs = _run_rider(_join_riders(_join_rider([owned[n] for n in BIG]), _small_d2d_rider(outs[len(LATE)])),
                      "join_halves")
    reduced, gathered = dict(zip(BIG, outs[:len(BIG)])), outs[len(BIG)]
    grads, delta, new_m, new_v = {}, {}, {}, {}
    for n in BIG:
        outs = (reduced[n], *_adamw(W[n], reduced[n], M[n], V[n]))
        grads[n], delta[n], new_m[n], new_v[n] = [flip(n, t) for t in outs]
    gs, ds, nms, nvs = _small_update(gathered, _pack_small(W, zero), _pack_small(M, zero), _pack_small(V, zero))
    sg, loss = _unpack_small(gs, W)
    sd, _ = _unpack_small(ds, W)
    sm, _ = _unpack_small(nms, W)
    sv_, _ = _unpack_small(nvs, W)
    for n in SMALL:
        grads[n], delta[n], new_m[n], new_v[n] = sg[n], sd[n], sm[n], sv_[n]

    return (loss, dx.reshape(x.shape), *[grads[n] for n in WEIGHTS], *[delta[n] for n in WEIGHTS],
            *[new_m[n] for n in WEIGHTS], *[new_v[n] for n in WEIGHTS])
```
